```python
import jax, jax.numpy as jnp
from jax import lax
import numpy as np

D_MODEL = 1024
BATCH = 8
SEQ = 16384
DEPTH = 2

CHUNK = 64
PLE_DIM = 256
BRANCH_WIDTH = D_MODEL // 2
N_BRANCH = 4
SG_BLOCK = 128
SG_GROUPS = 4
SG_WIDTH = BRANCH_WIDTH
GLA_HEADS = 4
GLA_DK = 64
GLA_DV = BRANCH_WIDTH // GLA_HEADS
GLA_RANK = 16
GLA_TAU = 16.0
ATT_HEADS = 8
ATT_HD = BRANCH_WIDTH // ATT_HEADS
ATT_BAND = 9
MAX_REL = 256
REL_TABLE = CHUNK + MAX_REL
CONV_WIDTH = BRANCH_WIDTH
CONV_K = 31
D_FF = 4 * D_MODEL
EPS = 1e-6
NEG_INF = -1e30

IN_SPLITS = (SG_WIDTH, SG_WIDTH,
             GLA_HEADS * GLA_DK, GLA_HEADS * GLA_DK, GLA_HEADS * GLA_DV, GLA_HEADS * GLA_DV, GLA_RANK,
             ATT_HEADS * ATT_HD, ATT_HEADS * ATT_HD, ATT_HEADS * ATT_HD,
             CONV_WIDTH, CONV_WIDTH)
IN_COLS = 2 * SG_WIDTH + 2 * GLA_HEADS * GLA_DK + 2 * GLA_HEADS * GLA_DV + GLA_RANK + 3 * ATT_HEADS * ATT_HD + 2 * CONV_WIDTH

kernel_name = "hybrid_gated_branch_streaming_encoder"


def rms_norm(x, g):
    xf = x.astype(jnp.float32)
    y = xf * lax.rsqrt(jnp.mean(xf * xf, axis=-1, keepdims=True) + EPS)
    return (y * g.astype(jnp.float32)).astype(x.dtype)


def layer_norm(x, g, b):
    xf = x.astype(jnp.float32)
    mu = jnp.mean(xf, axis=-1, keepdims=True)
    xc = xf - mu
    y = xc * lax.rsqrt(jnp.mean(xc * xc, axis=-1, keepdims=True) + EPS)
    return (y * g.astype(jnp.float32) + b.astype(jnp.float32)).astype(x.dtype)


def spatial_gating(u, v, ln_g, ln_b, w_s, b_s):
    bsz, s, _ = u.shape
    nb = s // SG_BLOCK
    cg = SG_WIDTH // SG_GROUPS
    v = layer_norm(v, ln_g, ln_b)
    vb = v.reshape(bsz, nb, SG_BLOCK, SG_GROUPS, cg)
    pos = jnp.arange(SG_BLOCK)
    mask = (pos[None, :] // CHUNK) <= (pos[:, None] // CHUNK)
    w = jnp.where(mask[None], w_s, 0.0)
    mixed = jnp.einsum('gij,bnjgc->bnigc', w, vb) + b_s.T[None, None, :, :, None]
    return u * mixed.reshape(bsz, s, SG_WIDTH)


def gated_linear_attention(q, k, v, r, a_lr, w_a2, b_a, norm_g):
    f32 = jnp.float32
    bsz, s, _ = q.shape
    nc = s // CHUNK
    qc = q.astype(f32).reshape(bsz, nc, CHUNK, GLA_HEADS, GLA_DK) * (GLA_DK ** -0.5)
    kc = k.astype(f32).reshape(bsz, nc, CHUNK, GLA_HEADS, GLA_DK)
    vc = v.astype(f32).reshape(bsz, nc, CHUNK, GLA_HEADS, GLA_DV)
    log_a = jax.nn.log_sigmoid(jnp.einsum('bsr,rk->bsk', a_lr.astype(f32), w_a2.astype(f32))
                               + b_a.astype(f32)) / GLA_TAU
    log_a = log_a.reshape(bsz, nc, CHUNK, GLA_HEADS, GLA_DK)
    cum = jnp.cumsum(log_a, axis=2)
    total = cum[:, :, -1]
    k_dec = kc * jnp.exp(total[:, :, None] - cum)
    upd = jnp.einsum('bclhd,bclhe->cbhde', k_dec, vc)
    decay = jnp.exp(total).transpose(1, 0, 2, 3)

    def step(state, inp):
        d, u_c = inp
        state = d[..., None] * state + u_c
        return state, state

    s0 = jnp.zeros((bsz, GLA_HEADS, GLA_DK, GLA_DV), f32)
    _, states = lax.scan(step, s0, (decay, upd))
    o = jnp.einsum('bclhd,cbhde->bclhe', qc, states)
    o = o * lax.rsqrt(jnp.mean(o * o, axis=-1, keepdims=True) + EPS)
    o = o.reshape(bsz, s, GLA_HEADS * GLA_DV) * norm_g.astype(f32)
    return (o * jax.nn.silu(r.astype(f32))).astype(r.dtype)


def band_chunk_attention(q, k, v, rel_bias):
    f32 = jnp.float32
    bsz, s, _ = q.shape
    nc = s // CHUNK
    prev = ATT_BAND - 1
    band = ATT_BAND * CHUNK
    qh = q.reshape(bsz, s, ATT_HEADS, ATT_HD)
    pad = ((0, 0), (prev * CHUNK, 0), (0, 0), (0, 0))
    kp = jnp.pad(k.reshape(bsz, s, ATT_HEADS, ATT_HD), pad)
    vp = jnp.pad(v.reshape(bsz, s, ATT_HEADS, ATT_HD), pad)
    l_idx = jnp.arange(CHUNK)
    m_idx = jnp.arange(band)
    rel = l_idx[:, None] + prev * CHUNK - m_idx[None, :]
    idx = jnp.clip(rel, -(CHUNK - 1), MAX_REL) + (CHUNK - 1)
    bias = rel_bias.astype(f32)[:, idx]
    scale = ATT_HD ** -0.5

    def one_chunk(c):
        qc = lax.dynamic_slice_in_dim(qh, c * CHUNK, CHUNK, axis=1)
        kc = lax.dynamic_slice_in_dim(kp, c * CHUNK, band, axis=1)
        vc = lax.dynamic_slice_in_dim(vp, c * CHUNK, band, axis=1)
        sc = jnp.einsum('blhd,bmhd->bhlm', qc, kc, preferred_element_type=f32) * scale + bias[None]
        key_ok = m_idx >= (prev - c) * CHUNK
        sc = jnp.where(key_ok[None, None, None, :], sc, NEG_INF)
        pw = jax.nn.softmax(sc, axis=-1)
        return jnp.einsum('bhlm,bmhd->blhd', pw.astype(vc.dtype), vc)

    out = lax.map(one_chunk, jnp.arange(nc))
    return out.transpose(1, 0, 2, 3, 4).reshape(bsz, s, ATT_HEADS * ATT_HD)


def conformer_conv(a, g, dw_w, dw_b, ln_g, ln_b):
    y = a * jax.nn.sigmoid(g)
    y = lax.conv_general_dilated(y, dw_w[:, None, :], window_strides=(1,),
                                 padding=((CONV_K - 1, 0),),
                                 dimension_numbers=('NWC', 'WIO', 'NWC'),
                                 feature_group_count=CONV_WIDTH) + dw_b
    return jax.nn.silu(layer_norm(y, ln_g, ln_b))


def _fwd_setup_inputs(seed: int = 0) -> dict:
    key = jax.random.key(seed)
    ks = iter(jax.random.split(key, 40))

    def nrm(shape, scale):
        return scale * jax.random.normal(next(ks), shape, jnp.float32)

    def gain(shape):
        return 1.0 + nrm(shape, 0.05)

    L = DEPTH
    return {
        "x": nrm((BATCH, SEQ, D_MODEL), 1.0),
        "p": nrm((DEPTH, BATCH, SEQ, PLE_DIM), 1.0),
        "norm1_g": gain((L, D_MODEL)),
        "w_in": nrm((L, D_MODEL, IN_COLS), D_MODEL ** -0.5),
        "sg_ln_g": gain((L, SG_WIDTH)),
        "sg_ln_b": nrm((L, SG_WIDTH), 0.02),
        "sg_w": nrm((L, SG_GROUPS, SG_BLOCK, SG_BLOCK), SG_BLOCK ** -0.5),
        "sg_b": 1.0 + nrm((L, SG_GROUPS, SG_BLOCK), 0.1),
        "gla_w_a2": nrm((L, GLA_RANK, GLA_HEADS * GLA_DK), GLA_RANK ** -0.5),
        "gla_b_a": nrm((L, GLA_HEADS * GLA_DK), 0.1),
        "gla_norm_g": gain((L, GLA_HEADS * GLA_DV)),
        "att_rel_bias": nrm((L, ATT_HEADS, REL_TABLE), 0.5),
        "conv_dw_w": nrm((L, CONV_K, CONV_WIDTH), CONV_K ** -0.5),
        "conv_dw_b": nrm((L, CONV_WIDTH), 0.02),
        "conv_ln_g": gain((L, CONV_WIDTH)),
        "conv_ln_b": nrm((L, CONV_WIDTH), 0.02),
        "w_branch": nrm((L, N_BRANCH, BRANCH_WIDTH, D_MODEL), BRANCH_WIDTH ** -0.5),
        "w_gate": nrm((L, N_BRANCH, D_MODEL, D_MODEL), D_MODEL ** -0.5),
        "b_gate": nrm((L, N_BRANCH, D_MODEL), 0.02),
        "w_out": nrm((L, D_MODEL, D_MODEL), D_MODEL ** -0.5),
        "norm2_g": gain((L, D_MODEL)),
        "w_ff1": nrm((L, D_MODEL, D_FF), D_MODEL ** -0.5),
        "w_ff2": nrm((L, D_FF, D_MODEL), D_FF ** -0.5),
        "norm3_g": gain((L, D_MODEL)),
        "w_ple_gate": nrm((L, D_MODEL, D_MODEL), D_MODEL ** -0.5),
        "b_ple_gate": nrm((L, D_MODEL), 0.02),
        "w_ple": nrm((L, PLE_DIM, D_MODEL), PLE_DIM ** -0.5),
        "final_g": gain((D_MODEL,)),
    }


def _fwd_reference(x, p, norm1_g, w_in, sg_ln_g, sg_ln_b, sg_w, sg_b, gla_w_a2, gla_b_a, gla_norm_g,
              att_rel_bias, conv_dw_w, conv_dw_b, conv_ln_g, conv_ln_b, w_branch, w_gate, b_gate,
              w_out, norm2_g, w_ff1, w_ff2, norm3_g, w_ple_gate, b_ple_gate, w_ple, final_g):
    cuts = np.cumsum(IN_SPLITS)[:-1].tolist()
    h = x
    for i in range(DEPTH):
        xn = rms_norm(h, norm1_g[i])
        proj = jnp.einsum('bsd,dk->bsk', xn, w_in[i])
        (sg_u, sg_v, g_q, g_k, g_v, g_r, g_a, a_q, a_k, a_v, c_a, c_g) = jnp.split(proj, cuts, axis=-1)

        y_a = spatial_gating(jax.nn.gelu(sg_u), jax.nn.gelu(sg_v), sg_ln_g[i], sg_ln_b[i], sg_w[i], sg_b[i])
        y_b = gated_linear_attention(g_q, g_k, g_v, g_r, g_a, gla_w_a2[i], gla_b_a[i], gla_norm_g[i])
        y_c = band_chunk_attention(a_q, a_k, a_v, att_rel_bias[i])
        y_d = conformer_conv(c_a, c_g, conv_dw_w[i], conv_dw_b[i], conv_ln_g[i], conv_ln_b[i])

        merged = jnp.zeros_like(h)
        for n, y in enumerate((y_a, y_b, y_c, y_d)):
            gate = jax.nn.sigmoid(jnp.einsum('bsd,de->bse', xn, w_gate[i, n]) + b_gate[i, n])
            merged = merged + gate * jnp.einsum('bsk,kd->bsd', y, w_branch[i, n])
        h = h + jnp.einsum('bsd,de->bse', merged, w_out[i])

        hn = rms_norm(h, norm2_g[i])
        ff = jnp.square(jax.nn.relu(jnp.einsum('bsd,df->bsf', hn, w_ff1[i])))
        h = h + jnp.einsum('bsf,fd->bsd', ff, w_ff2[i])

        hg = rms_norm(h, norm3_g[i])
        ple_gate = jax.nn.sigmoid(jnp.einsum('bsd,de->bse', hg, w_ple_gate[i]) + b_ple_gate[i])
        h = h + ple_gate * jnp.einsum('bsq,qd->bsd', p[i], w_ple[i])
    return rms_norm(h, final_g)


import jax as _jax
import jax.numpy as _jnp

TWIN_FORMAT = 'train_step'
FWD_PARAMS = ['x', 'p', 'norm1_g', 'w_in', 'sg_ln_g', 'sg_ln_b', 'sg_w', 'sg_b', 'gla_w_a2', 'gla_b_a', 'gla_norm_g', 'att_rel_bias', 'conv_dw_w', 'conv_dw_b', 'conv_ln_g', 'conv_ln_b', 'w_branch', 'w_gate', 'b_gate', 'w_out', 'norm2_g', 'w_ff1', 'w_ff2', 'norm3_g', 'w_ple_gate', 'b_ple_gate', 'w_ple', 'final_g']
TWIN_WEIGHTS = ['norm1_g', 'w_in', 'sg_ln_g', 'sg_ln_b', 'sg_w', 'sg_b', 'gla_w_a2', 'gla_b_a', 'gla_norm_g', 'att_rel_bias', 'conv_dw_w', 'conv_dw_b', 'conv_ln_g', 'conv_ln_b', 'w_branch', 'w_gate', 'b_gate', 'w_out', 'norm2_g', 'w_ff1', 'w_ff2', 'norm3_g', 'w_ple_gate', 'b_ple_gate', 'w_ple', 'final_g']
TWIN_DIFF_INPUT = 'x'
TWIN_INPUTS = ['x', 'p', 'norm1_g', 'w_in', 'sg_ln_g', 'sg_ln_b', 'sg_w', 'sg_b', 'gla_w_a2', 'gla_b_a', 'gla_norm_g', 'att_rel_bias', 'conv_dw_w', 'conv_dw_b', 'conv_ln_g', 'conv_ln_b', 'w_branch', 'w_gate', 'b_gate', 'w_out', 'norm2_g', 'w_ff1', 'w_ff2', 'norm3_g', 'w_ple_gate', 'b_ple_gate', 'w_ple', 'final_g', 'loss_target', 'm_norm1_g', 'm_w_in', 'm_sg_ln_g', 'm_sg_ln_b', 'm_sg_w', 'm_sg_b', 'm_gla_w_a2', 'm_gla_b_a', 'm_gla_norm_g', 'm_att_rel_bias', 'm_conv_dw_w', 'm_conv_dw_b', 'm_conv_ln_g', 'm_conv_ln_b', 'm_w_branch', 'm_w_gate', 'm_b_gate', 'm_w_out', 'm_norm2_g', 'm_w_ff1', 'm_w_ff2', 'm_norm3_g', 'm_w_ple_gate', 'm_b_ple_gate', 'm_w_ple', 'm_final_g', 'v_norm1_g', 'v_w_in', 'v_sg_ln_g', 'v_sg_ln_b', 'v_sg_w', 'v_sg_b', 'v_gla_w_a2', 'v_gla_b_a', 'v_gla_norm_g', 'v_att_rel_bias', 'v_conv_dw_w', 'v_conv_dw_b', 'v_conv_ln_g', 'v_conv_ln_b', 'v_w_branch', 'v_w_gate', 'v_b_gate', 'v_w_out', 'v_norm2_g', 'v_w_ff1', 'v_w_ff2', 'v_norm3_g', 'v_w_ple_gate', 'v_b_ple_gate', 'v_w_ple', 'v_final_g']
TWIN_OUTPUTS = ['loss', 'grad_x', 'grad_norm1_g', 'grad_w_in', 'grad_sg_ln_g', 'grad_sg_ln_b', 'grad_sg_w', 'grad_sg_b', 'grad_gla_w_a2', 'grad_gla_b_a', 'grad_gla_norm_g', 'grad_att_rel_bias', 'grad_conv_dw_w', 'grad_conv_dw_b', 'grad_conv_ln_g', 'grad_conv_ln_b', 'grad_w_branch', 'grad_w_gate', 'grad_b_gate', 'grad_w_out', 'grad_norm2_g', 'grad_w_ff1', 'grad_w_ff2', 'grad_norm3_g', 'grad_w_ple_gate', 'grad_b_ple_gate', 'grad_w_ple', 'grad_final_g', 'delta_norm1_g', 'delta_w_in', 'delta_sg_ln_g', 'delta_sg_ln_b', 'delta_sg_w', 'delta_sg_b', 'delta_gla_w_a2', 'delta_gla_b_a', 'delta_gla_norm_g', 'delta_att_rel_bias', 'delta_conv_dw_w', 'delta_conv_dw_b', 'delta_conv_ln_g', 'delta_conv_ln_b', 'delta_w_branch', 'delta_w_gate', 'delta_b_gate', 'delta_w_out', 'delta_norm2_g', 'delta_w_ff1', 'delta_w_ff2', 'delta_norm3_g', 'delta_w_ple_gate', 'delta_b_ple_gate', 'delta_w_ple', 'delta_final_g', 'new_m_norm1_g', 'new_m_w_in', 'new_m_sg_ln_g', 'new_m_sg_ln_b', 'new_m_sg_w', 'new_m_sg_b', 'new_m_gla_w_a2', 'new_m_gla_b_a', 'new_m_gla_norm_g', 'new_m_att_rel_bias', 'new_m_conv_dw_w', 'new_m_conv_dw_b', 'new_m_conv_ln_g', 'new_m_conv_ln_b', 'new_m_w_branch', 'new_m_w_gate', 'new_m_b_gate', 'new_m_w_out', 'new_m_norm2_g', 'new_m_w_ff1', 'new_m_w_ff2', 'new_m_norm3_g', 'new_m_w_ple_gate', 'new_m_b_ple_gate', 'new_m_w_ple', 'new_m_final_g', 'new_v_norm1_g', 'new_v_w_in', 'new_v_sg_ln_g', 'new_v_sg_ln_b', 'new_v_sg_w', 'new_v_sg_b', 'new_v_gla_w_a2', 'new_v_gla_b_a', 'new_v_gla_norm_g', 'new_v_att_rel_bias', 'new_v_conv_dw_w', 'new_v_conv_dw_b', 'new_v_conv_ln_g', 'new_v_conv_ln_b', 'new_v_w_branch', 'new_v_w_gate', 'new_v_b_gate', 'new_v_w_out', 'new_v_norm2_g', 'new_v_w_ff1', 'new_v_w_ff2', 'new_v_norm3_g', 'new_v_w_ple_gate', 'new_v_b_ple_gate', 'new_v_w_ple', 'new_v_final_g']
TWIN_LEAF_KINDS = {'loss': 'loss', 'grad_x': 'grad_x', 'grad_norm1_g': 'grad_w', 'grad_w_in': 'grad_w', 'grad_sg_ln_g': 'grad_w', 'grad_sg_ln_b': 'grad_w', 'grad_sg_w': 'grad_w', 'grad_sg_b': 'grad_w', 'grad_gla_w_a2': 'grad_w', 'grad_gla_b_a': 'grad_w', 'grad_gla_norm_g': 'grad_w', 'grad_att_rel_bias': 'grad_w', 'grad_conv_dw_w': 'grad_w', 'grad_conv_dw_b': 'grad_w', 'grad_conv_ln_g': 'grad_w', 'grad_conv_ln_b': 'grad_w', 'grad_w_branch': 'grad_w', 'grad_w_gate': 'grad_w', 'grad_b_gate': 'grad_w', 'grad_w_out': 'grad_w', 'grad_norm2_g': 'grad_w', 'grad_w_ff1': 'grad_w', 'grad_w_ff2': 'grad_w', 'grad_norm3_g': 'grad_w', 'grad_w_ple_gate': 'grad_w', 'grad_b_ple_gate': 'grad_w', 'grad_w_ple': 'grad_w', 'grad_final_g': 'grad_w', 'delta_norm1_g': 'delta_w', 'delta_w_in': 'delta_w', 'delta_sg_ln_g': 'delta_w', 'delta_sg_ln_b': 'delta_w', 'delta_sg_w': 'delta_w', 'delta_sg_b': 'delta_w', 'delta_gla_w_a2': 'delta_w', 'delta_gla_b_a': 'delta_w', 'delta_gla_norm_g': 'delta_w', 'delta_att_rel_bias': 'delta_w', 'delta_conv_dw_w': 'delta_w', 'delta_conv_dw_b': 'delta_w', 'delta_conv_ln_g': 'delta_w', 'delta_conv_ln_b': 'delta_w', 'delta_w_branch': 'delta_w', 'delta_w_gate': 'delta_w', 'delta_b_gate': 'delta_w', 'delta_w_out': 'delta_w', 'delta_norm2_g': 'delta_w', 'delta_w_ff1': 'delta_w', 'delta_w_ff2': 'delta_w', 'delta_norm3_g': 'delta_w', 'delta_w_ple_gate': 'delta_w', 'delta_b_ple_gate': 'delta_w', 'delta_w_ple': 'delta_w', 'delta_final_g': 'delta_w', 'new_m_norm1_g': 'new_m', 'new_m_w_in': 'new_m', 'new_m_sg_ln_g': 'new_m', 'new_m_sg_ln_b': 'new_m', 'new_m_sg_w': 'new_m', 'new_m_sg_b': 'new_m', 'new_m_gla_w_a2': 'new_m', 'new_m_gla_b_a': 'new_m', 'new_m_gla_norm_g': 'new_m', 'new_m_att_rel_bias': 'new_m', 'new_m_conv_dw_w': 'new_m', 'new_m_conv_dw_b': 'new_m', 'new_m_conv_ln_g': 'new_m', 'new_m_conv_ln_b': 'new_m', 'new_m_w_branch': 'new_m', 'new_m_w_gate': 'new_m', 'new_m_b_gate': 'new_m', 'new_m_w_out': 'new_m', 'new_m_norm2_g': 'new_m', 'new_m_w_ff1': 'new_m', 'new_m_w_ff2': 'new_m', 'new_m_norm3_g': 'new_m', 'new_m_w_ple_gate': 'new_m', 'new_m_b_ple_gate': 'new_m', 'new_m_w_ple': 'new_m', 'new_m_final_g': 'new_m', 'new_v_norm1_g': 'new_v', 'new_v_w_in': 'new_v', 'new_v_sg_ln_g': 'new_v', 'new_v_sg_ln_b': 'new_v', 'new_v_sg_w': 'new_v', 'new_v_sg_b': 'new_v', 'new_v_gla_w_a2': 'new_v', 'new_v_gla_b_a': 'new_v', 'new_v_gla_norm_g': 'new_v', 'new_v_att_rel_bias': 'new_v', 'new_v_conv_dw_w': 'new_v', 'new_v_conv_dw_b': 'new_v', 'new_v_conv_ln_g': 'new_v', 'new_v_conv_ln_b': 'new_v', 'new_v_w_branch': 'new_v', 'new_v_w_gate': 'new_v', 'new_v_b_gate': 'new_v', 'new_v_w_out': 'new_v', 'new_v_norm2_g': 'new_v', 'new_v_w_ff1': 'new_v', 'new_v_w_ff2': 'new_v', 'new_v_norm3_g': 'new_v', 'new_v_w_ple_gate': 'new_v', 'new_v_b_ple_gate': 'new_v', 'new_v_w_ple': 'new_v', 'new_v_final_g': 'new_v'}


def _forward(args):
    return _fwd_reference(*[args[k] for k in FWD_PARAMS])


def _output_shape():
    def fwd():
        inp = _fwd_setup_inputs(0)
        return _fwd_reference(*[inp[k] for k in FWD_PARAMS])
    out = _jax.eval_shape(fwd)
    return out.shape, out.dtype

N_MICROBATCH = 1
ADAM_LR = 0.001
ADAM_B1 = 0.9
ADAM_B2 = 0.999
ADAM_EPS = 1e-08
ADAM_WD = 0.01
ADAM_STEP = 10
PER_EXAMPLE_BATCH_AXIS = {'x': 0, 'p': 1, 'loss_target': 0}
SHARED_INPUTS = []
_WEIGHT_DTYPES = {'norm1_g': _jnp.float32, 'w_in': _jnp.float32, 'sg_ln_g': _jnp.float32, 'sg_ln_b': _jnp.float32, 'sg_w': _jnp.float32, 'sg_b': _jnp.float32, 'gla_w_a2': _jnp.float32, 'gla_b_a': _jnp.float32, 'gla_norm_g': _jnp.float32, 'att_rel_bias': _jnp.float32, 'conv_dw_w': _jnp.float32, 'conv_dw_b': _jnp.float32, 'conv_ln_g': _jnp.float32, 'conv_ln_b': _jnp.float32, 'w_branch': _jnp.float32, 'w_gate': _jnp.float32, 'b_gate': _jnp.float32, 'w_out': _jnp.float32, 'norm2_g': _jnp.float32, 'w_ff1': _jnp.float32, 'w_ff2': _jnp.float32, 'norm3_g': _jnp.float32, 'w_ple_gate': _jnp.float32, 'b_ple_gate': _jnp.float32, 'w_ple': _jnp.float32, 'final_g': _jnp.float32}
MOMENT_SCALE = {'norm1_g': 2.914711e-01, 'w_in': 1.295513e-01, 'sg_ln_g': 1.157496e-01, 'sg_ln_b': 1.213700e-01, 'sg_w': 1.194303e-01, 'sg_b': 1.459244e-01, 'gla_w_a2': 2.467684e-02, 'gla_b_a': 1.026394e-01, 'gla_norm_g': 1.318098e-01, 'att_rel_bias': 1.427510e-02, 'conv_dw_w': 1.495619e-01, 'conv_dw_b': 9.111868e-01, 'conv_ln_g': 3.614957e-01, 'conv_ln_b': 5.716700e-01, 'w_branch': 1.434885e-01, 'w_gate': 3.622269e-02, 'b_gate': 6.089089e-02, 'w_out': 2.821700e-01, 'norm2_g': 3.247473e-01, 'w_ff1': 1.606541e-01, 'w_ff2': 6.274760e-01, 'norm3_g': 4.866704e-02, 'w_ple_gate': 5.194006e-02, 'b_ple_gate': 1.443333e-01, 'w_ple': 1.029566e-01, 'final_g': 1.298010e+02}


def _to_microbatches(a, axis):
    t = _jnp.moveaxis(a, axis, 0)
    t = t.reshape((N_MICROBATCH, t.shape[0] // N_MICROBATCH) + t.shape[1:])
    return _jnp.moveaxis(t, 1, axis + 1)


def setup_inputs(seed: int = 0) -> dict:
    inp = _fwd_setup_inputs(seed)
    key = _jax.random.fold_in(_jax.random.key(seed), 7919)
    shape, _ = _output_shape()
    out = dict(inp)
    out["loss_target"] = _jax.random.normal(_jax.random.fold_in(key, 0), shape, _jnp.float32)
    for i, name in enumerate(TWIN_WEIGHTS):
        w = inp[name].astype(_jnp.float32)
        if MOMENT_SCALE is None:
            s = _jnp.sqrt(_jnp.mean(_jnp.square(w)) + 1e-30)
        else:
            s = MOMENT_SCALE[name]
        km, kv = _jax.random.split(_jax.random.fold_in(key, i + 1))
        out[name] = w
        out["m_" + name] = s * _jax.random.normal(km, w.shape, _jnp.float32)
        out["v_" + name] = (s * s) * _jax.random.uniform(kv, w.shape, _jnp.float32, 0.5, 1.5)
    if N_MICROBATCH > 1:
        for name, axis in PER_EXAMPLE_BATCH_AXIS.items():
            out[name] = _to_microbatches(out[name], axis)
    return {'x': out['x'], 'p': out['p'], 'norm1_g': out['norm1_g'], 'w_in': out['w_in'], 'sg_ln_g': out['sg_ln_g'], 'sg_ln_b': out['sg_ln_b'], 'sg_w': out['sg_w'], 'sg_b': out['sg_b'], 'gla_w_a2': out['gla_w_a2'], 'gla_b_a': out['gla_b_a'], 'gla_norm_g': out['gla_norm_g'], 'att_rel_bias': out['att_rel_bias'], 'conv_dw_w': out['conv_dw_w'], 'conv_dw_b': out['conv_dw_b'], 'conv_ln_g': out['conv_ln_g'], 'conv_ln_b': out['conv_ln_b'], 'w_branch': out['w_branch'], 'w_gate': out['w_gate'], 'b_gate': out['b_gate'], 'w_out': out['w_out'], 'norm2_g': out['norm2_g'], 'w_ff1': out['w_ff1'], 'w_ff2': out['w_ff2'], 'norm3_g': out['norm3_g'], 'w_ple_gate': out['w_ple_gate'], 'b_ple_gate': out['b_ple_gate'], 'w_ple': out['w_ple'], 'final_g': out['final_g'], 'loss_target': out['loss_target'], 'm_norm1_g': out['m_norm1_g'], 'm_w_in': out['m_w_in'], 'm_sg_ln_g': out['m_sg_ln_g'], 'm_sg_ln_b': out['m_sg_ln_b'], 'm_sg_w': out['m_sg_w'], 'm_sg_b': out['m_sg_b'], 'm_gla_w_a2': out['m_gla_w_a2'], 'm_gla_b_a': out['m_gla_b_a'], 'm_gla_norm_g': out['m_gla_norm_g'], 'm_att_rel_bias': out['m_att_rel_bias'], 'm_conv_dw_w': out['m_conv_dw_w'], 'm_conv_dw_b': out['m_conv_dw_b'], 'm_conv_ln_g': out['m_conv_ln_g'], 'm_conv_ln_b': out['m_conv_ln_b'], 'm_w_branch': out['m_w_branch'], 'm_w_gate': out['m_w_gate'], 'm_b_gate': out['m_b_gate'], 'm_w_out': out['m_w_out'], 'm_norm2_g': out['m_norm2_g'], 'm_w_ff1': out['m_w_ff1'], 'm_w_ff2': out['m_w_ff2'], 'm_norm3_g': out['m_norm3_g'], 'm_w_ple_gate': out['m_w_ple_gate'], 'm_b_ple_gate': out['m_b_ple_gate'], 'm_w_ple': out['m_w_ple'], 'm_final_g': out['m_final_g'], 'v_norm1_g': out['v_norm1_g'], 'v_w_in': out['v_w_in'], 'v_sg_ln_g': out['v_sg_ln_g'], 'v_sg_ln_b': out['v_sg_ln_b'], 'v_sg_w': out['v_sg_w'], 'v_sg_b': out['v_sg_b'], 'v_gla_w_a2': out['v_gla_w_a2'], 'v_gla_b_a': out['v_gla_b_a'], 'v_gla_norm_g': out['v_gla_norm_g'], 'v_att_rel_bias': out['v_att_rel_bias'], 'v_conv_dw_w': out['v_conv_dw_w'], 'v_conv_dw_b': out['v_conv_dw_b'], 'v_conv_ln_g': out['v_conv_ln_g'], 'v_conv_ln_b': out['v_conv_ln_b'], 'v_w_branch': out['v_w_branch'], 'v_w_gate': out['v_w_gate'], 'v_b_gate': out['v_b_gate'], 'v_w_out': out['v_w_out'], 'v_norm2_g': out['v_norm2_g'], 'v_w_ff1': out['v_w_ff1'], 'v_w_ff2': out['v_w_ff2'], 'v_norm3_g': out['v_norm3_g'], 'v_w_ple_gate': out['v_w_ple_gate'], 'v_b_ple_gate': out['v_b_ple_gate'], 'v_w_ple': out['v_w_ple'], 'v_final_g': out['v_final_g']}


def _loss(weights, diff, rest, loss_target):
    with _jax.named_scope("forward"):
        args = {**rest, TWIN_DIFF_INPUT: diff, **{k: w.astype(_WEIGHT_DTYPES[k]) for k, w in weights.items()}}
        y = _forward(args)
    with _jax.named_scope("loss_head"):
        err = _jnp.square(y.astype(_jnp.float32) - loss_target)
        return 0.5 * _jnp.sum(_jnp.mean(err, axis=-1)) if err.ndim else 0.5 * err


def _adamw(w, g, m, v):
    m = ADAM_B1 * m + (1.0 - ADAM_B1) * g
    v = ADAM_B2 * v + (1.0 - ADAM_B2) * _jnp.square(g)
    m_hat = m / (1.0 - ADAM_B1 ** ADAM_STEP)
    v_hat = v / (1.0 - ADAM_B2 ** ADAM_STEP)
    delta = -ADAM_LR * (m_hat / (_jnp.sqrt(v_hat) + ADAM_EPS) + ADAM_WD * w)
    return delta, m, v


def reference(x, p, norm1_g, w_in, sg_ln_g, sg_ln_b, sg_w, sg_b, gla_w_a2, gla_b_a, gla_norm_g, att_rel_bias, conv_dw_w, conv_dw_b, conv_ln_g, conv_ln_b, w_branch, w_gate, b_gate, w_out, norm2_g, w_ff1, w_ff2, norm3_g, w_ple_gate, b_ple_gate, w_ple, final_g, loss_target, m_norm1_g, m_w_in, m_sg_ln_g, m_sg_ln_b, m_sg_w, m_sg_b, m_gla_w_a2, m_gla_b_a, m_gla_norm_g, m_att_rel_bias, m_conv_dw_w, m_conv_dw_b, m_conv_ln_g, m_conv_ln_b, m_w_branch, m_w_gate, m_b_gate, m_w_out, m_norm2_g, m_w_ff1, m_w_ff2, m_norm3_g, m_w_ple_gate, m_b_ple_gate, m_w_ple, m_final_g, v_norm1_g, v_w_in, v_sg_ln_g, v_sg_ln_b, v_sg_w, v_sg_b, v_gla_w_a2, v_gla_b_a, v_gla_norm_g, v_att_rel_bias, v_conv_dw_w, v_conv_dw_b, v_conv_ln_g, v_conv_ln_b, v_w_branch, v_w_gate, v_b_gate, v_w_out, v_norm2_g, v_w_ff1, v_w_ff2, v_norm3_g, v_w_ple_gate, v_b_ple_gate, v_w_ple, v_final_g):
    given = dict(x=x, p=p, norm1_g=norm1_g, w_in=w_in, sg_ln_g=sg_ln_g, sg_ln_b=sg_ln_b, sg_w=sg_w, sg_b=sg_b, gla_w_a2=gla_w_a2, gla_b_a=gla_b_a, gla_norm_g=gla_norm_g, att_rel_bias=att_rel_bias, conv_dw_w=conv_dw_w, conv_dw_b=conv_dw_b, conv_ln_g=conv_ln_g, conv_ln_b=conv_ln_b, w_branch=w_branch, w_gate=w_gate, b_gate=b_gate, w_out=w_out, norm2_g=norm2_g, w_ff1=w_ff1, w_ff2=w_ff2, norm3_g=norm3_g, w_ple_gate=w_ple_gate, b_ple_gate=b_ple_gate, w_ple=w_ple, final_g=final_g, loss_target=loss_target, m_norm1_g=m_norm1_g, m_w_in=m_w_in, m_sg_ln_g=m_sg_ln_g, m_sg_ln_b=m_sg_ln_b, m_sg_w=m_sg_w, m_sg_b=m_sg_b, m_gla_w_a2=m_gla_w_a2, m_gla_b_a=m_gla_b_a, m_gla_norm_g=m_gla_norm_g, m_att_rel_bias=m_att_rel_bias, m_conv_dw_w=m_conv_dw_w, m_conv_dw_b=m_conv_dw_b, m_conv_ln_g=m_conv_ln_g, m_conv_ln_b=m_conv_ln_b, m_w_branch=m_w_branch, m_w_gate=m_w_gate, m_b_gate=m_b_gate, m_w_out=m_w_out, m_norm2_g=m_norm2_g, m_w_ff1=m_w_ff1, m_w_ff2=m_w_ff2, m_norm3_g=m_norm3_g, m_w_ple_gate=m_w_ple_gate, m_b_ple_gate=m_b_ple_gate, m_w_ple=m_w_ple, m_final_g=m_final_g, v_norm1_g=v_norm1_g, v_w_in=v_w_in, v_sg_ln_g=v_sg_ln_g, v_sg_ln_b=v_sg_ln_b, v_sg_w=v_sg_w, v_sg_b=v_sg_b, v_gla_w_a2=v_gla_w_a2, v_gla_b_a=v_gla_b_a, v_gla_norm_g=v_gla_norm_g, v_att_rel_bias=v_att_rel_bias, v_conv_dw_w=v_conv_dw_w, v_conv_dw_b=v_conv_dw_b, v_conv_ln_g=v_conv_ln_g, v_conv_ln_b=v_conv_ln_b, v_w_branch=v_w_branch, v_w_gate=v_w_gate, v_b_gate=v_b_gate, v_w_out=v_w_out, v_norm2_g=v_norm2_g, v_w_ff1=v_w_ff1, v_w_ff2=v_w_ff2, v_norm3_g=v_norm3_g, v_w_ple_gate=v_w_ple_gate, v_b_ple_gate=v_b_ple_gate, v_w_ple=v_w_ple, v_final_g=v_final_g)
    weights = {n: given[n] for n in TWIN_WEIGHTS}
    shared = {n: given[n] for n in SHARED_INPUTS}
    per_example = {n: given[n] for n in ['x', 'p']}
    grad_fn = _jax.value_and_grad(_loss, argnums=(0, 1))

    def one_microbatch(ex, loss_target):
        ex = dict(ex)
        diff = ex.pop(TWIN_DIFF_INPUT)
        return grad_fn(weights, diff, {**shared, **ex}, loss_target)

    if N_MICROBATCH == 1:
        loss, (grad_w, grad_x) = one_microbatch(per_example, given["loss_target"])
    else:
        def body(carry, xs):
            loss_sum, grad_sum = carry
            l_k, (gw_k, gx_k) = one_microbatch(xs[0], xs[1])
            with _jax.named_scope("update"):
                return (loss_sum + l_k, _jax.tree.map(_jnp.add, grad_sum, gw_k)), gx_k

        init = (_jnp.zeros((), _jnp.float32), _jax.tree.map(_jnp.zeros_like, weights))
        (loss, grad_w), grad_x = _jax.lax.scan(body, init, (per_example, given["loss_target"]))
    with _jax.named_scope("update"):
        delta_w, new_m, new_v = {}, {}, {}
        for n in TWIN_WEIGHTS:
            delta_w[n], new_m[n], new_v[n] = _adamw(weights[n], grad_w[n], given["m_" + n], given["v_" + n])
    return (loss, grad_x, *[grad_w[n] for n in TWIN_WEIGHTS], *[delta_w[n] for n in TWIN_WEIGHTS],
            *[new_m[n] for n in TWIN_WEIGHTS], *[new_v[n] for n in TWIN_WEIGHTS])
```

```python
import functools

import numpy as np
import jax
import jax.numpy as jnp
from jax import lax
from jax.experimental import pallas as pl
from jax.experimental.pallas import tpu as pltpu

F32 = jnp.float32
_MXU_DTYPE = jnp.bfloat16

N_DEV = 8
D_MODEL = 1024
DEPTH = 2
CHUNK = 64
PLE_DIM = 256
BW = 512
SG_BLOCK = 128
SG_GROUPS = 4
GLA_HEADS = 4
GLA_DK = 64
GLA_DV = 128
GLA_RANK = 16
GLA_TAU = 16.0
ATT_HEADS = 8
ATT_HD = 64
ATT_BAND = 9
BAND = ATT_BAND * CHUNK
MAX_REL = 256
REL_TABLE = CHUNK + MAX_REL
CONV_K = 31
CONV_HALO = 32
D_FF = 4096
EPS = 1e-6
NEG_INF = -1e30

IN_GROUPS = (("A", 0, 1024), ("B", 1024, 1536), ("a", 2560, 16), ("C", 2576, 1536), ("D", 4112, 1024))
IN_COLS = 5136

ADAM_LR = 0.001
ADAM_B1 = 0.9
ADAM_B2 = 0.999
ADAM_EPS = 1e-08
ADAM_WD = 0.01
ADAM_STEP = 10

ADAMW_TILE = 128
PACK_COLS = 1024
VMEM_LIMIT_MB = 56

_NN = (((1,), (0,)), ((), ()))
_NT = (((1,), (1,)), ((), ()))
_TN = (((0,), (0,)), ((), ()))

WEIGHTS = ['norm1_g', 'w_in', 'sg_ln_g', 'sg_ln_b', 'sg_w', 'sg_b', 'gla_w_a2', 'gla_b_a', 'gla_norm_g',
           'att_rel_bias', 'conv_dw_w', 'conv_dw_b', 'conv_ln_g', 'conv_ln_b', 'w_branch', 'w_gate', 'b_gate',
           'w_out', 'norm2_g', 'w_ff1', 'w_ff2', 'norm3_g', 'w_ple_gate', 'b_ple_gate', 'w_ple', 'final_g']
SHARD_AXIS = {'w_in': 2, 'gla_w_a2': 2, 'att_rel_bias': 2, 'conv_dw_w': 2, 'w_branch': 3, 'w_gate': 2,
              'b_gate': 2, 'w_out': 1, 'w_ff1': 2, 'w_ff2': 1, 'w_ple_gate': 1, 'w_ple': 2}
MXU_WEIGHTS = ('w_in', 'w_branch', 'w_gate', 'w_out', 'w_ff1', 'w_ff2', 'w_ple_gate', 'w_ple')
VEC_WEIGHTS = ('gla_w_a2', 'att_rel_bias', 'conv_dw_w', 'b_gate')
SHARDED = tuple(n for n in WEIGHTS if n in SHARD_AXIS)
REPLICATED = tuple(n for n in WEIGHTS if n not in SHARD_AXIS)


def _mm(a, b, dims=_NN):
    return lax.dot_general(a.astype(_MXU_DTYPE), b.astype(_MXU_DTYPE), dims, preferred_element_type=F32)


def _split3(x):
    x1 = x.astype(jnp.bfloat16)
    r1 = x - x1.astype(F32)
    x2 = r1.astype(jnp.bfloat16)
    x3 = (r1 - x2.astype(F32)).astype(jnp.bfloat16)
    return x1, x2, x3


def _mm_exact_rhs(m, x, dims=_NN):
    return sum(lax.dot_general(m, xi, dims, preferred_element_type=F32) for xi in _split3(x))


def _mm_exact_lhs(x, m, dims=_NN):
    return sum(lax.dot_general(xi, m, dims, preferred_element_type=F32) for xi in _split3(x))


def _sigmoid(x):
    return 1.0 / (1.0 + jnp.exp(-x))


def _gelu(x):
    c = 0.7978845608028654
    t = jnp.tanh(c * (x + 0.044715 * x * x * x))
    return 0.5 * x * (1.0 + t), t


def _gelu_grad(x, t):
    c = 0.7978845608028654
    return 0.5 * (1.0 + t) + 0.5 * x * (1.0 - t * t) * c * (1.0 + 3.0 * 0.044715 * x * x)


def _rms_stat(h):
    return lax.rsqrt(jnp.mean(h * h, axis=-1, keepdims=True) + EPS)


def _rms_bwd(dy, h, g, r):
    hh = h * r
    dhh = dy * g
    dh = r * (dhh - hh * jnp.mean(dhh * hh, axis=-1, keepdims=True))
    return dh, jnp.sum(dy * hh, axis=0, keepdims=True)


def _ln_fwd(x, g, b):
    mu = jnp.mean(x, axis=-1, keepdims=True)
    xc = x - mu
    rs = lax.rsqrt(jnp.mean(xc * xc, axis=-1, keepdims=True) + EPS)
    xh = xc * rs
    return xh * g + b, xh, rs


def _ln_bwd(dy, xh, rs, g):
    dxh = dy * g
    dx = rs * (dxh - jnp.mean(dxh, axis=-1, keepdims=True) - xh * jnp.mean(dxh * xh, axis=-1, keepdims=True))
    return dx, jnp.sum(dy * xh, axis=0, keepdims=True), jnp.sum(dy, axis=0, keepdims=True)


def _row_call(name, body, nt, rows=(), halos=(), res=(), outs=(), accs=(), scratch=(), reverse=False):
    def pos(i):
        return (nt - 1 - i) if reverse else i

    def lead(ndim, f):
        return lambda i: (f(pos(i)),) + (0,) * (ndim - 1)

    in_specs, operands = [], []
    for a, tile in rows:
        in_specs.append(pl.BlockSpec((tile,) + a.shape[1:], lead(a.ndim, lambda t: t)))
        operands.append(a)
    for a, blk, per, side in halos:
        last = a.shape[0] // blk - 1
        if side == 'prev':
            f = lambda t, per=per: jnp.maximum(t * per - 1, 0)
        else:
            f = lambda t, per=per, last=last: jnp.minimum((t + 1) * per, last)
        in_specs.append(pl.BlockSpec((blk,) + a.shape[1:], lead(a.ndim, f)))
        operands.append(a)
    for a in res:
        in_specs.append(pl.BlockSpec(a.shape, lambda i, nd=a.ndim: (0,) * nd, pipeline_mode=pl.Buffered(1)))
        operands.append(a)
    out_specs, out_shape = [], []
    for shape, dtype, tile in outs:
        out_specs.append(pl.BlockSpec((tile,) + tuple(shape[1:]), lead(len(shape), lambda t: t)))
        out_shape.append(jax.ShapeDtypeStruct(tuple(shape), dtype))
    for shape in accs:
        out_specs.append(pl.BlockSpec(tuple(shape), lambda i, nd=len(shape): (0,) * nd))
        out_shape.append(jax.ShapeDtypeStruct(tuple(shape), F32))
    n0, n1, n2, n3, n4 = len(rows), len(halos), len(res), len(outs), len(accs)

    def kern(*refs):
        i = pl.program_id(0)
        row_refs = refs[:n0]
        halo_refs = refs[n0:n0 + n1]
        res_refs = refs[n0 + n1:n0 + n1 + n2]
        out_refs = refs[n0 + n1 + n2:n0 + n1 + n2 + n3]
        acc_refs = refs[n0 + n1 + n2 + n3:n0 + n1 + n2 + n3 + n4]
        scr_refs = refs[n0 + n1 + n2 + n3 + n4:]

        @pl.when(i == 0)
        def _():
            for r in tuple(acc_refs) + tuple(scr_refs):
                r[...] = jnp.zeros(r.shape, r.dtype)

        body(pos(i), row_refs, halo_refs, res_refs, out_refs, acc_refs, scr_refs)

    result = pl.pallas_call(
        kern, grid=(nt,), in_specs=in_specs, out_specs=out_specs, out_shape=out_shape,
        scratch_shapes=[pltpu.VMEM(tuple(s), d) for s, d in scratch],
        compiler_params=pltpu.CompilerParams(dimension_semantics=("arbitrary",),
                                             vmem_limit_bytes=VMEM_LIMIT_MB << 20),
        name=name)(*operands)
    return tuple(result)


def _tn_call(name, a, b, k, n, nblk=1, a_col=False, b_col=True, b_off=0, out='cols', tile=1024):
    nt = a.shape[0] // tile
    if out == 'cols':
        o_shape, o_spec = (k, nblk * n), pl.BlockSpec((k, n), lambda j, t: (0, j))
    elif out == 'rows':
        o_shape, o_spec = (nblk * k, n), pl.BlockSpec((k, n), lambda j, t: (j, 0))
    else:
        o_shape, o_spec = (nblk, k, n), pl.BlockSpec((None, k, n), lambda j, t: (j, 0, 0))

    def kern(a_ref, b_ref, o_ref):
        @pl.when(pl.program_id(1) == 0)
        def _():
            o_ref[...] = jnp.zeros(o_ref.shape, o_ref.dtype)

        o_ref[...] += lax.dot_general(a_ref[...], b_ref[...], _TN, preferred_element_type=F32)

    return pl.pallas_call(
        kern, grid=(nblk, nt),
        in_specs=[pl.BlockSpec((tile, k), (lambda j, t: (t, j)) if a_col else (lambda j, t: (t, 0))),
                  pl.BlockSpec((tile, n), (lambda j, t: (t, j + b_off)) if b_col else (lambda j, t: (t, b_off)))],
        out_specs=o_spec, out_shape=jax.ShapeDtypeStruct(o_shape, F32),
        compiler_params=pltpu.CompilerParams(dimension_semantics=("arbitrary", "arbitrary"),
                                             vmem_limit_bytes=VMEM_LIMIT_MB << 20),
        name=name)(a, b)


def _inproj_fwd(tag, h, g1, w_groups, tm=256):
    t_len = h.shape[0]

    def body(t, rows, halos, res, outs, accs, scr):
        hv = rows[0][...]
        xn = (hv * _rms_stat(hv) * res[0][...]).astype(_MXU_DTYPE)
        outs[0][...] = xn
        for o, w in zip(outs[1:], res[1:]):
            o[...] = lax.dot_general(xn, w[...], _NN, preferred_element_type=F32)

    outs = [((t_len, D_MODEL), _MXU_DTYPE, tm)] + [((t_len, w.shape[1]), F32, tm) for w in w_groups]
    return _row_call("inproj_fwd" + tag, body, t_len // tm, rows=[(h, tm)], res=[g1] + list(w_groups), outs=outs)


def _sg_mask():
    row = lax.broadcasted_iota(jnp.int32, (SG_BLOCK, SG_BLOCK), 0)
    col = lax.broadcasted_iota(jnp.int32, (SG_BLOCK, SG_BLOCK), 1)
    return jnp.logical_or(row >= CHUNK, col < CHUNK)


def _sg_forward_parts(pa, lg, lb, w_ref, bt):
    tm = pa.shape[0]
    nb = tm // SG_BLOCK
    su, sv = pa[:, :BW], pa[:, BW:]
    u, tu = _gelu(su)
    gv, tv = _gelu(sv)
    vn, xh, rs = _ln_fwd(gv, lg, lb)
    mask = _sg_mask()
    wms, xs, ms = [], [], []
    for g in range(SG_GROUPS):
        wm = jnp.where(mask, w_ref[g], 0.0).astype(_MXU_DTYPE)
        xg = jnp.concatenate([vn[b * SG_BLOCK:(b + 1) * SG_BLOCK, g * 128:(g + 1) * 128] for b in range(nb)], axis=1)
        xg = xg.astype(_MXU_DTYPE)
        ms.append(lax.dot_general(wm, xg, _NN, preferred_element_type=F32) + bt[:, g:g + 1])
        wms.append(wm)
        xs.append(xg)
    mixed = _sg_unfold(ms, nb)
    return su, sv, u, tu, tv, xh, rs, wms, xs, mixed


def _sg_unfold(per_group, nb):
    return jnp.concatenate(
        [jnp.concatenate([per_group[g][:, b * 128:(b + 1) * 128] for g in range(SG_GROUPS)], axis=1)
         for b in range(nb)], axis=0)


def _sg_fwd(tag, proj_a, lg, lb, sg_w, sg_bt, tm=512):
    t_len = proj_a.shape[0]

    def body(t, rows, halos, res, outs, accs, scr):
        parts = _sg_forward_parts(rows[0][...], res[0][...], res[1][...], res[2], res[3][...])
        outs[0][...] = (parts[2] * parts[-1]).astype(_MXU_DTYPE)

    return _row_call("sg_fwd" + tag, body, t_len // tm, rows=[(proj_a, tm)], res=[lg, lb, sg_w, sg_bt],
                     outs=[((t_len, BW), _MXU_DTYPE, tm)])[0]


def _sg_bwd(tag, proj_a, dy, lg, lb, sg_w, sg_bt, tm=512):
    t_len = proj_a.shape[0]
    nb = tm // SG_BLOCK

    def body(t, rows, halos, res, outs, accs, scr):
        lgv = res[0][...]
        su, sv, u, tu, tv, xh, rs, wms, xs, mixed = _sg_forward_parts(rows[0][...], lgv, res[1][...], res[2], res[3][...])
        dyv = rows[1][...]
        dsu = dyv * mixed * _gelu_grad(su, tu)
        dmixed = dyv * u
        mask = _sg_mask()
        dxs, dbs = [], []
        for g in range(SG_GROUPS):
            dm = jnp.concatenate([dmixed[b * SG_BLOCK:(b + 1) * SG_BLOCK, g * 128:(g + 1) * 128] for b in range(nb)],
                                 axis=1)
            dmb = dm.astype(_MXU_DTYPE)
            dw = lax.dot_general(dmb, xs[g], _NT, preferred_element_type=F32)
            accs[0][g] += jnp.where(mask, dw, 0.0)
            dbs.append(jnp.sum(dm, axis=1, keepdims=True))
            dxs.append(lax.dot_general(wms[g], dmb, _TN, preferred_element_type=F32))
        accs[1][...] += jnp.concatenate(dbs, axis=1)
        dvn = _sg_unfold(dxs, nb)
        dgv, dlg, dlb = _ln_bwd(dvn, xh, rs, lgv)
        accs[2][...] += dlg
        accs[3][...] += dlb
        dsv = dgv * _gelu_grad(sv, tv)
        outs[0][...] = jnp.concatenate([dsu, dsv], axis=1).astype(_MXU_DTYPE)

    return _row_call("sg_bwd" + tag, body, t_len // tm, rows=[(proj_a, tm), (dy, tm)], res=[lg, lb, sg_w, sg_bt],
                     outs=[((t_len, 2 * BW), _MXU_DTYPE, tm)],
                     accs=[(SG_GROUPS, SG_BLOCK, SG_BLOCK), (SG_BLOCK, SG_GROUPS), (1, BW), (1, BW)])


def _tri(lower):
    row = lax.broadcasted_iota(jnp.int32, (CHUNK, CHUNK), 0)
    col = lax.broadcasted_iota(jnp.int32, (CHUNK, CHUNK), 1)
    return ((row >= col) if lower else (row <= col)).astype(jnp.bfloat16)


def _gla_gate(pa, wa2, ba):
    z = _mm(pa, wa2) + ba
    log_a = (jnp.minimum(z, 0.0) - jnp.log(1.0 + jnp.exp(-jnp.abs(z)))) * (1.0 / GLA_TAU)
    return z, log_a


def _gla_chunk_fwd(pb, log_a, j):
    sl = slice(j * CHUNK, (j + 1) * CHUNK)
    cum = _mm_exact_rhs(_tri(True), log_a[sl])
    tot = cum[CHUNK - 1:CHUNK]
    w = jnp.exp(tot - cum)
    k = pb[sl, 256:512]
    return sl, cum, tot, w, k * w, jnp.exp(tot)


def _gla_read(q, s_t):
    qs = (q * (GLA_DK ** -0.5)).astype(_MXU_DTYPE)
    sb = s_t.astype(_MXU_DTYPE)
    o = jnp.concatenate([lax.dot_general(qs[:, h * 64:(h + 1) * 64], sb[:, h * 64:(h + 1) * 64], _NT,
                                         preferred_element_type=F32) for h in range(GLA_HEADS)], axis=1)
    return qs, sb, o


def _gla_fwd(tag, proj_b, proj_a, wa2, ba, ng, tm=256):
    t_len = proj_b.shape[0]
    cpt = tm // CHUNK

    def body(t, rows, halos, res, outs, accs, scr):
        pb = rows[0][...]
        _, log_a = _gla_gate(rows[1][...], res[0][...], res[1][...])
        ngv = res[2][...]
        st = scr[0]
        for j in range(cpt):
            sl, cum, tot, w, kd, dec = _gla_chunk_fwd(pb, log_a, j)
            kdb = kd.astype(_MXU_DTYPE)
            vb = pb[sl, 512:1024].astype(_MXU_DTYPE)
            ut = jnp.concatenate([lax.dot_general(vb[:, h * 128:(h + 1) * 128], kdb[:, h * 64:(h + 1) * 64], _TN,
                                                  preferred_element_type=F32) for h in range(GLA_HEADS)], axis=1)
            s_new = dec * st[...] + ut
            st[...] = s_new
            outs[1][j] = s_new
            _, _, o = _gla_read(pb[sl, 0:256], s_new)
            on = jnp.concatenate(
                [o[:, h * 128:(h + 1) * 128] * lax.rsqrt(jnp.mean(jnp.square(o[:, h * 128:(h + 1) * 128]), axis=-1,
                                                                  keepdims=True) + EPS) for h in range(GLA_HEADS)], axis=1)
            r = pb[sl, 1024:1536]
            outs[0][sl, :] = (on * ngv * (r * _sigmoid(r))).astype(_MXU_DTYPE)

    return _row_call("gla_fwd" + tag, body, t_len // tm, rows=[(proj_b, tm), (proj_a, tm)], res=[wa2, ba, ng],
                     outs=[((t_len, BW), _MXU_DTYPE, tm), ((t_len // CHUNK, GLA_DV, 256), F32, cpt)],
                     scratch=[((GLA_DV, 256), F32)])


def _gla_bwd(tag, proj_b, proj_a, dy, states, wa2, ba, ng, tm=256):
    t_len = proj_b.shape[0]
    cpt = tm // CHUNK

    def body(t, rows, halos, res, outs, accs, scr):
        pb = rows[0][...]
        pa = rows[1][...]
        dyv = rows[2][...]
        st_ref = rows[3]
        wa2v = res[0][...]
        z, log_a = _gla_gate(pa, wa2v, res[1][...])
        ngv = res[2][...]
        dst = scr[0]
        s_before_tile = jnp.where(t > 0, halos[0][0], 0.0)
        dz_rows = [None] * cpt
        d_rows = [None] * cpt
        for j in reversed(range(cpt)):
            sl, cum, tot, w, kd, dec = _gla_chunk_fwd(pb, log_a, j)
            s_c = st_ref[j]
            s_prev = st_ref[j - 1] if j > 0 else s_before_tile
            qs, sb, o = _gla_read(pb[sl, 0:256], s_c)
            r = pb[sl, 1024:1536]
            sig = _sigmoid(r)
            sil = r * sig
            dyj = dyv[sl]
            dos, drs, dng = [], [], []
            for h in range(GLA_HEADS):
                hs = slice(h * 128, (h + 1) * 128)
                oh = o[:, hs]
                rstd = lax.rsqrt(jnp.mean(oh * oh, axis=-1, keepdims=True) + EPS)
                on = oh * rstd
                g_h = ngv[:, hs]
                don = dyj[:, hs] * g_h * sil[:, hs]
                dng.append(jnp.sum(dyj[:, hs] * on * sil[:, hs], axis=0, keepdims=True))
                drs.append(dyj[:, hs] * on * g_h * (sig[:, hs] * (1.0 + r[:, hs] * (1.0 - sig[:, hs]))))
                dos.append(rstd * (don - on * jnp.mean(don * on, axis=-1, keepdims=True)))
            accs[2][...] += jnp.concatenate(dng, axis=1)
            dr = jnp.concatenate(drs, axis=1)
            do = jnp.concatenate(dos, axis=1)
            dob = do.astype(_MXU_DTYPE)
            dst_tot = dst[...] + jnp.concatenate(
                [lax.dot_general(dob[:, h * 128:(h + 1) * 128], qs[:, h * 64:(h + 1) * 64], _TN,
                                 preferred_element_type=F32) for h in range(GLA_HEADS)], axis=1)
            dq = jnp.concatenate(
                [lax.dot_general(dob[:, h * 128:(h + 1) * 128], sb[:, h * 64:(h + 1) * 64], _NN,
                                 preferred_element_type=F32) for h in range(GLA_HEADS)], axis=1) * (GLA_DK ** -0.5)
            ddec = jnp.sum(dst_tot * s_prev, axis=0, keepdims=True)
            dst[...] = dec * dst_tot
            dub = dst_tot.astype(_MXU_DTYPE)
            vb = pb[sl, 512:1024].astype(_MXU_DTYPE)
            kdb = kd.astype(_MXU_DTYPE)
            dkd = jnp.concatenate(
                [lax.dot_general(vb[:, h * 128:(h + 1) * 128], dub[:, h * 64:(h + 1) * 64], _NN,
                                 preferred_element_type=F32) for h in range(GLA_HEADS)], axis=1)
            dv = jnp.concatenate(
                [lax.dot_general(kdb[:, h * 64:(h + 1) * 64], dub[:, h * 64:(h + 1) * 64], _NT,
                                 preferred_element_type=F32) for h in range(GLA_HEADS)], axis=1)
            dk = dkd * w
            e = dkd * kd
            dtot = jnp.sum(e, axis=0, keepdims=True) + ddec * dec
            last = lax.broadcasted_iota(jnp.int32, e.shape, 0) == CHUNK - 1
            dcum = jnp.where(last, dtot - e, -e)
            dla = _mm_exact_rhs(_tri(False), dcum)
            dz_rows[j] = dla * (1.0 / GLA_TAU) * _sigmoid(-z[sl])
            d_rows[j] = jnp.concatenate([dq, dk, dv, dr], axis=1)
        dz = jnp.concatenate(dz_rows, axis=0)
        dzb = dz.astype(_MXU_DTYPE)
        outs[0][...] = jnp.concatenate(d_rows, axis=0).astype(_MXU_DTYPE)
        outs[1][...] = lax.dot_general(dzb, wa2v.astype(_MXU_DTYPE), _NT, preferred_element_type=F32).astype(_MXU_DTYPE)
        accs[0][...] += lax.dot_general(pa.astype(_MXU_DTYPE), dzb, _TN, preferred_element_type=F32)
        accs[1][...] += jnp.sum(dz, axis=0, keepdims=True)

    return _row_call("gla_bwd" + tag, body, t_len // tm,
                     rows=[(proj_b, tm), (proj_a, tm), (dy, tm), (states, cpt)],
                     halos=[(states, 1, cpt, 'prev')], res=[wa2, ba, ng],
                     outs=[((t_len, 1536), _MXU_DTYPE, tm), ((t_len, GLA_RANK), _MXU_DTYPE, tm)],
                     accs=[(GLA_RANK, 256), (1, 256), (1, BW)], scratch=[((GLA_DV, 256), F32)], reverse=True)


def _rel_index():
    l_idx = np.arange(CHUNK)[:, None]
    m_idx = np.arange(BAND)[None, :]
    rel = l_idx + (ATT_BAND - 1) * CHUNK - m_idx
    return jnp.asarray((np.clip(rel, -(CHUNK - 1), MAX_REL) + (CHUNK - 1)).reshape(1, CHUNK * BAND), jnp.int32)


BIAS_COLS = 4096


def _bias_expand(tag, rel_bias):
    n = CHUNK * BAND

    def kern(rel_ref, idx_ref, o_ref):
        onehot = (lax.broadcasted_iota(jnp.int32, (REL_TABLE, BIAS_COLS), 0) == idx_ref[...]).astype(jnp.bfloat16)
        o_ref[...] = _mm_exact_lhs(rel_ref[...], onehot)

    return pl.pallas_call(
        kern, grid=(n // BIAS_COLS,),
        in_specs=[pl.BlockSpec((ATT_HEADS, REL_TABLE), lambda i: (0, 0)), pl.BlockSpec((1, BIAS_COLS), lambda i: (0, i))],
        out_specs=pl.BlockSpec((ATT_HEADS, BIAS_COLS), lambda i: (0, i)),
        out_shape=jax.ShapeDtypeStruct((ATT_HEADS, n), F32), name="bias_expand" + tag)(rel_bias, _rel_index())


def _bias_reduce(tag, dbias):
    n = CHUNK * BAND

    def kern(db_ref, idx_ref, o_ref):
        @pl.when(pl.program_id(0) == 0)
        def _():
            o_ref[...] = jnp.zeros(o_ref.shape, o_ref.dtype)

        onehot = (lax.broadcasted_iota(jnp.int32, (REL_TABLE, BIAS_COLS), 0) == idx_ref[...]).astype(jnp.bfloat16)
        o_ref[...] += _mm_exact_lhs(db_ref[...], onehot, _NT)

    return pl.pallas_call(
        kern, grid=(n // BIAS_COLS,),
        in_specs=[pl.BlockSpec((ATT_HEADS, BIAS_COLS), lambda i: (0, i)), pl.BlockSpec((1, BIAS_COLS), lambda i: (0, i))],
        out_specs=pl.BlockSpec((ATT_HEADS, REL_TABLE), lambda i: (0, 0)),
        out_shape=jax.ShapeDtypeStruct((ATT_HEADS, REL_TABLE), F32),
        compiler_params=pltpu.CompilerParams(dimension_semantics=("arbitrary",)),
        name="bias_reduce" + tag)(dbias, _rel_index())


def _attn_scores(q, kb, bias_h, ok, h):
    hs = slice(h * ATT_HD, (h + 1) * ATT_HD)
    s = lax.dot_general(q[:, hs], kb[:, hs], _NT, preferred_element_type=F32) + bias_h
    s = jnp.where(ok, s, NEG_INF)
    e = jnp.exp(s - jnp.max(s, axis=-1, keepdims=True))
    return e / jnp.sum(e, axis=-1, keepdims=True)


def _attn_stage_kv(t, pc_ref, pp_ref, kv):
    tm = pc_ref.shape[0]
    kv[0:tm, :] = jnp.where(t > 0, pp_ref[:, 512:1536], 0.0).astype(kv.dtype)
    kv[tm:2 * tm, :] = pc_ref[:, 512:1536].astype(kv.dtype)


def _attn_key_ok(t, j, tm):
    m_idx = lax.broadcasted_iota(jnp.int32, (CHUNK, BAND), 1)
    return (t * tm + j * CHUNK - (ATT_BAND - 1) * CHUNK + m_idx) >= 0


def _attn_fwd(tag, proj_c, bias, tm=512):
    t_len = proj_c.shape[0]
    off = tm - (ATT_BAND - 1) * CHUNK

    def body(t, rows, halos, res, outs, accs, scr):
        pc_ref, pp_ref, b_ref, kv = rows[0], halos[0], res[0], scr[0]
        _attn_stage_kv(t, pc_ref, pp_ref, kv)

        def chunk(j, carry):
            r0 = pl.multiple_of(j * CHUNK, CHUNK)
            q = (pc_ref[pl.ds(r0, CHUNK), 0:512] * (ATT_HD ** -0.5)).astype(_MXU_DTYPE)
            band = pl.ds(pl.multiple_of(off + j * CHUNK, CHUNK), BAND)
            kb = kv[band, 0:512]
            vb = kv[band, 512:1024]
            ok = _attn_key_ok(t, j, tm)
            o = [lax.dot_general(_attn_scores(q, kb, b_ref[h], ok, h).astype(_MXU_DTYPE),
                                 vb[:, h * ATT_HD:(h + 1) * ATT_HD], _NN, preferred_element_type=F32)
                 for h in range(ATT_HEADS)]
            outs[0][pl.ds(r0, CHUNK), :] = jnp.concatenate(o, axis=1).astype(_MXU_DTYPE)
            return carry

        lax.fori_loop(0, tm // CHUNK, chunk, 0)

    return _row_call("attn_fwd" + tag, body, t_len // tm, rows=[(proj_c, tm)], halos=[(proj_c, tm, 1, 'prev')],
                     res=[bias], outs=[((t_len, BW), _MXU_DTYPE, tm)], scratch=[((2 * tm, 1024), _MXU_DTYPE)])[0]


def _attn_bwd(tag, proj_c, dy, bias, tm=512):
    t_len = proj_c.shape[0]
    off = tm - (ATT_BAND - 1) * CHUNK
    scale = ATT_HD ** -0.5

    def body(t, rows, halos, res, outs, accs, scr):
        pc_ref, dy_ref, pp_ref, b_ref, kv, dkv = rows[0], rows[1], halos[0], res[0], scr[0], scr[1]
        _attn_stage_kv(t, pc_ref, pp_ref, kv)
        dkv[...] = jnp.zeros(dkv.shape, dkv.dtype)

        def chunk(j, carry):
            r0 = pl.multiple_of(j * CHUNK, CHUNK)
            q = (pc_ref[pl.ds(r0, CHUNK), 0:512] * scale).astype(_MXU_DTYPE)
            do = dy_ref[pl.ds(r0, CHUNK), :].astype(_MXU_DTYPE)
            band = pl.ds(pl.multiple_of(off + j * CHUNK, CHUNK), BAND)
            kb = kv[band, 0:512]
            vb = kv[band, 512:1024]
            ok = _attn_key_ok(t, j, tm)
            dqs, dks, dvs = [], [], []
            for h in range(ATT_HEADS):
                hs = slice(h * ATT_HD, (h + 1) * ATT_HD)
                p = _attn_scores(q, kb, b_ref[h], ok, h)
                dp = lax.dot_general(do[:, hs], vb[:, hs], _NT, preferred_element_type=F32)
                ds = p * (dp - jnp.sum(dp * p, axis=-1, keepdims=True))
                accs[0][h] += ds
                dsb = ds.astype(_MXU_DTYPE)
                dqs.append(lax.dot_general(dsb, kb[:, hs], _NN, preferred_element_type=F32) * scale)
                dks.append(lax.dot_general(dsb, q[:, hs], _TN, preferred_element_type=F32))
                dvs.append(lax.dot_general(p.astype(_MXU_DTYPE), do[:, hs], _TN, preferred_element_type=F32))
            outs[0][pl.ds(r0, CHUNK), :] = jnp.concatenate(dqs, axis=1).astype(_MXU_DTYPE)
            dkv[band, :] += jnp.concatenate(dks + dvs, axis=1)
            return carry

        lax.fori_loop(0, tm // CHUNK, chunk, 0)
        outs[1][...] = dkv[tm:2 * tm, :]
        outs[2][...] = dkv[0:tm, :]

    return _row_call("attn_bwd" + tag, body, t_len // tm, rows=[(proj_c, tm), (dy, tm)],
                     halos=[(proj_c, tm, 1, 'prev')], res=[bias],
                     outs=[((t_len, BW), _MXU_DTYPE, tm), ((t_len, 1024), F32, tm), ((t_len, 1024), F32, tm)],
                     accs=[(ATT_HEADS, CHUNK, BAND)],
                     scratch=[((2 * tm, 1024), _MXU_DTYPE), ((2 * tm, 1024), F32)])


def _attn_combine(tag, dq, dkv_own, dkv_prev, tm=512):
    t_len = dq.shape[0]
    nt = t_len // tm

    def body(t, rows, halos, res, outs, accs, scr):
        dkv = rows[1][...] + jnp.where(t < nt - 1, halos[0][...], 0.0)
        outs[0][...] = jnp.concatenate([rows[0][...], dkv.astype(_MXU_DTYPE)], axis=1)

    return _row_call("attn_combine" + tag, body, nt, rows=[(dq, tm), (dkv_own, tm)],
                     halos=[(dkv_prev, tm, 1, 'next')], outs=[((t_len, 1536), _MXU_DTYPE, tm)])[0]


def _conv_glu(pd):
    a, g = pd[:, :BW], pd[:, BW:]
    sig = _sigmoid(g)
    return a, sig, a * sig


def _conv_stage(t, pd_ref, ph_ref, win):
    pd = pd_ref[...]
    a, sig, y0 = _conv_glu(pd)
    win[0:CONV_HALO, :] = jnp.where(t > 0, _conv_glu(ph_ref[...])[2], 0.0)
    win[CONV_HALO:, :] = y0
    return a, sig


def _conv_fwd(tag, proj_d, dw_w, dw_b, ln_g, ln_b, tm=512):
    t_len = proj_d.shape[0]
    lead = CONV_HALO - (CONV_K - 1)

    def body(t, rows, halos, res, outs, accs, scr):
        win = scr[0]
        _conv_stage(t, rows[0], halos[0], win)
        w_ref = res[0]
        yc = res[1][...] + w_ref[0:1, :] * win[pl.ds(lead, tm), :]
        for j in range(1, CONV_K):
            yc = yc + w_ref[j:j + 1, :] * win[pl.ds(lead + j, tm), :]
        outs[1][...] = yc
        yl, _, _ = _ln_fwd(yc, res[2][...], res[3][...])
        outs[0][...] = (yl * _sigmoid(yl)).astype(_MXU_DTYPE)

    return _row_call("conv_fwd" + tag, body, t_len // tm, rows=[(proj_d, tm)],
                     halos=[(proj_d, CONV_HALO, tm // CONV_HALO, 'prev')], res=[dw_w, dw_b, ln_g, ln_b],
                     outs=[((t_len, BW), _MXU_DTYPE, tm), ((t_len, BW), F32, tm)],
                     scratch=[((tm + CONV_HALO, BW), F32)])


def _conv_bwd_norm(tag, yc, dy, ln_g, ln_b, tm=512):
    t_len = yc.shape[0]

    def body(t, rows, halos, res, outs, accs, scr):
        lgv = res[0][...]
        yl, xh, rs = _ln_fwd(rows[0][...], lgv, res[1][...])
        sig = _sigmoid(yl)
        dyl = rows[1][...] * (sig * (1.0 + yl * (1.0 - sig)))
        dyc, dlg, dlb = _ln_bwd(dyl, xh, rs, lgv)
        outs[0][...] = dyc
        accs[0][...] += dlg
        accs[1][...] += dlb
        accs[2][...] += jnp.sum(dyc, axis=0, keepdims=True)

    return _row_call("conv_bwd_norm" + tag, body, t_len // tm, rows=[(yc, tm), (dy, tm)], res=[ln_g, ln_b],
                     outs=[((t_len, BW), F32, tm)], accs=[(1, BW), (1, BW), (1, BW)])


def _conv_bwd_taps(tag, proj_d, dyc, dw_w, tm=512):
    t_len = proj_d.shape[0]
    nt = t_len // tm
    lead = CONV_HALO - (CONV_K - 1)

    def body(t, rows, halos, res, outs, accs, scr):
        win, wd = scr[0], scr[1]
        a, sig = _conv_stage(t, rows[0], halos[0], win)
        dycv = rows[1][...]
        wd[0:tm, :] = dycv
        wd[tm:, :] = jnp.where(t < nt - 1, halos[1][...], 0.0)
        w_ref = res[0]
        dy0 = jnp.zeros((tm, BW), F32)
        for j in range(CONV_K):
            dy0 = dy0 + w_ref[j:j + 1, :] * wd[pl.ds(CONV_K - 1 - j, tm), :]
            accs[0][j:j + 1, :] += jnp.sum(dycv * win[pl.ds(lead + j, tm), :], axis=0, keepdims=True)
        outs[0][...] = jnp.concatenate([dy0 * sig, dy0 * a * sig * (1.0 - sig)], axis=1).astype(_MXU_DTYPE)

    return _row_call("conv_bwd_taps" + tag, body, nt, rows=[(proj_d, tm), (dyc, tm)],
                     halos=[(proj_d, CONV_HALO, tm // CONV_HALO, 'prev'), (dyc, CONV_HALO, tm // CONV_HALO, 'next')],
                     res=[dw_w], outs=[((t_len, 2 * BW), _MXU_DTYPE, tm)], accs=[(CONV_K, BW)],
                     scratch=[((tm + CONV_HALO, BW), F32), ((tm + CONV_HALO, BW), F32)])


def _merge_fwd(tag, h, xn, ys, w_gate, b_gate, w_branch, w_out, tm=256):
    t_len = h.shape[0]

    def body(t, rows, halos, res, outs, accs, scr):
        xnv = rows[1][...]
        wg_ref, bg_ref, wb_ref, wo_ref = res
        merged = jnp.zeros((tm, D_MODEL), F32)
        for n in range(4):
            cs = slice(n * D_MODEL, (n + 1) * D_MODEL)
            gate = _sigmoid(lax.dot_general(xnv, wg_ref[n], _NN, preferred_element_type=F32) + bg_ref[n:n + 1, :])
            bo = lax.dot_general(rows[2 + n][...], wb_ref[n], _NN, preferred_element_type=F32)
            outs[0][:, cs] = gate
            outs[1][:, cs] = bo.astype(_MXU_DTYPE)
            merged = merged + gate * bo
        mb = merged.astype(_MXU_DTYPE)
        outs[2][...] = mb
        outs[3][...] = rows[0][...] + lax.dot_general(mb, wo_ref[...], _NN, preferred_element_type=F32)

    return _row_call("merge_fwd" + tag, body, t_len // tm, rows=[(h, tm), (xn, tm)] + [(y, tm) for y in ys],
                     res=[w_gate, b_gate, w_branch, w_out],
                     outs=[((t_len, 4 * D_MODEL), F32, tm), ((t_len, 4 * D_MODEL), _MXU_DTYPE, tm),
                           ((t_len, D_MODEL), _MXU_DTYPE, tm), ((t_len, D_MODEL), F32, tm)])


def _merge_bwd(tag, dh, gate, bo, w_gate, w_branch, w_out, tm=256):
    t_len = dh.shape[0]

    def body(t, rows, halos, res, outs, accs, scr):
        wg_ref, wb_ref, wo_ref = res
        dhb = rows[0][...].astype(_MXU_DTYPE)
        outs[0][...] = dhb
        dmerged = lax.dot_general(dhb, wo_ref[...], _NT, preferred_element_type=F32)
        dxn = jnp.zeros((tm, D_MODEL), F32)
        dbg = []
        for n in range(4):
            cs = slice(n * D_MODEL, (n + 1) * D_MODEL)
            g = rows[1][:, cs]
            dbo = (dmerged * g).astype(_MXU_DTYPE)
            dgp = dmerged * rows[2][:, cs].astype(F32) * (g * (1.0 - g))
            dgb = dgp.astype(_MXU_DTYPE)
            outs[1][:, cs] = dbo
            outs[2][:, cs] = dgb
            outs[3][:, n * BW:(n + 1) * BW] = lax.dot_general(dbo, wb_ref[n], _NT, preferred_element_type=F32)
            dxn = dxn + lax.dot_general(dgb, wg_ref[n], _NT, preferred_element_type=F32)
            dbg.append(jnp.sum(dgp, axis=0, keepdims=True))
        outs[4][...] = dxn
        accs[0][...] += jnp.concatenate(dbg, axis=1)

    return _row_call("merge_bwd" + tag, body, t_len // tm, rows=[(dh, tm), (gate, tm), (bo, tm)],
                     res=[w_gate, w_branch, w_out],
                     outs=[((t_len, D_MODEL), _MXU_DTYPE, tm), ((t_len, 4 * D_MODEL), _MXU_DTYPE, tm),
                           ((t_len, 4 * D_MODEL), _MXU_DTYPE, tm), ((t_len, 4 * BW), F32, tm),
                           ((t_len, D_MODEL), F32, tm)],
                     accs=[(1, 4 * D_MODEL)])


FF_COLS = 1024


def _ffn_fwd(tag, h, g2, w1, w2, tm=256):
    t_len = h.shape[0]

    def body(t, rows, halos, res, outs, accs, scr):
        hv = rows[0][...]
        hn = (hv * _rms_stat(hv) * res[0][...]).astype(_MXU_DTYPE)
        outs[0][...] = hn
        acc = hv
        for c in range(D_FF // FF_COLS):
            cs = slice(c * FF_COLS, (c + 1) * FF_COLS)
            pre = lax.dot_general(hn, res[1][:, cs], _NN, preferred_element_type=F32)
            outs[1][:, cs] = pre
            ff = jnp.square(jnp.maximum(pre, 0.0)).astype(_MXU_DTYPE)
            acc = acc + lax.dot_general(ff, res[2][cs, :], _NN, preferred_element_type=F32)
        outs[2][...] = acc

    return _row_call("ffn_fwd" + tag, body, t_len // tm, rows=[(h, tm)], res=[g2, w1, w2],
                     outs=[((t_len, D_MODEL), _MXU_DTYPE, tm), ((t_len, D_FF), F32, tm), ((t_len, D_MODEL), F32, tm)])


def _ffn_bwd(tag, dh, h, pre, g2, w1, w2, tm=256):
    t_len = dh.shape[0]

    def body(t, rows, halos, res, outs, accs, scr):
        dhv = rows[0][...]
        hv = rows[1][...]
        dhb = dhv.astype(_MXU_DTYPE)
        outs[0][...] = dhb
        dhn = jnp.zeros((tm, D_MODEL), F32)
        for c in range(D_FF // FF_COLS):
            cs = slice(c * FF_COLS, (c + 1) * FF_COLS)
            r = jnp.maximum(rows[2][:, cs], 0.0)
            outs[1][:, cs] = (r * r).astype(_MXU_DTYPE)
            dpre = (lax.dot_general(dhb, res[2][cs, :], _NT, preferred_element_type=F32) * (2.0 * r)).astype(_MXU_DTYPE)
            outs[2][:, cs] = dpre
            dhn = dhn + lax.dot_general(dpre, res[1][:, cs], _NT, preferred_element_type=F32)
        dres, dg = _rms_bwd(dhn, hv, res[0][...], _rms_stat(hv))
        outs[3][...] = dhv + dres
        accs[0][...] += dg

    return _row_call("ffn_bwd" + tag, body, t_len // tm, rows=[(dh, tm), (h, tm), (pre, tm)], res=[g2, w1, w2],
                     outs=[((t_len, D_MODEL), _MXU_DTYPE, tm), ((t_len, D_FF), _MXU_DTYPE, tm),
                           ((t_len, D_FF), _MXU_DTYPE, tm), ((t_len, D_MODEL), F32, tm)],
                     accs=[(1, D_MODEL)])


def _ple_fwd(tag, h, p, g3, w_pg, b_pg, w_ple, tm=256):
    t_len = h.shape[0]

    def body(t, rows, halos, res, outs, accs, scr):
        hv = rows[0][...]
        hg = (hv * _rms_stat(hv) * res[0][...]).astype(_MXU_DTYPE)
        pb = rows[1][...].astype(_MXU_DTYPE)
        pg = _sigmoid(lax.dot_general(hg, res[1][...], _NN, preferred_element_type=F32) + res[2][...])
        pe = lax.dot_general(pb, res[3][...], _NN, preferred_element_type=F32)
        outs[0][...] = hg
        outs[1][...] = pb
        outs[2][...] = pg
        outs[3][...] = pe
        outs[4][...] = hv + pg * pe

    return _row_call("ple_fwd" + tag, body, t_len // tm, rows=[(h, tm), (p, tm)], res=[g3, w_pg, b_pg, w_ple],
                     outs=[((t_len, D_MODEL), _MXU_DTYPE, tm), ((t_len, PLE_DIM), _MXU_DTYPE, tm),
                           ((t_len, D_MODEL), F32, tm), ((t_len, D_MODEL), F32, tm), ((t_len, D_MODEL), F32, tm)])


def _ple_bwd(tag, dh, h, pg, pe, g3, w_pg, tm=256):
    t_len = dh.shape[0]

    def body(t, rows, halos, res, outs, accs, scr):
        dhv = rows[0][...]
        hv = rows[1][...]
        pgv = rows[2][...]
        dgp = dhv * rows[3][...] * (pgv * (1.0 - pgv))
        dgb = dgp.astype(_MXU_DTYPE)
        outs[0][...] = dgb
        outs[1][...] = (dhv * pgv).astype(_MXU_DTYPE)
        dhg = lax.dot_general(dgb, res[1][...], _NT, preferred_element_type=F32)
        dres, dg = _rms_bwd(dhg, hv, res[0][...], _rms_stat(hv))
        outs[2][...] = dhv + dres
        accs[0][...] += jnp.sum(dgp, axis=0, keepdims=True)
        accs[1][...] += dg

    return _row_call("ple_bwd" + tag, body, t_len // tm, rows=[(dh, tm), (h, tm), (pg, tm), (pe, tm)],
                     res=[g3, w_pg],
                     outs=[((t_len, D_MODEL), _MXU_DTYPE, tm), ((t_len, D_MODEL), _MXU_DTYPE, tm),
                           ((t_len, D_MODEL), F32, tm)],
                     accs=[(1, D_MODEL), (1, D_MODEL)])


def _inproj_bwd(tag, dh, h, dxn_gate, dprojs, g1, w_groups, tm=256):
    t_len = dh.shape[0]

    def body(t, rows, halos, res, outs, accs, scr):
        hv = rows[1][...]
        dxn = rows[2][...]
        for dp, w in zip(rows[3:], res[1:]):
            dxn = dxn + lax.dot_general(dp[...], w[...], _NT, preferred_element_type=F32)
        dres, dg = _rms_bwd(dxn, hv, res[0][...], _rms_stat(hv))
        outs[0][...] = rows[0][...] + dres
        accs[0][...] += dg

    return _row_call("inproj_bwd" + tag, body, t_len // tm,
                     rows=[(dh, tm), (h, tm), (dxn_gate, tm)] + [(d, tm) for d in dprojs],
                     res=[g1] + list(w_groups), outs=[((t_len, D_MODEL), F32, tm)], accs=[(1, D_MODEL)])


def _loss_head(h, target, gf, tm=512):
    t_len = h.shape[0]

    def body(t, rows, halos, res, outs, accs, scr):
        hv = rows[0][...]
        g = res[0][...]
        r = _rms_stat(hv)
        diff = hv * r * g - rows[1][...]
        accs[0][...] += 0.5 * jnp.sum(jnp.mean(diff * diff, axis=-1, keepdims=True), axis=0, keepdims=True)
        dh, dg = _rms_bwd(diff * (1.0 / D_MODEL), hv, g, r)
        outs[0][...] = dh
        accs[1][...] += dg

    return _row_call("loss_head", body, t_len // tm, rows=[(h, tm), (target, tm)], res=[gf],
                     outs=[((t_len, D_MODEL), F32, tm)], accs=[(1, 128), (1, D_MODEL)])


def _row(v):
    return v.reshape(1, -1)


def _layer_fwd(i, h, p_i, w):
    tag = "_l%d" % i
    win = [w['w_in'][i][:, s:s + n] for _, s, n in IN_GROUPS]
    xn, pa, pb, pr, pc, pd = _inproj_fwd(tag, h, _row(w['norm1_g'][i]), win)
    sg_bt = w['sg_b'][i].T
    y_a = _sg_fwd(tag, pa, _row(w['sg_ln_g'][i]), _row(w['sg_ln_b'][i]), w['sg_w'][i], sg_bt)
    y_b, states = _gla_fwd(tag, pb, pr, w['gla_w_a2'][i], _row(w['gla_b_a'][i]), _row(w['gla_norm_g'][i]))
    bias = _bias_expand(tag, w['att_rel_bias'][i]).reshape(ATT_HEADS, CHUNK, BAND)
    y_c = _attn_fwd(tag, pc, bias)
    y_d, yc = _conv_fwd(tag, pd, w['conv_dw_w'][i], _row(w['conv_dw_b'][i]), _row(w['conv_ln_g'][i]),
                        _row(w['conv_ln_b'][i]))
    ys = (y_a, y_b, y_c, y_d)
    gate, bo, merged, h1 = _merge_fwd(tag, h, xn, ys, w['w_gate'][i], w['b_gate'][i], w['w_branch'][i], w['w_out'][i])
    hn, pre, h2 = _ffn_fwd(tag, h1, _row(w['norm2_g'][i]), w['w_ff1'][i], w['w_ff2'][i])
    hg, p_b, pg, pe, h3 = _ple_fwd(tag, h2, p_i, _row(w['norm3_g'][i]), w['w_ple_gate'][i], _row(w['b_ple_gate'][i]),
                                   w['w_ple'][i])
    saved = dict(h=h, xn=xn, pa=pa, pb=pb, pr=pr, pc=pc, pd=pd, states=states, bias=bias, yc=yc, ys=ys, gate=gate,
                 bo=bo, merged=merged, h1=h1, hn=hn, pre=pre, h2=h2, hg=hg, p_b=p_b, pg=pg, pe=pe, win=win,
                 sg_bt=sg_bt)
    return h3, saved


def _layer_bwd(i, dh3, s, w):
    tag = "_l%d" % i
    g = {}
    dgp, dpe, dh2, db_pg, dg3 = _ple_bwd(tag, dh3, s['h2'], s['pg'], s['pe'], _row(w['norm3_g'][i]), w['w_ple_gate'][i])
    g['b_ple_gate'], g['norm3_g'] = db_pg[0], dg3[0]
    g['w_ple_gate'] = _tn_call("dw_ple_gate" + tag, s['hg'], dgp, D_MODEL, D_MODEL)
    g['w_ple'] = _tn_call("dw_ple" + tag, s['p_b'], dpe, PLE_DIM, D_MODEL)

    dh2b, ffb, dpre, dh1, dg2 = _ffn_bwd(tag, dh2, s['h1'], s['pre'], _row(w['norm2_g'][i]), w['w_ff1'][i], w['w_ff2'][i])
    g['norm2_g'] = dg2[0]
    g['w_ff1'] = _tn_call("dw_ff1" + tag, s['hn'], dpre, D_MODEL, FF_COLS, nblk=D_FF // FF_COLS)
    g['w_ff2'] = _tn_call("dw_ff2" + tag, ffb, dh2b, FF_COLS, D_MODEL, nblk=D_FF // FF_COLS, a_col=True, b_col=False,
                          out='rows')

    dh1b, dbo, dgpre, dys, dxn_gate, db_gate = _merge_bwd(tag, dh1, s['gate'], s['bo'], w['w_gate'][i],
                                                          w['w_branch'][i], w['w_out'][i])
    g['b_gate'] = db_gate.reshape(4, D_MODEL)
    g['w_out'] = _tn_call("dw_out" + tag, s['merged'], dh1b, D_MODEL, D_MODEL)
    g['w_gate'] = _tn_call("dw_gate" + tag, s['xn'], dgpre, D_MODEL, D_MODEL, nblk=4, out='stack')
    g['w_branch'] = jnp.stack([_tn_call("dw_branch%d%s" % (n, tag), s['ys'][n], dbo, BW, D_MODEL, b_off=n)
                             for n in range(4)])
    dy_a, dy_b, dy_c, dy_d = (dys[:, n * BW:(n + 1) * BW] for n in range(4))

    lg, lb = _row(w['sg_ln_g'][i]), _row(w['sg_ln_b'][i])
    dpa, dsg_w, dsg_bt, dlg, dlb = _sg_bwd(tag, s['pa'], dy_a, lg, lb, w['sg_w'][i], s['sg_bt'])
    g['sg_w'], g['sg_b'], g['sg_ln_g'], g['sg_ln_b'] = dsg_w, dsg_bt.T, dlg[0], dlb[0]

    dpb, dpr, dwa2, dba, dng = _gla_bwd(tag, s['pb'], s['pr'], dy_b, s['states'], w['gla_w_a2'][i],
                                        _row(w['gla_b_a'][i]), _row(w['gla_norm_g'][i]))
    g['gla_w_a2'], g['gla_b_a'], g['gla_norm_g'] = dwa2, dba[0], dng[0]

    dq, dkv_own, dkv_prev, dbias = _attn_bwd(tag, s['pc'], dy_c, s['bias'])
    dpc = _attn_combine(tag, dq, dkv_own, dkv_prev)
    g['att_rel_bias'] = _bias_reduce(tag, dbias.reshape(ATT_HEADS, CHUNK * BAND))

    cg, cb = _row(w['conv_ln_g'][i]), _row(w['conv_ln_b'][i])
    dyc, dcg, dcb, ddwb = _conv_bwd_norm(tag, s['yc'], dy_d, cg, cb)
    dpd, ddw = _conv_bwd_taps(tag, s['pd'], dyc, w['conv_dw_w'][i])
    g['conv_ln_g'], g['conv_ln_b'], g['conv_dw_b'], g['conv_dw_w'] = dcg[0], dcb[0], ddwb[0], ddw

    dprojs = (dpa, dpb, dpr, dpc, dpd)
    dh0, dg1 = _inproj_bwd(tag, dh1, s['h'], dxn_gate, dprojs, _row(w['norm1_g'][i]), s['win'])
    g['norm1_g'] = dg1[0]
    g['w_in'] = jnp.concatenate([_tn_call("dw_in%s%s" % (name, tag), s['xn'], dp, D_MODEL, n)
                                 for (name, _, n), dp in zip(IN_GROUPS, dprojs)], axis=1)
    return dh0, g


def _local_step(x, p, target, w):
    h = x
    saved = []
    for i in range(DEPTH):
        h, s = _layer_fwd(i, h, p[i], w)
        saved.append(s)
    dh, loss, dgf = _loss_head(h, target, _row(w['final_g']))
    per_layer = [None] * DEPTH
    for i in reversed(range(DEPTH)):
        dh, per_layer[i] = _layer_bwd(i, dh, saved[i], w)
    grads = {n: jnp.stack([per_layer[i][n] for i in range(DEPTH)]) for n in WEIGHTS if n != 'final_g'}
    grads['final_g'] = dgf[0]
    return loss[0, 0], dh, grads


def _peers():
    x, y, c = lax.axis_index("x"), lax.axis_index("y"), lax.axis_index("c")
    me = 4 * x + 2 * y + c
    out = []
    for k in range(1, N_DEV):
        px = (1 - x) if k & 4 else x
        py = (1 - y) if k & 2 else y
        pc = (1 - c) if k & 1 else c
        out.append((k - 1, (px, py, pc), 4 * px + 2 * py + pc))
    return me, out


def _exchange(name, src, scatter):
    shape = src.shape[-2:]

    def body(src_ref, out_ref, send_sems, recv_sems, local_sem):
        me, peers = _peers()

        def piece(slot):
            return src_ref.at[slot] if scatter else src_ref

        local = pltpu.make_async_copy(piece(me), out_ref.at[me], local_sem)
        local.start()
        sends = []
        for k, pos, flat in peers:
            cp = pltpu.make_async_remote_copy(src_ref=piece(flat), dst_ref=out_ref.at[me], send_sem=send_sems.at[k],
                                              recv_sem=recv_sems.at[k], device_id=pos,
                                              device_id_type=pl.DeviceIdType.MESH)
            cp.start()
            sends.append(cp)
        for k, pos, flat in peers:
            pltpu.make_async_remote_copy(src_ref=piece(flat), dst_ref=out_ref.at[flat], send_sem=send_sems.at[k],
                                         recv_sem=recv_sems.at[k], device_id=pos,
                                         device_id_type=pl.DeviceIdType.MESH).wait_recv()
        for cp in sends:
            cp.wait_send()
        local.wait()

    return pl.pallas_call(
        body, out_shape=jax.ShapeDtypeStruct((N_DEV,) + tuple(shape), src.dtype),
        in_specs=[pl.BlockSpec(memory_space=pl.ANY)], out_specs=pl.BlockSpec(memory_space=pl.ANY),
        scratch_shapes=[pltpu.SemaphoreType.DMA((N_DEV - 1,)), pltpu.SemaphoreType.DMA((N_DEV - 1,)),
                        pltpu.SemaphoreType.DMA],
        name=name)(src)


def _pack(arrays, dtype, lead=None):
    flat = [a.astype(dtype).reshape((lead, -1) if lead else (-1,)) for a in arrays]
    cat = jnp.concatenate(flat, axis=-1)
    n = cat.shape[-1]
    unit = ADAMW_TILE if n >= ADAMW_TILE * PACK_COLS else 16
    rows = -(-n // (PACK_COLS * unit)) * unit
    pad = rows * PACK_COLS - n
    if pad:
        cat = jnp.pad(cat, ((0, 0), (0, pad)) if lead else ((0, pad),))
    return cat.reshape((lead, rows, PACK_COLS) if lead else (rows, PACK_COLS))


def _unpack(buf, shapes, lead=None):
    flat = buf.reshape((lead, -1) if lead else (-1,))
    out, off = [], 0
    for shp in shapes:
        n = int(np.prod(shp))
        piece = flat[..., off:off + n]
        out.append(piece.reshape(((lead,) if lead else ()) + tuple(shp)))
        off += n
    return out


def _to_slabs(full, axis):
    shp = full.shape
    split = full.reshape(shp[:axis] + (N_DEV, shp[axis] // N_DEV) + shp[axis + 1:])
    return jnp.moveaxis(split, axis, 0)


def _from_slabs(slabs, axis):
    moved = jnp.moveaxis(slabs, 0, axis)
    shp = moved.shape
    return moved.reshape(shp[:axis] + (shp[axis] * shp[axis + 1],) + shp[axis + 2:])


def _adamw(name, partials, w, m, v):
    rows = w.shape[0]
    tile = ADAMW_TILE if rows % ADAMW_TILE == 0 else 16
    c1 = 1.0 - ADAM_B1 ** ADAM_STEP
    c2 = 1.0 - ADAM_B2 ** ADAM_STEP

    def kern(p_ref, w_ref, m_ref, v_ref, g_ref, d_ref, nm_ref, nv_ref):
        g = p_ref[0]
        for s in range(1, N_DEV):
            g = g + p_ref[s]
        nm = ADAM_B1 * m_ref[...] + (1.0 - ADAM_B1) * g
        nv = ADAM_B2 * v_ref[...] + (1.0 - ADAM_B2) * jnp.square(g)
        g_ref[...] = g
        nm_ref[...] = nm
        nv_ref[...] = nv
        d_ref[...] = -ADAM_LR * ((nm / c1) / (jnp.sqrt(nv / c2) + ADAM_EPS) + ADAM_WD * w_ref[...])

    blk = pl.BlockSpec((tile, PACK_COLS), lambda i: (i, 0))
    return pl.pallas_call(
        kern, grid=(rows // tile,),
        in_specs=[pl.BlockSpec((N_DEV, tile, PACK_COLS), lambda i: (0, i, 0)), blk, blk, blk],
        out_specs=[blk] * 4, out_shape=[jax.ShapeDtypeStruct(w.shape, F32)] * 4,
        compiler_params=pltpu.CompilerParams(dimension_semantics=("arbitrary",)),
        name=name)(partials, w, m, v)


def kernel(x, p, norm1_g, w_in, sg_ln_g, sg_ln_b, sg_w, sg_b, gla_w_a2, gla_b_a, gla_norm_g, att_rel_bias, conv_dw_w, conv_dw_b, conv_ln_g, conv_ln_b, w_branch, w_gate, b_gate, w_out, norm2_g, w_ff1, w_ff2, norm3_g, w_ple_gate, b_ple_gate, w_ple, final_g, loss_target, m_norm1_g, m_w_in, m_sg_ln_g, m_sg_ln_b, m_sg_w, m_sg_b, m_gla_w_a2, m_gla_b_a, m_gla_norm_g, m_att_rel_bias, m_conv_dw_w, m_conv_dw_b, m_conv_ln_g, m_conv_ln_b, m_w_branch, m_w_gate, m_b_gate, m_w_out, m_norm2_g, m_w_ff1, m_w_ff2, m_norm3_g, m_w_ple_gate, m_b_ple_gate, m_w_ple, m_final_g, v_norm1_g, v_w_in, v_sg_ln_g, v_sg_ln_b, v_sg_w, v_sg_b, v_gla_w_a2, v_gla_b_a, v_gla_norm_g, v_att_rel_bias, v_conv_dw_w, v_conv_dw_b, v_conv_ln_g, v_conv_ln_b, v_w_branch, v_w_gate, v_b_gate, v_w_out, v_norm2_g, v_w_ff1, v_w_ff2, v_norm3_g, v_w_ple_gate, v_b_ple_gate, v_w_ple, v_final_g):
    args = locals()
    wts = {n: args[n] for n in WEIGHTS}
    mom = {n: args['m_' + n] for n in WEIGHTS}
    var = {n: args['v_' + n] for n in WEIGHTS}

    full = {n: wts[n] for n in REPLICATED}
    for names, dtype, call in ((MXU_WEIGHTS, _MXU_DTYPE, "gather_mxu_weights"), (VEC_WEIGHTS, F32, "gather_vec_weights")):
        gathered = _exchange(call, _pack([wts[n] for n in names], dtype), scatter=False)
        for n, slabs in zip(names, _unpack(gathered, [wts[n].shape for n in names], lead=N_DEV)):
            full[n] = _from_slabs(slabs, SHARD_AXIS[n])

    loss, grad_x, grads = _local_step(x[0], p[:, 0], loss_target[0], full)
    loss = lax.psum(loss, ("x", "y", "c"))

    to_owner = _pack([_to_slabs(grads[n], SHARD_AXIS[n]) for n in SHARDED], F32, lead=N_DEV)
    sharded_parts = _exchange("scatter_sharded_grads", to_owner, scatter=True)
    repl_parts = _exchange("gather_replicated_grads", _pack([grads[n] for n in REPLICATED], F32), scatter=False)

    results = {}
    for names, parts, call in ((SHARDED, sharded_parts, "adamw_sharded"), (REPLICATED, repl_parts, "adamw_replicated")):
        packed = [_pack([d[n] for n in names], F32) for d in (wts, mom, var)]
        outs = _adamw(call, parts, *packed)
        for kind, buf in zip(("grad", "delta", "new_m", "new_v"), outs):
            for n, a in zip(names, _unpack(buf, [wts[n].shape for n in names])):
                results[kind, n] = a
    return (loss, grad_x[None]) + tuple(results[kind, n] for kind in ("grad", "delta", "new_m", "new_v")
                                        for n in WEIGHTS)
```

```python
import functools

import numpy as np
import jax
import jax.numpy as jnp
from jax import lax
from jax.experimental import pallas as pl
from jax.experimental.pallas import tpu as pltpu

F32 = jnp.float32
_MXU_DTYPE = jnp.bfloat16
GRAD_DTYPE = jnp.bfloat16

N_DEV = 8
D_MODEL = 1024
DEPTH = 2
CHUNK = 64
PLE_DIM = 256
BW = 512
SG_BLOCK = 128
SG_GROUPS = 4
GLA_HEADS = 4
GLA_DK = 64
GLA_DV = 128
GLA_RANK = 16
GLA_TAU = 16.0
ATT_HEADS = 8
ATT_HD = 64
ATT_BAND = 9
BAND = ATT_BAND * CHUNK
MAX_REL = 256
REL_TABLE = CHUNK + MAX_REL
CONV_K = 31
CONV_HALO = 32
D_FF = 4096
EPS = 1e-6
NEG_INF = -1e30

IN_GROUPS = (("A", 0, 1024), ("B", 1024, 1536), ("a", 2560, 16), ("C", 2576, 1536), ("D", 4112, 1024))
IN_COLS = 5136

ADAM_LR = 0.001
ADAM_B1 = 0.9
ADAM_B2 = 0.999
ADAM_EPS = 1e-08
ADAM_WD = 0.01
ADAM_STEP = 10

ADAMW_TILE = 128
PACK_COLS = 1024
VMEM_LIMIT_MB = 56

_NN = (((1,), (0,)), ((), ()))
_NT = (((1,), (1,)), ((), ()))
_TN = (((0,), (0,)), ((), ()))

WEIGHTS = ['norm1_g', 'w_in', 'sg_ln_g', 'sg_ln_b', 'sg_w', 'sg_b', 'gla_w_a2', 'gla_b_a', 'gla_norm_g',
           'att_rel_bias', 'conv_dw_w', 'conv_dw_b', 'conv_ln_g', 'conv_ln_b', 'w_branch', 'w_gate', 'b_gate',
           'w_out', 'norm2_g', 'w_ff1', 'w_ff2', 'norm3_g', 'w_ple_gate', 'b_ple_gate', 'w_ple', 'final_g']
SHARD_AXIS = {'w_in': 2, 'gla_w_a2': 2, 'att_rel_bias': 2, 'conv_dw_w': 2, 'w_branch': 3, 'w_gate': 2,
              'b_gate': 2, 'w_out': 1, 'w_ff1': 2, 'w_ff2': 1, 'w_ple_gate': 1, 'w_ple': 2}
MXU_WEIGHTS = ('w_in', 'w_branch', 'w_gate', 'w_out', 'w_ff1', 'w_ff2', 'w_ple_gate', 'w_ple')
VEC_WEIGHTS = ('gla_w_a2', 'att_rel_bias', 'conv_dw_w', 'b_gate')
SHARDED = tuple(n for n in WEIGHTS if n in SHARD_AXIS)
REPLICATED = tuple(n for n in WEIGHTS if n not in SHARD_AXIS)


def _mm(a, b, dims=_NN):
    return lax.dot_general(a.astype(_MXU_DTYPE), b.astype(_MXU_DTYPE), dims, preferred_element_type=F32)


def _split3(x):
    x1 = x.astype(jnp.bfloat16)
    r1 = x - x1.astype(F32)
    x2 = r1.astype(jnp.bfloat16)
    x3 = (r1 - x2.astype(F32)).astype(jnp.bfloat16)
    return x1, x2, x3


def _mm_exact_rhs(m, x, dims=_NN):
    return sum(lax.dot_general(m, xi, dims, preferred_element_type=F32) for xi in _split3(x))


def _mm_exact_lhs(x, m, dims=_NN):
    return sum(lax.dot_general(xi, m, dims, preferred_element_type=F32) for xi in _split3(x))


def _sigmoid(x):
    return 1.0 / (1.0 + jnp.exp(-x))


def _gelu(x):
    c = 0.7978845608028654
    t = jnp.tanh(c * (x + 0.044715 * x * x * x))
    return 0.5 * x * (1.0 + t), t


def _gelu_grad(x, t):
    c = 0.7978845608028654
    return 0.5 * (1.0 + t) + 0.5 * x * (1.0 - t * t) * c * (1.0 + 3.0 * 0.044715 * x * x)


def _rms_stat(h):
    return lax.rsqrt(jnp.mean(h * h, axis=-1, keepdims=True) + EPS)


def _rms_bwd(dy, h, g, r):
    hh = h * r
    dhh = dy * g
    dh = r * (dhh - hh * jnp.mean(dhh * hh, axis=-1, keepdims=True))
    return dh, jnp.sum(dy * hh, axis=0, keepdims=True)


def _ln_fwd(x, g, b):
    mu = jnp.mean(x, axis=-1, keepdims=True)
    xc = x - mu
    rs = lax.rsqrt(jnp.mean(xc * xc, axis=-1, keepdims=True) + EPS)
    xh = xc * rs
    return xh * g + b, xh, rs


def _ln_bwd(dy, xh, rs, g):
    dxh = dy * g
    dx = rs * (dxh - jnp.mean(dxh, axis=-1, keepdims=True) - xh * jnp.mean(dxh * xh, axis=-1, keepdims=True))
    return dx, jnp.sum(dy * xh, axis=0, keepdims=True), jnp.sum(dy, axis=0, keepdims=True)


def _row_call(name, body, nt, rows=(), halos=(), res=(), outs=(), accs=(), scratch=(), reverse=False):
    def pos(i):
        return (nt - 1 - i) if reverse else i

    def lead(ndim, f):
        return lambda i: (f(pos(i)),) + (0,) * (ndim - 1)

    in_specs, operands = [], []
    for a, tile in rows:
        in_specs.append(pl.BlockSpec((tile,) + a.shape[1:], lead(a.ndim, lambda t: t)))
        operands.append(a)
    for a, blk, per, side in halos:
        last = a.shape[0] // blk - 1
        if side == 'prev':
            f = lambda t, per=per: jnp.maximum(t * per - 1, 0)
        else:
            f = lambda t, per=per, last=last: jnp.minimum((t + 1) * per, last)
        in_specs.append(pl.BlockSpec((blk,) + a.shape[1:], lead(a.ndim, f)))
        operands.append(a)
    for a in res:
        in_specs.append(pl.BlockSpec(a.shape, lambda i, nd=a.ndim: (0,) * nd, pipeline_mode=pl.Buffered(1)))
        operands.append(a)
    out_specs, out_shape = [], []
    for shape, dtype, tile in outs:
        out_specs.append(pl.BlockSpec((tile,) + tuple(shape[1:]), lead(len(shape), lambda t: t)))
        out_shape.append(jax.ShapeDtypeStruct(tuple(shape), dtype))
    for shape in accs:
        out_specs.append(pl.BlockSpec(tuple(shape), lambda i, nd=len(shape): (0,) * nd))
        out_shape.append(jax.ShapeDtypeStruct(tuple(shape), F32))
    n0, n1, n2, n3, n4 = len(rows), len(halos), len(res), len(outs), len(accs)

    def kern(*refs):
        i = pl.program_id(0)
        row_refs = refs[:n0]
        halo_refs = refs[n0:n0 + n1]
        res_refs = refs[n0 + n1:n0 + n1 + n2]
        out_refs = refs[n0 + n1 + n2:n0 + n1 + n2 + n3]
        acc_refs = refs[n0 + n1 + n2 + n3:n0 + n1 + n2 + n3 + n4]
        scr_refs = refs[n0 + n1 + n2 + n3 + n4:]

        @pl.when(i == 0)
        def _():
            for r in tuple(acc_refs) + tuple(scr_refs):
                r[...] = jnp.zeros(r.shape, r.dtype)

        body(pos(i), row_refs, halo_refs, res_refs, out_refs, acc_refs, scr_refs)

    result = pl.pallas_call(
        kern, grid=(nt,), in_specs=in_specs, out_specs=out_specs, out_shape=out_shape,
        scratch_shapes=[pltpu.VMEM(tuple(s), d) for s, d in scratch],
        compiler_params=pltpu.CompilerParams(dimension_semantics=("arbitrary",),
                                             vmem_limit_bytes=VMEM_LIMIT_MB << 20),
        name=name)(*operands)
    return tuple(result)


def _tn_call(name, a, b, k, n, nblk=1, a_col=False, b_col=True, b_off=0, out='cols', tile=1024):
    nt = a.shape[0] // tile
    if out == 'cols':
        o_shape, o_spec = (k, nblk * n), pl.BlockSpec((k, n), lambda j, t: (0, j))
    elif out == 'rows':
        o_shape, o_spec = (nblk * k, n), pl.BlockSpec((k, n), lambda j, t: (j, 0))
    else:
        o_shape, o_spec = (nblk, k, n), pl.BlockSpec((None, k, n), lambda j, t: (j, 0, 0))

    def kern(a_ref, b_ref, o_ref, acc):
        @pl.when(pl.program_id(1) == 0)
        def _():
            acc[...] = jnp.zeros(acc.shape, acc.dtype)

        acc[...] += lax.dot_general(a_ref[...], b_ref[...], _TN, preferred_element_type=F32)

        @pl.when(pl.program_id(1) == nt - 1)
        def _():
            o_ref[...] = acc[...].astype(o_ref.dtype)

    return pl.pallas_call(
        kern, grid=(nblk, nt),
        in_specs=[pl.BlockSpec((tile, k), (lambda j, t: (t, j)) if a_col else (lambda j, t: (t, 0))),
                  pl.BlockSpec((tile, n), (lambda j, t: (t, j + b_off)) if b_col else (lambda j, t: (t, b_off)))],
        out_specs=o_spec, out_shape=jax.ShapeDtypeStruct(o_shape, GRAD_DTYPE),
        scratch_shapes=[pltpu.VMEM((k, n), F32)],
        compiler_params=pltpu.CompilerParams(dimension_semantics=("arbitrary", "arbitrary"),
                                             vmem_limit_bytes=VMEM_LIMIT_MB << 20),
        name=name)(a, b)


def _inproj_fwd(tag, h, g1, w_groups, tm=256):
    t_len = h.shape[0]

    def body(t, rows, halos, res, outs, accs, scr):
        hv = rows[0][...]
        xn = (hv * _rms_stat(hv) * res[0][...]).astype(_MXU_DTYPE)
        outs[0][...] = xn
        for o, w in zip(outs[1:], res[1:]):
            o[...] = lax.dot_general(xn, w[...], _NT, preferred_element_type=F32)

    outs = [((t_len, D_MODEL), _MXU_DTYPE, tm)] + [((t_len, w.shape[0]), F32, tm) for w in w_groups]
    return _row_call("inproj_fwd" + tag, body, t_len // tm, rows=[(h, tm)], res=[g1] + list(w_groups), outs=outs)


def _sg_mask():
    row = lax.broadcasted_iota(jnp.int32, (SG_BLOCK, SG_BLOCK), 0)
    col = lax.broadcasted_iota(jnp.int32, (SG_BLOCK, SG_BLOCK), 1)
    return jnp.logical_or(row >= CHUNK, col < CHUNK)


def _sg_forward_parts(pa, lg, lb, w_ref, bt):
    tm = pa.shape[0]
    nb = tm // SG_BLOCK
    su, sv = pa[:, :BW], pa[:, BW:]
    u, tu = _gelu(su)
    gv, tv = _gelu(sv)
    vn, xh, rs = _ln_fwd(gv, lg, lb)
    mask = _sg_mask()
    wms, xs, ms = [], [], []
    for g in range(SG_GROUPS):
        wm = jnp.where(mask, w_ref[g], 0.0).astype(_MXU_DTYPE)
        xg = jnp.concatenate([vn[b * SG_BLOCK:(b + 1) * SG_BLOCK, g * 128:(g + 1) * 128] for b in range(nb)], axis=1)
        xg = xg.astype(_MXU_DTYPE)
        ms.append(lax.dot_general(wm, xg, _NN, preferred_element_type=F32) + bt[:, g:g + 1])
        wms.append(wm)
        xs.append(xg)
    mixed = _sg_unfold(ms, nb)
    return su, sv, u, tu, tv, xh, rs, wms, xs, mixed


def _sg_unfold(per_group, nb):
    return jnp.concatenate(
        [jnp.concatenate([per_group[g][:, b * 128:(b + 1) * 128] for g in range(SG_GROUPS)], axis=1)
         for b in range(nb)], axis=0)


def _sg_fwd(tag, proj_a, lg, lb, sg_w, sg_bt, tm=512):
    t_len = proj_a.shape[0]

    def body(t, rows, halos, res, outs, accs, scr):
        parts = _sg_forward_parts(rows[0][...], res[0][...], res[1][...], res[2], res[3][...])
        outs[0][...] = (parts[2] * parts[-1]).astype(_MXU_DTYPE)

    return _row_call("sg_fwd" + tag, body, t_len // tm, rows=[(proj_a, tm)], res=[lg, lb, sg_w, sg_bt],
                     outs=[((t_len, BW), _MXU_DTYPE, tm)])[0]


def _sg_bwd(tag, proj_a, dy, lg, lb, sg_w, sg_bt, tm=512):
    t_len = proj_a.shape[0]
    nb = tm // SG_BLOCK

    def body(t, rows, halos, res, outs, accs, scr):
        lgv = res[0][...]
        su, sv, u, tu, tv, xh, rs, wms, xs, mixed = _sg_forward_parts(rows[0][...], lgv, res[1][...], res[2], res[3][...])
        dyv = rows[1][...]
        dsu = dyv * mixed * _gelu_grad(su, tu)
        dmixed = dyv * u
        mask = _sg_mask()
        dxs, dbs = [], []
        for g in range(SG_GROUPS):
            dm = jnp.concatenate([dmixed[b * SG_BLOCK:(b + 1) * SG_BLOCK, g * 128:(g + 1) * 128] for b in range(nb)],
                                 axis=1)
            dmb = dm.astype(_MXU_DTYPE)
            dw = lax.dot_general(dmb, xs[g], _NT, preferred_element_type=F32)
            accs[0][g] += jnp.where(mask, dw, 0.0)
            dbs.append(jnp.sum(dm, axis=1, keepdims=True))
            dxs.append(lax.dot_general(wms[g], dmb, _TN, preferred_element_type=F32))
        accs[1][...] += jnp.concatenate(dbs, axis=1)
        dvn = _sg_unfold(dxs, nb)
        dgv, dlg, dlb = _ln_bwd(dvn, xh, rs, lgv)
        accs[2][...] += dlg
        accs[3][...] += dlb
        dsv = dgv * _gelu_grad(sv, tv)
        outs[0][...] = jnp.concatenate([dsu, dsv], axis=1).astype(_MXU_DTYPE)

    return _row_call("sg_bwd" + tag, body, t_len // tm, rows=[(proj_a, tm), (dy, tm)], res=[lg, lb, sg_w, sg_bt],
                     outs=[((t_len, 2 * BW), _MXU_DTYPE, tm)],
                     accs=[(SG_GROUPS, SG_BLOCK, SG_BLOCK), (SG_BLOCK, SG_GROUPS), (1, BW), (1, BW)])


def _tri(lower):
    row = lax.broadcasted_iota(jnp.int32, (CHUNK, CHUNK), 0)
    col = lax.broadcasted_iota(jnp.int32, (CHUNK, CHUNK), 1)
    return ((row >= col) if lower else (row <= col)).astype(jnp.bfloat16)


def _gla_gate(pa, wa2, ba):
    z = _mm(pa, wa2) + ba
    log_a = (jnp.minimum(z, 0.0) - jnp.log(1.0 + jnp.exp(-jnp.abs(z)))) * (1.0 / GLA_TAU)
    return z, log_a


def _gla_chunk_fwd(pb, log_a, j):
    sl = slice(j * CHUNK, (j + 1) * CHUNK)
    cum = _mm_exact_rhs(_tri(True), log_a[sl])
    tot = cum[CHUNK - 1:CHUNK]
    w = jnp.exp(tot - cum)
    k = pb[sl, 256:512]
    return sl, cum, tot, w, k * w, jnp.exp(tot)


def _gla_read(q, s_t):
    qs = (q * (GLA_DK ** -0.5)).astype(_MXU_DTYPE)
    sb = s_t.astype(_MXU_DTYPE)
    o = jnp.concatenate([lax.dot_general(qs[:, h * 64:(h + 1) * 64], sb[:, h * 64:(h + 1) * 64], _NT,
                                         preferred_element_type=F32) for h in range(GLA_HEADS)], axis=1)
    return qs, sb, o


def _gla_fwd(tag, proj_b, proj_a, wa2, ba, ng, tm=256):
    t_len = proj_b.shape[0]
    cpt = tm // CHUNK

    def body(t, rows, halos, res, outs, accs, scr):
        pb = rows[0][...]
        _, log_a = _gla_gate(rows[1][...], res[0][...], res[1][...])
        ngv = res[2][...]
        st = scr[0]
        for j in range(cpt):
            sl, cum, tot, w, kd, dec = _gla_chunk_fwd(pb, log_a, j)
            kdb = kd.astype(_MXU_DTYPE)
            vb = pb[sl, 512:1024].astype(_MXU_DTYPE)
            ut = jnp.concatenate([lax.dot_general(vb[:, h * 128:(h + 1) * 128], kdb[:, h * 64:(h + 1) * 64], _TN,
                                                  preferred_element_type=F32) for h in range(GLA_HEADS)], axis=1)
            s_new = dec * st[...] + ut
            st[...] = s_new
            outs[1][j] = s_new
            _, _, o = _gla_read(pb[sl, 0:256], s_new)
            on = jnp.concatenate(
                [o[:, h * 128:(h + 1) * 128] * lax.rsqrt(jnp.mean(jnp.square(o[:, h * 128:(h + 1) * 128]), axis=-1,
                                                                  keepdims=True) + EPS) for h in range(GLA_HEADS)], axis=1)
            r = pb[sl, 1024:1536]
            outs[0][sl, :] = (on * ngv * (r * _sigmoid(r))).astype(_MXU_DTYPE)

    return _row_call("gla_fwd" + tag, body, t_len // tm, rows=[(proj_b, tm), (proj_a, tm)], res=[wa2, ba, ng],
                     outs=[((t_len, BW), _MXU_DTYPE, tm), ((t_len // CHUNK, GLA_DV, 256), F32, cpt)],
                     scratch=[((GLA_DV, 256), F32)])


def _gla_bwd(tag, proj_b, proj_a, dy, states, wa2, ba, ng, tm=256):
    t_len = proj_b.shape[0]
    cpt = tm // CHUNK

    def body(t, rows, halos, res, outs, accs, scr):
        pb = rows[0][...]
        pa = rows[1][...]
        dyv = rows[2][...]
        st_ref = rows[3]
        wa2v = res[0][...]
        z, log_a = _gla_gate(pa, wa2v, res[1][...])
        ngv = res[2][...]
        dst = scr[0]
        s_before_tile = jnp.where(t > 0, halos[0][0], 0.0)
        dz_rows = [None] * cpt
        d_rows = [None] * cpt
        for j in reversed(range(cpt)):
            sl, cum, tot, w, kd, dec = _gla_chunk_fwd(pb, log_a, j)
            s_c = st_ref[j]
            s_prev = st_ref[j - 1] if j > 0 else s_before_tile
            qs, sb, o = _gla_read(pb[sl, 0:256], s_c)
            r = pb[sl, 1024:1536]
            sig = _sigmoid(r)
            sil = r * sig
            dyj = dyv[sl]
            dos, drs, dng = [], [], []
            for h in range(GLA_HEADS):
                hs = slice(h * 128, (h + 1) * 128)
                oh = o[:, hs]
                rstd = lax.rsqrt(jnp.mean(oh * oh, axis=-1, keepdims=True) + EPS)
                on = oh * rstd
                g_h = ngv[:, hs]
                don = dyj[:, hs] * g_h * sil[:, hs]
                dng.append(jnp.sum(dyj[:, hs] * on * sil[:, hs], axis=0, keepdims=True))
                drs.append(dyj[:, hs] * on * g_h * (sig[:, hs] * (1.0 + r[:, hs] * (1.0 - sig[:, hs]))))
                dos.append(rstd * (don - on * jnp.mean(don * on, axis=-1, keepdims=True)))
            accs[2][...] += jnp.concatenate(dng, axis=1)
            dr = jnp.concatenate(drs, axis=1)
            do = jnp.concatenate(dos, axis=1)
            dob = do.astype(_MXU_DTYPE)
            dst_tot = dst[...] + jnp.concatenate(
                [lax.dot_general(dob[:, h * 128:(h + 1) * 128], qs[:, h * 64:(h + 1) * 64], _TN,
                                 preferred_element_type=F32) for h in range(GLA_HEADS)], axis=1)
            dq = jnp.concatenate(
                [lax.dot_general(dob[:, h * 128:(h + 1) * 128], sb[:, h * 64:(h + 1) * 64], _NN,
                                 preferred_element_type=F32) for h in range(GLA_HEADS)], axis=1) * (GLA_DK ** -0.5)
            ddec = jnp.sum(dst_tot * s_prev, axis=0, keepdims=True)
            dst[...] = dec * dst_tot
            dub = dst_tot.astype(_MXU_DTYPE)
            vb = pb[sl, 512:1024].astype(_MXU_DTYPE)
            kdb = kd.astype(_MXU_DTYPE)
            dkd = jnp.concatenate(
                [lax.dot_general(vb[:, h * 128:(h + 1) * 128], dub[:, h * 64:(h + 1) * 64], _NN,
                                 preferred_element_type=F32) for h in range(GLA_HEADS)], axis=1)
            dv = jnp.concatenate(
                [lax.dot_general(kdb[:, h * 64:(h + 1) * 64], dub[:, h * 64:(h + 1) * 64], _NT,
                                 preferred_element_type=F32) for h in range(GLA_HEADS)], axis=1)
            dk = dkd * w
            e = dkd * kd
            dtot = jnp.sum(e, axis=0, keepdims=True) + ddec * dec
            last = lax.broadcasted_iota(jnp.int32, e.shape, 0) == CHUNK - 1
            dcum = jnp.where(last, dtot - e, -e)
            dla = _mm_exact_rhs(_tri(False), dcum)
            dz_rows[j] = dla * (1.0 / GLA_TAU) * _sigmoid(-z[sl])
            d_rows[j] = jnp.concatenate([dq, dk, dv, dr], axis=1)
        dz = jnp.concatenate(dz_rows, axis=0)
        dzb = dz.astype(_MXU_DTYPE)
        outs[0][...] = jnp.concatenate(d_rows, axis=0).astype(_MXU_DTYPE)
        outs[1][...] = lax.dot_general(dzb, wa2v.astype(_MXU_DTYPE), _NT, preferred_element_type=F32).astype(_MXU_DTYPE)
        accs[0][...] += lax.dot_general(pa.astype(_MXU_DTYPE), dzb, _TN, preferred_element_type=F32)
        accs[1][...] += jnp.sum(dz, axis=0, keepdims=True)

    return _row_call("gla_bwd" + tag, body, t_len // tm,
                     rows=[(proj_b, tm), (proj_a, tm), (dy, tm), (states, cpt)],
                     halos=[(states, 1, cpt, 'prev')], res=[wa2, ba, ng],
                     outs=[((t_len, 1536), _MXU_DTYPE, tm), ((t_len, GLA_RANK), _MXU_DTYPE, tm)],
                     accs=[(GLA_RANK, 256), (1, 256), (1, BW)], scratch=[((GLA_DV, 256), F32)], reverse=True)


def _rel_index():
    l_idx = np.arange(CHUNK)[:, None]
    m_idx = np.arange(BAND)[None, :]
    rel = l_idx + (ATT_BAND - 1) * CHUNK - m_idx
    return jnp.asarray((np.clip(rel, -(CHUNK - 1), MAX_REL) + (CHUNK - 1)).reshape(1, CHUNK * BAND), jnp.int32)


BIAS_COLS = 4096


def _bias_expand(tag, rel_bias):
    n = CHUNK * BAND

    def kern(rel_ref, idx_ref, o_ref):
        onehot = (lax.broadcasted_iota(jnp.int32, (REL_TABLE, BIAS_COLS), 0) == idx_ref[...]).astype(jnp.bfloat16)
        o_ref[...] = _mm_exact_lhs(rel_ref[...], onehot)

    return pl.pallas_call(
        kern, grid=(n // BIAS_COLS,),
        in_specs=[pl.BlockSpec((ATT_HEADS, REL_TABLE), lambda i: (0, 0)), pl.BlockSpec((1, BIAS_COLS), lambda i: (0, i))],
        out_specs=pl.BlockSpec((ATT_HEADS, BIAS_COLS), lambda i: (0, i)),
        out_shape=jax.ShapeDtypeStruct((ATT_HEADS, n), F32), name="bias_expand" + tag)(rel_bias, _rel_index())


def _bias_reduce(tag, dbias):
    n = CHUNK * BAND

    def kern(db_ref, idx_ref, o_ref):
        @pl.when(pl.program_id(0) == 0)
        def _():
            o_ref[...] = jnp.zeros(o_ref.shape, o_ref.dtype)

        onehot = (lax.broadcasted_iota(jnp.int32, (REL_TABLE, BIAS_COLS), 0) == idx_ref[...]).astype(jnp.bfloat16)
        o_ref[...] += _mm_exact_lhs(db_ref[...], onehot, _NT)

    return pl.pallas_call(
        kern, grid=(n // BIAS_COLS,),
        in_specs=[pl.BlockSpec((ATT_HEADS, BIAS_COLS), lambda i: (0, i)), pl.BlockSpec((1, BIAS_COLS), lambda i: (0, i))],
        out_specs=pl.BlockSpec((ATT_HEADS, REL_TABLE), lambda i: (0, 0)),
        out_shape=jax.ShapeDtypeStruct((ATT_HEADS, REL_TABLE), F32),
        compiler_params=pltpu.CompilerParams(dimension_semantics=("arbitrary",)),
        name="bias_reduce" + tag)(dbias, _rel_index())


def _attn_scores(q, kb, bias_h, ok, h):
    hs = slice(h * ATT_HD, (h + 1) * ATT_HD)
    s = lax.dot_general(q[:, hs], kb[:, hs], _NT, preferred_element_type=F32) + bias_h
    s = jnp.where(ok, s, NEG_INF)
    e = jnp.exp(s - jnp.max(s, axis=-1, keepdims=True))
    return e / jnp.sum(e, axis=-1, keepdims=True)


def _attn_stage_kv(t, pc_ref, pp_ref, kv):
    tm = pc_ref.shape[0]
    kv[0:tm, :] = jnp.where(t > 0, pp_ref[:, 512:1536], 0.0).astype(kv.dtype)
    kv[tm:2 * tm, :] = pc_ref[:, 512:1536].astype(kv.dtype)


def _attn_key_ok(t, j, tm):
    m_idx = lax.broadcasted_iota(jnp.int32, (CHUNK, BAND), 1)
    return (t * tm + j * CHUNK - (ATT_BAND - 1) * CHUNK + m_idx) >= 0


def _attn_fwd(tag, proj_c, bias, tm=512):
    t_len = proj_c.shape[0]
    off = tm - (ATT_BAND - 1) * CHUNK

    def body(t, rows, halos, res, outs, accs, scr):
        pc_ref, pp_ref, b_ref, kv = rows[0], halos[0], res[0], scr[0]
        _attn_stage_kv(t, pc_ref, pp_ref, kv)

        def chunk(j, carry):
            r0 = pl.multiple_of(j * CHUNK, CHUNK)
            q = (pc_ref[pl.ds(r0, CHUNK), 0:512] * (ATT_HD ** -0.5)).astype(_MXU_DTYPE)
            band = pl.ds(pl.multiple_of(off + j * CHUNK, CHUNK), BAND)
            kb = kv[band, 0:512]
            vb = kv[band, 512:1024]
            ok = _attn_key_ok(t, j, tm)
            o = [lax.dot_general(_attn_scores(q, kb, b_ref[h], ok, h).astype(_MXU_DTYPE),
                                 vb[:, h * ATT_HD:(h + 1) * ATT_HD], _NN, preferred_element_type=F32)
                 for h in range(ATT_HEADS)]
            outs[0][pl.ds(r0, CHUNK), :] = jnp.concatenate(o, axis=1).astype(_MXU_DTYPE)
            return carry

        lax.fori_loop(0, tm // CHUNK, chunk, 0)

    return _row_call("attn_fwd" + tag, body, t_len // tm, rows=[(proj_c, tm)], halos=[(proj_c, tm, 1, 'prev')],
                     res=[bias], outs=[((t_len, BW), _MXU_DTYPE, tm)], scratch=[((2 * tm, 1024), _MXU_DTYPE)])[0]


def _attn_bwd(tag, proj_c, dy, bias, tm=512):
    t_len = proj_c.shape[0]
    off = tm - (ATT_BAND - 1) * CHUNK
    scale = ATT_HD ** -0.5

    def body(t, rows, halos, res, outs, accs, scr):
        pc_ref, dy_ref, pp_ref, b_ref, kv, dkv = rows[0], rows[1], halos[0], res[0], scr[0], scr[1]
        _attn_stage_kv(t, pc_ref, pp_ref, kv)
        dkv[...] = jnp.zeros(dkv.shape, dkv.dtype)

        def chunk(j, carry):
            r0 = pl.multiple_of(j * CHUNK, CHUNK)
            q = (pc_ref[pl.ds(r0, CHUNK), 0:512] * scale).astype(_MXU_DTYPE)
            do = dy_ref[pl.ds(r0, CHUNK), :].astype(_MXU_DTYPE)
            band = pl.ds(pl.multiple_of(off + j * CHUNK, CHUNK), BAND)
            kb = kv[band, 0:512]
            vb = kv[band, 512:1024]
            ok = _attn_key_ok(t, j, tm)
            dqs, dks, dvs = [], [], []
            for h in range(ATT_HEADS):
                hs = slice(h * ATT_HD, (h + 1) * ATT_HD)
                p = _attn_scores(q, kb, b_ref[h], ok, h)
                dp = lax.dot_general(do[:, hs], vb[:, hs], _NT, preferred_element_type=F32)
                ds = p * (dp - jnp.sum(dp * p, axis=-1, keepdims=True))
                accs[0][h] += ds
                dsb = ds.astype(_MXU_DTYPE)
                dqs.append(lax.dot_general(dsb, kb[:, hs], _NN, preferred_element_type=F32) * scale)
                dks.append(lax.dot_general(dsb, q[:, hs], _TN, preferred_element_type=F32))
                dvs.append(lax.dot_general(p.astype(_MXU_DTYPE), do[:, hs], _TN, preferred_element_type=F32))
            outs[0][pl.ds(r0, CHUNK), :] = jnp.concatenate(dqs, axis=1).astype(_MXU_DTYPE)
            dkv[band, :] += jnp.concatenate(dks + dvs, axis=1)
            return carry

        lax.fori_loop(0, tm // CHUNK, chunk, 0)
        outs[1][...] = dkv[tm:2 * tm, :]
        outs[2][...] = dkv[0:tm, :]

    return _row_call("attn_bwd" + tag, body, t_len // tm, rows=[(proj_c, tm), (dy, tm)],
                     halos=[(proj_c, tm, 1, 'prev')], res=[bias],
                     outs=[((t_len, BW), _MXU_DTYPE, tm), ((t_len, 1024), F32, tm), ((t_len, 1024), F32, tm)],
                     accs=[(ATT_HEADS, CHUNK, BAND)],
                     scratch=[((2 * tm, 1024), _MXU_DTYPE), ((2 * tm, 1024), F32)])


def _attn_combine(tag, dq, dkv_own, dkv_prev, tm=512):
    t_len = dq.shape[0]
    nt = t_len // tm

    def body(t, rows, halos, res, outs, accs, scr):
        dkv = rows[1][...] + jnp.where(t < nt - 1, halos[0][...], 0.0)
        outs[0][...] = jnp.concatenate([rows[0][...], dkv.astype(_MXU_DTYPE)], axis=1)

    return _row_call("attn_combine" + tag, body, nt, rows=[(dq, tm), (dkv_own, tm)],
                     halos=[(dkv_prev, tm, 1, 'next')], outs=[((t_len, 1536), _MXU_DTYPE, tm)])[0]


def _conv_glu(pd):
    a, g = pd[:, :BW], pd[:, BW:]
    sig = _sigmoid(g)
    return a, sig, a * sig


def _conv_stage(t, pd_ref, ph_ref, win):
    pd = pd_ref[...]
    a, sig, y0 = _conv_glu(pd)
    win[0:CONV_HALO, :] = jnp.where(t > 0, _conv_glu(ph_ref[...])[2], 0.0)
    win[CONV_HALO:, :] = y0
    return a, sig


def _conv_fwd(tag, proj_d, dw_w, dw_b, ln_g, ln_b, tm=512):
    t_len = proj_d.shape[0]
    lead = CONV_HALO - (CONV_K - 1)

    def body(t, rows, halos, res, outs, accs, scr):
        win = scr[0]
        _conv_stage(t, rows[0], halos[0], win)
        w_ref = res[0]
        yc = res[1][...] + w_ref[0:1, :] * win[pl.ds(lead, tm), :]
        for j in range(1, CONV_K):
            yc = yc + w_ref[j:j + 1, :] * win[pl.ds(lead + j, tm), :]
        outs[1][...] = yc
        yl, _, _ = _ln_fwd(yc, res[2][...], res[3][...])
        outs[0][...] = (yl * _sigmoid(yl)).astype(_MXU_DTYPE)

    return _row_call("conv_fwd" + tag, body, t_len // tm, rows=[(proj_d, tm)],
                     halos=[(proj_d, CONV_HALO, tm // CONV_HALO, 'prev')], res=[dw_w, dw_b, ln_g, ln_b],
                     outs=[((t_len, BW), _MXU_DTYPE, tm), ((t_len, BW), F32, tm)],
                     scratch=[((tm + CONV_HALO, BW), F32)])


def _conv_bwd_norm(tag, yc, dy, ln_g, ln_b, tm=512):
    t_len = yc.shape[0]

    def body(t, rows, halos, res, outs, accs, scr):
        lgv = res[0][...]
        yl, xh, rs = _ln_fwd(rows[0][...], lgv, res[1][...])
        sig = _sigmoid(yl)
        dyl = rows[1][...] * (sig * (1.0 + yl * (1.0 - sig)))
        dyc, dlg, dlb = _ln_bwd(dyl, xh, rs, lgv)
        outs[0][...] = dyc
        accs[0][...] += dlg
        accs[1][...] += dlb
        accs[2][...] += jnp.sum(dyc, axis=0, keepdims=True)

    return _row_call("conv_bwd_norm" + tag, body, t_len // tm, rows=[(yc, tm), (dy, tm)], res=[ln_g, ln_b],
                     outs=[((t_len, BW), F32, tm)], accs=[(1, BW), (1, BW), (1, BW)])


def _conv_bwd_taps(tag, proj_d, dyc, dw_w, tm=512):
    t_len = proj_d.shape[0]
    nt = t_len // tm
    lead = CONV_HALO - (CONV_K - 1)

    def body(t, rows, halos, res, outs, accs, scr):
        win, wd = scr[0], scr[1]
        a, sig = _conv_stage(t, rows[0], halos[0], win)
        dycv = rows[1][...]
        wd[0:tm, :] = dycv
        wd[tm:, :] = jnp.where(t < nt - 1, halos[1][...], 0.0)
        w_ref = res[0]
        dy0 = jnp.zeros((tm, BW), F32)
        for j in range(CONV_K):
            dy0 = dy0 + w_ref[j:j + 1, :] * wd[pl.ds(CONV_K - 1 - j, tm), :]
            accs[0][j:j + 1, :] += jnp.sum(dycv * win[pl.ds(lead + j, tm), :], axis=0, keepdims=True)
        outs[0][...] = jnp.concatenate([dy0 * sig, dy0 * a * sig * (1.0 - sig)], axis=1).astype(_MXU_DTYPE)

    return _row_call("conv_bwd_taps" + tag, body, nt, rows=[(proj_d, tm), (dyc, tm)],
                     halos=[(proj_d, CONV_HALO, tm // CONV_HALO, 'prev'), (dyc, CONV_HALO, tm // CONV_HALO, 'next')],
                     res=[dw_w], outs=[((t_len, 2 * BW), _MXU_DTYPE, tm)], accs=[(CONV_K, BW)],
                     scratch=[((tm + CONV_HALO, BW), F32), ((tm + CONV_HALO, BW), F32)])


def _merge_fwd(tag, h, xn, ys, w_gate, b_gate, w_branch, w_out, tm=256):
    t_len = h.shape[0]

    def body(t, rows, halos, res, outs, accs, scr):
        xnv = rows[1][...]
        wg_ref, bg_ref, wb_ref, wo_ref = res
        merged = jnp.zeros((tm, D_MODEL), F32)
        for n in range(4):
            cs = slice(n * D_MODEL, (n + 1) * D_MODEL)
            gate = _sigmoid(lax.dot_general(xnv, wg_ref[n], _NN, preferred_element_type=F32) + bg_ref[n:n + 1, :])
            bo = lax.dot_general(rows[2 + n][...], wb_ref[n], _NN, preferred_element_type=F32)
            outs[0][:, cs] = gate
            outs[1][:, cs] = bo.astype(_MXU_DTYPE)
            merged = merged + gate * bo
        mb = merged.astype(_MXU_DTYPE)
        outs[2][...] = mb
        outs[3][...] = rows[0][...] + lax.dot_general(mb, wo_ref[...], _NN, preferred_element_type=F32)

    return _row_call("merge_fwd" + tag, body, t_len // tm, rows=[(h, tm), (xn, tm)] + [(y, tm) for y in ys],
                     res=[w_gate, b_gate, w_branch, w_out],
                     outs=[((t_len, 4 * D_MODEL), F32, tm), ((t_len, 4 * D_MODEL), _MXU_DTYPE, tm),
                           ((t_len, D_MODEL), _MXU_DTYPE, tm), ((t_len, D_MODEL), F32, tm)])


def _merge_bwd(tag, dh, gate, bo, w_gate, w_branch, w_out, tm=256):
    t_len = dh.shape[0]

    def body(t, rows, halos, res, outs, accs, scr):
        wg_ref, wb_ref, wo_ref = res
        dhb = rows[0][...].astype(_MXU_DTYPE)
        outs[0][...] = dhb
        dmerged = lax.dot_general(dhb, wo_ref[...], _NT, preferred_element_type=F32)
        dxn = jnp.zeros((tm, D_MODEL), F32)
        dbg = []
        for n in range(4):
            cs = slice(n * D_MODEL, (n + 1) * D_MODEL)
            g = rows[1][:, cs]
            dbo = (dmerged * g).astype(_MXU_DTYPE)
            dgp = dmerged * rows[2][:, cs].astype(F32) * (g * (1.0 - g))
            dgb = dgp.astype(_MXU_DTYPE)
            outs[1][:, cs] = dbo
            outs[2][:, cs] = dgb
            outs[4 + n][...] = lax.dot_general(dbo, wb_ref[n], _NT, preferred_element_type=F32)
            dxn = dxn + lax.dot_general(dgb, wg_ref[n], _NT, preferred_element_type=F32)
            dbg.append(jnp.sum(dgp, axis=0, keepdims=True))
        outs[3][...] = dxn
        accs[0][...] += jnp.concatenate(dbg, axis=1)

    return _row_call("merge_bwd" + tag, body, t_len // tm, rows=[(dh, tm), (gate, tm), (bo, tm)],
                     res=[w_gate, w_branch, w_out],
                     outs=[((t_len, D_MODEL), _MXU_DTYPE, tm), ((t_len, 4 * D_MODEL), _MXU_DTYPE, tm),
                           ((t_len, 4 * D_MODEL), _MXU_DTYPE, tm), ((t_len, D_MODEL), F32, tm)]
                     + [((t_len, BW), F32, tm)] * 4,
                     accs=[(1, 4 * D_MODEL)])


FF_COLS = 1024


def _ffn_fwd(tag, h, g2, w1, w2, tm=256):
    t_len = h.shape[0]

    def body(t, rows, halos, res, outs, accs, scr):
        hv = rows[0][...]
        hn = (hv * _rms_stat(hv) * res[0][...]).astype(_MXU_DTYPE)
        outs[0][...] = hn
        acc = hv
        for c in range(D_FF // FF_COLS):
            cs = slice(c * FF_COLS, (c + 1) * FF_COLS)
            pre = lax.dot_general(hn, res[1][:, cs], _NN, preferred_element_type=F32)
            outs[1][:, cs] = pre
            ff = jnp.square(jnp.maximum(pre, 0.0)).astype(_MXU_DTYPE)
            acc = acc + lax.dot_general(ff, res[2][cs, :], _NN, preferred_element_type=F32)
        outs[2][...] = acc

    return _row_call("ffn_fwd" + tag, body, t_len // tm, rows=[(h, tm)], res=[g2, w1, w2],
                     outs=[((t_len, D_MODEL), _MXU_DTYPE, tm), ((t_len, D_FF), F32, tm), ((t_len, D_MODEL), F32, tm)])


def _ffn_bwd(tag, dh, h, pre, g2, w1, w2, tm=256):
    t_len = dh.shape[0]

    def body(t, rows, halos, res, outs, accs, scr):
        dhv = rows[0][...]
        hv = rows[1][...]
        dhb = dhv.astype(_MXU_DTYPE)
        outs[0][...] = dhb
        dhn = jnp.zeros((tm, D_MODEL), F32)
        for c in range(D_FF // FF_COLS):
            cs = slice(c * FF_COLS, (c + 1) * FF_COLS)
            r = jnp.maximum(rows[2][:, cs], 0.0)
            outs[1][:, cs] = (r * r).astype(_MXU_DTYPE)
            dpre = (lax.dot_general(dhb, res[2][cs, :], _NT, preferred_element_type=F32) * (2.0 * r)).astype(_MXU_DTYPE)
            outs[2][:, cs] = dpre
            dhn = dhn + lax.dot_general(dpre, res[1][:, cs], _NT, preferred_element_type=F32)
        dres, dg = _rms_bwd(dhn, hv, res[0][...], _rms_stat(hv))
        outs[3][...] = dhv + dres
        accs[0][...] += dg

    return _row_call("ffn_bwd" + tag, body, t_len // tm, rows=[(dh, tm), (h, tm), (pre, tm)], res=[g2, w1, w2],
                     outs=[((t_len, D_MODEL), _MXU_DTYPE, tm), ((t_len, D_FF), _MXU_DTYPE, tm),
                           ((t_len, D_FF), _MXU_DTYPE, tm), ((t_len, D_MODEL), F32, tm)],
                     accs=[(1, D_MODEL)])


def _ple_fwd(tag, h, p, g3, w_pg, b_pg, w_ple, tm=256):
    t_len = h.shape[0]

    def body(t, rows, halos, res, outs, accs, scr):
        hv = rows[0][...]
        hg = (hv * _rms_stat(hv) * res[0][...]).astype(_MXU_DTYPE)
        pb = rows[1][...].astype(_MXU_DTYPE)
        pg = _sigmoid(lax.dot_general(hg, res[1][...], _NN, preferred_element_type=F32) + res[2][...])
        pe = lax.dot_general(pb, res[3][...], _NN, preferred_element_type=F32)
        outs[0][...] = hg
        outs[1][...] = pb
        outs[2][...] = pg
        outs[3][...] = pe
        outs[4][...] = hv + pg * pe

    return _row_call("ple_fwd" + tag, body, t_len // tm, rows=[(h, tm), (p, tm)], res=[g3, w_pg, b_pg, w_ple],
                     outs=[((t_len, D_MODEL), _MXU_DTYPE, tm), ((t_len, PLE_DIM), _MXU_DTYPE, tm),
                           ((t_len, D_MODEL), F32, tm), ((t_len, D_MODEL), F32, tm), ((t_len, D_MODEL), F32, tm)])


def _ple_bwd(tag, dh, h, pg, pe, g3, w_pg, tm=256):
    t_len = dh.shape[0]

    def body(t, rows, halos, res, outs, accs, scr):
        dhv = rows[0][...]
        hv = rows[1][...]
        pgv = rows[2][...]
        dgp = dhv * rows[3][...] * (pgv * (1.0 - pgv))
        dgb = dgp.astype(_MXU_DTYPE)
        outs[0][...] = dgb
        outs[1][...] = (dhv * pgv).astype(_MXU_DTYPE)
        dhg = lax.dot_general(dgb, res[1][...], _NT, preferred_element_type=F32)
        dres, dg = _rms_bwd(dhg, hv, res[0][...], _rms_stat(hv))
        outs[2][...] = dhv + dres
        accs[0][...] += jnp.sum(dgp, axis=0, keepdims=True)
        accs[1][...] += dg

    return _row_call("ple_bwd" + tag, body, t_len // tm, rows=[(dh, tm), (h, tm), (pg, tm), (pe, tm)],
                     res=[g3, w_pg],
                     outs=[((t_len, D_MODEL), _MXU_DTYPE, tm), ((t_len, D_MODEL), _MXU_DTYPE, tm),
                           ((t_len, D_MODEL), F32, tm)],
                     accs=[(1, D_MODEL), (1, D_MODEL)])


def _inproj_bwd(tag, dh, h, dxn_gate, dprojs, g1, w_groups, tm=256):
    t_len = dh.shape[0]

    def body(t, rows, halos, res, outs, accs, scr):
        hv = rows[1][...]
        dxn = rows[2][...]
        for dp, w in zip(rows[3:], res[1:]):
            dxn = dxn + lax.dot_general(dp[...], w[...], _NN, preferred_element_type=F32)
        dres, dg = _rms_bwd(dxn, hv, res[0][...], _rms_stat(hv))
        outs[0][...] = rows[0][...] + dres
        accs[0][...] += dg

    return _row_call("inproj_bwd" + tag, body, t_len // tm,
                     rows=[(dh, tm), (h, tm), (dxn_gate, tm)] + [(d, tm) for d in dprojs],
                     res=[g1] + list(w_groups), outs=[((t_len, D_MODEL), F32, tm)], accs=[(1, D_MODEL)])


def _loss_head(h, target, gf, tm=512):
    t_len = h.shape[0]

    def body(t, rows, halos, res, outs, accs, scr):
        hv = rows[0][...]
        g = res[0][...]
        r = _rms_stat(hv)
        diff = hv * r * g - rows[1][...]
        accs[0][...] += 0.5 * jnp.sum(jnp.mean(diff * diff, axis=-1, keepdims=True), axis=0, keepdims=True)
        dh, dg = _rms_bwd(diff * (1.0 / D_MODEL), hv, g, r)
        outs[0][...] = dh
        accs[1][...] += dg

    return _row_call("loss_head", body, t_len // tm, rows=[(h, tm), (target, tm)], res=[gf],
                     outs=[((t_len, D_MODEL), F32, tm)], accs=[(1, 128), (1, D_MODEL)])


def _row(v):
    return v.reshape(1, -1)


def _layer_fwd(i, h, p_i, w):
    tag = "_l%d" % i
    win = [w['w_in'][i][s:s + n] for _, s, n in IN_GROUPS]
    xn, pa, pb, pr, pc, pd = _inproj_fwd(tag, h, _row(w['norm1_g'][i]), win)
    sg_bt = w['sg_b'][i].T
    y_a = _sg_fwd(tag, pa, _row(w['sg_ln_g'][i]), _row(w['sg_ln_b'][i]), w['sg_w'][i], sg_bt)
    y_b, states = _gla_fwd(tag, pb, pr, w['gla_w_a2'][i], _row(w['gla_b_a'][i]), _row(w['gla_norm_g'][i]))
    bias = _bias_expand(tag, w['att_rel_bias'][i]).reshape(ATT_HEADS, CHUNK, BAND)
    y_c = _attn_fwd(tag, pc, bias)
    y_d, yc = _conv_fwd(tag, pd, w['conv_dw_w'][i], _row(w['conv_dw_b'][i]), _row(w['conv_ln_g'][i]),
                        _row(w['conv_ln_b'][i]))
    ys = (y_a, y_b, y_c, y_d)
    gate, bo, merged, h1 = _merge_fwd(tag, h, xn, ys, w['w_gate'][i], w['b_gate'][i], w['w_branch'][i], w['w_out'][i])
    hn, pre, h2 = _ffn_fwd(tag, h1, _row(w['norm2_g'][i]), w['w_ff1'][i], w['w_ff2'][i])
    hg, p_b, pg, pe, h3 = _ple_fwd(tag, h2, p_i, _row(w['norm3_g'][i]), w['w_ple_gate'][i], _row(w['b_ple_gate'][i]),
                                   w['w_ple'][i])
    saved = dict(h=h, xn=xn, pa=pa, pb=pb, pr=pr, pc=pc, pd=pd, states=states, bias=bias, yc=yc, ys=ys, gate=gate,
                 bo=bo, merged=merged, h1=h1, hn=hn, pre=pre, h2=h2, hg=hg, p_b=p_b, pg=pg, pe=pe, win=win,
                 sg_bt=sg_bt)
    return h3, saved


def _layer_bwd(i, dh3, s, w):
    tag = "_l%d" % i
    g = {}
    dgp, dpe, dh2, db_pg, dg3 = _ple_bwd(tag, dh3, s['h2'], s['pg'], s['pe'], _row(w['norm3_g'][i]), w['w_ple_gate'][i])
    g['b_ple_gate'], g['norm3_g'] = db_pg[0], dg3[0]
    g['w_ple_gate'] = _tn_call("dw_ple_gate" + tag, s['hg'], dgp, D_MODEL, D_MODEL)
    g['w_ple'] = _tn_call("dw_ple" + tag, s['p_b'], dpe, PLE_DIM, D_MODEL)

    dh2b, ffb, dpre, dh1, dg2 = _ffn_bwd(tag, dh2, s['h1'], s['pre'], _row(w['norm2_g'][i]), w['w_ff1'][i], w['w_ff2'][i])
    g['norm2_g'] = dg2[0]
    g['w_ff1'] = _tn_call("dw_ff1" + tag, s['hn'], dpre, D_MODEL, FF_COLS, nblk=D_FF // FF_COLS)
    g['w_ff2'] = _tn_call("dw_ff2" + tag, ffb, dh2b, FF_COLS, D_MODEL, nblk=D_FF // FF_COLS, a_col=True, b_col=False,
                          out='rows')

    dh1b, dbo, dgpre, dxn_gate, dy_a, dy_b, dy_c, dy_d, db_gate = _merge_bwd(
        tag, dh1, s['gate'], s['bo'], w['w_gate'][i], w['w_branch'][i], w['w_out'][i])
    g['b_gate'] = db_gate.reshape(4, D_MODEL)
    g['w_out'] = _tn_call("dw_out" + tag, s['merged'], dh1b, D_MODEL, D_MODEL)
    g['w_gate'] = _tn_call("dw_gate" + tag, s['xn'], dgpre, D_MODEL, D_MODEL, nblk=4, out='stack')
    g['w_branch'] = jnp.stack([_tn_call("dw_branch%d%s" % (n, tag), s['ys'][n], dbo, BW, D_MODEL, b_off=n)
                             for n in range(4)])

    lg, lb = _row(w['sg_ln_g'][i]), _row(w['sg_ln_b'][i])
    dpa, dsg_w, dsg_bt, dlg, dlb = _sg_bwd(tag, s['pa'], dy_a, lg, lb, w['sg_w'][i], s['sg_bt'])
    g['sg_w'], g['sg_b'], g['sg_ln_g'], g['sg_ln_b'] = dsg_w, dsg_bt.T, dlg[0], dlb[0]

    dpb, dpr, dwa2, dba, dng = _gla_bwd(tag, s['pb'], s['pr'], dy_b, s['states'], w['gla_w_a2'][i],
                                        _row(w['gla_b_a'][i]), _row(w['gla_norm_g'][i]))
    g['gla_w_a2'], g['gla_b_a'], g['gla_norm_g'] = dwa2, dba[0], dng[0]

    dq, dkv_own, dkv_prev, dbias = _attn_bwd(tag, s['pc'], dy_c, s['bias'])
    dpc = _attn_combine(tag, dq, dkv_own, dkv_prev)
    g['att_rel_bias'] = _bias_reduce(tag, dbias.reshape(ATT_HEADS, CHUNK * BAND))

    cg, cb = _row(w['conv_ln_g'][i]), _row(w['conv_ln_b'][i])
    dyc, dcg, dcb, ddwb = _conv_bwd_norm(tag, s['yc'], dy_d, cg, cb)
    dpd, ddw = _conv_bwd_taps(tag, s['pd'], dyc, w['conv_dw_w'][i])
    g['conv_ln_g'], g['conv_ln_b'], g['conv_dw_b'], g['conv_dw_w'] = dcg[0], dcb[0], ddwb[0], ddw

    dprojs = (dpa, dpb, dpr, dpc, dpd)
    dh0, dg1 = _inproj_bwd(tag, dh1, s['h'], dxn_gate, dprojs, _row(w['norm1_g'][i]), s['win'])
    g['norm1_g'] = dg1[0]
    g['w_in'] = jnp.concatenate([_tn_call("dw_in%s%s" % (name, tag), dp, s['xn'], n, D_MODEL)
                                 for (name, _, n), dp in zip(IN_GROUPS, dprojs)], axis=0)
    return dh0, g


def _local_step(x, p, target, w):
    h = x
    saved = []
    for i in range(DEPTH):
        h, s = _layer_fwd(i, h, p[i], w)
        saved.append(s)
    dh, loss, dgf = _loss_head(h, target, _row(w['final_g']))
    per_layer = [None] * DEPTH
    for i in reversed(range(DEPTH)):
        dh, per_layer[i] = _layer_bwd(i, dh, saved[i], w)
    grads = {n: jnp.stack([per_layer[i][n] for i in range(DEPTH)]) for n in WEIGHTS if n != 'final_g'}
    grads['final_g'] = dgf[0]
    return loss[0, 0], dh, grads


def _peers():
    x, y, c = lax.axis_index("x"), lax.axis_index("y"), lax.axis_index("c")
    me = 4 * x + 2 * y + c
    out = []
    for k in range(1, N_DEV):
        px = (1 - x) if k & 4 else x
        py = (1 - y) if k & 2 else y
        pc = (1 - c) if k & 1 else c
        out.append((k - 1, (px, py, pc), 4 * px + 2 * py + pc))
    return me, out


def _block(ref, axis, idx, width):
    ix = [slice(None)] * len(ref.shape)
    ix[axis] = pl.ds(pl.multiple_of(idx * width, width), width)
    return ref.at[tuple(ix)]


def _slot(ref, idx):
    return ref.at[idx]


def _whole(ref, idx):
    return ref


def _gather_item(src, out_shape=None, axis=None):
    if axis is None:
        return dict(src=src, out=(N_DEV,) + src.shape, take=_whole, put=_slot)
    return dict(src=src, out=tuple(out_shape), take=_whole,
                put=lambda ref, s: _block(ref, axis, s, src.shape[axis]))


def _scatter_item(src, axis=None, lead=0):
    if axis is None:
        shape = src.shape[:lead] + src.shape[lead + 1:]
        take = lambda ref, s: ref.at[(slice(None),) * lead + (s,)]
    else:
        width = src.shape[axis] // N_DEV
        shape = src.shape[:axis] + (width,) + src.shape[axis + 1:]
        take = lambda ref, s: _block(ref, axis, s, width)
    return dict(src=src, out=(N_DEV,) + shape, take=take, put=_slot)


def _exchange(name, items):
    n = len(items)

    def body(*refs):
        src_refs, out_refs = refs[:n], refs[n:2 * n]
        send_sems, recv_sems, local_sems = refs[2 * n:]
        me, peers = _peers()
        started = []
        for i, it in enumerate(items):
            cp = pltpu.make_async_copy(it['take'](src_refs[i], me), it['put'](out_refs[i], me), local_sems.at[i])
            cp.start()
            started.append(cp)

        def remote(i, k, pos, receiver, sender):
            it = items[i]
            return pltpu.make_async_remote_copy(
                src_ref=it['take'](src_refs[i], receiver), dst_ref=it['put'](out_refs[i], sender),
                send_sem=send_sems.at[i * (N_DEV - 1) + k], recv_sem=recv_sems.at[i * (N_DEV - 1) + k],
                device_id=pos, device_id_type=pl.DeviceIdType.MESH)

        for k, pos, flat in peers:
            for i in range(n):
                cp = remote(i, k, pos, flat, me)
                cp.start()
                started.append(cp)
        for k, pos, flat in peers:
            for i in range(n):
                remote(i, k, pos, flat, flat).wait_recv()
        for cp in started[n:]:
            cp.wait_send()
        for cp in started[:n]:
            cp.wait()

    any_spec = pl.BlockSpec(memory_space=pl.ANY)
    return pl.pallas_call(
        body, out_shape=[jax.ShapeDtypeStruct(it['out'], it['src'].dtype) for it in items],
        in_specs=[any_spec] * n, out_specs=[any_spec] * n,
        scratch_shapes=[pltpu.SemaphoreType.DMA((n * (N_DEV - 1),)), pltpu.SemaphoreType.DMA((n * (N_DEV - 1),)),
                        pltpu.SemaphoreType.DMA((n,))],
        name=name)(*[it['src'] for it in items])


def _pack(arrays, dtype, lead=None):
    flat = [a.astype(dtype).reshape((lead, -1) if lead else (-1,)) for a in arrays]
    cat = jnp.concatenate(flat, axis=-1)
    n = cat.shape[-1]
    unit = ADAMW_TILE if n >= ADAMW_TILE * PACK_COLS else 16
    rows = -(-n // (PACK_COLS * unit)) * unit
    pad = rows * PACK_COLS - n
    if pad:
        cat = jnp.pad(cat, ((0, 0), (0, pad)) if lead else ((0, pad),))
    return cat.reshape((lead, rows, PACK_COLS) if lead else (rows, PACK_COLS))


def _unpack(buf, shapes, lead=None):
    flat = buf.reshape((lead, -1) if lead else (-1,))
    out, off = [], 0
    for shp in shapes:
        n = int(np.prod(shp))
        piece = flat[..., off:off + n]
        out.append(piece.reshape(((lead,) if lead else ()) + tuple(shp)))
        off += n
    return out


def _to_slabs(full, axis):
    shp = full.shape
    split = full.reshape(shp[:axis] + (N_DEV, shp[axis] // N_DEV) + shp[axis + 1:])
    return jnp.moveaxis(split, axis, 0)


def _from_slabs(slabs, axis):
    moved = jnp.moveaxis(slabs, 0, axis)
    shp = moved.shape
    return moved.reshape(shp[:axis] + (shp[axis] * shp[axis + 1],) + shp[axis + 2:])


def _adamw_block(r, c):
    if r % 8:
        return r, 256
    br = min(r, max(8, ADAMW_TILE * PACK_COLS // c))
    while r % br:
        br //= 2
    return br, c


def _adamw(name, partials, w, m, v):
    n_lead, r, c = w.shape
    br, bc = _adamw_block(r, c)
    c1 = 1.0 - ADAM_B1 ** ADAM_STEP
    c2 = 1.0 - ADAM_B2 ** ADAM_STEP

    def kern(p_ref, w_ref, m_ref, v_ref, g_ref, d_ref, nm_ref, nv_ref):
        g = p_ref[0].astype(F32)
        for s in range(1, N_DEV):
            g = g + p_ref[s].astype(F32)
        nm = ADAM_B1 * m_ref[...] + (1.0 - ADAM_B1) * g
        nv = ADAM_B2 * v_ref[...] + (1.0 - ADAM_B2) * jnp.square(g)
        g_ref[...] = g
        nm_ref[...] = nm
        nv_ref[...] = nv
        d_ref[...] = -ADAM_LR * ((nm / c1) / (jnp.sqrt(nv / c2) + ADAM_EPS) + ADAM_WD * w_ref[...])

    blk = pl.BlockSpec((None, br, bc), lambda l, i, j: (l, i, j))
    return pl.pallas_call(
        kern, grid=(n_lead, r // br, c // bc),
        in_specs=[pl.BlockSpec((N_DEV, None, br, bc), lambda l, i, j: (0, l, i, j)), blk, blk, blk],
        out_specs=[blk] * 4, out_shape=[jax.ShapeDtypeStruct(w.shape, F32)] * 4,
        compiler_params=pltpu.CompilerParams(dimension_semantics=("arbitrary",) * 3),
        name=name)(partials, w, m, v)


def _as_rows(a, lead):
    return a.reshape(a.shape[:lead] + (-1, a.shape[-1]))


def kernel(x, p, norm1_g, w_in, sg_ln_g, sg_ln_b, sg_w, sg_b, gla_w_a2, gla_b_a, gla_norm_g, att_rel_bias, conv_dw_w, conv_dw_b, conv_ln_g, conv_ln_b, w_branch, w_gate, b_gate, w_out, norm2_g, w_ff1, w_ff2, norm3_g, w_ple_gate, b_ple_gate, w_ple, final_g, loss_target, m_norm1_g, m_w_in, m_sg_ln_g, m_sg_ln_b, m_sg_w, m_sg_b, m_gla_w_a2, m_gla_b_a, m_gla_norm_g, m_att_rel_bias, m_conv_dw_w, m_conv_dw_b, m_conv_ln_g, m_conv_ln_b, m_w_branch, m_w_gate, m_b_gate, m_w_out, m_norm2_g, m_w_ff1, m_w_ff2, m_norm3_g, m_w_ple_gate, m_b_ple_gate, m_w_ple, m_final_g, v_norm1_g, v_w_in, v_sg_ln_g, v_sg_ln_b, v_sg_w, v_sg_b, v_gla_w_a2, v_gla_b_a, v_gla_norm_g, v_att_rel_bias, v_conv_dw_w, v_conv_dw_b, v_conv_ln_g, v_conv_ln_b, v_w_branch, v_w_gate, v_b_gate, v_w_out, v_norm2_g, v_w_ff1, v_w_ff2, v_norm3_g, v_w_ple_gate, v_b_ple_gate, v_w_ple, v_final_g):
    args = locals()
    wts = {n: args[n] for n in WEIGHTS}
    mom = {n: args['m_' + n] for n in WEIGHTS}
    var = {n: args['v_' + n] for n in WEIGHTS}

    local = {d_name: dict(d, w_in=jnp.swapaxes(d['w_in'], 1, 2))
             for d_name, d in (("w", wts), ("m", mom), ("v", var))}
    others = [n for n in MXU_WEIGHTS if n != 'w_in']

    items = [_gather_item(local["w"]['w_in'].astype(_MXU_DTYPE))]
    for n in others:
        shard = wts[n].astype(_MXU_DTYPE)
        ax = SHARD_AXIS[n]
        items.append(_gather_item(shard, shard.shape[:ax] + (N_DEV * shard.shape[ax],) + shard.shape[ax + 1:], ax))
    items.append(_gather_item(_pack([wts[n] for n in VEC_WEIGHTS], F32)))
    got = _exchange("gather_weights", items)
    full = {n: wts[n] for n in REPLICATED}
    full['w_in'] = jnp.moveaxis(got[0], 0, 1).reshape(DEPTH, IN_COLS, D_MODEL)
    for n, a in zip(others, got[1:]):
        full[n] = a
    for n, slabs in zip(VEC_WEIGHTS, _unpack(got[-1], [wts[n].shape for n in VEC_WEIGHTS], lead=N_DEV)):
        full[n] = _from_slabs(slabs, SHARD_AXIS[n])

    loss, grad_x, grads = _local_step(x[0], p[:, 0], loss_target[0], full)
    loss = lax.psum(loss, ("x", "y", "c"))

    items = [_scatter_item(grads['w_in'].reshape(DEPTH, N_DEV, IN_COLS // N_DEV, D_MODEL), lead=1)]
    items += [_scatter_item(grads[n], axis=SHARD_AXIS[n]) for n in others]
    items.append(_scatter_item(_pack([_to_slabs(grads[n], SHARD_AXIS[n]) for n in VEC_WEIGHTS], F32, lead=N_DEV)))
    items.append(_gather_item(_pack([grads[n] for n in REPLICATED], F32)))
    parts = _exchange("exchange_grads", items)

    results = {}
    for n, part in zip(MXU_WEIGHTS, parts):
        w3, m3, v3 = (_as_rows(local[d][n], 1) for d in ("w", "m", "v"))
        outs = _adamw("adamw_" + n, part.reshape((N_DEV,) + w3.shape), w3, m3, v3)
        for kind, a in zip(("grad", "delta", "new_m", "new_v"), outs):
            a = a.reshape(local["w"][n].shape)
            results[kind, n] = jnp.swapaxes(a, 1, 2) if n == 'w_in' else a
    for names, part, call in ((VEC_WEIGHTS, parts[-2], "adamw_vec"), (REPLICATED, parts[-1], "adamw_replicated")):
        packed = [_pack([d[n] for n in names], F32)[None] for d in (wts, mom, var)]
        outs = _adamw(call, part[:, None], *packed)
        for kind, buf in zip(("grad", "delta", "new_m", "new_v"), outs):
            for n, a in zip(names, _unpack(buf[0], [wts[n].shape for n in names])):
                results[kind, n] = a
    return (loss, grad_x[None]) + tuple(results[kind, n] for kind in ("grad", "delta", "new_m", "new_v")
                                        for n in WEIGHTS)
```

```python
import functools

import numpy as np
import jax
import jax.numpy as jnp
from jax import lax
from jax.experimental import pallas as pl
from jax.experimental.pallas import tpu as pltpu

F32 = jnp.float32
_MXU_DTYPE = jnp.bfloat16
GRAD_DTYPE = jnp.bfloat16

N_DEV = 8
D_MODEL = 1024
DEPTH = 2
CHUNK = 64
PLE_DIM = 256
BW = 512
SG_BLOCK = 128
SG_GROUPS = 4
GLA_HEADS = 4
GLA_DK = 64
GLA_DV = 128
GLA_RANK = 16
GLA_TAU = 16.0
ATT_HEADS = 8
ATT_HD = 64
ATT_BAND = 9
BAND = ATT_BAND * CHUNK
MAX_REL = 256
REL_TABLE = CHUNK + MAX_REL
CONV_K = 31
CONV_HALO = 32
D_FF = 4096
EPS = 1e-6
NEG_INF = -1e30

IN_GROUPS = (("A", 0, 1024), ("B", 1024, 1536), ("a", 2560, 16), ("C", 2576, 1536), ("D", 4112, 1024))
IN_COLS = 5136

ADAM_LR = 0.001
ADAM_B1 = 0.9
ADAM_B2 = 0.999
ADAM_EPS = 1e-08
ADAM_WD = 0.01
ADAM_STEP = 10

ADAMW_TILE = 128
PACK_COLS = 1024
VMEM_LIMIT_MB = 56

_NN = (((1,), (0,)), ((), ()))
_NT = (((1,), (1,)), ((), ()))
_TN = (((0,), (0,)), ((), ()))

WEIGHTS = ['norm1_g', 'w_in', 'sg_ln_g', 'sg_ln_b', 'sg_w', 'sg_b', 'gla_w_a2', 'gla_b_a', 'gla_norm_g',
           'att_rel_bias', 'conv_dw_w', 'conv_dw_b', 'conv_ln_g', 'conv_ln_b', 'w_branch', 'w_gate', 'b_gate',
           'w_out', 'norm2_g', 'w_ff1', 'w_ff2', 'norm3_g', 'w_ple_gate', 'b_ple_gate', 'w_ple', 'final_g']
SHARD_AXIS = {'w_in': 2, 'gla_w_a2': 2, 'att_rel_bias': 2, 'conv_dw_w': 2, 'w_branch': 3, 'w_gate': 2,
              'b_gate': 2, 'w_out': 1, 'w_ff1': 2, 'w_ff2': 1, 'w_ple_gate': 1, 'w_ple': 2}
MXU_WEIGHTS = ('w_in', 'w_branch', 'w_gate', 'w_out', 'w_ff1', 'w_ff2', 'w_ple_gate', 'w_ple')
VEC_WEIGHTS = ('gla_w_a2', 'att_rel_bias', 'conv_dw_w', 'b_gate')
SHARDED = tuple(n for n in WEIGHTS if n in SHARD_AXIS)
REPLICATED = tuple(n for n in WEIGHTS if n not in SHARD_AXIS)


def _mm(a, b, dims=_NN):
    return lax.dot_general(a.astype(_MXU_DTYPE), b.astype(_MXU_DTYPE), dims, preferred_element_type=F32)


def _split3(x):
    x1 = x.astype(jnp.bfloat16)
    r1 = x - x1.astype(F32)
    x2 = r1.astype(jnp.bfloat16)
    x3 = (r1 - x2.astype(F32)).astype(jnp.bfloat16)
    return x1, x2, x3


def _mm_exact_rhs(m, x, dims=_NN):
    return sum(lax.dot_general(m, xi, dims, preferred_element_type=F32) for xi in _split3(x))


def _mm_exact_lhs(x, m, dims=_NN):
    return sum(lax.dot_general(xi, m, dims, preferred_element_type=F32) for xi in _split3(x))


def _sigmoid(x):
    return 1.0 / (1.0 + jnp.exp(-x))


def _gelu(x):
    c = 0.7978845608028654
    t = jnp.tanh(c * (x + 0.044715 * x * x * x))
    return 0.5 * x * (1.0 + t), t


def _gelu_grad(x, t):
    c = 0.7978845608028654
    return 0.5 * (1.0 + t) + 0.5 * x * (1.0 - t * t) * c * (1.0 + 3.0 * 0.044715 * x * x)


def _rms_stat(h):
    return lax.rsqrt(jnp.mean(h * h, axis=-1, keepdims=True) + EPS)


def _rms_bwd(dy, h, g, r):
    hh = h * r
    dhh = dy * g
    dh = r * (dhh - hh * jnp.mean(dhh * hh, axis=-1, keepdims=True))
    return dh, jnp.sum(dy * hh, axis=0, keepdims=True)


def _ln_fwd(x, g, b):
    mu = jnp.mean(x, axis=-1, keepdims=True)
    xc = x - mu
    rs = lax.rsqrt(jnp.mean(xc * xc, axis=-1, keepdims=True) + EPS)
    xh = xc * rs
    return xh * g + b, xh, rs


def _ln_bwd(dy, xh, rs, g):
    dxh = dy * g
    dx = rs * (dxh - jnp.mean(dxh, axis=-1, keepdims=True) - xh * jnp.mean(dxh * xh, axis=-1, keepdims=True))
    return dx, jnp.sum(dy * xh, axis=0, keepdims=True), jnp.sum(dy, axis=0, keepdims=True)


def _row_call(name, body, nt, rows=(), halos=(), res=(), outs=(), accs=(), scratch=(), reverse=False):
    def pos(i):
        return (nt - 1 - i) if reverse else i

    def lead(ndim, f):
        return lambda i: (f(pos(i)),) + (0,) * (ndim - 1)

    in_specs, operands = [], []
    for a, tile in rows:
        in_specs.append(pl.BlockSpec((tile,) + a.shape[1:], lead(a.ndim, lambda t: t)))
        operands.append(a)
    for a, blk, per, side in halos:
        last = a.shape[0] // blk - 1
        delta = {'prev2': -2, 'prev': -1, 'next': per, 'next2': per + 1}[side]
        f = lambda t, per=per, last=last, delta=delta: jnp.clip(t * per + delta, 0, last)
        in_specs.append(pl.BlockSpec((blk,) + a.shape[1:], lead(a.ndim, f)))
        operands.append(a)
    for a in res:
        in_specs.append(pl.BlockSpec(a.shape, lambda i, nd=a.ndim: (0,) * nd, pipeline_mode=pl.Buffered(1)))
        operands.append(a)
    out_specs, out_shape = [], []
    for shape, dtype, tile in outs:
        out_specs.append(pl.BlockSpec((tile,) + tuple(shape[1:]), lead(len(shape), lambda t: t)))
        out_shape.append(jax.ShapeDtypeStruct(tuple(shape), dtype))
    for shape in accs:
        out_specs.append(pl.BlockSpec(tuple(shape), lambda i, nd=len(shape): (0,) * nd))
        out_shape.append(jax.ShapeDtypeStruct(tuple(shape), F32))
    n0, n1, n2, n3, n4 = len(rows), len(halos), len(res), len(outs), len(accs)

    def kern(*refs):
        i = pl.program_id(0)
        row_refs = refs[:n0]
        halo_refs = refs[n0:n0 + n1]
        res_refs = refs[n0 + n1:n0 + n1 + n2]
        out_refs = refs[n0 + n1 + n2:n0 + n1 + n2 + n3]
        acc_refs = refs[n0 + n1 + n2 + n3:n0 + n1 + n2 + n3 + n4]
        scr_refs = refs[n0 + n1 + n2 + n3 + n4:]

        @pl.when(i == 0)
        def _():
            for r in tuple(acc_refs) + tuple(scr_refs):
                r[...] = jnp.zeros(r.shape, r.dtype)

        body(pos(i), row_refs, halo_refs, res_refs, out_refs, acc_refs, scr_refs)

    result = pl.pallas_call(
        kern, grid=(nt,), in_specs=in_specs, out_specs=out_specs, out_shape=out_shape,
        scratch_shapes=[pltpu.VMEM(tuple(s), d) for s, d in scratch],
        compiler_params=pltpu.CompilerParams(dimension_semantics=("arbitrary",),
                                             vmem_limit_bytes=VMEM_LIMIT_MB << 20),
        name=name)(*operands)
    return tuple(result)


def _tn_call(name, a, b, k, n, nblk=1, a_col=False, b_col=True, b_off=0, out='cols', tile=1024):
    nt = a.shape[0] // tile
    if out == 'cols':
        o_shape, o_spec = (k, nblk * n), pl.BlockSpec((k, n), lambda j, t: (0, j))
    elif out == 'rows':
        o_shape, o_spec = (nblk * k, n), pl.BlockSpec((k, n), lambda j, t: (j, 0))
    else:
        o_shape, o_spec = (nblk, k, n), pl.BlockSpec((None, k, n), lambda j, t: (j, 0, 0))

    def kern(a_ref, b_ref, o_ref, acc):
        @pl.when(pl.program_id(1) == 0)
        def _():
            acc[...] = jnp.zeros(acc.shape, acc.dtype)

        acc[...] += lax.dot_general(a_ref[...], b_ref[...], _TN, preferred_element_type=F32)

        @pl.when(pl.program_id(1) == nt - 1)
        def _():
            o_ref[...] = acc[...].astype(o_ref.dtype)

    return pl.pallas_call(
        kern, grid=(nblk, nt),
        in_specs=[pl.BlockSpec((tile, k), (lambda j, t: (t, j)) if a_col else (lambda j, t: (t, 0))),
                  pl.BlockSpec((tile, n), (lambda j, t: (t, j + b_off)) if b_col else (lambda j, t: (t, b_off)))],
        out_specs=o_spec, out_shape=jax.ShapeDtypeStruct(o_shape, GRAD_DTYPE),
        scratch_shapes=[pltpu.VMEM((k, n), F32)],
        compiler_params=pltpu.CompilerParams(dimension_semantics=("arbitrary", "arbitrary"),
                                             vmem_limit_bytes=VMEM_LIMIT_MB << 20),
        name=name)(a, b)


def _inproj_fwd(tag, h, g1, w_groups, tm=256):
    t_len = h.shape[0]

    def body(t, rows, halos, res, outs, accs, scr):
        hv = rows[0][...]
        xn = (hv * _rms_stat(hv) * res[0][...]).astype(_MXU_DTYPE)
        outs[0][...] = xn
        for o, w in zip(outs[1:], res[1:]):
            o[...] = lax.dot_general(xn, w[...], _NT, preferred_element_type=F32)

    outs = [((t_len, D_MODEL), _MXU_DTYPE, tm)] + [((t_len, w.shape[0]), F32, tm) for w in w_groups]
    return _row_call("inproj_fwd" + tag, body, t_len // tm, rows=[(h, tm)], res=[g1] + list(w_groups), outs=outs)


def _sg_mask():
    row = lax.broadcasted_iota(jnp.int32, (SG_BLOCK, SG_BLOCK), 0)
    col = lax.broadcasted_iota(jnp.int32, (SG_BLOCK, SG_BLOCK), 1)
    return jnp.logical_or(row >= CHUNK, col < CHUNK)


def _sg_forward_parts(pa, lg, lb, w_ref, bt):
    tm = pa.shape[0]
    nb = tm // SG_BLOCK
    su, sv = pa[:, :BW], pa[:, BW:]
    u, tu = _gelu(su)
    gv, tv = _gelu(sv)
    vn, xh, rs = _ln_fwd(gv, lg, lb)
    mask = _sg_mask()
    wms, xs, ms = [], [], []
    for g in range(SG_GROUPS):
        wm = jnp.where(mask, w_ref[g], 0.0).astype(_MXU_DTYPE)
        xg = jnp.concatenate([vn[b * SG_BLOCK:(b + 1) * SG_BLOCK, g * 128:(g + 1) * 128] for b in range(nb)], axis=1)
        xg = xg.astype(_MXU_DTYPE)
        ms.append(lax.dot_general(wm, xg, _NN, preferred_element_type=F32) + bt[:, g:g + 1])
        wms.append(wm)
        xs.append(xg)
    mixed = _sg_unfold(ms, nb)
    return su, sv, u, tu, tv, xh, rs, wms, xs, mixed


def _sg_unfold(per_group, nb):
    return jnp.concatenate(
        [jnp.concatenate([per_group[g][:, b * 128:(b + 1) * 128] for g in range(SG_GROUPS)], axis=1)
         for b in range(nb)], axis=0)


def _sg_fwd(tag, proj_a, lg, lb, sg_w, sg_bt, tm=512):
    t_len = proj_a.shape[0]

    def body(t, rows, halos, res, outs, accs, scr):
        parts = _sg_forward_parts(rows[0][...], res[0][...], res[1][...], res[2], res[3][...])
        outs[0][...] = (parts[2] * parts[-1]).astype(_MXU_DTYPE)

    return _row_call("sg_fwd" + tag, body, t_len // tm, rows=[(proj_a, tm)], res=[lg, lb, sg_w, sg_bt],
                     outs=[((t_len, BW), _MXU_DTYPE, tm)])[0]


def _sg_bwd(tag, proj_a, dy, lg, lb, sg_w, sg_bt, tm=512):
    t_len = proj_a.shape[0]
    nb = tm // SG_BLOCK

    def body(t, rows, halos, res, outs, accs, scr):
        lgv = res[0][...]
        su, sv, u, tu, tv, xh, rs, wms, xs, mixed = _sg_forward_parts(rows[0][...], lgv, res[1][...], res[2], res[3][...])
        dyv = rows[1][...]
        dsu = dyv * mixed * _gelu_grad(su, tu)
        dmixed = dyv * u
        mask = _sg_mask()
        dxs, dbs = [], []
        for g in range(SG_GROUPS):
            dm = jnp.concatenate([dmixed[b * SG_BLOCK:(b + 1) * SG_BLOCK, g * 128:(g + 1) * 128] for b in range(nb)],
                                 axis=1)
            dmb = dm.astype(_MXU_DTYPE)
            dw = lax.dot_general(dmb, xs[g], _NT, preferred_element_type=F32)
            accs[0][g] += jnp.where(mask, dw, 0.0)
            dbs.append(jnp.sum(dm, axis=1, keepdims=True))
            dxs.append(lax.dot_general(wms[g], dmb, _TN, preferred_element_type=F32))
        accs[1][...] += jnp.concatenate(dbs, axis=1)
        dvn = _sg_unfold(dxs, nb)
        dgv, dlg, dlb = _ln_bwd(dvn, xh, rs, lgv)
        accs[2][...] += dlg
        accs[3][...] += dlb
        dsv = dgv * _gelu_grad(sv, tv)
        outs[0][...] = jnp.concatenate([dsu, dsv], axis=1).astype(_MXU_DTYPE)

    return _row_call("sg_bwd" + tag, body, t_len // tm, rows=[(proj_a, tm), (dy, tm)], res=[lg, lb, sg_w, sg_bt],
                     outs=[((t_len, 2 * BW), _MXU_DTYPE, tm)],
                     accs=[(SG_GROUPS, SG_BLOCK, SG_BLOCK), (SG_BLOCK, SG_GROUPS), (1, BW), (1, BW)])


def _tri(lower):
    row = lax.broadcasted_iota(jnp.int32, (CHUNK, CHUNK), 0)
    col = lax.broadcasted_iota(jnp.int32, (CHUNK, CHUNK), 1)
    return ((row >= col) if lower else (row <= col)).astype(jnp.bfloat16)


def _gla_gate(pa, wa2, ba):
    z = _mm(pa, wa2) + ba
    log_a = (jnp.minimum(z, 0.0) - jnp.log(1.0 + jnp.exp(-jnp.abs(z)))) * (1.0 / GLA_TAU)
    return z, log_a


def _gla_chunk_fwd(pb, log_a, j):
    sl = slice(j * CHUNK, (j + 1) * CHUNK)
    cum = _mm_exact_rhs(_tri(True), log_a[sl])
    tot = cum[CHUNK - 1:CHUNK]
    w = jnp.exp(tot - cum)
    k = pb[sl, 256:512]
    return sl, cum, tot, w, k * w, jnp.exp(tot)


def _gla_read(q, s_t):
    qs = (q * (GLA_DK ** -0.5)).astype(_MXU_DTYPE)
    sb = s_t.astype(_MXU_DTYPE)
    o = jnp.concatenate([lax.dot_general(qs[:, h * 64:(h + 1) * 64], sb[:, h * 64:(h + 1) * 64], _NT,
                                         preferred_element_type=F32) for h in range(GLA_HEADS)], axis=1)
    return qs, sb, o


def _gla_fwd(tag, proj_b, proj_a, wa2, ba, ng, tm=256):
    t_len = proj_b.shape[0]
    cpt = tm // CHUNK

    def body(t, rows, halos, res, outs, accs, scr):
        pb = rows[0][...]
        _, log_a = _gla_gate(rows[1][...], res[0][...], res[1][...])
        ngv = res[2][...]
        st = scr[0]
        for j in range(cpt):
            sl, cum, tot, w, kd, dec = _gla_chunk_fwd(pb, log_a, j)
            kdb = kd.astype(_MXU_DTYPE)
            vb = pb[sl, 512:1024].astype(_MXU_DTYPE)
            ut = jnp.concatenate([lax.dot_general(vb[:, h * 128:(h + 1) * 128], kdb[:, h * 64:(h + 1) * 64], _TN,
                                                  preferred_element_type=F32) for h in range(GLA_HEADS)], axis=1)
            s_new = dec * st[...] + ut
            st[...] = s_new
            outs[1][j] = s_new
            _, _, o = _gla_read(pb[sl, 0:256], s_new)
            on = jnp.concatenate(
                [o[:, h * 128:(h + 1) * 128] * lax.rsqrt(jnp.mean(jnp.square(o[:, h * 128:(h + 1) * 128]), axis=-1,
                                                                  keepdims=True) + EPS) for h in range(GLA_HEADS)], axis=1)
            r = pb[sl, 1024:1536]
            outs[0][sl, :] = (on * ngv * (r * _sigmoid(r))).astype(_MXU_DTYPE)

    return _row_call("gla_fwd" + tag, body, t_len // tm, rows=[(proj_b, tm), (proj_a, tm)], res=[wa2, ba, ng],
                     outs=[((t_len, BW), _MXU_DTYPE, tm), ((t_len // CHUNK, GLA_DV, 256), F32, cpt)],
                     scratch=[((GLA_DV, 256), F32)])


def _gla_bwd(tag, proj_b, proj_a, dy, states, wa2, ba, ng, tm=256):
    t_len = proj_b.shape[0]
    cpt = tm // CHUNK

    def body(t, rows, halos, res, outs, accs, scr):
        pb = rows[0][...]
        pa = rows[1][...]
        dyv = rows[2][...]
        st_ref = rows[3]
        wa2v = res[0][...]
        z, log_a = _gla_gate(pa, wa2v, res[1][...])
        ngv = res[2][...]
        dst = scr[0]
        s_before_tile = jnp.where(t > 0, halos[0][0], 0.0)
        dz_rows = [None] * cpt
        d_rows = [None] * cpt
        for j in reversed(range(cpt)):
            sl, cum, tot, w, kd, dec = _gla_chunk_fwd(pb, log_a, j)
            s_c = st_ref[j]
            s_prev = st_ref[j - 1] if j > 0 else s_before_tile
            qs, sb, o = _gla_read(pb[sl, 0:256], s_c)
            r = pb[sl, 1024:1536]
            sig = _sigmoid(r)
            sil = r * sig
            dyj = dyv[sl]
            dos, drs, dng = [], [], []
            for h in range(GLA_HEADS):
                hs = slice(h * 128, (h + 1) * 128)
                oh = o[:, hs]
                rstd = lax.rsqrt(jnp.mean(oh * oh, axis=-1, keepdims=True) + EPS)
                on = oh * rstd
                g_h = ngv[:, hs]
                don = dyj[:, hs] * g_h * sil[:, hs]
                dng.append(jnp.sum(dyj[:, hs] * on * sil[:, hs], axis=0, keepdims=True))
                drs.append(dyj[:, hs] * on * g_h * (sig[:, hs] * (1.0 + r[:, hs] * (1.0 - sig[:, hs]))))
                dos.append(rstd * (don - on * jnp.mean(don * on, axis=-1, keepdims=True)))
            accs[2][...] += jnp.concatenate(dng, axis=1)
            dr = jnp.concatenate(drs, axis=1)
            do = jnp.concatenate(dos, axis=1)
            dob = do.astype(_MXU_DTYPE)
            dst_tot = dst[...] + jnp.concatenate(
                [lax.dot_general(dob[:, h * 128:(h + 1) * 128], qs[:, h * 64:(h + 1) * 64], _TN,
                                 preferred_element_type=F32) for h in range(GLA_HEADS)], axis=1)
            dq = jnp.concatenate(
                [lax.dot_general(dob[:, h * 128:(h + 1) * 128], sb[:, h * 64:(h + 1) * 64], _NN,
                                 preferred_element_type=F32) for h in range(GLA_HEADS)], axis=1) * (GLA_DK ** -0.5)
            ddec = jnp.sum(dst_tot * s_prev, axis=0, keepdims=True)
            dst[...] = dec * dst_tot
            dub = dst_tot.astype(_MXU_DTYPE)
            vb = pb[sl, 512:1024].astype(_MXU_DTYPE)
            kdb = kd.astype(_MXU_DTYPE)
            dkd = jnp.concatenate(
                [lax.dot_general(vb[:, h * 128:(h + 1) * 128], dub[:, h * 64:(h + 1) * 64], _NN,
                                 preferred_element_type=F32) for h in range(GLA_HEADS)], axis=1)
            dv = jnp.concatenate(
                [lax.dot_general(kdb[:, h * 64:(h + 1) * 64], dub[:, h * 64:(h + 1) * 64], _NT,
                                 preferred_element_type=F32) for h in range(GLA_HEADS)], axis=1)
            dk = dkd * w
            e = dkd * kd
            dtot = jnp.sum(e, axis=0, keepdims=True) + ddec * dec
            last = lax.broadcasted_iota(jnp.int32, e.shape, 0) == CHUNK - 1
            dcum = jnp.where(last, dtot - e, -e)
            dla = _mm_exact_rhs(_tri(False), dcum)
            dz_rows[j] = dla * (1.0 / GLA_TAU) * _sigmoid(-z[sl])
            d_rows[j] = jnp.concatenate([dq, dk, dv, dr], axis=1)
        dz = jnp.concatenate(dz_rows, axis=0)
        dzb = dz.astype(_MXU_DTYPE)
        outs[0][...] = jnp.concatenate(d_rows, axis=0).astype(_MXU_DTYPE)
        outs[1][...] = lax.dot_general(dzb, wa2v.astype(_MXU_DTYPE), _NT, preferred_element_type=F32).astype(_MXU_DTYPE)
        accs[0][...] += lax.dot_general(pa.astype(_MXU_DTYPE), dzb, _TN, preferred_element_type=F32)
        accs[1][...] += jnp.sum(dz, axis=0, keepdims=True)

    return _row_call("gla_bwd" + tag, body, t_len // tm,
                     rows=[(proj_b, tm), (proj_a, tm), (dy, tm), (states, cpt)],
                     halos=[(states, 1, cpt, 'prev')], res=[wa2, ba, ng],
                     outs=[((t_len, 1536), _MXU_DTYPE, tm), ((t_len, GLA_RANK), _MXU_DTYPE, tm)],
                     accs=[(GLA_RANK, 256), (1, 256), (1, BW)], scratch=[((GLA_DV, 256), F32)], reverse=True)


ATT_TM = 256
ATT_KEYS = ATT_TM + (ATT_BAND - 1) * CHUNK


def _rel_index():
    r = np.arange(ATT_TM)[:, None]
    c = np.arange(ATT_KEYS)[None, :]
    m = c - (r // CHUNK) * CHUNK
    rel = (r % CHUNK) + (ATT_BAND - 1) * CHUNK - m
    idx = np.clip(rel, -(CHUNK - 1), MAX_REL) + (CHUNK - 1)
    idx = np.where((m >= 0) & (m < BAND), idx, -1)
    return jnp.asarray(idx.reshape(1, ATT_TM * ATT_KEYS), jnp.int32)


BIAS_COLS = 4096


def _bias_expand(tag, rel_bias):
    n = ATT_TM * ATT_KEYS

    def kern(rel_ref, idx_ref, o_ref):
        idx = idx_ref[...]
        onehot = (lax.broadcasted_iota(jnp.int32, (REL_TABLE, BIAS_COLS), 0) == idx).astype(jnp.bfloat16)
        o_ref[...] = _mm_exact_lhs(rel_ref[...], onehot) + jnp.where(idx < 0, NEG_INF, 0.0)

    return pl.pallas_call(
        kern, grid=(n // BIAS_COLS,),
        in_specs=[pl.BlockSpec((ATT_HEADS, REL_TABLE), lambda i: (0, 0)), pl.BlockSpec((1, BIAS_COLS), lambda i: (0, i))],
        out_specs=pl.BlockSpec((ATT_HEADS, BIAS_COLS), lambda i: (0, i)),
        out_shape=jax.ShapeDtypeStruct((ATT_HEADS, n), F32), name="bias_expand" + tag)(rel_bias, _rel_index())


def _bias_reduce(tag, dbias):
    n = ATT_TM * ATT_KEYS

    def kern(db_ref, idx_ref, o_ref):
        @pl.when(pl.program_id(0) == 0)
        def _():
            o_ref[...] = jnp.zeros(o_ref.shape, o_ref.dtype)

        onehot = (lax.broadcasted_iota(jnp.int32, (REL_TABLE, BIAS_COLS), 0) == idx_ref[...]).astype(jnp.bfloat16)
        o_ref[...] += _mm_exact_lhs(db_ref[...], onehot, _NT)

    return pl.pallas_call(
        kern, grid=(n // BIAS_COLS,),
        in_specs=[pl.BlockSpec((ATT_HEADS, BIAS_COLS), lambda i: (0, i)), pl.BlockSpec((1, BIAS_COLS), lambda i: (0, i))],
        out_specs=pl.BlockSpec((ATT_HEADS, REL_TABLE), lambda i: (0, 0)),
        out_shape=jax.ShapeDtypeStruct((ATT_HEADS, REL_TABLE), F32),
        compiler_params=pltpu.CompilerParams(dimension_semantics=("arbitrary",)),
        name="bias_reduce" + tag)(dbias, _rel_index())


def _attn_stage(t, pc_ref, p1_ref, p2_ref, kv):
    tm = ATT_TM
    kv[0:tm, :] = jnp.where(t > 1, p2_ref[:, 512:1536], 0.0).astype(kv.dtype)
    kv[tm:2 * tm, :] = jnp.where(t > 0, p1_ref[:, 512:1536], 0.0).astype(kv.dtype)
    kv[2 * tm:, :] = pc_ref[:, 512:1536].astype(kv.dtype)
    q = (pc_ref[:, 0:512] * (ATT_HD ** -0.5)).astype(_MXU_DTYPE)
    ok = lax.broadcasted_iota(jnp.int32, (tm, ATT_KEYS), 1) >= (2 - t) * tm
    return q, ok


def _attn_probs(q, kv, bias_h, ok, h):
    hs = slice(h * ATT_HD, (h + 1) * ATT_HD)
    s = lax.dot_general(q[:, hs], kv[:, hs], _NT, preferred_element_type=F32) + bias_h
    s = jnp.where(ok, s, NEG_INF)
    e = jnp.exp(s - jnp.max(s, axis=-1, keepdims=True))
    return e / jnp.sum(e, axis=-1, keepdims=True)


def _attn_halos(proj_c):
    return [(proj_c, ATT_TM, 1, 'prev'), (proj_c, ATT_TM, 1, 'prev2')]


def _attn_fwd(tag, proj_c, bias):
    t_len = proj_c.shape[0]
    tm = ATT_TM

    def body(t, rows, halos, res, outs, accs, scr):
        b_ref, kv = res[0], scr[0]
        q, ok = _attn_stage(t, rows[0], halos[0], halos[1], kv)
        o = [lax.dot_general(_attn_probs(q, kv, b_ref[h], ok, h).astype(_MXU_DTYPE),
                             kv[:, BW + h * ATT_HD:BW + (h + 1) * ATT_HD], _NN, preferred_element_type=F32)
             for h in range(ATT_HEADS)]
        outs[0][...] = jnp.concatenate(o, axis=1).astype(_MXU_DTYPE)

    return _row_call("attn_fwd" + tag, body, t_len // tm, rows=[(proj_c, tm)], halos=_attn_halos(proj_c),
                     res=[bias], outs=[((t_len, BW), _MXU_DTYPE, tm)], scratch=[((ATT_KEYS, 1024), _MXU_DTYPE)])[0]


def _attn_bwd(tag, proj_c, dy, bias):
    t_len = proj_c.shape[0]
    tm = ATT_TM
    scale = ATT_HD ** -0.5

    def body(t, rows, halos, res, outs, accs, scr):
        b_ref, kv = res[0], scr[0]
        q, ok = _attn_stage(t, rows[0], halos[0], halos[1], kv)
        do = rows[1][...].astype(_MXU_DTYPE)
        dqs, dks, dvs = [], [], []
        for h in range(ATT_HEADS):
            hs = slice(h * ATT_HD, (h + 1) * ATT_HD)
            vs = slice(BW + h * ATT_HD, BW + (h + 1) * ATT_HD)
            p = _attn_probs(q, kv, b_ref[h], ok, h)
            dp = lax.dot_general(do[:, hs], kv[:, vs], _NT, preferred_element_type=F32)
            ds = p * (dp - jnp.sum(dp * p, axis=-1, keepdims=True))
            accs[0][h] += ds
            dsb = ds.astype(_MXU_DTYPE)
            dqs.append(lax.dot_general(dsb, kv[:, hs], _NN, preferred_element_type=F32) * scale)
            dks.append(lax.dot_general(dsb, q[:, hs], _TN, preferred_element_type=F32))
            dvs.append(lax.dot_general(p.astype(_MXU_DTYPE), do[:, hs], _TN, preferred_element_type=F32))
        outs[0][...] = jnp.concatenate(dqs, axis=1).astype(_MXU_DTYPE)
        dkv = jnp.concatenate(dks + dvs, axis=1)
        outs[1][...] = dkv[2 * tm:, :]
        outs[2][...] = dkv[tm:2 * tm, :]
        outs[3][...] = dkv[0:tm, :]

    return _row_call("attn_bwd" + tag, body, t_len // tm, rows=[(proj_c, tm), (dy, tm)],
                     halos=_attn_halos(proj_c), res=[bias],
                     outs=[((t_len, BW), _MXU_DTYPE, tm)] + [((t_len, 1024), F32, tm)] * 3,
                     accs=[(ATT_HEADS, ATT_TM, ATT_KEYS)], scratch=[((ATT_KEYS, 1024), _MXU_DTYPE)])


def _attn_combine(tag, dq, dkv_own, dkv_prev, dkv_prev2):
    t_len = dq.shape[0]
    tm = ATT_TM
    nt = t_len // tm

    def body(t, rows, halos, res, outs, accs, scr):
        dkv = (rows[1][...] + jnp.where(t < nt - 1, halos[0][...], 0.0)) + jnp.where(t < nt - 2, halos[1][...], 0.0)
        outs[0][...] = jnp.concatenate([rows[0][...], dkv.astype(_MXU_DTYPE)], axis=1)

    return _row_call("attn_combine" + tag, body, nt, rows=[(dq, tm), (dkv_own, tm)],
                     halos=[(dkv_prev, tm, 1, 'next'), (dkv_prev2, tm, 1, 'next2')],
                     outs=[((t_len, 1536), _MXU_DTYPE, tm)])[0]


def _conv_glu(pd):
    a, g = pd[:, :BW], pd[:, BW:]
    sig = _sigmoid(g)
    return a, sig, a * sig


def _conv_stage(t, pd_ref, ph_ref, win):
    pd = pd_ref[...]
    a, sig, y0 = _conv_glu(pd)
    win[0:CONV_HALO, :] = jnp.where(t > 0, _conv_glu(ph_ref[...])[2], 0.0)
    win[CONV_HALO:, :] = y0
    return a, sig


def _conv_fwd(tag, proj_d, dw_w, dw_b, ln_g, ln_b, tm=512):
    t_len = proj_d.shape[0]
    lead = CONV_HALO - (CONV_K - 1)

    def body(t, rows, halos, res, outs, accs, scr):
        win = scr[0]
        _conv_stage(t, rows[0], halos[0], win)
        w_ref = res[0]
        yc = res[1][...] + w_ref[0:1, :] * win[pl.ds(lead, tm), :]
        for j in range(1, CONV_K):
            yc = yc + w_ref[j:j + 1, :] * win[pl.ds(lead + j, tm), :]
        outs[1][...] = yc
        yl, _, _ = _ln_fwd(yc, res[2][...], res[3][...])
        outs[0][...] = (yl * _sigmoid(yl)).astype(_MXU_DTYPE)

    return _row_call("conv_fwd" + tag, body, t_len // tm, rows=[(proj_d, tm)],
                     halos=[(proj_d, CONV_HALO, tm // CONV_HALO, 'prev')], res=[dw_w, dw_b, ln_g, ln_b],
                     outs=[((t_len, BW), _MXU_DTYPE, tm), ((t_len, BW), F32, tm)],
                     scratch=[((tm + CONV_HALO, BW), F32)])


def _conv_bwd_norm(tag, yc, dy, ln_g, ln_b, tm=512):
    t_len = yc.shape[0]

    def body(t, rows, halos, res, outs, accs, scr):
        lgv = res[0][...]
        yl, xh, rs = _ln_fwd(rows[0][...], lgv, res[1][...])
        sig = _sigmoid(yl)
        dyl = rows[1][...] * (sig * (1.0 + yl * (1.0 - sig)))
        dyc, dlg, dlb = _ln_bwd(dyl, xh, rs, lgv)
        outs[0][...] = dyc
        accs[0][...] += dlg
        accs[1][...] += dlb
        accs[2][...] += jnp.sum(dyc, axis=0, keepdims=True)

    return _row_call("conv_bwd_norm" + tag, body, t_len // tm, rows=[(yc, tm), (dy, tm)], res=[ln_g, ln_b],
                     outs=[((t_len, BW), F32, tm)], accs=[(1, BW), (1, BW), (1, BW)])


def _conv_bwd_taps(tag, proj_d, dyc, dw_w, tm=512):
    t_len = proj_d.shape[0]
    nt = t_len // tm
    lead = CONV_HALO - (CONV_K - 1)

    def body(t, rows, halos, res, outs, accs, scr):
        win, wd = scr[0], scr[1]
        a, sig = _conv_stage(t, rows[0], halos[0], win)
        dycv = rows[1][...]
        wd[0:tm, :] = dycv
        wd[tm:, :] = jnp.where(t < nt - 1, halos[1][...], 0.0)
        w_ref = res[0]
        dy0 = jnp.zeros((tm, BW), F32)
        for j in range(CONV_K):
            dy0 = dy0 + w_ref[j:j + 1, :] * wd[pl.ds(CONV_K - 1 - j, tm), :]
            accs[0][j:j + 1, :] += jnp.sum(dycv * win[pl.ds(lead + j, tm), :], axis=0, keepdims=True)
        outs[0][...] = jnp.concatenate([dy0 * sig, dy0 * a * sig * (1.0 - sig)], axis=1).astype(_MXU_DTYPE)

    return _row_call("conv_bwd_taps" + tag, body, nt, rows=[(proj_d, tm), (dyc, tm)],
                     halos=[(proj_d, CONV_HALO, tm // CONV_HALO, 'prev'), (dyc, CONV_HALO, tm // CONV_HALO, 'next')],
                     res=[dw_w], outs=[((t_len, 2 * BW), _MXU_DTYPE, tm)], accs=[(CONV_K, BW)],
                     scratch=[((tm + CONV_HALO, BW), F32), ((tm + CONV_HALO, BW), F32)])


def _merge_fwd(tag, h, xn, ys, w_gate, b_gate, w_branch, w_out, tm=256):
    t_len = h.shape[0]

    def body(t, rows, halos, res, outs, accs, scr):
        xnv = rows[1][...]
        wg_ref, bg_ref, wb_ref, wo_ref = res
        merged = jnp.zeros((tm, D_MODEL), F32)
        for n in range(4):
            cs = slice(n * D_MODEL, (n + 1) * D_MODEL)
            gate = _sigmoid(lax.dot_general(xnv, wg_ref[n], _NN, preferred_element_type=F32) + bg_ref[n:n + 1, :])
            bo = lax.dot_general(rows[2 + n][...], wb_ref[n], _NN, preferred_element_type=F32)
            outs[0][:, cs] = gate
            outs[1][:, cs] = bo.astype(_MXU_DTYPE)
            merged = merged + gate * bo
        mb = merged.astype(_MXU_DTYPE)
        outs[2][...] = mb
        outs[3][...] = rows[0][...] + lax.dot_general(mb, wo_ref[...], _NN, preferred_element_type=F32)

    return _row_call("merge_fwd" + tag, body, t_len // tm, rows=[(h, tm), (xn, tm)] + [(y, tm) for y in ys],
                     res=[w_gate, b_gate, w_branch, w_out],
                     outs=[((t_len, 4 * D_MODEL), F32, tm), ((t_len, 4 * D_MODEL), _MXU_DTYPE, tm),
                           ((t_len, D_MODEL), _MXU_DTYPE, tm), ((t_len, D_MODEL), F32, tm)])


def _merge_bwd(tag, dh, gate, bo, w_gate, w_branch, w_out, tm=256):
    t_len = dh.shape[0]

    def body(t, rows, halos, res, outs, accs, scr):
        wg_ref, wb_ref, wo_ref = res
        dhb = rows[0][...].astype(_MXU_DTYPE)
        outs[0][...] = dhb
        dmerged = lax.dot_general(dhb, wo_ref[...], _NT, preferred_element_type=F32)
        dxn = jnp.zeros((tm, D_MODEL), F32)
        dbg = []
        for n in range(4):
            cs = slice(n * D_MODEL, (n + 1) * D_MODEL)
            g = rows[1][:, cs]
            dbo = (dmerged * g).astype(_MXU_DTYPE)
            dgp = dmerged * rows[2][:, cs].astype(F32) * (g * (1.0 - g))
            dgb = dgp.astype(_MXU_DTYPE)
            outs[1][:, cs] = dbo
            outs[2][:, cs] = dgb
            outs[4 + n][...] = lax.dot_general(dbo, wb_ref[n], _NT, preferred_element_type=F32)
            dxn = dxn + lax.dot_general(dgb, wg_ref[n], _NT, preferred_element_type=F32)
            dbg.append(jnp.sum(dgp, axis=0, keepdims=True))
        outs[3][...] = dxn
        accs[0][...] += jnp.concatenate(dbg, axis=1)

    return _row_call("merge_bwd" + tag, body, t_len // tm, rows=[(dh, tm), (gate, tm), (bo, tm)],
                     res=[w_gate, w_branch, w_out],
                     outs=[((t_len, D_MODEL), _MXU_DTYPE, tm), ((t_len, 4 * D_MODEL), _MXU_DTYPE, tm),
                           ((t_len, 4 * D_MODEL), _MXU_DTYPE, tm), ((t_len, D_MODEL), F32, tm)]
                     + [((t_len, BW), F32, tm)] * 4,
                     accs=[(1, 4 * D_MODEL)])


FF_COLS = 1024


def _ffn_fwd(tag, h, g2, w1, w2, tm=256):
    t_len = h.shape[0]

    def body(t, rows, halos, res, outs, accs, scr):
        hv = rows[0][...]
        hn = (hv * _rms_stat(hv) * res[0][...]).astype(_MXU_DTYPE)
        outs[0][...] = hn
        acc = hv
        for c in range(D_FF // FF_COLS):
            cs = slice(c * FF_COLS, (c + 1) * FF_COLS)
            pre = lax.dot_general(hn, res[1][:, cs], _NN, preferred_element_type=F32)
            outs[1][:, cs] = pre
            ff = jnp.square(jnp.maximum(pre, 0.0)).astype(_MXU_DTYPE)
            acc = acc + lax.dot_general(ff, res[2][cs, :], _NN, preferred_element_type=F32)
        outs[2][...] = acc

    return _row_call("ffn_fwd" + tag, body, t_len // tm, rows=[(h, tm)], res=[g2, w1, w2],
                     outs=[((t_len, D_MODEL), _MXU_DTYPE, tm), ((t_len, D_FF), F32, tm), ((t_len, D_MODEL), F32, tm)])


def _ffn_bwd(tag, dh, h, pre, g2, w1, w2, tm=256):
    t_len = dh.shape[0]

    def body(t, rows, halos, res, outs, accs, scr):
        dhv = rows[0][...]
        hv = rows[1][...]
        dhb = dhv.astype(_MXU_DTYPE)
        outs[0][...] = dhb
        dhn = jnp.zeros((tm, D_MODEL), F32)
        for c in range(D_FF // FF_COLS):
            cs = slice(c * FF_COLS, (c + 1) * FF_COLS)
            r = jnp.maximum(rows[2][:, cs], 0.0)
            outs[1][:, cs] = (r * r).astype(_MXU_DTYPE)
            dpre = (lax.dot_general(dhb, res[2][cs, :], _NT, preferred_element_type=F32) * (2.0 * r)).astype(_MXU_DTYPE)
            outs[2][:, cs] = dpre
            dhn = dhn + lax.dot_general(dpre, res[1][:, cs], _NT, preferred_element_type=F32)
        dres, dg = _rms_bwd(dhn, hv, res[0][...], _rms_stat(hv))
        outs[3][...] = dhv + dres
        accs[0][...] += dg

    return _row_call("ffn_bwd" + tag, body, t_len // tm, rows=[(dh, tm), (h, tm), (pre, tm)], res=[g2, w1, w2],
                     outs=[((t_len, D_MODEL), _MXU_DTYPE, tm), ((t_len, D_FF), _MXU_DTYPE, tm),
                           ((t_len, D_FF), _MXU_DTYPE, tm), ((t_len, D_MODEL), F32, tm)],
                     accs=[(1, D_MODEL)])


def _ple_fwd(tag, h, p, g3, w_pg, b_pg, w_ple, tm=256):
    t_len = h.shape[0]

    def body(t, rows, halos, res, outs, accs, scr):
        hv = rows[0][...]
        hg = (hv * _rms_stat(hv) * res[0][...]).astype(_MXU_DTYPE)
        pb = rows[1][...].astype(_MXU_DTYPE)
        pg = _sigmoid(lax.dot_general(hg, res[1][...], _NN, preferred_element_type=F32) + res[2][...])
        pe = lax.dot_general(pb, res[3][...], _NN, preferred_element_type=F32)
        outs[0][...] = hg
        outs[1][...] = pb
        outs[2][...] = pg
        outs[3][...] = pe
        outs[4][...] = hv + pg * pe

    return _row_call("ple_fwd" + tag, body, t_len // tm, rows=[(h, tm), (p, tm)], res=[g3, w_pg, b_pg, w_ple],
                     outs=[((t_len, D_MODEL), _MXU_DTYPE, tm), ((t_len, PLE_DIM), _MXU_DTYPE, tm),
                           ((t_len, D_MODEL), F32, tm), ((t_len, D_MODEL), F32, tm), ((t_len, D_MODEL), F32, tm)])


def _ple_bwd(tag, dh, h, pg, pe, g3, w_pg, tm=256):
    t_len = dh.shape[0]

    def body(t, rows, halos, res, outs, accs, scr):
        dhv = rows[0][...]
        hv = rows[1][...]
        pgv = rows[2][...]
        dgp = dhv * rows[3][...] * (pgv * (1.0 - pgv))
        dgb = dgp.astype(_MXU_DTYPE)
        outs[0][...] = dgb
        outs[1][...] = (dhv * pgv).astype(_MXU_DTYPE)
        dhg = lax.dot_general(dgb, res[1][...], _NT, preferred_element_type=F32)
        dres, dg = _rms_bwd(dhg, hv, res[0][...], _rms_stat(hv))
        outs[2][...] = dhv + dres
        accs[0][...] += jnp.sum(dgp, axis=0, keepdims=True)
        accs[1][...] += dg

    return _row_call("ple_bwd" + tag, body, t_len // tm, rows=[(dh, tm), (h, tm), (pg, tm), (pe, tm)],
                     res=[g3, w_pg],
                     outs=[((t_len, D_MODEL), _MXU_DTYPE, tm), ((t_len, D_MODEL), _MXU_DTYPE, tm),
                           ((t_len, D_MODEL), F32, tm)],
                     accs=[(1, D_MODEL), (1, D_MODEL)])


def _inproj_bwd(tag, dh, h, dxn_gate, dprojs, g1, w_groups, tm=256):
    t_len = dh.shape[0]

    def body(t, rows, halos, res, outs, accs, scr):
        hv = rows[1][...]
        dxn = rows[2][...]
        for dp, w in zip(rows[3:], res[1:]):
            dxn = dxn + lax.dot_general(dp[...], w[...], _NN, preferred_element_type=F32)
        dres, dg = _rms_bwd(dxn, hv, res[0][...], _rms_stat(hv))
        outs[0][...] = rows[0][...] + dres
        accs[0][...] += dg

    return _row_call("inproj_bwd" + tag, body, t_len // tm,
                     rows=[(dh, tm), (h, tm), (dxn_gate, tm)] + [(d, tm) for d in dprojs],
                     res=[g1] + list(w_groups), outs=[((t_len, D_MODEL), F32, tm)], accs=[(1, D_MODEL)])


def _loss_head(h, target, gf, tm=512):
    t_len = h.shape[0]

    def body(t, rows, halos, res, outs, accs, scr):
        hv = rows[0][...]
        g = res[0][...]
        r = _rms_stat(hv)
        diff = hv * r * g - rows[1][...]
        accs[0][...] += 0.5 * jnp.sum(jnp.mean(diff * diff, axis=-1, keepdims=True), axis=0, keepdims=True)
        dh, dg = _rms_bwd(diff * (1.0 / D_MODEL), hv, g, r)
        outs[0][...] = dh
        accs[1][...] += dg

    return _row_call("loss_head", body, t_len // tm, rows=[(h, tm), (target, tm)], res=[gf],
                     outs=[((t_len, D_MODEL), F32, tm)], accs=[(1, 128), (1, D_MODEL)])


def _row(v):
    return v.reshape(1, -1)


def _layer_fwd(i, h, p_i, w):
    tag = "_l%d" % i
    win = [w['w_in'][i][s:s + n] for _, s, n in IN_GROUPS]
    xn, pa, pb, pr, pc, pd = _inproj_fwd(tag, h, _row(w['norm1_g'][i]), win)
    sg_bt = w['sg_b'][i].T
    y_a = _sg_fwd(tag, pa, _row(w['sg_ln_g'][i]), _row(w['sg_ln_b'][i]), w['sg_w'][i], sg_bt)
    y_b, states = _gla_fwd(tag, pb, pr, w['gla_w_a2'][i], _row(w['gla_b_a'][i]), _row(w['gla_norm_g'][i]))
    bias = _bias_expand(tag, w['att_rel_bias'][i]).reshape(ATT_HEADS, ATT_TM, ATT_KEYS)
    y_c = _attn_fwd(tag, pc, bias)
    y_d, yc = _conv_fwd(tag, pd, w['conv_dw_w'][i], _row(w['conv_dw_b'][i]), _row(w['conv_ln_g'][i]),
                        _row(w['conv_ln_b'][i]))
    ys = (y_a, y_b, y_c, y_d)
    gate, bo, merged, h1 = _merge_fwd(tag, h, xn, ys, w['w_gate'][i], w['b_gate'][i], w['w_branch'][i], w['w_out'][i])
    hn, pre, h2 = _ffn_fwd(tag, h1, _row(w['norm2_g'][i]), w['w_ff1'][i], w['w_ff2'][i])
    hg, p_b, pg, pe, h3 = _ple_fwd(tag, h2, p_i, _row(w['norm3_g'][i]), w['w_ple_gate'][i], _row(w['b_ple_gate'][i]),
                                   w['w_ple'][i])
    saved = dict(h=h, xn=xn, pa=pa, pb=pb, pr=pr, pc=pc, pd=pd, states=states, bias=bias, yc=yc, ys=ys, gate=gate,
                 bo=bo, merged=merged, h1=h1, hn=hn, pre=pre, h2=h2, hg=hg, p_b=p_b, pg=pg, pe=pe, win=win,
                 sg_bt=sg_bt)
    return h3, saved


def _layer_bwd(i, dh3, s, w):
    tag = "_l%d" % i
    g = {}
    dgp, dpe, dh2, db_pg, dg3 = _ple_bwd(tag, dh3, s['h2'], s['pg'], s['pe'], _row(w['norm3_g'][i]), w['w_ple_gate'][i])
    g['b_ple_gate'], g['norm3_g'] = db_pg[0], dg3[0]
    g['w_ple_gate'] = _tn_call("dw_ple_gate" + tag, s['hg'], dgp, D_MODEL, D_MODEL)
    g['w_ple'] = _tn_call("dw_ple" + tag, s['p_b'], dpe, PLE_DIM, D_MODEL)

    dh2b, ffb, dpre, dh1, dg2 = _ffn_bwd(tag, dh2, s['h1'], s['pre'], _row(w['norm2_g'][i]), w['w_ff1'][i], w['w_ff2'][i])
    g['norm2_g'] = dg2[0]
    g['w_ff1'] = _tn_call("dw_ff1" + tag, s['hn'], dpre, D_MODEL, FF_COLS, nblk=D_FF // FF_COLS)
    g['w_ff2'] = _tn_call("dw_ff2" + tag, ffb, dh2b, FF_COLS, D_MODEL, nblk=D_FF // FF_COLS, a_col=True, b_col=False,
                          out='rows')

    dh1b, dbo, dgpre, dxn_gate, dy_a, dy_b, dy_c, dy_d, db_gate = _merge_bwd(
        tag, dh1, s['gate'], s['bo'], w['w_gate'][i], w['w_branch'][i], w['w_out'][i])
    g['b_gate'] = db_gate.reshape(4, D_MODEL)
    g['w_out'] = _tn_call("dw_out" + tag, s['merged'], dh1b, D_MODEL, D_MODEL)
    g['w_gate'] = _tn_call("dw_gate" + tag, s['xn'], dgpre, D_MODEL, D_MODEL, nblk=4, out='stack')
    g['w_branch'] = jnp.stack([_tn_call("dw_branch%d%s" % (n, tag), s['ys'][n], dbo, BW, D_MODEL, b_off=n)
                             for n in range(4)])

    lg, lb = _row(w['sg_ln_g'][i]), _row(w['sg_ln_b'][i])
    dpa, dsg_w, dsg_bt, dlg, dlb = _sg_bwd(tag, s['pa'], dy_a, lg, lb, w['sg_w'][i], s['sg_bt'])
    g['sg_w'], g['sg_b'], g['sg_ln_g'], g['sg_ln_b'] = dsg_w, dsg_bt.T, dlg[0], dlb[0]

    dpb, dpr, dwa2, dba, dng = _gla_bwd(tag, s['pb'], s['pr'], dy_b, s['states'], w['gla_w_a2'][i],
                                        _row(w['gla_b_a'][i]), _row(w['gla_norm_g'][i]))
    g['gla_w_a2'], g['gla_b_a'], g['gla_norm_g'] = dwa2, dba[0], dng[0]

    dq, dkv_own, dkv_prev, dkv_prev2, dbias = _attn_bwd(tag, s['pc'], dy_c, s['bias'])
    dpc = _attn_combine(tag, dq, dkv_own, dkv_prev, dkv_prev2)
    g['att_rel_bias'] = _bias_reduce(tag, dbias.reshape(ATT_HEADS, ATT_TM * ATT_KEYS))

    cg, cb = _row(w['conv_ln_g'][i]), _row(w['conv_ln_b'][i])
    dyc, dcg, dcb, ddwb = _conv_bwd_norm(tag, s['yc'], dy_d, cg, cb)
    dpd, ddw = _conv_bwd_taps(tag, s['pd'], dyc, w['conv_dw_w'][i])
    g['conv_ln_g'], g['conv_ln_b'], g['conv_dw_b'], g['conv_dw_w'] = dcg[0], dcb[0], ddwb[0], ddw

    dprojs = (dpa, dpb, dpr, dpc, dpd)
    dh0, dg1 = _inproj_bwd(tag, dh1, s['h'], dxn_gate, dprojs, _row(w['norm1_g'][i]), s['win'])
    g['norm1_g'] = dg1[0]
    g['w_in'] = jnp.concatenate([_tn_call("dw_in%s%s" % (name, tag), dp, s['xn'], n, D_MODEL)
                                 for (name, _, n), dp in zip(IN_GROUPS, dprojs)], axis=0)
    return dh0, g


def _local_step(x, p, target, w):
    h = x
    saved = []
    for i in range(DEPTH):
        h, s = _layer_fwd(i, h, p[i], w)
        saved.append(s)
    dh, loss, dgf = _loss_head(h, target, _row(w['final_g']))
    per_layer = [None] * DEPTH
    for i in reversed(range(DEPTH)):
        dh, per_layer[i] = _layer_bwd(i, dh, saved[i], w)
    grads = {n: jnp.stack([per_layer[i][n] for i in range(DEPTH)]) for n in WEIGHTS if n != 'final_g'}
    grads['final_g'] = dgf[0]
    return loss[0, 0], dh, grads


def _peers():
    x, y, c = lax.axis_index("x"), lax.axis_index("y"), lax.axis_index("c")
    me = 4 * x + 2 * y + c
    out = []
    for k in range(1, N_DEV):
        px = (1 - x) if k & 4 else x
        py = (1 - y) if k & 2 else y
        pc = (1 - c) if k & 1 else c
        out.append((k - 1, (px, py, pc), 4 * px + 2 * py + pc))
    return me, out


def _block(ref, axis, idx, width):
    ix = [slice(None)] * len(ref.shape)
    ix[axis] = pl.ds(pl.multiple_of(idx * width, width), width)
    return ref.at[tuple(ix)]


def _slot(ref, idx):
    return ref.at[idx]


def _whole(ref, idx):
    return ref


def _gather_item(src, out_shape=None, axis=None):
    if axis is None:
        return dict(src=src, out=(N_DEV,) + src.shape, take=_whole, put=_slot)
    return dict(src=src, out=tuple(out_shape), take=_whole,
                put=lambda ref, s: _block(ref, axis, s, src.shape[axis]))


def _scatter_item(src, axis=None, lead=0):
    if axis is None:
        shape = src.shape[:lead] + src.shape[lead + 1:]
        take = lambda ref, s: ref.at[(slice(None),) * lead + (s,)]
    else:
        width = src.shape[axis] // N_DEV
        shape = src.shape[:axis] + (width,) + src.shape[axis + 1:]
        take = lambda ref, s: _block(ref, axis, s, width)
    return dict(src=src, out=(N_DEV,) + shape, take=take, put=_slot)


def _exchange(name, items):
    n = len(items)

    def body(*refs):
        src_refs, out_refs = refs[:n], refs[n:2 * n]
        send_sems, recv_sems, local_sems = refs[2 * n:]
        me, peers = _peers()
        started = []
        for i, it in enumerate(items):
            cp = pltpu.make_async_copy(it['take'](src_refs[i], me), it['put'](out_refs[i], me), local_sems.at[i])
            cp.start()
            started.append(cp)

        def remote(i, k, pos, receiver, sender):
            it = items[i]
            return pltpu.make_async_remote_copy(
                src_ref=it['take'](src_refs[i], receiver), dst_ref=it['put'](out_refs[i], sender),
                send_sem=send_sems.at[i * (N_DEV - 1) + k], recv_sem=recv_sems.at[i * (N_DEV - 1) + k],
                device_id=pos, device_id_type=pl.DeviceIdType.MESH)

        for k, pos, flat in peers:
            for i in range(n):
                cp = remote(i, k, pos, flat, me)
                cp.start()
                started.append(cp)
        for k, pos, flat in peers:
            for i in range(n):
                remote(i, k, pos, flat, flat).wait_recv()
        for cp in started[n:]:
            cp.wait_send()
        for cp in started[:n]:
            cp.wait()

    any_spec = pl.BlockSpec(memory_space=pl.ANY)
    return pl.pallas_call(
        body, out_shape=[jax.ShapeDtypeStruct(it['out'], it['src'].dtype) for it in items],
        in_specs=[any_spec] * n, out_specs=[any_spec] * n,
        scratch_shapes=[pltpu.SemaphoreType.DMA((n * (N_DEV - 1),)), pltpu.SemaphoreType.DMA((n * (N_DEV - 1),)),
                        pltpu.SemaphoreType.DMA((n,))],
        name=name)(*[it['src'] for it in items])


def _pack(arrays, dtype, lead=None):
    flat = [a.astype(dtype).reshape((lead, -1) if lead else (-1,)) for a in arrays]
    cat = jnp.concatenate(flat, axis=-1)
    n = cat.shape[-1]
    unit = ADAMW_TILE if n >= ADAMW_TILE * PACK_COLS else 16
    rows = -(-n // (PACK_COLS * unit)) * unit
    pad = rows * PACK_COLS - n
    if pad:
        cat = jnp.pad(cat, ((0, 0), (0, pad)) if lead else ((0, pad),))
    return cat.reshape((lead, rows, PACK_COLS) if lead else (rows, PACK_COLS))


def _unpack(buf, shapes, lead=None):
    flat = buf.reshape((lead, -1) if lead else (-1,))
    out, off = [], 0
    for shp in shapes:
        n = int(np.prod(shp))
        piece = flat[..., off:off + n]
        out.append(piece.reshape(((lead,) if lead else ()) + tuple(shp)))
        off += n
    return out


def _to_slabs(full, axis):
    shp = full.shape
    split = full.reshape(shp[:axis] + (N_DEV, shp[axis] // N_DEV) + shp[axis + 1:])
    return jnp.moveaxis(split, axis, 0)


def _from_slabs(slabs, axis):
    moved = jnp.moveaxis(slabs, 0, axis)
    shp = moved.shape
    return moved.reshape(shp[:axis] + (shp[axis] * shp[axis + 1],) + shp[axis + 2:])


def _adamw_block(r, c):
    if r % 8:
        return r, 256
    br = min(r, max(8, ADAMW_TILE * PACK_COLS // c))
    while r % br:
        br //= 2
    return br, c


def _adamw(name, partials, w, m, v):
    n_lead, r, c = w.shape
    br, bc = _adamw_block(r, c)
    c1 = 1.0 - ADAM_B1 ** ADAM_STEP
    c2 = 1.0 - ADAM_B2 ** ADAM_STEP

    def kern(p_ref, w_ref, m_ref, v_ref, g_ref, d_ref, nm_ref, nv_ref):
        g = p_ref[0].astype(F32)
        for s in range(1, N_DEV):
            g = g + p_ref[s].astype(F32)
        nm = ADAM_B1 * m_ref[...] + (1.0 - ADAM_B1) * g
        nv = ADAM_B2 * v_ref[...] + (1.0 - ADAM_B2) * jnp.square(g)
        g_ref[...] = g
        nm_ref[...] = nm
        nv_ref[...] = nv
        d_ref[...] = -ADAM_LR * ((nm / c1) / (jnp.sqrt(nv / c2) + ADAM_EPS) + ADAM_WD * w_ref[...])

    blk = pl.BlockSpec((None, br, bc), lambda l, i, j: (l, i, j))
    return pl.pallas_call(
        kern, grid=(n_lead, r // br, c // bc),
        in_specs=[pl.BlockSpec((N_DEV, None, br, bc), lambda l, i, j: (0, l, i, j)), blk, blk, blk],
        out_specs=[blk] * 4, out_shape=[jax.ShapeDtypeStruct(w.shape, F32)] * 4,
        compiler_params=pltpu.CompilerParams(dimension_semantics=("arbitrary",) * 3),
        name=name)(partials, w, m, v)


def _as_rows(a, lead):
    return a.reshape(a.shape[:lead] + (-1, a.shape[-1]))


def kernel(x, p, norm1_g, w_in, sg_ln_g, sg_ln_b, sg_w, sg_b, gla_w_a2, gla_b_a, gla_norm_g, att_rel_bias, conv_dw_w, conv_dw_b, conv_ln_g, conv_ln_b, w_branch, w_gate, b_gate, w_out, norm2_g, w_ff1, w_ff2, norm3_g, w_ple_gate, b_ple_gate, w_ple, final_g, loss_target, m_norm1_g, m_w_in, m_sg_ln_g, m_sg_ln_b, m_sg_w, m_sg_b, m_gla_w_a2, m_gla_b_a, m_gla_norm_g, m_att_rel_bias, m_conv_dw_w, m_conv_dw_b, m_conv_ln_g, m_conv_ln_b, m_w_branch, m_w_gate, m_b_gate, m_w_out, m_norm2_g, m_w_ff1, m_w_ff2, m_norm3_g, m_w_ple_gate, m_b_ple_gate, m_w_ple, m_final_g, v_norm1_g, v_w_in, v_sg_ln_g, v_sg_ln_b, v_sg_w, v_sg_b, v_gla_w_a2, v_gla_b_a, v_gla_norm_g, v_att_rel_bias, v_conv_dw_w, v_conv_dw_b, v_conv_ln_g, v_conv_ln_b, v_w_branch, v_w_gate, v_b_gate, v_w_out, v_norm2_g, v_w_ff1, v_w_ff2, v_norm3_g, v_w_ple_gate, v_b_ple_gate, v_w_ple, v_final_g):
    args = locals()
    wts = {n: args[n] for n in WEIGHTS}
    mom = {n: args['m_' + n] for n in WEIGHTS}
    var = {n: args['v_' + n] for n in WEIGHTS}

    local = {d_name: dict(d, w_in=jnp.swapaxes(d['w_in'], 1, 2))
             for d_name, d in (("w", wts), ("m", mom), ("v", var))}
    others = [n for n in MXU_WEIGHTS if n != 'w_in']

    items = [_gather_item(local["w"]['w_in'].astype(_MXU_DTYPE))]
    for n in others:
        shard = wts[n].astype(_MXU_DTYPE)
        ax = SHARD_AXIS[n]
        items.append(_gather_item(shard, shard.shape[:ax] + (N_DEV * shard.shape[ax],) + shard.shape[ax + 1:], ax))
    items.append(_gather_item(_pack([wts[n] for n in VEC_WEIGHTS], F32)))
    got = _exchange("gather_weights", items)
    full = {n: wts[n] for n in REPLICATED}
    full['w_in'] = jnp.moveaxis(got[0], 0, 1).reshape(DEPTH, IN_COLS, D_MODEL)
    for n, a in zip(others, got[1:]):
        full[n] = a
    for n, slabs in zip(VEC_WEIGHTS, _unpack(got[-1], [wts[n].shape for n in VEC_WEIGHTS], lead=N_DEV)):
        full[n] = _from_slabs(slabs, SHARD_AXIS[n])

    loss, grad_x, grads = _local_step(x[0], p[:, 0], loss_target[0], full)
    loss = lax.psum(loss, ("x", "y", "c"))

    items = [_scatter_item(grads['w_in'].reshape(DEPTH, N_DEV, IN_COLS // N_DEV, D_MODEL), lead=1)]
    items += [_scatter_item(grads[n], axis=SHARD_AXIS[n]) for n in others]
    items.append(_scatter_item(_pack([_to_slabs(grads[n], SHARD_AXIS[n]) for n in VEC_WEIGHTS], F32, lead=N_DEV)))
    items.append(_gather_item(_pack([grads[n] for n in REPLICATED], F32)))
    parts = _exchange("exchange_grads", items)

    results = {}
    for n, part in zip(MXU_WEIGHTS, parts):
        w3, m3, v3 = (_as_rows(local[d][n], 1) for d in ("w", "m", "v"))
        outs = _adamw("adamw_" + n, part.reshape((N_DEV,) + w3.shape), w3, m3, v3)
        for kind, a in zip(("grad", "delta", "new_m", "new_v"), outs):
            a = a.reshape(local["w"][n].shape)
            results[kind, n] = jnp.swapaxes(a, 1, 2) if n == 'w_in' else a
    for names, part, call in ((VEC_WEIGHTS, parts[-2], "adamw_vec"), (REPLICATED, parts[-1], "adamw_replicated")):
        packed = [_pack([d[n] for n in names], F32)[None] for d in (wts, mom, var)]
        outs = _adamw(call, part[:, None], *packed)
        for kind, buf in zip(("grad", "delta", "new_m", "new_v"), outs):
            for n, a in zip(names, _unpack(buf[0], [wts[n].shape for n in names])):
                results[kind, n] = a
    return (loss, grad_x[None]) + tuple(results[kind, n] for kind in ("grad", "delta", "new_m", "new_v")
                                        for n in WEIGHTS)
```

```python
import functools

import numpy as np
import jax
import jax.numpy as jnp
from jax import lax
from jax.experimental import pallas as pl
from jax.experimental.pallas import tpu as pltpu

F32 = jnp.float32
_MXU_DTYPE = jnp.bfloat16
GRAD_DTYPE = jnp.bfloat16

N_DEV = 8
D_MODEL = 1024
DEPTH = 2
CHUNK = 64
PLE_DIM = 256
BW = 512
SG_BLOCK = 128
SG_GROUPS = 4
GLA_HEADS = 4
GLA_DK = 64
GLA_DV = 128
GLA_RANK = 16
GLA_TAU = 16.0
ATT_HEADS = 8
ATT_HD = 64
ATT_BAND = 9
BAND = ATT_BAND * CHUNK
MAX_REL = 256
REL_TABLE = CHUNK + MAX_REL
CONV_K = 31
CONV_HALO = 32
D_FF = 4096
EPS = 1e-6
NEG_INF = -1e30

IN_GROUPS = (("A", 0, 1024), ("B", 1024, 1536), ("a", 2560, 16), ("C", 2576, 1536), ("D", 4112, 1024))
IN_COLS = 5136

ADAM_LR = 0.001
ADAM_B1 = 0.9
ADAM_B2 = 0.999
ADAM_EPS = 1e-08
ADAM_WD = 0.01
ADAM_STEP = 10

ADAMW_TILE = 128
PACK_COLS = 1024
VMEM_LIMIT_MB = 56

_NN = (((1,), (0,)), ((), ()))
_NT = (((1,), (1,)), ((), ()))
_TN = (((0,), (0,)), ((), ()))

WEIGHTS = ['norm1_g', 'w_in', 'sg_ln_g', 'sg_ln_b', 'sg_w', 'sg_b', 'gla_w_a2', 'gla_b_a', 'gla_norm_g',
           'att_rel_bias', 'conv_dw_w', 'conv_dw_b', 'conv_ln_g', 'conv_ln_b', 'w_branch', 'w_gate', 'b_gate',
           'w_out', 'norm2_g', 'w_ff1', 'w_ff2', 'norm3_g', 'w_ple_gate', 'b_ple_gate', 'w_ple', 'final_g']
SHARD_AXIS = {'w_in': 2, 'gla_w_a2': 2, 'att_rel_bias': 2, 'conv_dw_w': 2, 'w_branch': 3, 'w_gate': 2,
              'b_gate': 2, 'w_out': 1, 'w_ff1': 2, 'w_ff2': 1, 'w_ple_gate': 1, 'w_ple': 2}
MXU_WEIGHTS = ('w_in', 'w_branch', 'w_gate', 'w_out', 'w_ff1', 'w_ff2', 'w_ple_gate', 'w_ple')
VEC_WEIGHTS = ('gla_w_a2', 'att_rel_bias', 'conv_dw_w', 'b_gate')
SHARDED = tuple(n for n in WEIGHTS if n in SHARD_AXIS)
REPLICATED = tuple(n for n in WEIGHTS if n not in SHARD_AXIS)


def _mm(a, b, dims=_NN):
    return lax.dot_general(a.astype(_MXU_DTYPE), b.astype(_MXU_DTYPE), dims, preferred_element_type=F32)


def _split3(x):
    x1 = x.astype(jnp.bfloat16)
    r1 = x - x1.astype(F32)
    x2 = r1.astype(jnp.bfloat16)
    x3 = (r1 - x2.astype(F32)).astype(jnp.bfloat16)
    return x1, x2, x3


def _mm_exact_rhs(m, x, dims=_NN):
    return sum(lax.dot_general(m, xi, dims, preferred_element_type=F32) for xi in _split3(x))


def _mm_exact_lhs(x, m, dims=_NN):
    return sum(lax.dot_general(xi, m, dims, preferred_element_type=F32) for xi in _split3(x))


def _sigmoid(x):
    return 1.0 / (1.0 + jnp.exp(-x))


def _gelu(x):
    c = 0.7978845608028654
    t = jnp.tanh(c * (x + 0.044715 * x * x * x))
    return 0.5 * x * (1.0 + t), t


def _gelu_grad(x, t):
    c = 0.7978845608028654
    return 0.5 * (1.0 + t) + 0.5 * x * (1.0 - t * t) * c * (1.0 + 3.0 * 0.044715 * x * x)


def _rms_stat(h):
    return lax.rsqrt(jnp.mean(h * h, axis=-1, keepdims=True) + EPS)


def _rms_bwd(dy, h, g, r):
    hh = h * r
    dhh = dy * g
    dh = r * (dhh - hh * jnp.mean(dhh * hh, axis=-1, keepdims=True))
    return dh, jnp.sum(dy * hh, axis=0, keepdims=True)


def _ln_fwd(x, g, b):
    mu = jnp.mean(x, axis=-1, keepdims=True)
    xc = x - mu
    rs = lax.rsqrt(jnp.mean(xc * xc, axis=-1, keepdims=True) + EPS)
    xh = xc * rs
    return xh * g + b, xh, rs


def _ln_bwd(dy, xh, rs, g):
    dxh = dy * g
    dx = rs * (dxh - jnp.mean(dxh, axis=-1, keepdims=True) - xh * jnp.mean(dxh * xh, axis=-1, keepdims=True))
    return dx, jnp.sum(dy * xh, axis=0, keepdims=True), jnp.sum(dy, axis=0, keepdims=True)


def _row_call(name, body, nt, rows=(), halos=(), res=(), outs=(), accs=(), scratch=(), reverse=False, exchange=()):
    def pos(i):
        return (nt - 1 - i) if reverse else i

    def lead(ndim, f):
        return lambda i: (f(pos(i)),) + (0,) * (ndim - 1)

    in_specs, operands = [], []
    for a, tile in rows:
        in_specs.append(pl.BlockSpec((tile,) + a.shape[1:], lead(a.ndim, lambda t: t)))
        operands.append(a)
    for a, blk, per, side in halos:
        last = a.shape[0] // blk - 1
        delta = {'prev2': -2, 'prev': -1, 'next': per, 'next2': per + 1}[side]
        f = lambda t, per=per, last=last, delta=delta: jnp.clip(t * per + delta, 0, last)
        in_specs.append(pl.BlockSpec((blk,) + a.shape[1:], lead(a.ndim, f)))
        operands.append(a)
    for a in res:
        in_specs.append(pl.BlockSpec(a.shape, lambda i, nd=a.ndim: (0,) * nd, pipeline_mode=pl.Buffered(1)))
        operands.append(a)
    out_specs, out_shape = [], []
    for shape, dtype, tile in outs:
        out_specs.append(pl.BlockSpec((tile,) + tuple(shape[1:]), lead(len(shape), lambda t: t)))
        out_shape.append(jax.ShapeDtypeStruct(tuple(shape), dtype))
    for shape in accs:
        out_specs.append(pl.BlockSpec(tuple(shape), lambda i, nd=len(shape): (0,) * nd))
        out_shape.append(jax.ShapeDtypeStruct(tuple(shape), F32))
    nx = len(exchange)
    any_spec = pl.BlockSpec(memory_space=pl.ANY)
    for it in exchange:
        in_specs.append(any_spec)
        operands.append(it['src'])
        out_specs.append(any_spec)
        out_shape.append(jax.ShapeDtypeStruct(it['out'], it['src'].dtype))
    sizes = (len(rows), len(halos), len(res), nx, len(outs), len(accs), nx, len(scratch), 3 if nx else 0)

    def kern(*refs):
        i = pl.program_id(0)
        groups, at = [], 0
        for n in sizes:
            groups.append(refs[at:at + n])
            at += n
        row_refs, halo_refs, res_refs, x_src, out_refs, acc_refs, x_dst, scr_refs, sems = groups

        @pl.when(i == 0)
        def _():
            for r in tuple(acc_refs) + tuple(scr_refs):
                r[...] = jnp.zeros(r.shape, r.dtype)
            if nx:
                _exchange_copies(exchange, x_src, x_dst, sems, start=True)

        body(pos(i), row_refs, halo_refs, res_refs, out_refs, acc_refs, scr_refs)

        if nx:
            @pl.when(i == nt - 1)
            def _():
                _exchange_copies(exchange, x_src, x_dst, sems, start=False)

    result = pl.pallas_call(
        kern, grid=(nt,), in_specs=in_specs, out_specs=out_specs, out_shape=out_shape,
        scratch_shapes=[pltpu.VMEM(tuple(s), d) for s, d in scratch] + (_exchange_sems(nx) if nx else []),
        compiler_params=pltpu.CompilerParams(dimension_semantics=("arbitrary",),
                                             vmem_limit_bytes=VMEM_LIMIT_MB << 20),
        name=name)(*operands)
    return tuple(result)


def _tn_call(name, a, b, k, n, nblk=1, a_col=False, b_col=True, b_off=0, out='cols', tile=1024):
    nt = a.shape[0] // tile
    if out == 'cols':
        o_shape, o_spec = (k, nblk * n), pl.BlockSpec((k, n), lambda j, t: (0, j))
    elif out == 'rows':
        o_shape, o_spec = (nblk * k, n), pl.BlockSpec((k, n), lambda j, t: (j, 0))
    else:
        o_shape, o_spec = (nblk, k, n), pl.BlockSpec((None, k, n), lambda j, t: (j, 0, 0))

    def kern(a_ref, b_ref, o_ref, acc):
        @pl.when(pl.program_id(1) == 0)
        def _():
            acc[...] = jnp.zeros(acc.shape, acc.dtype)

        acc[...] += lax.dot_general(a_ref[...], b_ref[...], _TN, preferred_element_type=F32)

        @pl.when(pl.program_id(1) == nt - 1)
        def _():
            o_ref[...] = acc[...].astype(o_ref.dtype)

    return pl.pallas_call(
        kern, grid=(nblk, nt),
        in_specs=[pl.BlockSpec((tile, k), (lambda j, t: (t, j)) if a_col else (lambda j, t: (t, 0))),
                  pl.BlockSpec((tile, n), (lambda j, t: (t, j + b_off)) if b_col else (lambda j, t: (t, b_off)))],
        out_specs=o_spec, out_shape=jax.ShapeDtypeStruct(o_shape, GRAD_DTYPE),
        scratch_shapes=[pltpu.VMEM((k, n), F32)],
        compiler_params=pltpu.CompilerParams(dimension_semantics=("arbitrary", "arbitrary"),
                                             vmem_limit_bytes=VMEM_LIMIT_MB << 20),
        name=name)(a, b)


def _inproj_fwd(tag, h, g1, w_groups, tm=256, exchange=()):
    t_len = h.shape[0]

    def body(t, rows, halos, res, outs, accs, scr):
        hv = rows[0][...]
        xn = (hv * _rms_stat(hv) * res[0][...]).astype(_MXU_DTYPE)
        outs[0][...] = xn
        for o, w in zip(outs[1:], res[1:]):
            o[...] = lax.dot_general(xn, w[...], _NT, preferred_element_type=F32)

    outs = [((t_len, D_MODEL), _MXU_DTYPE, tm)] + [((t_len, w.shape[0]), F32, tm) for w in w_groups]
    return _row_call("inproj_fwd" + tag, body, t_len // tm, rows=[(h, tm)], res=[g1] + list(w_groups), outs=outs,
                     exchange=exchange)


def _sg_mask():
    row = lax.broadcasted_iota(jnp.int32, (SG_BLOCK, SG_BLOCK), 0)
    col = lax.broadcasted_iota(jnp.int32, (SG_BLOCK, SG_BLOCK), 1)
    return jnp.logical_or(row >= CHUNK, col < CHUNK)


def _sg_forward_parts(pa, lg, lb, w_ref, bt):
    tm = pa.shape[0]
    nb = tm // SG_BLOCK
    su, sv = pa[:, :BW], pa[:, BW:]
    u, tu = _gelu(su)
    gv, tv = _gelu(sv)
    vn, xh, rs = _ln_fwd(gv, lg, lb)
    mask = _sg_mask()
    wms, xs, ms = [], [], []
    for g in range(SG_GROUPS):
        wm = jnp.where(mask, w_ref[g], 0.0).astype(_MXU_DTYPE)
        xg = jnp.concatenate([vn[b * SG_BLOCK:(b + 1) * SG_BLOCK, g * 128:(g + 1) * 128] for b in range(nb)], axis=1)
        xg = xg.astype(_MXU_DTYPE)
        ms.append(lax.dot_general(wm, xg, _NN, preferred_element_type=F32) + bt[:, g:g + 1])
        wms.append(wm)
        xs.append(xg)
    mixed = _sg_unfold(ms, nb)
    return su, sv, u, tu, tv, xh, rs, wms, xs, mixed


def _sg_unfold(per_group, nb):
    return jnp.concatenate(
        [jnp.concatenate([per_group[g][:, b * 128:(b + 1) * 128] for g in range(SG_GROUPS)], axis=1)
         for b in range(nb)], axis=0)


def _sg_fwd(tag, proj_a, lg, lb, sg_w, sg_bt, tm=512):
    t_len = proj_a.shape[0]

    def body(t, rows, halos, res, outs, accs, scr):
        parts = _sg_forward_parts(rows[0][...], res[0][...], res[1][...], res[2], res[3][...])
        outs[0][...] = (parts[2] * parts[-1]).astype(_MXU_DTYPE)

    return _row_call("sg_fwd" + tag, body, t_len // tm, rows=[(proj_a, tm)], res=[lg, lb, sg_w, sg_bt],
                     outs=[((t_len, BW), _MXU_DTYPE, tm)])[0]


def _sg_bwd(tag, proj_a, dy, lg, lb, sg_w, sg_bt, tm=512):
    t_len = proj_a.shape[0]
    nb = tm // SG_BLOCK

    def body(t, rows, halos, res, outs, accs, scr):
        lgv = res[0][...]
        su, sv, u, tu, tv, xh, rs, wms, xs, mixed = _sg_forward_parts(rows[0][...], lgv, res[1][...], res[2], res[3][...])
        dyv = rows[1][...]
        dsu = dyv * mixed * _gelu_grad(su, tu)
        dmixed = dyv * u
        mask = _sg_mask()
        dxs, dbs = [], []
        for g in range(SG_GROUPS):
            dm = jnp.concatenate([dmixed[b * SG_BLOCK:(b + 1) * SG_BLOCK, g * 128:(g + 1) * 128] for b in range(nb)],
                                 axis=1)
            dmb = dm.astype(_MXU_DTYPE)
            dw = lax.dot_general(dmb, xs[g], _NT, preferred_element_type=F32)
            accs[0][g] += jnp.where(mask, dw, 0.0)
            dbs.append(jnp.sum(dm, axis=1, keepdims=True))
            dxs.append(lax.dot_general(wms[g], dmb, _TN, preferred_element_type=F32))
        accs[1][...] += jnp.concatenate(dbs, axis=1)
        dvn = _sg_unfold(dxs, nb)
        dgv, dlg, dlb = _ln_bwd(dvn, xh, rs, lgv)
        accs[2][...] += dlg
        accs[3][...] += dlb
        dsv = dgv * _gelu_grad(sv, tv)
        outs[0][...] = jnp.concatenate([dsu, dsv], axis=1).astype(_MXU_DTYPE)

    return _row_call("sg_bwd" + tag, body, t_len // tm, rows=[(proj_a, tm), (dy, tm)], res=[lg, lb, sg_w, sg_bt],
                     outs=[((t_len, 2 * BW), _MXU_DTYPE, tm)],
                     accs=[(SG_GROUPS, SG_BLOCK, SG_BLOCK), (SG_BLOCK, SG_GROUPS), (1, BW), (1, BW)])


def _tri(lower):
    row = lax.broadcasted_iota(jnp.int32, (CHUNK, CHUNK), 0)
    col = lax.broadcasted_iota(jnp.int32, (CHUNK, CHUNK), 1)
    return ((row >= col) if lower else (row <= col)).astype(jnp.bfloat16)


def _gla_gate(pa, wa2, ba):
    z = _mm(pa, wa2) + ba
    log_a = (jnp.minimum(z, 0.0) - jnp.log(1.0 + jnp.exp(-jnp.abs(z)))) * (1.0 / GLA_TAU)
    return z, log_a


def _gla_chunk_fwd(pb, log_a, j):
    sl = slice(j * CHUNK, (j + 1) * CHUNK)
    cum = _mm_exact_rhs(_tri(True), log_a[sl])
    tot = cum[CHUNK - 1:CHUNK]
    w = jnp.exp(tot - cum)
    k = pb[sl, 256:512]
    return sl, cum, tot, w, k * w, jnp.exp(tot)


def _gla_read(q, s_t):
    qs = (q * (GLA_DK ** -0.5)).astype(_MXU_DTYPE)
    sb = s_t.astype(_MXU_DTYPE)
    o = jnp.concatenate([lax.dot_general(qs[:, h * 64:(h + 1) * 64], sb[:, h * 64:(h + 1) * 64], _NT,
                                         preferred_element_type=F32) for h in range(GLA_HEADS)], axis=1)
    return qs, sb, o


def _gla_fwd(tag, proj_b, proj_a, wa2, ba, ng, tm=256, exchange=()):
    t_len = proj_b.shape[0]
    cpt = tm // CHUNK

    def body(t, rows, halos, res, outs, accs, scr):
        pb = rows[0][...]
        _, log_a = _gla_gate(rows[1][...], res[0][...], res[1][...])
        ngv = res[2][...]
        st = scr[0]
        for j in range(cpt):
            sl, cum, tot, w, kd, dec = _gla_chunk_fwd(pb, log_a, j)
            kdb = kd.astype(_MXU_DTYPE)
            vb = pb[sl, 512:1024].astype(_MXU_DTYPE)
            ut = jnp.concatenate([lax.dot_general(vb[:, h * 128:(h + 1) * 128], kdb[:, h * 64:(h + 1) * 64], _TN,
                                                  preferred_element_type=F32) for h in range(GLA_HEADS)], axis=1)
            s_new = dec * st[...] + ut
            st[...] = s_new
            outs[1][j] = s_new
            _, _, o = _gla_read(pb[sl, 0:256], s_new)
            on = jnp.concatenate(
                [o[:, h * 128:(h + 1) * 128] * lax.rsqrt(jnp.mean(jnp.square(o[:, h * 128:(h + 1) * 128]), axis=-1,
                                                                  keepdims=True) + EPS) for h in range(GLA_HEADS)], axis=1)
            r = pb[sl, 1024:1536]
            outs[0][sl, :] = (on * ngv * (r * _sigmoid(r))).astype(_MXU_DTYPE)

    return _row_call("gla_fwd" + tag, body, t_len // tm, rows=[(proj_b, tm), (proj_a, tm)], res=[wa2, ba, ng],
                     outs=[((t_len, BW), _MXU_DTYPE, tm), ((t_len // CHUNK, GLA_DV, 256), F32, cpt)],
                     scratch=[((GLA_DV, 256), F32)], exchange=exchange)


def _gla_bwd(tag, proj_b, proj_a, dy, states, wa2, ba, ng, tm=256):
    t_len = proj_b.shape[0]
    cpt = tm // CHUNK

    def body(t, rows, halos, res, outs, accs, scr):
        pb = rows[0][...]
        pa = rows[1][...]
        dyv = rows[2][...]
        st_ref = rows[3]
        wa2v = res[0][...]
        z, log_a = _gla_gate(pa, wa2v, res[1][...])
        ngv = res[2][...]
        dst = scr[0]
        s_before_tile = jnp.where(t > 0, halos[0][0], 0.0)
        dz_rows = [None] * cpt
        d_rows = [None] * cpt
        for j in reversed(range(cpt)):
            sl, cum, tot, w, kd, dec = _gla_chunk_fwd(pb, log_a, j)
            s_c = st_ref[j]
            s_prev = st_ref[j - 1] if j > 0 else s_before_tile
            qs, sb, o = _gla_read(pb[sl, 0:256], s_c)
            r = pb[sl, 1024:1536]
            sig = _sigmoid(r)
            sil = r * sig
            dyj = dyv[sl]
            dos, drs, dng = [], [], []
            for h in range(GLA_HEADS):
                hs = slice(h * 128, (h + 1) * 128)
                oh = o[:, hs]
                rstd = lax.rsqrt(jnp.mean(oh * oh, axis=-1, keepdims=True) + EPS)
                on = oh * rstd
                g_h = ngv[:, hs]
                don = dyj[:, hs] * g_h * sil[:, hs]
                dng.append(jnp.sum(dyj[:, hs] * on * sil[:, hs], axis=0, keepdims=True))
                drs.append(dyj[:, hs] * on * g_h * (sig[:, hs] * (1.0 + r[:, hs] * (1.0 - sig[:, hs]))))
                dos.append(rstd * (don - on * jnp.mean(don * on, axis=-1, keepdims=True)))
            accs[2][...] += jnp.concatenate(dng, axis=1)
            dr = jnp.concatenate(drs, axis=1)
            do = jnp.concatenate(dos, axis=1)
            dob = do.astype(_MXU_DTYPE)
            dst_tot = dst[...] + jnp.concatenate(
                [lax.dot_general(dob[:, h * 128:(h + 1) * 128], qs[:, h * 64:(h + 1) * 64], _TN,
                                 preferred_element_type=F32) for h in range(GLA_HEADS)], axis=1)
            dq = jnp.concatenate(
                [lax.dot_general(dob[:, h * 128:(h + 1) * 128], sb[:, h * 64:(h + 1) * 64], _NN,
                                 preferred_element_type=F32) for h in range(GLA_HEADS)], axis=1) * (GLA_DK ** -0.5)
            ddec = jnp.sum(dst_tot * s_prev, axis=0, keepdims=True)
            dst[...] = dec * dst_tot
            dub = dst_tot.astype(_MXU_DTYPE)
            vb = pb[sl, 512:1024].astype(_MXU_DTYPE)
            kdb = kd.astype(_MXU_DTYPE)
            dkd = jnp.concatenate(
                [lax.dot_general(vb[:, h * 128:(h + 1) * 128], dub[:, h * 64:(h + 1) * 64], _NN,
                                 preferred_element_type=F32) for h in range(GLA_HEADS)], axis=1)
            dv = jnp.concatenate(
                [lax.dot_general(kdb[:, h * 64:(h + 1) * 64], dub[:, h * 64:(h + 1) * 64], _NT,
                                 preferred_element_type=F32) for h in range(GLA_HEADS)], axis=1)
            dk = dkd * w
            e = dkd * kd
            dtot = jnp.sum(e, axis=0, keepdims=True) + ddec * dec
            last = lax.broadcasted_iota(jnp.int32, e.shape, 0) == CHUNK - 1
            dcum = jnp.where(last, dtot - e, -e)
            dla = _mm_exact_rhs(_tri(False), dcum)
            dz_rows[j] = dla * (1.0 / GLA_TAU) * _sigmoid(-z[sl])
            d_rows[j] = jnp.concatenate([dq, dk, dv, dr], axis=1)
        dz = jnp.concatenate(dz_rows, axis=0)
        dzb = dz.astype(_MXU_DTYPE)
        outs[0][...] = jnp.concatenate(d_rows, axis=0).astype(_MXU_DTYPE)
        outs[1][...] = lax.dot_general(dzb, wa2v.astype(_MXU_DTYPE), _NT, preferred_element_type=F32).astype(_MXU_DTYPE)
        accs[0][...] += lax.dot_general(pa.astype(_MXU_DTYPE), dzb, _TN, preferred_element_type=F32)
        accs[1][...] += jnp.sum(dz, axis=0, keepdims=True)

    return _row_call("gla_bwd" + tag, body, t_len // tm,
                     rows=[(proj_b, tm), (proj_a, tm), (dy, tm), (states, cpt)],
                     halos=[(states, 1, cpt, 'prev')], res=[wa2, ba, ng],
                     outs=[((t_len, 1536), _MXU_DTYPE, tm), ((t_len, GLA_RANK), _MXU_DTYPE, tm)],
                     accs=[(GLA_RANK, 256), (1, 256), (1, BW)], scratch=[((GLA_DV, 256), F32)], reverse=True)


ATT_TM = 256
ATT_KEYS = ATT_TM + (ATT_BAND - 1) * CHUNK


def _rel_index():
    l_idx = np.arange(CHUNK)[:, None]
    m_idx = np.arange(BAND)[None, :]
    rel = l_idx + (ATT_BAND - 1) * CHUNK - m_idx
    return jnp.asarray((np.clip(rel, -(CHUNK - 1), MAX_REL) + (CHUNK - 1)).reshape(1, CHUNK * BAND), jnp.int32)


BIAS_COLS = 4096


def _bias_expand(tag, rel_bias):
    n = CHUNK * BAND

    def kern(rel_ref, idx_ref, o_ref):
        onehot = (lax.broadcasted_iota(jnp.int32, (REL_TABLE, BIAS_COLS), 0) == idx_ref[...]).astype(jnp.bfloat16)
        o_ref[...] = _mm_exact_lhs(rel_ref[...], onehot)

    return pl.pallas_call(
        kern, grid=(n // BIAS_COLS,),
        in_specs=[pl.BlockSpec((ATT_HEADS, REL_TABLE), lambda i: (0, 0)), pl.BlockSpec((1, BIAS_COLS), lambda i: (0, i))],
        out_specs=pl.BlockSpec((ATT_HEADS, BIAS_COLS), lambda i: (0, i)),
        out_shape=jax.ShapeDtypeStruct((ATT_HEADS, n), F32), name="bias_expand" + tag)(rel_bias, _rel_index())


def _bias_tile(tag, bias):
    per = ATT_TM // CHUNK

    def kern(b_ref, o_ref):
        bv = b_ref[...]
        for j in range(per):
            parts = [jnp.full((CHUNK, j * CHUNK), NEG_INF, F32)] if j else []
            parts.append(bv)
            if j < per - 1:
                parts.append(jnp.full((CHUNK, (per - 1 - j) * CHUNK), NEG_INF, F32))
            o_ref[j * CHUNK:(j + 1) * CHUNK, :] = jnp.concatenate(parts, axis=1)

    return pl.pallas_call(
        kern, grid=(ATT_HEADS,), in_specs=[pl.BlockSpec((None, CHUNK, BAND), lambda h: (h, 0, 0))],
        out_specs=pl.BlockSpec((None, ATT_TM, ATT_KEYS), lambda h: (h, 0, 0)),
        out_shape=jax.ShapeDtypeStruct((ATT_HEADS, ATT_TM, ATT_KEYS), F32), name="bias_tile" + tag)(bias)


def _bias_untile(tag, dbias):
    per = ATT_TM // CHUNK

    def kern(d_ref, o_ref):
        acc = d_ref[0:CHUNK, 0:BAND]
        for j in range(1, per):
            acc = acc + d_ref[j * CHUNK:(j + 1) * CHUNK, j * CHUNK:j * CHUNK + BAND]
        o_ref[...] = acc

    return pl.pallas_call(
        kern, grid=(ATT_HEADS,), in_specs=[pl.BlockSpec((None, ATT_TM, ATT_KEYS), lambda h: (h, 0, 0))],
        out_specs=pl.BlockSpec((None, CHUNK, BAND), lambda h: (h, 0, 0)),
        out_shape=jax.ShapeDtypeStruct((ATT_HEADS, CHUNK, BAND), F32), name="bias_untile" + tag)(dbias)


def _bias_reduce(tag, dbias):
    n = CHUNK * BAND

    def kern(db_ref, idx_ref, o_ref):
        @pl.when(pl.program_id(0) == 0)
        def _():
            o_ref[...] = jnp.zeros(o_ref.shape, o_ref.dtype)

        onehot = (lax.broadcasted_iota(jnp.int32, (REL_TABLE, BIAS_COLS), 0) == idx_ref[...]).astype(jnp.bfloat16)
        o_ref[...] += _mm_exact_lhs(db_ref[...], onehot, _NT)

    return pl.pallas_call(
        kern, grid=(n // BIAS_COLS,),
        in_specs=[pl.BlockSpec((ATT_HEADS, BIAS_COLS), lambda i: (0, i)), pl.BlockSpec((1, BIAS_COLS), lambda i: (0, i))],
        out_specs=pl.BlockSpec((ATT_HEADS, REL_TABLE), lambda i: (0, 0)),
        out_shape=jax.ShapeDtypeStruct((ATT_HEADS, REL_TABLE), F32),
        compiler_params=pltpu.CompilerParams(dimension_semantics=("arbitrary",)),
        name="bias_reduce" + tag)(dbias, _rel_index())


def _attn_stage(t, pc_ref, p1_ref, p2_ref, kv):
    tm = ATT_TM
    kv[0:tm, :] = jnp.where(t > 1, p2_ref[:, 512:1536], 0.0).astype(kv.dtype)
    kv[tm:2 * tm, :] = jnp.where(t > 0, p1_ref[:, 512:1536], 0.0).astype(kv.dtype)
    kv[2 * tm:, :] = pc_ref[:, 512:1536].astype(kv.dtype)
    q = (pc_ref[:, 0:512] * (ATT_HD ** -0.5)).astype(_MXU_DTYPE)
    ok = lax.broadcasted_iota(jnp.int32, (tm, ATT_KEYS), 1) >= (2 - t) * tm
    return q, ok


def _attn_probs(q, kv, bias_h, ok, h):
    hs = slice(h * ATT_HD, (h + 1) * ATT_HD)
    s = lax.dot_general(q[:, hs], kv[:, hs], _NT, preferred_element_type=F32) + bias_h
    s = jnp.where(ok, s, NEG_INF)
    e = jnp.exp(s - jnp.max(s, axis=-1, keepdims=True))
    return e * (1.0 / jnp.sum(e, axis=-1, keepdims=True))


def _attn_halos(proj_c):
    return [(proj_c, ATT_TM, 1, 'prev'), (proj_c, ATT_TM, 1, 'prev2')]


def _attn_fwd(tag, proj_c, bias, exchange=()):
    t_len = proj_c.shape[0]
    tm = ATT_TM

    def body(t, rows, halos, res, outs, accs, scr):
        b_ref, kv = res[0], scr[0]
        q, ok = _attn_stage(t, rows[0], halos[0], halos[1], kv)
        o = [lax.dot_general(_attn_probs(q, kv, b_ref[h], ok, h).astype(_MXU_DTYPE),
                             kv[:, BW + h * ATT_HD:BW + (h + 1) * ATT_HD], _NN, preferred_element_type=F32)
             for h in range(ATT_HEADS)]
        outs[0][...] = jnp.concatenate(o, axis=1).astype(_MXU_DTYPE)

    return _row_call("attn_fwd" + tag, body, t_len // tm, rows=[(proj_c, tm)], halos=_attn_halos(proj_c),
                     res=[bias], outs=[((t_len, BW), _MXU_DTYPE, tm)], scratch=[((ATT_KEYS, 1024), _MXU_DTYPE)],
                     exchange=exchange)


def _attn_bwd(tag, proj_c, dy, bias, exchange=()):
    t_len = proj_c.shape[0]
    tm = ATT_TM
    scale = ATT_HD ** -0.5

    def body(t, rows, halos, res, outs, accs, scr):
        b_ref, kv = res[0], scr[0]
        q, ok = _attn_stage(t, rows[0], halos[0], halos[1], kv)
        do = rows[1][...].astype(_MXU_DTYPE)
        dqs, dks, dvs = [], [], []
        for h in range(ATT_HEADS):
            hs = slice(h * ATT_HD, (h + 1) * ATT_HD)
            vs = slice(BW + h * ATT_HD, BW + (h + 1) * ATT_HD)
            p = _attn_probs(q, kv, b_ref[h], ok, h)
            dp = lax.dot_general(do[:, hs], kv[:, vs], _NT, preferred_element_type=F32)
            ds = p * (dp - jnp.sum(dp * p, axis=-1, keepdims=True))
            accs[0][h] += ds
            dsb = ds.astype(_MXU_DTYPE)
            dqs.append(lax.dot_general(dsb, kv[:, hs], _NN, preferred_element_type=F32) * scale)
            dks.append(lax.dot_general(dsb, q[:, hs], _TN, preferred_element_type=F32))
            dvs.append(lax.dot_general(p.astype(_MXU_DTYPE), do[:, hs], _TN, preferred_element_type=F32))
        outs[0][...] = jnp.concatenate(dqs, axis=1).astype(_MXU_DTYPE)
        dkv = jnp.concatenate(dks + dvs, axis=1)
        outs[1][...] = dkv[2 * tm:, :]
        outs[2][...] = dkv[tm:2 * tm, :]
        outs[3][...] = dkv[0:tm, :]

    return _row_call("attn_bwd" + tag, body, t_len // tm, rows=[(proj_c, tm), (dy, tm)],
                     halos=_attn_halos(proj_c), res=[bias],
                     outs=[((t_len, BW), _MXU_DTYPE, tm)] + [((t_len, 1024), F32, tm)] * 3,
                     accs=[(ATT_HEADS, ATT_TM, ATT_KEYS)], scratch=[((ATT_KEYS, 1024), _MXU_DTYPE)],
                     exchange=exchange)


def _attn_combine(tag, dq, dkv_own, dkv_prev, dkv_prev2):
    t_len = dq.shape[0]
    tm = ATT_TM
    nt = t_len // tm

    def body(t, rows, halos, res, outs, accs, scr):
        dkv = (rows[1][...] + jnp.where(t < nt - 1, halos[0][...], 0.0)) + jnp.where(t < nt - 2, halos[1][...], 0.0)
        outs[0][...] = jnp.concatenate([rows[0][...], dkv.astype(_MXU_DTYPE)], axis=1)

    return _row_call("attn_combine" + tag, body, nt, rows=[(dq, tm), (dkv_own, tm)],
                     halos=[(dkv_prev, tm, 1, 'next'), (dkv_prev2, tm, 1, 'next2')],
                     outs=[((t_len, 1536), _MXU_DTYPE, tm)])[0]


def _conv_glu(pd):
    a, g = pd[:, :BW], pd[:, BW:]
    sig = _sigmoid(g)
    return a, sig, a * sig


def _conv_stage(t, pd_ref, ph_ref, win):
    pd = pd_ref[...]
    a, sig, y0 = _conv_glu(pd)
    win[0:CONV_HALO, :] = jnp.where(t > 0, _conv_glu(ph_ref[...])[2], 0.0)
    win[CONV_HALO:CONV_HALO + pd.shape[0], :] = y0
    return a, sig


SUBLANES = 8


def _conv_shifted(win, sh):
    for b in range(SUBLANES):
        sh[b] = win[pl.ds(b, sh.shape[1]), :]


def _conv_read(sh, r0, offset, rows):
    rem = offset % SUBLANES
    return sh[rem, pl.ds(pl.multiple_of(r0 + (offset - rem), SUBLANES), rows), :]


def _conv_tap_sum(sh, w_ref, offsets, out_ref, init=None, rb=32):
    def block(i, carry):
        r0 = pl.multiple_of(i * rb, rb)
        acc = jnp.zeros((rb, BW), F32) if init is None else jnp.broadcast_to(init, (rb, BW))
        for j, o in enumerate(offsets):
            acc = acc + w_ref[j:j + 1, :] * _conv_read(sh, r0, o, rb)
        out_ref[pl.ds(r0, rb), :] = acc
        return carry

    lax.fori_loop(0, out_ref.shape[0] // rb, block, 0)


def _conv_tap_corr(sh, d_ref, offsets, acc_ref, group=8):
    for g0 in range(0, len(offsets), group):
        offs = offsets[g0:g0 + group]

        def block(i, sums, offs=offs):
            r0 = pl.multiple_of(i * SUBLANES, SUBLANES)
            d = d_ref[pl.ds(r0, SUBLANES), :]
            return tuple(s + d * _conv_read(sh, r0, o, SUBLANES) for s, o in zip(sums, offs))

        sums = lax.fori_loop(0, d_ref.shape[0] // SUBLANES, block,
                             tuple(jnp.zeros((SUBLANES, BW), F32) for _ in offs), unroll=4)
        for j, s in enumerate(sums):
            acc_ref[g0 + j:g0 + j + 1, :] += jnp.sum(s, axis=0, keepdims=True)


def _conv_scratch(tm):
    return [((tm + CONV_HALO + SUBLANES, BW), F32), ((SUBLANES, tm + CONV_HALO, BW), F32)]


def _conv_fwd(tag, proj_d, dw_w, dw_b, ln_g, ln_b, tm=512):
    t_len = proj_d.shape[0]
    lead = CONV_HALO - (CONV_K - 1)

    def body(t, rows, halos, res, outs, accs, scr):
        win, sh = scr
        _conv_stage(t, rows[0], halos[0], win)
        _conv_shifted(win, sh)
        _conv_tap_sum(sh, res[0], [lead + j for j in range(CONV_K)], outs[1], init=res[1][...])
        yl, _, _ = _ln_fwd(outs[1][...], res[2][...], res[3][...])
        outs[0][...] = (yl * _sigmoid(yl)).astype(_MXU_DTYPE)

    return _row_call("conv_fwd" + tag, body, t_len // tm, rows=[(proj_d, tm)],
                     halos=[(proj_d, CONV_HALO, tm // CONV_HALO, 'prev')], res=[dw_w, dw_b, ln_g, ln_b],
                     outs=[((t_len, BW), _MXU_DTYPE, tm), ((t_len, BW), F32, tm)], scratch=_conv_scratch(tm))


def _conv_bwd_norm(tag, yc, dy, ln_g, ln_b, tm=512):
    t_len = yc.shape[0]

    def body(t, rows, halos, res, outs, accs, scr):
        lgv = res[0][...]
        yl, xh, rs = _ln_fwd(rows[0][...], lgv, res[1][...])
        sig = _sigmoid(yl)
        dyl = rows[1][...] * (sig * (1.0 + yl * (1.0 - sig)))
        dyc, dlg, dlb = _ln_bwd(dyl, xh, rs, lgv)
        outs[0][...] = dyc
        accs[0][...] += dlg
        accs[1][...] += dlb
        accs[2][...] += jnp.sum(dyc, axis=0, keepdims=True)

    return _row_call("conv_bwd_norm" + tag, body, t_len // tm, rows=[(yc, tm), (dy, tm)], res=[ln_g, ln_b],
                     outs=[((t_len, BW), F32, tm)], accs=[(1, BW), (1, BW), (1, BW)])


def _conv_bwd_taps(tag, proj_d, dyc, dw_w, tm=512):
    t_len = proj_d.shape[0]
    nt = t_len // tm
    lead = CONV_HALO - (CONV_K - 1)

    def body(t, rows, halos, res, outs, accs, scr):
        win, sh, wd, shd, dy0_ref = scr
        a, sig = _conv_stage(t, rows[0], halos[0], win)
        _conv_shifted(win, sh)
        wd[0:tm, :] = rows[1][...]
        wd[tm:tm + CONV_HALO, :] = jnp.where(t < nt - 1, halos[1][...], 0.0)
        _conv_shifted(wd, shd)
        _conv_tap_corr(sh, rows[1], [lead + j for j in range(CONV_K)], accs[0])
        _conv_tap_sum(shd, res[0], [CONV_K - 1 - j for j in range(CONV_K)], dy0_ref)
        dy0 = dy0_ref[...]
        outs[0][...] = jnp.concatenate([dy0 * sig, dy0 * a * sig * (1.0 - sig)], axis=1).astype(_MXU_DTYPE)

    return _row_call("conv_bwd_taps" + tag, body, nt, rows=[(proj_d, tm), (dyc, tm)],
                     halos=[(proj_d, CONV_HALO, tm // CONV_HALO, 'prev'), (dyc, CONV_HALO, tm // CONV_HALO, 'next')],
                     res=[dw_w], outs=[((t_len, 2 * BW), _MXU_DTYPE, tm)], accs=[(CONV_K, BW)],
                     scratch=_conv_scratch(tm) + _conv_scratch(tm) + [((tm, BW), F32)])


def _merge_fwd(tag, h, xn, ys, w_gate, b_gate, w_branch, w_out, tm=256):
    t_len = h.shape[0]

    def body(t, rows, halos, res, outs, accs, scr):
        xnv = rows[1][...]
        wg_ref, bg_ref, wb_ref, wo_ref = res
        merged = jnp.zeros((tm, D_MODEL), F32)
        for n in range(4):
            cs = slice(n * D_MODEL, (n + 1) * D_MODEL)
            gate = _sigmoid(lax.dot_general(xnv, wg_ref[n], _NN, preferred_element_type=F32) + bg_ref[n:n + 1, :])
            bo = lax.dot_general(rows[2 + n][...], wb_ref[n], _NN, preferred_element_type=F32)
            outs[0][:, cs] = gate
            outs[1][:, cs] = bo.astype(_MXU_DTYPE)
            merged = merged + gate * bo
        mb = merged.astype(_MXU_DTYPE)
        outs[2][...] = mb
        outs[3][...] = rows[0][...] + lax.dot_general(mb, wo_ref[...], _NN, preferred_element_type=F32)

    return _row_call("merge_fwd" + tag, body, t_len // tm, rows=[(h, tm), (xn, tm)] + [(y, tm) for y in ys],
                     res=[w_gate, b_gate, w_branch, w_out],
                     outs=[((t_len, 4 * D_MODEL), F32, tm), ((t_len, 4 * D_MODEL), _MXU_DTYPE, tm),
                           ((t_len, D_MODEL), _MXU_DTYPE, tm), ((t_len, D_MODEL), F32, tm)])


def _merge_bwd(tag, dh, gate, bo, w_gate, w_branch, w_out, tm=256):
    t_len = dh.shape[0]

    def body(t, rows, halos, res, outs, accs, scr):
        wg_ref, wb_ref, wo_ref = res
        dhb = rows[0][...].astype(_MXU_DTYPE)
        outs[0][...] = dhb
        dmerged = lax.dot_general(dhb, wo_ref[...], _NT, preferred_element_type=F32)
        dxn = jnp.zeros((tm, D_MODEL), F32)
        dbg = []
        for n in range(4):
            cs = slice(n * D_MODEL, (n + 1) * D_MODEL)
            g = rows[1][:, cs]
            dbo = (dmerged * g).astype(_MXU_DTYPE)
            dgp = dmerged * rows[2][:, cs].astype(F32) * (g * (1.0 - g))
            dgb = dgp.astype(_MXU_DTYPE)
            outs[1][:, cs] = dbo
            outs[2][:, cs] = dgb
            outs[4 + n][...] = lax.dot_general(dbo, wb_ref[n], _NT, preferred_element_type=F32)
            dxn = dxn + lax.dot_general(dgb, wg_ref[n], _NT, preferred_element_type=F32)
            dbg.append(jnp.sum(dgp, axis=0, keepdims=True))
        outs[3][...] = dxn
        accs[0][...] += jnp.concatenate(dbg, axis=1)

    return _row_call("merge_bwd" + tag, body, t_len // tm, rows=[(dh, tm), (gate, tm), (bo, tm)],
                     res=[w_gate, w_branch, w_out],
                     outs=[((t_len, D_MODEL), _MXU_DTYPE, tm), ((t_len, 4 * D_MODEL), _MXU_DTYPE, tm),
                           ((t_len, 4 * D_MODEL), _MXU_DTYPE, tm), ((t_len, D_MODEL), F32, tm)]
                     + [((t_len, BW), F32, tm)] * 4,
                     accs=[(1, 4 * D_MODEL)])


FF_COLS = 1024


def _ffn_fwd(tag, h, g2, w1, w2, tm=256):
    t_len = h.shape[0]

    def body(t, rows, halos, res, outs, accs, scr):
        hv = rows[0][...]
        hn = (hv * _rms_stat(hv) * res[0][...]).astype(_MXU_DTYPE)
        outs[0][...] = hn
        acc = hv
        for c in range(D_FF // FF_COLS):
            cs = slice(c * FF_COLS, (c + 1) * FF_COLS)
            pre = lax.dot_general(hn, res[1][:, cs], _NN, preferred_element_type=F32)
            outs[1][:, cs] = pre
            ff = jnp.square(jnp.maximum(pre, 0.0)).astype(_MXU_DTYPE)
            acc = acc + lax.dot_general(ff, res[2][cs, :], _NN, preferred_element_type=F32)
        outs[2][...] = acc

    return _row_call("ffn_fwd" + tag, body, t_len // tm, rows=[(h, tm)], res=[g2, w1, w2],
                     outs=[((t_len, D_MODEL), _MXU_DTYPE, tm), ((t_len, D_FF), F32, tm), ((t_len, D_MODEL), F32, tm)])


def _ffn_bwd(tag, dh, h, pre, g2, w1, w2, tm=256, exchange=()):
    t_len = dh.shape[0]

    def body(t, rows, halos, res, outs, accs, scr):
        dhv = rows[0][...]
        hv = rows[1][...]
        dhb = dhv.astype(_MXU_DTYPE)
        outs[0][...] = dhb
        dhn = jnp.zeros((tm, D_MODEL), F32)
        for c in range(D_FF // FF_COLS):
            cs = slice(c * FF_COLS, (c + 1) * FF_COLS)
            r = jnp.maximum(rows[2][:, cs], 0.0)
            outs[1][:, cs] = (r * r).astype(_MXU_DTYPE)
            dpre = (lax.dot_general(dhb, res[2][cs, :], _NT, preferred_element_type=F32) * (2.0 * r)).astype(_MXU_DTYPE)
            outs[2][:, cs] = dpre
            dhn = dhn + lax.dot_general(dpre, res[1][:, cs], _NT, preferred_element_type=F32)
        dres, dg = _rms_bwd(dhn, hv, res[0][...], _rms_stat(hv))
        outs[3][...] = dhv + dres
        accs[0][...] += dg

    return _row_call("ffn_bwd" + tag, body, t_len // tm, rows=[(dh, tm), (h, tm), (pre, tm)], res=[g2, w1, w2],
                     outs=[((t_len, D_MODEL), _MXU_DTYPE, tm), ((t_len, D_FF), _MXU_DTYPE, tm),
                           ((t_len, D_FF), _MXU_DTYPE, tm), ((t_len, D_MODEL), F32, tm)],
                     accs=[(1, D_MODEL)], exchange=exchange)


def _ple_fwd(tag, h, p, g3, w_pg, b_pg, w_ple, tm=256):
    t_len = h.shape[0]

    def body(t, rows, halos, res, outs, accs, scr):
        hv = rows[0][...]
        hg = (hv * _rms_stat(hv) * res[0][...]).astype(_MXU_DTYPE)
        pb = rows[1][...].astype(_MXU_DTYPE)
        pg = _sigmoid(lax.dot_general(hg, res[1][...], _NN, preferred_element_type=F32) + res[2][...])
        pe = lax.dot_general(pb, res[3][...], _NN, preferred_element_type=F32)
        outs[0][...] = hg
        outs[1][...] = pb
        outs[2][...] = pg
        outs[3][...] = pe
        outs[4][...] = hv + pg * pe

    return _row_call("ple_fwd" + tag, body, t_len // tm, rows=[(h, tm), (p, tm)], res=[g3, w_pg, b_pg, w_ple],
                     outs=[((t_len, D_MODEL), _MXU_DTYPE, tm), ((t_len, PLE_DIM), _MXU_DTYPE, tm),
                           ((t_len, D_MODEL), F32, tm), ((t_len, D_MODEL), F32, tm), ((t_len, D_MODEL), F32, tm)])


def _ple_bwd(tag, dh, h, pg, pe, g3, w_pg, tm=256):
    t_len = dh.shape[0]

    def body(t, rows, halos, res, outs, accs, scr):
        dhv = rows[0][...]
        hv = rows[1][...]
        pgv = rows[2][...]
        dgp = dhv * rows[3][...] * (pgv * (1.0 - pgv))
        dgb = dgp.astype(_MXU_DTYPE)
        outs[0][...] = dgb
        outs[1][...] = (dhv * pgv).astype(_MXU_DTYPE)
        dhg = lax.dot_general(dgb, res[1][...], _NT, preferred_element_type=F32)
        dres, dg = _rms_bwd(dhg, hv, res[0][...], _rms_stat(hv))
        outs[2][...] = dhv + dres
        accs[0][...] += jnp.sum(dgp, axis=0, keepdims=True)
        accs[1][...] += dg

    return _row_call("ple_bwd" + tag, body, t_len // tm, rows=[(dh, tm), (h, tm), (pg, tm), (pe, tm)],
                     res=[g3, w_pg],
                     outs=[((t_len, D_MODEL), _MXU_DTYPE, tm), ((t_len, D_MODEL), _MXU_DTYPE, tm),
                           ((t_len, D_MODEL), F32, tm)],
                     accs=[(1, D_MODEL), (1, D_MODEL)])


def _inproj_bwd(tag, dh, h, dxn_gate, dprojs, g1, w_groups, tm=256):
    t_len = dh.shape[0]

    def body(t, rows, halos, res, outs, accs, scr):
        hv = rows[1][...]
        dxn = rows[2][...]
        for dp, w in zip(rows[3:], res[1:]):
            dxn = dxn + lax.dot_general(dp[...], w[...], _NN, preferred_element_type=F32)
        dres, dg = _rms_bwd(dxn, hv, res[0][...], _rms_stat(hv))
        outs[0][...] = rows[0][...] + dres
        accs[0][...] += dg

    return _row_call("inproj_bwd" + tag, body, t_len // tm,
                     rows=[(dh, tm), (h, tm), (dxn_gate, tm)] + [(d, tm) for d in dprojs],
                     res=[g1] + list(w_groups), outs=[((t_len, D_MODEL), F32, tm)], accs=[(1, D_MODEL)])


def _loss_head(h, target, gf, tm=512):
    t_len = h.shape[0]

    def body(t, rows, halos, res, outs, accs, scr):
        hv = rows[0][...]
        g = res[0][...]
        r = _rms_stat(hv)
        diff = hv * r * g - rows[1][...]
        accs[0][...] += 0.5 * jnp.sum(jnp.mean(diff * diff, axis=-1, keepdims=True), axis=0, keepdims=True)
        dh, dg = _rms_bwd(diff * (1.0 / D_MODEL), hv, g, r)
        outs[0][...] = dh
        accs[1][...] += dg

    return _row_call("loss_head", body, t_len // tm, rows=[(h, tm), (target, tm)], res=[gf],
                     outs=[((t_len, D_MODEL), F32, tm)], accs=[(1, 128), (1, D_MODEL)])


def _row(v):
    return v.reshape(1, -1)


GATHER_DURING = (('inproj', ('w_gate', 'w_branch', 'w_out')), ('gla', ('w_ff1',)), ('attn', ('w_ff2', 'w_ple_gate', 'w_ple')))
SCATTER_DURING_ATTN = ('w_ple_gate', 'w_ple', 'w_ff1', 'w_ff2', 'w_out', 'w_gate', 'w_branch')


def _gather_items(shards, names):
    items = []
    for n in names:
        s = shards[n]
        ax = SHARD_AXIS[n] - 1
        if n == 'w_in':
            items.append(_gather_item(s))
        else:
            items.append(_gather_item(s, s.shape[:ax] + (N_DEV * s.shape[ax],) + s.shape[ax + 1:], ax))
    return items


def _land(w, names, arrays):
    for n, a in zip(names, arrays):
        w[n] = a.reshape(IN_COLS, D_MODEL) if n == 'w_in' else a


def _layer_fwd(i, h, p_i, w, shards, next_shards):
    tag = "_l%d" % i
    during = dict(GATHER_DURING)
    win = [w['w_in'][s:s + n] for _, s, n in IN_GROUPS]
    res = _inproj_fwd(tag, h, _row(w['norm1_g']), win, exchange=_gather_items(shards, during['inproj']))
    xn, pa, pb, pr, pc, pd = res[:6]
    _land(w, during['inproj'], res[6:])
    sg_bt = w['sg_b'].T
    y_a = _sg_fwd(tag, pa, _row(w['sg_ln_g']), _row(w['sg_ln_b']), w['sg_w'], sg_bt)
    res = _gla_fwd(tag, pb, pr, w['gla_w_a2'], _row(w['gla_b_a']), _row(w['gla_norm_g']),
                   exchange=_gather_items(shards, during['gla']))
    y_b, states = res[:2]
    _land(w, during['gla'], res[2:])
    bias = _bias_tile(tag, _bias_expand(tag, w['att_rel_bias']).reshape(ATT_HEADS, CHUNK, BAND))
    items = _gather_items(shards, during['attn']) + (_gather_items(next_shards, ['w_in']) if next_shards else [])
    res = _attn_fwd(tag, pc, bias, exchange=items)
    y_c = res[0]
    _land(w, during['attn'], res[1:1 + len(during['attn'])])
    next_w_in = res[-1].reshape(IN_COLS, D_MODEL) if next_shards else None
    y_d, yc = _conv_fwd(tag, pd, w['conv_dw_w'], _row(w['conv_dw_b']), _row(w['conv_ln_g']), _row(w['conv_ln_b']))
    ys = (y_a, y_b, y_c, y_d)
    gate, bo, merged, h1 = _merge_fwd(tag, h, xn, ys, w['w_gate'], w['b_gate'], w['w_branch'], w['w_out'])
    hn, pre, h2 = _ffn_fwd(tag, h1, _row(w['norm2_g']), w['w_ff1'], w['w_ff2'])
    hg, p_b, pg, pe, h3 = _ple_fwd(tag, h2, p_i, _row(w['norm3_g']), w['w_ple_gate'], _row(w['b_ple_gate']), w['w_ple'])
    saved = dict(h=h, xn=xn, pa=pa, pb=pb, pr=pr, pc=pc, pd=pd, states=states, bias=bias, yc=yc, ys=ys, gate=gate,
                 bo=bo, merged=merged, h1=h1, hn=hn, pre=pre, h2=h2, hg=hg, p_b=p_b, pg=pg, pe=pe, win=win,
                 sg_bt=sg_bt)
    return h3, saved, next_w_in


def _layer_bwd(i, dh3, s, w, carried):
    tag = "_l%d" % i
    g = {}
    dgp, dpe, dh2, db_pg, dg3 = _ple_bwd(tag, dh3, s['h2'], s['pg'], s['pe'], _row(w['norm3_g']), w['w_ple_gate'])
    g['b_ple_gate'], g['norm3_g'] = db_pg[0], dg3[0]
    g['w_ple_gate'] = _tn_call("dw_ple_gate" + tag, s['hg'], dgp, D_MODEL, D_MODEL)
    g['w_ple'] = _tn_call("dw_ple" + tag, s['p_b'], dpe, PLE_DIM, D_MODEL)

    res = _ffn_bwd(tag, dh2, s['h1'], s['pre'], _row(w['norm2_g']), w['w_ff1'], w['w_ff2'], exchange=carried)
    dh2b, ffb, dpre, dh1, dg2 = res[:5]
    carried_out = res[5:]
    g['norm2_g'] = dg2[0]
    g['w_ff1'] = _tn_call("dw_ff1" + tag, s['hn'], dpre, D_MODEL, FF_COLS, nblk=D_FF // FF_COLS)
    g['w_ff2'] = _tn_call("dw_ff2" + tag, ffb, dh2b, FF_COLS, D_MODEL, nblk=D_FF // FF_COLS, a_col=True, b_col=False,
                          out='rows')

    dh1b, dbo, dgpre, dxn_gate, dy_a, dy_b, dy_c, dy_d, db_gate = _merge_bwd(
        tag, dh1, s['gate'], s['bo'], w['w_gate'], w['w_branch'], w['w_out'])
    g['b_gate'] = db_gate.reshape(4, D_MODEL)
    g['w_out'] = _tn_call("dw_out" + tag, s['merged'], dh1b, D_MODEL, D_MODEL)
    g['w_gate'] = _tn_call("dw_gate" + tag, s['xn'], dgpre, D_MODEL, D_MODEL, nblk=4, out='stack')
    g['w_branch'] = jnp.stack([_tn_call("dw_branch%d%s" % (n, tag), s['ys'][n], dbo, BW, D_MODEL, b_off=n)
                             for n in range(4)])

    lg, lb = _row(w['sg_ln_g']), _row(w['sg_ln_b'])
    dpa, dsg_w, dsg_bt, dlg, dlb = _sg_bwd(tag, s['pa'], dy_a, lg, lb, w['sg_w'], s['sg_bt'])
    g['sg_w'], g['sg_b'], g['sg_ln_g'], g['sg_ln_b'] = dsg_w, dsg_bt.T, dlg[0], dlb[0]

    dpb, dpr, dwa2, dba, dng = _gla_bwd(tag, s['pb'], s['pr'], dy_b, s['states'], w['gla_w_a2'],
                                        _row(w['gla_b_a']), _row(w['gla_norm_g']))
    g['gla_w_a2'], g['gla_b_a'], g['gla_norm_g'] = dwa2, dba[0], dng[0]

    items = [_scatter_item(g.pop(n), axis=SHARD_AXIS[n] - 1) for n in SCATTER_DURING_ATTN]
    res = _attn_bwd(tag, s['pc'], dy_c, s['bias'], exchange=items)
    dq, dkv_own, dkv_prev, dkv_prev2, dbias = res[:5]
    parts = dict(zip(SCATTER_DURING_ATTN, res[5:]))
    dpc = _attn_combine(tag, dq, dkv_own, dkv_prev, dkv_prev2)
    g['att_rel_bias'] = _bias_reduce(tag, _bias_untile(tag, dbias).reshape(ATT_HEADS, CHUNK * BAND))

    cg, cb = _row(w['conv_ln_g']), _row(w['conv_ln_b'])
    dyc, dcg, dcb, ddwb = _conv_bwd_norm(tag, s['yc'], dy_d, cg, cb)
    dpd, ddw = _conv_bwd_taps(tag, s['pd'], dyc, w['conv_dw_w'])
    g['conv_ln_g'], g['conv_ln_b'], g['conv_dw_b'], g['conv_dw_w'] = dcg[0], dcb[0], ddwb[0], ddw

    dprojs = (dpa, dpb, dpr, dpc, dpd)
    dh0, dg1 = _inproj_bwd(tag, dh1, s['h'], dxn_gate, dprojs, _row(w['norm1_g']), s['win'])
    g['norm1_g'] = dg1[0]
    dw_in = jnp.concatenate([_tn_call("dw_in%s%s" % (name, tag), dp, s['xn'], n, D_MODEL)
                             for (name, _, n), dp in zip(IN_GROUPS, dprojs)], axis=0)
    w_in_item = _scatter_item(dw_in.reshape(N_DEV, IN_COLS // N_DEV, D_MODEL))
    return dh0, g, parts, carried_out, w_in_item


def _local_step(x, p, target, final_g, layers, shards):
    h = x
    saved = []
    for i in range(DEPTH):
        nxt = shards[i + 1] if i + 1 < DEPTH else None
        h, s, next_w_in = _layer_fwd(i, h, p[i], layers[i], shards[i], nxt)
        saved.append(s)
        if nxt:
            layers[i + 1]['w_in'] = next_w_in
    dh, loss, dgf = _loss_head(h, target, _row(final_g))
    small, parts, carried = [None] * DEPTH, [None] * DEPTH, []
    for i in reversed(range(DEPTH)):
        dh, small[i], parts[i], carried_out, carried_item = _layer_bwd(i, dh, saved[i], layers[i], carried)
        if carried:
            parts[i + 1]['w_in'] = carried_out[0]
        carried = [carried_item]
    return loss[0, 0], dh, dgf[0], small, parts, carried[0]


def _peers():
    x, y, c = lax.axis_index("x"), lax.axis_index("y"), lax.axis_index("c")
    me = 4 * x + 2 * y + c
    out = []
    for k in range(1, N_DEV):
        px = (1 - x) if k & 4 else x
        py = (1 - y) if k & 2 else y
        pc = (1 - c) if k & 1 else c
        out.append((k - 1, (px, py, pc), 4 * px + 2 * py + pc))
    return me, out


def _block(ref, axis, idx, width):
    ix = [slice(None)] * len(ref.shape)
    ix[axis] = pl.ds(pl.multiple_of(idx * width, width), width)
    return ref.at[tuple(ix)]


def _slot(ref, idx):
    return ref.at[idx]


def _whole(ref, idx):
    return ref


def _gather_item(src, out_shape=None, axis=None):
    if axis is None:
        return dict(src=src, out=(N_DEV,) + src.shape, take=_whole, put=_slot)
    return dict(src=src, out=tuple(out_shape), take=_whole,
                put=lambda ref, s: _block(ref, axis, s, src.shape[axis]))


def _scatter_item(src, axis=None, lead=0):
    if axis is None:
        shape = src.shape[:lead] + src.shape[lead + 1:]
        take = lambda ref, s: ref.at[(slice(None),) * lead + (s,)]
    else:
        width = src.shape[axis] // N_DEV
        shape = src.shape[:axis] + (width,) + src.shape[axis + 1:]
        take = lambda ref, s: _block(ref, axis, s, width)
    return dict(src=src, out=(N_DEV,) + shape, take=take, put=_slot)


def _exchange_sems(n):
    return [pltpu.SemaphoreType.DMA((n * (N_DEV - 1),)), pltpu.SemaphoreType.DMA((n * (N_DEV - 1),)),
            pltpu.SemaphoreType.DMA((n,))]


def _exchange_copies(items, src_refs, out_refs, sems, start):
    send_sems, recv_sems, local_sems = sems
    me, peers = _peers()

    def remote(i, k, pos, receiver, sender):
        it = items[i]
        return pltpu.make_async_remote_copy(
            src_ref=it['take'](src_refs[i], receiver), dst_ref=it['put'](out_refs[i], sender),
            send_sem=send_sems.at[i * (N_DEV - 1) + k], recv_sem=recv_sems.at[i * (N_DEV - 1) + k],
            device_id=pos, device_id_type=pl.DeviceIdType.MESH)

    local = [pltpu.make_async_copy(it['take'](src_refs[i], me), it['put'](out_refs[i], me), local_sems.at[i])
             for i, it in enumerate(items)]
    if start:
        for cp in local:
            cp.start()
        for k, pos, flat in peers:
            for i in range(len(items)):
                remote(i, k, pos, flat, me).start()
    else:
        for k, pos, flat in peers:
            for i in range(len(items)):
                remote(i, k, pos, flat, flat).wait_recv()
        for k, pos, flat in peers:
            for i in range(len(items)):
                remote(i, k, pos, flat, me).wait_send()
        for cp in local:
            cp.wait()


def _exchange(name, items):
    n = len(items)

    def body(*refs):
        _exchange_copies(items, refs[:n], refs[n:2 * n], refs[2 * n:], start=True)
        _exchange_copies(items, refs[:n], refs[n:2 * n], refs[2 * n:], start=False)

    any_spec = pl.BlockSpec(memory_space=pl.ANY)
    return pl.pallas_call(
        body, out_shape=[jax.ShapeDtypeStruct(it['out'], it['src'].dtype) for it in items],
        in_specs=[any_spec] * n, out_specs=[any_spec] * n, scratch_shapes=_exchange_sems(n),
        name=name)(*[it['src'] for it in items])


def _pack(arrays, dtype, lead=None):
    flat = [a.astype(dtype).reshape((lead, -1) if lead else (-1,)) for a in arrays]
    cat = jnp.concatenate(flat, axis=-1)
    n = cat.shape[-1]
    unit = ADAMW_TILE if n >= ADAMW_TILE * PACK_COLS else 16
    rows = -(-n // (PACK_COLS * unit)) * unit
    pad = rows * PACK_COLS - n
    if pad:
        cat = jnp.pad(cat, ((0, 0), (0, pad)) if lead else ((0, pad),))
    return cat.reshape((lead, rows, PACK_COLS) if lead else (rows, PACK_COLS))


def _unpack(buf, shapes, lead=None):
    flat = buf.reshape((lead, -1) if lead else (-1,))
    out, off = [], 0
    for shp in shapes:
        n = int(np.prod(shp))
        piece = flat[..., off:off + n]
        out.append(piece.reshape(((lead,) if lead else ()) + tuple(shp)))
        off += n
    return out


def _to_slabs(full, axis):
    shp = full.shape
    split = full.reshape(shp[:axis] + (N_DEV, shp[axis] // N_DEV) + shp[axis + 1:])
    return jnp.moveaxis(split, axis, 0)


def _from_slabs(slabs, axis):
    moved = jnp.moveaxis(slabs, 0, axis)
    shp = moved.shape
    return moved.reshape(shp[:axis] + (shp[axis] * shp[axis + 1],) + shp[axis + 2:])


def _adamw_block(r, c):
    if r % 8:
        return r, 256
    br = min(r, max(8, ADAMW_TILE * PACK_COLS // c))
    while r % br:
        br //= 2
    return br, c


def _adamw(name, partials, w, m, v):
    n_lead, r, c = w.shape
    br, bc = _adamw_block(r, c)
    ni, nj = r // br, c // bc
    c1 = 1.0 - ADAM_B1 ** ADAM_STEP
    c2 = 1.0 - ADAM_B2 ** ADAM_STEP

    def kern(*refs):
        p_refs = refs[:n_lead]
        w_ref, m_ref, v_ref, g_ref, d_ref, nm_ref, nv_ref = refs[n_lead:]
        layer = pl.program_id(0)
        g = None
        for l, p_ref in enumerate(p_refs):
            gl = p_ref[0].astype(F32)
            for s in range(1, N_DEV):
                gl = gl + p_ref[s].astype(F32)
            g = gl if g is None else jnp.where(layer == l, gl, g)
        nm = ADAM_B1 * m_ref[...] + (1.0 - ADAM_B1) * g
        nv = ADAM_B2 * v_ref[...] + (1.0 - ADAM_B2) * jnp.square(g)
        g_ref[...] = g
        nm_ref[...] = nm
        nv_ref[...] = nv
        d_ref[...] = -ADAM_LR * ((nm / c1) / (jnp.sqrt(nv / c2) + ADAM_EPS) + ADAM_WD * w_ref[...])

    def part_spec(mine):
        def index(l, i, j):
            before, after = l < mine, l > mine
            return (0, jnp.where(before, 0, jnp.where(after, ni - 1, i)), jnp.where(before, 0, jnp.where(after, nj - 1, j)))
        return pl.BlockSpec((N_DEV, br, bc), index)

    blk = pl.BlockSpec((None, br, bc), lambda l, i, j: (l, i, j))
    return pl.pallas_call(
        kern, grid=(n_lead, ni, nj),
        in_specs=[part_spec(l) for l in range(n_lead)] + [blk, blk, blk],
        out_specs=[blk] * 4, out_shape=[jax.ShapeDtypeStruct(w.shape, F32)] * 4,
        compiler_params=pltpu.CompilerParams(dimension_semantics=("arbitrary",) * 3),
        name=name)(*partials, w, m, v)


def _as_rows(a, lead):
    return a.reshape(a.shape[:lead] + (-1, a.shape[-1]))


def kernel(x, p, norm1_g, w_in, sg_ln_g, sg_ln_b, sg_w, sg_b, gla_w_a2, gla_b_a, gla_norm_g, att_rel_bias, conv_dw_w, conv_dw_b, conv_ln_g, conv_ln_b, w_branch, w_gate, b_gate, w_out, norm2_g, w_ff1, w_ff2, norm3_g, w_ple_gate, b_ple_gate, w_ple, final_g, loss_target, m_norm1_g, m_w_in, m_sg_ln_g, m_sg_ln_b, m_sg_w, m_sg_b, m_gla_w_a2, m_gla_b_a, m_gla_norm_g, m_att_rel_bias, m_conv_dw_w, m_conv_dw_b, m_conv_ln_g, m_conv_ln_b, m_w_branch, m_w_gate, m_b_gate, m_w_out, m_norm2_g, m_w_ff1, m_w_ff2, m_norm3_g, m_w_ple_gate, m_b_ple_gate, m_w_ple, m_final_g, v_norm1_g, v_w_in, v_sg_ln_g, v_sg_ln_b, v_sg_w, v_sg_b, v_gla_w_a2, v_gla_b_a, v_gla_norm_g, v_att_rel_bias, v_conv_dw_w, v_conv_dw_b, v_conv_ln_g, v_conv_ln_b, v_w_branch, v_w_gate, v_b_gate, v_w_out, v_norm2_g, v_w_ff1, v_w_ff2, v_norm3_g, v_w_ple_gate, v_b_ple_gate, v_w_ple, v_final_g):
    args = locals()
    wts = {n: args[n] for n in WEIGHTS}
    mom = {n: args['m_' + n] for n in WEIGHTS}
    var = {n: args['v_' + n] for n in WEIGHTS}

    local = {d_name: dict(d, w_in=jnp.swapaxes(d['w_in'], 1, 2))
             for d_name, d in (("w", wts), ("m", mom), ("v", var))}
    shards = [{n: local["w"][n][i].astype(_MXU_DTYPE) for n in MXU_WEIGHTS} for i in range(DEPTH)]

    first_w_in, vec = _exchange("gather_first_weights", _gather_items(shards[0], ['w_in'])
                                + [_gather_item(_pack([wts[n] for n in VEC_WEIGHTS], F32))])
    vec_full = {n: _from_slabs(slabs, SHARD_AXIS[n])
                for n, slabs in zip(VEC_WEIGHTS, _unpack(vec, [wts[n].shape for n in VEC_WEIGHTS], lead=N_DEV))}
    small_names = [n for n in WEIGHTS if n not in MXU_WEIGHTS and n != 'final_g']
    layers = [{n: (vec_full[n] if n in vec_full else wts[n])[i] for n in small_names} for i in range(DEPTH)]
    _land(layers[0], ['w_in'], [first_w_in])

    loss, grad_x, dgf, small, parts, last_item = _local_step(x[0], p[:, 0], loss_target[0], final_g, layers, shards)
    loss = lax.psum(loss, ("x", "y", "c"))
    grads = {n: jnp.stack([small[i][n] for i in range(DEPTH)]) for n in small_names}
    grads['final_g'] = dgf

    parts[0]['w_in'], vec_parts, repl_parts = _exchange("exchange_last_grads", [
        last_item,
        _scatter_item(_pack([_to_slabs(grads[n], SHARD_AXIS[n]) for n in VEC_WEIGHTS], F32, lead=N_DEV)),
        _gather_item(_pack([grads[n] for n in REPLICATED], F32))])

    results = {}
    for n in MXU_WEIGHTS:
        w3, m3, v3 = (_as_rows(local[d][n], 1) for d in ("w", "m", "v"))
        outs = _adamw("adamw_" + n, [parts[i][n].reshape((N_DEV,) + w3.shape[1:]) for i in range(DEPTH)], w3, m3, v3)
        for kind, a in zip(("grad", "delta", "new_m", "new_v"), outs):
            a = a.reshape(local["w"][n].shape)
            results[kind, n] = jnp.swapaxes(a, 1, 2) if n == 'w_in' else a
    for names, part, call in ((VEC_WEIGHTS, vec_parts, "adamw_vec"), (REPLICATED, repl_parts, "adamw_replicated")):
        packed = [_pack([d[n] for n in names], F32)[None] for d in (wts, mom, var)]
        outs = _adamw(call, [part], *packed)
        for kind, buf in zip(("grad", "delta", "new_m", "new_v"), outs):
            for n, a in zip(names, _unpack(buf[0], [wts[n].shape for n in names])):
                results[kind, n] = a
    return (loss, grad_x[None]) + tuple(results[kind, n] for kind in ("grad", "delta", "new_m", "new_v")
                                        for n in WEIGHTS)
```

```python
import functools

import numpy as np
import jax
import jax.numpy as jnp
from jax import lax
from jax.experimental import pallas as pl
from jax.experimental.pallas import tpu as pltpu

F32 = jnp.float32
_MXU_DTYPE = jnp.bfloat16
GRAD_DTYPE = jnp.bfloat16

N_DEV = 8
D_MODEL = 1024
DEPTH = 2
CHUNK = 64
PLE_DIM = 256
BW = 512
SG_BLOCK = 128
SG_GROUPS = 4
GLA_HEADS = 4
GLA_DK = 64
GLA_DV = 128
GLA_RANK = 16
GLA_TAU = 16.0
ATT_HEADS = 8
ATT_HD = 64
ATT_BAND = 9
BAND = ATT_BAND * CHUNK
MAX_REL = 256
REL_TABLE = CHUNK + MAX_REL
CONV_K = 31
CONV_HALO = 32
D_FF = 4096
EPS = 1e-6
NEG_INF = -1e30

IN_GROUPS = (("A", 0, 1024), ("B", 1024, 1536), ("a", 2560, 16), ("C", 2576, 1536), ("D", 4112, 1024))
IN_COLS = 5136

ADAM_LR = 0.001
ADAM_B1 = 0.9
ADAM_B2 = 0.999
ADAM_EPS = 1e-08
ADAM_WD = 0.01
ADAM_STEP = 10

ADAMW_TILE = 128
PACK_COLS = 1024
VMEM_LIMIT_MB = 56

_NN = (((1,), (0,)), ((), ()))
_NT = (((1,), (1,)), ((), ()))
_TN = (((0,), (0,)), ((), ()))

WEIGHTS = ['norm1_g', 'w_in', 'sg_ln_g', 'sg_ln_b', 'sg_w', 'sg_b', 'gla_w_a2', 'gla_b_a', 'gla_norm_g',
           'att_rel_bias', 'conv_dw_w', 'conv_dw_b', 'conv_ln_g', 'conv_ln_b', 'w_branch', 'w_gate', 'b_gate',
           'w_out', 'norm2_g', 'w_ff1', 'w_ff2', 'norm3_g', 'w_ple_gate', 'b_ple_gate', 'w_ple', 'final_g']
SHARD_AXIS = {'w_in': 2, 'gla_w_a2': 2, 'att_rel_bias': 2, 'conv_dw_w': 2, 'w_branch': 3, 'w_gate': 2,
              'b_gate': 2, 'w_out': 1, 'w_ff1': 2, 'w_ff2': 1, 'w_ple_gate': 1, 'w_ple': 2}
MXU_WEIGHTS = ('w_in', 'w_branch', 'w_gate', 'w_out', 'w_ff1', 'w_ff2', 'w_ple_gate', 'w_ple')
VEC_WEIGHTS = ('gla_w_a2', 'att_rel_bias', 'conv_dw_w', 'b_gate')
SHARDED = tuple(n for n in WEIGHTS if n in SHARD_AXIS)
REPLICATED = tuple(n for n in WEIGHTS if n not in SHARD_AXIS)


def _mm(a, b, dims=_NN):
    return lax.dot_general(a.astype(_MXU_DTYPE), b.astype(_MXU_DTYPE), dims, preferred_element_type=F32)


def _split3(x):
    x1 = x.astype(jnp.bfloat16)
    r1 = x - x1.astype(F32)
    x2 = r1.astype(jnp.bfloat16)
    x3 = (r1 - x2.astype(F32)).astype(jnp.bfloat16)
    return x1, x2, x3


def _mm_exact_rhs(m, x, dims=_NN):
    return sum(lax.dot_general(m, xi, dims, preferred_element_type=F32) for xi in _split3(x))


def _mm_exact_lhs(x, m, dims=_NN):
    return sum(lax.dot_general(xi, m, dims, preferred_element_type=F32) for xi in _split3(x))


def _sigmoid(x):
    return 1.0 / (1.0 + jnp.exp(-x))


def _gelu(x):
    c = 0.7978845608028654
    t = jnp.tanh(c * (x + 0.044715 * x * x * x))
    return 0.5 * x * (1.0 + t), t


def _gelu_grad(x, t):
    c = 0.7978845608028654
    return 0.5 * (1.0 + t) + 0.5 * x * (1.0 - t * t) * c * (1.0 + 3.0 * 0.044715 * x * x)


def _rms_stat(h):
    return lax.rsqrt(jnp.mean(h * h, axis=-1, keepdims=True) + EPS)


def _rms_bwd(dy, h, g, r):
    hh = h * r
    dhh = dy * g
    dh = r * (dhh - hh * jnp.mean(dhh * hh, axis=-1, keepdims=True))
    return dh, jnp.sum(dy * hh, axis=0, keepdims=True)


def _ln_fwd(x, g, b):
    mu = jnp.mean(x, axis=-1, keepdims=True)
    xc = x - mu
    rs = lax.rsqrt(jnp.mean(xc * xc, axis=-1, keepdims=True) + EPS)
    xh = xc * rs
    return xh * g + b, xh, rs


def _ln_bwd(dy, xh, rs, g):
    dxh = dy * g
    dx = rs * (dxh - jnp.mean(dxh, axis=-1, keepdims=True) - xh * jnp.mean(dxh * xh, axis=-1, keepdims=True))
    return dx, jnp.sum(dy * xh, axis=0, keepdims=True), jnp.sum(dy, axis=0, keepdims=True)


def _row_call(name, body, nt, rows=(), halos=(), res=(), outs=(), accs=(), scratch=(), reverse=False, exchange=()):
    def pos(i):
        return (nt - 1 - i) if reverse else i

    def lead(ndim, f):
        return lambda i: (f(pos(i)),) + (0,) * (ndim - 1)

    in_specs, operands = [], []
    for a, tile in rows:
        in_specs.append(pl.BlockSpec((tile,) + a.shape[1:], lead(a.ndim, lambda t: t)))
        operands.append(a)
    for a, blk, per, side in halos:
        last = a.shape[0] // blk - 1
        delta = {'prev2': -2, 'prev': -1, 'next': per, 'next2': per + 1}[side]
        f = lambda t, per=per, last=last, delta=delta: jnp.clip(t * per + delta, 0, last)
        in_specs.append(pl.BlockSpec((blk,) + a.shape[1:], lead(a.ndim, f)))
        operands.append(a)
    for a in res:
        in_specs.append(pl.BlockSpec(a.shape, lambda i, nd=a.ndim: (0,) * nd, pipeline_mode=pl.Buffered(1)))
        operands.append(a)
    out_specs, out_shape = [], []
    for shape, dtype, tile in outs:
        out_specs.append(pl.BlockSpec((tile,) + tuple(shape[1:]), lead(len(shape), lambda t: t)))
        out_shape.append(jax.ShapeDtypeStruct(tuple(shape), dtype))
    for shape in accs:
        out_specs.append(pl.BlockSpec(tuple(shape), lambda i, nd=len(shape): (0,) * nd))
        out_shape.append(jax.ShapeDtypeStruct(tuple(shape), F32))
    nx = len(exchange)
    any_spec = pl.BlockSpec(memory_space=pl.ANY)
    for it in exchange:
        in_specs.append(any_spec)
        operands.append(it['src'])
        out_specs.append(any_spec)
        out_shape.append(jax.ShapeDtypeStruct(it['out'], it['src'].dtype))
    sizes = (len(rows), len(halos), len(res), nx, len(outs), len(accs), nx, len(scratch), 3 if nx else 0)

    def kern(*refs):
        i = pl.program_id(0)
        groups, at = [], 0
        for n in sizes:
            groups.append(refs[at:at + n])
            at += n
        row_refs, halo_refs, res_refs, x_src, out_refs, acc_refs, x_dst, scr_refs, sems = groups

        @pl.when(i == 0)
        def _():
            for r in tuple(acc_refs) + tuple(scr_refs):
                r[...] = jnp.zeros(r.shape, r.dtype)
            if nx:
                _exchange_copies(exchange, x_src, x_dst, sems, start=True)

        body(pos(i), row_refs, halo_refs, res_refs, out_refs, acc_refs, scr_refs)

        if nx:
            @pl.when(i == nt - 1)
            def _():
                _exchange_copies(exchange, x_src, x_dst, sems, start=False)

    result = pl.pallas_call(
        kern, grid=(nt,), in_specs=in_specs, out_specs=out_specs, out_shape=out_shape,
        scratch_shapes=[pltpu.VMEM(tuple(s), d) for s, d in scratch] + (_exchange_sems(nx) if nx else []),
        compiler_params=pltpu.CompilerParams(dimension_semantics=("arbitrary",),
                                             vmem_limit_bytes=VMEM_LIMIT_MB << 20),
        name=name)(*operands)
    return tuple(result)


def _tn_call(name, a, b, k, n, nblk=1, a_col=False, b_col=True, b_off=0, out='cols', tile=2048):
    tile = min(tile, a.shape[0])
    nt = a.shape[0] // tile
    if out == 'cols':
        o_shape, o_spec = (k, nblk * n), pl.BlockSpec((k, n), lambda j, t: (0, j))
    elif out == 'rows':
        o_shape, o_spec = (nblk * k, n), pl.BlockSpec((k, n), lambda j, t: (j, 0))
    else:
        o_shape, o_spec = (nblk, k, n), pl.BlockSpec((None, k, n), lambda j, t: (j, 0, 0))

    def kern(a_ref, b_ref, o_ref, acc):
        @pl.when(pl.program_id(1) == 0)
        def _():
            acc[...] = jnp.zeros(acc.shape, acc.dtype)

        acc[...] += lax.dot_general(a_ref[...], b_ref[...], _TN, preferred_element_type=F32)

        @pl.when(pl.program_id(1) == nt - 1)
        def _():
            o_ref[...] = acc[...].astype(o_ref.dtype)

    return pl.pallas_call(
        kern, grid=(nblk, nt),
        in_specs=[pl.BlockSpec((tile, k), (lambda j, t: (t, j)) if a_col else (lambda j, t: (t, 0))),
                  pl.BlockSpec((tile, n), (lambda j, t: (t, j + b_off)) if b_col else (lambda j, t: (t, b_off)))],
        out_specs=o_spec, out_shape=jax.ShapeDtypeStruct(o_shape, GRAD_DTYPE),
        scratch_shapes=[pltpu.VMEM((k, n), F32)],
        compiler_params=pltpu.CompilerParams(dimension_semantics=("arbitrary", "arbitrary"),
                                             vmem_limit_bytes=VMEM_LIMIT_MB << 20),
        name=name)(a, b)


def _inproj_fwd(tag, h, g1, w_groups, tm=512, exchange=()):
    t_len = h.shape[0]

    def body(t, rows, halos, res, outs, accs, scr):
        hv = rows[0][...]
        xn = (hv * _rms_stat(hv) * res[0][...]).astype(_MXU_DTYPE)
        outs[0][...] = xn
        for o, w in zip(outs[1:], res[1:]):
            o[...] = lax.dot_general(xn, w[...], _NT, preferred_element_type=F32)

    outs = [((t_len, D_MODEL), _MXU_DTYPE, tm)] + [((t_len, w.shape[0]), F32, tm) for w in w_groups]
    return _row_call("inproj_fwd" + tag, body, t_len // tm, rows=[(h, tm)], res=[g1] + list(w_groups), outs=outs,
                     exchange=exchange)


def _sg_mask():
    row = lax.broadcasted_iota(jnp.int32, (SG_BLOCK, SG_BLOCK), 0)
    col = lax.broadcasted_iota(jnp.int32, (SG_BLOCK, SG_BLOCK), 1)
    return jnp.logical_or(row >= CHUNK, col < CHUNK)


def _sg_forward_parts(pa, lg, lb, w_ref, bt):
    tm = pa.shape[0]
    nb = tm // SG_BLOCK
    su, sv = pa[:, :BW], pa[:, BW:]
    u, tu = _gelu(su)
    gv, tv = _gelu(sv)
    vn, xh, rs = _ln_fwd(gv, lg, lb)
    mask = _sg_mask()
    wms, xs, ms = [], [], []
    for g in range(SG_GROUPS):
        wm = jnp.where(mask, w_ref[g], 0.0).astype(_MXU_DTYPE)
        xg = jnp.concatenate([vn[b * SG_BLOCK:(b + 1) * SG_BLOCK, g * 128:(g + 1) * 128] for b in range(nb)], axis=1)
        xg = xg.astype(_MXU_DTYPE)
        ms.append(lax.dot_general(wm, xg, _NN, preferred_element_type=F32) + bt[:, g:g + 1])
        wms.append(wm)
        xs.append(xg)
    mixed = _sg_unfold(ms, nb)
    return su, sv, u, tu, tv, xh, rs, wms, xs, mixed


def _sg_unfold(per_group, nb):
    return jnp.concatenate(
        [jnp.concatenate([per_group[g][:, b * 128:(b + 1) * 128] for g in range(SG_GROUPS)], axis=1)
         for b in range(nb)], axis=0)


def _sg_fwd(tag, proj_a, lg, lb, sg_w, sg_bt, tm=512):
    t_len = proj_a.shape[0]

    def body(t, rows, halos, res, outs, accs, scr):
        parts = _sg_forward_parts(rows[0][...], res[0][...], res[1][...], res[2], res[3][...])
        outs[0][...] = (parts[2] * parts[-1]).astype(_MXU_DTYPE)

    return _row_call("sg_fwd" + tag, body, t_len // tm, rows=[(proj_a, tm)], res=[lg, lb, sg_w, sg_bt],
                     outs=[((t_len, BW), _MXU_DTYPE, tm)])[0]


def _sg_bwd(tag, proj_a, dy, lg, lb, sg_w, sg_bt, tm=512):
    t_len = proj_a.shape[0]
    nb = tm // SG_BLOCK

    def body(t, rows, halos, res, outs, accs, scr):
        lgv = res[0][...]
        su, sv, u, tu, tv, xh, rs, wms, xs, mixed = _sg_forward_parts(rows[0][...], lgv, res[1][...], res[2], res[3][...])
        dyv = rows[1][...]
        dsu = dyv * mixed * _gelu_grad(su, tu)
        dmixed = dyv * u
        mask = _sg_mask()
        dxs, dbs = [], []
        for g in range(SG_GROUPS):
            dm = jnp.concatenate([dmixed[b * SG_BLOCK:(b + 1) * SG_BLOCK, g * 128:(g + 1) * 128] for b in range(nb)],
                                 axis=1)
            dmb = dm.astype(_MXU_DTYPE)
            dw = lax.dot_general(dmb, xs[g], _NT, preferred_element_type=F32)
            accs[0][g] += jnp.where(mask, dw, 0.0)
            dbs.append(jnp.sum(dm, axis=1, keepdims=True))
            dxs.append(lax.dot_general(wms[g], dmb, _TN, preferred_element_type=F32))
        accs[1][...] += jnp.concatenate(dbs, axis=1)
        dvn = _sg_unfold(dxs, nb)
        dgv, dlg, dlb = _ln_bwd(dvn, xh, rs, lgv)
        accs[2][...] += dlg
        accs[3][...] += dlb
        dsv = dgv * _gelu_grad(sv, tv)
        outs[0][...] = jnp.concatenate([dsu, dsv], axis=1).astype(_MXU_DTYPE)

    return _row_call("sg_bwd" + tag, body, t_len // tm, rows=[(proj_a, tm), (dy, tm)], res=[lg, lb, sg_w, sg_bt],
                     outs=[((t_len, 2 * BW), _MXU_DTYPE, tm)],
                     accs=[(SG_GROUPS, SG_BLOCK, SG_BLOCK), (SG_BLOCK, SG_GROUPS), (1, BW), (1, BW)])


def _chunk_matrix(tm, kind):
    row = lax.broadcasted_iota(jnp.int32, (tm, tm), 0)
    col = lax.broadcasted_iota(jnp.int32, (tm, tm), 1)
    same = lax.shift_right_logical(row, 6) == lax.shift_right_logical(col, 6)
    if kind == 'cumsum':
        same = jnp.logical_and(same, row >= col)
    elif kind == 'revsum':
        same = jnp.logical_and(same, row <= col)
    return same.astype(jnp.bfloat16)


def _gla_gate(pa, wa2, ba):
    z = _mm(pa, wa2) + ba
    log_a = (jnp.minimum(z, 0.0) - jnp.log(1.0 + jnp.exp(-jnp.abs(z)))) * (1.0 / GLA_TAU)
    return z, log_a


def _gla_decay(pb, log_a):
    tm = pb.shape[0]
    cum = _mm_exact_rhs(_chunk_matrix(tm, 'cumsum'), log_a)
    tot = _mm_exact_rhs(_chunk_matrix(tm, 'total'), log_a)
    w = jnp.exp(tot - cum)
    return w, pb[:, 256:512] * w, jnp.exp(tot)


def _per_head(fn):
    return jnp.concatenate([fn(h) for h in range(GLA_HEADS)], axis=1)


def _gla_read(qs, sb, c):
    rows = slice(c * CHUNK, (c + 1) * CHUNK)
    return _per_head(lambda h: lax.dot_general(qs[rows, h * 64:(h + 1) * 64], sb[:, h * 64:(h + 1) * 64], _NT,
                                               preferred_element_type=F32))


def _gla_fwd(tag, proj_b, proj_a, wa2, ba, ng, tm=256, exchange=()):
    t_len = proj_b.shape[0]
    cpt = tm // CHUNK

    def body(t, rows, halos, res, outs, accs, scr):
        pb = rows[0][...]
        _, log_a = _gla_gate(rows[1][...], res[0][...], res[1][...])
        _, kd, dec = _gla_decay(pb, log_a)
        kdb = kd.astype(_MXU_DTYPE)
        vb = pb[:, 512:1024].astype(_MXU_DTYPE)
        qs = (pb[:, 0:256] * (GLA_DK ** -0.5)).astype(_MXU_DTYPE)
        uts = []
        for c in range(cpt):
            rs = slice(c * CHUNK, (c + 1) * CHUNK)
            uts.append(_per_head(lambda h: lax.dot_general(vb[rs, h * 128:(h + 1) * 128], kdb[rs, h * 64:(h + 1) * 64],
                                                           _TN, preferred_element_type=F32)))
        s_new = scr[0][...]
        o = []
        for c in range(cpt):
            s_new = dec[c * CHUNK:c * CHUNK + 1] * s_new + uts[c]
            outs[1][c] = s_new
            o.append(_gla_read(qs, s_new.astype(_MXU_DTYPE), c))
        scr[0][...] = s_new
        o = jnp.concatenate(o, axis=0)
        on = _per_head(lambda h: o[:, h * 128:(h + 1) * 128] * lax.rsqrt(
            jnp.mean(jnp.square(o[:, h * 128:(h + 1) * 128]), axis=-1, keepdims=True) + EPS))
        r = pb[:, 1024:1536]
        outs[0][...] = (on * res[2][...] * (r * _sigmoid(r))).astype(_MXU_DTYPE)

    return _row_call("gla_fwd" + tag, body, t_len // tm, rows=[(proj_b, tm), (proj_a, tm)], res=[wa2, ba, ng],
                     outs=[((t_len, BW), _MXU_DTYPE, tm), ((t_len // CHUNK, GLA_DV, 256), F32, cpt)],
                     scratch=[((GLA_DV, 256), F32)], exchange=exchange)


def _gla_bwd(tag, proj_b, proj_a, dy, states, wa2, ba, ng, tm=256):
    t_len = proj_b.shape[0]
    cpt = tm // CHUNK

    def body(t, rows, halos, res, outs, accs, scr):
        pb = rows[0][...]
        pa = rows[1][...]
        dyv = rows[2][...]
        st_ref = rows[3]
        wa2v = res[0][...]
        z, log_a = _gla_gate(pa, wa2v, res[1][...])
        ngv = res[2][...]
        w, kd, dec = _gla_decay(pb, log_a)
        kdb = kd.astype(_MXU_DTYPE)
        vb = pb[:, 512:1024].astype(_MXU_DTYPE)
        qs = (pb[:, 0:256] * (GLA_DK ** -0.5)).astype(_MXU_DTYPE)
        chunks = [slice(c * CHUNK, (c + 1) * CHUNK) for c in range(cpt)]
        sbs = [st_ref[c].astype(_MXU_DTYPE) for c in range(cpt)]
        o = jnp.concatenate([_gla_read(qs, sbs[c], c) for c in range(cpt)], axis=0)
        r = pb[:, 1024:1536]
        sig = _sigmoid(r)
        sil = r * sig
        dos, ons = [], []
        for h in range(GLA_HEADS):
            hs = slice(h * 128, (h + 1) * 128)
            oh = o[:, hs]
            rstd = lax.rsqrt(jnp.mean(oh * oh, axis=-1, keepdims=True) + EPS)
            on = oh * rstd
            don = dyv[:, hs] * ngv[:, hs] * sil[:, hs]
            dos.append(rstd * (don - on * jnp.mean(don * on, axis=-1, keepdims=True)))
            ons.append(on)
        on = jnp.concatenate(ons, axis=1)
        accs[2][...] += jnp.sum(dyv * on * sil, axis=0, keepdims=True)
        dr = dyv * on * ngv * (sig * (1.0 + r * (1.0 - sig)))
        dob = jnp.concatenate(dos, axis=1).astype(_MXU_DTYPE)
        reads, dqs = [], []
        for c, rs in enumerate(chunks):
            reads.append(_per_head(lambda h: lax.dot_general(dob[rs, h * 128:(h + 1) * 128], qs[rs, h * 64:(h + 1) * 64],
                                                             _TN, preferred_element_type=F32)))
            dqs.append(_per_head(lambda h: lax.dot_general(dob[rs, h * 128:(h + 1) * 128], sbs[c][:, h * 64:(h + 1) * 64],
                                                           _NN, preferred_element_type=F32)))
        dst = scr[0][...]
        dubs, ddecs = [None] * cpt, [None] * cpt
        for c in reversed(range(cpt)):
            dst_tot = dst + reads[c]
            s_prev = st_ref[c - 1] if c > 0 else jnp.where(t > 0, halos[0][0], 0.0)
            ddecs[c] = jnp.broadcast_to(jnp.sum(dst_tot * s_prev, axis=0, keepdims=True), (CHUNK, 256))
            dst = dec[c * CHUNK:c * CHUNK + 1] * dst_tot
            dubs[c] = dst_tot.astype(_MXU_DTYPE)
        scr[0][...] = dst
        dkd = jnp.concatenate(
            [_per_head(lambda h: lax.dot_general(vb[rs, h * 128:(h + 1) * 128], dubs[c][:, h * 64:(h + 1) * 64], _NN,
                                                 preferred_element_type=F32)) for c, rs in enumerate(chunks)], axis=0)
        dv = jnp.concatenate(
            [_per_head(lambda h: lax.dot_general(kdb[rs, h * 64:(h + 1) * 64], dubs[c][:, h * 64:(h + 1) * 64], _NT,
                                                 preferred_element_type=F32)) for c, rs in enumerate(chunks)], axis=0)
        e = dkd * kd
        dtot = _mm_exact_rhs(_chunk_matrix(tm, 'total'), e) + jnp.concatenate(ddecs, axis=0) * dec
        last = (lax.broadcasted_iota(jnp.int32, e.shape, 0) & (CHUNK - 1)) == CHUNK - 1
        dla = _mm_exact_rhs(_chunk_matrix(tm, 'revsum'), jnp.where(last, dtot - e, -e))
        dz = dla * (1.0 / GLA_TAU) * _sigmoid(-z)
        dzb = dz.astype(_MXU_DTYPE)
        dq = jnp.concatenate(dqs, axis=0) * (GLA_DK ** -0.5)
        outs[0][...] = jnp.concatenate([dq, dkd * w, dv, dr], axis=1).astype(_MXU_DTYPE)
        outs[1][...] = lax.dot_general(dzb, wa2v.astype(_MXU_DTYPE), _NT, preferred_element_type=F32).astype(_MXU_DTYPE)
        accs[0][...] += lax.dot_general(pa.astype(_MXU_DTYPE), dzb, _TN, preferred_element_type=F32)
        accs[1][...] += jnp.sum(dz, axis=0, keepdims=True)

    return _row_call("gla_bwd" + tag, body, t_len // tm,
                     rows=[(proj_b, tm), (proj_a, tm), (dy, tm), (states, cpt)],
                     halos=[(states, 1, cpt, 'prev')], res=[wa2, ba, ng],
                     outs=[((t_len, 1536), _MXU_DTYPE, tm), ((t_len, GLA_RANK), _MXU_DTYPE, tm)],
                     accs=[(GLA_RANK, 256), (1, 256), (1, BW)], scratch=[((GLA_DV, 256), F32)], reverse=True)


ATT_TM = 256
ATT_KEYS = ATT_TM + (ATT_BAND - 1) * CHUNK


def _rel_index():
    l_idx = np.arange(CHUNK)[:, None]
    m_idx = np.arange(BAND)[None, :]
    rel = l_idx + (ATT_BAND - 1) * CHUNK - m_idx
    return jnp.asarray((np.clip(rel, -(CHUNK - 1), MAX_REL) + (CHUNK - 1)).reshape(1, CHUNK * BAND), jnp.int32)


BIAS_COLS = 4096


def _bias_expand(tag, rel_bias):
    n = CHUNK * BAND

    def kern(rel_ref, idx_ref, o_ref):
        onehot = (lax.broadcasted_iota(jnp.int32, (REL_TABLE, BIAS_COLS), 0) == idx_ref[...]).astype(jnp.bfloat16)
        o_ref[...] = _mm_exact_lhs(rel_ref[...], onehot)

    return pl.pallas_call(
        kern, grid=(n // BIAS_COLS,),
        in_specs=[pl.BlockSpec((ATT_HEADS, REL_TABLE), lambda i: (0, 0)), pl.BlockSpec((1, BIAS_COLS), lambda i: (0, i))],
        out_specs=pl.BlockSpec((ATT_HEADS, BIAS_COLS), lambda i: (0, i)),
        out_shape=jax.ShapeDtypeStruct((ATT_HEADS, n), F32), name="bias_expand" + tag)(rel_bias, _rel_index())


def _bias_tile(tag, bias):
    per = ATT_TM // CHUNK

    def kern(b_ref, o_ref):
        bv = b_ref[...]
        for j in range(per):
            parts = [jnp.full((CHUNK, j * CHUNK), NEG_INF, F32)] if j else []
            parts.append(bv)
            if j < per - 1:
                parts.append(jnp.full((CHUNK, (per - 1 - j) * CHUNK), NEG_INF, F32))
            o_ref[j * CHUNK:(j + 1) * CHUNK, :] = jnp.concatenate(parts, axis=1)

    return pl.pallas_call(
        kern, grid=(ATT_HEADS,), in_specs=[pl.BlockSpec((None, CHUNK, BAND), lambda h: (h, 0, 0))],
        out_specs=pl.BlockSpec((None, ATT_TM, ATT_KEYS), lambda h: (h, 0, 0)),
        out_shape=jax.ShapeDtypeStruct((ATT_HEADS, ATT_TM, ATT_KEYS), F32), name="bias_tile" + tag)(bias)


def _bias_untile(tag, dbias):
    per = ATT_TM // CHUNK

    def kern(d_ref, o_ref):
        acc = d_ref[0:CHUNK, 0:BAND]
        for j in range(1, per):
            acc = acc + d_ref[j * CHUNK:(j + 1) * CHUNK, j * CHUNK:j * CHUNK + BAND]
        o_ref[...] = acc

    return pl.pallas_call(
        kern, grid=(ATT_HEADS,), in_specs=[pl.BlockSpec((None, ATT_TM, ATT_KEYS), lambda h: (h, 0, 0))],
        out_specs=pl.BlockSpec((None, CHUNK, BAND), lambda h: (h, 0, 0)),
        out_shape=jax.ShapeDtypeStruct((ATT_HEADS, CHUNK, BAND), F32), name="bias_untile" + tag)(dbias)


def _bias_reduce(tag, dbias):
    n = CHUNK * BAND

    def kern(db_ref, idx_ref, o_ref):
        @pl.when(pl.program_id(0) == 0)
        def _():
            o_ref[...] = jnp.zeros(o_ref.shape, o_ref.dtype)

        onehot = (lax.broadcasted_iota(jnp.int32, (REL_TABLE, BIAS_COLS), 0) == idx_ref[...]).astype(jnp.bfloat16)
        o_ref[...] += _mm_exact_lhs(db_ref[...], onehot, _NT)

    return pl.pallas_call(
        kern, grid=(n // BIAS_COLS,),
        in_specs=[pl.BlockSpec((ATT_HEADS, BIAS_COLS), lambda i: (0, i)), pl.BlockSpec((1, BIAS_COLS), lambda i: (0, i))],
        out_specs=pl.BlockSpec((ATT_HEADS, REL_TABLE), lambda i: (0, 0)),
        out_shape=jax.ShapeDtypeStruct((ATT_HEADS, REL_TABLE), F32),
        compiler_params=pltpu.CompilerParams(dimension_semantics=("arbitrary",)),
        name="bias_reduce" + tag)(dbias, _rel_index())


def _attn_stage(t, pc_ref, p1_ref, p2_ref, kv):
    tm = ATT_TM
    kv[0:tm, :] = jnp.where(t > 1, p2_ref[:, 512:1536], 0.0).astype(kv.dtype)
    kv[tm:2 * tm, :] = jnp.where(t > 0, p1_ref[:, 512:1536], 0.0).astype(kv.dtype)
    kv[2 * tm:, :] = pc_ref[:, 512:1536].astype(kv.dtype)
    q = (pc_ref[:, 0:512] * (ATT_HD ** -0.5)).astype(_MXU_DTYPE)
    ok = lax.broadcasted_iota(jnp.int32, (tm, ATT_KEYS), 1) >= (2 - t) * tm
    return q, ok


def _attn_probs(q, kv, bias_h, ok, h):
    hs = slice(h * ATT_HD, (h + 1) * ATT_HD)
    s = lax.dot_general(q[:, hs], kv[:, hs], _NT, preferred_element_type=F32) + bias_h
    s = jnp.where(ok, s, NEG_INF)
    e = jnp.exp(s - jnp.max(s, axis=-1, keepdims=True))
    return e * (1.0 / jnp.sum(e, axis=-1, keepdims=True))


def _attn_halos(proj_c):
    return [(proj_c, ATT_TM, 1, 'prev'), (proj_c, ATT_TM, 1, 'prev2')]


def _attn_fwd(tag, proj_c, bias, exchange=()):
    t_len = proj_c.shape[0]
    tm = ATT_TM

    def body(t, rows, halos, res, outs, accs, scr):
        b_ref, kv = res[0], scr[0]
        q, ok = _attn_stage(t, rows[0], halos[0], halos[1], kv)
        o = [lax.dot_general(_attn_probs(q, kv, b_ref[h], ok, h).astype(_MXU_DTYPE),
                             kv[:, BW + h * ATT_HD:BW + (h + 1) * ATT_HD], _NN, preferred_element_type=F32)
             for h in range(ATT_HEADS)]
        outs[0][...] = jnp.concatenate(o, axis=1).astype(_MXU_DTYPE)

    return _row_call("attn_fwd" + tag, body, t_len // tm, rows=[(proj_c, tm)], halos=_attn_halos(proj_c),
                     res=[bias], outs=[((t_len, BW), _MXU_DTYPE, tm)], scratch=[((ATT_KEYS, 1024), _MXU_DTYPE)],
                     exchange=exchange)


def _attn_bwd(tag, proj_c, dy, bias, exchange=()):
    t_len = proj_c.shape[0]
    tm = ATT_TM
    scale = ATT_HD ** -0.5

    def body(t, rows, halos, res, outs, accs, scr):
        b_ref, kv = res[0], scr[0]
        q, ok = _attn_stage(t, rows[0], halos[0], halos[1], kv)
        do = rows[1][...].astype(_MXU_DTYPE)
        dqs, dks, dvs = [], [], []
        for h in range(ATT_HEADS):
            hs = slice(h * ATT_HD, (h + 1) * ATT_HD)
            vs = slice(BW + h * ATT_HD, BW + (h + 1) * ATT_HD)
            p = _attn_probs(q, kv, b_ref[h], ok, h)
            dp = lax.dot_general(do[:, hs], kv[:, vs], _NT, preferred_element_type=F32)
            ds = p * (dp - jnp.sum(dp * p, axis=-1, keepdims=True))
            accs[0][h] += ds
            dsb = ds.astype(_MXU_DTYPE)
            dqs.append(lax.dot_general(dsb, kv[:, hs], _NN, preferred_element_type=F32) * scale)
            dks.append(lax.dot_general(dsb, q[:, hs], _TN, preferred_element_type=F32))
            dvs.append(lax.dot_general(p.astype(_MXU_DTYPE), do[:, hs], _TN, preferred_element_type=F32))
        outs[0][...] = jnp.concatenate(dqs, axis=1).astype(_MXU_DTYPE)
        dkv = jnp.concatenate(dks + dvs, axis=1)
        outs[1][...] = dkv[2 * tm:, :]
        outs[2][...] = dkv[tm:2 * tm, :]
        outs[3][...] = dkv[0:tm, :]

    return _row_call("attn_bwd" + tag, body, t_len // tm, rows=[(proj_c, tm), (dy, tm)],
                     halos=_attn_halos(proj_c), res=[bias],
                     outs=[((t_len, BW), _MXU_DTYPE, tm)] + [((t_len, 1024), F32, tm)] * 3,
                     accs=[(ATT_HEADS, ATT_TM, ATT_KEYS)], scratch=[((ATT_KEYS, 1024), _MXU_DTYPE)],
                     exchange=exchange)


def _attn_combine(tag, dq, dkv_own, dkv_prev, dkv_prev2):
    t_len = dq.shape[0]
    tm = ATT_TM
    nt = t_len // tm

    def body(t, rows, halos, res, outs, accs, scr):
        dkv = (rows[1][...] + jnp.where(t < nt - 1, halos[0][...], 0.0)) + jnp.where(t < nt - 2, halos[1][...], 0.0)
        outs[0][...] = jnp.concatenate([rows[0][...], dkv.astype(_MXU_DTYPE)], axis=1)

    return _row_call("attn_combine" + tag, body, nt, rows=[(dq, tm), (dkv_own, tm)],
                     halos=[(dkv_prev, tm, 1, 'next'), (dkv_prev2, tm, 1, 'next2')],
                     outs=[((t_len, 1536), _MXU_DTYPE, tm)])[0]


def _conv_glu(pd):
    a, g = pd[:, :BW], pd[:, BW:]
    sig = _sigmoid(g)
    return a, sig, a * sig


def _conv_stage(t, pd_ref, ph_ref, win):
    pd = pd_ref[...]
    a, sig, y0 = _conv_glu(pd)
    win[0:CONV_HALO, :] = jnp.where(t > 0, _conv_glu(ph_ref[...])[2], 0.0)
    win[CONV_HALO:CONV_HALO + pd.shape[0], :] = y0
    return a, sig


SUBLANES = 8


def _conv_shifted(win, sh):
    for b in range(SUBLANES):
        sh[b] = win[pl.ds(b, sh.shape[1]), :]


def _conv_read(sh, r0, offset, rows):
    rem = offset % SUBLANES
    return sh[rem, pl.ds(pl.multiple_of(r0 + (offset - rem), SUBLANES), rows), :]


def _conv_tap_sum(sh, w_ref, offsets, out_ref, init=None, rb=32):
    def block(i, carry):
        r0 = pl.multiple_of(i * rb, rb)
        acc = jnp.zeros((rb, BW), F32) if init is None else jnp.broadcast_to(init, (rb, BW))
        for j, o in enumerate(offsets):
            acc = acc + w_ref[j:j + 1, :] * _conv_read(sh, r0, o, rb)
        out_ref[pl.ds(r0, rb), :] = acc
        return carry

    lax.fori_loop(0, out_ref.shape[0] // rb, block, 0)


def _conv_tap_corr(sh, d_ref, offsets, acc_ref, group=8):
    for g0 in range(0, len(offsets), group):
        offs = offsets[g0:g0 + group]

        def block(i, sums, offs=offs):
            r0 = pl.multiple_of(i * SUBLANES, SUBLANES)
            d = d_ref[pl.ds(r0, SUBLANES), :]
            return tuple(s + d * _conv_read(sh, r0, o, SUBLANES) for s, o in zip(sums, offs))

        sums = lax.fori_loop(0, d_ref.shape[0] // SUBLANES, block,
                             tuple(jnp.zeros((SUBLANES, BW), F32) for _ in offs), unroll=4)
        for j, s in enumerate(sums):
            acc_ref[g0 + j:g0 + j + 1, :] += jnp.sum(s, axis=0, keepdims=True)


def _conv_scratch(tm):
    return [((tm + CONV_HALO + SUBLANES, BW), F32), ((SUBLANES, tm + CONV_HALO, BW), F32)]


def _conv_fwd(tag, proj_d, dw_w, dw_b, ln_g, ln_b, tm=512):
    t_len = proj_d.shape[0]
    lead = CONV_HALO - (CONV_K - 1)

    def body(t, rows, halos, res, outs, accs, scr):
        win, sh = scr
        _conv_stage(t, rows[0], halos[0], win)
        _conv_shifted(win, sh)
        _conv_tap_sum(sh, res[0], [lead + j for j in range(CONV_K)], outs[1], init=res[1][...])
        yl, _, _ = _ln_fwd(outs[1][...], res[2][...], res[3][...])
        outs[0][...] = (yl * _sigmoid(yl)).astype(_MXU_DTYPE)

    return _row_call("conv_fwd" + tag, body, t_len // tm, rows=[(proj_d, tm)],
                     halos=[(proj_d, CONV_HALO, tm // CONV_HALO, 'prev')], res=[dw_w, dw_b, ln_g, ln_b],
                     outs=[((t_len, BW), _MXU_DTYPE, tm), ((t_len, BW), F32, tm)], scratch=_conv_scratch(tm))


def _conv_bwd_norm(tag, yc, dy, ln_g, ln_b, tm=512):
    t_len = yc.shape[0]

    def body(t, rows, halos, res, outs, accs, scr):
        lgv = res[0][...]
        yl, xh, rs = _ln_fwd(rows[0][...], lgv, res[1][...])
        sig = _sigmoid(yl)
        dyl = rows[1][...] * (sig * (1.0 + yl * (1.0 - sig)))
        dyc, dlg, dlb = _ln_bwd(dyl, xh, rs, lgv)
        outs[0][...] = dyc
        accs[0][...] += dlg
        accs[1][...] += dlb
        accs[2][...] += jnp.sum(dyc, axis=0, keepdims=True)

    return _row_call("conv_bwd_norm" + tag, body, t_len // tm, rows=[(yc, tm), (dy, tm)], res=[ln_g, ln_b],
                     outs=[((t_len, BW), F32, tm)], accs=[(1, BW), (1, BW), (1, BW)])


def _conv_bwd_taps(tag, proj_d, dyc, dw_w, tm=512):
    t_len = proj_d.shape[0]
    nt = t_len // tm
    lead = CONV_HALO - (CONV_K - 1)

    def body(t, rows, halos, res, outs, accs, scr):
        win, sh, wd, shd, dy0_ref = scr
        a, sig = _conv_stage(t, rows[0], halos[0], win)
        _conv_shifted(win, sh)
        wd[0:tm, :] = rows[1][...]
        wd[tm:tm + CONV_HALO, :] = jnp.where(t < nt - 1, halos[1][...], 0.0)
        _conv_shifted(wd, shd)
        _conv_tap_corr(sh, rows[1], [lead + j for j in range(CONV_K)], accs[0])
        _conv_tap_sum(shd, res[0], [CONV_K - 1 - j for j in range(CONV_K)], dy0_ref)
        dy0 = dy0_ref[...]
        outs[0][...] = jnp.concatenate([dy0 * sig, dy0 * a * sig * (1.0 - sig)], axis=1).astype(_MXU_DTYPE)

    return _row_call("conv_bwd_taps" + tag, body, nt, rows=[(proj_d, tm), (dyc, tm)],
                     halos=[(proj_d, CONV_HALO, tm // CONV_HALO, 'prev'), (dyc, CONV_HALO, tm // CONV_HALO, 'next')],
                     res=[dw_w], outs=[((t_len, 2 * BW), _MXU_DTYPE, tm)], accs=[(CONV_K, BW)],
                     scratch=_conv_scratch(tm) + _conv_scratch(tm) + [((tm, BW), F32)])


def _merge_fwd(tag, h, xn, ys, w_gate, b_gate, w_branch, w_out, tm=256):
    t_len = h.shape[0]

    def body(t, rows, halos, res, outs, accs, scr):
        xnv = rows[1][...]
        wg_ref, bg_ref, wb_ref, wo_ref = res
        merged = jnp.zeros((tm, D_MODEL), F32)
        for n in range(4):
            cs = slice(n * D_MODEL, (n + 1) * D_MODEL)
            gate = _sigmoid(lax.dot_general(xnv, wg_ref[n], _NN, preferred_element_type=F32) + bg_ref[n:n + 1, :])
            bo = lax.dot_general(rows[2 + n][...], wb_ref[n], _NN, preferred_element_type=F32)
            outs[0][:, cs] = gate
            outs[1][:, cs] = bo.astype(_MXU_DTYPE)
            merged = merged + gate * bo
        mb = merged.astype(_MXU_DTYPE)
        outs[2][...] = mb
        outs[3][...] = rows[0][...] + lax.dot_general(mb, wo_ref[...], _NN, preferred_element_type=F32)

    return _row_call("merge_fwd" + tag, body, t_len // tm, rows=[(h, tm), (xn, tm)] + [(y, tm) for y in ys],
                     res=[w_gate, b_gate, w_branch, w_out],
                     outs=[((t_len, 4 * D_MODEL), F32, tm), ((t_len, 4 * D_MODEL), _MXU_DTYPE, tm),
                           ((t_len, D_MODEL), _MXU_DTYPE, tm), ((t_len, D_MODEL), F32, tm)])


def _merge_bwd(tag, dh, gate, bo, w_gate, w_branch, w_out, tm=256):
    t_len = dh.shape[0]

    def body(t, rows, halos, res, outs, accs, scr):
        wg_ref, wb_ref, wo_ref = res
        dhb = rows[0][...].astype(_MXU_DTYPE)
        outs[0][...] = dhb
        dmerged = lax.dot_general(dhb, wo_ref[...], _NT, preferred_element_type=F32)
        dxn = jnp.zeros((tm, D_MODEL), F32)
        dbg = []
        for n in range(4):
            cs = slice(n * D_MODEL, (n + 1) * D_MODEL)
            g = rows[1][:, cs]
            dbo = (dmerged * g).astype(_MXU_DTYPE)
            dgp = dmerged * rows[2][:, cs].astype(F32) * (g * (1.0 - g))
            dgb = dgp.astype(_MXU_DTYPE)
            outs[1][:, cs] = dbo
            outs[2][:, cs] = dgb
            outs[4 + n][...] = lax.dot_general(dbo, wb_ref[n], _NT, preferred_element_type=F32)
            dxn = dxn + lax.dot_general(dgb, wg_ref[n], _NT, preferred_element_type=F32)
            dbg.append(jnp.sum(dgp, axis=0, keepdims=True))
        outs[3][...] = dxn
        accs[0][...] += jnp.concatenate(dbg, axis=1)

    return _row_call("merge_bwd" + tag, body, t_len // tm, rows=[(dh, tm), (gate, tm), (bo, tm)],
                     res=[w_gate, w_branch, w_out],
                     outs=[((t_len, D_MODEL), _MXU_DTYPE, tm), ((t_len, 4 * D_MODEL), _MXU_DTYPE, tm),
                           ((t_len, 4 * D_MODEL), _MXU_DTYPE, tm), ((t_len, D_MODEL), F32, tm)]
                     + [((t_len, BW), F32, tm)] * 4,
                     accs=[(1, 4 * D_MODEL)])


FF_COLS = 1024


def _ffn_fwd(tag, h, g2, w1, w2, tm=512):
    t_len = h.shape[0]

    def body(t, rows, halos, res, outs, accs, scr):
        hv = rows[0][...]
        hn = (hv * _rms_stat(hv) * res[0][...]).astype(_MXU_DTYPE)
        outs[0][...] = hn
        acc = hv
        for c in range(D_FF // FF_COLS):
            cs = slice(c * FF_COLS, (c + 1) * FF_COLS)
            pre = lax.dot_general(hn, res[1][:, cs], _NN, preferred_element_type=F32)
            outs[1][:, cs] = pre
            ff = jnp.square(jnp.maximum(pre, 0.0)).astype(_MXU_DTYPE)
            acc = acc + lax.dot_general(ff, res[2][cs, :], _NN, preferred_element_type=F32)
        outs[2][...] = acc

    return _row_call("ffn_fwd" + tag, body, t_len // tm, rows=[(h, tm)], res=[g2, w1, w2],
                     outs=[((t_len, D_MODEL), _MXU_DTYPE, tm), ((t_len, D_FF), F32, tm), ((t_len, D_MODEL), F32, tm)])


def _ffn_bwd(tag, dh, h, pre, g2, w1, w2, tm=256, exchange=()):
    t_len = dh.shape[0]

    def body(t, rows, halos, res, outs, accs, scr):
        dhv = rows[0][...]
        hv = rows[1][...]
        dhb = dhv.astype(_MXU_DTYPE)
        outs[0][...] = dhb
        dhn = jnp.zeros((tm, D_MODEL), F32)
        for c in range(D_FF // FF_COLS):
            cs = slice(c * FF_COLS, (c + 1) * FF_COLS)
            r = jnp.maximum(rows[2][:, cs], 0.0)
            outs[1][:, cs] = (r * r).astype(_MXU_DTYPE)
            dpre = (lax.dot_general(dhb, res[2][cs, :], _NT, preferred_element_type=F32) * (2.0 * r)).astype(_MXU_DTYPE)
            outs[2][:, cs] = dpre
            dhn = dhn + lax.dot_general(dpre, res[1][:, cs], _NT, preferred_element_type=F32)
        dres, dg = _rms_bwd(dhn, hv, res[0][...], _rms_stat(hv))
        outs[3][...] = dhv + dres
        accs[0][...] += dg

    return _row_call("ffn_bwd" + tag, body, t_len // tm, rows=[(dh, tm), (h, tm), (pre, tm)], res=[g2, w1, w2],
                     outs=[((t_len, D_MODEL), _MXU_DTYPE, tm), ((t_len, D_FF), _MXU_DTYPE, tm),
                           ((t_len, D_FF), _MXU_DTYPE, tm), ((t_len, D_MODEL), F32, tm)],
                     accs=[(1, D_MODEL)], exchange=exchange)


def _ple_fwd(tag, h, p, g3, w_pg, b_pg, w_ple, tm=512):
    t_len = h.shape[0]

    def body(t, rows, halos, res, outs, accs, scr):
        hv = rows[0][...]
        hg = (hv * _rms_stat(hv) * res[0][...]).astype(_MXU_DTYPE)
        pb = rows[1][...].astype(_MXU_DTYPE)
        pg = _sigmoid(lax.dot_general(hg, res[1][...], _NN, preferred_element_type=F32) + res[2][...])
        pe = lax.dot_general(pb, res[3][...], _NN, preferred_element_type=F32)
        outs[0][...] = hg
        outs[1][...] = pb
        outs[2][...] = pg
        outs[3][...] = pe
        outs[4][...] = hv + pg * pe

    return _row_call("ple_fwd" + tag, body, t_len // tm, rows=[(h, tm), (p, tm)], res=[g3, w_pg, b_pg, w_ple],
                     outs=[((t_len, D_MODEL), _MXU_DTYPE, tm), ((t_len, PLE_DIM), _MXU_DTYPE, tm),
                           ((t_len, D_MODEL), F32, tm), ((t_len, D_MODEL), F32, tm), ((t_len, D_MODEL), F32, tm)])


def _ple_bwd(tag, dh, h, pg, pe, g3, w_pg, tm=512):
    t_len = dh.shape[0]

    def body(t, rows, halos, res, outs, accs, scr):
        dhv = rows[0][...]
        hv = rows[1][...]
        pgv = rows[2][...]
        dgp = dhv * rows[3][...] * (pgv * (1.0 - pgv))
        dgb = dgp.astype(_MXU_DTYPE)
        outs[0][...] = dgb
        outs[1][...] = (dhv * pgv).astype(_MXU_DTYPE)
        dhg = lax.dot_general(dgb, res[1][...], _NT, preferred_element_type=F32)
        dres, dg = _rms_bwd(dhg, hv, res[0][...], _rms_stat(hv))
        outs[2][...] = dhv + dres
        accs[0][...] += jnp.sum(dgp, axis=0, keepdims=True)
        accs[1][...] += dg

    return _row_call("ple_bwd" + tag, body, t_len // tm, rows=[(dh, tm), (h, tm), (pg, tm), (pe, tm)],
                     res=[g3, w_pg],
                     outs=[((t_len, D_MODEL), _MXU_DTYPE, tm), ((t_len, D_MODEL), _MXU_DTYPE, tm),
                           ((t_len, D_MODEL), F32, tm)],
                     accs=[(1, D_MODEL), (1, D_MODEL)])


def _inproj_bwd(tag, dh, h, dxn_gate, dprojs, g1, w_groups, tm=512, exchange=()):
    t_len = dh.shape[0]

    def body(t, rows, halos, res, outs, accs, scr):
        hv = rows[1][...]
        dxn = rows[2][...]
        for dp, w in zip(rows[3:], res[1:]):
            dxn = dxn + lax.dot_general(dp[...], w[...], _NN, preferred_element_type=F32)
        dres, dg = _rms_bwd(dxn, hv, res[0][...], _rms_stat(hv))
        outs[0][...] = rows[0][...] + dres
        accs[0][...] += dg

    return _row_call("inproj_bwd" + tag, body, t_len // tm,
                     rows=[(dh, tm), (h, tm), (dxn_gate, tm)] + [(d, tm) for d in dprojs],
                     res=[g1] + list(w_groups), outs=[((t_len, D_MODEL), F32, tm)], accs=[(1, D_MODEL)],
                     exchange=exchange)


def _loss_head(h, target, gf, tm=512):
    t_len = h.shape[0]

    def body(t, rows, halos, res, outs, accs, scr):
        hv = rows[0][...]
        g = res[0][...]
        r = _rms_stat(hv)
        diff = hv * r * g - rows[1][...]
        accs[0][...] += 0.5 * jnp.sum(jnp.mean(diff * diff, axis=-1, keepdims=True), axis=0, keepdims=True)
        dh, dg = _rms_bwd(diff * (1.0 / D_MODEL), hv, g, r)
        outs[0][...] = dh
        accs[1][...] += dg

    return _row_call("loss_head", body, t_len // tm, rows=[(h, tm), (target, tm)], res=[gf],
                     outs=[((t_len, D_MODEL), F32, tm)], accs=[(1, 128), (1, D_MODEL)])


def _row(v):
    return v.reshape(1, -1)


GATHER_DURING = (('inproj', ('w_gate', 'w_branch', 'w_out')), ('gla', ('w_ff1',)), ('attn', ('w_ff2', 'w_ple_gate', 'w_ple')))
SCATTER_DURING_ATTN = ('w_ple_gate', 'w_ple', 'w_ff1', 'w_ff2', 'w_out', 'w_gate', 'w_branch')


def _gather_items(shards, names):
    items = []
    for n in names:
        s = shards[n]
        ax = SHARD_AXIS[n] - 1
        if n == 'w_in':
            items.append(_gather_item(s))
        else:
            items.append(_gather_item(s, s.shape[:ax] + (N_DEV * s.shape[ax],) + s.shape[ax + 1:], ax))
    return items


def _land(w, names, arrays):
    for n, a in zip(names, arrays):
        w[n] = a.reshape(IN_COLS, D_MODEL) if n == 'w_in' else a


def _layer_fwd(i, h, p_i, w, shards, next_shards):
    tag = "_l%d" % i
    during = dict(GATHER_DURING)
    win = [w['w_in'][s:s + n] for _, s, n in IN_GROUPS]
    res = _inproj_fwd(tag, h, _row(w['norm1_g']), win, exchange=_gather_items(shards, during['inproj']))
    xn, pa, pb, pr, pc, pd = res[:6]
    _land(w, during['inproj'], res[6:])
    sg_bt = w['sg_b'].T
    y_a = _sg_fwd(tag, pa, _row(w['sg_ln_g']), _row(w['sg_ln_b']), w['sg_w'], sg_bt)
    res = _gla_fwd(tag, pb, pr, w['gla_w_a2'], _row(w['gla_b_a']), _row(w['gla_norm_g']),
                   exchange=_gather_items(shards, during['gla']))
    y_b, states = res[:2]
    _land(w, during['gla'], res[2:])
    bias = _bias_tile(tag, _bias_expand(tag, w['att_rel_bias']).reshape(ATT_HEADS, CHUNK, BAND))
    items = _gather_items(shards, during['attn']) + (_gather_items(next_shards, ['w_in']) if next_shards else [])
    res = _attn_fwd(tag, pc, bias, exchange=items)
    y_c = res[0]
    _land(w, during['attn'], res[1:1 + len(during['attn'])])
    next_w_in = res[-1].reshape(IN_COLS, D_MODEL) if next_shards else None
    y_d, yc = _conv_fwd(tag, pd, w['conv_dw_w'], _row(w['conv_dw_b']), _row(w['conv_ln_g']), _row(w['conv_ln_b']))
    ys = (y_a, y_b, y_c, y_d)
    gate, bo, merged, h1 = _merge_fwd(tag, h, xn, ys, w['w_gate'], w['b_gate'], w['w_branch'], w['w_out'])
    hn, pre, h2 = _ffn_fwd(tag, h1, _row(w['norm2_g']), w['w_ff1'], w['w_ff2'])
    hg, p_b, pg, pe, h3 = _ple_fwd(tag, h2, p_i, _row(w['norm3_g']), w['w_ple_gate'], _row(w['b_ple_gate']), w['w_ple'])
    saved = dict(h=h, xn=xn, pa=pa, pb=pb, pr=pr, pc=pc, pd=pd, states=states, bias=bias, yc=yc, ys=ys, gate=gate,
                 bo=bo, merged=merged, h1=h1, hn=hn, pre=pre, h2=h2, hg=hg, p_b=p_b, pg=pg, pe=pe, win=win,
                 sg_bt=sg_bt)
    return h3, saved, next_w_in


def _layer_bwd(i, dh3, s, w, tail=None):
    tag = "_l%d" % i
    g = {}
    dgp, dpe, dh2, db_pg, dg3 = _ple_bwd(tag, dh3, s['h2'], s['pg'], s['pe'], _row(w['norm3_g']), w['w_ple_gate'])
    g['b_ple_gate'], g['norm3_g'] = db_pg[0], dg3[0]
    g['w_ple_gate'] = _tn_call("dw_ple_gate" + tag, s['hg'], dgp, D_MODEL, D_MODEL)
    g['w_ple'] = _tn_call("dw_ple" + tag, s['p_b'], dpe, PLE_DIM, D_MODEL)

    dh2b, ffb, dpre, dh1, dg2 = _ffn_bwd(tag, dh2, s['h1'], s['pre'], _row(w['norm2_g']), w['w_ff1'], w['w_ff2'])
    g['norm2_g'] = dg2[0]
    g['w_ff1'] = _tn_call("dw_ff1" + tag, s['hn'], dpre, D_MODEL, FF_COLS, nblk=D_FF // FF_COLS)
    g['w_ff2'] = _tn_call("dw_ff2" + tag, ffb, dh2b, FF_COLS, D_MODEL, nblk=D_FF // FF_COLS, a_col=True, b_col=False,
                          out='rows')

    dh1b, dbo, dgpre, dxn_gate, dy_a, dy_b, dy_c, dy_d, db_gate = _merge_bwd(
        tag, dh1, s['gate'], s['bo'], w['w_gate'], w['w_branch'], w['w_out'])
    g['b_gate'] = db_gate.reshape(4, D_MODEL)
    g['w_out'] = _tn_call("dw_out" + tag, s['merged'], dh1b, D_MODEL, D_MODEL)
    g['w_gate'] = _tn_call("dw_gate" + tag, s['xn'], dgpre, D_MODEL, D_MODEL, nblk=4, out='stack')
    g['w_branch'] = jnp.stack([_tn_call("dw_branch%d%s" % (n, tag), s['ys'][n], dbo, BW, D_MODEL, b_off=n)
                             for n in range(4)])

    lg, lb = _row(w['sg_ln_g']), _row(w['sg_ln_b'])
    dpa, dsg_w, dsg_bt, dlg, dlb = _sg_bwd(tag, s['pa'], dy_a, lg, lb, w['sg_w'], s['sg_bt'])
    g['sg_w'], g['sg_b'], g['sg_ln_g'], g['sg_ln_b'] = dsg_w, dsg_bt.T, dlg[0], dlb[0]

    dpb, dpr, dwa2, dba, dng = _gla_bwd(tag, s['pb'], s['pr'], dy_b, s['states'], w['gla_w_a2'],
                                        _row(w['gla_b_a']), _row(w['gla_norm_g']))
    g['gla_w_a2'], g['gla_b_a'], g['gla_norm_g'] = dwa2, dba[0], dng[0]

    items = [_scatter_item(g.pop(n), axis=SHARD_AXIS[n] - 1) for n in SCATTER_DURING_ATTN]
    res = _attn_bwd(tag, s['pc'], dy_c, s['bias'], exchange=items)
    dq, dkv_own, dkv_prev, dkv_prev2, dbias = res[:5]
    parts = dict(zip(SCATTER_DURING_ATTN, res[5:]))
    dpc = _attn_combine(tag, dq, dkv_own, dkv_prev, dkv_prev2)
    g['att_rel_bias'] = _bias_reduce(tag, _bias_untile(tag, dbias).reshape(ATT_HEADS, CHUNK * BAND))

    cg, cb = _row(w['conv_ln_g']), _row(w['conv_ln_b'])
    dyc, dcg, dcb, ddwb = _conv_bwd_norm(tag, s['yc'], dy_d, cg, cb)
    dpd, ddw = _conv_bwd_taps(tag, s['pd'], dyc, w['conv_dw_w'])
    g['conv_ln_g'], g['conv_ln_b'], g['conv_dw_b'], g['conv_dw_w'] = dcg[0], dcb[0], ddwb[0], ddw

    dprojs = (dpa, dpb, dpr, dpc, dpd)
    dw_in = jnp.concatenate([_tn_call("dw_in%s%s" % (name, tag), dp, s['xn'], n, D_MODEL)
                             for (name, _, n), dp in zip(IN_GROUPS, dprojs)], axis=0)
    items = [_scatter_item(dw_in.reshape(N_DEV, IN_COLS // N_DEV, D_MODEL))] + (tail(g) if tail else [])
    res = _inproj_bwd(tag, dh1, s['h'], dxn_gate, dprojs, _row(w['norm1_g']), s['win'], exchange=items)
    dh0, dg1 = res[:2]
    g['norm1_g'] = dg1[0]
    parts['w_in'] = res[2]
    return dh0, g, parts, res[3:]


def _local_step(x, p, target, final_g, layers, shards, tail):
    h = x
    saved = []
    for i in range(DEPTH):
        nxt = shards[i + 1] if i + 1 < DEPTH else None
        h, s, next_w_in = _layer_fwd(i, h, p[i], layers[i], shards[i], nxt)
        saved.append(s)
        if nxt:
            layers[i + 1]['w_in'] = next_w_in
    dh, loss, dgf = _loss_head(h, target, _row(final_g))
    small, parts, tail_out = [None] * DEPTH, [None] * DEPTH, None
    for i in reversed(range(DEPTH)):
        hook = (lambda g: tail([g] + small[1:])) if i == 0 else None
        dh, small[i], parts[i], out = _layer_bwd(i, dh, saved[i], layers[i], hook)
        if i == 0:
            tail_out = out
    return loss[0, 0], dh, dgf[0], small, parts, tail_out


def _peers():
    x, y, c = lax.axis_index("x"), lax.axis_index("y"), lax.axis_index("c")
    me = 4 * x + 2 * y + c
    out = []
    for k in range(1, N_DEV):
        px = (1 - x) if k & 4 else x
        py = (1 - y) if k & 2 else y
        pc = (1 - c) if k & 1 else c
        out.append((k - 1, (px, py, pc), 4 * px + 2 * py + pc))
    return me, out


def _block(ref, axis, idx, width):
    ix = [slice(None)] * len(ref.shape)
    ix[axis] = pl.ds(pl.multiple_of(idx * width, width), width)
    return ref.at[tuple(ix)]


def _slot(ref, idx):
    return ref.at[idx]


def _whole(ref, idx):
    return ref


def _gather_item(src, out_shape=None, axis=None):
    if axis is None:
        return dict(src=src, out=(N_DEV,) + src.shape, take=_whole, put=_slot)
    return dict(src=src, out=tuple(out_shape), take=_whole,
                put=lambda ref, s: _block(ref, axis, s, src.shape[axis]))


def _scatter_item(src, axis=None, lead=0):
    if axis is None:
        shape = src.shape[:lead] + src.shape[lead + 1:]
        take = lambda ref, s: ref.at[(slice(None),) * lead + (s,)]
    else:
        width = src.shape[axis] // N_DEV
        shape = src.shape[:axis] + (width,) + src.shape[axis + 1:]
        take = lambda ref, s: _block(ref, axis, s, width)
    return dict(src=src, out=(N_DEV,) + shape, take=take, put=_slot)


def _exchange_sems(n):
    return [pltpu.SemaphoreType.DMA((n * (N_DEV - 1),)), pltpu.SemaphoreType.DMA((n * (N_DEV - 1),)),
            pltpu.SemaphoreType.DMA((n,))]


def _exchange_copies(items, src_refs, out_refs, sems, start):
    send_sems, recv_sems, local_sems = sems
    me, peers = _peers()

    def remote(i, k, pos, receiver, sender):
        it = items[i]
        return pltpu.make_async_remote_copy(
            src_ref=it['take'](src_refs[i], receiver), dst_ref=it['put'](out_refs[i], sender),
            send_sem=send_sems.at[i * (N_DEV - 1) + k], recv_sem=recv_sems.at[i * (N_DEV - 1) + k],
            device_id=pos, device_id_type=pl.DeviceIdType.MESH)

    local = [pltpu.make_async_copy(it['take'](src_refs[i], me), it['put'](out_refs[i], me), local_sems.at[i])
             for i, it in enumerate(items)]
    if start:
        for cp in local:
            cp.start()
        for k, pos, flat in peers:
            for i in range(len(items)):
                remote(i, k, pos, flat, me).start()
    else:
        for k, pos, flat in peers:
            for i in range(len(items)):
                remote(i, k, pos, flat, flat).wait_recv()
        for k, pos, flat in peers:
            for i in range(len(items)):
                remote(i, k, pos, flat, me).wait_send()
        for cp in local:
            cp.wait()


def _exchange(name, items):
    n = len(items)

    def body(*refs):
        _exchange_copies(items, refs[:n], refs[n:2 * n], refs[2 * n:], start=True)
        _exchange_copies(items, refs[:n], refs[n:2 * n], refs[2 * n:], start=False)

    any_spec = pl.BlockSpec(memory_space=pl.ANY)
    return pl.pallas_call(
        body, out_shape=[jax.ShapeDtypeStruct(it['out'], it['src'].dtype) for it in items],
        in_specs=[any_spec] * n, out_specs=[any_spec] * n, scratch_shapes=_exchange_sems(n),
        name=name)(*[it['src'] for it in items])


def _pack(arrays, dtype, lead=None):
    flat = [a.astype(dtype).reshape((lead, -1) if lead else (-1,)) for a in arrays]
    cat = jnp.concatenate(flat, axis=-1)
    n = cat.shape[-1]
    rows = -(-n // (PACK_COLS * SUBLANES)) * SUBLANES
    pad = rows * PACK_COLS - n
    if pad:
        cat = jnp.pad(cat, ((0, 0), (0, pad)) if lead else ((0, pad),))
    return cat.reshape((lead, rows, PACK_COLS) if lead else (rows, PACK_COLS))


def _unpack(buf, shapes, lead=None):
    flat = buf.reshape((lead, -1) if lead else (-1,))
    out, off = [], 0
    for shp in shapes:
        n = int(np.prod(shp))
        piece = flat[..., off:off + n]
        out.append(piece.reshape(((lead,) if lead else ()) + tuple(shp)))
        off += n
    return out


def _to_slabs(full, axis):
    shp = full.shape
    split = full.reshape(shp[:axis] + (N_DEV, shp[axis] // N_DEV) + shp[axis + 1:])
    return jnp.moveaxis(split, axis, 0)


def _from_slabs(slabs, axis):
    moved = jnp.moveaxis(slabs, 0, axis)
    shp = moved.shape
    return moved.reshape(shp[:axis] + (shp[axis] * shp[axis + 1],) + shp[axis + 2:])


def _adamw_block(r, c):
    if r % 8:
        return r, 256
    br = min(r, max(8, ADAMW_TILE * PACK_COLS // c))
    while r % br:
        br //= 2
    return br, c


def _adamw(name, partials, w, m, v):
    n_lead, r, c = w.shape
    br, bc = _adamw_block(r, c)
    ni, nj = r // br, c // bc
    c1 = 1.0 - ADAM_B1 ** ADAM_STEP
    c2 = 1.0 - ADAM_B2 ** ADAM_STEP

    def kern(*refs):
        p_refs = refs[:n_lead]
        w_ref, m_ref, v_ref, g_ref, d_ref, nm_ref, nv_ref = refs[n_lead:]
        layer = pl.program_id(0)
        g = None
        for l, p_ref in enumerate(p_refs):
            gl = p_ref[0].astype(F32)
            for s in range(1, N_DEV):
                gl = gl + p_ref[s].astype(F32)
            g = gl if g is None else jnp.where(layer == l, gl, g)
        nm = ADAM_B1 * m_ref[...] + (1.0 - ADAM_B1) * g
        nv = ADAM_B2 * v_ref[...] + (1.0 - ADAM_B2) * jnp.square(g)
        g_ref[...] = g
        nm_ref[...] = nm
        nv_ref[...] = nv
        d_ref[...] = -ADAM_LR * ((nm / c1) / (jnp.sqrt(nv / c2) + ADAM_EPS) + ADAM_WD * w_ref[...])

    def part_spec(mine):
        def index(l, i, j):
            before, after = l < mine, l > mine
            return (0, jnp.where(before, 0, jnp.where(after, ni - 1, i)), jnp.where(before, 0, jnp.where(after, nj - 1, j)))
        return pl.BlockSpec((N_DEV, br, bc), index)

    blk = pl.BlockSpec((None, br, bc), lambda l, i, j: (l, i, j))
    return pl.pallas_call(
        kern, grid=(n_lead, ni, nj),
        in_specs=[part_spec(l) for l in range(n_lead)] + [blk, blk, blk],
        out_specs=[blk] * 4, out_shape=[jax.ShapeDtypeStruct(w.shape, F32)] * 4,
        compiler_params=pltpu.CompilerParams(dimension_semantics=("arbitrary",) * 3),
        name=name)(*partials, w, m, v)


def _as_rows(a, lead):
    return a.reshape(a.shape[:lead] + (-1, a.shape[-1]))


def kernel(x, p, norm1_g, w_in, sg_ln_g, sg_ln_b, sg_w, sg_b, gla_w_a2, gla_b_a, gla_norm_g, att_rel_bias, conv_dw_w, conv_dw_b, conv_ln_g, conv_ln_b, w_branch, w_gate, b_gate, w_out, norm2_g, w_ff1, w_ff2, norm3_g, w_ple_gate, b_ple_gate, w_ple, final_g, loss_target, m_norm1_g, m_w_in, m_sg_ln_g, m_sg_ln_b, m_sg_w, m_sg_b, m_gla_w_a2, m_gla_b_a, m_gla_norm_g, m_att_rel_bias, m_conv_dw_w, m_conv_dw_b, m_conv_ln_g, m_conv_ln_b, m_w_branch, m_w_gate, m_b_gate, m_w_out, m_norm2_g, m_w_ff1, m_w_ff2, m_norm3_g, m_w_ple_gate, m_b_ple_gate, m_w_ple, m_final_g, v_norm1_g, v_w_in, v_sg_ln_g, v_sg_ln_b, v_sg_w, v_sg_b, v_gla_w_a2, v_gla_b_a, v_gla_norm_g, v_att_rel_bias, v_conv_dw_w, v_conv_dw_b, v_conv_ln_g, v_conv_ln_b, v_w_branch, v_w_gate, v_b_gate, v_w_out, v_norm2_g, v_w_ff1, v_w_ff2, v_norm3_g, v_w_ple_gate, v_b_ple_gate, v_w_ple, v_final_g):
    args = locals()
    wts = {n: args[n] for n in WEIGHTS}
    mom = {n: args['m_' + n] for n in WEIGHTS}
    var = {n: args['v_' + n] for n in WEIGHTS}

    local = {d_name: dict(d, w_in=jnp.swapaxes(d['w_in'], 1, 2))
             for d_name, d in (("w", wts), ("m", mom), ("v", var))}
    shards = [{n: local["w"][n][i].astype(_MXU_DTYPE) for n in MXU_WEIGHTS} for i in range(DEPTH)]

    first_w_in, vec = _exchange("gather_first_weights", _gather_items(shards[0], ['w_in'])
                                + [_gather_item(_pack([wts[n] for n in VEC_WEIGHTS], F32))])
    vec_full = {n: _from_slabs(slabs, SHARD_AXIS[n])
                for n, slabs in zip(VEC_WEIGHTS, _unpack(vec, [wts[n].shape for n in VEC_WEIGHTS], lead=N_DEV))}
    small_names = [n for n in WEIGHTS if n not in MXU_WEIGHTS and n != 'final_g']
    layers = [{n: (vec_full[n] if n in vec_full else wts[n])[i] for n in small_names} for i in range(DEPTH)]
    _land(layers[0], ['w_in'], [first_w_in])

    def vec_items(small):
        stacked = [jnp.stack([g[n] for g in small]) for n in VEC_WEIGHTS]
        return [_scatter_item(_pack([_to_slabs(a, SHARD_AXIS[n]) for n, a in zip(VEC_WEIGHTS, stacked)], F32,
                                    lead=N_DEV))]

    loss, grad_x, dgf, small, parts, (vec_parts,) = _local_step(x[0], p[:, 0], loss_target[0], final_g, layers,
                                                               shards, vec_items)
    loss = lax.psum(loss, ("x", "y", "c"))

    grads = {n: jnp.stack([small[i][n] for i in range(DEPTH)]) for n in REPLICATED if n != 'final_g'}
    grads['final_g'] = dgf
    repl_parts, = _exchange("gather_replicated_grads", [_gather_item(_pack([grads[n] for n in REPLICATED], F32))])

    results = {}
    for n in MXU_WEIGHTS:
        w3, m3, v3 = (_as_rows(local[d][n], 1) for d in ("w", "m", "v"))
        outs = _adamw("adamw_" + n, [parts[i][n].reshape((N_DEV,) + w3.shape[1:]) for i in range(DEPTH)], w3, m3, v3)
        for kind, a in zip(("grad", "delta", "new_m", "new_v"), outs):
            a = a.reshape(local["w"][n].shape)
            results[kind, n] = jnp.swapaxes(a, 1, 2) if n == 'w_in' else a
    for names, part, call in ((VEC_WEIGHTS, vec_parts, "adamw_vec"), (REPLICATED, repl_parts, "adamw_replicated")):
        packed = [_pack([d[n] for n in names], F32)[None] for d in (wts, mom, var)]
        outs = _adamw(call, [part], *packed)
        for kind, buf in zip(("grad", "delta", "new_m", "new_v"), outs):
            for n, a in zip(names, _unpack(buf[0], [wts[n].shape for n in names])):
                results[kind, n] = a
    return (loss, grad_x[None]) + tuple(results[kind, n] for kind in ("grad", "delta", "new_m", "new_v")
                                        for n in WEIGHTS)
```

```python
import functools

import numpy as np
import jax
import jax.numpy as jnp
from jax import lax
from jax.experimental import pallas as pl
from jax.experimental.pallas import tpu as pltpu

F32 = jnp.float32
_MXU_DTYPE = jnp.bfloat16
GRAD_DTYPE = jnp.bfloat16

N_DEV = 8
D_MODEL = 1024
DEPTH = 2
CHUNK = 64
PLE_DIM = 256
BW = 512
SG_BLOCK = 128
SG_GROUPS = 4
GLA_HEADS = 4
GLA_DK = 64
GLA_DV = 128
GLA_RANK = 16
GLA_TAU = 16.0
ATT_HEADS = 8
ATT_HD = 64
ATT_BAND = 9
BAND = ATT_BAND * CHUNK
MAX_REL = 256
REL_TABLE = CHUNK + MAX_REL
CONV_K = 31
CONV_HALO = 32
D_FF = 4096
EPS = 1e-6
NEG_INF = -1e30

IN_GROUPS = (("A", 0, 1024), ("B", 1024, 1536), ("a", 2560, 16), ("C", 2576, 1536), ("D", 4112, 1024))
IN_COLS = 5136

ADAM_LR = 0.001
ADAM_B1 = 0.9
ADAM_B2 = 0.999
ADAM_EPS = 1e-08
ADAM_WD = 0.01
ADAM_STEP = 10

ADAMW_TILE = 128
PACK_COLS = 1024
VMEM_LIMIT_MB = 56

_NN = (((1,), (0,)), ((), ()))
_NT = (((1,), (1,)), ((), ()))
_TN = (((0,), (0,)), ((), ()))

WEIGHTS = ['norm1_g', 'w_in', 'sg_ln_g', 'sg_ln_b', 'sg_w', 'sg_b', 'gla_w_a2', 'gla_b_a', 'gla_norm_g',
           'att_rel_bias', 'conv_dw_w', 'conv_dw_b', 'conv_ln_g', 'conv_ln_b', 'w_branch', 'w_gate', 'b_gate',
           'w_out', 'norm2_g', 'w_ff1', 'w_ff2', 'norm3_g', 'w_ple_gate', 'b_ple_gate', 'w_ple', 'final_g']
SHARD_AXIS = {'w_in': 2, 'gla_w_a2': 2, 'att_rel_bias': 2, 'conv_dw_w': 2, 'w_branch': 3, 'w_gate': 2,
              'b_gate': 2, 'w_out': 1, 'w_ff1': 2, 'w_ff2': 1, 'w_ple_gate': 1, 'w_ple': 2}
MXU_WEIGHTS = ('w_in', 'w_branch', 'w_gate', 'w_out', 'w_ff1', 'w_ff2', 'w_ple_gate', 'w_ple')
VEC_WEIGHTS = ('gla_w_a2', 'att_rel_bias', 'conv_dw_w', 'b_gate')
SHARDED = tuple(n for n in WEIGHTS if n in SHARD_AXIS)
REPLICATED = tuple(n for n in WEIGHTS if n not in SHARD_AXIS)


def _mm(a, b, dims=_NN):
    return lax.dot_general(a.astype(_MXU_DTYPE), b.astype(_MXU_DTYPE), dims, preferred_element_type=F32)


def _split3(x):
    x1 = x.astype(jnp.bfloat16)
    r1 = x - x1.astype(F32)
    x2 = r1.astype(jnp.bfloat16)
    x3 = (r1 - x2.astype(F32)).astype(jnp.bfloat16)
    return x1, x2, x3


def _mm_exact_rhs(m, x, dims=_NN):
    return sum(lax.dot_general(m, xi, dims, preferred_element_type=F32) for xi in _split3(x))


def _mm_exact_lhs(x, m, dims=_NN):
    return sum(lax.dot_general(xi, m, dims, preferred_element_type=F32) for xi in _split3(x))


def _sigmoid(x):
    return 1.0 / (1.0 + jnp.exp(-x))


def _gelu(x):
    c = 0.7978845608028654
    t = jnp.tanh(c * (x + 0.044715 * x * x * x))
    return 0.5 * x * (1.0 + t), t


def _gelu_grad(x, t):
    c = 0.7978845608028654
    return 0.5 * (1.0 + t) + 0.5 * x * (1.0 - t * t) * c * (1.0 + 3.0 * 0.044715 * x * x)


def _rms_stat(h):
    return lax.rsqrt(jnp.mean(h * h, axis=-1, keepdims=True) + EPS)


def _rms_bwd(dy, h, g, r):
    hh = h * r
    dhh = dy * g
    dh = r * (dhh - hh * jnp.mean(dhh * hh, axis=-1, keepdims=True))
    return dh, jnp.sum(dy * hh, axis=0, keepdims=True)


def _ln_fwd(x, g, b):
    mu = jnp.mean(x, axis=-1, keepdims=True)
    xc = x - mu
    rs = lax.rsqrt(jnp.mean(xc * xc, axis=-1, keepdims=True) + EPS)
    xh = xc * rs
    return xh * g + b, xh, rs


def _ln_bwd(dy, xh, rs, g):
    dxh = dy * g
    dx = rs * (dxh - jnp.mean(dxh, axis=-1, keepdims=True) - xh * jnp.mean(dxh * xh, axis=-1, keepdims=True))
    return dx, jnp.sum(dy * xh, axis=0, keepdims=True), jnp.sum(dy, axis=0, keepdims=True)


def _row_call(name, body, nt, rows=(), halos=(), res=(), outs=(), accs=(), scratch=(), reverse=False, exchange=()):
    def pos(i):
        return (nt - 1 - i) if reverse else i

    def lead(ndim, f):
        return lambda i: (f(pos(i)),) + (0,) * (ndim - 1)

    in_specs, operands = [], []
    for a, tile in rows:
        in_specs.append(pl.BlockSpec((tile,) + a.shape[1:], lead(a.ndim, lambda t: t)))
        operands.append(a)
    for a, blk, per, side in halos:
        last = a.shape[0] // blk - 1
        delta = {'prev2': -2, 'prev': -1, 'next': per, 'next2': per + 1}[side]
        f = lambda t, per=per, last=last, delta=delta: jnp.clip(t * per + delta, 0, last)
        in_specs.append(pl.BlockSpec((blk,) + a.shape[1:], lead(a.ndim, f)))
        operands.append(a)
    for a in res:
        in_specs.append(pl.BlockSpec(a.shape, lambda i, nd=a.ndim: (0,) * nd, pipeline_mode=pl.Buffered(1)))
        operands.append(a)
    out_specs, out_shape = [], []
    for shape, dtype, tile in outs:
        out_specs.append(pl.BlockSpec((tile,) + tuple(shape[1:]), lead(len(shape), lambda t: t)))
        out_shape.append(jax.ShapeDtypeStruct(tuple(shape), dtype))
    for shape in accs:
        out_specs.append(pl.BlockSpec(tuple(shape), lambda i, nd=len(shape): (0,) * nd))
        out_shape.append(jax.ShapeDtypeStruct(tuple(shape), F32))
    nx = len(exchange)
    any_spec = pl.BlockSpec(memory_space=pl.ANY)
    for it in exchange:
        in_specs.append(any_spec)
        operands.append(it['src'])
        out_specs.append(any_spec)
        out_shape.append(jax.ShapeDtypeStruct(it['out'], it['src'].dtype))
    sizes = (len(rows), len(halos), len(res), nx, len(outs), len(accs), nx, len(scratch), 3 if nx else 0)

    def kern(*refs):
        i = pl.program_id(0)
        groups, at = [], 0
        for n in sizes:
            groups.append(refs[at:at + n])
            at += n
        row_refs, halo_refs, res_refs, x_src, out_refs, acc_refs, x_dst, scr_refs, sems = groups

        @pl.when(i == 0)
        def _():
            for r in tuple(acc_refs) + tuple(scr_refs):
                r[...] = jnp.zeros(r.shape, r.dtype)
            if nx:
                _exchange_copies(exchange, x_src, x_dst, sems, start=True)

        body(pos(i), row_refs, halo_refs, res_refs, out_refs, acc_refs, scr_refs)

        if nx:
            @pl.when(i == nt - 1)
            def _():
                _exchange_copies(exchange, x_src, x_dst, sems, start=False)

    result = pl.pallas_call(
        kern, grid=(nt,), in_specs=in_specs, out_specs=out_specs, out_shape=out_shape,
        scratch_shapes=[pltpu.VMEM(tuple(s), d) for s, d in scratch] + (_exchange_sems(nx) if nx else []),
        compiler_params=pltpu.CompilerParams(dimension_semantics=("arbitrary",),
                                             vmem_limit_bytes=VMEM_LIMIT_MB << 20),
        name=name)(*operands)
    return tuple(result)


def _tn_call(name, a, b, k, n, nblk=1, a_col=False, b_col=True, b_off=0, out='cols', tile=2048):
    tile = min(tile, a.shape[0])
    nt = a.shape[0] // tile
    if out == 'cols':
        o_shape, o_spec = (k, nblk * n), pl.BlockSpec((k, n), lambda j, t: (0, j))
    elif out == 'rows':
        o_shape, o_spec = (nblk * k, n), pl.BlockSpec((k, n), lambda j, t: (j, 0))
    else:
        o_shape, o_spec = (nblk, k, n), pl.BlockSpec((None, k, n), lambda j, t: (j, 0, 0))

    def kern(a_ref, b_ref, o_ref, acc):
        @pl.when(pl.program_id(1) == 0)
        def _():
            acc[...] = jnp.zeros(acc.shape, acc.dtype)

        acc[...] += lax.dot_general(a_ref[...], b_ref[...], _TN, preferred_element_type=F32)

        @pl.when(pl.program_id(1) == nt - 1)
        def _():
            o_ref[...] = acc[...].astype(o_ref.dtype)

    return pl.pallas_call(
        kern, grid=(nblk, nt),
        in_specs=[pl.BlockSpec((tile, k), (lambda j, t: (t, j)) if a_col else (lambda j, t: (t, 0))),
                  pl.BlockSpec((tile, n), (lambda j, t: (t, j + b_off)) if b_col else (lambda j, t: (t, b_off)))],
        out_specs=o_spec, out_shape=jax.ShapeDtypeStruct(o_shape, GRAD_DTYPE),
        scratch_shapes=[pltpu.VMEM((k, n), F32)],
        compiler_params=pltpu.CompilerParams(dimension_semantics=("arbitrary", "arbitrary"),
                                             vmem_limit_bytes=VMEM_LIMIT_MB << 20),
        name=name)(a, b)


def _inproj_fwd(tag, h, g1, w_groups, tm=512, exchange=()):
    t_len = h.shape[0]

    def body(t, rows, halos, res, outs, accs, scr):
        hv = rows[0][...]
        xn = (hv * _rms_stat(hv) * res[0][...]).astype(_MXU_DTYPE)
        outs[0][...] = xn
        for o, w in zip(outs[1:], res[1:]):
            o[...] = lax.dot_general(xn, w[...], _NT, preferred_element_type=F32)

    outs = [((t_len, D_MODEL), _MXU_DTYPE, tm)] + [((t_len, w.shape[0]), F32, tm) for w in w_groups]
    return _row_call("inproj_fwd" + tag, body, t_len // tm, rows=[(h, tm)], res=[g1] + list(w_groups), outs=outs,
                     exchange=exchange)


def _sg_mask():
    row = lax.broadcasted_iota(jnp.int32, (SG_BLOCK, SG_BLOCK), 0)
    col = lax.broadcasted_iota(jnp.int32, (SG_BLOCK, SG_BLOCK), 1)
    return jnp.logical_or(row >= CHUNK, col < CHUNK)


def _sg_forward_parts(pa, lg, lb, w_ref, bt):
    tm = pa.shape[0]
    nb = tm // SG_BLOCK
    su, sv = pa[:, :BW], pa[:, BW:]
    u, tu = _gelu(su)
    gv, tv = _gelu(sv)
    vn, xh, rs = _ln_fwd(gv, lg, lb)
    mask = _sg_mask()
    wms, xs, ms = [], [], []
    for g in range(SG_GROUPS):
        wm = jnp.where(mask, w_ref[g], 0.0).astype(_MXU_DTYPE)
        xg = jnp.concatenate([vn[b * SG_BLOCK:(b + 1) * SG_BLOCK, g * 128:(g + 1) * 128] for b in range(nb)], axis=1)
        xg = xg.astype(_MXU_DTYPE)
        ms.append(lax.dot_general(wm, xg, _NN, preferred_element_type=F32) + bt[:, g:g + 1])
        wms.append(wm)
        xs.append(xg)
    mixed = _sg_unfold(ms, nb)
    return su, sv, u, tu, tv, xh, rs, wms, xs, mixed


def _sg_unfold(per_group, nb):
    return jnp.concatenate(
        [jnp.concatenate([per_group[g][:, b * 128:(b + 1) * 128] for g in range(SG_GROUPS)], axis=1)
         for b in range(nb)], axis=0)


def _sg_fwd(tag, proj_a, lg, lb, sg_w, sg_bt, tm=512):
    t_len = proj_a.shape[0]

    def body(t, rows, halos, res, outs, accs, scr):
        parts = _sg_forward_parts(rows[0][...], res[0][...], res[1][...], res[2], res[3][...])
        outs[0][...] = (parts[2] * parts[-1]).astype(_MXU_DTYPE)

    return _row_call("sg_fwd" + tag, body, t_len // tm, rows=[(proj_a, tm)], res=[lg, lb, sg_w, sg_bt],
                     outs=[((t_len, BW), _MXU_DTYPE, tm)])[0]


def _sg_bwd(tag, proj_a, dy, lg, lb, sg_w, sg_bt, tm=512):
    t_len = proj_a.shape[0]
    nb = tm // SG_BLOCK

    def body(t, rows, halos, res, outs, accs, scr):
        lgv = res[0][...]
        su, sv, u, tu, tv, xh, rs, wms, xs, mixed = _sg_forward_parts(rows[0][...], lgv, res[1][...], res[2], res[3][...])
        dyv = rows[1][...]
        dsu = dyv * mixed * _gelu_grad(su, tu)
        dmixed = dyv * u
        mask = _sg_mask()
        dxs, dbs = [], []
        for g in range(SG_GROUPS):
            dm = jnp.concatenate([dmixed[b * SG_BLOCK:(b + 1) * SG_BLOCK, g * 128:(g + 1) * 128] for b in range(nb)],
                                 axis=1)
            dmb = dm.astype(_MXU_DTYPE)
            dw = lax.dot_general(dmb, xs[g], _NT, preferred_element_type=F32)
            accs[0][g] += jnp.where(mask, dw, 0.0)
            dbs.append(jnp.sum(dm, axis=1, keepdims=True))
            dxs.append(lax.dot_general(wms[g], dmb, _TN, preferred_element_type=F32))
        accs[1][...] += jnp.concatenate(dbs, axis=1)
        dvn = _sg_unfold(dxs, nb)
        dgv, dlg, dlb = _ln_bwd(dvn, xh, rs, lgv)
        accs[2][...] += dlg
        accs[3][...] += dlb
        dsv = dgv * _gelu_grad(sv, tv)
        outs[0][...] = jnp.concatenate([dsu, dsv], axis=1).astype(_MXU_DTYPE)

    return _row_call("sg_bwd" + tag, body, t_len // tm, rows=[(proj_a, tm), (dy, tm)], res=[lg, lb, sg_w, sg_bt],
                     outs=[((t_len, 2 * BW), _MXU_DTYPE, tm)],
                     accs=[(SG_GROUPS, SG_BLOCK, SG_BLOCK), (SG_BLOCK, SG_GROUPS), (1, BW), (1, BW)])


def _chunk_matrix(tm, kind):
    row = lax.broadcasted_iota(jnp.int32, (tm, tm), 0)
    col = lax.broadcasted_iota(jnp.int32, (tm, tm), 1)
    same = lax.shift_right_logical(row, 6) == lax.shift_right_logical(col, 6)
    if kind == 'cumsum':
        same = jnp.logical_and(same, row >= col)
    elif kind == 'revsum':
        same = jnp.logical_and(same, row <= col)
    return same.astype(jnp.bfloat16)


def _gla_gate(pa, wa2, ba):
    z = _mm(pa, wa2) + ba
    log_a = (jnp.minimum(z, 0.0) - jnp.log(1.0 + jnp.exp(-jnp.abs(z)))) * (1.0 / GLA_TAU)
    return z, log_a


def _gla_decay(pb, log_a):
    tm = pb.shape[0]
    cum = _mm_exact_rhs(_chunk_matrix(tm, 'cumsum'), log_a)
    tot = _mm_exact_rhs(_chunk_matrix(tm, 'total'), log_a)
    w = jnp.exp(tot - cum)
    return w, pb[:, 256:512] * w, jnp.exp(tot)


def _per_head(fn):
    return jnp.concatenate([fn(h) for h in range(GLA_HEADS)], axis=1)


def _gla_read(qs, sb, c):
    rows = slice(c * CHUNK, (c + 1) * CHUNK)
    return _per_head(lambda h: lax.dot_general(qs[rows, h * 64:(h + 1) * 64], sb[:, h * 64:(h + 1) * 64], _NT,
                                               preferred_element_type=F32))


def _gla_fwd(tag, proj_b, proj_a, wa2, ba, ng, tm=256, exchange=()):
    t_len = proj_b.shape[0]
    cpt = tm // CHUNK

    def body(t, rows, halos, res, outs, accs, scr):
        pb = rows[0][...]
        _, log_a = _gla_gate(rows[1][...], res[0][...], res[1][...])
        _, kd, dec = _gla_decay(pb, log_a)
        kdb = kd.astype(_MXU_DTYPE)
        vb = pb[:, 512:1024].astype(_MXU_DTYPE)
        qs = (pb[:, 0:256] * (GLA_DK ** -0.5)).astype(_MXU_DTYPE)
        uts = []
        for c in range(cpt):
            rs = slice(c * CHUNK, (c + 1) * CHUNK)
            uts.append(_per_head(lambda h: lax.dot_general(vb[rs, h * 128:(h + 1) * 128], kdb[rs, h * 64:(h + 1) * 64],
                                                           _TN, preferred_element_type=F32)))
        s_new = scr[0][...]
        o = []
        for c in range(cpt):
            s_new = dec[c * CHUNK:c * CHUNK + 1] * s_new + uts[c]
            outs[1][c] = s_new
            o.append(_gla_read(qs, s_new.astype(_MXU_DTYPE), c))
        scr[0][...] = s_new
        o = jnp.concatenate(o, axis=0)
        on = _per_head(lambda h: o[:, h * 128:(h + 1) * 128] * lax.rsqrt(
            jnp.mean(jnp.square(o[:, h * 128:(h + 1) * 128]), axis=-1, keepdims=True) + EPS))
        r = pb[:, 1024:1536]
        outs[0][...] = (on * res[2][...] * (r * _sigmoid(r))).astype(_MXU_DTYPE)

    return _row_call("gla_fwd" + tag, body, t_len // tm, rows=[(proj_b, tm), (proj_a, tm)], res=[wa2, ba, ng],
                     outs=[((t_len, BW), _MXU_DTYPE, tm), ((t_len // CHUNK, GLA_DV, 256), F32, cpt)],
                     scratch=[((GLA_DV, 256), F32)], exchange=exchange)


def _gla_bwd(tag, proj_b, proj_a, dy, states, wa2, ba, ng, tm=256):
    t_len = proj_b.shape[0]
    cpt = tm // CHUNK

    def body(t, rows, halos, res, outs, accs, scr):
        pb = rows[0][...]
        pa = rows[1][...]
        dyv = rows[2][...]
        st_ref = rows[3]
        wa2v = res[0][...]
        z, log_a = _gla_gate(pa, wa2v, res[1][...])
        ngv = res[2][...]
        w, kd, dec = _gla_decay(pb, log_a)
        kdb = kd.astype(_MXU_DTYPE)
        vb = pb[:, 512:1024].astype(_MXU_DTYPE)
        qs = (pb[:, 0:256] * (GLA_DK ** -0.5)).astype(_MXU_DTYPE)
        chunks = [slice(c * CHUNK, (c + 1) * CHUNK) for c in range(cpt)]
        sbs = [st_ref[c].astype(_MXU_DTYPE) for c in range(cpt)]
        o = jnp.concatenate([_gla_read(qs, sbs[c], c) for c in range(cpt)], axis=0)
        r = pb[:, 1024:1536]
        sig = _sigmoid(r)
        sil = r * sig
        dos, ons = [], []
        for h in range(GLA_HEADS):
            hs = slice(h * 128, (h + 1) * 128)
            oh = o[:, hs]
            rstd = lax.rsqrt(jnp.mean(oh * oh, axis=-1, keepdims=True) + EPS)
            on = oh * rstd
            don = dyv[:, hs] * ngv[:, hs] * sil[:, hs]
            dos.append(rstd * (don - on * jnp.mean(don * on, axis=-1, keepdims=True)))
            ons.append(on)
        on = jnp.concatenate(ons, axis=1)
        accs[2][...] += jnp.sum(dyv * on * sil, axis=0, keepdims=True)
        dr = dyv * on * ngv * (sig * (1.0 + r * (1.0 - sig)))
        dob = jnp.concatenate(dos, axis=1).astype(_MXU_DTYPE)
        reads, dqs = [], []
        for c, rs in enumerate(chunks):
            reads.append(_per_head(lambda h: lax.dot_general(dob[rs, h * 128:(h + 1) * 128], qs[rs, h * 64:(h + 1) * 64],
                                                             _TN, preferred_element_type=F32)))
            dqs.append(_per_head(lambda h: lax.dot_general(dob[rs, h * 128:(h + 1) * 128], sbs[c][:, h * 64:(h + 1) * 64],
                                                           _NN, preferred_element_type=F32)))
        dst = scr[0][...]
        dubs, ddecs = [None] * cpt, [None] * cpt
        for c in reversed(range(cpt)):
            dst_tot = dst + reads[c]
            s_prev = st_ref[c - 1] if c > 0 else jnp.where(t > 0, halos[0][0], 0.0)
            ddecs[c] = jnp.broadcast_to(jnp.sum(dst_tot * s_prev, axis=0, keepdims=True), (CHUNK, 256))
            dst = dec[c * CHUNK:c * CHUNK + 1] * dst_tot
            dubs[c] = dst_tot.astype(_MXU_DTYPE)
        scr[0][...] = dst
        dkd = jnp.concatenate(
            [_per_head(lambda h: lax.dot_general(vb[rs, h * 128:(h + 1) * 128], dubs[c][:, h * 64:(h + 1) * 64], _NN,
                                                 preferred_element_type=F32)) for c, rs in enumerate(chunks)], axis=0)
        dv = jnp.concatenate(
            [_per_head(lambda h: lax.dot_general(kdb[rs, h * 64:(h + 1) * 64], dubs[c][:, h * 64:(h + 1) * 64], _NT,
                                                 preferred_element_type=F32)) for c, rs in enumerate(chunks)], axis=0)
        e = dkd * kd
        dtot = _mm_exact_rhs(_chunk_matrix(tm, 'total'), e) + jnp.concatenate(ddecs, axis=0) * dec
        last = (lax.broadcasted_iota(jnp.int32, e.shape, 0) & (CHUNK - 1)) == CHUNK - 1
        dla = _mm_exact_rhs(_chunk_matrix(tm, 'revsum'), jnp.where(last, dtot - e, -e))
        dz = dla * (1.0 / GLA_TAU) * _sigmoid(-z)
        dzb = dz.astype(_MXU_DTYPE)
        dq = jnp.concatenate(dqs, axis=0) * (GLA_DK ** -0.5)
        outs[0][...] = jnp.concatenate([dq, dkd * w, dv, dr], axis=1).astype(_MXU_DTYPE)
        outs[1][...] = lax.dot_general(dzb, wa2v.astype(_MXU_DTYPE), _NT, preferred_element_type=F32).astype(_MXU_DTYPE)
        accs[0][...] += lax.dot_general(pa.astype(_MXU_DTYPE), dzb, _TN, preferred_element_type=F32)
        accs[1][...] += jnp.sum(dz, axis=0, keepdims=True)

    return _row_call("gla_bwd" + tag, body, t_len // tm,
                     rows=[(proj_b, tm), (proj_a, tm), (dy, tm), (states, cpt)],
                     halos=[(states, 1, cpt, 'prev')], res=[wa2, ba, ng],
                     outs=[((t_len, 1536), _MXU_DTYPE, tm), ((t_len, GLA_RANK), _MXU_DTYPE, tm)],
                     accs=[(GLA_RANK, 256), (1, 256), (1, BW)], scratch=[((GLA_DV, 256), F32)], reverse=True)


ATT_TM = 256
ATT_KEYS = ATT_TM + (ATT_BAND - 1) * CHUNK


def _rel_index():
    l_idx = np.arange(CHUNK)[:, None]
    m_idx = np.arange(BAND)[None, :]
    rel = l_idx + (ATT_BAND - 1) * CHUNK - m_idx
    return jnp.asarray((np.clip(rel, -(CHUNK - 1), MAX_REL) + (CHUNK - 1)).reshape(1, CHUNK * BAND), jnp.int32)


BIAS_COLS = 4096


def _bias_expand(tag, rel_bias):
    n = CHUNK * BAND

    def kern(rel_ref, idx_ref, o_ref):
        onehot = (lax.broadcasted_iota(jnp.int32, (REL_TABLE, BIAS_COLS), 0) == idx_ref[...]).astype(jnp.bfloat16)
        o_ref[...] = _mm_exact_lhs(rel_ref[...], onehot)

    return pl.pallas_call(
        kern, grid=(n // BIAS_COLS,),
        in_specs=[pl.BlockSpec((ATT_HEADS, REL_TABLE), lambda i: (0, 0)), pl.BlockSpec((1, BIAS_COLS), lambda i: (0, i))],
        out_specs=pl.BlockSpec((ATT_HEADS, BIAS_COLS), lambda i: (0, i)),
        out_shape=jax.ShapeDtypeStruct((ATT_HEADS, n), F32), name="bias_expand" + tag)(rel_bias, _rel_index())


def _bias_tile(tag, bias):
    per = ATT_TM // CHUNK

    def kern(b_ref, o_ref):
        bv = b_ref[...]
        for j in range(per):
            parts = [jnp.full((CHUNK, j * CHUNK), NEG_INF, F32)] if j else []
            parts.append(bv)
            if j < per - 1:
                parts.append(jnp.full((CHUNK, (per - 1 - j) * CHUNK), NEG_INF, F32))
            o_ref[j * CHUNK:(j + 1) * CHUNK, :] = jnp.concatenate(parts, axis=1)

    return pl.pallas_call(
        kern, grid=(ATT_HEADS,), in_specs=[pl.BlockSpec((None, CHUNK, BAND), lambda h: (h, 0, 0))],
        out_specs=pl.BlockSpec((None, ATT_TM, ATT_KEYS), lambda h: (h, 0, 0)),
        out_shape=jax.ShapeDtypeStruct((ATT_HEADS, ATT_TM, ATT_KEYS), F32), name="bias_tile" + tag)(bias)


def _bias_untile(tag, dbias):
    per = ATT_TM // CHUNK

    def kern(d_ref, o_ref):
        acc = d_ref[0:CHUNK, 0:BAND]
        for j in range(1, per):
            acc = acc + d_ref[j * CHUNK:(j + 1) * CHUNK, j * CHUNK:j * CHUNK + BAND]
        o_ref[...] = acc

    return pl.pallas_call(
        kern, grid=(ATT_HEADS,), in_specs=[pl.BlockSpec((None, ATT_TM, ATT_KEYS), lambda h: (h, 0, 0))],
        out_specs=pl.BlockSpec((None, CHUNK, BAND), lambda h: (h, 0, 0)),
        out_shape=jax.ShapeDtypeStruct((ATT_HEADS, CHUNK, BAND), F32), name="bias_untile" + tag)(dbias)


def _bias_reduce(tag, dbias):
    n = CHUNK * BAND

    def kern(db_ref, idx_ref, o_ref):
        @pl.when(pl.program_id(0) == 0)
        def _():
            o_ref[...] = jnp.zeros(o_ref.shape, o_ref.dtype)

        onehot = (lax.broadcasted_iota(jnp.int32, (REL_TABLE, BIAS_COLS), 0) == idx_ref[...]).astype(jnp.bfloat16)
        o_ref[...] += _mm_exact_lhs(db_ref[...], onehot, _NT)

    return pl.pallas_call(
        kern, grid=(n // BIAS_COLS,),
        in_specs=[pl.BlockSpec((ATT_HEADS, BIAS_COLS), lambda i: (0, i)), pl.BlockSpec((1, BIAS_COLS), lambda i: (0, i))],
        out_specs=pl.BlockSpec((ATT_HEADS, REL_TABLE), lambda i: (0, 0)),
        out_shape=jax.ShapeDtypeStruct((ATT_HEADS, REL_TABLE), F32),
        compiler_params=pltpu.CompilerParams(dimension_semantics=("arbitrary",)),
        name="bias_reduce" + tag)(dbias, _rel_index())


def _attn_stage(t, pc_ref, p1_ref, p2_ref, kv):
    tm = ATT_TM
    kv[0:tm, :] = jnp.where(t > 1, p2_ref[:, 512:1536], 0.0).astype(kv.dtype)
    kv[tm:2 * tm, :] = jnp.where(t > 0, p1_ref[:, 512:1536], 0.0).astype(kv.dtype)
    kv[2 * tm:, :] = pc_ref[:, 512:1536].astype(kv.dtype)
    q = (pc_ref[:, 0:512] * (ATT_HD ** -0.5)).astype(_MXU_DTYPE)
    ok = lax.broadcasted_iota(jnp.int32, (tm, ATT_KEYS), 1) >= (2 - t) * tm
    return q, ok


def _attn_probs(q, kv, bias_h, ok, h):
    hs = slice(h * ATT_HD, (h + 1) * ATT_HD)
    s = lax.dot_general(q[:, hs], kv[:, hs], _NT, preferred_element_type=F32) + bias_h
    s = jnp.where(ok, s, NEG_INF)
    e = jnp.exp(s - jnp.max(s, axis=-1, keepdims=True))
    return e * (1.0 / jnp.sum(e, axis=-1, keepdims=True))


def _attn_halos(proj_c):
    return [(proj_c, ATT_TM, 1, 'prev'), (proj_c, ATT_TM, 1, 'prev2')]


def _attn_fwd(tag, proj_c, bias, exchange=()):
    t_len = proj_c.shape[0]
    tm = ATT_TM

    def body(t, rows, halos, res, outs, accs, scr):
        b_ref, kv = res[0], scr[0]
        q, ok = _attn_stage(t, rows[0], halos[0], halos[1], kv)
        o = [lax.dot_general(_attn_probs(q, kv, b_ref[h], ok, h).astype(_MXU_DTYPE),
                             kv[:, BW + h * ATT_HD:BW + (h + 1) * ATT_HD], _NN, preferred_element_type=F32)
             for h in range(ATT_HEADS)]
        outs[0][...] = jnp.concatenate(o, axis=1).astype(_MXU_DTYPE)

    return _row_call("attn_fwd" + tag, body, t_len // tm, rows=[(proj_c, tm)], halos=_attn_halos(proj_c),
                     res=[bias], outs=[((t_len, BW), _MXU_DTYPE, tm)], scratch=[((ATT_KEYS, 1024), _MXU_DTYPE)],
                     exchange=exchange)


def _attn_bwd(tag, proj_c, dy, bias, exchange=()):
    t_len = proj_c.shape[0]
    tm = ATT_TM
    scale = ATT_HD ** -0.5

    def body(t, rows, halos, res, outs, accs, scr):
        b_ref, kv = res[0], scr[0]
        q, ok = _attn_stage(t, rows[0], halos[0], halos[1], kv)
        do = rows[1][...].astype(_MXU_DTYPE)
        dqs, dks, dvs = [], [], []
        for h in range(ATT_HEADS):
            hs = slice(h * ATT_HD, (h + 1) * ATT_HD)
            vs = slice(BW + h * ATT_HD, BW + (h + 1) * ATT_HD)
            p = _attn_probs(q, kv, b_ref[h], ok, h)
            dp = lax.dot_general(do[:, hs], kv[:, vs], _NT, preferred_element_type=F32)
            ds = p * (dp - jnp.sum(dp * p, axis=-1, keepdims=True))
            accs[0][h] += ds
            dsb = ds.astype(_MXU_DTYPE)
            dqs.append(lax.dot_general(dsb, kv[:, hs], _NN, preferred_element_type=F32) * scale)
            dks.append(lax.dot_general(dsb, q[:, hs], _TN, preferred_element_type=F32))
            dvs.append(lax.dot_general(p.astype(_MXU_DTYPE), do[:, hs], _TN, preferred_element_type=F32))
        outs[0][...] = jnp.concatenate(dqs, axis=1).astype(_MXU_DTYPE)
        dkv = jnp.concatenate(dks + dvs, axis=1).astype(GRAD_DTYPE)
        outs[1][...] = dkv[2 * tm:, :]
        outs[2][...] = dkv[tm:2 * tm, :]
        outs[3][...] = dkv[0:tm, :]

    return _row_call("attn_bwd" + tag, body, t_len // tm, rows=[(proj_c, tm), (dy, tm)],
                     halos=_attn_halos(proj_c), res=[bias],
                     outs=[((t_len, BW), _MXU_DTYPE, tm)] + [((t_len, 1024), GRAD_DTYPE, tm)] * 3,
                     accs=[(ATT_HEADS, ATT_TM, ATT_KEYS)], scratch=[((ATT_KEYS, 1024), _MXU_DTYPE)],
                     exchange=exchange)


def _attn_combine(tag, dq, dkv_own, dkv_prev, dkv_prev2):
    t_len = dq.shape[0]
    tm = ATT_TM
    nt = t_len // tm

    def body(t, rows, halos, res, outs, accs, scr):
        dkv = ((rows[1][...].astype(F32) + jnp.where(t < nt - 1, halos[0][...].astype(F32), 0.0))
               + jnp.where(t < nt - 2, halos[1][...].astype(F32), 0.0))
        outs[0][...] = jnp.concatenate([rows[0][...], dkv.astype(_MXU_DTYPE)], axis=1)

    return _row_call("attn_combine" + tag, body, nt, rows=[(dq, tm), (dkv_own, tm)],
                     halos=[(dkv_prev, tm, 1, 'next'), (dkv_prev2, tm, 1, 'next2')],
                     outs=[((t_len, 1536), _MXU_DTYPE, tm)])[0]


def _conv_glu(pd):
    a, g = pd[:, :BW], pd[:, BW:]
    sig = _sigmoid(g)
    return a, sig, a * sig


def _conv_stage(t, pd_ref, ph_ref, win):
    pd = pd_ref[...]
    a, sig, y0 = _conv_glu(pd)
    win[0:CONV_HALO, :] = jnp.where(t > 0, _conv_glu(ph_ref[...])[2], 0.0)
    win[CONV_HALO:CONV_HALO + pd.shape[0], :] = y0
    return a, sig


SUBLANES = 8


def _conv_shifted(win, sh):
    for b in range(SUBLANES):
        sh[b] = win[pl.ds(b, sh.shape[1]), :]


def _conv_taps_by_copy(offsets):
    groups = {}
    for j, o in enumerate(offsets):
        groups.setdefault(o % SUBLANES, []).append((j, o - o % SUBLANES))
    return [(rem, min(a for _, a in taps), max(a for _, a in taps) - min(a for _, a in taps), taps)
            for rem, taps in sorted(groups.items())]


def _conv_span(sh, rem, r0, lo, rows):
    return sh[rem, pl.ds(pl.multiple_of(r0 + lo, SUBLANES), rows), :]


def _conv_tap_sum(sh, w_ref, offsets, out_ref, init=None, rb=32):
    plan = _conv_taps_by_copy(offsets)

    def block(i, carry):
        r0 = pl.multiple_of(i * rb, rb)
        acc = jnp.zeros((rb, BW), F32) if init is None else jnp.broadcast_to(init, (rb, BW))
        for rem, lo, extra, taps in plan:
            span = _conv_span(sh, rem, r0, lo, extra + rb)
            for j, a in taps:
                acc = acc + w_ref[j:j + 1, :] * span[a - lo:a - lo + rb]
        out_ref[pl.ds(r0, rb), :] = acc
        return carry

    lax.fori_loop(0, out_ref.shape[0] // rb, block, 0)


def _conv_tap_corr(sh, d_ref, offsets, acc_ref, rb=16):
    for rem, lo, extra, taps in _conv_taps_by_copy(offsets):
        def block(i, sums, rem=rem, lo=lo, extra=extra, taps=taps):
            r0 = pl.multiple_of(i * rb, rb)
            d = d_ref[pl.ds(r0, rb), :]
            span = _conv_span(sh, rem, r0, lo, extra + rb)
            out = []
            for s, (j, a) in zip(sums, taps):
                prod = d * span[a - lo:a - lo + rb]
                for k in range(0, rb, SUBLANES):
                    s = s + prod[k:k + SUBLANES]
                out.append(s)
            return tuple(out)

        sums = lax.fori_loop(0, d_ref.shape[0] // rb, block,
                             tuple(jnp.zeros((SUBLANES, BW), F32) for _ in taps), unroll=2)
        for (j, _), s in zip(taps, sums):
            acc_ref[j:j + 1, :] += jnp.sum(s, axis=0, keepdims=True)


def _conv_scratch(tm):
    return [((tm + CONV_HALO + SUBLANES, BW), F32), ((SUBLANES, tm + CONV_HALO, BW), F32)]


def _conv_fwd(tag, proj_d, dw_w, dw_b, ln_g, ln_b, tm=512):
    t_len = proj_d.shape[0]
    lead = CONV_HALO - (CONV_K - 1)

    def body(t, rows, halos, res, outs, accs, scr):
        win, sh = scr
        _conv_stage(t, rows[0], halos[0], win)
        _conv_shifted(win, sh)
        _conv_tap_sum(sh, res[0], [lead + j for j in range(CONV_K)], outs[1], init=res[1][...])
        yl, _, _ = _ln_fwd(outs[1][...], res[2][...], res[3][...])
        outs[0][...] = (yl * _sigmoid(yl)).astype(_MXU_DTYPE)

    return _row_call("conv_fwd" + tag, body, t_len // tm, rows=[(proj_d, tm)],
                     halos=[(proj_d, CONV_HALO, tm // CONV_HALO, 'prev')], res=[dw_w, dw_b, ln_g, ln_b],
                     outs=[((t_len, BW), _MXU_DTYPE, tm), ((t_len, BW), F32, tm)], scratch=_conv_scratch(tm))


def _conv_bwd_norm(tag, yc, dy, ln_g, ln_b, tm=512):
    t_len = yc.shape[0]

    def body(t, rows, halos, res, outs, accs, scr):
        lgv = res[0][...]
        yl, xh, rs = _ln_fwd(rows[0][...], lgv, res[1][...])
        sig = _sigmoid(yl)
        dyl = rows[1][...] * (sig * (1.0 + yl * (1.0 - sig)))
        dyc, dlg, dlb = _ln_bwd(dyl, xh, rs, lgv)
        outs[0][...] = dyc
        accs[0][...] += dlg
        accs[1][...] += dlb
        accs[2][...] += jnp.sum(dyc, axis=0, keepdims=True)

    return _row_call("conv_bwd_norm" + tag, body, t_len // tm, rows=[(yc, tm), (dy, tm)], res=[ln_g, ln_b],
                     outs=[((t_len, BW), F32, tm)], accs=[(1, BW), (1, BW), (1, BW)])


def _conv_bwd_taps(tag, proj_d, dyc, dw_w, tm=512):
    t_len = proj_d.shape[0]
    nt = t_len // tm
    lead = CONV_HALO - (CONV_K - 1)

    def body(t, rows, halos, res, outs, accs, scr):
        win, sh, wd, shd, dy0_ref = scr
        a, sig = _conv_stage(t, rows[0], halos[0], win)
        _conv_shifted(win, sh)
        wd[0:tm, :] = rows[1][...]
        wd[tm:tm + CONV_HALO, :] = jnp.where(t < nt - 1, halos[1][...], 0.0)
        _conv_shifted(wd, shd)
        _conv_tap_corr(sh, rows[1], [lead + j for j in range(CONV_K)], accs[0])
        _conv_tap_sum(shd, res[0], [CONV_K - 1 - j for j in range(CONV_K)], dy0_ref)
        dy0 = dy0_ref[...]
        outs[0][...] = jnp.concatenate([dy0 * sig, dy0 * a * sig * (1.0 - sig)], axis=1).astype(_MXU_DTYPE)

    return _row_call("conv_bwd_taps" + tag, body, nt, rows=[(proj_d, tm), (dyc, tm)],
                     halos=[(proj_d, CONV_HALO, tm // CONV_HALO, 'prev'), (dyc, CONV_HALO, tm // CONV_HALO, 'next')],
                     res=[dw_w], outs=[((t_len, 2 * BW), _MXU_DTYPE, tm)], accs=[(CONV_K, BW)],
                     scratch=_conv_scratch(tm) + _conv_scratch(tm) + [((tm, BW), F32)])


def _merge_fwd(tag, h, xn, ys, w_gate, b_gate, w_branch, w_out, tm=256):
    t_len = h.shape[0]

    def body(t, rows, halos, res, outs, accs, scr):
        xnv = rows[1][...]
        wg_ref, bg_ref, wb_ref, wo_ref = res
        merged = jnp.zeros((tm, D_MODEL), F32)
        for n in range(4):
            cs = slice(n * D_MODEL, (n + 1) * D_MODEL)
            z = lax.dot_general(xnv, wg_ref[n], _NN, preferred_element_type=F32) + bg_ref[n:n + 1, :]
            bo = lax.dot_general(rows[2 + n][...], wb_ref[n], _NN, preferred_element_type=F32)
            outs[0][:, cs] = z.astype(_MXU_DTYPE)
            outs[1][:, cs] = bo.astype(_MXU_DTYPE)
            merged = merged + _sigmoid(z) * bo
        mb = merged.astype(_MXU_DTYPE)
        outs[2][...] = mb
        outs[3][...] = rows[0][...] + lax.dot_general(mb, wo_ref[...], _NN, preferred_element_type=F32)

    return _row_call("merge_fwd" + tag, body, t_len // tm, rows=[(h, tm), (xn, tm)] + [(y, tm) for y in ys],
                     res=[w_gate, b_gate, w_branch, w_out],
                     outs=[((t_len, 4 * D_MODEL), _MXU_DTYPE, tm), ((t_len, 4 * D_MODEL), _MXU_DTYPE, tm),
                           ((t_len, D_MODEL), _MXU_DTYPE, tm), ((t_len, D_MODEL), F32, tm)])


def _merge_bwd(tag, dh, gate_pre, bo, w_gate, w_branch, w_out, tm=256):
    t_len = dh.shape[0]

    def body(t, rows, halos, res, outs, accs, scr):
        wg_ref, wb_ref, wo_ref = res
        dhb = rows[0][...].astype(_MXU_DTYPE)
        outs[0][...] = dhb
        dmerged = lax.dot_general(dhb, wo_ref[...], _NT, preferred_element_type=F32)
        dxn = jnp.zeros((tm, D_MODEL), F32)
        dbg = []
        for n in range(4):
            cs = slice(n * D_MODEL, (n + 1) * D_MODEL)
            g = _sigmoid(rows[1][:, cs].astype(F32))
            dbo = (dmerged * g).astype(_MXU_DTYPE)
            dgp = dmerged * rows[2][:, cs].astype(F32) * (g * (1.0 - g))
            dgb = dgp.astype(_MXU_DTYPE)
            outs[1][:, cs] = dbo
            outs[2][:, cs] = dgb
            outs[4 + n][...] = lax.dot_general(dbo, wb_ref[n], _NT, preferred_element_type=F32)
            dxn = dxn + lax.dot_general(dgb, wg_ref[n], _NT, preferred_element_type=F32)
            dbg.append(jnp.sum(dgp, axis=0, keepdims=True))
        outs[3][...] = dxn
        accs[0][...] += jnp.concatenate(dbg, axis=1)

    return _row_call("merge_bwd" + tag, body, t_len // tm, rows=[(dh, tm), (gate_pre, tm), (bo, tm)],
                     res=[w_gate, w_branch, w_out],
                     outs=[((t_len, D_MODEL), _MXU_DTYPE, tm), ((t_len, 4 * D_MODEL), _MXU_DTYPE, tm),
                           ((t_len, 4 * D_MODEL), _MXU_DTYPE, tm), ((t_len, D_MODEL), F32, tm)]
                     + [((t_len, BW), F32, tm)] * 4,
                     accs=[(1, 4 * D_MODEL)])


FF_COLS = 1024


def _ffn_fwd(tag, h, g2, w1, w2, tm=512):
    t_len = h.shape[0]

    def body(t, rows, halos, res, outs, accs, scr):
        hv = rows[0][...]
        hn = (hv * _rms_stat(hv) * res[0][...]).astype(_MXU_DTYPE)
        outs[0][...] = hn
        acc = hv
        for c in range(D_FF // FF_COLS):
            cs = slice(c * FF_COLS, (c + 1) * FF_COLS)
            pre = lax.dot_general(hn, res[1][:, cs], _NN, preferred_element_type=F32)
            outs[1][:, cs] = pre
            ff = jnp.square(jnp.maximum(pre, 0.0)).astype(_MXU_DTYPE)
            acc = acc + lax.dot_general(ff, res[2][cs, :], _NN, preferred_element_type=F32)
        outs[2][...] = acc

    return _row_call("ffn_fwd" + tag, body, t_len // tm, rows=[(h, tm)], res=[g2, w1, w2],
                     outs=[((t_len, D_MODEL), _MXU_DTYPE, tm), ((t_len, D_FF), F32, tm), ((t_len, D_MODEL), F32, tm)])


def _ffn_bwd(tag, dh, h, pre, g2, w1, w2, tm=256, exchange=()):
    t_len = dh.shape[0]

    def body(t, rows, halos, res, outs, accs, scr):
        dhv = rows[0][...]
        hv = rows[1][...]
        dhb = dhv.astype(_MXU_DTYPE)
        outs[0][...] = dhb
        dhn = jnp.zeros((tm, D_MODEL), F32)
        for c in range(D_FF // FF_COLS):
            cs = slice(c * FF_COLS, (c + 1) * FF_COLS)
            r = jnp.maximum(rows[2][:, cs], 0.0)
            outs[1][:, cs] = (r * r).astype(_MXU_DTYPE)
            dpre = (lax.dot_general(dhb, res[2][cs, :], _NT, preferred_element_type=F32) * (2.0 * r)).astype(_MXU_DTYPE)
            outs[2][:, cs] = dpre
            dhn = dhn + lax.dot_general(dpre, res[1][:, cs], _NT, preferred_element_type=F32)
        dres, dg = _rms_bwd(dhn, hv, res[0][...], _rms_stat(hv))
        outs[3][...] = dhv + dres
        accs[0][...] += dg

    return _row_call("ffn_bwd" + tag, body, t_len // tm, rows=[(dh, tm), (h, tm), (pre, tm)], res=[g2, w1, w2],
                     outs=[((t_len, D_MODEL), _MXU_DTYPE, tm), ((t_len, D_FF), _MXU_DTYPE, tm),
                           ((t_len, D_FF), _MXU_DTYPE, tm), ((t_len, D_MODEL), F32, tm)],
                     accs=[(1, D_MODEL)], exchange=exchange)


def _ple_fwd(tag, h, p, g3, w_pg, b_pg, w_ple, tm=512):
    t_len = h.shape[0]

    def body(t, rows, halos, res, outs, accs, scr):
        hv = rows[0][...]
        hg = (hv * _rms_stat(hv) * res[0][...]).astype(_MXU_DTYPE)
        pb = rows[1][...].astype(_MXU_DTYPE)
        pg = _sigmoid(lax.dot_general(hg, res[1][...], _NN, preferred_element_type=F32) + res[2][...])
        pe = lax.dot_general(pb, res[3][...], _NN, preferred_element_type=F32)
        outs[0][...] = hg
        outs[1][...] = pb
        outs[2][...] = pg
        outs[3][...] = hv + pg * pe

    return _row_call("ple_fwd" + tag, body, t_len // tm, rows=[(h, tm), (p, tm)], res=[g3, w_pg, b_pg, w_ple],
                     outs=[((t_len, D_MODEL), _MXU_DTYPE, tm), ((t_len, PLE_DIM), _MXU_DTYPE, tm),
                           ((t_len, D_MODEL), F32, tm), ((t_len, D_MODEL), F32, tm)])


def _ple_bwd(tag, dh, h, pg, p_b, g3, w_pg, w_ple, tm=512):
    t_len = dh.shape[0]

    def body(t, rows, halos, res, outs, accs, scr):
        dhv = rows[0][...]
        hv = rows[1][...]
        pgv = rows[2][...]
        pe = lax.dot_general(rows[3][...], res[2][...], _NN, preferred_element_type=F32)
        dgp = dhv * pe * (pgv * (1.0 - pgv))
        dgb = dgp.astype(_MXU_DTYPE)
        outs[0][...] = dgb
        outs[1][...] = (dhv * pgv).astype(_MXU_DTYPE)
        dhg = lax.dot_general(dgb, res[1][...], _NT, preferred_element_type=F32)
        dres, dg = _rms_bwd(dhg, hv, res[0][...], _rms_stat(hv))
        outs[2][...] = dhv + dres
        accs[0][...] += jnp.sum(dgp, axis=0, keepdims=True)
        accs[1][...] += dg

    return _row_call("ple_bwd" + tag, body, t_len // tm, rows=[(dh, tm), (h, tm), (pg, tm), (p_b, tm)],
                     res=[g3, w_pg, w_ple],
                     outs=[((t_len, D_MODEL), _MXU_DTYPE, tm), ((t_len, D_MODEL), _MXU_DTYPE, tm),
                           ((t_len, D_MODEL), F32, tm)],
                     accs=[(1, D_MODEL), (1, D_MODEL)])


def _inproj_bwd(tag, dh, h, dxn_gate, dprojs, g1, w_groups, tm=512, exchange=()):
    t_len = dh.shape[0]

    def body(t, rows, halos, res, outs, accs, scr):
        hv = rows[1][...]
        dxn = rows[2][...]
        for dp, w in zip(rows[3:], res[1:]):
            dxn = dxn + lax.dot_general(dp[...], w[...], _NN, preferred_element_type=F32)
        dres, dg = _rms_bwd(dxn, hv, res[0][...], _rms_stat(hv))
        outs[0][...] = rows[0][...] + dres
        accs[0][...] += dg

    return _row_call("inproj_bwd" + tag, body, t_len // tm,
                     rows=[(dh, tm), (h, tm), (dxn_gate, tm)] + [(d, tm) for d in dprojs],
                     res=[g1] + list(w_groups), outs=[((t_len, D_MODEL), F32, tm)], accs=[(1, D_MODEL)],
                     exchange=exchange)


def _loss_head(h, target, gf, tm=512):
    t_len = h.shape[0]

    def body(t, rows, halos, res, outs, accs, scr):
        hv = rows[0][...]
        g = res[0][...]
        r = _rms_stat(hv)
        diff = hv * r * g - rows[1][...]
        accs[0][...] += 0.5 * jnp.sum(jnp.mean(diff * diff, axis=-1, keepdims=True), axis=0, keepdims=True)
        dh, dg = _rms_bwd(diff * (1.0 / D_MODEL), hv, g, r)
        outs[0][...] = dh
        accs[1][...] += dg

    return _row_call("loss_head", body, t_len // tm, rows=[(h, tm), (target, tm)], res=[gf],
                     outs=[((t_len, D_MODEL), F32, tm)], accs=[(1, 128), (1, D_MODEL)])


def _row(v):
    return v.reshape(1, -1)


GATHER_DURING = (('inproj', ('w_gate', 'w_branch', 'w_out')), ('gla', ('w_ff1',)), ('attn', ('w_ff2', 'w_ple_gate', 'w_ple')))
SCATTER_DURING_ATTN = ('w_ple_gate', 'w_ple', 'w_ff1', 'w_ff2', 'w_out', 'w_gate', 'w_branch')


def _gather_items(shards, names):
    items = []
    for n in names:
        s = shards[n]
        ax = SHARD_AXIS[n] - 1
        if n == 'w_in':
            items.append(_gather_item(s))
        else:
            items.append(_gather_item(s, s.shape[:ax] + (N_DEV * s.shape[ax],) + s.shape[ax + 1:], ax))
    return items


def _land(w, names, arrays):
    for n, a in zip(names, arrays):
        w[n] = a.reshape(IN_COLS, D_MODEL) if n == 'w_in' else a


def _layer_fwd(i, h, p_i, w, shards, next_shards):
    tag = "_l%d" % i
    during = dict(GATHER_DURING)
    win = [w['w_in'][s:s + n] for _, s, n in IN_GROUPS]
    res = _inproj_fwd(tag, h, _row(w['norm1_g']), win, exchange=_gather_items(shards, during['inproj']))
    xn, pa, pb, pr, pc, pd = res[:6]
    _land(w, during['inproj'], res[6:])
    sg_bt = w['sg_b'].T
    y_a = _sg_fwd(tag, pa, _row(w['sg_ln_g']), _row(w['sg_ln_b']), w['sg_w'], sg_bt)
    res = _gla_fwd(tag, pb, pr, w['gla_w_a2'], _row(w['gla_b_a']), _row(w['gla_norm_g']),
                   exchange=_gather_items(shards, during['gla']))
    y_b, states = res[:2]
    _land(w, during['gla'], res[2:])
    bias = _bias_tile(tag, _bias_expand(tag, w['att_rel_bias']).reshape(ATT_HEADS, CHUNK, BAND))
    items = _gather_items(shards, during['attn']) + (_gather_items(next_shards, ['w_in']) if next_shards else [])
    res = _attn_fwd(tag, pc, bias, exchange=items)
    y_c = res[0]
    _land(w, during['attn'], res[1:1 + len(during['attn'])])
    next_w_in = res[-1].reshape(IN_COLS, D_MODEL) if next_shards else None
    y_d, yc = _conv_fwd(tag, pd, w['conv_dw_w'], _row(w['conv_dw_b']), _row(w['conv_ln_g']), _row(w['conv_ln_b']))
    ys = (y_a, y_b, y_c, y_d)
    gate, bo, merged, h1 = _merge_fwd(tag, h, xn, ys, w['w_gate'], w['b_gate'], w['w_branch'], w['w_out'])
    hn, pre, h2 = _ffn_fwd(tag, h1, _row(w['norm2_g']), w['w_ff1'], w['w_ff2'])
    hg, p_b, pg, h3 = _ple_fwd(tag, h2, p_i, _row(w['norm3_g']), w['w_ple_gate'], _row(w['b_ple_gate']), w['w_ple'])
    saved = dict(h=h, xn=xn, pa=pa, pb=pb, pr=pr, pc=pc, pd=pd, states=states, bias=bias, yc=yc, ys=ys, gate=gate,
                 bo=bo, merged=merged, h1=h1, hn=hn, pre=pre, h2=h2, hg=hg, p_b=p_b, pg=pg, win=win, sg_bt=sg_bt)
    return h3, saved, next_w_in


def _layer_bwd(i, dh3, s, w, tail=None):
    tag = "_l%d" % i
    g = {}
    dgp, dpe, dh2, db_pg, dg3 = _ple_bwd(tag, dh3, s['h2'], s['pg'], s['p_b'], _row(w['norm3_g']), w['w_ple_gate'],
                                         w['w_ple'])
    g['b_ple_gate'], g['norm3_g'] = db_pg[0], dg3[0]
    g['w_ple_gate'] = _tn_call("dw_ple_gate" + tag, s['hg'], dgp, D_MODEL, D_MODEL)
    g['w_ple'] = _tn_call("dw_ple" + tag, s['p_b'], dpe, PLE_DIM, D_MODEL)

    dh2b, ffb, dpre, dh1, dg2 = _ffn_bwd(tag, dh2, s['h1'], s['pre'], _row(w['norm2_g']), w['w_ff1'], w['w_ff2'])
    g['norm2_g'] = dg2[0]
    g['w_ff1'] = _tn_call("dw_ff1" + tag, s['hn'], dpre, D_MODEL, FF_COLS, nblk=D_FF // FF_COLS)
    g['w_ff2'] = _tn_call("dw_ff2" + tag, ffb, dh2b, FF_COLS, D_MODEL, nblk=D_FF // FF_COLS, a_col=True, b_col=False,
                          out='rows')

    dh1b, dbo, dgpre, dxn_gate, dy_a, dy_b, dy_c, dy_d, db_gate = _merge_bwd(
        tag, dh1, s['gate'], s['bo'], w['w_gate'], w['w_branch'], w['w_out'])
    g['b_gate'] = db_gate.reshape(4, D_MODEL)
    g['w_out'] = _tn_call("dw_out" + tag, s['merged'], dh1b, D_MODEL, D_MODEL)
    g['w_gate'] = _tn_call("dw_gate" + tag, s['xn'], dgpre, D_MODEL, D_MODEL, nblk=4, out='stack')
    g['w_branch'] = jnp.stack([_tn_call("dw_branch%d%s" % (n, tag), s['ys'][n], dbo, BW, D_MODEL, b_off=n)
                             for n in range(4)])

    lg, lb = _row(w['sg_ln_g']), _row(w['sg_ln_b'])
    dpa, dsg_w, dsg_bt, dlg, dlb = _sg_bwd(tag, s['pa'], dy_a, lg, lb, w['sg_w'], s['sg_bt'])
    g['sg_w'], g['sg_b'], g['sg_ln_g'], g['sg_ln_b'] = dsg_w, dsg_bt.T, dlg[0], dlb[0]

    dpb, dpr, dwa2, dba, dng = _gla_bwd(tag, s['pb'], s['pr'], dy_b, s['states'], w['gla_w_a2'],
                                        _row(w['gla_b_a']), _row(w['gla_norm_g']))
    g['gla_w_a2'], g['gla_b_a'], g['gla_norm_g'] = dwa2, dba[0], dng[0]

    items = [_scatter_item(g.pop(n), axis=SHARD_AXIS[n] - 1) for n in SCATTER_DURING_ATTN]
    res = _attn_bwd(tag, s['pc'], dy_c, s['bias'], exchange=items)
    dq, dkv_own, dkv_prev, dkv_prev2, dbias = res[:5]
    parts = dict(zip(SCATTER_DURING_ATTN, res[5:]))
    dpc = _attn_combine(tag, dq, dkv_own, dkv_prev, dkv_prev2)
    g['att_rel_bias'] = _bias_reduce(tag, _bias_untile(tag, dbias).reshape(ATT_HEADS, CHUNK * BAND))

    cg, cb = _row(w['conv_ln_g']), _row(w['conv_ln_b'])
    dyc, dcg, dcb, ddwb = _conv_bwd_norm(tag, s['yc'], dy_d, cg, cb)
    dpd, ddw = _conv_bwd_taps(tag, s['pd'], dyc, w['conv_dw_w'])
    g['conv_ln_g'], g['conv_ln_b'], g['conv_dw_b'], g['conv_dw_w'] = dcg[0], dcb[0], ddwb[0], ddw

    dprojs = (dpa, dpb, dpr, dpc, dpd)
    dw_in = jnp.concatenate([_tn_call("dw_in%s%s" % (name, tag), dp, s['xn'], n, D_MODEL)
                             for (name, _, n), dp in zip(IN_GROUPS, dprojs)], axis=0)
    items = [_scatter_item(dw_in.reshape(N_DEV, IN_COLS // N_DEV, D_MODEL))] + (tail(g) if tail else [])
    res = _inproj_bwd(tag, dh1, s['h'], dxn_gate, dprojs, _row(w['norm1_g']), s['win'], exchange=items)
    dh0, dg1 = res[:2]
    g['norm1_g'] = dg1[0]
    parts['w_in'] = res[2]
    return dh0, g, parts, res[3:]


def _local_step(x, p, target, final_g, layers, shards, tail):
    h = x
    saved = []
    for i in range(DEPTH):
        nxt = shards[i + 1] if i + 1 < DEPTH else None
        h, s, next_w_in = _layer_fwd(i, h, p[i], layers[i], shards[i], nxt)
        saved.append(s)
        if nxt:
            layers[i + 1]['w_in'] = next_w_in
    dh, loss, dgf = _loss_head(h, target, _row(final_g))
    small, parts, tail_out = [None] * DEPTH, [None] * DEPTH, None
    for i in reversed(range(DEPTH)):
        hook = (lambda g: tail([g] + small[1:])) if i == 0 else None
        dh, small[i], parts[i], out = _layer_bwd(i, dh, saved[i], layers[i], hook)
        if i == 0:
            tail_out = out
    return loss[0, 0], dh, dgf[0], small, parts, tail_out


def _peers():
    x, y, c = lax.axis_index("x"), lax.axis_index("y"), lax.axis_index("c")
    me = 4 * x + 2 * y + c
    out = []
    for k in range(1, N_DEV):
        px = (1 - x) if k & 4 else x
        py = (1 - y) if k & 2 else y
        pc = (1 - c) if k & 1 else c
        out.append((k - 1, (px, py, pc), 4 * px + 2 * py + pc))
    return me, out


def _block(ref, axis, idx, width):
    ix = [slice(None)] * len(ref.shape)
    ix[axis] = pl.ds(pl.multiple_of(idx * width, width), width)
    return ref.at[tuple(ix)]


def _slot(ref, idx):
    return ref.at[idx]


def _whole(ref, idx):
    return ref


def _gather_item(src, out_shape=None, axis=None):
    if axis is None:
        return dict(src=src, out=(N_DEV,) + src.shape, take=_whole, put=_slot)
    return dict(src=src, out=tuple(out_shape), take=_whole,
                put=lambda ref, s: _block(ref, axis, s, src.shape[axis]))


def _scatter_item(src, axis=None, lead=0):
    if axis is None:
        shape = src.shape[:lead] + src.shape[lead + 1:]
        take = lambda ref, s: ref.at[(slice(None),) * lead + (s,)]
    else:
        width = src.shape[axis] // N_DEV
        shape = src.shape[:axis] + (width,) + src.shape[axis + 1:]
        take = lambda ref, s: _block(ref, axis, s, width)
    return dict(src=src, out=(N_DEV,) + shape, take=take, put=_slot)


def _exchange_sems(n):
    return [pltpu.SemaphoreType.DMA((n * (N_DEV - 1),)), pltpu.SemaphoreType.DMA((n * (N_DEV - 1),)),
            pltpu.SemaphoreType.DMA((n,))]


def _exchange_copies(items, src_refs, out_refs, sems, start):
    send_sems, recv_sems, local_sems = sems
    me, peers = _peers()

    def remote(i, k, pos, receiver, sender):
        it = items[i]
        return pltpu.make_async_remote_copy(
            src_ref=it['take'](src_refs[i], receiver), dst_ref=it['put'](out_refs[i], sender),
            send_sem=send_sems.at[i * (N_DEV - 1) + k], recv_sem=recv_sems.at[i * (N_DEV - 1) + k],
            device_id=pos, device_id_type=pl.DeviceIdType.MESH)

    local = [pltpu.make_async_copy(it['take'](src_refs[i], me), it['put'](out_refs[i], me), local_sems.at[i])
             for i, it in enumerate(items)]
    if start:
        for cp in local:
            cp.start()
        for k, pos, flat in peers:
            for i in range(len(items)):
                remote(i, k, pos, flat, me).start()
    else:
        for k, pos, flat in peers:
            for i in range(len(items)):
                remote(i, k, pos, flat, flat).wait_recv()
        for k, pos, flat in peers:
            for i in range(len(items)):
                remote(i, k, pos, flat, me).wait_send()
        for cp in local:
            cp.wait()


def _exchange(name, items):
    n = len(items)

    def body(*refs):
        _exchange_copies(items, refs[:n], refs[n:2 * n], refs[2 * n:], start=True)
        _exchange_copies(items, refs[:n], refs[n:2 * n], refs[2 * n:], start=False)

    any_spec = pl.BlockSpec(memory_space=pl.ANY)
    return pl.pallas_call(
        body, out_shape=[jax.ShapeDtypeStruct(it['out'], it['src'].dtype) for it in items],
        in_specs=[any_spec] * n, out_specs=[any_spec] * n, scratch_shapes=_exchange_sems(n),
        name=name)(*[it['src'] for it in items])


def _pack(arrays, dtype, lead=None):
    flat = [a.astype(dtype).reshape((lead, -1) if lead else (-1,)) for a in arrays]
    cat = jnp.concatenate(flat, axis=-1)
    n = cat.shape[-1]
    rows = -(-n // (PACK_COLS * SUBLANES)) * SUBLANES
    pad = rows * PACK_COLS - n
    if pad:
        cat = jnp.pad(cat, ((0, 0), (0, pad)) if lead else ((0, pad),))
    return cat.reshape((lead, rows, PACK_COLS) if lead else (rows, PACK_COLS))


def _unpack(buf, shapes, lead=None):
    flat = buf.reshape((lead, -1) if lead else (-1,))
    out, off = [], 0
    for shp in shapes:
        n = int(np.prod(shp))
        piece = flat[..., off:off + n]
        out.append(piece.reshape(((lead,) if lead else ()) + tuple(shp)))
        off += n
    return out


def _to_slabs(full, axis):
    shp = full.shape
    split = full.reshape(shp[:axis] + (N_DEV, shp[axis] // N_DEV) + shp[axis + 1:])
    return jnp.moveaxis(split, axis, 0)


def _from_slabs(slabs, axis):
    moved = jnp.moveaxis(slabs, 0, axis)
    shp = moved.shape
    return moved.reshape(shp[:axis] + (shp[axis] * shp[axis + 1],) + shp[axis + 2:])


def _adamw_block(r, c):
    if r % 8:
        return r, 256
    br = min(r, max(8, ADAMW_TILE * PACK_COLS // c))
    while r % br:
        br //= 2
    return br, c


def _adamw(name, partials, w, m, v):
    n_lead, r, c = w.shape
    br, bc = _adamw_block(r, c)
    ni, nj = r // br, c // bc
    c1 = 1.0 - ADAM_B1 ** ADAM_STEP
    c2 = 1.0 - ADAM_B2 ** ADAM_STEP

    def kern(*refs):
        p_refs = refs[:n_lead]
        w_ref, m_ref, v_ref, g_ref, d_ref, nm_ref, nv_ref = refs[n_lead:]
        layer = pl.program_id(0)
        g = None
        for l, p_ref in enumerate(p_refs):
            gl = p_ref[0].astype(F32)
            for s in range(1, N_DEV):
                gl = gl + p_ref[s].astype(F32)
            g = gl if g is None else jnp.where(layer == l, gl, g)
        nm = ADAM_B1 * m_ref[...] + (1.0 - ADAM_B1) * g
        nv = ADAM_B2 * v_ref[...] + (1.0 - ADAM_B2) * jnp.square(g)
        g_ref[...] = g
        nm_ref[...] = nm
        nv_ref[...] = nv
        d_ref[...] = -ADAM_LR * ((nm / c1) / (jnp.sqrt(nv / c2) + ADAM_EPS) + ADAM_WD * w_ref[...])

    def part_spec(mine):
        def index(l, i, j):
            before, after = l < mine, l > mine
            return (0, jnp.where(before, 0, jnp.where(after, ni - 1, i)), jnp.where(before, 0, jnp.where(after, nj - 1, j)))
        return pl.BlockSpec((N_DEV, br, bc), index)

    blk = pl.BlockSpec((None, br, bc), lambda l, i, j: (l, i, j))
    return pl.pallas_call(
        kern, grid=(n_lead, ni, nj),
        in_specs=[part_spec(l) for l in range(n_lead)] + [blk, blk, blk],
        out_specs=[blk] * 4, out_shape=[jax.ShapeDtypeStruct(w.shape, F32)] * 4,
        compiler_params=pltpu.CompilerParams(dimension_semantics=("arbitrary",) * 3),
        name=name)(*partials, w, m, v)


def _as_rows(a, lead):
    return a.reshape(a.shape[:lead] + (-1, a.shape[-1]))


def kernel(x, p, norm1_g, w_in, sg_ln_g, sg_ln_b, sg_w, sg_b, gla_w_a2, gla_b_a, gla_norm_g, att_rel_bias, conv_dw_w, conv_dw_b, conv_ln_g, conv_ln_b, w_branch, w_gate, b_gate, w_out, norm2_g, w_ff1, w_ff2, norm3_g, w_ple_gate, b_ple_gate, w_ple, final_g, loss_target, m_norm1_g, m_w_in, m_sg_ln_g, m_sg_ln_b, m_sg_w, m_sg_b, m_gla_w_a2, m_gla_b_a, m_gla_norm_g, m_att_rel_bias, m_conv_dw_w, m_conv_dw_b, m_conv_ln_g, m_conv_ln_b, m_w_branch, m_w_gate, m_b_gate, m_w_out, m_norm2_g, m_w_ff1, m_w_ff2, m_norm3_g, m_w_ple_gate, m_b_ple_gate, m_w_ple, m_final_g, v_norm1_g, v_w_in, v_sg_ln_g, v_sg_ln_b, v_sg_w, v_sg_b, v_gla_w_a2, v_gla_b_a, v_gla_norm_g, v_att_rel_bias, v_conv_dw_w, v_conv_dw_b, v_conv_ln_g, v_conv_ln_b, v_w_branch, v_w_gate, v_b_gate, v_w_out, v_norm2_g, v_w_ff1, v_w_ff2, v_norm3_g, v_w_ple_gate, v_b_ple_gate, v_w_ple, v_final_g):
    args = locals()
    wts = {n: args[n] for n in WEIGHTS}
    mom = {n: args['m_' + n] for n in WEIGHTS}
    var = {n: args['v_' + n] for n in WEIGHTS}

    local = {d_name: dict(d, w_in=jnp.swapaxes(d['w_in'], 1, 2))
             for d_name, d in (("w", wts), ("m", mom), ("v", var))}
    shards = [{n: local["w"][n][i].astype(_MXU_DTYPE) for n in MXU_WEIGHTS} for i in range(DEPTH)]

    first_w_in, vec = _exchange("gather_first_weights", _gather_items(shards[0], ['w_in'])
                                + [_gather_item(_pack([wts[n] for n in VEC_WEIGHTS], F32))])
    vec_full = {n: _from_slabs(slabs, SHARD_AXIS[n])
                for n, slabs in zip(VEC_WEIGHTS, _unpack(vec, [wts[n].shape for n in VEC_WEIGHTS], lead=N_DEV))}
    small_names = [n for n in WEIGHTS if n not in MXU_WEIGHTS and n != 'final_g']
    layers = [{n: (vec_full[n] if n in vec_full else wts[n])[i] for n in small_names} for i in range(DEPTH)]
    _land(layers[0], ['w_in'], [first_w_in])

    def vec_items(small):
        stacked = [jnp.stack([g[n] for g in small]) for n in VEC_WEIGHTS]
        return [_scatter_item(_pack([_to_slabs(a, SHARD_AXIS[n]) for n, a in zip(VEC_WEIGHTS, stacked)], F32,
                                    lead=N_DEV))]

    loss, grad_x, dgf, small, parts, (vec_parts,) = _local_step(x[0], p[:, 0], loss_target[0], final_g, layers,
                                                               shards, vec_items)
    loss = lax.psum(loss, ("x", "y", "c"))

    grads = {n: jnp.stack([small[i][n] for i in range(DEPTH)]) for n in REPLICATED if n != 'final_g'}
    grads['final_g'] = dgf
    repl_parts, = _exchange("gather_replicated_grads", [_gather_item(_pack([grads[n] for n in REPLICATED], F32))])

    results = {}
    for n in MXU_WEIGHTS:
        w3, m3, v3 = (_as_rows(local[d][n], 1) for d in ("w", "m", "v"))
        outs = _adamw("adamw_" + n, [parts[i][n].reshape((N_DEV,) + w3.shape[1:]) for i in range(DEPTH)], w3, m3, v3)
        for kind, a in zip(("grad", "delta", "new_m", "new_v"), outs):
            a = a.reshape(local["w"][n].shape)
            results[kind, n] = jnp.swapaxes(a, 1, 2) if n == 'w_in' else a
    for names, part, call in ((VEC_WEIGHTS, vec_parts, "adamw_vec"), (REPLICATED, repl_parts, "adamw_replicated")):
        packed = [_pack([d[n] for n in names], F32)[None] for d in (wts, mom, var)]
        outs = _adamw(call, [part], *packed)
        for kind, buf in zip(("grad", "delta", "new_m", "new_v"), outs):
            for n, a in zip(names, _unpack(buf[0], [wts[n].shape for n in names])):
                results[kind, n] = a
    return (loss, grad_x[None]) + tuple(results[kind, n] for kind in ("grad", "delta", "new_m", "new_v")
                                        for n in WEIGHTS)
```

```python
import functools

import numpy as np
import jax
import jax.numpy as jnp
from jax import lax
from jax.experimental import pallas as pl
from jax.experimental.pallas import tpu as pltpu

F32 = jnp.float32
_MXU_DTYPE = jnp.bfloat16
GRAD_DTYPE = jnp.bfloat16

N_DEV = 8
D_MODEL = 1024
DEPTH = 2
CHUNK = 64
PLE_DIM = 256
BW = 512
SG_BLOCK = 128
SG_GROUPS = 4
GLA_HEADS = 4
GLA_DK = 64
GLA_DV = 128
GLA_RANK = 16
GLA_TAU = 16.0
ATT_HEADS = 8
ATT_HD = 64
ATT_BAND = 9
BAND = ATT_BAND * CHUNK
MAX_REL = 256
REL_TABLE = CHUNK + MAX_REL
CONV_K = 31
CONV_HALO = 32
D_FF = 4096
EPS = 1e-6
NEG_INF = -1e30

IN_GROUPS = (("A", 0, 1024), ("B", 1024, 1536), ("a", 2560, 16), ("C", 2576, 1536), ("D", 4112, 1024))
IN_COLS = 5136

ADAM_LR = 0.001
ADAM_B1 = 0.9
ADAM_B2 = 0.999
ADAM_EPS = 1e-08
ADAM_WD = 0.01
ADAM_STEP = 10

ADAMW_TILE = 128
PACK_COLS = 1024
VMEM_LIMIT_MB = 56

_NN = (((1,), (0,)), ((), ()))
_NT = (((1,), (1,)), ((), ()))
_TN = (((0,), (0,)), ((), ()))

WEIGHTS = ['norm1_g', 'w_in', 'sg_ln_g', 'sg_ln_b', 'sg_w', 'sg_b', 'gla_w_a2', 'gla_b_a', 'gla_norm_g',
           'att_rel_bias', 'conv_dw_w', 'conv_dw_b', 'conv_ln_g', 'conv_ln_b', 'w_branch', 'w_gate', 'b_gate',
           'w_out', 'norm2_g', 'w_ff1', 'w_ff2', 'norm3_g', 'w_ple_gate', 'b_ple_gate', 'w_ple', 'final_g']
SHARD_AXIS = {'w_in': 2, 'gla_w_a2': 2, 'att_rel_bias': 2, 'conv_dw_w': 2, 'w_branch': 3, 'w_gate': 2,
              'b_gate': 2, 'w_out': 1, 'w_ff1': 2, 'w_ff2': 1, 'w_ple_gate': 1, 'w_ple': 2}
MXU_WEIGHTS = ('w_in', 'w_branch', 'w_gate', 'w_out', 'w_ff1', 'w_ff2', 'w_ple_gate', 'w_ple')
VEC_WEIGHTS = ('gla_w_a2', 'att_rel_bias', 'conv_dw_w', 'b_gate')
SHARDED = tuple(n for n in WEIGHTS if n in SHARD_AXIS)
REPLICATED = tuple(n for n in WEIGHTS if n not in SHARD_AXIS)


def _mm(a, b, dims=_NN):
    return lax.dot_general(a.astype(_MXU_DTYPE), b.astype(_MXU_DTYPE), dims, preferred_element_type=F32)


def _split3(x):
    x1 = x.astype(jnp.bfloat16)
    r1 = x - x1.astype(F32)
    x2 = r1.astype(jnp.bfloat16)
    x3 = (r1 - x2.astype(F32)).astype(jnp.bfloat16)
    return x1, x2, x3


def _mm_exact_rhs(m, x, dims=_NN):
    return sum(lax.dot_general(m, xi, dims, preferred_element_type=F32) for xi in _split3(x))


def _mm_exact_lhs(x, m, dims=_NN):
    return sum(lax.dot_general(xi, m, dims, preferred_element_type=F32) for xi in _split3(x))


def _sigmoid(x):
    return 1.0 / (1.0 + jnp.exp(-x))


def _gelu(x):
    c = 0.7978845608028654
    t = jnp.tanh(c * (x + 0.044715 * x * x * x))
    return 0.5 * x * (1.0 + t), t


def _gelu_grad(x, t):
    c = 0.7978845608028654
    return 0.5 * (1.0 + t) + 0.5 * x * (1.0 - t * t) * c * (1.0 + 3.0 * 0.044715 * x * x)


def _rms_stat(h):
    return lax.rsqrt(jnp.mean(h * h, axis=-1, keepdims=True) + EPS)


def _rms_bwd(dy, h, g, r):
    hh = h * r
    dhh = dy * g
    dh = r * (dhh - hh * jnp.mean(dhh * hh, axis=-1, keepdims=True))
    return dh, jnp.sum(dy * hh, axis=0, keepdims=True)


def _ln_fwd(x, g, b):
    mu = jnp.mean(x, axis=-1, keepdims=True)
    xc = x - mu
    rs = lax.rsqrt(jnp.mean(xc * xc, axis=-1, keepdims=True) + EPS)
    xh = xc * rs
    return xh * g + b, xh, rs


def _ln_bwd(dy, xh, rs, g):
    dxh = dy * g
    dx = rs * (dxh - jnp.mean(dxh, axis=-1, keepdims=True) - xh * jnp.mean(dxh * xh, axis=-1, keepdims=True))
    return dx, jnp.sum(dy * xh, axis=0, keepdims=True), jnp.sum(dy, axis=0, keepdims=True)


def _row_call(name, body, nt, rows=(), halos=(), res=(), outs=(), accs=(), scratch=(), reverse=False, exchange=()):
    def pos(i):
        return (nt - 1 - i) if reverse else i

    def lead(ndim, f):
        return lambda i: (f(pos(i)),) + (0,) * (ndim - 1)

    in_specs, operands = [], []
    for a, tile in rows:
        in_specs.append(pl.BlockSpec((tile,) + a.shape[1:], lead(a.ndim, lambda t: t)))
        operands.append(a)
    for a, blk, per, side in halos:
        last = a.shape[0] // blk - 1
        delta = {'prev2': -2, 'prev': -1, 'next': per, 'next2': per + 1}[side]
        f = lambda t, per=per, last=last, delta=delta: jnp.clip(t * per + delta, 0, last)
        in_specs.append(pl.BlockSpec((blk,) + a.shape[1:], lead(a.ndim, f)))
        operands.append(a)
    for a in res:
        in_specs.append(pl.BlockSpec(a.shape, lambda i, nd=a.ndim: (0,) * nd, pipeline_mode=pl.Buffered(1)))
        operands.append(a)
    out_specs, out_shape = [], []
    for shape, dtype, tile in outs:
        out_specs.append(pl.BlockSpec((tile,) + tuple(shape[1:]), lead(len(shape), lambda t: t)))
        out_shape.append(jax.ShapeDtypeStruct(tuple(shape), dtype))
    for shape in accs:
        out_specs.append(pl.BlockSpec(tuple(shape), lambda i, nd=len(shape): (0,) * nd))
        out_shape.append(jax.ShapeDtypeStruct(tuple(shape), F32))
    nx = len(exchange)
    any_spec = pl.BlockSpec(memory_space=pl.ANY)
    for it in exchange:
        in_specs.append(any_spec)
        operands.append(it['src'])
        out_specs.append(any_spec)
        out_shape.append(jax.ShapeDtypeStruct(it['out'], it['src'].dtype))
    sizes = (len(rows), len(halos), len(res), nx, len(outs), len(accs), nx, len(scratch), 3 if nx else 0)

    def kern(*refs):
        i = pl.program_id(0)
        groups, at = [], 0
        for n in sizes:
            groups.append(refs[at:at + n])
            at += n
        row_refs, halo_refs, res_refs, x_src, out_refs, acc_refs, x_dst, scr_refs, sems = groups

        @pl.when(i == 0)
        def _():
            for r in tuple(acc_refs) + tuple(scr_refs):
                r[...] = jnp.zeros(r.shape, r.dtype)
            if nx:
                _exchange_copies(exchange, x_src, x_dst, sems, start=True)

        body(pos(i), row_refs, halo_refs, res_refs, out_refs, acc_refs, scr_refs)

        if nx:
            @pl.when(i == nt - 1)
            def _():
                _exchange_copies(exchange, x_src, x_dst, sems, start=False)

    result = pl.pallas_call(
        kern, grid=(nt,), in_specs=in_specs, out_specs=out_specs, out_shape=out_shape,
        scratch_shapes=[pltpu.VMEM(tuple(s), d) for s, d in scratch] + (_exchange_sems(nx) if nx else []),
        compiler_params=pltpu.CompilerParams(dimension_semantics=("arbitrary",),
                                             vmem_limit_bytes=VMEM_LIMIT_MB << 20),
        name=name)(*operands)
    return tuple(result)


def _tn_call(name, a, b, k, n, nblk=1, a_col=False, b_col=True, b_off=0, out='cols', tile=2048):
    tile = min(tile, a.shape[0])
    nt = a.shape[0] // tile
    if out == 'cols':
        o_shape, o_spec = (k, nblk * n), pl.BlockSpec((k, n), lambda j, t: (0, j))
    elif out == 'rows':
        o_shape, o_spec = (nblk * k, n), pl.BlockSpec((k, n), lambda j, t: (j, 0))
    else:
        o_shape, o_spec = (nblk, k, n), pl.BlockSpec((None, k, n), lambda j, t: (j, 0, 0))

    def kern(a_ref, b_ref, o_ref, acc):
        @pl.when(pl.program_id(1) == 0)
        def _():
            acc[...] = jnp.zeros(acc.shape, acc.dtype)

        acc[...] += lax.dot_general(a_ref[...], b_ref[...], _TN, preferred_element_type=F32)

        @pl.when(pl.program_id(1) == nt - 1)
        def _():
            o_ref[...] = acc[...].astype(o_ref.dtype)

    return pl.pallas_call(
        kern, grid=(nblk, nt),
        in_specs=[pl.BlockSpec((tile, k), (lambda j, t: (t, j)) if a_col else (lambda j, t: (t, 0))),
                  pl.BlockSpec((tile, n), (lambda j, t: (t, j + b_off)) if b_col else (lambda j, t: (t, b_off)))],
        out_specs=o_spec, out_shape=jax.ShapeDtypeStruct(o_shape, GRAD_DTYPE),
        scratch_shapes=[pltpu.VMEM((k, n), F32)],
        compiler_params=pltpu.CompilerParams(dimension_semantics=("arbitrary", "arbitrary"),
                                             vmem_limit_bytes=VMEM_LIMIT_MB << 20),
        name=name)(a, b)


def _inproj_fwd(tag, h, g1, w_groups, tm=512, exchange=()):
    t_len = h.shape[0]

    def body(t, rows, halos, res, outs, accs, scr):
        hv = rows[0][...]
        xn = (hv * _rms_stat(hv) * res[0][...]).astype(_MXU_DTYPE)
        outs[0][...] = xn
        for o, w in zip(outs[1:], res[1:]):
            o[...] = lax.dot_general(xn, w[...], _NT, preferred_element_type=F32)

    outs = [((t_len, D_MODEL), _MXU_DTYPE, tm)] + [((t_len, w.shape[0]), F32, tm) for w in w_groups]
    return _row_call("inproj_fwd" + tag, body, t_len // tm, rows=[(h, tm)], res=[g1] + list(w_groups), outs=outs,
                     exchange=exchange)


def _sg_mask():
    row = lax.broadcasted_iota(jnp.int32, (SG_BLOCK, SG_BLOCK), 0)
    col = lax.broadcasted_iota(jnp.int32, (SG_BLOCK, SG_BLOCK), 1)
    return jnp.logical_or(row >= CHUNK, col < CHUNK)


def _sg_forward_parts(pa, lg, lb, w_ref, bt):
    tm = pa.shape[0]
    nb = tm // SG_BLOCK
    su, sv = pa[:, :BW], pa[:, BW:]
    u, tu = _gelu(su)
    gv, tv = _gelu(sv)
    vn, xh, rs = _ln_fwd(gv, lg, lb)
    mask = _sg_mask()
    wms, xs, ms = [], [], []
    for g in range(SG_GROUPS):
        wm = jnp.where(mask, w_ref[g], 0.0).astype(_MXU_DTYPE)
        xg = jnp.concatenate([vn[b * SG_BLOCK:(b + 1) * SG_BLOCK, g * 128:(g + 1) * 128] for b in range(nb)], axis=1)
        xg = xg.astype(_MXU_DTYPE)
        ms.append(lax.dot_general(wm, xg, _NN, preferred_element_type=F32) + bt[:, g:g + 1])
        wms.append(wm)
        xs.append(xg)
    mixed = _sg_unfold(ms, nb)
    return su, sv, u, tu, tv, xh, rs, wms, xs, mixed


def _sg_unfold(per_group, nb):
    return jnp.concatenate(
        [jnp.concatenate([per_group[g][:, b * 128:(b + 1) * 128] for g in range(SG_GROUPS)], axis=1)
         for b in range(nb)], axis=0)


def _sg_fwd(tag, proj_a, lg, lb, sg_w, sg_bt, tm=512):
    t_len = proj_a.shape[0]

    def body(t, rows, halos, res, outs, accs, scr):
        parts = _sg_forward_parts(rows[0][...], res[0][...], res[1][...], res[2], res[3][...])
        outs[0][...] = (parts[2] * parts[-1]).astype(_MXU_DTYPE)

    return _row_call("sg_fwd" + tag, body, t_len // tm, rows=[(proj_a, tm)], res=[lg, lb, sg_w, sg_bt],
                     outs=[((t_len, BW), _MXU_DTYPE, tm)])[0]


def _sg_bwd(tag, proj_a, dy, lg, lb, sg_w, sg_bt, tm=512):
    t_len = proj_a.shape[0]
    nb = tm // SG_BLOCK

    def body(t, rows, halos, res, outs, accs, scr):
        lgv = res[0][...]
        su, sv, u, tu, tv, xh, rs, wms, xs, mixed = _sg_forward_parts(rows[0][...], lgv, res[1][...], res[2], res[3][...])
        dyv = rows[1][...]
        dsu = dyv * mixed * _gelu_grad(su, tu)
        dmixed = dyv * u
        mask = _sg_mask()
        dxs, dbs = [], []
        for g in range(SG_GROUPS):
            dm = jnp.concatenate([dmixed[b * SG_BLOCK:(b + 1) * SG_BLOCK, g * 128:(g + 1) * 128] for b in range(nb)],
                                 axis=1)
            dmb = dm.astype(_MXU_DTYPE)
            dw = lax.dot_general(dmb, xs[g], _NT, preferred_element_type=F32)
            accs[0][g] += jnp.where(mask, dw, 0.0)
            dbs.append(jnp.sum(dm, axis=1, keepdims=True))
            dxs.append(lax.dot_general(wms[g], dmb, _TN, preferred_element_type=F32))
        accs[1][...] += jnp.concatenate(dbs, axis=1)
        dvn = _sg_unfold(dxs, nb)
        dgv, dlg, dlb = _ln_bwd(dvn, xh, rs, lgv)
        accs[2][...] += dlg
        accs[3][...] += dlb
        dsv = dgv * _gelu_grad(sv, tv)
        outs[0][...] = jnp.concatenate([dsu, dsv], axis=1).astype(_MXU_DTYPE)

    return _row_call("sg_bwd" + tag, body, t_len // tm, rows=[(proj_a, tm), (dy, tm)], res=[lg, lb, sg_w, sg_bt],
                     outs=[((t_len, 2 * BW), _MXU_DTYPE, tm)],
                     accs=[(SG_GROUPS, SG_BLOCK, SG_BLOCK), (SG_BLOCK, SG_GROUPS), (1, BW), (1, BW)])


def _chunk_matrix(tm, kind):
    row = lax.broadcasted_iota(jnp.int32, (tm, tm), 0)
    col = lax.broadcasted_iota(jnp.int32, (tm, tm), 1)
    same = lax.shift_right_logical(row, 6) == lax.shift_right_logical(col, 6)
    if kind == 'cumsum':
        same = jnp.logical_and(same, row >= col)
    elif kind == 'revsum':
        same = jnp.logical_and(same, row <= col)
    return same.astype(jnp.bfloat16)


def _gla_gate(pa, wa2, ba):
    z = _mm(pa, wa2) + ba
    log_a = (jnp.minimum(z, 0.0) - jnp.log(1.0 + jnp.exp(-jnp.abs(z)))) * (1.0 / GLA_TAU)
    return z, log_a


def _gla_decay(pb, log_a):
    tm = pb.shape[0]
    cum = _mm_exact_rhs(_chunk_matrix(tm, 'cumsum'), log_a)
    tot = _mm_exact_rhs(_chunk_matrix(tm, 'total'), log_a)
    w = jnp.exp(tot - cum)
    return w, pb[:, 256:512] * w, jnp.exp(tot)


def _per_head(fn):
    return jnp.concatenate([fn(h) for h in range(GLA_HEADS)], axis=1)


def _gla_read(qs, sb, c):
    rows = slice(c * CHUNK, (c + 1) * CHUNK)
    return _per_head(lambda h: lax.dot_general(qs[rows, h * 64:(h + 1) * 64], sb[:, h * 64:(h + 1) * 64], _NT,
                                               preferred_element_type=F32))


def _gla_fwd(tag, proj_b, proj_a, wa2, ba, ng, tm=256, exchange=()):
    t_len = proj_b.shape[0]
    cpt = tm // CHUNK

    def body(t, rows, halos, res, outs, accs, scr):
        pb = rows[0][...]
        _, log_a = _gla_gate(rows[1][...], res[0][...], res[1][...])
        _, kd, dec = _gla_decay(pb, log_a)
        kdb = kd.astype(_MXU_DTYPE)
        vb = pb[:, 512:1024].astype(_MXU_DTYPE)
        qs = (pb[:, 0:256] * (GLA_DK ** -0.5)).astype(_MXU_DTYPE)
        uts = []
        for c in range(cpt):
            rs = slice(c * CHUNK, (c + 1) * CHUNK)
            uts.append(_per_head(lambda h: lax.dot_general(vb[rs, h * 128:(h + 1) * 128], kdb[rs, h * 64:(h + 1) * 64],
                                                           _TN, preferred_element_type=F32)))
        s_new = scr[0][...]
        o = []
        for c in range(cpt):
            s_new = dec[c * CHUNK:c * CHUNK + 1] * s_new + uts[c]
            outs[1][c] = s_new
            o.append(_gla_read(qs, s_new.astype(_MXU_DTYPE), c))
        scr[0][...] = s_new
        o = jnp.concatenate(o, axis=0)
        on = _per_head(lambda h: o[:, h * 128:(h + 1) * 128] * lax.rsqrt(
            jnp.mean(jnp.square(o[:, h * 128:(h + 1) * 128]), axis=-1, keepdims=True) + EPS))
        r = pb[:, 1024:1536]
        outs[0][...] = (on * res[2][...] * (r * _sigmoid(r))).astype(_MXU_DTYPE)

    return _row_call("gla_fwd" + tag, body, t_len // tm, rows=[(proj_b, tm), (proj_a, tm)], res=[wa2, ba, ng],
                     outs=[((t_len, BW), _MXU_DTYPE, tm), ((t_len // CHUNK, GLA_DV, 256), F32, cpt)],
                     scratch=[((GLA_DV, 256), F32)], exchange=exchange)


def _gla_bwd(tag, proj_b, proj_a, dy, states, wa2, ba, ng, tm=256):
    t_len = proj_b.shape[0]
    cpt = tm // CHUNK

    def body(t, rows, halos, res, outs, accs, scr):
        pb = rows[0][...]
        pa = rows[1][...]
        dyv = rows[2][...]
        st_ref = rows[3]
        wa2v = res[0][...]
        z, log_a = _gla_gate(pa, wa2v, res[1][...])
        ngv = res[2][...]
        w, kd, dec = _gla_decay(pb, log_a)
        kdb = kd.astype(_MXU_DTYPE)
        vb = pb[:, 512:1024].astype(_MXU_DTYPE)
        qs = (pb[:, 0:256] * (GLA_DK ** -0.5)).astype(_MXU_DTYPE)
        chunks = [slice(c * CHUNK, (c + 1) * CHUNK) for c in range(cpt)]
        sbs = [st_ref[c].astype(_MXU_DTYPE) for c in range(cpt)]
        o = jnp.concatenate([_gla_read(qs, sbs[c], c) for c in range(cpt)], axis=0)
        r = pb[:, 1024:1536]
        sig = _sigmoid(r)
        sil = r * sig
        dos, ons = [], []
        for h in range(GLA_HEADS):
            hs = slice(h * 128, (h + 1) * 128)
            oh = o[:, hs]
            rstd = lax.rsqrt(jnp.mean(oh * oh, axis=-1, keepdims=True) + EPS)
            on = oh * rstd
            don = dyv[:, hs] * ngv[:, hs] * sil[:, hs]
            dos.append(rstd * (don - on * jnp.mean(don * on, axis=-1, keepdims=True)))
            ons.append(on)
        on = jnp.concatenate(ons, axis=1)
        accs[2][...] += jnp.sum(dyv * on * sil, axis=0, keepdims=True)
        dr = dyv * on * ngv * (sig * (1.0 + r * (1.0 - sig)))
        dob = jnp.concatenate(dos, axis=1).astype(_MXU_DTYPE)
        reads, dqs = [], []
        for c, rs in enumerate(chunks):
            reads.append(_per_head(lambda h: lax.dot_general(dob[rs, h * 128:(h + 1) * 128], qs[rs, h * 64:(h + 1) * 64],
                                                             _TN, preferred_element_type=F32)))
            dqs.append(_per_head(lambda h: lax.dot_general(dob[rs, h * 128:(h + 1) * 128], sbs[c][:, h * 64:(h + 1) * 64],
                                                           _NN, preferred_element_type=F32)))
        dst = scr[0][...]
        dubs, ddecs = [None] * cpt, [None] * cpt
        for c in reversed(range(cpt)):
            dst_tot = dst + reads[c]
            s_prev = st_ref[c - 1] if c > 0 else jnp.where(t > 0, halos[0][0], 0.0)
            ddecs[c] = jnp.broadcast_to(jnp.sum(dst_tot * s_prev, axis=0, keepdims=True), (CHUNK, 256))
            dst = dec[c * CHUNK:c * CHUNK + 1] * dst_tot
            dubs[c] = dst_tot.astype(_MXU_DTYPE)
        scr[0][...] = dst
        dkd = jnp.concatenate(
            [_per_head(lambda h: lax.dot_general(vb[rs, h * 128:(h + 1) * 128], dubs[c][:, h * 64:(h + 1) * 64], _NN,
                                                 preferred_element_type=F32)) for c, rs in enumerate(chunks)], axis=0)
        dv = jnp.concatenate(
            [_per_head(lambda h: lax.dot_general(kdb[rs, h * 64:(h + 1) * 64], dubs[c][:, h * 64:(h + 1) * 64], _NT,
                                                 preferred_element_type=F32)) for c, rs in enumerate(chunks)], axis=0)
        e = dkd * kd
        dtot = _mm_exact_rhs(_chunk_matrix(tm, 'total'), e) + jnp.concatenate(ddecs, axis=0) * dec
        last = (lax.broadcasted_iota(jnp.int32, e.shape, 0) & (CHUNK - 1)) == CHUNK - 1
        dla = _mm_exact_rhs(_chunk_matrix(tm, 'revsum'), jnp.where(last, dtot - e, -e))
        dz = dla * (1.0 / GLA_TAU) * _sigmoid(-z)
        dzb = dz.astype(_MXU_DTYPE)
        dq = jnp.concatenate(dqs, axis=0) * (GLA_DK ** -0.5)
        outs[0][...] = jnp.concatenate([dq, dkd * w, dv, dr], axis=1).astype(_MXU_DTYPE)
        outs[1][...] = lax.dot_general(dzb, wa2v.astype(_MXU_DTYPE), _NT, preferred_element_type=F32).astype(_MXU_DTYPE)
        accs[0][...] += lax.dot_general(pa.astype(_MXU_DTYPE), dzb, _TN, preferred_element_type=F32)
        accs[1][...] += jnp.sum(dz, axis=0, keepdims=True)

    return _row_call("gla_bwd" + tag, body, t_len // tm,
                     rows=[(proj_b, tm), (proj_a, tm), (dy, tm), (states, cpt)],
                     halos=[(states, 1, cpt, 'prev')], res=[wa2, ba, ng],
                     outs=[((t_len, 1536), _MXU_DTYPE, tm), ((t_len, GLA_RANK), _MXU_DTYPE, tm)],
                     accs=[(GLA_RANK, 256), (1, 256), (1, BW)], scratch=[((GLA_DV, 256), F32)], reverse=True)


ATT_TM = 256
ATT_KEYS = ATT_TM + (ATT_BAND - 1) * CHUNK


def _rel_index():
    l_idx = np.arange(CHUNK)[:, None]
    m_idx = np.arange(BAND)[None, :]
    rel = l_idx + (ATT_BAND - 1) * CHUNK - m_idx
    return jnp.asarray((np.clip(rel, -(CHUNK - 1), MAX_REL) + (CHUNK - 1)).reshape(1, CHUNK * BAND), jnp.int32)


BIAS_COLS = 4096


def _bias_expand(tag, rel_bias):
    n = CHUNK * BAND

    def kern(rel_ref, idx_ref, o_ref):
        onehot = (lax.broadcasted_iota(jnp.int32, (REL_TABLE, BIAS_COLS), 0) == idx_ref[...]).astype(jnp.bfloat16)
        o_ref[...] = _mm_exact_lhs(rel_ref[...], onehot)

    return pl.pallas_call(
        kern, grid=(n // BIAS_COLS,),
        in_specs=[pl.BlockSpec((ATT_HEADS, REL_TABLE), lambda i: (0, 0)), pl.BlockSpec((1, BIAS_COLS), lambda i: (0, i))],
        out_specs=pl.BlockSpec((ATT_HEADS, BIAS_COLS), lambda i: (0, i)),
        out_shape=jax.ShapeDtypeStruct((ATT_HEADS, n), F32), name="bias_expand" + tag)(rel_bias, _rel_index())


def _bias_tile(tag, bias):
    per = ATT_TM // CHUNK

    def kern(b_ref, o_ref):
        bv = b_ref[...]
        for j in range(per):
            parts = [jnp.full((CHUNK, j * CHUNK), NEG_INF, F32)] if j else []
            parts.append(bv)
            if j < per - 1:
                parts.append(jnp.full((CHUNK, (per - 1 - j) * CHUNK), NEG_INF, F32))
            o_ref[j * CHUNK:(j + 1) * CHUNK, :] = jnp.concatenate(parts, axis=1)

    return pl.pallas_call(
        kern, grid=(ATT_HEADS,), in_specs=[pl.BlockSpec((None, CHUNK, BAND), lambda h: (h, 0, 0))],
        out_specs=pl.BlockSpec((None, ATT_TM, ATT_KEYS), lambda h: (h, 0, 0)),
        out_shape=jax.ShapeDtypeStruct((ATT_HEADS, ATT_TM, ATT_KEYS), F32), name="bias_tile" + tag)(bias)


def _bias_untile(tag, dbias):
    per = ATT_TM // CHUNK

    def kern(d_ref, o_ref):
        acc = d_ref[0:CHUNK, 0:BAND]
        for j in range(1, per):
            acc = acc + d_ref[j * CHUNK:(j + 1) * CHUNK, j * CHUNK:j * CHUNK + BAND]
        o_ref[...] = acc

    return pl.pallas_call(
        kern, grid=(ATT_HEADS,), in_specs=[pl.BlockSpec((None, ATT_TM, ATT_KEYS), lambda h: (h, 0, 0))],
        out_specs=pl.BlockSpec((None, CHUNK, BAND), lambda h: (h, 0, 0)),
        out_shape=jax.ShapeDtypeStruct((ATT_HEADS, CHUNK, BAND), F32), name="bias_untile" + tag)(dbias)


def _bias_reduce(tag, dbias):
    n = CHUNK * BAND

    def kern(db_ref, idx_ref, o_ref):
        @pl.when(pl.program_id(0) == 0)
        def _():
            o_ref[...] = jnp.zeros(o_ref.shape, o_ref.dtype)

        onehot = (lax.broadcasted_iota(jnp.int32, (REL_TABLE, BIAS_COLS), 0) == idx_ref[...]).astype(jnp.bfloat16)
        o_ref[...] += _mm_exact_lhs(db_ref[...], onehot, _NT)

    return pl.pallas_call(
        kern, grid=(n // BIAS_COLS,),
        in_specs=[pl.BlockSpec((ATT_HEADS, BIAS_COLS), lambda i: (0, i)), pl.BlockSpec((1, BIAS_COLS), lambda i: (0, i))],
        out_specs=pl.BlockSpec((ATT_HEADS, REL_TABLE), lambda i: (0, 0)),
        out_shape=jax.ShapeDtypeStruct((ATT_HEADS, REL_TABLE), F32),
        compiler_params=pltpu.CompilerParams(dimension_semantics=("arbitrary",)),
        name="bias_reduce" + tag)(dbias, _rel_index())


def _attn_stage(t, pc_ref, p1_ref, p2_ref, kv):
    tm = ATT_TM
    kv[0:tm, :] = jnp.where(t > 1, p2_ref[:, 512:1536], 0.0).astype(kv.dtype)
    kv[tm:2 * tm, :] = jnp.where(t > 0, p1_ref[:, 512:1536], 0.0).astype(kv.dtype)
    kv[2 * tm:, :] = pc_ref[:, 512:1536].astype(kv.dtype)
    q = (pc_ref[:, 0:512] * (ATT_HD ** -0.5)).astype(_MXU_DTYPE)
    ok = lax.broadcasted_iota(jnp.int32, (tm, ATT_KEYS), 1) >= (2 - t) * tm
    return q, ok


def _attn_probs(q, kv, bias_h, ok, h):
    hs = slice(h * ATT_HD, (h + 1) * ATT_HD)
    s = lax.dot_general(q[:, hs], kv[:, hs], _NT, preferred_element_type=F32) + bias_h
    s = jnp.where(ok, s, NEG_INF)
    e = jnp.exp(s - jnp.max(s, axis=-1, keepdims=True))
    return e * (1.0 / jnp.sum(e, axis=-1, keepdims=True))


def _attn_halos(proj_c):
    return [(proj_c, ATT_TM, 1, 'prev'), (proj_c, ATT_TM, 1, 'prev2')]


def _attn_fwd(tag, proj_c, bias, exchange=()):
    t_len = proj_c.shape[0]
    tm = ATT_TM

    def body(t, rows, halos, res, outs, accs, scr):
        b_ref, kv = res[0], scr[0]
        q, ok = _attn_stage(t, rows[0], halos[0], halos[1], kv)
        o = []
        for h in range(ATT_HEADS):
            p = _attn_probs(q, kv, b_ref[h], ok, h).astype(_MXU_DTYPE)
            outs[1][:, h * ATT_KEYS:(h + 1) * ATT_KEYS] = p
            o.append(lax.dot_general(p, kv[:, BW + h * ATT_HD:BW + (h + 1) * ATT_HD], _NN, preferred_element_type=F32))
        outs[0][...] = jnp.concatenate(o, axis=1).astype(_MXU_DTYPE)

    return _row_call("attn_fwd" + tag, body, t_len // tm, rows=[(proj_c, tm)], halos=_attn_halos(proj_c),
                     res=[bias], outs=[((t_len, BW), _MXU_DTYPE, tm), ((t_len, ATT_HEADS * ATT_KEYS), _MXU_DTYPE, tm)],
                     scratch=[((ATT_KEYS, 1024), _MXU_DTYPE)], exchange=exchange)


def _attn_bwd(tag, proj_c, dy, probs, exchange=()):
    t_len = proj_c.shape[0]
    tm = ATT_TM
    scale = ATT_HD ** -0.5

    def body(t, rows, halos, res, outs, accs, scr):
        kv = scr[0]
        q, _ = _attn_stage(t, rows[0], halos[0], halos[1], kv)
        do = rows[1][...].astype(_MXU_DTYPE)
        dqs, dks, dvs = [], [], []
        for h in range(ATT_HEADS):
            hs = slice(h * ATT_HD, (h + 1) * ATT_HD)
            vs = slice(BW + h * ATT_HD, BW + (h + 1) * ATT_HD)
            pb = rows[2][:, h * ATT_KEYS:(h + 1) * ATT_KEYS]
            p = pb.astype(F32)
            dp = lax.dot_general(do[:, hs], kv[:, vs], _NT, preferred_element_type=F32)
            ds = p * (dp - jnp.sum(dp * p, axis=-1, keepdims=True))
            accs[0][h] += ds
            dsb = ds.astype(_MXU_DTYPE)
            dqs.append(lax.dot_general(dsb, kv[:, hs], _NN, preferred_element_type=F32) * scale)
            dks.append(lax.dot_general(dsb, q[:, hs], _TN, preferred_element_type=F32))
            dvs.append(lax.dot_general(pb, do[:, hs], _TN, preferred_element_type=F32))
        outs[0][...] = jnp.concatenate(dqs, axis=1).astype(_MXU_DTYPE)
        dkv = jnp.concatenate(dks + dvs, axis=1).astype(GRAD_DTYPE)
        outs[1][...] = dkv[2 * tm:, :]
        outs[2][...] = dkv[tm:2 * tm, :]
        outs[3][...] = dkv[0:tm, :]

    return _row_call("attn_bwd" + tag, body, t_len // tm, rows=[(proj_c, tm), (dy, tm), (probs, tm)],
                     halos=_attn_halos(proj_c),
                     outs=[((t_len, BW), _MXU_DTYPE, tm)] + [((t_len, 1024), GRAD_DTYPE, tm)] * 3,
                     accs=[(ATT_HEADS, ATT_TM, ATT_KEYS)], scratch=[((ATT_KEYS, 1024), _MXU_DTYPE)],
                     exchange=exchange)


def _attn_combine(tag, dq, dkv_own, dkv_prev, dkv_prev2):
    t_len = dq.shape[0]
    tm = ATT_TM
    nt = t_len // tm

    def body(t, rows, halos, res, outs, accs, scr):
        dkv = ((rows[1][...].astype(F32) + jnp.where(t < nt - 1, halos[0][...].astype(F32), 0.0))
               + jnp.where(t < nt - 2, halos[1][...].astype(F32), 0.0))
        outs[0][...] = jnp.concatenate([rows[0][...], dkv.astype(_MXU_DTYPE)], axis=1)

    return _row_call("attn_combine" + tag, body, nt, rows=[(dq, tm), (dkv_own, tm)],
                     halos=[(dkv_prev, tm, 1, 'next'), (dkv_prev2, tm, 1, 'next2')],
                     outs=[((t_len, 1536), _MXU_DTYPE, tm)])[0]


def _conv_glu(pd):
    a, g = pd[:, :BW], pd[:, BW:]
    sig = _sigmoid(g)
    return a, sig, a * sig


def _conv_stage(t, pd_ref, ph_ref, win):
    pd = pd_ref[...]
    a, sig, y0 = _conv_glu(pd)
    win[0:CONV_HALO, :] = jnp.where(t > 0, _conv_glu(ph_ref[...])[2], 0.0)
    win[CONV_HALO:CONV_HALO + pd.shape[0], :] = y0
    return a, sig


SUBLANES = 8


def _conv_shifted(win, sh):
    for b in range(SUBLANES):
        sh[b] = win[pl.ds(b, sh.shape[1]), :]


def _conv_taps_by_copy(offsets):
    groups = {}
    for j, o in enumerate(offsets):
        groups.setdefault(o % SUBLANES, []).append((j, o - o % SUBLANES))
    return [(rem, min(a for _, a in taps), max(a for _, a in taps) - min(a for _, a in taps), taps)
            for rem, taps in sorted(groups.items())]


def _conv_span(sh, rem, r0, lo, rows):
    return sh[rem, pl.ds(pl.multiple_of(r0 + lo, SUBLANES), rows), :]


def _conv_tap_sum(sh, w_ref, offsets, out_ref, init=None, rb=32):
    plan = _conv_taps_by_copy(offsets)

    def block(i, carry):
        r0 = pl.multiple_of(i * rb, rb)
        acc = jnp.zeros((rb, BW), F32) if init is None else jnp.broadcast_to(init, (rb, BW))
        for rem, lo, extra, taps in plan:
            span = _conv_span(sh, rem, r0, lo, extra + rb)
            for j, a in taps:
                acc = acc + w_ref[j:j + 1, :] * span[a - lo:a - lo + rb]
        out_ref[pl.ds(r0, rb), :] = acc
        return carry

    lax.fori_loop(0, out_ref.shape[0] // rb, block, 0)


def _conv_tap_corr(sh, d_ref, offsets, acc_ref, rb=16):
    for rem, lo, extra, taps in _conv_taps_by_copy(offsets):
        def block(i, sums, rem=rem, lo=lo, extra=extra, taps=taps):
            r0 = pl.multiple_of(i * rb, rb)
            d = d_ref[pl.ds(r0, rb), :]
            span = _conv_span(sh, rem, r0, lo, extra + rb)
            out = []
            for s, (j, a) in zip(sums, taps):
                prod = d * span[a - lo:a - lo + rb]
                for k in range(0, rb, SUBLANES):
                    s = s + prod[k:k + SUBLANES]
                out.append(s)
            return tuple(out)

        sums = lax.fori_loop(0, d_ref.shape[0] // rb, block,
                             tuple(jnp.zeros((SUBLANES, BW), F32) for _ in taps), unroll=2)
        for (j, _), s in zip(taps, sums):
            acc_ref[j:j + 1, :] += jnp.sum(s, axis=0, keepdims=True)


def _conv_scratch(tm):
    return [((tm + CONV_HALO + SUBLANES, BW), F32), ((SUBLANES, tm + CONV_HALO, BW), F32)]


def _conv_fwd(tag, proj_d, dw_w, dw_b, ln_g, ln_b, tm=512):
    t_len = proj_d.shape[0]
    lead = CONV_HALO - (CONV_K - 1)

    def body(t, rows, halos, res, outs, accs, scr):
        win, sh = scr
        _conv_stage(t, rows[0], halos[0], win)
        _conv_shifted(win, sh)
        _conv_tap_sum(sh, res[0], [lead + j for j in range(CONV_K)], outs[1], init=res[1][...])
        yl, _, _ = _ln_fwd(outs[1][...], res[2][...], res[3][...])
        outs[0][...] = (yl * _sigmoid(yl)).astype(_MXU_DTYPE)

    return _row_call("conv_fwd" + tag, body, t_len // tm, rows=[(proj_d, tm)],
                     halos=[(proj_d, CONV_HALO, tm // CONV_HALO, 'prev')], res=[dw_w, dw_b, ln_g, ln_b],
                     outs=[((t_len, BW), _MXU_DTYPE, tm), ((t_len, BW), F32, tm)], scratch=_conv_scratch(tm))


def _conv_bwd_norm(tag, yc, dy, ln_g, ln_b, tm=512):
    t_len = yc.shape[0]

    def body(t, rows, halos, res, outs, accs, scr):
        lgv = res[0][...]
        yl, xh, rs = _ln_fwd(rows[0][...], lgv, res[1][...])
        sig = _sigmoid(yl)
        dyl = rows[1][...] * (sig * (1.0 + yl * (1.0 - sig)))
        dyc, dlg, dlb = _ln_bwd(dyl, xh, rs, lgv)
        outs[0][...] = dyc
        accs[0][...] += dlg
        accs[1][...] += dlb
        accs[2][...] += jnp.sum(dyc, axis=0, keepdims=True)

    return _row_call("conv_bwd_norm" + tag, body, t_len // tm, rows=[(yc, tm), (dy, tm)], res=[ln_g, ln_b],
                     outs=[((t_len, BW), F32, tm)], accs=[(1, BW), (1, BW), (1, BW)])


def _conv_bwd_taps(tag, proj_d, dyc, dw_w, tm=512):
    t_len = proj_d.shape[0]
    nt = t_len // tm
    lead = CONV_HALO - (CONV_K - 1)

    def body(t, rows, halos, res, outs, accs, scr):
        win, sh, wd, shd, dy0_ref = scr
        a, sig = _conv_stage(t, rows[0], halos[0], win)
        _conv_shifted(win, sh)
        wd[0:tm, :] = rows[1][...]
        wd[tm:tm + CONV_HALO, :] = jnp.where(t < nt - 1, halos[1][...], 0.0)
        _conv_shifted(wd, shd)
        _conv_tap_corr(sh, rows[1], [lead + j for j in range(CONV_K)], accs[0])
        _conv_tap_sum(shd, res[0], [CONV_K - 1 - j for j in range(CONV_K)], dy0_ref)
        dy0 = dy0_ref[...]
        outs[0][...] = jnp.concatenate([dy0 * sig, dy0 * a * sig * (1.0 - sig)], axis=1).astype(_MXU_DTYPE)

    return _row_call("conv_bwd_taps" + tag, body, nt, rows=[(proj_d, tm), (dyc, tm)],
                     halos=[(proj_d, CONV_HALO, tm // CONV_HALO, 'prev'), (dyc, CONV_HALO, tm // CONV_HALO, 'next')],
                     res=[dw_w], outs=[((t_len, 2 * BW), _MXU_DTYPE, tm)], accs=[(CONV_K, BW)],
                     scratch=_conv_scratch(tm) + _conv_scratch(tm) + [((tm, BW), F32)])


def _merge_fwd(tag, h, xn, ys, w_gate, b_gate, w_branch, w_out, tm=256):
    t_len = h.shape[0]

    def body(t, rows, halos, res, outs, accs, scr):
        xnv = rows[1][...]
        wg_ref, bg_ref, wb_ref, wo_ref = res
        merged = jnp.zeros((tm, D_MODEL), F32)
        for n in range(4):
            cs = slice(n * D_MODEL, (n + 1) * D_MODEL)
            z = lax.dot_general(xnv, wg_ref[n], _NN, preferred_element_type=F32) + bg_ref[n:n + 1, :]
            bo = lax.dot_general(rows[2 + n][...], wb_ref[n], _NN, preferred_element_type=F32)
            outs[0][:, cs] = z.astype(_MXU_DTYPE)
            outs[1][:, cs] = bo.astype(_MXU_DTYPE)
            merged = merged + _sigmoid(z) * bo
        mb = merged.astype(_MXU_DTYPE)
        outs[2][...] = mb
        outs[3][...] = rows[0][...] + lax.dot_general(mb, wo_ref[...], _NN, preferred_element_type=F32)

    return _row_call("merge_fwd" + tag, body, t_len // tm, rows=[(h, tm), (xn, tm)] + [(y, tm) for y in ys],
                     res=[w_gate, b_gate, w_branch, w_out],
                     outs=[((t_len, 4 * D_MODEL), _MXU_DTYPE, tm), ((t_len, 4 * D_MODEL), _MXU_DTYPE, tm),
                           ((t_len, D_MODEL), _MXU_DTYPE, tm), ((t_len, D_MODEL), F32, tm)])


def _merge_bwd(tag, dh, gate_pre, bo, w_gate, w_branch, w_out, tm=256):
    t_len = dh.shape[0]

    def body(t, rows, halos, res, outs, accs, scr):
        wg_ref, wb_ref, wo_ref = res
        dhb = rows[0][...].astype(_MXU_DTYPE)
        outs[0][...] = dhb
        dmerged = lax.dot_general(dhb, wo_ref[...], _NT, preferred_element_type=F32)
        dxn = jnp.zeros((tm, D_MODEL), F32)
        dbg = []
        for n in range(4):
            cs = slice(n * D_MODEL, (n + 1) * D_MODEL)
            g = _sigmoid(rows[1][:, cs].astype(F32))
            dbo = (dmerged * g).astype(_MXU_DTYPE)
            dgp = dmerged * rows[2][:, cs].astype(F32) * (g * (1.0 - g))
            dgb = dgp.astype(_MXU_DTYPE)
            outs[1][:, cs] = dbo
            outs[2][:, cs] = dgb
            outs[4 + n][...] = lax.dot_general(dbo, wb_ref[n], _NT, preferred_element_type=F32)
            dxn = dxn + lax.dot_general(dgb, wg_ref[n], _NT, preferred_element_type=F32)
            dbg.append(jnp.sum(dgp, axis=0, keepdims=True))
        outs[3][...] = dxn
        accs[0][...] += jnp.concatenate(dbg, axis=1)

    return _row_call("merge_bwd" + tag, body, t_len // tm, rows=[(dh, tm), (gate_pre, tm), (bo, tm)],
                     res=[w_gate, w_branch, w_out],
                     outs=[((t_len, D_MODEL), _MXU_DTYPE, tm), ((t_len, 4 * D_MODEL), _MXU_DTYPE, tm),
                           ((t_len, 4 * D_MODEL), _MXU_DTYPE, tm), ((t_len, D_MODEL), F32, tm)]
                     + [((t_len, BW), F32, tm)] * 4,
                     accs=[(1, 4 * D_MODEL)])


FF_COLS = 1024


def _ffn_fwd(tag, h, g2, w1, w2, tm=512):
    t_len = h.shape[0]

    def body(t, rows, halos, res, outs, accs, scr):
        hv = rows[0][...]
        hn = (hv * _rms_stat(hv) * res[0][...]).astype(_MXU_DTYPE)
        outs[0][...] = hn
        acc = hv
        for c in range(D_FF // FF_COLS):
            cs = slice(c * FF_COLS, (c + 1) * FF_COLS)
            pre = lax.dot_general(hn, res[1][:, cs], _NN, preferred_element_type=F32)
            outs[1][:, cs] = pre
            ff = jnp.square(jnp.maximum(pre, 0.0)).astype(_MXU_DTYPE)
            acc = acc + lax.dot_general(ff, res[2][cs, :], _NN, preferred_element_type=F32)
        outs[2][...] = acc

    return _row_call("ffn_fwd" + tag, body, t_len // tm, rows=[(h, tm)], res=[g2, w1, w2],
                     outs=[((t_len, D_MODEL), _MXU_DTYPE, tm), ((t_len, D_FF), F32, tm), ((t_len, D_MODEL), F32, tm)])


def _ffn_bwd(tag, dh, h, pre, g2, w1, w2, tm=256, exchange=()):
    t_len = dh.shape[0]

    def body(t, rows, halos, res, outs, accs, scr):
        dhv = rows[0][...]
        hv = rows[1][...]
        dhb = dhv.astype(_MXU_DTYPE)
        outs[0][...] = dhb
        dhn = jnp.zeros((tm, D_MODEL), F32)
        for c in range(D_FF // FF_COLS):
            cs = slice(c * FF_COLS, (c + 1) * FF_COLS)
            r = jnp.maximum(rows[2][:, cs], 0.0)
            outs[1][:, cs] = (r * r).astype(_MXU_DTYPE)
            dpre = (lax.dot_general(dhb, res[2][cs, :], _NT, preferred_element_type=F32) * (2.0 * r)).astype(_MXU_DTYPE)
            outs[2][:, cs] = dpre
            dhn = dhn + lax.dot_general(dpre, res[1][:, cs], _NT, preferred_element_type=F32)
        dres, dg = _rms_bwd(dhn, hv, res[0][...], _rms_stat(hv))
        outs[3][...] = dhv + dres
        accs[0][...] += dg

    return _row_call("ffn_bwd" + tag, body, t_len // tm, rows=[(dh, tm), (h, tm), (pre, tm)], res=[g2, w1, w2],
                     outs=[((t_len, D_MODEL), _MXU_DTYPE, tm), ((t_len, D_FF), _MXU_DTYPE, tm),
                           ((t_len, D_FF), _MXU_DTYPE, tm), ((t_len, D_MODEL), F32, tm)],
                     accs=[(1, D_MODEL)], exchange=exchange)


def _ple_fwd(tag, h, p, g3, w_pg, b_pg, w_ple, tm=512):
    t_len = h.shape[0]

    def body(t, rows, halos, res, outs, accs, scr):
        hv = rows[0][...]
        hg = (hv * _rms_stat(hv) * res[0][...]).astype(_MXU_DTYPE)
        pb = rows[1][...].astype(_MXU_DTYPE)
        pg = _sigmoid(lax.dot_general(hg, res[1][...], _NN, preferred_element_type=F32) + res[2][...])
        pe = lax.dot_general(pb, res[3][...], _NN, preferred_element_type=F32)
        outs[0][...] = hg
        outs[1][...] = pb
        outs[2][...] = pg
        outs[3][...] = hv + pg * pe

    return _row_call("ple_fwd" + tag, body, t_len // tm, rows=[(h, tm), (p, tm)], res=[g3, w_pg, b_pg, w_ple],
                     outs=[((t_len, D_MODEL), _MXU_DTYPE, tm), ((t_len, PLE_DIM), _MXU_DTYPE, tm),
                           ((t_len, D_MODEL), F32, tm), ((t_len, D_MODEL), F32, tm)])


def _ple_bwd(tag, dh, h, pg, p_b, g3, w_pg, w_ple, tm=512):
    t_len = dh.shape[0]

    def body(t, rows, halos, res, outs, accs, scr):
        dhv = rows[0][...]
        hv = rows[1][...]
        pgv = rows[2][...]
        pe = lax.dot_general(rows[3][...], res[2][...], _NN, preferred_element_type=F32)
        dgp = dhv * pe * (pgv * (1.0 - pgv))
        dgb = dgp.astype(_MXU_DTYPE)
        outs[0][...] = dgb
        outs[1][...] = (dhv * pgv).astype(_MXU_DTYPE)
        dhg = lax.dot_general(dgb, res[1][...], _NT, preferred_element_type=F32)
        dres, dg = _rms_bwd(dhg, hv, res[0][...], _rms_stat(hv))
        outs[2][...] = dhv + dres
        accs[0][...] += jnp.sum(dgp, axis=0, keepdims=True)
        accs[1][...] += dg

    return _row_call("ple_bwd" + tag, body, t_len // tm, rows=[(dh, tm), (h, tm), (pg, tm), (p_b, tm)],
                     res=[g3, w_pg, w_ple],
                     outs=[((t_len, D_MODEL), _MXU_DTYPE, tm), ((t_len, D_MODEL), _MXU_DTYPE, tm),
                           ((t_len, D_MODEL), F32, tm)],
                     accs=[(1, D_MODEL), (1, D_MODEL)])


def _inproj_bwd(tag, dh, h, dxn_gate, dprojs, g1, w_groups, tm=512, exchange=()):
    t_len = dh.shape[0]

    def body(t, rows, halos, res, outs, accs, scr):
        hv = rows[1][...]
        dxn = rows[2][...]
        for dp, w in zip(rows[3:], res[1:]):
            dxn = dxn + lax.dot_general(dp[...], w[...], _NN, preferred_element_type=F32)
        dres, dg = _rms_bwd(dxn, hv, res[0][...], _rms_stat(hv))
        outs[0][...] = rows[0][...] + dres
        accs[0][...] += dg

    return _row_call("inproj_bwd" + tag, body, t_len // tm,
                     rows=[(dh, tm), (h, tm), (dxn_gate, tm)] + [(d, tm) for d in dprojs],
                     res=[g1] + list(w_groups), outs=[((t_len, D_MODEL), F32, tm)], accs=[(1, D_MODEL)],
                     exchange=exchange)


def _loss_head(h, target, gf, tm=512):
    t_len = h.shape[0]

    def body(t, rows, halos, res, outs, accs, scr):
        hv = rows[0][...]
        g = res[0][...]
        r = _rms_stat(hv)
        diff = hv * r * g - rows[1][...]
        accs[0][...] += 0.5 * jnp.sum(jnp.mean(diff * diff, axis=-1, keepdims=True), axis=0, keepdims=True)
        dh, dg = _rms_bwd(diff * (1.0 / D_MODEL), hv, g, r)
        outs[0][...] = dh
        accs[1][...] += dg

    return _row_call("loss_head", body, t_len // tm, rows=[(h, tm), (target, tm)], res=[gf],
                     outs=[((t_len, D_MODEL), F32, tm)], accs=[(1, 128), (1, D_MODEL)])


def _row(v):
    return v.reshape(1, -1)


GATHER_DURING = (('inproj', ('w_gate', 'w_branch', 'w_out')), ('gla', ('w_ff1',)), ('attn', ('w_ff2', 'w_ple_gate', 'w_ple')))
SCATTER_DURING_ATTN = ('w_ple_gate', 'w_ple', 'w_ff1', 'w_ff2', 'w_out', 'w_gate', 'w_branch')


def _gather_items(shards, names):
    items = []
    for n in names:
        s = shards[n]
        ax = SHARD_AXIS[n] - 1
        if n == 'w_in':
            items.append(_gather_item(s))
        else:
            items.append(_gather_item(s, s.shape[:ax] + (N_DEV * s.shape[ax],) + s.shape[ax + 1:], ax))
    return items


def _land(w, names, arrays):
    for n, a in zip(names, arrays):
        w[n] = a.reshape(IN_COLS, D_MODEL) if n == 'w_in' else a


def _layer_fwd(i, h, p_i, w, shards, next_shards):
    tag = "_l%d" % i
    during = dict(GATHER_DURING)
    win = [w['w_in'][s:s + n] for _, s, n in IN_GROUPS]
    res = _inproj_fwd(tag, h, _row(w['norm1_g']), win, exchange=_gather_items(shards, during['inproj']))
    xn, pa, pb, pr, pc, pd = res[:6]
    _land(w, during['inproj'], res[6:])
    sg_bt = w['sg_b'].T
    y_a = _sg_fwd(tag, pa, _row(w['sg_ln_g']), _row(w['sg_ln_b']), w['sg_w'], sg_bt)
    res = _gla_fwd(tag, pb, pr, w['gla_w_a2'], _row(w['gla_b_a']), _row(w['gla_norm_g']),
                   exchange=_gather_items(shards, during['gla']))
    y_b, states = res[:2]
    _land(w, during['gla'], res[2:])
    bias = _bias_tile(tag, _bias_expand(tag, w['att_rel_bias']).reshape(ATT_HEADS, CHUNK, BAND))
    items = _gather_items(shards, during['attn']) + (_gather_items(next_shards, ['w_in']) if next_shards else [])
    res = _attn_fwd(tag, pc, bias, exchange=items)
    y_c, probs = res[:2]
    _land(w, during['attn'], res[2:2 + len(during['attn'])])
    next_w_in = res[-1].reshape(IN_COLS, D_MODEL) if next_shards else None
    y_d, yc = _conv_fwd(tag, pd, w['conv_dw_w'], _row(w['conv_dw_b']), _row(w['conv_ln_g']), _row(w['conv_ln_b']))
    ys = (y_a, y_b, y_c, y_d)
    gate, bo, merged, h1 = _merge_fwd(tag, h, xn, ys, w['w_gate'], w['b_gate'], w['w_branch'], w['w_out'])
    hn, pre, h2 = _ffn_fwd(tag, h1, _row(w['norm2_g']), w['w_ff1'], w['w_ff2'])
    hg, p_b, pg, h3 = _ple_fwd(tag, h2, p_i, _row(w['norm3_g']), w['w_ple_gate'], _row(w['b_ple_gate']), w['w_ple'])
    saved = dict(h=h, xn=xn, pa=pa, pb=pb, pr=pr, pc=pc, pd=pd, states=states, probs=probs, yc=yc, ys=ys, gate=gate,
                 bo=bo, merged=merged, h1=h1, hn=hn, pre=pre, h2=h2, hg=hg, p_b=p_b, pg=pg, win=win, sg_bt=sg_bt)
    return h3, saved, next_w_in


def _layer_bwd(i, dh3, s, w, tail=None):
    tag = "_l%d" % i
    g = {}
    dgp, dpe, dh2, db_pg, dg3 = _ple_bwd(tag, dh3, s['h2'], s['pg'], s['p_b'], _row(w['norm3_g']), w['w_ple_gate'],
                                         w['w_ple'])
    g['b_ple_gate'], g['norm3_g'] = db_pg[0], dg3[0]
    g['w_ple_gate'] = _tn_call("dw_ple_gate" + tag, s['hg'], dgp, D_MODEL, D_MODEL)
    g['w_ple'] = _tn_call("dw_ple" + tag, s['p_b'], dpe, PLE_DIM, D_MODEL)

    dh2b, ffb, dpre, dh1, dg2 = _ffn_bwd(tag, dh2, s['h1'], s['pre'], _row(w['norm2_g']), w['w_ff1'], w['w_ff2'])
    g['norm2_g'] = dg2[0]
    g['w_ff1'] = _tn_call("dw_ff1" + tag, s['hn'], dpre, D_MODEL, FF_COLS, nblk=D_FF // FF_COLS)
    g['w_ff2'] = _tn_call("dw_ff2" + tag, ffb, dh2b, FF_COLS, D_MODEL, nblk=D_FF // FF_COLS, a_col=True, b_col=False,
                          out='rows')

    dh1b, dbo, dgpre, dxn_gate, dy_a, dy_b, dy_c, dy_d, db_gate = _merge_bwd(
        tag, dh1, s['gate'], s['bo'], w['w_gate'], w['w_branch'], w['w_out'])
    g['b_gate'] = db_gate.reshape(4, D_MODEL)
    g['w_out'] = _tn_call("dw_out" + tag, s['merged'], dh1b, D_MODEL, D_MODEL)
    g['w_gate'] = _tn_call("dw_gate" + tag, s['xn'], dgpre, D_MODEL, D_MODEL, nblk=4, out='stack')
    g['w_branch'] = jnp.stack([_tn_call("dw_branch%d%s" % (n, tag), s['ys'][n], dbo, BW, D_MODEL, b_off=n)
                             for n in range(4)])

    lg, lb = _row(w['sg_ln_g']), _row(w['sg_ln_b'])
    dpa, dsg_w, dsg_bt, dlg, dlb = _sg_bwd(tag, s['pa'], dy_a, lg, lb, w['sg_w'], s['sg_bt'])
    g['sg_w'], g['sg_b'], g['sg_ln_g'], g['sg_ln_b'] = dsg_w, dsg_bt.T, dlg[0], dlb[0]

    dpb, dpr, dwa2, dba, dng = _gla_bwd(tag, s['pb'], s['pr'], dy_b, s['states'], w['gla_w_a2'],
                                        _row(w['gla_b_a']), _row(w['gla_norm_g']))
    g['gla_w_a2'], g['gla_b_a'], g['gla_norm_g'] = dwa2, dba[0], dng[0]

    items = [_scatter_item(g.pop(n), axis=SHARD_AXIS[n] - 1) for n in SCATTER_DURING_ATTN]
    res = _attn_bwd(tag, s['pc'], dy_c, s['probs'], exchange=items)
    dq, dkv_own, dkv_prev, dkv_prev2, dbias = res[:5]
    parts = dict(zip(SCATTER_DURING_ATTN, res[5:]))
    dpc = _attn_combine(tag, dq, dkv_own, dkv_prev, dkv_prev2)
    g['att_rel_bias'] = _bias_reduce(tag, _bias_untile(tag, dbias).reshape(ATT_HEADS, CHUNK * BAND))

    cg, cb = _row(w['conv_ln_g']), _row(w['conv_ln_b'])
    dyc, dcg, dcb, ddwb = _conv_bwd_norm(tag, s['yc'], dy_d, cg, cb)
    dpd, ddw = _conv_bwd_taps(tag, s['pd'], dyc, w['conv_dw_w'])
    g['conv_ln_g'], g['conv_ln_b'], g['conv_dw_b'], g['conv_dw_w'] = dcg[0], dcb[0], ddwb[0], ddw

    dprojs = (dpa, dpb, dpr, dpc, dpd)
    dw_in = jnp.concatenate([_tn_call("dw_in%s%s" % (name, tag), dp, s['xn'], n, D_MODEL)
                             for (name, _, n), dp in zip(IN_GROUPS, dprojs)], axis=0)
    items = [_scatter_item(dw_in.reshape(N_DEV, IN_COLS // N_DEV, D_MODEL))] + (tail(g) if tail else [])
    res = _inproj_bwd(tag, dh1, s['h'], dxn_gate, dprojs, _row(w['norm1_g']), s['win'], exchange=items)
    dh0, dg1 = res[:2]
    g['norm1_g'] = dg1[0]
    parts['w_in'] = res[2]
    return dh0, g, parts, res[3:]


def _local_step(x, p, target, final_g, layers, shards, tail):
    h = x
    saved = []
    for i in range(DEPTH):
        nxt = shards[i + 1] if i + 1 < DEPTH else None
        h, s, next_w_in = _layer_fwd(i, h, p[i], layers[i], shards[i], nxt)
        saved.append(s)
        if nxt:
            layers[i + 1]['w_in'] = next_w_in
    dh, loss, dgf = _loss_head(h, target, _row(final_g))
    small, parts, tail_out = [None] * DEPTH, [None] * DEPTH, None
    for i in reversed(range(DEPTH)):
        hook = (lambda g: tail([g] + small[1:])) if i == 0 else None
        dh, small[i], parts[i], out = _layer_bwd(i, dh, saved[i], layers[i], hook)
        if i == 0:
            tail_out = out
    return loss[0, 0], dh, dgf[0], small, parts, tail_out


def _peers():
    x, y, c = lax.axis_index("x"), lax.axis_index("y"), lax.axis_index("c")
    me = 4 * x + 2 * y + c
    out = []
    for k in range(1, N_DEV):
        px = (1 - x) if k & 4 else x
        py = (1 - y) if k & 2 else y
        pc = (1 - c) if k & 1 else c
        out.append((k - 1, (px, py, pc), 4 * px + 2 * py + pc))
    return me, out


def _block(ref, axis, idx, width):
    ix = [slice(None)] * len(ref.shape)
    ix[axis] = pl.ds(pl.multiple_of(idx * width, width), width)
    return ref.at[tuple(ix)]


def _slot(ref, idx):
    return ref.at[idx]


def _whole(ref, idx):
    return ref


def _gather_item(src, out_shape=None, axis=None):
    if axis is None:
        return dict(src=src, out=(N_DEV,) + src.shape, take=_whole, put=_slot)
    return dict(src=src, out=tuple(out_shape), take=_whole,
                put=lambda ref, s: _block(ref, axis, s, src.shape[axis]))


def _scatter_item(src, axis=None, lead=0):
    if axis is None:
        shape = src.shape[:lead] + src.shape[lead + 1:]
        take = lambda ref, s: ref.at[(slice(None),) * lead + (s,)]
    else:
        width = src.shape[axis] // N_DEV
        shape = src.shape[:axis] + (width,) + src.shape[axis + 1:]
        take = lambda ref, s: _block(ref, axis, s, width)
    return dict(src=src, out=(N_DEV,) + shape, take=take, put=_slot)


def _exchange_sems(n):
    return [pltpu.SemaphoreType.DMA((n * (N_DEV - 1),)), pltpu.SemaphoreType.DMA((n * (N_DEV - 1),)),
            pltpu.SemaphoreType.DMA((n,))]


def _exchange_copies(items, src_refs, out_refs, sems, start):
    send_sems, recv_sems, local_sems = sems
    me, peers = _peers()

    def remote(i, k, pos, receiver, sender):
        it = items[i]
        return pltpu.make_async_remote_copy(
            src_ref=it['take'](src_refs[i], receiver), dst_ref=it['put'](out_refs[i], sender),
            send_sem=send_sems.at[i * (N_DEV - 1) + k], recv_sem=recv_sems.at[i * (N_DEV - 1) + k],
            device_id=pos, device_id_type=pl.DeviceIdType.MESH)

    local = [pltpu.make_async_copy(it['take'](src_refs[i], me), it['put'](out_refs[i], me), local_sems.at[i])
             for i, it in enumerate(items)]
    if start:
        for cp in local:
            cp.start()
        for k, pos, flat in peers:
            for i in range(len(items)):
                remote(i, k, pos, flat, me).start()
    else:
        for k, pos, flat in peers:
            for i in range(len(items)):
                remote(i, k, pos, flat, flat).wait_recv()
        for k, pos, flat in peers:
            for i in range(len(items)):
                remote(i, k, pos, flat, me).wait_send()
        for cp in local:
            cp.wait()


def _exchange(name, items):
    n = len(items)

    def body(*refs):
        _exchange_copies(items, refs[:n], refs[n:2 * n], refs[2 * n:], start=True)
        _exchange_copies(items, refs[:n], refs[n:2 * n], refs[2 * n:], start=False)

    any_spec = pl.BlockSpec(memory_space=pl.ANY)
    return pl.pallas_call(
        body, out_shape=[jax.ShapeDtypeStruct(it['out'], it['src'].dtype) for it in items],
        in_specs=[any_spec] * n, out_specs=[any_spec] * n, scratch_shapes=_exchange_sems(n),
        name=name)(*[it['src'] for it in items])


def _pack(arrays, dtype, lead=None):
    flat = [a.astype(dtype).reshape((lead, -1) if lead else (-1,)) for a in arrays]
    cat = jnp.concatenate(flat, axis=-1)
    n = cat.shape[-1]
    rows = -(-n // (PACK_COLS * SUBLANES)) * SUBLANES
    pad = rows * PACK_COLS - n
    if pad:
        cat = jnp.pad(cat, ((0, 0), (0, pad)) if lead else ((0, pad),))
    return cat.reshape((lead, rows, PACK_COLS) if lead else (rows, PACK_COLS))


def _unpack(buf, shapes, lead=None):
    flat = buf.reshape((lead, -1) if lead else (-1,))
    out, off = [], 0
    for shp in shapes:
        n = int(np.prod(shp))
        piece = flat[..., off:off + n]
        out.append(piece.reshape(((lead,) if lead else ()) + tuple(shp)))
        off += n
    return out


def _to_slabs(full, axis):
    shp = full.shape
    split = full.reshape(shp[:axis] + (N_DEV, shp[axis] // N_DEV) + shp[axis + 1:])
    return jnp.moveaxis(split, axis, 0)


def _from_slabs(slabs, axis):
    moved = jnp.moveaxis(slabs, 0, axis)
    shp = moved.shape
    return moved.reshape(shp[:axis] + (shp[axis] * shp[axis + 1],) + shp[axis + 2:])


def _adamw_block(r, c):
    if r % 8:
        return r, 256
    br = min(r, max(8, ADAMW_TILE * PACK_COLS // c))
    while r % br:
        br //= 2
    return br, c


def _adamw(name, partials, w, m, v):
    n_lead, r, c = w.shape
    br, bc = _adamw_block(r, c)
    ni, nj = r // br, c // bc
    c1 = 1.0 - ADAM_B1 ** ADAM_STEP
    c2 = 1.0 - ADAM_B2 ** ADAM_STEP

    def kern(*refs):
        p_refs = refs[:n_lead]
        w_ref, m_ref, v_ref, g_ref, d_ref, nm_ref, nv_ref = refs[n_lead:]
        layer = pl.program_id(0)
        g = None
        for l, p_ref in enumerate(p_refs):
            gl = p_ref[0].astype(F32)
            for s in range(1, N_DEV):
                gl = gl + p_ref[s].astype(F32)
            g = gl if g is None else jnp.where(layer == l, gl, g)
        nm = ADAM_B1 * m_ref[...] + (1.0 - ADAM_B1) * g
        nv = ADAM_B2 * v_ref[...] + (1.0 - ADAM_B2) * jnp.square(g)
        g_ref[...] = g
        nm_ref[...] = nm
        nv_ref[...] = nv
        d_ref[...] = -ADAM_LR * ((nm / c1) / (jnp.sqrt(nv / c2) + ADAM_EPS) + ADAM_WD * w_ref[...])

    def part_spec(mine):
        def index(l, i, j):
            before, after = l < mine, l > mine
            return (0, jnp.where(before, 0, jnp.where(after, ni - 1, i)), jnp.where(before, 0, jnp.where(after, nj - 1, j)))
        return pl.BlockSpec((N_DEV, br, bc), index)

    blk = pl.BlockSpec((None, br, bc), lambda l, i, j: (l, i, j))
    return pl.pallas_call(
        kern, grid=(n_lead, ni, nj),
        in_specs=[part_spec(l) for l in range(n_lead)] + [blk, blk, blk],
        out_specs=[blk] * 4, out_shape=[jax.ShapeDtypeStruct(w.shape, F32)] * 4,
        compiler_params=pltpu.CompilerParams(dimension_semantics=("arbitrary",) * 3),
        name=name)(*partials, w, m, v)


def _as_rows(a, lead):
    return a.reshape(a.shape[:lead] + (-1, a.shape[-1]))


def kernel(x, p, norm1_g, w_in, sg_ln_g, sg_ln_b, sg_w, sg_b, gla_w_a2, gla_b_a, gla_norm_g, att_rel_bias, conv_dw_w, conv_dw_b, conv_ln_g, conv_ln_b, w_branch, w_gate, b_gate, w_out, norm2_g, w_ff1, w_ff2, norm3_g, w_ple_gate, b_ple_gate, w_ple, final_g, loss_target, m_norm1_g, m_w_in, m_sg_ln_g, m_sg_ln_b, m_sg_w, m_sg_b, m_gla_w_a2, m_gla_b_a, m_gla_norm_g, m_att_rel_bias, m_conv_dw_w, m_conv_dw_b, m_conv_ln_g, m_conv_ln_b, m_w_branch, m_w_gate, m_b_gate, m_w_out, m_norm2_g, m_w_ff1, m_w_ff2, m_norm3_g, m_w_ple_gate, m_b_ple_gate, m_w_ple, m_final_g, v_norm1_g, v_w_in, v_sg_ln_g, v_sg_ln_b, v_sg_w, v_sg_b, v_gla_w_a2, v_gla_b_a, v_gla_norm_g, v_att_rel_bias, v_conv_dw_w, v_conv_dw_b, v_conv_ln_g, v_conv_ln_b, v_w_branch, v_w_gate, v_b_gate, v_w_out, v_norm2_g, v_w_ff1, v_w_ff2, v_norm3_g, v_w_ple_gate, v_b_ple_gate, v_w_ple, v_final_g):
    args = locals()
    wts = {n: args[n] for n in WEIGHTS}
    mom = {n: args['m_' + n] for n in WEIGHTS}
    var = {n: args['v_' + n] for n in WEIGHTS}

    local = {d_name: dict(d, w_in=jnp.swapaxes(d['w_in'], 1, 2))
             for d_name, d in (("w", wts), ("m", mom), ("v", var))}
    shards = [{n: local["w"][n][i].astype(_MXU_DTYPE) for n in MXU_WEIGHTS} for i in range(DEPTH)]

    first_w_in, vec = _exchange("gather_first_weights", _gather_items(shards[0], ['w_in'])
                                + [_gather_item(_pack([wts[n] for n in VEC_WEIGHTS], F32))])
    vec_full = {n: _from_slabs(slabs, SHARD_AXIS[n])
                for n, slabs in zip(VEC_WEIGHTS, _unpack(vec, [wts[n].shape for n in VEC_WEIGHTS], lead=N_DEV))}
    small_names = [n for n in WEIGHTS if n not in MXU_WEIGHTS and n != 'final_g']
    layers = [{n: (vec_full[n] if n in vec_full else wts[n])[i] for n in small_names} for i in range(DEPTH)]
    _land(layers[0], ['w_in'], [first_w_in])

    def vec_items(small):
        stacked = [jnp.stack([g[n] for g in small]) for n in VEC_WEIGHTS]
        return [_scatter_item(_pack([_to_slabs(a, SHARD_AXIS[n]) for n, a in zip(VEC_WEIGHTS, stacked)], F32,
                                    lead=N_DEV))]

    loss, grad_x, dgf, small, parts, (vec_parts,) = _local_step(x[0], p[:, 0], loss_target[0], final_g, layers,
                                                               shards, vec_items)
    loss = lax.psum(loss, ("x", "y", "c"))

    grads = {n: jnp.stack([small[i][n] for i in range(DEPTH)]) for n in REPLICATED if n != 'final_g'}
    grads['final_g'] = dgf
    repl_parts, = _exchange("gather_replicated_grads", [_gather_item(_pack([grads[n] for n in REPLICATED], F32))])

    results = {}
    for n in MXU_WEIGHTS:
        w3, m3, v3 = (_as_rows(local[d][n], 1) for d in ("w", "m", "v"))
        outs = _adamw("adamw_" + n, [parts[i][n].reshape((N_DEV,) + w3.shape[1:]) for i in range(DEPTH)], w3, m3, v3)
        for kind, a in zip(("grad", "delta", "new_m", "new_v"), outs):
            a = a.reshape(local["w"][n].shape)
            results[kind, n] = jnp.swapaxes(a, 1, 2) if n == 'w_in' else a
    for names, part, call in ((VEC_WEIGHTS, vec_parts, "adamw_vec"), (REPLICATED, repl_parts, "adamw_replicated")):
        packed = [_pack([d[n] for n in names], F32)[None] for d in (wts, mom, var)]
        outs = _adamw(call, [part], *packed)
        for kind, buf in zip(("grad", "delta", "new_m", "new_v"), outs):
            for n, a in zip(names, _unpack(buf[0], [wts[n].shape for n in names])):
                results[kind, n] = a
    return (loss, grad_x[None]) + tuple(results[kind, n] for kind in ("grad", "delta", "new_m", "new_v")
                                        for n in WEIGHTS)
```

```python
import functools

import numpy as np
import jax
import jax.numpy as jnp
from jax import lax
from jax.experimental import pallas as pl
from jax.experimental.pallas import tpu as pltpu

F32 = jnp.float32
_MXU_DTYPE = jnp.bfloat16
GRAD_DTYPE = jnp.bfloat16

N_DEV = 8
D_MODEL = 1024
DEPTH = 2
CHUNK = 64
PLE_DIM = 256
BW = 512
SG_BLOCK = 128
SG_GROUPS = 4
GLA_HEADS = 4
GLA_DK = 64
GLA_DV = 128
GLA_RANK = 16
GLA_TAU = 16.0
ATT_HEADS = 8
ATT_HD = 64
ATT_BAND = 9
BAND = ATT_BAND * CHUNK
MAX_REL = 256
REL_TABLE = CHUNK + MAX_REL
CONV_K = 31
CONV_HALO = 32
D_FF = 4096
EPS = 1e-6
NEG_INF = -1e30

IN_GROUPS = (("A", 0, 1024), ("B", 1024, 1536), ("a", 2560, 16), ("C", 2576, 1536), ("D", 4112, 1024))
IN_GROUPS_BWD = IN_GROUPS[:3] + (("Cq", 2576, 512), ("Ckv", 3088, 1024)) + IN_GROUPS[4:]
IN_COLS = 5136

ADAM_LR = 0.001
ADAM_B1 = 0.9
ADAM_B2 = 0.999
ADAM_EPS = 1e-08
ADAM_WD = 0.01
ADAM_STEP = 10

ADAMW_TILE = 128
PACK_COLS = 1024
VMEM_LIMIT_MB = 56

_NN = (((1,), (0,)), ((), ()))
_NT = (((1,), (1,)), ((), ()))
_TN = (((0,), (0,)), ((), ()))

WEIGHTS = ['norm1_g', 'w_in', 'sg_ln_g', 'sg_ln_b', 'sg_w', 'sg_b', 'gla_w_a2', 'gla_b_a', 'gla_norm_g',
           'att_rel_bias', 'conv_dw_w', 'conv_dw_b', 'conv_ln_g', 'conv_ln_b', 'w_branch', 'w_gate', 'b_gate',
           'w_out', 'norm2_g', 'w_ff1', 'w_ff2', 'norm3_g', 'w_ple_gate', 'b_ple_gate', 'w_ple', 'final_g']
SHARD_AXIS = {'w_in': 2, 'gla_w_a2': 2, 'att_rel_bias': 2, 'conv_dw_w': 2, 'w_branch': 3, 'w_gate': 2,
              'b_gate': 2, 'w_out': 1, 'w_ff1': 2, 'w_ff2': 1, 'w_ple_gate': 1, 'w_ple': 2}
MXU_WEIGHTS = ('w_in', 'w_branch', 'w_gate', 'w_out', 'w_ff1', 'w_ff2', 'w_ple_gate', 'w_ple')
VEC_WEIGHTS = ('gla_w_a2', 'att_rel_bias', 'conv_dw_w', 'b_gate')
SHARDED = tuple(n for n in WEIGHTS if n in SHARD_AXIS)
REPLICATED = tuple(n for n in WEIGHTS if n not in SHARD_AXIS)


def _mm(a, b, dims=_NN):
    return lax.dot_general(a.astype(_MXU_DTYPE), b.astype(_MXU_DTYPE), dims, preferred_element_type=F32)


def _split3(x):
    x1 = x.astype(jnp.bfloat16)
    r1 = x - x1.astype(F32)
    x2 = r1.astype(jnp.bfloat16)
    x3 = (r1 - x2.astype(F32)).astype(jnp.bfloat16)
    return x1, x2, x3


def _mm_exact_rhs(m, x, dims=_NN):
    return sum(lax.dot_general(m, xi, dims, preferred_element_type=F32) for xi in _split3(x))


def _mm_exact_lhs(x, m, dims=_NN):
    return sum(lax.dot_general(xi, m, dims, preferred_element_type=F32) for xi in _split3(x))


def _sigmoid(x):
    return 1.0 / (1.0 + jnp.exp(-x))


def _gelu(x):
    c = 0.7978845608028654
    t = jnp.tanh(c * (x + 0.044715 * x * x * x))
    return 0.5 * x * (1.0 + t), t


def _gelu_grad(x, t):
    c = 0.7978845608028654
    return 0.5 * (1.0 + t) + 0.5 * x * (1.0 - t * t) * c * (1.0 + 3.0 * 0.044715 * x * x)


def _rms_stat(h):
    return lax.rsqrt(jnp.mean(h * h, axis=-1, keepdims=True) + EPS)


def _rms_bwd(dy, h, g, r):
    hh = h * r
    dhh = dy * g
    dh = r * (dhh - hh * jnp.mean(dhh * hh, axis=-1, keepdims=True))
    return dh, jnp.sum(dy * hh, axis=0, keepdims=True)


def _ln_fwd(x, g, b):
    mu = jnp.mean(x, axis=-1, keepdims=True)
    xc = x - mu
    rs = lax.rsqrt(jnp.mean(xc * xc, axis=-1, keepdims=True) + EPS)
    xh = xc * rs
    return xh * g + b, xh, rs


def _ln_bwd(dy, xh, rs, g):
    dxh = dy * g
    dx = rs * (dxh - jnp.mean(dxh, axis=-1, keepdims=True) - xh * jnp.mean(dxh * xh, axis=-1, keepdims=True))
    return dx, jnp.sum(dy * xh, axis=0, keepdims=True), jnp.sum(dy, axis=0, keepdims=True)


def _row_call(name, body, nt, rows=(), halos=(), res=(), outs=(), accs=(), scratch=(), reverse=False, exchange=()):
    def pos(i):
        return (nt - 1 - i) if reverse else i

    def lead(ndim, f):
        return lambda i: (f(pos(i)),) + (0,) * (ndim - 1)

    in_specs, operands = [], []
    for a, tile in rows:
        in_specs.append(pl.BlockSpec((tile,) + a.shape[1:], lead(a.ndim, lambda t: t)))
        operands.append(a)
    for a, blk, per, side in halos:
        last = a.shape[0] // blk - 1
        delta = {'prev2': -2, 'prev': -1, 'next': per}[side]
        f = lambda t, per=per, last=last, delta=delta: jnp.clip(t * per + delta, 0, last)
        in_specs.append(pl.BlockSpec((blk,) + a.shape[1:], lead(a.ndim, f)))
        operands.append(a)
    for a in res:
        in_specs.append(pl.BlockSpec(a.shape, lambda i, nd=a.ndim: (0,) * nd, pipeline_mode=pl.Buffered(1)))
        operands.append(a)
    out_specs, out_shape = [], []
    for shape, dtype, tile in outs:
        out_specs.append(pl.BlockSpec((tile,) + tuple(shape[1:]), lead(len(shape), lambda t: t)))
        out_shape.append(jax.ShapeDtypeStruct(tuple(shape), dtype))
    for shape in accs:
        out_specs.append(pl.BlockSpec(tuple(shape), lambda i, nd=len(shape): (0,) * nd))
        out_shape.append(jax.ShapeDtypeStruct(tuple(shape), F32))
    nx = len(exchange)
    any_spec = pl.BlockSpec(memory_space=pl.ANY)
    for it in exchange:
        in_specs.append(any_spec)
        operands.append(it['src'])
        out_specs.append(any_spec)
        out_shape.append(jax.ShapeDtypeStruct(it['out'], it['src'].dtype))
    sizes = (len(rows), len(halos), len(res), nx, len(outs), len(accs), nx, len(scratch), 3 if nx else 0)

    def kern(*refs):
        i = pl.program_id(0)
        groups, at = [], 0
        for n in sizes:
            groups.append(refs[at:at + n])
            at += n
        row_refs, halo_refs, res_refs, x_src, out_refs, acc_refs, x_dst, scr_refs, sems = groups

        @pl.when(i == 0)
        def _():
            for r in tuple(acc_refs) + tuple(scr_refs):
                r[...] = jnp.zeros(r.shape, r.dtype)
            if nx:
                _exchange_copies(exchange, x_src, x_dst, sems, start=True)

        body(pos(i), row_refs, halo_refs, res_refs, out_refs, acc_refs, scr_refs)

        if nx:
            @pl.when(i == nt - 1)
            def _():
                _exchange_copies(exchange, x_src, x_dst, sems, start=False)

    result = pl.pallas_call(
        kern, grid=(nt,), in_specs=in_specs, out_specs=out_specs, out_shape=out_shape,
        scratch_shapes=[pltpu.VMEM(tuple(s), d) for s, d in scratch] + (_exchange_sems(nx) if nx else []),
        compiler_params=pltpu.CompilerParams(dimension_semantics=("arbitrary",),
                                             vmem_limit_bytes=VMEM_LIMIT_MB << 20),
        name=name)(*operands)
    return tuple(result)


def _tn_call(name, a, b, k, n, nblk=1, a_col=False, b_col=True, b_off=0, out='cols', tile=2048):
    tile = min(tile, a.shape[0])
    nt = a.shape[0] // tile
    if out == 'cols':
        o_shape, o_spec = (k, nblk * n), pl.BlockSpec((k, n), lambda j, t: (0, j))
    elif out == 'rows':
        o_shape, o_spec = (nblk * k, n), pl.BlockSpec((k, n), lambda j, t: (j, 0))
    else:
        o_shape, o_spec = (nblk, k, n), pl.BlockSpec((None, k, n), lambda j, t: (j, 0, 0))

    def kern(a_ref, b_ref, o_ref, acc):
        @pl.when(pl.program_id(1) == 0)
        def _():
            acc[...] = jnp.zeros(acc.shape, acc.dtype)

        acc[...] += lax.dot_general(a_ref[...], b_ref[...], _TN, preferred_element_type=F32)

        @pl.when(pl.program_id(1) == nt - 1)
        def _():
            o_ref[...] = acc[...].astype(o_ref.dtype)

    return pl.pallas_call(
        kern, grid=(nblk, nt),
        in_specs=[pl.BlockSpec((tile, k), (lambda j, t: (t, j)) if a_col else (lambda j, t: (t, 0))),
                  pl.BlockSpec((tile, n), (lambda j, t: (t, j + b_off)) if b_col else (lambda j, t: (t, b_off)))],
        out_specs=o_spec, out_shape=jax.ShapeDtypeStruct(o_shape, GRAD_DTYPE),
        scratch_shapes=[pltpu.VMEM((k, n), F32)],
        compiler_params=pltpu.CompilerParams(dimension_semantics=("arbitrary", "arbitrary"),
                                             vmem_limit_bytes=VMEM_LIMIT_MB << 20),
        name=name)(a, b)


def _inproj_fwd(tag, h, g1, w_groups, tm=512, exchange=()):
    t_len = h.shape[0]

    def body(t, rows, halos, res, outs, accs, scr):
        hv = rows[0][...]
        xn = (hv * _rms_stat(hv) * res[0][...]).astype(_MXU_DTYPE)
        outs[0][...] = xn
        for o, w in zip(outs[1:], res[1:]):
            o[...] = lax.dot_general(xn, w[...], _NT, preferred_element_type=F32)

    outs = [((t_len, D_MODEL), _MXU_DTYPE, tm)] + [((t_len, w.shape[0]), F32, tm) for w in w_groups]
    return _row_call("inproj_fwd" + tag, body, t_len // tm, rows=[(h, tm)], res=[g1] + list(w_groups), outs=outs,
                     exchange=exchange)


def _sg_mask():
    row = lax.broadcasted_iota(jnp.int32, (SG_BLOCK, SG_BLOCK), 0)
    col = lax.broadcasted_iota(jnp.int32, (SG_BLOCK, SG_BLOCK), 1)
    return jnp.logical_or(row >= CHUNK, col < CHUNK)


def _sg_forward_parts(pa, lg, lb, w_ref, bt):
    tm = pa.shape[0]
    nb = tm // SG_BLOCK
    su, sv = pa[:, :BW], pa[:, BW:]
    u, tu = _gelu(su)
    gv, tv = _gelu(sv)
    vn, xh, rs = _ln_fwd(gv, lg, lb)
    mask = _sg_mask()
    wms, xs, ms = [], [], []
    for g in range(SG_GROUPS):
        wm = jnp.where(mask, w_ref[g], 0.0).astype(_MXU_DTYPE)
        xg = jnp.concatenate([vn[b * SG_BLOCK:(b + 1) * SG_BLOCK, g * 128:(g + 1) * 128] for b in range(nb)], axis=1)
        xg = xg.astype(_MXU_DTYPE)
        ms.append(lax.dot_general(wm, xg, _NN, preferred_element_type=F32) + bt[:, g:g + 1])
        wms.append(wm)
        xs.append(xg)
    mixed = _sg_unfold(ms, nb)
    return su, sv, u, tu, tv, xh, rs, wms, xs, mixed


def _sg_unfold(per_group, nb):
    return jnp.concatenate(
        [jnp.concatenate([per_group[g][:, b * 128:(b + 1) * 128] for g in range(SG_GROUPS)], axis=1)
         for b in range(nb)], axis=0)


def _sg_fwd(tag, proj_a, lg, lb, sg_w, sg_bt, tm=512):
    t_len = proj_a.shape[0]

    def body(t, rows, halos, res, outs, accs, scr):
        parts = _sg_forward_parts(rows[0][...], res[0][...], res[1][...], res[2], res[3][...])
        outs[0][...] = (parts[2] * parts[-1]).astype(_MXU_DTYPE)

    return _row_call("sg_fwd" + tag, body, t_len // tm, rows=[(proj_a, tm)], res=[lg, lb, sg_w, sg_bt],
                     outs=[((t_len, BW), _MXU_DTYPE, tm)])[0]


def _sg_bwd(tag, proj_a, dy, lg, lb, sg_w, sg_bt, tm=512):
    t_len = proj_a.shape[0]
    nb = tm // SG_BLOCK

    def body(t, rows, halos, res, outs, accs, scr):
        lgv = res[0][...]
        su, sv, u, tu, tv, xh, rs, wms, xs, mixed = _sg_forward_parts(rows[0][...], lgv, res[1][...], res[2], res[3][...])
        dyv = rows[1][...]
        dsu = dyv * mixed * _gelu_grad(su, tu)
        dmixed = dyv * u
        mask = _sg_mask()
        dxs, dbs = [], []
        for g in range(SG_GROUPS):
            dm = jnp.concatenate([dmixed[b * SG_BLOCK:(b + 1) * SG_BLOCK, g * 128:(g + 1) * 128] for b in range(nb)],
                                 axis=1)
            dmb = dm.astype(_MXU_DTYPE)
            dw = lax.dot_general(dmb, xs[g], _NT, preferred_element_type=F32)
            accs[0][g] += jnp.where(mask, dw, 0.0)
            dbs.append(jnp.sum(dm, axis=1, keepdims=True))
            dxs.append(lax.dot_general(wms[g], dmb, _TN, preferred_element_type=F32))
        accs[1][...] += jnp.concatenate(dbs, axis=1)
        dvn = _sg_unfold(dxs, nb)
        dgv, dlg, dlb = _ln_bwd(dvn, xh, rs, lgv)
        accs[2][...] += dlg
        accs[3][...] += dlb
        dsv = dgv * _gelu_grad(sv, tv)
        outs[0][...] = jnp.concatenate([dsu, dsv], axis=1).astype(_MXU_DTYPE)

    return _row_call("sg_bwd" + tag, body, t_len // tm, rows=[(proj_a, tm), (dy, tm)], res=[lg, lb, sg_w, sg_bt],
                     outs=[((t_len, 2 * BW), _MXU_DTYPE, tm)],
                     accs=[(SG_GROUPS, SG_BLOCK, SG_BLOCK), (SG_BLOCK, SG_GROUPS), (1, BW), (1, BW)])


def _chunk_matrix(tm, kind):
    row = lax.broadcasted_iota(jnp.int32, (tm, tm), 0)
    col = lax.broadcasted_iota(jnp.int32, (tm, tm), 1)
    same = lax.shift_right_logical(row, 6) == lax.shift_right_logical(col, 6)
    if kind == 'cumsum':
        same = jnp.logical_and(same, row >= col)
    elif kind == 'revsum':
        same = jnp.logical_and(same, row <= col)
    return same.astype(jnp.bfloat16)


def _gla_gate(pa, wa2, ba):
    z = _mm(pa, wa2) + ba
    log_a = (jnp.minimum(z, 0.0) - jnp.log(1.0 + jnp.exp(-jnp.abs(z)))) * (1.0 / GLA_TAU)
    return z, log_a


def _gla_decay(pb, log_a):
    tm = pb.shape[0]
    cum = _mm_exact_rhs(_chunk_matrix(tm, 'cumsum'), log_a)
    tot = _mm_exact_rhs(_chunk_matrix(tm, 'total'), log_a)
    w = jnp.exp(tot - cum)
    return w, pb[:, 256:512] * w, jnp.exp(tot)


def _per_head(fn):
    return jnp.concatenate([fn(h) for h in range(GLA_HEADS)], axis=1)


def _gla_read(qs, sb, c):
    rows = slice(c * CHUNK, (c + 1) * CHUNK)
    return _per_head(lambda h: lax.dot_general(qs[rows, h * 64:(h + 1) * 64], sb[:, h * 64:(h + 1) * 64], _NT,
                                               preferred_element_type=F32))


def _gla_fwd(tag, proj_b, proj_a, wa2, ba, ng, tm=512, exchange=()):
    t_len = proj_b.shape[0]
    cpt = tm // CHUNK

    def body(t, rows, halos, res, outs, accs, scr):
        pb = rows[0][...]
        _, log_a = _gla_gate(rows[1][...], res[0][...], res[1][...])
        _, kd, dec = _gla_decay(pb, log_a)
        kdb = kd.astype(_MXU_DTYPE)
        vb = pb[:, 512:1024].astype(_MXU_DTYPE)
        qs = (pb[:, 0:256] * (GLA_DK ** -0.5)).astype(_MXU_DTYPE)
        uts = []
        for c in range(cpt):
            rs = slice(c * CHUNK, (c + 1) * CHUNK)
            uts.append(_per_head(lambda h: lax.dot_general(vb[rs, h * 128:(h + 1) * 128], kdb[rs, h * 64:(h + 1) * 64],
                                                           _TN, preferred_element_type=F32)))
        s_new = scr[0][...]
        o = []
        for c in range(cpt):
            s_new = dec[c * CHUNK:c * CHUNK + 1] * s_new + uts[c]
            outs[1][c] = s_new
            o.append(_gla_read(qs, s_new.astype(_MXU_DTYPE), c))
        scr[0][...] = s_new
        o = jnp.concatenate(o, axis=0)
        on = _per_head(lambda h: o[:, h * 128:(h + 1) * 128] * lax.rsqrt(
            jnp.mean(jnp.square(o[:, h * 128:(h + 1) * 128]), axis=-1, keepdims=True) + EPS))
        r = pb[:, 1024:1536]
        outs[0][...] = (on * res[2][...] * (r * _sigmoid(r))).astype(_MXU_DTYPE)

    return _row_call("gla_fwd" + tag, body, t_len // tm, rows=[(proj_b, tm), (proj_a, tm)], res=[wa2, ba, ng],
                     outs=[((t_len, BW), _MXU_DTYPE, tm), ((t_len // CHUNK, GLA_DV, 256), F32, cpt)],
                     scratch=[((GLA_DV, 256), F32)], exchange=exchange)


def _gla_bwd(tag, proj_b, proj_a, dy, states, wa2, ba, ng, tm=512):
    t_len = proj_b.shape[0]
    cpt = tm // CHUNK

    def body(t, rows, halos, res, outs, accs, scr):
        pb = rows[0][...]
        pa = rows[1][...]
        dyv = rows[2][...]
        st_ref = rows[3]
        wa2v = res[0][...]
        z, log_a = _gla_gate(pa, wa2v, res[1][...])
        ngv = res[2][...]
        w, kd, dec = _gla_decay(pb, log_a)
        kdb = kd.astype(_MXU_DTYPE)
        vb = pb[:, 512:1024].astype(_MXU_DTYPE)
        qs = (pb[:, 0:256] * (GLA_DK ** -0.5)).astype(_MXU_DTYPE)
        chunks = [slice(c * CHUNK, (c + 1) * CHUNK) for c in range(cpt)]
        sbs = [st_ref[c].astype(_MXU_DTYPE) for c in range(cpt)]
        o = jnp.concatenate([_gla_read(qs, sbs[c], c) for c in range(cpt)], axis=0)
        r = pb[:, 1024:1536]
        sig = _sigmoid(r)
        sil = r * sig
        dos, ons = [], []
        for h in range(GLA_HEADS):
            hs = slice(h * 128, (h + 1) * 128)
            oh = o[:, hs]
            rstd = lax.rsqrt(jnp.mean(oh * oh, axis=-1, keepdims=True) + EPS)
            on = oh * rstd
            don = dyv[:, hs] * ngv[:, hs] * sil[:, hs]
            dos.append(rstd * (don - on * jnp.mean(don * on, axis=-1, keepdims=True)))
            ons.append(on)
        on = jnp.concatenate(ons, axis=1)
        accs[2][...] += jnp.sum(dyv * on * sil, axis=0, keepdims=True)
        dr = dyv * on * ngv * (sig * (1.0 + r * (1.0 - sig)))
        dob = jnp.concatenate(dos, axis=1).astype(_MXU_DTYPE)
        reads, dqs = [], []
        for c, rs in enumerate(chunks):
            reads.append(_per_head(lambda h: lax.dot_general(dob[rs, h * 128:(h + 1) * 128], qs[rs, h * 64:(h + 1) * 64],
                                                             _TN, preferred_element_type=F32)))
            dqs.append(_per_head(lambda h: lax.dot_general(dob[rs, h * 128:(h + 1) * 128], sbs[c][:, h * 64:(h + 1) * 64],
                                                           _NN, preferred_element_type=F32)))
        dst = scr[0][...]
        dubs, ddecs = [None] * cpt, [None] * cpt
        for c in reversed(range(cpt)):
            dst_tot = dst + reads[c]
            s_prev = st_ref[c - 1] if c > 0 else jnp.where(t > 0, halos[0][0], 0.0)
            ddecs[c] = jnp.broadcast_to(jnp.sum(dst_tot * s_prev, axis=0, keepdims=True), (CHUNK, 256))
            dst = dec[c * CHUNK:c * CHUNK + 1] * dst_tot
            dubs[c] = dst_tot.astype(_MXU_DTYPE)
        scr[0][...] = dst
        dkd = jnp.concatenate(
            [_per_head(lambda h: lax.dot_general(vb[rs, h * 128:(h + 1) * 128], dubs[c][:, h * 64:(h + 1) * 64], _NN,
                                                 preferred_element_type=F32)) for c, rs in enumerate(chunks)], axis=0)
        dv = jnp.concatenate(
            [_per_head(lambda h: lax.dot_general(kdb[rs, h * 64:(h + 1) * 64], dubs[c][:, h * 64:(h + 1) * 64], _NT,
                                                 preferred_element_type=F32)) for c, rs in enumerate(chunks)], axis=0)
        e = dkd * kd
        dtot = _mm_exact_rhs(_chunk_matrix(tm, 'total'), e) + jnp.concatenate(ddecs, axis=0) * dec
        last = (lax.broadcasted_iota(jnp.int32, e.shape, 0) & (CHUNK - 1)) == CHUNK - 1
        dla = _mm_exact_rhs(_chunk_matrix(tm, 'revsum'), jnp.where(last, dtot - e, -e))
        dz = dla * (1.0 / GLA_TAU) * _sigmoid(-z)
        dzb = dz.astype(_MXU_DTYPE)
        dq = jnp.concatenate(dqs, axis=0) * (GLA_DK ** -0.5)
        outs[0][...] = jnp.concatenate([dq, dkd * w, dv, dr], axis=1).astype(_MXU_DTYPE)
        outs[1][...] = lax.dot_general(dzb, wa2v.astype(_MXU_DTYPE), _NT, preferred_element_type=F32).astype(_MXU_DTYPE)
        accs[0][...] += lax.dot_general(pa.astype(_MXU_DTYPE), dzb, _TN, preferred_element_type=F32)
        accs[1][...] += jnp.sum(dz, axis=0, keepdims=True)

    return _row_call("gla_bwd" + tag, body, t_len // tm,
                     rows=[(proj_b, tm), (proj_a, tm), (dy, tm), (states, cpt)],
                     halos=[(states, 1, cpt, 'prev')], res=[wa2, ba, ng],
                     outs=[((t_len, 1536), _MXU_DTYPE, tm), ((t_len, GLA_RANK), _MXU_DTYPE, tm)],
                     accs=[(GLA_RANK, 256), (1, 256), (1, BW)], scratch=[((GLA_DV, 256), F32)], reverse=True)


ATT_TM = 256
ATT_KEYS = ATT_TM + (ATT_BAND - 1) * CHUNK


def _rel_index():
    l_idx = np.arange(CHUNK)[:, None]
    m_idx = np.arange(BAND)[None, :]
    rel = l_idx + (ATT_BAND - 1) * CHUNK - m_idx
    return jnp.asarray((np.clip(rel, -(CHUNK - 1), MAX_REL) + (CHUNK - 1)).reshape(1, CHUNK * BAND), jnp.int32)


BIAS_COLS = 4096


def _bias_expand(tag, rel_bias):
    n = CHUNK * BAND

    def kern(rel_ref, idx_ref, o_ref):
        onehot = (lax.broadcasted_iota(jnp.int32, (REL_TABLE, BIAS_COLS), 0) == idx_ref[...]).astype(jnp.bfloat16)
        o_ref[...] = _mm_exact_lhs(rel_ref[...], onehot)

    return pl.pallas_call(
        kern, grid=(n // BIAS_COLS,),
        in_specs=[pl.BlockSpec((ATT_HEADS, REL_TABLE), lambda i: (0, 0)), pl.BlockSpec((1, BIAS_COLS), lambda i: (0, i))],
        out_specs=pl.BlockSpec((ATT_HEADS, BIAS_COLS), lambda i: (0, i)),
        out_shape=jax.ShapeDtypeStruct((ATT_HEADS, n), F32), name="bias_expand" + tag)(rel_bias, _rel_index())


def _bias_tile(tag, bias):
    per = ATT_TM // CHUNK

    def kern(b_ref, o_ref):
        bv = b_ref[...]
        for j in range(per):
            parts = [jnp.full((CHUNK, j * CHUNK), NEG_INF, F32)] if j else []
            parts.append(bv)
            if j < per - 1:
                parts.append(jnp.full((CHUNK, (per - 1 - j) * CHUNK), NEG_INF, F32))
            o_ref[j * CHUNK:(j + 1) * CHUNK, :] = jnp.concatenate(parts, axis=1)

    return pl.pallas_call(
        kern, grid=(ATT_HEADS,), in_specs=[pl.BlockSpec((None, CHUNK, BAND), lambda h: (h, 0, 0))],
        out_specs=pl.BlockSpec((None, ATT_TM, ATT_KEYS), lambda h: (h, 0, 0)),
        out_shape=jax.ShapeDtypeStruct((ATT_HEADS, ATT_TM, ATT_KEYS), F32), name="bias_tile" + tag)(bias)


def _bias_untile(tag, dbias):
    per = ATT_TM // CHUNK

    def kern(d_ref, o_ref):
        acc = d_ref[0:CHUNK, 0:BAND]
        for j in range(1, per):
            acc = acc + d_ref[j * CHUNK:(j + 1) * CHUNK, j * CHUNK:j * CHUNK + BAND]
        o_ref[...] = acc

    return pl.pallas_call(
        kern, grid=(ATT_HEADS,), in_specs=[pl.BlockSpec((None, ATT_TM, ATT_KEYS), lambda h: (h, 0, 0))],
        out_specs=pl.BlockSpec((None, CHUNK, BAND), lambda h: (h, 0, 0)),
        out_shape=jax.ShapeDtypeStruct((ATT_HEADS, CHUNK, BAND), F32), name="bias_untile" + tag)(dbias)


def _bias_reduce(tag, dbias):
    n = CHUNK * BAND

    def kern(db_ref, idx_ref, o_ref):
        @pl.when(pl.program_id(0) == 0)
        def _():
            o_ref[...] = jnp.zeros(o_ref.shape, o_ref.dtype)

        onehot = (lax.broadcasted_iota(jnp.int32, (REL_TABLE, BIAS_COLS), 0) == idx_ref[...]).astype(jnp.bfloat16)
        o_ref[...] += _mm_exact_lhs(db_ref[...], onehot, _NT)

    return pl.pallas_call(
        kern, grid=(n // BIAS_COLS,),
        in_specs=[pl.BlockSpec((ATT_HEADS, BIAS_COLS), lambda i: (0, i)), pl.BlockSpec((1, BIAS_COLS), lambda i: (0, i))],
        out_specs=pl.BlockSpec((ATT_HEADS, REL_TABLE), lambda i: (0, 0)),
        out_shape=jax.ShapeDtypeStruct((ATT_HEADS, REL_TABLE), F32),
        compiler_params=pltpu.CompilerParams(dimension_semantics=("arbitrary",)),
        name="bias_reduce" + tag)(dbias, _rel_index())


def _attn_stage(t, pc_ref, p1_ref, p2_ref, kv):
    tm = ATT_TM
    kv[0:tm, :] = jnp.where(t > 1, p2_ref[:, 512:1536], 0.0).astype(kv.dtype)
    kv[tm:2 * tm, :] = jnp.where(t > 0, p1_ref[:, 512:1536], 0.0).astype(kv.dtype)
    kv[2 * tm:, :] = pc_ref[:, 512:1536].astype(kv.dtype)
    q = (pc_ref[:, 0:512] * (ATT_HD ** -0.5)).astype(_MXU_DTYPE)
    ok = lax.broadcasted_iota(jnp.int32, (tm, ATT_KEYS), 1) >= (2 - t) * tm
    return q, ok


def _attn_probs(q, kv, bias_h, ok, h):
    hs = slice(h * ATT_HD, (h + 1) * ATT_HD)
    s = lax.dot_general(q[:, hs], kv[:, hs], _NT, preferred_element_type=F32) + bias_h
    s = jnp.where(ok, s, NEG_INF)
    e = jnp.exp(s - jnp.max(s, axis=-1, keepdims=True))
    return e * (1.0 / jnp.sum(e, axis=-1, keepdims=True))


def _attn_halos(proj_c):
    return [(proj_c, ATT_TM, 1, 'prev'), (proj_c, ATT_TM, 1, 'prev2')]


def _attn_fwd(tag, proj_c, bias, exchange=()):
    t_len = proj_c.shape[0]
    tm = ATT_TM

    def body(t, rows, halos, res, outs, accs, scr):
        b_ref, kv = res[0], scr[0]
        q, ok = _attn_stage(t, rows[0], halos[0], halos[1], kv)
        o = []
        for h in range(ATT_HEADS):
            p = _attn_probs(q, kv, b_ref[h], ok, h).astype(_MXU_DTYPE)
            outs[1][:, h * ATT_KEYS:(h + 1) * ATT_KEYS] = p
            o.append(lax.dot_general(p, kv[:, BW + h * ATT_HD:BW + (h + 1) * ATT_HD], _NN, preferred_element_type=F32))
        outs[0][...] = jnp.concatenate(o, axis=1).astype(_MXU_DTYPE)

    return _row_call("attn_fwd" + tag, body, t_len // tm, rows=[(proj_c, tm)], halos=_attn_halos(proj_c),
                     res=[bias], outs=[((t_len, BW), _MXU_DTYPE, tm), ((t_len, ATT_HEADS * ATT_KEYS), _MXU_DTYPE, tm)],
                     scratch=[((ATT_KEYS, 1024), _MXU_DTYPE)], exchange=exchange)


def _attn_bwd(tag, proj_c, dy, probs, exchange=()):
    t_len = proj_c.shape[0]
    tm = ATT_TM
    scale = ATT_HD ** -0.5

    def body(t, rows, halos, res, outs, accs, scr):
        kv = scr[0]
        q, _ = _attn_stage(t, rows[0], halos[0], halos[1], kv)
        do = rows[1][...].astype(_MXU_DTYPE)
        dqs, dks, dvs = [], [], []
        for h in range(ATT_HEADS):
            hs = slice(h * ATT_HD, (h + 1) * ATT_HD)
            vs = slice(BW + h * ATT_HD, BW + (h + 1) * ATT_HD)
            pb = rows[2][:, h * ATT_KEYS:(h + 1) * ATT_KEYS]
            p = pb.astype(F32)
            dp = lax.dot_general(do[:, hs], kv[:, vs], _NT, preferred_element_type=F32)
            ds = p * (dp - jnp.sum(dp * p, axis=-1, keepdims=True))
            accs[0][h] += ds
            dsb = ds.astype(_MXU_DTYPE)
            dqs.append(lax.dot_general(dsb, kv[:, hs], _NN, preferred_element_type=F32) * scale)
            dks.append(lax.dot_general(dsb, q[:, hs], _TN, preferred_element_type=F32))
            dvs.append(lax.dot_general(pb, do[:, hs], _TN, preferred_element_type=F32))
        outs[0][...] = jnp.concatenate(dqs, axis=1).astype(_MXU_DTYPE)
        dkv = jnp.concatenate(dks + dvs, axis=1)
        after_one, after_two = scr[1], scr[2]
        outs[1][...] = (dkv[2 * tm:, :] + after_two[...]).astype(_MXU_DTYPE)
        after_two[...] = dkv[tm:2 * tm, :] + after_one[...]
        after_one[...] = dkv[0:tm, :]

    return _row_call("attn_bwd" + tag, body, t_len // tm, rows=[(proj_c, tm), (dy, tm), (probs, tm)],
                     halos=_attn_halos(proj_c),
                     outs=[((t_len, BW), _MXU_DTYPE, tm), ((t_len, 1024), _MXU_DTYPE, tm)],
                     accs=[(ATT_HEADS, ATT_TM, ATT_KEYS)],
                     scratch=[((ATT_KEYS, 1024), _MXU_DTYPE), ((tm, 1024), F32), ((tm, 1024), F32)],
                     reverse=True, exchange=exchange)


def _conv_glu(pd):
    a, g = pd[:, :BW], pd[:, BW:]
    sig = _sigmoid(g)
    return a, sig, a * sig


def _conv_stage(t, pd_ref, ph_ref, win):
    pd = pd_ref[...]
    a, sig, y0 = _conv_glu(pd)
    win[0:CONV_HALO, :] = jnp.where(t > 0, _conv_glu(ph_ref[...])[2], 0.0)
    win[CONV_HALO:CONV_HALO + pd.shape[0], :] = y0
    return a, sig


SUBLANES = 8


def _conv_shifted(win, sh):
    for b in range(SUBLANES):
        sh[b] = win[pl.ds(b, sh.shape[1]), :]


def _conv_taps_by_copy(offsets):
    groups = {}
    for j, o in enumerate(offsets):
        groups.setdefault(o % SUBLANES, []).append((j, o - o % SUBLANES))
    return [(rem, min(a for _, a in taps), max(a for _, a in taps) - min(a for _, a in taps), taps)
            for rem, taps in sorted(groups.items())]


def _conv_span(sh, rem, r0, lo, rows):
    return sh[rem, pl.ds(pl.multiple_of(r0 + lo, SUBLANES), rows), :]


def _conv_tap_sum(sh, w_ref, offsets, out_ref, init=None, rb=32):
    plan = _conv_taps_by_copy(offsets)

    def block(i, carry):
        r0 = pl.multiple_of(i * rb, rb)
        acc = jnp.zeros((rb, BW), F32) if init is None else jnp.broadcast_to(init, (rb, BW))
        for rem, lo, extra, taps in plan:
            span = _conv_span(sh, rem, r0, lo, extra + rb)
            for j, a in taps:
                acc = acc + w_ref[j:j + 1, :] * span[a - lo:a - lo + rb]
        out_ref[pl.ds(r0, rb), :] = acc
        return carry

    lax.fori_loop(0, out_ref.shape[0] // rb, block, 0)


def _conv_tap_corr(sh, d_ref, offsets, acc_ref, rb=16):
    for rem, lo, extra, taps in _conv_taps_by_copy(offsets):
        def block(i, sums, rem=rem, lo=lo, extra=extra, taps=taps):
            r0 = pl.multiple_of(i * rb, rb)
            d = d_ref[pl.ds(r0, rb), :]
            span = _conv_span(sh, rem, r0, lo, extra + rb)
            out = []
            for s, (j, a) in zip(sums, taps):
                prod = d * span[a - lo:a - lo + rb]
                for k in range(0, rb, SUBLANES):
                    s = s + prod[k:k + SUBLANES]
                out.append(s)
            return tuple(out)

        sums = lax.fori_loop(0, d_ref.shape[0] // rb, block,
                             tuple(jnp.zeros((SUBLANES, BW), F32) for _ in taps), unroll=2)
        for (j, _), s in zip(taps, sums):
            acc_ref[j:j + 1, :] += jnp.sum(s, axis=0, keepdims=True)


def _conv_scratch(tm):
    return [((tm + CONV_HALO + SUBLANES, BW), F32), ((SUBLANES, tm + CONV_HALO, BW), F32)]


def _conv_fwd(tag, proj_d, dw_w, dw_b, ln_g, ln_b, tm=512):
    t_len = proj_d.shape[0]
    lead = CONV_HALO - (CONV_K - 1)

    def body(t, rows, halos, res, outs, accs, scr):
        win, sh = scr
        _conv_stage(t, rows[0], halos[0], win)
        _conv_shifted(win, sh)
        _conv_tap_sum(sh, res[0], [lead + j for j in range(CONV_K)], outs[1], init=res[1][...])
        yl, _, _ = _ln_fwd(outs[1][...], res[2][...], res[3][...])
        outs[0][...] = (yl * _sigmoid(yl)).astype(_MXU_DTYPE)

    return _row_call("conv_fwd" + tag, body, t_len // tm, rows=[(proj_d, tm)],
                     halos=[(proj_d, CONV_HALO, tm // CONV_HALO, 'prev')], res=[dw_w, dw_b, ln_g, ln_b],
                     outs=[((t_len, BW), _MXU_DTYPE, tm), ((t_len, BW), F32, tm)], scratch=_conv_scratch(tm))


def _conv_bwd_norm(tag, yc, dy, ln_g, ln_b, tm=512):
    t_len = yc.shape[0]

    def body(t, rows, halos, res, outs, accs, scr):
        lgv = res[0][...]
        yl, xh, rs = _ln_fwd(rows[0][...], lgv, res[1][...])
        sig = _sigmoid(yl)
        dyl = rows[1][...] * (sig * (1.0 + yl * (1.0 - sig)))
        dyc, dlg, dlb = _ln_bwd(dyl, xh, rs, lgv)
        outs[0][...] = dyc
        accs[0][...] += dlg
        accs[1][...] += dlb
        accs[2][...] += jnp.sum(dyc, axis=0, keepdims=True)

    return _row_call("conv_bwd_norm" + tag, body, t_len // tm, rows=[(yc, tm), (dy, tm)], res=[ln_g, ln_b],
                     outs=[((t_len, BW), F32, tm)], accs=[(1, BW), (1, BW), (1, BW)])


def _conv_bwd_taps(tag, proj_d, dyc, dw_w, tm=512):
    t_len = proj_d.shape[0]
    nt = t_len // tm
    lead = CONV_HALO - (CONV_K - 1)

    def body(t, rows, halos, res, outs, accs, scr):
        win, sh, wd, shd, dy0_ref = scr
        a, sig = _conv_stage(t, rows[0], halos[0], win)
        _conv_shifted(win, sh)
        wd[0:tm, :] = rows[1][...]
        wd[tm:tm + CONV_HALO, :] = jnp.where(t < nt - 1, halos[1][...], 0.0)
        _conv_shifted(wd, shd)
        _conv_tap_corr(sh, rows[1], [lead + j for j in range(CONV_K)], accs[0])
        _conv_tap_sum(shd, res[0], [CONV_K - 1 - j for j in range(CONV_K)], dy0_ref)
        dy0 = dy0_ref[...]
        outs[0][...] = jnp.concatenate([dy0 * sig, dy0 * a * sig * (1.0 - sig)], axis=1).astype(_MXU_DTYPE)

    return _row_call("conv_bwd_taps" + tag, body, nt, rows=[(proj_d, tm), (dyc, tm)],
                     halos=[(proj_d, CONV_HALO, tm // CONV_HALO, 'prev'), (dyc, CONV_HALO, tm // CONV_HALO, 'next')],
                     res=[dw_w], outs=[((t_len, 2 * BW), _MXU_DTYPE, tm)], accs=[(CONV_K, BW)],
                     scratch=_conv_scratch(tm) + _conv_scratch(tm) + [((tm, BW), F32)])


def _merge_fwd(tag, h, xn, ys, w_gate, b_gate, w_branch, w_out, tm=512):
    t_len = h.shape[0]

    def body(t, rows, halos, res, outs, accs, scr):
        xnv = rows[1][...]
        wg_ref, bg_ref, wb_ref, wo_ref = res
        merged = jnp.zeros((tm, D_MODEL), F32)
        for n in range(4):
            cs = slice(n * D_MODEL, (n + 1) * D_MODEL)
            z = lax.dot_general(xnv, wg_ref[n], _NN, preferred_element_type=F32) + bg_ref[n:n + 1, :]
            bo = lax.dot_general(rows[2 + n][...], wb_ref[n], _NN, preferred_element_type=F32)
            outs[0][:, cs] = z.astype(_MXU_DTYPE)
            outs[1][:, cs] = bo.astype(_MXU_DTYPE)
            merged = merged + _sigmoid(z) * bo
        mb = merged.astype(_MXU_DTYPE)
        outs[2][...] = mb
        outs[3][...] = rows[0][...] + lax.dot_general(mb, wo_ref[...], _NN, preferred_element_type=F32)

    return _row_call("merge_fwd" + tag, body, t_len // tm, rows=[(h, tm), (xn, tm)] + [(y, tm) for y in ys],
                     res=[w_gate, b_gate, w_branch, w_out],
                     outs=[((t_len, 4 * D_MODEL), _MXU_DTYPE, tm), ((t_len, 4 * D_MODEL), _MXU_DTYPE, tm),
                           ((t_len, D_MODEL), _MXU_DTYPE, tm), ((t_len, D_MODEL), F32, tm)])


def _merge_bwd(tag, dh, gate_pre, bo, w_gate, w_branch, w_out, tm=256):
    t_len = dh.shape[0]

    def body(t, rows, halos, res, outs, accs, scr):
        wg_ref, wb_ref, wo_ref = res
        dhb = rows[0][...].astype(_MXU_DTYPE)
        outs[0][...] = dhb
        dmerged = lax.dot_general(dhb, wo_ref[...], _NT, preferred_element_type=F32)
        dxn = jnp.zeros((tm, D_MODEL), F32)
        dbg = []
        for n in range(4):
            cs = slice(n * D_MODEL, (n + 1) * D_MODEL)
            g = _sigmoid(rows[1][:, cs].astype(F32))
            dbo = (dmerged * g).astype(_MXU_DTYPE)
            dgp = dmerged * rows[2][:, cs].astype(F32) * (g * (1.0 - g))
            dgb = dgp.astype(_MXU_DTYPE)
            outs[1][:, cs] = dbo
            outs[2][:, cs] = dgb
            outs[4 + n][...] = lax.dot_general(dbo, wb_ref[n], _NT, preferred_element_type=F32)
            dxn = dxn + lax.dot_general(dgb, wg_ref[n], _NT, preferred_element_type=F32)
            dbg.append(jnp.sum(dgp, axis=0, keepdims=True))
        outs[3][...] = dxn
        accs[0][...] += jnp.concatenate(dbg, axis=1)

    return _row_call("merge_bwd" + tag, body, t_len // tm, rows=[(dh, tm), (gate_pre, tm), (bo, tm)],
                     res=[w_gate, w_branch, w_out],
                     outs=[((t_len, D_MODEL), _MXU_DTYPE, tm), ((t_len, 4 * D_MODEL), _MXU_DTYPE, tm),
                           ((t_len, 4 * D_MODEL), _MXU_DTYPE, tm), ((t_len, D_MODEL), F32, tm)]
                     + [((t_len, BW), F32, tm)] * 4,
                     accs=[(1, 4 * D_MODEL)])


FF_COLS = 1024


def _ffn_fwd(tag, h, g2, w1, w2, tm=512):
    t_len = h.shape[0]

    def body(t, rows, halos, res, outs, accs, scr):
        hv = rows[0][...]
        hn = (hv * _rms_stat(hv) * res[0][...]).astype(_MXU_DTYPE)
        outs[0][...] = hn
        acc = hv
        for c in range(D_FF // FF_COLS):
            cs = slice(c * FF_COLS, (c + 1) * FF_COLS)
            pre = lax.dot_general(hn, res[1][:, cs], _NN, preferred_element_type=F32)
            outs[1][:, cs] = pre
            ff = jnp.square(jnp.maximum(pre, 0.0)).astype(_MXU_DTYPE)
            acc = acc + lax.dot_general(ff, res[2][cs, :], _NN, preferred_element_type=F32)
        outs[2][...] = acc

    return _row_call("ffn_fwd" + tag, body, t_len // tm, rows=[(h, tm)], res=[g2, w1, w2],
                     outs=[((t_len, D_MODEL), _MXU_DTYPE, tm), ((t_len, D_FF), F32, tm), ((t_len, D_MODEL), F32, tm)])


def _ffn_bwd(tag, dh, h, pre, g2, w1, w2, tm=256, exchange=()):
    t_len = dh.shape[0]

    def body(t, rows, halos, res, outs, accs, scr):
        dhv = rows[0][...]
        hv = rows[1][...]
        dhb = dhv.astype(_MXU_DTYPE)
        outs[0][...] = dhb
        dhn = jnp.zeros((tm, D_MODEL), F32)
        for c in range(D_FF // FF_COLS):
            cs = slice(c * FF_COLS, (c + 1) * FF_COLS)
            r = jnp.maximum(rows[2][:, cs], 0.0)
            outs[1][:, cs] = (r * r).astype(_MXU_DTYPE)
            dpre = (lax.dot_general(dhb, res[2][cs, :], _NT, preferred_element_type=F32) * (2.0 * r)).astype(_MXU_DTYPE)
            outs[2][:, cs] = dpre
            dhn = dhn + lax.dot_general(dpre, res[1][:, cs], _NT, preferred_element_type=F32)
        dres, dg = _rms_bwd(dhn, hv, res[0][...], _rms_stat(hv))
        outs[3][...] = dhv + dres
        accs[0][...] += dg

    return _row_call("ffn_bwd" + tag, body, t_len // tm, rows=[(dh, tm), (h, tm), (pre, tm)], res=[g2, w1, w2],
                     outs=[((t_len, D_MODEL), _MXU_DTYPE, tm), ((t_len, D_FF), _MXU_DTYPE, tm),
                           ((t_len, D_FF), _MXU_DTYPE, tm), ((t_len, D_MODEL), F32, tm)],
                     accs=[(1, D_MODEL)], exchange=exchange)


def _ple_fwd(tag, h, p, g3, w_pg, b_pg, w_ple, tm=512):
    t_len = h.shape[0]

    def body(t, rows, halos, res, outs, accs, scr):
        hv = rows[0][...]
        hg = (hv * _rms_stat(hv) * res[0][...]).astype(_MXU_DTYPE)
        pb = rows[1][...].astype(_MXU_DTYPE)
        pg = _sigmoid(lax.dot_general(hg, res[1][...], _NN, preferred_element_type=F32) + res[2][...])
        pe = lax.dot_general(pb, res[3][...], _NN, preferred_element_type=F32)
        outs[0][...] = hg
        outs[1][...] = pb
        outs[2][...] = pg
        outs[3][...] = hv + pg * pe

    return _row_call("ple_fwd" + tag, body, t_len // tm, rows=[(h, tm), (p, tm)], res=[g3, w_pg, b_pg, w_ple],
                     outs=[((t_len, D_MODEL), _MXU_DTYPE, tm), ((t_len, PLE_DIM), _MXU_DTYPE, tm),
                           ((t_len, D_MODEL), F32, tm), ((t_len, D_MODEL), F32, tm)])


def _ple_bwd(tag, dh, h, pg, p_b, g3, w_pg, w_ple, tm=512):
    t_len = dh.shape[0]

    def body(t, rows, halos, res, outs, accs, scr):
        dhv = rows[0][...]
        hv = rows[1][...]
        pgv = rows[2][...]
        pe = lax.dot_general(rows[3][...], res[2][...], _NN, preferred_element_type=F32)
        dgp = dhv * pe * (pgv * (1.0 - pgv))
        dgb = dgp.astype(_MXU_DTYPE)
        outs[0][...] = dgb
        outs[1][...] = (dhv * pgv).astype(_MXU_DTYPE)
        dhg = lax.dot_general(dgb, res[1][...], _NT, preferred_element_type=F32)
        dres, dg = _rms_bwd(dhg, hv, res[0][...], _rms_stat(hv))
        outs[2][...] = dhv + dres
        accs[0][...] += jnp.sum(dgp, axis=0, keepdims=True)
        accs[1][...] += dg

    return _row_call("ple_bwd" + tag, body, t_len // tm, rows=[(dh, tm), (h, tm), (pg, tm), (p_b, tm)],
                     res=[g3, w_pg, w_ple],
                     outs=[((t_len, D_MODEL), _MXU_DTYPE, tm), ((t_len, D_MODEL), _MXU_DTYPE, tm),
                           ((t_len, D_MODEL), F32, tm)],
                     accs=[(1, D_MODEL), (1, D_MODEL)])


def _inproj_bwd(tag, dh, h, dxn_gate, dprojs, g1, w_groups, tm=512, exchange=()):
    t_len = dh.shape[0]

    def body(t, rows, halos, res, outs, accs, scr):
        hv = rows[1][...]
        dxn = rows[2][...]
        for dp, w in zip(rows[3:], res[1:]):
            dxn = dxn + lax.dot_general(dp[...], w[...], _NN, preferred_element_type=F32)
        dres, dg = _rms_bwd(dxn, hv, res[0][...], _rms_stat(hv))
        outs[0][...] = rows[0][...] + dres
        accs[0][...] += dg

    return _row_call("inproj_bwd" + tag, body, t_len // tm,
                     rows=[(dh, tm), (h, tm), (dxn_gate, tm)] + [(d, tm) for d in dprojs],
                     res=[g1] + list(w_groups), outs=[((t_len, D_MODEL), F32, tm)], accs=[(1, D_MODEL)],
                     exchange=exchange)


def _loss_head(h, target, gf, tm=512):
    t_len = h.shape[0]

    def body(t, rows, halos, res, outs, accs, scr):
        hv = rows[0][...]
        g = res[0][...]
        r = _rms_stat(hv)
        diff = hv * r * g - rows[1][...]
        accs[0][...] += 0.5 * jnp.sum(jnp.mean(diff * diff, axis=-1, keepdims=True), axis=0, keepdims=True)
        dh, dg = _rms_bwd(diff * (1.0 / D_MODEL), hv, g, r)
        outs[0][...] = dh
        accs[1][...] += dg

    return _row_call("loss_head", body, t_len // tm, rows=[(h, tm), (target, tm)], res=[gf],
                     outs=[((t_len, D_MODEL), F32, tm)], accs=[(1, 128), (1, D_MODEL)])


def _row(v):
    return v.reshape(1, -1)


GATHER_DURING = (('inproj', ('w_gate', 'w_branch', 'w_out')), ('gla', ('w_ff1',)), ('attn', ('w_ff2', 'w_ple_gate', 'w_ple')))
SCATTER_DURING_ATTN = ('w_ple_gate', 'w_ple', 'w_ff1', 'w_ff2', 'w_out', 'w_gate', 'w_branch')


def _gather_items(shards, names):
    items = []
    for n in names:
        s = shards[n]
        ax = SHARD_AXIS[n] - 1
        if n == 'w_in':
            items.append(_gather_item(s))
        else:
            items.append(_gather_item(s, s.shape[:ax] + (N_DEV * s.shape[ax],) + s.shape[ax + 1:], ax))
    return items


def _land(w, names, arrays):
    for n, a in zip(names, arrays):
        w[n] = a.reshape(IN_COLS, D_MODEL) if n == 'w_in' else a


def _layer_fwd(i, h, p_i, w, shards, next_shards):
    tag = "_l%d" % i
    during = dict(GATHER_DURING)
    win = [w['w_in'][s:s + n] for _, s, n in IN_GROUPS]
    res = _inproj_fwd(tag, h, _row(w['norm1_g']), win, exchange=_gather_items(shards, during['inproj']))
    xn, pa, pb, pr, pc, pd = res[:6]
    _land(w, during['inproj'], res[6:])
    sg_bt = w['sg_b'].T
    y_a = _sg_fwd(tag, pa, _row(w['sg_ln_g']), _row(w['sg_ln_b']), w['sg_w'], sg_bt)
    res = _gla_fwd(tag, pb, pr, w['gla_w_a2'], _row(w['gla_b_a']), _row(w['gla_norm_g']),
                   exchange=_gather_items(shards, during['gla']))
    y_b, states = res[:2]
    _land(w, during['gla'], res[2:])
    bias = _bias_tile(tag, _bias_expand(tag, w['att_rel_bias']).reshape(ATT_HEADS, CHUNK, BAND))
    items = _gather_items(shards, during['attn']) + (_gather_items(next_shards, ['w_in']) if next_shards else [])
    res = _attn_fwd(tag, pc, bias, exchange=items)
    y_c, probs = res[:2]
    _land(w, during['attn'], res[2:2 + len(during['attn'])])
    next_w_in = res[-1].reshape(IN_COLS, D_MODEL) if next_shards else None
    y_d, yc = _conv_fwd(tag, pd, w['conv_dw_w'], _row(w['conv_dw_b']), _row(w['conv_ln_g']), _row(w['conv_ln_b']))
    ys = (y_a, y_b, y_c, y_d)
    gate, bo, merged, h1 = _merge_fwd(tag, h, xn, ys, w['w_gate'], w['b_gate'], w['w_branch'], w['w_out'])
    hn, pre, h2 = _ffn_fwd(tag, h1, _row(w['norm2_g']), w['w_ff1'], w['w_ff2'])
    hg, p_b, pg, h3 = _ple_fwd(tag, h2, p_i, _row(w['norm3_g']), w['w_ple_gate'], _row(w['b_ple_gate']), w['w_ple'])
    saved = dict(h=h, xn=xn, pa=pa, pb=pb, pr=pr, pc=pc, pd=pd, states=states, probs=probs, yc=yc, ys=ys, gate=gate,
                 bo=bo, merged=merged, h1=h1, hn=hn, pre=pre, h2=h2, hg=hg, p_b=p_b, pg=pg, sg_bt=sg_bt)
    return h3, saved, next_w_in


def _layer_bwd(i, dh3, s, w, tail=None):
    tag = "_l%d" % i
    g = {}
    dgp, dpe, dh2, db_pg, dg3 = _ple_bwd(tag, dh3, s['h2'], s['pg'], s['p_b'], _row(w['norm3_g']), w['w_ple_gate'],
                                         w['w_ple'])
    g['b_ple_gate'], g['norm3_g'] = db_pg[0], dg3[0]
    g['w_ple_gate'] = _tn_call("dw_ple_gate" + tag, s['hg'], dgp, D_MODEL, D_MODEL)
    g['w_ple'] = _tn_call("dw_ple" + tag, s['p_b'], dpe, PLE_DIM, D_MODEL)

    dh2b, ffb, dpre, dh1, dg2 = _ffn_bwd(tag, dh2, s['h1'], s['pre'], _row(w['norm2_g']), w['w_ff1'], w['w_ff2'])
    g['norm2_g'] = dg2[0]
    g['w_ff1'] = _tn_call("dw_ff1" + tag, s['hn'], dpre, D_MODEL, FF_COLS, nblk=D_FF // FF_COLS)
    g['w_ff2'] = _tn_call("dw_ff2" + tag, ffb, dh2b, FF_COLS, D_MODEL, nblk=D_FF // FF_COLS, a_col=True, b_col=False,
                          out='rows')

    dh1b, dbo, dgpre, dxn_gate, dy_a, dy_b, dy_c, dy_d, db_gate = _merge_bwd(
        tag, dh1, s['gate'], s['bo'], w['w_gate'], w['w_branch'], w['w_out'])
    g['b_gate'] = db_gate.reshape(4, D_MODEL)
    g['w_out'] = _tn_call("dw_out" + tag, s['merged'], dh1b, D_MODEL, D_MODEL)
    g['w_gate'] = _tn_call("dw_gate" + tag, s['xn'], dgpre, D_MODEL, D_MODEL, nblk=4, out='stack')
    g['w_branch'] = jnp.stack([_tn_call("dw_branch%d%s" % (n, tag), s['ys'][n], dbo, BW, D_MODEL, b_off=n)
                             for n in range(4)])

    lg, lb = _row(w['sg_ln_g']), _row(w['sg_ln_b'])
    dpa, dsg_w, dsg_bt, dlg, dlb = _sg_bwd(tag, s['pa'], dy_a, lg, lb, w['sg_w'], s['sg_bt'])
    g['sg_w'], g['sg_b'], g['sg_ln_g'], g['sg_ln_b'] = dsg_w, dsg_bt.T, dlg[0], dlb[0]

    dpb, dpr, dwa2, dba, dng = _gla_bwd(tag, s['pb'], s['pr'], dy_b, s['states'], w['gla_w_a2'],
                                        _row(w['gla_b_a']), _row(w['gla_norm_g']))
    g['gla_w_a2'], g['gla_b_a'], g['gla_norm_g'] = dwa2, dba[0], dng[0]

    items = [_scatter_item(g.pop(n), axis=SHARD_AXIS[n] - 1) for n in SCATTER_DURING_ATTN]
    res = _attn_bwd(tag, s['pc'], dy_c, s['probs'], exchange=items)
    dq, dkv, dbias = res[:3]
    parts = dict(zip(SCATTER_DURING_ATTN, res[3:]))
    g['att_rel_bias'] = _bias_reduce(tag, _bias_untile(tag, dbias).reshape(ATT_HEADS, CHUNK * BAND))

    cg, cb = _row(w['conv_ln_g']), _row(w['conv_ln_b'])
    dyc, dcg, dcb, ddwb = _conv_bwd_norm(tag, s['yc'], dy_d, cg, cb)
    dpd, ddw = _conv_bwd_taps(tag, s['pd'], dyc, w['conv_dw_w'])
    g['conv_ln_g'], g['conv_ln_b'], g['conv_dw_b'], g['conv_dw_w'] = dcg[0], dcb[0], ddwb[0], ddw

    dprojs = (dpa, dpb, dpr, dq, dkv, dpd)
    dw_in = jnp.concatenate([_tn_call("dw_in%s%s" % (name, tag), dp, s['xn'], n, D_MODEL)
                             for (name, _, n), dp in zip(IN_GROUPS_BWD, dprojs)], axis=0)
    items = [_scatter_item(dw_in.reshape(N_DEV, IN_COLS // N_DEV, D_MODEL))] + (tail(g) if tail else [])
    win = [w['w_in'][s0:s0 + n] for _, s0, n in IN_GROUPS_BWD]
    res = _inproj_bwd(tag, dh1, s['h'], dxn_gate, dprojs, _row(w['norm1_g']), win, exchange=items)
    dh0, dg1 = res[:2]
    g['norm1_g'] = dg1[0]
    parts['w_in'] = res[2]
    return dh0, g, parts, res[3:]


def _local_step(x, p, target, final_g, layers, shards, tail):
    h = x
    saved = []
    for i in range(DEPTH):
        nxt = shards[i + 1] if i + 1 < DEPTH else None
        h, s, next_w_in = _layer_fwd(i, h, p[i], layers[i], shards[i], nxt)
        saved.append(s)
        if nxt:
            layers[i + 1]['w_in'] = next_w_in
    dh, loss, dgf = _loss_head(h, target, _row(final_g))
    small, parts, tail_out = [None] * DEPTH, [None] * DEPTH, None
    for i in reversed(range(DEPTH)):
        hook = (lambda g: tail([g] + small[1:])) if i == 0 else None
        dh, small[i], parts[i], out = _layer_bwd(i, dh, saved[i], layers[i], hook)
        if i == 0:
            tail_out = out
    return loss[0, 0], dh, dgf[0], small, parts, tail_out


def _peers():
    x, y, c = lax.axis_index("x"), lax.axis_index("y"), lax.axis_index("c")
    me = 4 * x + 2 * y + c
    out = []
    for k in range(1, N_DEV):
        px = (1 - x) if k & 4 else x
        py = (1 - y) if k & 2 else y
        pc = (1 - c) if k & 1 else c
        out.append((k - 1, (px, py, pc), 4 * px + 2 * py + pc))
    return me, out


def _block(ref, axis, idx, width):
    ix = [slice(None)] * len(ref.shape)
    ix[axis] = pl.ds(pl.multiple_of(idx * width, width), width)
    return ref.at[tuple(ix)]


def _slot(ref, idx):
    return ref.at[idx]


def _whole(ref, idx):
    return ref


def _gather_item(src, out_shape=None, axis=None):
    if axis is None:
        return dict(src=src, out=(N_DEV,) + src.shape, take=_whole, put=_slot)
    return dict(src=src, out=tuple(out_shape), take=_whole,
                put=lambda ref, s: _block(ref, axis, s, src.shape[axis]))


def _scatter_item(src, axis=None, lead=0):
    if axis is None:
        shape = src.shape[:lead] + src.shape[lead + 1:]
        take = lambda ref, s: ref.at[(slice(None),) * lead + (s,)]
    else:
        width = src.shape[axis] // N_DEV
        shape = src.shape[:axis] + (width,) + src.shape[axis + 1:]
        take = lambda ref, s: _block(ref, axis, s, width)
    return dict(src=src, out=(N_DEV,) + shape, take=take, put=_slot)


def _exchange_sems(n):
    return [pltpu.SemaphoreType.DMA((n * (N_DEV - 1),)), pltpu.SemaphoreType.DMA((n * (N_DEV - 1),)),
            pltpu.SemaphoreType.DMA((n,))]


def _exchange_copies(items, src_refs, out_refs, sems, start):
    send_sems, recv_sems, local_sems = sems
    me, peers = _peers()

    def remote(i, k, pos, receiver, sender):
        it = items[i]
        return pltpu.make_async_remote_copy(
            src_ref=it['take'](src_refs[i], receiver), dst_ref=it['put'](out_refs[i], sender),
            send_sem=send_sems.at[i * (N_DEV - 1) + k], recv_sem=recv_sems.at[i * (N_DEV - 1) + k],
            device_id=pos, device_id_type=pl.DeviceIdType.MESH)

    local = [pltpu.make_async_copy(it['take'](src_refs[i], me), it['put'](out_refs[i], me), local_sems.at[i])
             for i, it in enumerate(items)]
    if start:
        for cp in local:
            cp.start()
        for k, pos, flat in peers:
            for i in range(len(items)):
                remote(i, k, pos, flat, me).start()
    else:
        for k, pos, flat in peers:
            for i in range(len(items)):
                remote(i, k, pos, flat, flat).wait_recv()
        for k, pos, flat in peers:
            for i in range(len(items)):
                remote(i, k, pos, flat, me).wait_send()
        for cp in local:
            cp.wait()


def _exchange(name, items):
    n = len(items)

    def body(*refs):
        _exchange_copies(items, refs[:n], refs[n:2 * n], refs[2 * n:], start=True)
        _exchange_copies(items, refs[:n], refs[n:2 * n], refs[2 * n:], start=False)

    any_spec = pl.BlockSpec(memory_space=pl.ANY)
    return pl.pallas_call(
        body, out_shape=[jax.ShapeDtypeStruct(it['out'], it['src'].dtype) for it in items],
        in_specs=[any_spec] * n, out_specs=[any_spec] * n, scratch_shapes=_exchange_sems(n),
        name=name)(*[it['src'] for it in items])


def _pack(arrays, dtype, lead=None):
    flat = [a.astype(dtype).reshape((lead, -1) if lead else (-1,)) for a in arrays]
    cat = jnp.concatenate(flat, axis=-1)
    n = cat.shape[-1]
    rows = -(-n // (PACK_COLS * SUBLANES)) * SUBLANES
    pad = rows * PACK_COLS - n
    if pad:
        cat = jnp.pad(cat, ((0, 0), (0, pad)) if lead else ((0, pad),))
    return cat.reshape((lead, rows, PACK_COLS) if lead else (rows, PACK_COLS))


def _unpack(buf, shapes, lead=None):
    flat = buf.reshape((lead, -1) if lead else (-1,))
    out, off = [], 0
    for shp in shapes:
        n = int(np.prod(shp))
        piece = flat[..., off:off + n]
        out.append(piece.reshape(((lead,) if lead else ()) + tuple(shp)))
        off += n
    return out


def _to_slabs(full, axis):
    shp = full.shape
    split = full.reshape(shp[:axis] + (N_DEV, shp[axis] // N_DEV) + shp[axis + 1:])
    return jnp.moveaxis(split, axis, 0)


def _from_slabs(slabs, axis):
    moved = jnp.moveaxis(slabs, 0, axis)
    shp = moved.shape
    return moved.reshape(shp[:axis] + (shp[axis] * shp[axis + 1],) + shp[axis + 2:])


def _adamw_block(r, c):
    if r % 8:
        return r, 256
    br = min(r, max(8, ADAMW_TILE * PACK_COLS // c))
    while r % br:
        br //= 2
    return br, c


def _adamw(name, partials, w, m, v):
    n_lead, r, c = w.shape
    br, bc = _adamw_block(r, c)
    ni, nj = r // br, c // bc
    c1 = 1.0 - ADAM_B1 ** ADAM_STEP
    c2 = 1.0 - ADAM_B2 ** ADAM_STEP

    def kern(*refs):
        p_refs = refs[:n_lead]
        w_ref, m_ref, v_ref, g_ref, d_ref, nm_ref, nv_ref = refs[n_lead:]
        layer = pl.program_id(0)
        g = None
        for l, p_ref in enumerate(p_refs):
            gl = p_ref[0].astype(F32)
            for s in range(1, N_DEV):
                gl = gl + p_ref[s].astype(F32)
            g = gl if g is None else jnp.where(layer == l, gl, g)
        nm = ADAM_B1 * m_ref[...] + (1.0 - ADAM_B1) * g
        nv = ADAM_B2 * v_ref[...] + (1.0 - ADAM_B2) * jnp.square(g)
        g_ref[...] = g
        nm_ref[...] = nm
        nv_ref[...] = nv
        d_ref[...] = -ADAM_LR * ((nm / c1) / (jnp.sqrt(nv / c2) + ADAM_EPS) + ADAM_WD * w_ref[...])

    def part_spec(mine):
        def index(l, i, j):
            before, after = l < mine, l > mine
            return (0, jnp.where(before, 0, jnp.where(after, ni - 1, i)), jnp.where(before, 0, jnp.where(after, nj - 1, j)))
        return pl.BlockSpec((N_DEV, br, bc), index)

    blk = pl.BlockSpec((None, br, bc), lambda l, i, j: (l, i, j))
    return pl.pallas_call(
        kern, grid=(n_lead, ni, nj),
        in_specs=[part_spec(l) for l in range(n_lead)] + [blk, blk, blk],
        out_specs=[blk] * 4, out_shape=[jax.ShapeDtypeStruct(w.shape, F32)] * 4,
        compiler_params=pltpu.CompilerParams(dimension_semantics=("arbitrary",) * 3),
        name=name)(*partials, w, m, v)


def _as_rows(a, lead):
    return a.reshape(a.shape[:lead] + (-1, a.shape[-1]))


def kernel(x, p, norm1_g, w_in, sg_ln_g, sg_ln_b, sg_w, sg_b, gla_w_a2, gla_b_a, gla_norm_g, att_rel_bias, conv_dw_w, conv_dw_b, conv_ln_g, conv_ln_b, w_branch, w_gate, b_gate, w_out, norm2_g, w_ff1, w_ff2, norm3_g, w_ple_gate, b_ple_gate, w_ple, final_g, loss_target, m_norm1_g, m_w_in, m_sg_ln_g, m_sg_ln_b, m_sg_w, m_sg_b, m_gla_w_a2, m_gla_b_a, m_gla_norm_g, m_att_rel_bias, m_conv_dw_w, m_conv_dw_b, m_conv_ln_g, m_conv_ln_b, m_w_branch, m_w_gate, m_b_gate, m_w_out, m_norm2_g, m_w_ff1, m_w_ff2, m_norm3_g, m_w_ple_gate, m_b_ple_gate, m_w_ple, m_final_g, v_norm1_g, v_w_in, v_sg_ln_g, v_sg_ln_b, v_sg_w, v_sg_b, v_gla_w_a2, v_gla_b_a, v_gla_norm_g, v_att_rel_bias, v_conv_dw_w, v_conv_dw_b, v_conv_ln_g, v_conv_ln_b, v_w_branch, v_w_gate, v_b_gate, v_w_out, v_norm2_g, v_w_ff1, v_w_ff2, v_norm3_g, v_w_ple_gate, v_b_ple_gate, v_w_ple, v_final_g):
    args = locals()
    wts = {n: args[n] for n in WEIGHTS}
    mom = {n: args['m_' + n] for n in WEIGHTS}
    var = {n: args['v_' + n] for n in WEIGHTS}

    local = {d_name: dict(d, w_in=jnp.swapaxes(d['w_in'], 1, 2))
             for d_name, d in (("w", wts), ("m", mom), ("v", var))}
    shards = [{n: local["w"][n][i].astype(_MXU_DTYPE) for n in MXU_WEIGHTS} for i in range(DEPTH)]

    first_w_in, vec = _exchange("gather_first_weights", _gather_items(shards[0], ['w_in'])
                                + [_gather_item(_pack([wts[n] for n in VEC_WEIGHTS], F32))])
    vec_full = {n: _from_slabs(slabs, SHARD_AXIS[n])
                for n, slabs in zip(VEC_WEIGHTS, _unpack(vec, [wts[n].shape for n in VEC_WEIGHTS], lead=N_DEV))}
    small_names = [n for n in WEIGHTS if n not in MXU_WEIGHTS and n != 'final_g']
    layers = [{n: (vec_full[n] if n in vec_full else wts[n])[i] for n in small_names} for i in range(DEPTH)]
    _land(layers[0], ['w_in'], [first_w_in])

    def vec_items(small):
        stacked = [jnp.stack([g[n] for g in small]) for n in VEC_WEIGHTS]
        return [_scatter_item(_pack([_to_slabs(a, SHARD_AXIS[n]) for n, a in zip(VEC_WEIGHTS, stacked)], F32,
                                    lead=N_DEV))]

    loss, grad_x, dgf, small, parts, (vec_parts,) = _local_step(x[0], p[:, 0], loss_target[0], final_g, layers,
                                                               shards, vec_items)
    loss = lax.psum(loss, ("x", "y", "c"))

    grads = {n: jnp.stack([small[i][n] for i in range(DEPTH)]) for n in REPLICATED if n != 'final_g'}
    grads['final_g'] = dgf
    repl_parts, = _exchange("gather_replicated_grads", [_gather_item(_pack([grads[n] for n in REPLICATED], F32))])

    results = {}
    for n in MXU_WEIGHTS:
        w3, m3, v3 = (_as_rows(local[d][n], 1) for d in ("w", "m", "v"))
        outs = _adamw("adamw_" + n, [parts[i][n].reshape((N_DEV,) + w3.shape[1:]) for i in range(DEPTH)], w3, m3, v3)
        for kind, a in zip(("grad", "delta", "new_m", "new_v"), outs):
            a = a.reshape(local["w"][n].shape)
            results[kind, n] = jnp.swapaxes(a, 1, 2) if n == 'w_in' else a
    for names, part, call in ((VEC_WEIGHTS, vec_parts, "adamw_vec"), (REPLICATED, repl_parts, "adamw_replicated")):
        packed = [_pack([d[n] for n in names], F32)[None] for d in (wts, mom, var)]
        outs = _adamw(call, [part], *packed)
        for kind, buf in zip(("grad", "delta", "new_m", "new_v"), outs):
            for n, a in zip(names, _unpack(buf[0], [wts[n].shape for n in names])):
                results[kind, n] = a
    return (loss, grad_x[None]) + tuple(results[kind, n] for kind in ("grad", "delta", "new_m", "new_v")
                                        for n in WEIGHTS)
```

```python
import functools

import numpy as np
import jax
import jax.numpy as jnp
from jax import lax
from jax.experimental import pallas as pl
from jax.experimental.pallas import tpu as pltpu

F32 = jnp.float32
_MXU_DTYPE = jnp.bfloat16
GRAD_DTYPE = jnp.bfloat16

N_DEV = 8
D_MODEL = 1024
DEPTH = 2
CHUNK = 64
PLE_DIM = 256
BW = 512
SG_BLOCK = 128
SG_GROUPS = 4
GLA_HEADS = 4
GLA_DK = 64
GLA_DV = 128
GLA_RANK = 16
GLA_TAU = 16.0
ATT_HEADS = 8
ATT_HD = 64
ATT_BAND = 9
BAND = ATT_BAND * CHUNK
MAX_REL = 256
REL_TABLE = CHUNK + MAX_REL
CONV_K = 31
CONV_HALO = 32
D_FF = 4096
EPS = 1e-6
NEG_INF = -1e30

IN_GROUPS = (("A", 0, 1024), ("B", 1024, 1536), ("a", 2560, 16), ("C", 2576, 1536), ("D", 4112, 1024))
IN_GROUPS_BWD = IN_GROUPS[:3] + (("Cq", 2576, 512), ("Ckv", 3088, 1024)) + IN_GROUPS[4:]
IN_COLS = 5136

ADAM_LR = 0.001
ADAM_B1 = 0.9
ADAM_B2 = 0.999
ADAM_EPS = 1e-08
ADAM_WD = 0.01
ADAM_STEP = 10

ADAMW_TILE = 128
PACK_COLS = 1024
VMEM_LIMIT_MB = 56

_NN = (((1,), (0,)), ((), ()))
_NT = (((1,), (1,)), ((), ()))
_TN = (((0,), (0,)), ((), ()))

WEIGHTS = ['norm1_g', 'w_in', 'sg_ln_g', 'sg_ln_b', 'sg_w', 'sg_b', 'gla_w_a2', 'gla_b_a', 'gla_norm_g',
           'att_rel_bias', 'conv_dw_w', 'conv_dw_b', 'conv_ln_g', 'conv_ln_b', 'w_branch', 'w_gate', 'b_gate',
           'w_out', 'norm2_g', 'w_ff1', 'w_ff2', 'norm3_g', 'w_ple_gate', 'b_ple_gate', 'w_ple', 'final_g']
SHARD_AXIS = {'w_in': 2, 'gla_w_a2': 2, 'att_rel_bias': 2, 'conv_dw_w': 2, 'w_branch': 3, 'w_gate': 2,
              'b_gate': 2, 'w_out': 1, 'w_ff1': 2, 'w_ff2': 1, 'w_ple_gate': 1, 'w_ple': 2}
MXU_WEIGHTS = ('w_in', 'w_branch', 'w_gate', 'w_out', 'w_ff1', 'w_ff2', 'w_ple_gate', 'w_ple')
VEC_WEIGHTS = ('gla_w_a2', 'att_rel_bias', 'conv_dw_w', 'b_gate')
SHARDED = tuple(n for n in WEIGHTS if n in SHARD_AXIS)
REPLICATED = tuple(n for n in WEIGHTS if n not in SHARD_AXIS)


def _mm(a, b, dims=_NN):
    return lax.dot_general(a.astype(_MXU_DTYPE), b.astype(_MXU_DTYPE), dims, preferred_element_type=F32)


def _split3(x):
    x1 = x.astype(jnp.bfloat16)
    r1 = x - x1.astype(F32)
    x2 = r1.astype(jnp.bfloat16)
    x3 = (r1 - x2.astype(F32)).astype(jnp.bfloat16)
    return x1, x2, x3


def _mm_exact_rhs(m, x, dims=_NN):
    return sum(lax.dot_general(m, xi, dims, preferred_element_type=F32) for xi in _split3(x))


def _mm_exact_lhs(x, m, dims=_NN):
    return sum(lax.dot_general(xi, m, dims, preferred_element_type=F32) for xi in _split3(x))


def _sigmoid(x):
    return 1.0 / (1.0 + jnp.exp(-x))


def _gelu(x):
    c = 0.7978845608028654
    t = jnp.tanh(c * (x + 0.044715 * x * x * x))
    return 0.5 * x * (1.0 + t), t


def _gelu_grad(x, t):
    c = 0.7978845608028654
    return 0.5 * (1.0 + t) + 0.5 * x * (1.0 - t * t) * c * (1.0 + 3.0 * 0.044715 * x * x)


def _rms_stat(h):
    return lax.rsqrt(jnp.mean(h * h, axis=-1, keepdims=True) + EPS)


def _rms_bwd(dy, h, g, r):
    hh = h * r
    dhh = dy * g
    dh = r * (dhh - hh * jnp.mean(dhh * hh, axis=-1, keepdims=True))
    return dh, jnp.sum(dy * hh, axis=0, keepdims=True)


def _ln_fwd(x, g, b):
    mu = jnp.mean(x, axis=-1, keepdims=True)
    xc = x - mu
    rs = lax.rsqrt(jnp.mean(xc * xc, axis=-1, keepdims=True) + EPS)
    xh = xc * rs
    return xh * g + b, xh, rs


def _ln_bwd(dy, xh, rs, g):
    dxh = dy * g
    dx = rs * (dxh - jnp.mean(dxh, axis=-1, keepdims=True) - xh * jnp.mean(dxh * xh, axis=-1, keepdims=True))
    return dx, jnp.sum(dy * xh, axis=0, keepdims=True), jnp.sum(dy, axis=0, keepdims=True)


def _row_call(name, body, nt, rows=(), halos=(), res=(), outs=(), accs=(), scratch=(), reverse=False, exchange=()):
    def pos(i):
        return (nt - 1 - i) if reverse else i

    def lead(ndim, f):
        return lambda i: (f(pos(i)),) + (0,) * (ndim - 1)

    in_specs, operands = [], []
    for a, tile in rows:
        in_specs.append(pl.BlockSpec((tile,) + a.shape[1:], lead(a.ndim, lambda t: t)))
        operands.append(a)
    for a, blk, per, side in halos:
        last = a.shape[0] // blk - 1
        delta = {'prev2': -2, 'prev': -1, 'next': per}[side]
        f = lambda t, per=per, last=last, delta=delta: jnp.clip(t * per + delta, 0, last)
        in_specs.append(pl.BlockSpec((blk,) + a.shape[1:], lead(a.ndim, f)))
        operands.append(a)
    for a in res:
        in_specs.append(pl.BlockSpec(a.shape, lambda i, nd=a.ndim: (0,) * nd, pipeline_mode=pl.Buffered(1)))
        operands.append(a)
    out_specs, out_shape = [], []
    for shape, dtype, tile in outs:
        out_specs.append(pl.BlockSpec((tile,) + tuple(shape[1:]), lead(len(shape), lambda t: t)))
        out_shape.append(jax.ShapeDtypeStruct(tuple(shape), dtype))
    for shape in accs:
        out_specs.append(pl.BlockSpec(tuple(shape), lambda i, nd=len(shape): (0,) * nd))
        out_shape.append(jax.ShapeDtypeStruct(tuple(shape), F32))
    nx = len(exchange)
    any_spec = pl.BlockSpec(memory_space=pl.ANY)
    for it in exchange:
        in_specs.append(any_spec)
        operands.append(it['src'])
        out_specs.append(any_spec)
        out_shape.append(jax.ShapeDtypeStruct(it['out'], it['src'].dtype))
    sizes = (len(rows), len(halos), len(res), nx, len(outs), len(accs), nx, len(scratch), 3 if nx else 0)

    def kern(*refs):
        i = pl.program_id(0)
        groups, at = [], 0
        for n in sizes:
            groups.append(refs[at:at + n])
            at += n
        row_refs, halo_refs, res_refs, x_src, out_refs, acc_refs, x_dst, scr_refs, sems = groups

        @pl.when(i == 0)
        def _():
            for r in tuple(acc_refs) + tuple(scr_refs):
                r[...] = jnp.zeros(r.shape, r.dtype)
            if nx:
                _exchange_copies(exchange, x_src, x_dst, sems, start=True)

        body(pos(i), row_refs, halo_refs, res_refs, out_refs, acc_refs, scr_refs)

        if nx:
            @pl.when(i == nt - 1)
            def _():
                _exchange_copies(exchange, x_src, x_dst, sems, start=False)

    result = pl.pallas_call(
        kern, grid=(nt,), in_specs=in_specs, out_specs=out_specs, out_shape=out_shape,
        scratch_shapes=[pltpu.VMEM(tuple(s), d) for s, d in scratch] + (_exchange_sems(nx) if nx else []),
        compiler_params=pltpu.CompilerParams(dimension_semantics=("arbitrary",),
                                             vmem_limit_bytes=VMEM_LIMIT_MB << 20),
        name=name)(*operands)
    return tuple(result)


def _tn_call(name, a, b, k, n, nblk=1, a_col=False, b_col=True, b_off=0, out='cols', tile=2048):
    tile = min(tile, a.shape[0])
    nt = a.shape[0] // tile
    if out == 'cols':
        o_shape, o_spec = (k, nblk * n), pl.BlockSpec((k, n), lambda j, t: (0, j))
    elif out == 'rows':
        o_shape, o_spec = (nblk * k, n), pl.BlockSpec((k, n), lambda j, t: (j, 0))
    else:
        o_shape, o_spec = (nblk, k, n), pl.BlockSpec((None, k, n), lambda j, t: (j, 0, 0))

    def kern(a_ref, b_ref, o_ref, acc):
        @pl.when(pl.program_id(1) == 0)
        def _():
            acc[...] = jnp.zeros(acc.shape, acc.dtype)

        acc[...] += lax.dot_general(a_ref[...], b_ref[...], _TN, preferred_element_type=F32)

        @pl.when(pl.program_id(1) == nt - 1)
        def _():
            o_ref[...] = acc[...].astype(o_ref.dtype)

    return pl.pallas_call(
        kern, grid=(nblk, nt),
        in_specs=[pl.BlockSpec((tile, k), (lambda j, t: (t, j)) if a_col else (lambda j, t: (t, 0))),
                  pl.BlockSpec((tile, n), (lambda j, t: (t, j + b_off)) if b_col else (lambda j, t: (t, b_off)))],
        out_specs=o_spec, out_shape=jax.ShapeDtypeStruct(o_shape, GRAD_DTYPE),
        scratch_shapes=[pltpu.VMEM((k, n), F32)],
        compiler_params=pltpu.CompilerParams(dimension_semantics=("arbitrary", "arbitrary"),
                                             vmem_limit_bytes=VMEM_LIMIT_MB << 20),
        name=name)(a, b)


def _inproj_fwd(tag, h, g1, w_in_t, tm=512, exchange=()):
    t_len = h.shape[0]

    def body(t, rows, halos, res, outs, accs, scr):
        hv = rows[0][...]
        xn = (hv * _rms_stat(hv) * res[0][...]).astype(_MXU_DTYPE)
        outs[0][...] = xn
        for o, (_, s0, n) in zip(outs[1:], IN_GROUPS):
            o[...] = lax.dot_general(xn, res[1][s0:s0 + n, :], _NT, preferred_element_type=F32)

    outs = [((t_len, D_MODEL), _MXU_DTYPE, tm)] + [((t_len, n), F32, tm) for _, _, n in IN_GROUPS]
    return _row_call("inproj_fwd" + tag, body, t_len // tm, rows=[(h, tm)], res=[g1, w_in_t], outs=outs,
                     exchange=exchange)


def _sg_mask():
    row = lax.broadcasted_iota(jnp.int32, (SG_BLOCK, SG_BLOCK), 0)
    col = lax.broadcasted_iota(jnp.int32, (SG_BLOCK, SG_BLOCK), 1)
    return jnp.logical_or(row >= CHUNK, col < CHUNK)


def _sg_forward_parts(pa, lg, lb, w_ref, bt):
    tm = pa.shape[0]
    nb = tm // SG_BLOCK
    su, sv = pa[:, :BW], pa[:, BW:]
    u, tu = _gelu(su)
    gv, tv = _gelu(sv)
    vn, xh, rs = _ln_fwd(gv, lg, lb)
    mask = _sg_mask()
    wms, xs, ms = [], [], []
    for g in range(SG_GROUPS):
        wm = jnp.where(mask, w_ref[g], 0.0).astype(_MXU_DTYPE)
        xg = jnp.concatenate([vn[b * SG_BLOCK:(b + 1) * SG_BLOCK, g * 128:(g + 1) * 128] for b in range(nb)], axis=1)
        xg = xg.astype(_MXU_DTYPE)
        ms.append(lax.dot_general(wm, xg, _NN, preferred_element_type=F32) + bt[:, g:g + 1])
        wms.append(wm)
        xs.append(xg)
    mixed = _sg_unfold(ms, nb)
    return su, sv, u, tu, tv, xh, rs, wms, xs, mixed


def _sg_unfold(per_group, nb):
    return jnp.concatenate(
        [jnp.concatenate([per_group[g][:, b * 128:(b + 1) * 128] for g in range(SG_GROUPS)], axis=1)
         for b in range(nb)], axis=0)


def _sg_fwd(tag, proj_a, lg, lb, sg_w, sg_bt, tm=512):
    t_len = proj_a.shape[0]

    def body(t, rows, halos, res, outs, accs, scr):
        parts = _sg_forward_parts(rows[0][...], res[0][...], res[1][...], res[2], res[3][...])
        outs[0][...] = (parts[2] * parts[-1]).astype(_MXU_DTYPE)

    return _row_call("sg_fwd" + tag, body, t_len // tm, rows=[(proj_a, tm)], res=[lg, lb, sg_w, sg_bt],
                     outs=[((t_len, BW), _MXU_DTYPE, tm)])[0]


def _sg_bwd(tag, proj_a, dy, lg, lb, sg_w, sg_bt, tm=512):
    t_len = proj_a.shape[0]
    nb = tm // SG_BLOCK

    def body(t, rows, halos, res, outs, accs, scr):
        lgv = res[0][...]
        su, sv, u, tu, tv, xh, rs, wms, xs, mixed = _sg_forward_parts(rows[0][...], lgv, res[1][...], res[2], res[3][...])
        dyv = rows[1][...]
        dsu = dyv * mixed * _gelu_grad(su, tu)
        dmixed = dyv * u
        mask = _sg_mask()
        dxs, dbs = [], []
        for g in range(SG_GROUPS):
            dm = jnp.concatenate([dmixed[b * SG_BLOCK:(b + 1) * SG_BLOCK, g * 128:(g + 1) * 128] for b in range(nb)],
                                 axis=1)
            dmb = dm.astype(_MXU_DTYPE)
            dw = lax.dot_general(dmb, xs[g], _NT, preferred_element_type=F32)
            accs[0][g] += jnp.where(mask, dw, 0.0)
            dbs.append(jnp.sum(dm, axis=1, keepdims=True))
            dxs.append(lax.dot_general(wms[g], dmb, _TN, preferred_element_type=F32))
        accs[1][...] += jnp.concatenate(dbs, axis=1)
        dvn = _sg_unfold(dxs, nb)
        dgv, dlg, dlb = _ln_bwd(dvn, xh, rs, lgv)
        accs[2][...] += dlg
        accs[3][...] += dlb
        dsv = dgv * _gelu_grad(sv, tv)
        outs[0][...] = jnp.concatenate([dsu, dsv], axis=1).astype(_MXU_DTYPE)

    return _row_call("sg_bwd" + tag, body, t_len // tm, rows=[(proj_a, tm), (dy, tm)], res=[lg, lb, sg_w, sg_bt],
                     outs=[((t_len, 2 * BW), _MXU_DTYPE, tm)],
                     accs=[(SG_GROUPS, SG_BLOCK, SG_BLOCK), (SG_BLOCK, SG_GROUPS), (1, BW), (1, BW)])


def _chunk_matrix(tm, kind):
    row = lax.broadcasted_iota(jnp.int32, (tm, tm), 0)
    col = lax.broadcasted_iota(jnp.int32, (tm, tm), 1)
    same = lax.shift_right_logical(row, 6) == lax.shift_right_logical(col, 6)
    if kind == 'cumsum':
        same = jnp.logical_and(same, row >= col)
    elif kind == 'revsum':
        same = jnp.logical_and(same, row <= col)
    return same.astype(jnp.bfloat16)


def _gla_gate(pa, wa2, ba):
    z = _mm(pa, wa2) + ba
    log_a = (jnp.minimum(z, 0.0) - jnp.log(1.0 + jnp.exp(-jnp.abs(z)))) * (1.0 / GLA_TAU)
    return z, log_a


def _gla_decay(pb, log_a):
    tm = pb.shape[0]
    cum = _mm_exact_rhs(_chunk_matrix(tm, 'cumsum'), log_a)
    tot = _mm_exact_rhs(_chunk_matrix(tm, 'total'), log_a)
    w = jnp.exp(tot - cum)
    return w, pb[:, 256:512] * w, jnp.exp(tot)


def _per_head(fn):
    return jnp.concatenate([fn(h) for h in range(GLA_HEADS)], axis=1)


def _gla_read(qs, sb, c):
    rows = slice(c * CHUNK, (c + 1) * CHUNK)
    return _per_head(lambda h: lax.dot_general(qs[rows, h * 64:(h + 1) * 64], sb[:, h * 64:(h + 1) * 64], _NT,
                                               preferred_element_type=F32))


def _gla_fwd(tag, proj_b, proj_a, wa2, ba, ng, tm=512, exchange=()):
    t_len = proj_b.shape[0]
    cpt = tm // CHUNK

    def body(t, rows, halos, res, outs, accs, scr):
        pb = rows[0][...]
        _, log_a = _gla_gate(rows[1][...], res[0][...], res[1][...])
        _, kd, dec = _gla_decay(pb, log_a)
        kdb = kd.astype(_MXU_DTYPE)
        vb = pb[:, 512:1024].astype(_MXU_DTYPE)
        qs = (pb[:, 0:256] * (GLA_DK ** -0.5)).astype(_MXU_DTYPE)
        uts = []
        for c in range(cpt):
            rs = slice(c * CHUNK, (c + 1) * CHUNK)
            uts.append(_per_head(lambda h: lax.dot_general(vb[rs, h * 128:(h + 1) * 128], kdb[rs, h * 64:(h + 1) * 64],
                                                           _TN, preferred_element_type=F32)))
        s_new = scr[0][...]
        o = []
        for c in range(cpt):
            s_new = dec[c * CHUNK:c * CHUNK + 1] * s_new + uts[c]
            outs[1][c] = s_new
            o.append(_gla_read(qs, s_new.astype(_MXU_DTYPE), c))
        scr[0][...] = s_new
        o = jnp.concatenate(o, axis=0)
        on = _per_head(lambda h: o[:, h * 128:(h + 1) * 128] * lax.rsqrt(
            jnp.mean(jnp.square(o[:, h * 128:(h + 1) * 128]), axis=-1, keepdims=True) + EPS))
        r = pb[:, 1024:1536]
        outs[0][...] = (on * res[2][...] * (r * _sigmoid(r))).astype(_MXU_DTYPE)

    return _row_call("gla_fwd" + tag, body, t_len // tm, rows=[(proj_b, tm), (proj_a, tm)], res=[wa2, ba, ng],
                     outs=[((t_len, BW), _MXU_DTYPE, tm), ((t_len // CHUNK, GLA_DV, 256), F32, cpt)],
                     scratch=[((GLA_DV, 256), F32)], exchange=exchange)


def _gla_bwd(tag, proj_b, proj_a, dy, states, wa2, ba, ng, tm=512):
    t_len = proj_b.shape[0]
    cpt = tm // CHUNK

    def body(t, rows, halos, res, outs, accs, scr):
        pb = rows[0][...]
        pa = rows[1][...]
        dyv = rows[2][...]
        st_ref = rows[3]
        wa2v = res[0][...]
        z, log_a = _gla_gate(pa, wa2v, res[1][...])
        ngv = res[2][...]
        w, kd, dec = _gla_decay(pb, log_a)
        kdb = kd.astype(_MXU_DTYPE)
        vb = pb[:, 512:1024].astype(_MXU_DTYPE)
        qs = (pb[:, 0:256] * (GLA_DK ** -0.5)).astype(_MXU_DTYPE)
        chunks = [slice(c * CHUNK, (c + 1) * CHUNK) for c in range(cpt)]
        sbs = [st_ref[c].astype(_MXU_DTYPE) for c in range(cpt)]
        o = jnp.concatenate([_gla_read(qs, sbs[c], c) for c in range(cpt)], axis=0)
        r = pb[:, 1024:1536]
        sig = _sigmoid(r)
        sil = r * sig
        dos, ons = [], []
        for h in range(GLA_HEADS):
            hs = slice(h * 128, (h + 1) * 128)
            oh = o[:, hs]
            rstd = lax.rsqrt(jnp.mean(oh * oh, axis=-1, keepdims=True) + EPS)
            on = oh * rstd
            don = dyv[:, hs] * ngv[:, hs] * sil[:, hs]
            dos.append(rstd * (don - on * jnp.mean(don * on, axis=-1, keepdims=True)))
            ons.append(on)
        on = jnp.concatenate(ons, axis=1)
        accs[2][...] += jnp.sum(dyv * on * sil, axis=0, keepdims=True)
        dr = dyv * on * ngv * (sig * (1.0 + r * (1.0 - sig)))
        dob = jnp.concatenate(dos, axis=1).astype(_MXU_DTYPE)
        reads, dqs = [], []
        for c, rs in enumerate(chunks):
            reads.append(_per_head(lambda h: lax.dot_general(dob[rs, h * 128:(h + 1) * 128], qs[rs, h * 64:(h + 1) * 64],
                                                             _TN, preferred_element_type=F32)))
            dqs.append(_per_head(lambda h: lax.dot_general(dob[rs, h * 128:(h + 1) * 128], sbs[c][:, h * 64:(h + 1) * 64],
                                                           _NN, preferred_element_type=F32)))
        dst = scr[0][...]
        dubs, ddecs = [None] * cpt, [None] * cpt
        for c in reversed(range(cpt)):
            dst_tot = dst + reads[c]
            s_prev = st_ref[c - 1] if c > 0 else jnp.where(t > 0, halos[0][0], 0.0)
            ddecs[c] = jnp.broadcast_to(jnp.sum(dst_tot * s_prev, axis=0, keepdims=True), (CHUNK, 256))
            dst = dec[c * CHUNK:c * CHUNK + 1] * dst_tot
            dubs[c] = dst_tot.astype(_MXU_DTYPE)
        scr[0][...] = dst
        dkd = jnp.concatenate(
            [_per_head(lambda h: lax.dot_general(vb[rs, h * 128:(h + 1) * 128], dubs[c][:, h * 64:(h + 1) * 64], _NN,
                                                 preferred_element_type=F32)) for c, rs in enumerate(chunks)], axis=0)
        dv = jnp.concatenate(
            [_per_head(lambda h: lax.dot_general(kdb[rs, h * 64:(h + 1) * 64], dubs[c][:, h * 64:(h + 1) * 64], _NT,
                                                 preferred_element_type=F32)) for c, rs in enumerate(chunks)], axis=0)
        e = dkd * kd
        dtot = _mm_exact_rhs(_chunk_matrix(tm, 'total'), e) + jnp.concatenate(ddecs, axis=0) * dec
        last = (lax.broadcasted_iota(jnp.int32, e.shape, 0) & (CHUNK - 1)) == CHUNK - 1
        dla = _mm_exact_rhs(_chunk_matrix(tm, 'revsum'), jnp.where(last, dtot - e, -e))
        dz = dla * (1.0 / GLA_TAU) * _sigmoid(-z)
        dzb = dz.astype(_MXU_DTYPE)
        dq = jnp.concatenate(dqs, axis=0) * (GLA_DK ** -0.5)
        outs[0][...] = jnp.concatenate([dq, dkd * w, dv, dr], axis=1).astype(_MXU_DTYPE)
        outs[1][...] = lax.dot_general(dzb, wa2v.astype(_MXU_DTYPE), _NT, preferred_element_type=F32).astype(_MXU_DTYPE)
        accs[0][...] += lax.dot_general(pa.astype(_MXU_DTYPE), dzb, _TN, preferred_element_type=F32)
        accs[1][...] += jnp.sum(dz, axis=0, keepdims=True)

    return _row_call("gla_bwd" + tag, body, t_len // tm,
                     rows=[(proj_b, tm), (proj_a, tm), (dy, tm), (states, cpt)],
                     halos=[(states, 1, cpt, 'prev')], res=[wa2, ba, ng],
                     outs=[((t_len, 1536), _MXU_DTYPE, tm), ((t_len, GLA_RANK), _MXU_DTYPE, tm)],
                     accs=[(GLA_RANK, 256), (1, 256), (1, BW)], scratch=[((GLA_DV, 256), F32)], reverse=True)


ATT_TM = 256
ATT_KEYS = ATT_TM + (ATT_BAND - 1) * CHUNK


def _rel_index():
    l_idx = np.arange(CHUNK)[:, None]
    m_idx = np.arange(BAND)[None, :]
    rel = l_idx + (ATT_BAND - 1) * CHUNK - m_idx
    return jnp.asarray((np.clip(rel, -(CHUNK - 1), MAX_REL) + (CHUNK - 1)).reshape(1, CHUNK * BAND), jnp.int32)


BIAS_COLS = 4096


def _bias_expand(tag, rel_bias):
    n = CHUNK * BAND

    def kern(rel_ref, idx_ref, o_ref):
        onehot = (lax.broadcasted_iota(jnp.int32, (REL_TABLE, BIAS_COLS), 0) == idx_ref[...]).astype(jnp.bfloat16)
        o_ref[...] = _mm_exact_lhs(rel_ref[...], onehot)

    return pl.pallas_call(
        kern, grid=(n // BIAS_COLS,),
        in_specs=[pl.BlockSpec((ATT_HEADS, REL_TABLE), lambda i: (0, 0)), pl.BlockSpec((1, BIAS_COLS), lambda i: (0, i))],
        out_specs=pl.BlockSpec((ATT_HEADS, BIAS_COLS), lambda i: (0, i)),
        out_shape=jax.ShapeDtypeStruct((ATT_HEADS, n), F32), name="bias_expand" + tag)(rel_bias, _rel_index())


def _bias_tile(tag, bias):
    per = ATT_TM // CHUNK

    def kern(b_ref, o_ref):
        bv = b_ref[...]
        for j in range(per):
            parts = [jnp.full((CHUNK, j * CHUNK), NEG_INF, F32)] if j else []
            parts.append(bv)
            if j < per - 1:
                parts.append(jnp.full((CHUNK, (per - 1 - j) * CHUNK), NEG_INF, F32))
            o_ref[j * CHUNK:(j + 1) * CHUNK, :] = jnp.concatenate(parts, axis=1)

    return pl.pallas_call(
        kern, grid=(ATT_HEADS,), in_specs=[pl.BlockSpec((None, CHUNK, BAND), lambda h: (h, 0, 0))],
        out_specs=pl.BlockSpec((None, ATT_TM, ATT_KEYS), lambda h: (h, 0, 0)),
        out_shape=jax.ShapeDtypeStruct((ATT_HEADS, ATT_TM, ATT_KEYS), F32), name="bias_tile" + tag)(bias)


def _bias_untile(tag, dbias):
    per = ATT_TM // CHUNK

    def kern(d_ref, o_ref):
        acc = d_ref[0:CHUNK, 0:BAND]
        for j in range(1, per):
            acc = acc + d_ref[j * CHUNK:(j + 1) * CHUNK, j * CHUNK:j * CHUNK + BAND]
        o_ref[...] = acc

    return pl.pallas_call(
        kern, grid=(ATT_HEADS,), in_specs=[pl.BlockSpec((None, ATT_TM, ATT_KEYS), lambda h: (h, 0, 0))],
        out_specs=pl.BlockSpec((None, CHUNK, BAND), lambda h: (h, 0, 0)),
        out_shape=jax.ShapeDtypeStruct((ATT_HEADS, CHUNK, BAND), F32), name="bias_untile" + tag)(dbias)


def _bias_reduce(tag, dbias):
    n = CHUNK * BAND

    def kern(db_ref, idx_ref, o_ref):
        @pl.when(pl.program_id(0) == 0)
        def _():
            o_ref[...] = jnp.zeros(o_ref.shape, o_ref.dtype)

        onehot = (lax.broadcasted_iota(jnp.int32, (REL_TABLE, BIAS_COLS), 0) == idx_ref[...]).astype(jnp.bfloat16)
        o_ref[...] += _mm_exact_lhs(db_ref[...], onehot, _NT)

    return pl.pallas_call(
        kern, grid=(n // BIAS_COLS,),
        in_specs=[pl.BlockSpec((ATT_HEADS, BIAS_COLS), lambda i: (0, i)), pl.BlockSpec((1, BIAS_COLS), lambda i: (0, i))],
        out_specs=pl.BlockSpec((ATT_HEADS, REL_TABLE), lambda i: (0, 0)),
        out_shape=jax.ShapeDtypeStruct((ATT_HEADS, REL_TABLE), F32),
        compiler_params=pltpu.CompilerParams(dimension_semantics=("arbitrary",)),
        name="bias_reduce" + tag)(dbias, _rel_index())


def _attn_stage(t, pc_ref, p1_ref, p2_ref, kv):
    tm = ATT_TM
    kv[0:tm, :] = jnp.where(t > 1, p2_ref[:, 512:1536], 0.0).astype(kv.dtype)
    kv[tm:2 * tm, :] = jnp.where(t > 0, p1_ref[:, 512:1536], 0.0).astype(kv.dtype)
    kv[2 * tm:, :] = pc_ref[:, 512:1536].astype(kv.dtype)
    q = (pc_ref[:, 0:512] * (ATT_HD ** -0.5)).astype(_MXU_DTYPE)
    ok = lax.broadcasted_iota(jnp.int32, (tm, ATT_KEYS), 1) >= (2 - t) * tm
    return q, ok


def _attn_probs(q, kv, bias_h, ok, h):
    hs = slice(h * ATT_HD, (h + 1) * ATT_HD)
    s = lax.dot_general(q[:, hs], kv[:, hs], _NT, preferred_element_type=F32) + bias_h
    if ok is not None:
        s = jnp.where(ok, s, NEG_INF)
    e = jnp.exp(s - jnp.max(s, axis=-1, keepdims=True))
    return e * (1.0 / jnp.sum(e, axis=-1, keepdims=True))


def _attn_halos(proj_c):
    return [(proj_c, ATT_TM, 1, 'prev'), (proj_c, ATT_TM, 1, 'prev2')]


def _attn_fwd(tag, proj_c, bias, exchange=()):
    t_len = proj_c.shape[0]
    tm = ATT_TM

    def body(t, rows, halos, res, outs, accs, scr):
        b_ref, kv = res[0], scr[0]
        q, ok = _attn_stage(t, rows[0], halos[0], halos[1], kv)

        def heads(ok):
            o = []
            for h in range(ATT_HEADS):
                p = _attn_probs(q, kv, b_ref[h], ok, h).astype(_MXU_DTYPE)
                outs[1][:, h * ATT_KEYS:(h + 1) * ATT_KEYS] = p
                o.append(lax.dot_general(p, kv[:, BW + h * ATT_HD:BW + (h + 1) * ATT_HD], _NN,
                                         preferred_element_type=F32))
            outs[0][...] = jnp.concatenate(o, axis=1).astype(_MXU_DTYPE)

        pl.when(t < 2)(lambda: heads(ok))
        pl.when(t >= 2)(lambda: heads(None))

    return _row_call("attn_fwd" + tag, body, t_len // tm, rows=[(proj_c, tm)], halos=_attn_halos(proj_c),
                     res=[bias], outs=[((t_len, BW), _MXU_DTYPE, tm), ((t_len, ATT_HEADS * ATT_KEYS), _MXU_DTYPE, tm)],
                     scratch=[((ATT_KEYS, 1024), _MXU_DTYPE)], exchange=exchange)


def _attn_bwd(tag, proj_c, dy, probs, exchange=()):
    t_len = proj_c.shape[0]
    tm = ATT_TM
    scale = ATT_HD ** -0.5

    def body(t, rows, halos, res, outs, accs, scr):
        kv = scr[0]
        q, _ = _attn_stage(t, rows[0], halos[0], halos[1], kv)
        do = rows[1][...].astype(_MXU_DTYPE)
        dqs, dks, dvs = [], [], []
        for h in range(ATT_HEADS):
            hs = slice(h * ATT_HD, (h + 1) * ATT_HD)
            vs = slice(BW + h * ATT_HD, BW + (h + 1) * ATT_HD)
            pb = rows[2][:, h * ATT_KEYS:(h + 1) * ATT_KEYS]
            p = pb.astype(F32)
            dp = lax.dot_general(do[:, hs], kv[:, vs], _NT, preferred_element_type=F32)
            ds = p * (dp - jnp.sum(dp * p, axis=-1, keepdims=True))
            accs[0][h] += ds
            dsb = ds.astype(_MXU_DTYPE)
            dqs.append(lax.dot_general(dsb, kv[:, hs], _NN, preferred_element_type=F32) * scale)
            dks.append(lax.dot_general(dsb, q[:, hs], _TN, preferred_element_type=F32))
            dvs.append(lax.dot_general(pb, do[:, hs], _TN, preferred_element_type=F32))
        outs[0][...] = jnp.concatenate(dqs, axis=1).astype(_MXU_DTYPE)
        dkv = jnp.concatenate(dks + dvs, axis=1)
        after_one, after_two = scr[1], scr[2]
        outs[1][...] = (dkv[2 * tm:, :] + after_two[...]).astype(_MXU_DTYPE)
        after_two[...] = dkv[tm:2 * tm, :] + after_one[...]
        after_one[...] = dkv[0:tm, :]

    return _row_call("attn_bwd" + tag, body, t_len // tm, rows=[(proj_c, tm), (dy, tm), (probs, tm)],
                     halos=_attn_halos(proj_c),
                     outs=[((t_len, BW), _MXU_DTYPE, tm), ((t_len, 1024), _MXU_DTYPE, tm)],
                     accs=[(ATT_HEADS, ATT_TM, ATT_KEYS)],
                     scratch=[((ATT_KEYS, 1024), _MXU_DTYPE), ((tm, 1024), F32), ((tm, 1024), F32)],
                     reverse=True, exchange=exchange)


def _conv_glu(pd):
    a, g = pd[:, :BW], pd[:, BW:]
    sig = _sigmoid(g)
    return a, sig, a * sig


def _conv_stage(t, pd_ref, ph_ref, win):
    pd = pd_ref[...]
    a, sig, y0 = _conv_glu(pd)
    win[0:CONV_HALO, :] = jnp.where(t > 0, _conv_glu(ph_ref[...])[2], 0.0)
    win[CONV_HALO:CONV_HALO + pd.shape[0], :] = y0
    return a, sig


SUBLANES = 8


def _conv_shifted(win, sh):
    for b in range(SUBLANES):
        sh[b] = win[pl.ds(b, sh.shape[1]), :]


def _conv_taps_by_copy(offsets):
    groups = {}
    for j, o in enumerate(offsets):
        groups.setdefault(o % SUBLANES, []).append((j, o - o % SUBLANES))
    return [(rem, min(a for _, a in taps), max(a for _, a in taps) - min(a for _, a in taps), taps)
            for rem, taps in sorted(groups.items())]


def _conv_span(sh, rem, r0, lo, rows):
    return sh[rem, pl.ds(pl.multiple_of(r0 + lo, SUBLANES), rows), :]


def _conv_tap_sum(sh, w_ref, offsets, out_ref, init=None, rb=32):
    plan = _conv_taps_by_copy(offsets)

    def block(i, carry):
        r0 = pl.multiple_of(i * rb, rb)
        acc = jnp.zeros((rb, BW), F32) if init is None else jnp.broadcast_to(init, (rb, BW))
        for rem, lo, extra, taps in plan:
            span = _conv_span(sh, rem, r0, lo, extra + rb)
            for j, a in taps:
                acc = acc + w_ref[j:j + 1, :] * span[a - lo:a - lo + rb]
        out_ref[pl.ds(r0, rb), :] = acc
        return carry

    lax.fori_loop(0, out_ref.shape[0] // rb, block, 0)


def _conv_tap_corr(sh, d_ref, offsets, acc_ref, rb=16):
    for rem, lo, extra, taps in _conv_taps_by_copy(offsets):
        def block(i, sums, rem=rem, lo=lo, extra=extra, taps=taps):
            r0 = pl.multiple_of(i * rb, rb)
            d = d_ref[pl.ds(r0, rb), :]
            span = _conv_span(sh, rem, r0, lo, extra + rb)
            out = []
            for s, (j, a) in zip(sums, taps):
                prod = d * span[a - lo:a - lo + rb]
                for k in range(0, rb, SUBLANES):
                    s = s + prod[k:k + SUBLANES]
                out.append(s)
            return tuple(out)

        sums = lax.fori_loop(0, d_ref.shape[0] // rb, block,
                             tuple(jnp.zeros((SUBLANES, BW), F32) for _ in taps), unroll=2)
        for (j, _), s in zip(taps, sums):
            acc_ref[j:j + 1, :] += jnp.sum(s, axis=0, keepdims=True)


def _conv_scratch(tm):
    return [((tm + CONV_HALO + SUBLANES, BW), F32), ((SUBLANES, tm + CONV_HALO, BW), F32)]


def _conv_fwd(tag, proj_d, dw_w, dw_b, ln_g, ln_b, tm=512):
    t_len = proj_d.shape[0]
    lead = CONV_HALO - (CONV_K - 1)

    def body(t, rows, halos, res, outs, accs, scr):
        win, sh = scr
        _conv_stage(t, rows[0], halos[0], win)
        _conv_shifted(win, sh)
        _conv_tap_sum(sh, res[0], [lead + j for j in range(CONV_K)], outs[1], init=res[1][...])
        yl, _, _ = _ln_fwd(outs[1][...], res[2][...], res[3][...])
        outs[0][...] = (yl * _sigmoid(yl)).astype(_MXU_DTYPE)

    return _row_call("conv_fwd" + tag, body, t_len // tm, rows=[(proj_d, tm)],
                     halos=[(proj_d, CONV_HALO, tm // CONV_HALO, 'prev')], res=[dw_w, dw_b, ln_g, ln_b],
                     outs=[((t_len, BW), _MXU_DTYPE, tm), ((t_len, BW), F32, tm)], scratch=_conv_scratch(tm))


def _conv_bwd_norm(tag, yc, dy, ln_g, ln_b, tm=512):
    t_len = yc.shape[0]

    def body(t, rows, halos, res, outs, accs, scr):
        lgv = res[0][...]
        yl, xh, rs = _ln_fwd(rows[0][...], lgv, res[1][...])
        sig = _sigmoid(yl)
        dyl = rows[1][...] * (sig * (1.0 + yl * (1.0 - sig)))
        dyc, dlg, dlb = _ln_bwd(dyl, xh, rs, lgv)
        outs[0][...] = dyc
        accs[0][...] += dlg
        accs[1][...] += dlb
        accs[2][...] += jnp.sum(dyc, axis=0, keepdims=True)

    return _row_call("conv_bwd_norm" + tag, body, t_len // tm, rows=[(yc, tm), (dy, tm)], res=[ln_g, ln_b],
                     outs=[((t_len, BW), F32, tm)], accs=[(1, BW), (1, BW), (1, BW)])


def _conv_bwd_taps(tag, proj_d, dyc, dw_w, tm=512):
    t_len = proj_d.shape[0]
    nt = t_len // tm
    lead = CONV_HALO - (CONV_K - 1)

    def body(t, rows, halos, res, outs, accs, scr):
        win, sh, wd, shd, dy0_ref = scr
        a, sig = _conv_stage(t, rows[0], halos[0], win)
        _conv_shifted(win, sh)
        wd[0:tm, :] = rows[1][...]
        wd[tm:tm + CONV_HALO, :] = jnp.where(t < nt - 1, halos[1][...], 0.0)
        _conv_shifted(wd, shd)
        _conv_tap_corr(sh, rows[1], [lead + j for j in range(CONV_K)], accs[0])
        _conv_tap_sum(shd, res[0], [CONV_K - 1 - j for j in range(CONV_K)], dy0_ref)
        dy0 = dy0_ref[...]
        outs[0][...] = jnp.concatenate([dy0 * sig, dy0 * a * sig * (1.0 - sig)], axis=1).astype(_MXU_DTYPE)

    return _row_call("conv_bwd_taps" + tag, body, nt, rows=[(proj_d, tm), (dyc, tm)],
                     halos=[(proj_d, CONV_HALO, tm // CONV_HALO, 'prev'), (dyc, CONV_HALO, tm // CONV_HALO, 'next')],
                     res=[dw_w], outs=[((t_len, 2 * BW), _MXU_DTYPE, tm)], accs=[(CONV_K, BW)],
                     scratch=_conv_scratch(tm) + _conv_scratch(tm) + [((tm, BW), F32)])


def _merge_fwd(tag, h, xn, ys, w_gate, b_gate, w_branch, w_out, tm=512):
    t_len = h.shape[0]

    def body(t, rows, halos, res, outs, accs, scr):
        xnv = rows[1][...]
        wg_ref, bg_ref, wb_ref, wo_ref = res
        merged = jnp.zeros((tm, D_MODEL), F32)
        for n in range(4):
            cs = slice(n * D_MODEL, (n + 1) * D_MODEL)
            z = lax.dot_general(xnv, wg_ref[n], _NN, preferred_element_type=F32) + bg_ref[n:n + 1, :]
            bo = lax.dot_general(rows[2 + n][...], wb_ref[n], _NN, preferred_element_type=F32)
            outs[0][:, cs] = z.astype(_MXU_DTYPE)
            outs[1][:, cs] = bo.astype(_MXU_DTYPE)
            merged = merged + _sigmoid(z) * bo
        mb = merged.astype(_MXU_DTYPE)
        outs[2][...] = mb
        outs[3][...] = rows[0][...] + lax.dot_general(mb, wo_ref[...], _NN, preferred_element_type=F32)

    return _row_call("merge_fwd" + tag, body, t_len // tm, rows=[(h, tm), (xn, tm)] + [(y, tm) for y in ys],
                     res=[w_gate, b_gate, w_branch, w_out],
                     outs=[((t_len, 4 * D_MODEL), _MXU_DTYPE, tm), ((t_len, 4 * D_MODEL), _MXU_DTYPE, tm),
                           ((t_len, D_MODEL), _MXU_DTYPE, tm), ((t_len, D_MODEL), F32, tm)])


def _merge_bwd(tag, dh, gate_pre, bo, w_gate, w_branch, w_out, tm=256):
    t_len = dh.shape[0]

    def body(t, rows, halos, res, outs, accs, scr):
        wg_ref, wb_ref, wo_ref = res
        dhb = rows[0][...].astype(_MXU_DTYPE)
        outs[0][...] = dhb
        dmerged = lax.dot_general(dhb, wo_ref[...], _NT, preferred_element_type=F32)
        dxn = jnp.zeros((tm, D_MODEL), F32)
        dbg = []
        for n in range(4):
            cs = slice(n * D_MODEL, (n + 1) * D_MODEL)
            g = _sigmoid(rows[1][:, cs].astype(F32))
            dbo = (dmerged * g).astype(_MXU_DTYPE)
            dgp = dmerged * rows[2][:, cs].astype(F32) * (g * (1.0 - g))
            dgb = dgp.astype(_MXU_DTYPE)
            outs[1][:, cs] = dbo
            outs[2][:, cs] = dgb
            outs[4 + n][...] = lax.dot_general(dbo, wb_ref[n], _NT, preferred_element_type=F32)
            dxn = dxn + lax.dot_general(dgb, wg_ref[n], _NT, preferred_element_type=F32)
            dbg.append(jnp.sum(dgp, axis=0, keepdims=True))
        outs[3][...] = dxn
        accs[0][...] += jnp.concatenate(dbg, axis=1)

    return _row_call("merge_bwd" + tag, body, t_len // tm, rows=[(dh, tm), (gate_pre, tm), (bo, tm)],
                     res=[w_gate, w_branch, w_out],
                     outs=[((t_len, D_MODEL), _MXU_DTYPE, tm), ((t_len, 4 * D_MODEL), _MXU_DTYPE, tm),
                           ((t_len, 4 * D_MODEL), _MXU_DTYPE, tm), ((t_len, D_MODEL), F32, tm)]
                     + [((t_len, BW), F32, tm)] * 4,
                     accs=[(1, 4 * D_MODEL)])


FF_COLS = 1024


def _ffn_fwd(tag, h, g2, w1, w2, tm=512):
    t_len = h.shape[0]

    def body(t, rows, halos, res, outs, accs, scr):
        hv = rows[0][...]
        hn = (hv * _rms_stat(hv) * res[0][...]).astype(_MXU_DTYPE)
        outs[0][...] = hn
        acc = hv
        for c in range(D_FF // FF_COLS):
            cs = slice(c * FF_COLS, (c + 1) * FF_COLS)
            pre = lax.dot_general(hn, res[1][:, cs], _NN, preferred_element_type=F32)
            outs[1][:, cs] = pre
            ff = jnp.square(jnp.maximum(pre, 0.0)).astype(_MXU_DTYPE)
            acc = acc + lax.dot_general(ff, res[2][cs, :], _NN, preferred_element_type=F32)
        outs[2][...] = acc

    return _row_call("ffn_fwd" + tag, body, t_len // tm, rows=[(h, tm)], res=[g2, w1, w2],
                     outs=[((t_len, D_MODEL), _MXU_DTYPE, tm), ((t_len, D_FF), F32, tm), ((t_len, D_MODEL), F32, tm)])


def _ffn_bwd(tag, dh, h, pre, g2, w1, w2, tm=256, exchange=()):
    t_len = dh.shape[0]

    def body(t, rows, halos, res, outs, accs, scr):
        dhv = rows[0][...]
        hv = rows[1][...]
        dhb = dhv.astype(_MXU_DTYPE)
        outs[0][...] = dhb
        dhn = jnp.zeros((tm, D_MODEL), F32)
        for c in range(D_FF // FF_COLS):
            cs = slice(c * FF_COLS, (c + 1) * FF_COLS)
            r = jnp.maximum(rows[2][:, cs], 0.0)
            outs[1][:, cs] = (r * r).astype(_MXU_DTYPE)
            dpre = (lax.dot_general(dhb, res[2][cs, :], _NT, preferred_element_type=F32) * (2.0 * r)).astype(_MXU_DTYPE)
            outs[2][:, cs] = dpre
            dhn = dhn + lax.dot_general(dpre, res[1][:, cs], _NT, preferred_element_type=F32)
        dres, dg = _rms_bwd(dhn, hv, res[0][...], _rms_stat(hv))
        outs[3][...] = dhv + dres
        accs[0][...] += dg

    return _row_call("ffn_bwd" + tag, body, t_len // tm, rows=[(dh, tm), (h, tm), (pre, tm)], res=[g2, w1, w2],
                     outs=[((t_len, D_MODEL), _MXU_DTYPE, tm), ((t_len, D_FF), _MXU_DTYPE, tm),
                           ((t_len, D_FF), _MXU_DTYPE, tm), ((t_len, D_MODEL), F32, tm)],
                     accs=[(1, D_MODEL)], exchange=exchange)


def _ple_fwd(tag, h, p, g3, w_pg, b_pg, w_ple, tm=512):
    t_len = h.shape[0]

    def body(t, rows, halos, res, outs, accs, scr):
        hv = rows[0][...]
        hg = (hv * _rms_stat(hv) * res[0][...]).astype(_MXU_DTYPE)
        pb = rows[1][...].astype(_MXU_DTYPE)
        pg = _sigmoid(lax.dot_general(hg, res[1][...], _NN, preferred_element_type=F32) + res[2][...])
        pe = lax.dot_general(pb, res[3][...], _NN, preferred_element_type=F32)
        outs[0][...] = hg
        outs[1][...] = pb
        outs[2][...] = pg
        outs[3][...] = hv + pg * pe

    return _row_call("ple_fwd" + tag, body, t_len // tm, rows=[(h, tm), (p, tm)], res=[g3, w_pg, b_pg, w_ple],
                     outs=[((t_len, D_MODEL), _MXU_DTYPE, tm), ((t_len, PLE_DIM), _MXU_DTYPE, tm),
                           ((t_len, D_MODEL), F32, tm), ((t_len, D_MODEL), F32, tm)])


def _ple_bwd(tag, dh, h, pg, p_b, g3, w_pg, w_ple, tm=512):
    t_len = dh.shape[0]

    def body(t, rows, halos, res, outs, accs, scr):
        dhv = rows[0][...]
        hv = rows[1][...]
        pgv = rows[2][...]
        pe = lax.dot_general(rows[3][...], res[2][...], _NN, preferred_element_type=F32)
        dgp = dhv * pe * (pgv * (1.0 - pgv))
        dgb = dgp.astype(_MXU_DTYPE)
        outs[0][...] = dgb
        outs[1][...] = (dhv * pgv).astype(_MXU_DTYPE)
        dhg = lax.dot_general(dgb, res[1][...], _NT, preferred_element_type=F32)
        dres, dg = _rms_bwd(dhg, hv, res[0][...], _rms_stat(hv))
        outs[2][...] = dhv + dres
        accs[0][...] += jnp.sum(dgp, axis=0, keepdims=True)
        accs[1][...] += dg

    return _row_call("ple_bwd" + tag, body, t_len // tm, rows=[(dh, tm), (h, tm), (pg, tm), (p_b, tm)],
                     res=[g3, w_pg, w_ple],
                     outs=[((t_len, D_MODEL), _MXU_DTYPE, tm), ((t_len, D_MODEL), _MXU_DTYPE, tm),
                           ((t_len, D_MODEL), F32, tm)],
                     accs=[(1, D_MODEL), (1, D_MODEL)])


def _inproj_bwd(tag, dh, h, dxn_gate, dprojs, g1, w_in_t, tm=512, exchange=()):
    t_len = dh.shape[0]

    def body(t, rows, halos, res, outs, accs, scr):
        hv = rows[1][...]
        dxn = rows[2][...]
        for dp, (_, s0, n) in zip(rows[3:], IN_GROUPS_BWD):
            dxn = dxn + lax.dot_general(dp[...], res[1][s0:s0 + n, :], _NN, preferred_element_type=F32)
        dres, dg = _rms_bwd(dxn, hv, res[0][...], _rms_stat(hv))
        outs[0][...] = rows[0][...] + dres
        accs[0][...] += dg

    return _row_call("inproj_bwd" + tag, body, t_len // tm,
                     rows=[(dh, tm), (h, tm), (dxn_gate, tm)] + [(d, tm) for d in dprojs],
                     res=[g1, w_in_t], outs=[((t_len, D_MODEL), F32, tm)], accs=[(1, D_MODEL)],
                     exchange=exchange)


def _loss_head(h, target, gf, tm=512):
    t_len = h.shape[0]

    def body(t, rows, halos, res, outs, accs, scr):
        hv = rows[0][...]
        g = res[0][...]
        r = _rms_stat(hv)
        diff = hv * r * g - rows[1][...]
        accs[0][...] += 0.5 * jnp.sum(jnp.mean(diff * diff, axis=-1, keepdims=True), axis=0, keepdims=True)
        dh, dg = _rms_bwd(diff * (1.0 / D_MODEL), hv, g, r)
        outs[0][...] = dh
        accs[1][...] += dg

    return _row_call("loss_head", body, t_len // tm, rows=[(h, tm), (target, tm)], res=[gf],
                     outs=[((t_len, D_MODEL), F32, tm)], accs=[(1, 128), (1, D_MODEL)])


def _row(v):
    return v.reshape(1, -1)


GATHER_DURING = (('inproj', ('w_gate', 'w_branch', 'w_out')), ('gla', ('w_ff1',)), ('attn', ('w_ff2', 'w_ple_gate', 'w_ple')))
SCATTER_DURING_ATTN = ('w_ple_gate', 'w_ple', 'w_ff1', 'w_ff2', 'w_out', 'w_gate', 'w_branch')


def _gather_items(shards, names):
    items = []
    for n in names:
        s = shards[n]
        ax = SHARD_AXIS[n] - 1
        if n == 'w_in':
            items.append(_gather_item(s))
        else:
            items.append(_gather_item(s, s.shape[:ax] + (N_DEV * s.shape[ax],) + s.shape[ax + 1:], ax))
    return items


def _land(w, names, arrays):
    for n, a in zip(names, arrays):
        w[n] = a.reshape(IN_COLS, D_MODEL) if n == 'w_in' else a


def _layer_fwd(i, h, p_i, w, shards, next_shards):
    tag = "_l%d" % i
    during = dict(GATHER_DURING)
    res = _inproj_fwd(tag, h, _row(w['norm1_g']), w['w_in'], exchange=_gather_items(shards, during['inproj']))
    xn, pa, pb, pr, pc, pd = res[:6]
    _land(w, during['inproj'], res[6:])
    sg_bt = w['sg_b'].T
    y_a = _sg_fwd(tag, pa, _row(w['sg_ln_g']), _row(w['sg_ln_b']), w['sg_w'], sg_bt)
    res = _gla_fwd(tag, pb, pr, w['gla_w_a2'], _row(w['gla_b_a']), _row(w['gla_norm_g']),
                   exchange=_gather_items(shards, during['gla']))
    y_b, states = res[:2]
    _land(w, during['gla'], res[2:])
    bias = _bias_tile(tag, _bias_expand(tag, w['att_rel_bias']).reshape(ATT_HEADS, CHUNK, BAND))
    items = _gather_items(shards, during['attn']) + (_gather_items(next_shards, ['w_in']) if next_shards else [])
    res = _attn_fwd(tag, pc, bias, exchange=items)
    y_c, probs = res[:2]
    _land(w, during['attn'], res[2:2 + len(during['attn'])])
    next_w_in = res[-1].reshape(IN_COLS, D_MODEL) if next_shards else None
    y_d, yc = _conv_fwd(tag, pd, w['conv_dw_w'], _row(w['conv_dw_b']), _row(w['conv_ln_g']), _row(w['conv_ln_b']))
    ys = (y_a, y_b, y_c, y_d)
    gate, bo, merged, h1 = _merge_fwd(tag, h, xn, ys, w['w_gate'], w['b_gate'], w['w_branch'], w['w_out'])
    hn, pre, h2 = _ffn_fwd(tag, h1, _row(w['norm2_g']), w['w_ff1'], w['w_ff2'])
    hg, p_b, pg, h3 = _ple_fwd(tag, h2, p_i, _row(w['norm3_g']), w['w_ple_gate'], _row(w['b_ple_gate']), w['w_ple'])
    saved = dict(h=h, xn=xn, pa=pa, pb=pb, pr=pr, pc=pc, pd=pd, states=states, probs=probs, yc=yc, ys=ys, gate=gate,
                 bo=bo, merged=merged, h1=h1, hn=hn, pre=pre, h2=h2, hg=hg, p_b=p_b, pg=pg, sg_bt=sg_bt)
    return h3, saved, next_w_in


def _layer_bwd(i, dh3, s, w, tail=None):
    tag = "_l%d" % i
    g = {}
    dgp, dpe, dh2, db_pg, dg3 = _ple_bwd(tag, dh3, s['h2'], s['pg'], s['p_b'], _row(w['norm3_g']), w['w_ple_gate'],
                                         w['w_ple'])
    g['b_ple_gate'], g['norm3_g'] = db_pg[0], dg3[0]
    g['w_ple_gate'] = _tn_call("dw_ple_gate" + tag, s['hg'], dgp, D_MODEL, D_MODEL)
    g['w_ple'] = _tn_call("dw_ple" + tag, s['p_b'], dpe, PLE_DIM, D_MODEL)

    dh2b, ffb, dpre, dh1, dg2 = _ffn_bwd(tag, dh2, s['h1'], s['pre'], _row(w['norm2_g']), w['w_ff1'], w['w_ff2'])
    g['norm2_g'] = dg2[0]
    g['w_ff1'] = _tn_call("dw_ff1" + tag, s['hn'], dpre, D_MODEL, FF_COLS, nblk=D_FF // FF_COLS)
    g['w_ff2'] = _tn_call("dw_ff2" + tag, ffb, dh2b, FF_COLS, D_MODEL, nblk=D_FF // FF_COLS, a_col=True, b_col=False,
                          out='rows')

    dh1b, dbo, dgpre, dxn_gate, dy_a, dy_b, dy_c, dy_d, db_gate = _merge_bwd(
        tag, dh1, s['gate'], s['bo'], w['w_gate'], w['w_branch'], w['w_out'])
    g['b_gate'] = db_gate.reshape(4, D_MODEL)
    g['w_out'] = _tn_call("dw_out" + tag, s['merged'], dh1b, D_MODEL, D_MODEL)
    g['w_gate'] = _tn_call("dw_gate" + tag, s['xn'], dgpre, D_MODEL, D_MODEL, nblk=4, out='stack')
    g['w_branch'] = jnp.stack([_tn_call("dw_branch%d%s" % (n, tag), s['ys'][n], dbo, BW, D_MODEL, b_off=n)
                             for n in range(4)])

    lg, lb = _row(w['sg_ln_g']), _row(w['sg_ln_b'])
    dpa, dsg_w, dsg_bt, dlg, dlb = _sg_bwd(tag, s['pa'], dy_a, lg, lb, w['sg_w'], s['sg_bt'])
    g['sg_w'], g['sg_b'], g['sg_ln_g'], g['sg_ln_b'] = dsg_w, dsg_bt.T, dlg[0], dlb[0]

    dpb, dpr, dwa2, dba, dng = _gla_bwd(tag, s['pb'], s['pr'], dy_b, s['states'], w['gla_w_a2'],
                                        _row(w['gla_b_a']), _row(w['gla_norm_g']))
    g['gla_w_a2'], g['gla_b_a'], g['gla_norm_g'] = dwa2, dba[0], dng[0]

    items = [_scatter_item(g.pop(n), axis=SHARD_AXIS[n] - 1) for n in SCATTER_DURING_ATTN]
    res = _attn_bwd(tag, s['pc'], dy_c, s['probs'], exchange=items)
    dq, dkv, dbias = res[:3]
    parts = dict(zip(SCATTER_DURING_ATTN, res[3:]))
    g['att_rel_bias'] = _bias_reduce(tag, _bias_untile(tag, dbias).reshape(ATT_HEADS, CHUNK * BAND))

    cg, cb = _row(w['conv_ln_g']), _row(w['conv_ln_b'])
    dyc, dcg, dcb, ddwb = _conv_bwd_norm(tag, s['yc'], dy_d, cg, cb)
    dpd, ddw = _conv_bwd_taps(tag, s['pd'], dyc, w['conv_dw_w'])
    g['conv_ln_g'], g['conv_ln_b'], g['conv_dw_b'], g['conv_dw_w'] = dcg[0], dcb[0], ddwb[0], ddw

    dprojs = (dpa, dpb, dpr, dq, dkv, dpd)
    dw_in = jnp.concatenate([_tn_call("dw_in%s%s" % (name, tag), dp, s['xn'], n, D_MODEL)
                             for (name, _, n), dp in zip(IN_GROUPS_BWD, dprojs)], axis=0)
    items = [_scatter_item(dw_in.reshape(N_DEV, IN_COLS // N_DEV, D_MODEL))] + (tail(g) if tail else [])
    res = _inproj_bwd(tag, dh1, s['h'], dxn_gate, dprojs, _row(w['norm1_g']), w['w_in'], exchange=items)
    dh0, dg1 = res[:2]
    g['norm1_g'] = dg1[0]
    parts['w_in'] = res[2]
    return dh0, g, parts, res[3:]


def _local_step(x, p, target, final_g, layers, shards, tail):
    h = x
    saved = []
    for i in range(DEPTH):
        nxt = shards[i + 1] if i + 1 < DEPTH else None
        h, s, next_w_in = _layer_fwd(i, h, p[i], layers[i], shards[i], nxt)
        saved.append(s)
        if nxt:
            layers[i + 1]['w_in'] = next_w_in
    dh, loss, dgf = _loss_head(h, target, _row(final_g))
    small, parts, tail_out = [None] * DEPTH, [None] * DEPTH, None
    for i in reversed(range(DEPTH)):
        hook = (lambda g: tail([g] + small[1:])) if i == 0 else None
        dh, small[i], parts[i], out = _layer_bwd(i, dh, saved[i], layers[i], hook)
        if i == 0:
            tail_out = out
    return loss[0, 0], dh, dgf[0], small, parts, tail_out


def _peers():
    x, y, c = lax.axis_index("x"), lax.axis_index("y"), lax.axis_index("c")
    me = 4 * x + 2 * y + c
    out = []
    for k in range(1, N_DEV):
        px = (1 - x) if k & 4 else x
        py = (1 - y) if k & 2 else y
        pc = (1 - c) if k & 1 else c
        out.append((k - 1, (px, py, pc), 4 * px + 2 * py + pc))
    return me, out


def _block(ref, axis, idx, width):
    ix = [slice(None)] * len(ref.shape)
    ix[axis] = pl.ds(pl.multiple_of(idx * width, width), width)
    return ref.at[tuple(ix)]


def _slot(ref, idx):
    return ref.at[idx]


def _whole(ref, idx):
    return ref


def _gather_item(src, out_shape=None, axis=None):
    if axis is None:
        return dict(src=src, out=(N_DEV,) + src.shape, take=_whole, put=_slot)
    return dict(src=src, out=tuple(out_shape), take=_whole,
                put=lambda ref, s: _block(ref, axis, s, src.shape[axis]))


def _scatter_item(src, axis=None, lead=0):
    if axis is None:
        shape = src.shape[:lead] + src.shape[lead + 1:]
        take = lambda ref, s: ref.at[(slice(None),) * lead + (s,)]
    else:
        width = src.shape[axis] // N_DEV
        shape = src.shape[:axis] + (width,) + src.shape[axis + 1:]
        take = lambda ref, s: _block(ref, axis, s, width)
    return dict(src=src, out=(N_DEV,) + shape, take=take, put=_slot)


def _exchange_sems(n):
    return [pltpu.SemaphoreType.DMA((n * (N_DEV - 1),)), pltpu.SemaphoreType.DMA((n * (N_DEV - 1),)),
            pltpu.SemaphoreType.DMA((n,))]


def _exchange_copies(items, src_refs, out_refs, sems, start):
    send_sems, recv_sems, local_sems = sems
    me, peers = _peers()

    def remote(i, k, pos, receiver, sender):
        it = items[i]
        return pltpu.make_async_remote_copy(
            src_ref=it['take'](src_refs[i], receiver), dst_ref=it['put'](out_refs[i], sender),
            send_sem=send_sems.at[i * (N_DEV - 1) + k], recv_sem=recv_sems.at[i * (N_DEV - 1) + k],
            device_id=pos, device_id_type=pl.DeviceIdType.MESH)

    local = [pltpu.make_async_copy(it['take'](src_refs[i], me), it['put'](out_refs[i], me), local_sems.at[i])
             for i, it in enumerate(items)]
    if start:
        for cp in local:
            cp.start()
        for k, pos, flat in peers:
            for i in range(len(items)):
                remote(i, k, pos, flat, me).start()
    else:
        for k, pos, flat in peers:
            for i in range(len(items)):
                remote(i, k, pos, flat, flat).wait_recv()
        for k, pos, flat in peers:
            for i in range(len(items)):
                remote(i, k, pos, flat, me).wait_send()
        for cp in local:
            cp.wait()


def _gather_via_sibling(name, srcs):
    n = len(srcs)

    def body(*refs):
        src_refs, out_refs = refs[:n], refs[n:2 * n]
        send_sems, recv_sems, local_sems = refs[2 * n:]
        x, y, c = lax.axis_index("x"), lax.axis_index("y"), lax.axis_index("c")
        flat = lambda px, py, pc: 4 * px + 2 * py + pc
        me, sibling = (x, y, c), (x, y, 1 - c)
        chips = [(1 - x, y), (x, 1 - y), (1 - x, 1 - y)]

        def copy(i, k, block, to, own):
            return pltpu.make_async_remote_copy(
                src_ref=src_refs[i] if own else out_refs[i].at[flat(*block)], dst_ref=out_refs[i].at[flat(*block)],
                send_sem=send_sems.at[i * (N_DEV - 1) + k], recv_sem=recv_sems.at[i * (N_DEV - 1) + k],
                device_id=to, device_id_type=pl.DeviceIdType.MESH)

        local, sent = [], []
        for i in range(n):
            local.append(pltpu.make_async_copy(src_refs[i], out_refs[i].at[flat(*me)], local_sems.at[i]))
            local[-1].start()
            sent.append(copy(i, 0, me, sibling, True))
            sent += [copy(i, 1 + j, me, (*chip, c), True) for j, chip in enumerate(chips)]
            for cp in sent[-4:]:
                cp.start()
        for i in range(n):
            for j, chip in enumerate(chips):
                copy(i, 1 + j, (*chip, c), me, True).wait_recv()
                sent.append(copy(i, 4 + j, (*chip, c), sibling, False))
                sent[-1].start()
        for i in range(n):
            copy(i, 0, sibling, me, True).wait_recv()
            for j, chip in enumerate(chips):
                copy(i, 4 + j, (*chip, 1 - c), me, False).wait_recv()
        for cp in sent:
            cp.wait_send()
        for cp in local:
            cp.wait()

    any_spec = pl.BlockSpec(memory_space=pl.ANY)
    return pl.pallas_call(
        body, out_shape=[jax.ShapeDtypeStruct((N_DEV,) + s.shape, s.dtype) for s in srcs],
        in_specs=[any_spec] * n, out_specs=[any_spec] * n, scratch_shapes=_exchange_sems(n), name=name)(*srcs)


def _pack(arrays, dtype, lead=None):
    flat = [a.astype(dtype).reshape((lead, -1) if lead else (-1,)) for a in arrays]
    cat = jnp.concatenate(flat, axis=-1)
    n = cat.shape[-1]
    rows = -(-n // (PACK_COLS * SUBLANES)) * SUBLANES
    pad = rows * PACK_COLS - n
    if pad:
        cat = jnp.pad(cat, ((0, 0), (0, pad)) if lead else ((0, pad),))
    return cat.reshape((lead, rows, PACK_COLS) if lead else (rows, PACK_COLS))


def _unpack(buf, shapes, lead=None):
    flat = buf.reshape((lead, -1) if lead else (-1,))
    out, off = [], 0
    for shp in shapes:
        n = int(np.prod(shp))
        piece = flat[..., off:off + n]
        out.append(piece.reshape(((lead,) if lead else ()) + tuple(shp)))
        off += n
    return out


def _to_slabs(full, axis):
    shp = full.shape
    split = full.reshape(shp[:axis] + (N_DEV, shp[axis] // N_DEV) + shp[axis + 1:])
    return jnp.moveaxis(split, axis, 0)


def _from_slabs(slabs, axis):
    moved = jnp.moveaxis(slabs, 0, axis)
    shp = moved.shape
    return moved.reshape(shp[:axis] + (shp[axis] * shp[axis + 1],) + shp[axis + 2:])


def _adamw_block(r, c):
    if r % 8:
        return r, 256
    br = min(r, max(8, ADAMW_TILE * PACK_COLS // c))
    while r % br:
        br //= 2
    return br, c


def _adamw(name, partials, w, m, v):
    n_lead, r, c = w.shape
    br, bc = _adamw_block(r, c)
    ni, nj = r // br, c // bc
    c1 = 1.0 - ADAM_B1 ** ADAM_STEP
    c2 = 1.0 - ADAM_B2 ** ADAM_STEP

    def kern(*refs):
        p_refs = refs[:n_lead]
        w_ref, m_ref, v_ref, g_ref, d_ref, nm_ref, nv_ref = refs[n_lead:]
        layer = pl.program_id(0)
        g = None
        for l, p_ref in enumerate(p_refs):
            gl = p_ref[0].astype(F32)
            for s in range(1, N_DEV):
                gl = gl + p_ref[s].astype(F32)
            g = gl if g is None else jnp.where(layer == l, gl, g)
        nm = ADAM_B1 * m_ref[...] + (1.0 - ADAM_B1) * g
        nv = ADAM_B2 * v_ref[...] + (1.0 - ADAM_B2) * jnp.square(g)
        g_ref[...] = g
        nm_ref[...] = nm
        nv_ref[...] = nv
        d_ref[...] = -ADAM_LR * ((nm / c1) / (jnp.sqrt(nv / c2) + ADAM_EPS) + ADAM_WD * w_ref[...])

    def part_spec(mine):
        def index(l, i, j):
            before, after = l < mine, l > mine
            return (0, jnp.where(before, 0, jnp.where(after, ni - 1, i)), jnp.where(before, 0, jnp.where(after, nj - 1, j)))
        return pl.BlockSpec((N_DEV, br, bc), index)

    blk = pl.BlockSpec((None, br, bc), lambda l, i, j: (l, i, j))
    return pl.pallas_call(
        kern, grid=(n_lead, ni, nj),
        in_specs=[part_spec(l) for l in range(n_lead)] + [blk, blk, blk],
        out_specs=[blk] * 4, out_shape=[jax.ShapeDtypeStruct(w.shape, F32)] * 4,
        compiler_params=pltpu.CompilerParams(dimension_semantics=("arbitrary",) * 3),
        name=name)(*partials, w, m, v)


def _as_rows(a, lead):
    return a.reshape(a.shape[:lead] + (-1, a.shape[-1]))


def kernel(x, p, norm1_g, w_in, sg_ln_g, sg_ln_b, sg_w, sg_b, gla_w_a2, gla_b_a, gla_norm_g, att_rel_bias, conv_dw_w, conv_dw_b, conv_ln_g, conv_ln_b, w_branch, w_gate, b_gate, w_out, norm2_g, w_ff1, w_ff2, norm3_g, w_ple_gate, b_ple_gate, w_ple, final_g, loss_target, m_norm1_g, m_w_in, m_sg_ln_g, m_sg_ln_b, m_sg_w, m_sg_b, m_gla_w_a2, m_gla_b_a, m_gla_norm_g, m_att_rel_bias, m_conv_dw_w, m_conv_dw_b, m_conv_ln_g, m_conv_ln_b, m_w_branch, m_w_gate, m_b_gate, m_w_out, m_norm2_g, m_w_ff1, m_w_ff2, m_norm3_g, m_w_ple_gate, m_b_ple_gate, m_w_ple, m_final_g, v_norm1_g, v_w_in, v_sg_ln_g, v_sg_ln_b, v_sg_w, v_sg_b, v_gla_w_a2, v_gla_b_a, v_gla_norm_g, v_att_rel_bias, v_conv_dw_w, v_conv_dw_b, v_conv_ln_g, v_conv_ln_b, v_w_branch, v_w_gate, v_b_gate, v_w_out, v_norm2_g, v_w_ff1, v_w_ff2, v_norm3_g, v_w_ple_gate, v_b_ple_gate, v_w_ple, v_final_g):
    args = locals()
    wts = {n: args[n] for n in WEIGHTS}
    mom = {n: args['m_' + n] for n in WEIGHTS}
    var = {n: args['v_' + n] for n in WEIGHTS}

    local = {d_name: dict(d, w_in=jnp.swapaxes(d['w_in'], 1, 2))
             for d_name, d in (("w", wts), ("m", mom), ("v", var))}
    shards = [{n: local["w"][n][i].astype(_MXU_DTYPE) for n in MXU_WEIGHTS} for i in range(DEPTH)]

    first_w_in, vec = _gather_via_sibling("gather_first_weights",
                                          [shards[0]['w_in'], _pack([wts[n] for n in VEC_WEIGHTS], F32)])
    vec_full = {n: _from_slabs(slabs, SHARD_AXIS[n])
                for n, slabs in zip(VEC_WEIGHTS, _unpack(vec, [wts[n].shape for n in VEC_WEIGHTS], lead=N_DEV))}
    small_names = [n for n in WEIGHTS if n not in MXU_WEIGHTS and n != 'final_g']
    layers = [{n: (vec_full[n] if n in vec_full else wts[n])[i] for n in small_names} for i in range(DEPTH)]
    _land(layers[0], ['w_in'], [first_w_in])

    def vec_items(small):
        stacked = [jnp.stack([g[n] for g in small]) for n in VEC_WEIGHTS]
        return [_scatter_item(_pack([_to_slabs(a, SHARD_AXIS[n]) for n, a in zip(VEC_WEIGHTS, stacked)], F32,
                                    lead=N_DEV))]

    loss, grad_x, dgf, small, parts, (vec_parts,) = _local_step(x[0], p[:, 0], loss_target[0], final_g, layers,
                                                               shards, vec_items)
    loss = lax.psum(loss, ("x", "y", "c"))

    grads = {n: jnp.stack([small[i][n] for i in range(DEPTH)]) for n in REPLICATED if n != 'final_g'}
    grads['final_g'] = dgf
    repl_parts, = _gather_via_sibling("gather_replicated_grads", [_pack([grads[n] for n in REPLICATED], F32)])

    results = {}
    for n in MXU_WEIGHTS:
        w3, m3, v3 = (_as_rows(local[d][n], 1) for d in ("w", "m", "v"))
        outs = _adamw("adamw_" + n, [parts[i][n].reshape((N_DEV,) + w3.shape[1:]) for i in range(DEPTH)], w3, m3, v3)
        for kind, a in zip(("grad", "delta", "new_m", "new_v"), outs):
            a = a.reshape(local["w"][n].shape)
            results[kind, n] = jnp.swapaxes(a, 1, 2) if n == 'w_in' else a
    for names, part, call in ((VEC_WEIGHTS, vec_parts, "adamw_vec"), (REPLICATED, repl_parts, "adamw_replicated")):
        packed = [_pack([d[n] for n in names], F32)[None] for d in (wts, mom, var)]
        outs = _adamw(call, [part], *packed)
        for kind, buf in zip(("grad", "delta", "new_m", "new_v"), outs):
            for n, a in zip(names, _unpack(buf[0], [wts[n].shape for n in names])):
                results[kind, n] = a
    return (loss, grad_x[None]) + tuple(results[kind, n] for kind in ("grad", "delta", "new_m", "new_v")
                                        for n in WEIGHTS)
```

```python
import functools

import numpy as np
import jax
import jax.numpy as jnp
from jax import lax
from jax.experimental import pallas as pl
from jax.experimental.pallas import tpu as pltpu

F32 = jnp.float32
_MXU_DTYPE = jnp.bfloat16
GRAD_DTYPE = jnp.bfloat16

N_DEV = 8
D_MODEL = 1024
DEPTH = 2
CHUNK = 64
PLE_DIM = 256
BW = 512
SG_BLOCK = 128
SG_GROUPS = 4
GLA_HEADS = 4
GLA_DK = 64
GLA_DV = 128
GLA_RANK = 16
GLA_TAU = 16.0
ATT_HEADS = 8
ATT_HD = 64
ATT_BAND = 9
BAND = ATT_BAND * CHUNK
MAX_REL = 256
REL_TABLE = CHUNK + MAX_REL
CONV_K = 31
CONV_HALO = 32
D_FF = 4096
EPS = 1e-6
NEG_INF = -1e30

IN_GROUPS = (("A", 0, 1024), ("B", 1024, 1536), ("a", 2560, 16), ("C", 2576, 1536), ("D", 4112, 1024))
IN_GROUPS_BWD = (IN_GROUPS[0], ("Ba", 1024, 1552), ("Cq", 2576, 512), ("Ckv", 3088, 1024), IN_GROUPS[4])
IN_COLS = 5136

ADAM_LR = 0.001
ADAM_B1 = 0.9
ADAM_B2 = 0.999
ADAM_EPS = 1e-08
ADAM_WD = 0.01
ADAM_STEP = 10

ADAMW_TILE = 128
PACK_COLS = 1024
VMEM_LIMIT_MB = 56

_NN = (((1,), (0,)), ((), ()))
_NT = (((1,), (1,)), ((), ()))
_TN = (((0,), (0,)), ((), ()))

WEIGHTS = ['norm1_g', 'w_in', 'sg_ln_g', 'sg_ln_b', 'sg_w', 'sg_b', 'gla_w_a2', 'gla_b_a', 'gla_norm_g',
           'att_rel_bias', 'conv_dw_w', 'conv_dw_b', 'conv_ln_g', 'conv_ln_b', 'w_branch', 'w_gate', 'b_gate',
           'w_out', 'norm2_g', 'w_ff1', 'w_ff2', 'norm3_g', 'w_ple_gate', 'b_ple_gate', 'w_ple', 'final_g']
SHARD_AXIS = {'w_in': 2, 'gla_w_a2': 2, 'att_rel_bias': 2, 'conv_dw_w': 2, 'w_branch': 3, 'w_gate': 2,
              'b_gate': 2, 'w_out': 1, 'w_ff1': 2, 'w_ff2': 1, 'w_ple_gate': 1, 'w_ple': 2}
MXU_WEIGHTS = ('w_in', 'w_branch', 'w_gate', 'w_out', 'w_ff1', 'w_ff2', 'w_ple_gate', 'w_ple')
VEC_WEIGHTS = ('gla_w_a2', 'att_rel_bias', 'conv_dw_w', 'b_gate')
SHARDED = tuple(n for n in WEIGHTS if n in SHARD_AXIS)
REPLICATED = tuple(n for n in WEIGHTS if n not in SHARD_AXIS)


def _mm(a, b, dims=_NN):
    return lax.dot_general(a.astype(_MXU_DTYPE), b.astype(_MXU_DTYPE), dims, preferred_element_type=F32)


def _split3(x):
    x1 = x.astype(jnp.bfloat16)
    r1 = x - x1.astype(F32)
    x2 = r1.astype(jnp.bfloat16)
    x3 = (r1 - x2.astype(F32)).astype(jnp.bfloat16)
    return x1, x2, x3


def _mm_exact_rhs(m, x, dims=_NN):
    return sum(lax.dot_general(m, xi, dims, preferred_element_type=F32) for xi in _split3(x))


def _mm_exact_lhs(x, m, dims=_NN):
    return sum(lax.dot_general(xi, m, dims, preferred_element_type=F32) for xi in _split3(x))


def _sigmoid(x):
    return 1.0 / (1.0 + jnp.exp(-x))


def _gelu(x):
    c = 0.7978845608028654
    t = jnp.tanh(c * (x + 0.044715 * x * x * x))
    return 0.5 * x * (1.0 + t), t


def _gelu_grad(x, t):
    c = 0.7978845608028654
    return 0.5 * (1.0 + t) + 0.5 * x * (1.0 - t * t) * c * (1.0 + 3.0 * 0.044715 * x * x)


def _rms_stat(h):
    return lax.rsqrt(jnp.mean(h * h, axis=-1, keepdims=True) + EPS)


def _rms_bwd(dy, h, g, r):
    hh = h * r
    dhh = dy * g
    dh = r * (dhh - hh * jnp.mean(dhh * hh, axis=-1, keepdims=True))
    return dh, jnp.sum(dy * hh, axis=0, keepdims=True)


def _ln_fwd(x, g, b):
    mu = jnp.mean(x, axis=-1, keepdims=True)
    xc = x - mu
    rs = lax.rsqrt(jnp.mean(xc * xc, axis=-1, keepdims=True) + EPS)
    xh = xc * rs
    return xh * g + b, xh, rs


def _ln_bwd(dy, xh, rs, g):
    dxh = dy * g
    dx = rs * (dxh - jnp.mean(dxh, axis=-1, keepdims=True) - xh * jnp.mean(dxh * xh, axis=-1, keepdims=True))
    return dx, jnp.sum(dy * xh, axis=0, keepdims=True), jnp.sum(dy, axis=0, keepdims=True)


def _row_call(name, body, nt, rows=(), halos=(), res=(), outs=(), accs=(), scratch=(), reverse=False, exchange=()):
    def pos(i):
        return (nt - 1 - i) if reverse else i

    def lead(ndim, f):
        return lambda i: (f(pos(i)),) + (0,) * (ndim - 1)

    in_specs, operands = [], []
    for a, tile in rows:
        in_specs.append(pl.BlockSpec((tile,) + a.shape[1:], lead(a.ndim, lambda t: t)))
        operands.append(a)
    for a, blk, per, side in halos:
        last = a.shape[0] // blk - 1
        delta = {'prev2': -2, 'prev': -1, 'next': per}[side]
        f = lambda t, per=per, last=last, delta=delta: jnp.clip(t * per + delta, 0, last)
        in_specs.append(pl.BlockSpec((blk,) + a.shape[1:], lead(a.ndim, f)))
        operands.append(a)
    for a in res:
        in_specs.append(pl.BlockSpec(a.shape, lambda i, nd=a.ndim: (0,) * nd, pipeline_mode=pl.Buffered(1)))
        operands.append(a)
    out_specs, out_shape = [], []
    for shape, dtype, tile in outs:
        out_specs.append(pl.BlockSpec((tile,) + tuple(shape[1:]), lead(len(shape), lambda t: t)))
        out_shape.append(jax.ShapeDtypeStruct(tuple(shape), dtype))
    for shape in accs:
        out_specs.append(pl.BlockSpec(tuple(shape), lambda i, nd=len(shape): (0,) * nd))
        out_shape.append(jax.ShapeDtypeStruct(tuple(shape), F32))
    nx = len(exchange)
    any_spec = pl.BlockSpec(memory_space=pl.ANY)
    for it in exchange:
        in_specs.append(any_spec)
        operands.append(it['src'])
        out_specs.append(any_spec)
        out_shape.append(jax.ShapeDtypeStruct(it['out'], it['src'].dtype))
    sizes = (len(rows), len(halos), len(res), nx, len(outs), len(accs), nx, len(scratch), 3 if nx else 0)

    def kern(*refs):
        i = pl.program_id(0)
        groups, at = [], 0
        for n in sizes:
            groups.append(refs[at:at + n])
            at += n
        row_refs, halo_refs, res_refs, x_src, out_refs, acc_refs, x_dst, scr_refs, sems = groups

        @pl.when(i == 0)
        def _():
            for r in tuple(acc_refs) + tuple(scr_refs):
                r[...] = jnp.zeros(r.shape, r.dtype)
            if nx:
                _exchange_copies(exchange, x_src, x_dst, sems, start=True)

        body(pos(i), row_refs, halo_refs, res_refs, out_refs, acc_refs, scr_refs)

        if nx:
            @pl.when(i == nt - 1)
            def _():
                _exchange_copies(exchange, x_src, x_dst, sems, start=False)

    result = pl.pallas_call(
        kern, grid=(nt,), in_specs=in_specs, out_specs=out_specs, out_shape=out_shape,
        scratch_shapes=[pltpu.VMEM(tuple(s), d) for s, d in scratch] + (_exchange_sems(nx) if nx else []),
        compiler_params=pltpu.CompilerParams(dimension_semantics=("arbitrary",),
                                             vmem_limit_bytes=VMEM_LIMIT_MB << 20),
        name=name)(*operands)
    return tuple(result)


def _tn_call(name, a, b, k, n, nblk=1, a_col=False, b_col=True, b_off=0, out='cols', tile=2048):
    tile = min(tile, a.shape[0])
    nt = a.shape[0] // tile
    if out == 'cols':
        o_shape, o_spec = (k, nblk * n), pl.BlockSpec((k, n), lambda j, t: (0, j))
    elif out == 'rows':
        o_shape, o_spec = (nblk * k, n), pl.BlockSpec((k, n), lambda j, t: (j, 0))
    else:
        o_shape, o_spec = (nblk, k, n), pl.BlockSpec((None, k, n), lambda j, t: (j, 0, 0))

    def kern(a_ref, b_ref, o_ref, acc):
        @pl.when(pl.program_id(1) == 0)
        def _():
            acc[...] = jnp.zeros(acc.shape, acc.dtype)

        acc[...] += lax.dot_general(a_ref[...], b_ref[...], _TN, preferred_element_type=F32)

        @pl.when(pl.program_id(1) == nt - 1)
        def _():
            o_ref[...] = acc[...].astype(o_ref.dtype)

    return pl.pallas_call(
        kern, grid=(nblk, nt),
        in_specs=[pl.BlockSpec((tile, k), (lambda j, t: (t, j)) if a_col else (lambda j, t: (t, 0))),
                  pl.BlockSpec((tile, n), (lambda j, t: (t, j + b_off)) if b_col else (lambda j, t: (t, b_off)))],
        out_specs=o_spec, out_shape=jax.ShapeDtypeStruct(o_shape, GRAD_DTYPE),
        scratch_shapes=[pltpu.VMEM((k, n), F32)],
        compiler_params=pltpu.CompilerParams(dimension_semantics=("arbitrary", "arbitrary"),
                                             vmem_limit_bytes=VMEM_LIMIT_MB << 20),
        name=name)(a, b)


def _inproj_fwd(tag, h, g1, w_in_t, tm=512, exchange=()):
    t_len = h.shape[0]

    def body(t, rows, halos, res, outs, accs, scr):
        hv = rows[0][...]
        xn = (hv * _rms_stat(hv) * res[0][...]).astype(_MXU_DTYPE)
        outs[0][...] = xn
        for o, (_, s0, n) in zip(outs[1:], IN_GROUPS):
            o[...] = lax.dot_general(xn, res[1][s0:s0 + n, :], _NT, preferred_element_type=F32)

    outs = [((t_len, D_MODEL), _MXU_DTYPE, tm)] + [((t_len, n), F32, tm) for _, _, n in IN_GROUPS]
    return _row_call("inproj_fwd" + tag, body, t_len // tm, rows=[(h, tm)], res=[g1, w_in_t], outs=outs,
                     exchange=exchange)


def _sg_mask():
    row = lax.broadcasted_iota(jnp.int32, (SG_BLOCK, SG_BLOCK), 0)
    col = lax.broadcasted_iota(jnp.int32, (SG_BLOCK, SG_BLOCK), 1)
    return jnp.logical_or(row >= CHUNK, col < CHUNK)


def _sg_forward_parts(pa, lg, lb, w_ref, bt):
    tm = pa.shape[0]
    nb = tm // SG_BLOCK
    su, sv = pa[:, :BW], pa[:, BW:]
    u, tu = _gelu(su)
    gv, tv = _gelu(sv)
    vn, xh, rs = _ln_fwd(gv, lg, lb)
    mask = _sg_mask()
    wms, xs, ms = [], [], []
    for g in range(SG_GROUPS):
        wm = jnp.where(mask, w_ref[g], 0.0).astype(_MXU_DTYPE)
        xg = jnp.concatenate([vn[b * SG_BLOCK:(b + 1) * SG_BLOCK, g * 128:(g + 1) * 128] for b in range(nb)], axis=1)
        xg = xg.astype(_MXU_DTYPE)
        ms.append(lax.dot_general(wm, xg, _NN, preferred_element_type=F32) + bt[:, g:g + 1])
        wms.append(wm)
        xs.append(xg)
    mixed = _sg_unfold(ms, nb)
    return su, sv, u, tu, tv, xh, rs, wms, xs, mixed


def _sg_unfold(per_group, nb):
    return jnp.concatenate(
        [jnp.concatenate([per_group[g][:, b * 128:(b + 1) * 128] for g in range(SG_GROUPS)], axis=1)
         for b in range(nb)], axis=0)


def _sg_fwd(tag, proj_a, lg, lb, sg_w, sg_bt, tm=512):
    t_len = proj_a.shape[0]

    def body(t, rows, halos, res, outs, accs, scr):
        parts = _sg_forward_parts(rows[0][...], res[0][...], res[1][...], res[2], res[3][...])
        outs[0][...] = (parts[2] * parts[-1]).astype(_MXU_DTYPE)

    return _row_call("sg_fwd" + tag, body, t_len // tm, rows=[(proj_a, tm)], res=[lg, lb, sg_w, sg_bt],
                     outs=[((t_len, BW), _MXU_DTYPE, tm)])[0]


def _sg_bwd(tag, proj_a, dy, lg, lb, sg_w, sg_bt, tm=512):
    t_len = proj_a.shape[0]
    nb = tm // SG_BLOCK

    def body(t, rows, halos, res, outs, accs, scr):
        lgv = res[0][...]
        su, sv, u, tu, tv, xh, rs, wms, xs, mixed = _sg_forward_parts(rows[0][...], lgv, res[1][...], res[2], res[3][...])
        dyv = rows[1][...]
        dsu = dyv * mixed * _gelu_grad(su, tu)
        dmixed = dyv * u
        mask = _sg_mask()
        dxs, dbs = [], []
        for g in range(SG_GROUPS):
            dm = jnp.concatenate([dmixed[b * SG_BLOCK:(b + 1) * SG_BLOCK, g * 128:(g + 1) * 128] for b in range(nb)],
                                 axis=1)
            dmb = dm.astype(_MXU_DTYPE)
            dw = lax.dot_general(dmb, xs[g], _NT, preferred_element_type=F32)
            accs[0][g] += jnp.where(mask, dw, 0.0)
            dbs.append(jnp.sum(dm, axis=1, keepdims=True))
            dxs.append(lax.dot_general(wms[g], dmb, _TN, preferred_element_type=F32))
        accs[1][...] += jnp.concatenate(dbs, axis=1)
        dvn = _sg_unfold(dxs, nb)
        dgv, dlg, dlb = _ln_bwd(dvn, xh, rs, lgv)
        accs[2][...] += dlg
        accs[3][...] += dlb
        dsv = dgv * _gelu_grad(sv, tv)
        outs[0][...] = jnp.concatenate([dsu, dsv], axis=1).astype(_MXU_DTYPE)

    return _row_call("sg_bwd" + tag, body, t_len // tm, rows=[(proj_a, tm), (dy, tm)], res=[lg, lb, sg_w, sg_bt],
                     outs=[((t_len, 2 * BW), _MXU_DTYPE, tm)],
                     accs=[(SG_GROUPS, SG_BLOCK, SG_BLOCK), (SG_BLOCK, SG_GROUPS), (1, BW), (1, BW)])


def _chunk_matrix(tm, kind):
    row = lax.broadcasted_iota(jnp.int32, (tm, tm), 0)
    col = lax.broadcasted_iota(jnp.int32, (tm, tm), 1)
    same = lax.shift_right_logical(row, 6) == lax.shift_right_logical(col, 6)
    if kind == 'cumsum':
        same = jnp.logical_and(same, row >= col)
    elif kind == 'revsum':
        same = jnp.logical_and(same, row <= col)
    return same.astype(jnp.bfloat16)


def _gla_gate(pa, wa2, ba):
    z = _mm(pa, wa2) + ba
    log_a = (jnp.minimum(z, 0.0) - jnp.log(1.0 + jnp.exp(-jnp.abs(z)))) * (1.0 / GLA_TAU)
    return z, log_a


def _gla_decay(pb, log_a):
    tm = pb.shape[0]
    cum = _mm_exact_rhs(_chunk_matrix(tm, 'cumsum'), log_a)
    tot = _mm_exact_rhs(_chunk_matrix(tm, 'total'), log_a)
    w = jnp.exp(tot - cum)
    return w, pb[:, 256:512] * w, jnp.exp(tot)


def _per_head(fn):
    return jnp.concatenate([fn(h) for h in range(GLA_HEADS)], axis=1)


def _gla_read(qs, sb, c):
    rows = slice(c * CHUNK, (c + 1) * CHUNK)
    return _per_head(lambda h: lax.dot_general(qs[rows, h * 64:(h + 1) * 64], sb[:, h * 64:(h + 1) * 64], _NT,
                                               preferred_element_type=F32))


def _gla_fwd(tag, proj_b, proj_a, wa2, ba, ng, tm=512, exchange=()):
    t_len = proj_b.shape[0]
    cpt = tm // CHUNK

    def body(t, rows, halos, res, outs, accs, scr):
        pb = rows[0][...]
        _, log_a = _gla_gate(rows[1][...], res[0][...], res[1][...])
        _, kd, dec = _gla_decay(pb, log_a)
        kdb = kd.astype(_MXU_DTYPE)
        vb = pb[:, 512:1024].astype(_MXU_DTYPE)
        qs = (pb[:, 0:256] * (GLA_DK ** -0.5)).astype(_MXU_DTYPE)
        uts = []
        for c in range(cpt):
            rs = slice(c * CHUNK, (c + 1) * CHUNK)
            uts.append(_per_head(lambda h: lax.dot_general(vb[rs, h * 128:(h + 1) * 128], kdb[rs, h * 64:(h + 1) * 64],
                                                           _TN, preferred_element_type=F32)))
        s_new = scr[0][...]
        o = []
        for c in range(cpt):
            s_new = dec[c * CHUNK:c * CHUNK + 1] * s_new + uts[c]
            outs[1][c] = s_new
            o.append(_gla_read(qs, s_new.astype(_MXU_DTYPE), c))
        scr[0][...] = s_new
        o = jnp.concatenate(o, axis=0)
        on = _per_head(lambda h: o[:, h * 128:(h + 1) * 128] * lax.rsqrt(
            jnp.mean(jnp.square(o[:, h * 128:(h + 1) * 128]), axis=-1, keepdims=True) + EPS))
        r = pb[:, 1024:1536]
        outs[0][...] = (on * res[2][...] * (r * _sigmoid(r))).astype(_MXU_DTYPE)

    return _row_call("gla_fwd" + tag, body, t_len // tm, rows=[(proj_b, tm), (proj_a, tm)], res=[wa2, ba, ng],
                     outs=[((t_len, BW), _MXU_DTYPE, tm), ((t_len // CHUNK, GLA_DV, 256), F32, cpt)],
                     scratch=[((GLA_DV, 256), F32)], exchange=exchange)


def _gla_bwd(tag, proj_b, proj_a, dy, states, wa2, ba, ng, tm=512):
    t_len = proj_b.shape[0]
    cpt = tm // CHUNK

    def body(t, rows, halos, res, outs, accs, scr):
        pb = rows[0][...]
        pa = rows[1][...]
        dyv = rows[2][...]
        st_ref = rows[3]
        wa2v = res[0][...]
        z, log_a = _gla_gate(pa, wa2v, res[1][...])
        ngv = res[2][...]
        w, kd, dec = _gla_decay(pb, log_a)
        kdb = kd.astype(_MXU_DTYPE)
        vb = pb[:, 512:1024].astype(_MXU_DTYPE)
        qs = (pb[:, 0:256] * (GLA_DK ** -0.5)).astype(_MXU_DTYPE)
        chunks = [slice(c * CHUNK, (c + 1) * CHUNK) for c in range(cpt)]
        sbs = [st_ref[c].astype(_MXU_DTYPE) for c in range(cpt)]
        o = jnp.concatenate([_gla_read(qs, sbs[c], c) for c in range(cpt)], axis=0)
        r = pb[:, 1024:1536]
        sig = _sigmoid(r)
        sil = r * sig
        dos, ons = [], []
        for h in range(GLA_HEADS):
            hs = slice(h * 128, (h + 1) * 128)
            oh = o[:, hs]
            rstd = lax.rsqrt(jnp.mean(oh * oh, axis=-1, keepdims=True) + EPS)
            on = oh * rstd
            don = dyv[:, hs] * ngv[:, hs] * sil[:, hs]
            dos.append(rstd * (don - on * jnp.mean(don * on, axis=-1, keepdims=True)))
            ons.append(on)
        on = jnp.concatenate(ons, axis=1)
        accs[2][...] += jnp.sum(dyv * on * sil, axis=0, keepdims=True)
        dr = dyv * on * ngv * (sig * (1.0 + r * (1.0 - sig)))
        dob = jnp.concatenate(dos, axis=1).astype(_MXU_DTYPE)
        reads, dqs = [], []
        for c, rs in enumerate(chunks):
            reads.append(_per_head(lambda h: lax.dot_general(dob[rs, h * 128:(h + 1) * 128], qs[rs, h * 64:(h + 1) * 64],
                                                             _TN, preferred_element_type=F32)))
            dqs.append(_per_head(lambda h: lax.dot_general(dob[rs, h * 128:(h + 1) * 128], sbs[c][:, h * 64:(h + 1) * 64],
                                                           _NN, preferred_element_type=F32)))
        dst = scr[0][...]
        dubs, ddecs = [None] * cpt, [None] * cpt
        for c in reversed(range(cpt)):
            dst_tot = dst + reads[c]
            s_prev = st_ref[c - 1] if c > 0 else jnp.where(t > 0, halos[0][0], 0.0)
            ddecs[c] = jnp.broadcast_to(jnp.sum(dst_tot * s_prev, axis=0, keepdims=True), (CHUNK, 256))
            dst = dec[c * CHUNK:c * CHUNK + 1] * dst_tot
            dubs[c] = dst_tot.astype(_MXU_DTYPE)
        scr[0][...] = dst
        dkd = jnp.concatenate(
            [_per_head(lambda h: lax.dot_general(vb[rs, h * 128:(h + 1) * 128], dubs[c][:, h * 64:(h + 1) * 64], _NN,
                                                 preferred_element_type=F32)) for c, rs in enumerate(chunks)], axis=0)
        dv = jnp.concatenate(
            [_per_head(lambda h: lax.dot_general(kdb[rs, h * 64:(h + 1) * 64], dubs[c][:, h * 64:(h + 1) * 64], _NT,
                                                 preferred_element_type=F32)) for c, rs in enumerate(chunks)], axis=0)
        e = dkd * kd
        dtot = _mm_exact_rhs(_chunk_matrix(tm, 'total'), e) + jnp.concatenate(ddecs, axis=0) * dec
        last = (lax.broadcasted_iota(jnp.int32, e.shape, 0) & (CHUNK - 1)) == CHUNK - 1
        dla = _mm_exact_rhs(_chunk_matrix(tm, 'revsum'), jnp.where(last, dtot - e, -e))
        dz = dla * (1.0 / GLA_TAU) * _sigmoid(-z)
        dzb = dz.astype(_MXU_DTYPE)
        dq = jnp.concatenate(dqs, axis=0) * (GLA_DK ** -0.5)
        da = lax.dot_general(dzb, wa2v.astype(_MXU_DTYPE), _NT, preferred_element_type=F32)
        outs[0][...] = jnp.concatenate([dq, dkd * w, dv, dr, da], axis=1).astype(_MXU_DTYPE)
        accs[0][...] += lax.dot_general(pa.astype(_MXU_DTYPE), dzb, _TN, preferred_element_type=F32)
        accs[1][...] += jnp.sum(dz, axis=0, keepdims=True)

    return _row_call("gla_bwd" + tag, body, t_len // tm,
                     rows=[(proj_b, tm), (proj_a, tm), (dy, tm), (states, cpt)],
                     halos=[(states, 1, cpt, 'prev')], res=[wa2, ba, ng],
                     outs=[((t_len, 1536 + GLA_RANK), _MXU_DTYPE, tm)],
                     accs=[(GLA_RANK, 256), (1, 256), (1, BW)], scratch=[((GLA_DV, 256), F32)], reverse=True)


ATT_TM = 256
ATT_KEYS = ATT_TM + (ATT_BAND - 1) * CHUNK


def _rel_index():
    l_idx = np.arange(CHUNK)[:, None]
    m_idx = np.arange(BAND)[None, :]
    rel = l_idx + (ATT_BAND - 1) * CHUNK - m_idx
    return jnp.asarray((np.clip(rel, -(CHUNK - 1), MAX_REL) + (CHUNK - 1)).reshape(1, CHUNK * BAND), jnp.int32)


BIAS_COLS = 4096


def _bias_expand(tag, rel_bias):
    n = CHUNK * BAND

    def kern(rel_ref, idx_ref, o_ref):
        onehot = (lax.broadcasted_iota(jnp.int32, (REL_TABLE, BIAS_COLS), 0) == idx_ref[...]).astype(jnp.bfloat16)
        o_ref[...] = _mm_exact_lhs(rel_ref[...], onehot)

    return pl.pallas_call(
        kern, grid=(n // BIAS_COLS,),
        in_specs=[pl.BlockSpec((ATT_HEADS, REL_TABLE), lambda i: (0, 0)), pl.BlockSpec((1, BIAS_COLS), lambda i: (0, i))],
        out_specs=pl.BlockSpec((ATT_HEADS, BIAS_COLS), lambda i: (0, i)),
        out_shape=jax.ShapeDtypeStruct((ATT_HEADS, n), F32), name="bias_expand" + tag)(rel_bias, _rel_index())


def _bias_tile(tag, bias):
    per = ATT_TM // CHUNK

    def kern(b_ref, o_ref):
        bv = b_ref[...]
        for j in range(per):
            parts = [jnp.full((CHUNK, j * CHUNK), NEG_INF, F32)] if j else []
            parts.append(bv)
            if j < per - 1:
                parts.append(jnp.full((CHUNK, (per - 1 - j) * CHUNK), NEG_INF, F32))
            o_ref[j * CHUNK:(j + 1) * CHUNK, :] = jnp.concatenate(parts, axis=1)

    return pl.pallas_call(
        kern, grid=(ATT_HEADS,), in_specs=[pl.BlockSpec((None, CHUNK, BAND), lambda h: (h, 0, 0))],
        out_specs=pl.BlockSpec((None, ATT_TM, ATT_KEYS), lambda h: (h, 0, 0)),
        out_shape=jax.ShapeDtypeStruct((ATT_HEADS, ATT_TM, ATT_KEYS), F32), name="bias_tile" + tag)(bias)


def _bias_untile(tag, dbias):
    per = ATT_TM // CHUNK

    def kern(d_ref, o_ref):
        acc = d_ref[0:CHUNK, 0:BAND]
        for j in range(1, per):
            acc = acc + d_ref[j * CHUNK:(j + 1) * CHUNK, j * CHUNK:j * CHUNK + BAND]
        o_ref[...] = acc

    return pl.pallas_call(
        kern, grid=(ATT_HEADS,), in_specs=[pl.BlockSpec((None, ATT_TM, ATT_KEYS), lambda h: (h, 0, 0))],
        out_specs=pl.BlockSpec((None, CHUNK, BAND), lambda h: (h, 0, 0)),
        out_shape=jax.ShapeDtypeStruct((ATT_HEADS, CHUNK, BAND), F32), name="bias_untile" + tag)(dbias)


def _bias_reduce(tag, dbias):
    n = CHUNK * BAND

    def kern(db_ref, idx_ref, o_ref):
        @pl.when(pl.program_id(0) == 0)
        def _():
            o_ref[...] = jnp.zeros(o_ref.shape, o_ref.dtype)

        onehot = (lax.broadcasted_iota(jnp.int32, (REL_TABLE, BIAS_COLS), 0) == idx_ref[...]).astype(jnp.bfloat16)
        o_ref[...] += _mm_exact_lhs(db_ref[...], onehot, _NT)

    return pl.pallas_call(
        kern, grid=(n // BIAS_COLS,),
        in_specs=[pl.BlockSpec((ATT_HEADS, BIAS_COLS), lambda i: (0, i)), pl.BlockSpec((1, BIAS_COLS), lambda i: (0, i))],
        out_specs=pl.BlockSpec((ATT_HEADS, REL_TABLE), lambda i: (0, 0)),
        out_shape=jax.ShapeDtypeStruct((ATT_HEADS, REL_TABLE), F32),
        compiler_params=pltpu.CompilerParams(dimension_semantics=("arbitrary",)),
        name="bias_reduce" + tag)(dbias, _rel_index())


def _attn_stage(t, pc_ref, p1_ref, p2_ref, kv):
    tm = ATT_TM
    kv[0:tm, :] = jnp.where(t > 1, p2_ref[:, 512:1536], 0.0).astype(kv.dtype)
    kv[tm:2 * tm, :] = jnp.where(t > 0, p1_ref[:, 512:1536], 0.0).astype(kv.dtype)
    kv[2 * tm:, :] = pc_ref[:, 512:1536].astype(kv.dtype)
    q = (pc_ref[:, 0:512] * (ATT_HD ** -0.5)).astype(_MXU_DTYPE)
    ok = lax.broadcasted_iota(jnp.int32, (tm, ATT_KEYS), 1) >= (2 - t) * tm
    return q, ok


def _attn_probs(q, kv, bias_h, ok, h):
    hs = slice(h * ATT_HD, (h + 1) * ATT_HD)
    s = lax.dot_general(q[:, hs], kv[:, hs], _NT, preferred_element_type=F32) + bias_h
    if ok is not None:
        s = jnp.where(ok, s, NEG_INF)
    e = jnp.exp(s - jnp.max(s, axis=-1, keepdims=True))
    return e * (1.0 / jnp.sum(e, axis=-1, keepdims=True))


def _attn_halos(proj_c):
    return [(proj_c, ATT_TM, 1, 'prev'), (proj_c, ATT_TM, 1, 'prev2')]


def _attn_fwd(tag, proj_c, bias, exchange=()):
    t_len = proj_c.shape[0]
    tm = ATT_TM

    def body(t, rows, halos, res, outs, accs, scr):
        b_ref, kv = res[0], scr[0]
        q, ok = _attn_stage(t, rows[0], halos[0], halos[1], kv)

        def heads(ok):
            o = []
            for h in range(ATT_HEADS):
                p = _attn_probs(q, kv, b_ref[h], ok, h).astype(_MXU_DTYPE)
                outs[1][:, h * ATT_KEYS:(h + 1) * ATT_KEYS] = p
                o.append(lax.dot_general(p, kv[:, BW + h * ATT_HD:BW + (h + 1) * ATT_HD], _NN,
                                         preferred_element_type=F32))
            outs[0][...] = jnp.concatenate(o, axis=1).astype(_MXU_DTYPE)

        pl.when(t < 2)(lambda: heads(ok))
        pl.when(t >= 2)(lambda: heads(None))

    return _row_call("attn_fwd" + tag, body, t_len // tm, rows=[(proj_c, tm)], halos=_attn_halos(proj_c),
                     res=[bias], outs=[((t_len, BW), _MXU_DTYPE, tm), ((t_len, ATT_HEADS * ATT_KEYS), _MXU_DTYPE, tm)],
                     scratch=[((ATT_KEYS, 1024), _MXU_DTYPE)], exchange=exchange)


def _attn_bwd(tag, proj_c, dy, probs, exchange=()):
    t_len = proj_c.shape[0]
    tm = ATT_TM
    scale = ATT_HD ** -0.5

    def body(t, rows, halos, res, outs, accs, scr):
        kv = scr[0]
        q, _ = _attn_stage(t, rows[0], halos[0], halos[1], kv)
        do = rows[1][...].astype(_MXU_DTYPE)
        dqs, dks, dvs = [], [], []
        for h in range(ATT_HEADS):
            hs = slice(h * ATT_HD, (h + 1) * ATT_HD)
            vs = slice(BW + h * ATT_HD, BW + (h + 1) * ATT_HD)
            pb = rows[2][:, h * ATT_KEYS:(h + 1) * ATT_KEYS]
            p = pb.astype(F32)
            dp = lax.dot_general(do[:, hs], kv[:, vs], _NT, preferred_element_type=F32)
            ds = p * (dp - jnp.sum(dp * p, axis=-1, keepdims=True))
            accs[0][h] += ds
            dsb = ds.astype(_MXU_DTYPE)
            dqs.append(lax.dot_general(dsb, kv[:, hs], _NN, preferred_element_type=F32) * scale)
            dks.append(lax.dot_general(dsb, q[:, hs], _TN, preferred_element_type=F32))
            dvs.append(lax.dot_general(pb, do[:, hs], _TN, preferred_element_type=F32))
        outs[0][...] = jnp.concatenate(dqs, axis=1).astype(_MXU_DTYPE)
        dkv = jnp.concatenate(dks + dvs, axis=1)
        after_one, after_two = scr[1], scr[2]
        outs[1][...] = (dkv[2 * tm:, :] + after_two[...]).astype(_MXU_DTYPE)
        after_two[...] = dkv[tm:2 * tm, :] + after_one[...]
        after_one[...] = dkv[0:tm, :]

    return _row_call("attn_bwd" + tag, body, t_len // tm, rows=[(proj_c, tm), (dy, tm), (probs, tm)],
                     halos=_attn_halos(proj_c),
                     outs=[((t_len, BW), _MXU_DTYPE, tm), ((t_len, 1024), _MXU_DTYPE, tm)],
                     accs=[(ATT_HEADS, ATT_TM, ATT_KEYS)],
                     scratch=[((ATT_KEYS, 1024), _MXU_DTYPE), ((tm, 1024), F32), ((tm, 1024), F32)],
                     reverse=True, exchange=exchange)


def _conv_glu(pd):
    a, g = pd[:, :BW], pd[:, BW:]
    sig = _sigmoid(g)
    return a, sig, a * sig


def _conv_stage(t, pd_ref, ph_ref, win):
    pd = pd_ref[...]
    a, sig, y0 = _conv_glu(pd)
    win[0:CONV_HALO, :] = jnp.where(t > 0, _conv_glu(ph_ref[...])[2], 0.0)
    win[CONV_HALO:CONV_HALO + pd.shape[0], :] = y0
    return a, sig


SUBLANES = 8


def _conv_shifted(win, sh):
    for b in range(SUBLANES):
        sh[b] = win[pl.ds(b, sh.shape[1]), :]


def _conv_taps_by_copy(offsets):
    groups = {}
    for j, o in enumerate(offsets):
        groups.setdefault(o % SUBLANES, []).append((j, o - o % SUBLANES))
    return [(rem, min(a for _, a in taps), max(a for _, a in taps) - min(a for _, a in taps), taps)
            for rem, taps in sorted(groups.items())]


def _conv_span(sh, rem, r0, lo, rows):
    return sh[rem, pl.ds(pl.multiple_of(r0 + lo, SUBLANES), rows), :]


def _conv_tap_sum(sh, w_ref, offsets, out_ref, init=None, rb=32):
    plan = _conv_taps_by_copy(offsets)

    def block(i, carry):
        r0 = pl.multiple_of(i * rb, rb)
        acc = jnp.zeros((rb, BW), F32) if init is None else jnp.broadcast_to(init, (rb, BW))
        for rem, lo, extra, taps in plan:
            span = _conv_span(sh, rem, r0, lo, extra + rb)
            for j, a in taps:
                acc = acc + w_ref[j:j + 1, :] * span[a - lo:a - lo + rb]
        out_ref[pl.ds(r0, rb), :] = acc
        return carry

    lax.fori_loop(0, out_ref.shape[0] // rb, block, 0)


def _conv_tap_corr(sh, d_ref, offsets, acc_ref, rb=32):
    for rem, lo, extra, taps in _conv_taps_by_copy(offsets):
        def block(i, sums, rem=rem, lo=lo, extra=extra, taps=taps):
            r0 = pl.multiple_of(i * rb, rb)
            d = d_ref[pl.ds(r0, rb), :]
            span = _conv_span(sh, rem, r0, lo, extra + rb)
            out = []
            for s, (j, a) in zip(sums, taps):
                prod = d * span[a - lo:a - lo + rb]
                for k in range(0, rb, SUBLANES):
                    s = s + prod[k:k + SUBLANES]
                out.append(s)
            return tuple(out)

        sums = lax.fori_loop(0, d_ref.shape[0] // rb, block,
                             tuple(jnp.zeros((SUBLANES, BW), F32) for _ in taps))
        for (j, _), s in zip(taps, sums):
            acc_ref[j:j + 1, :] += jnp.sum(s, axis=0, keepdims=True)


def _conv_scratch(tm):
    return [((tm + CONV_HALO + SUBLANES, BW), F32), ((SUBLANES, tm + CONV_HALO, BW), F32)]


def _conv_fwd(tag, proj_d, dw_w, dw_b, ln_g, ln_b, tm=512):
    t_len = proj_d.shape[0]
    lead = CONV_HALO - (CONV_K - 1)

    def body(t, rows, halos, res, outs, accs, scr):
        win, sh = scr
        _conv_stage(t, rows[0], halos[0], win)
        _conv_shifted(win, sh)
        _conv_tap_sum(sh, res[0], [lead + j for j in range(CONV_K)], outs[1], init=res[1][...])
        yl, _, _ = _ln_fwd(outs[1][...], res[2][...], res[3][...])
        outs[0][...] = (yl * _sigmoid(yl)).astype(_MXU_DTYPE)

    return _row_call("conv_fwd" + tag, body, t_len // tm, rows=[(proj_d, tm)],
                     halos=[(proj_d, CONV_HALO, tm // CONV_HALO, 'prev')], res=[dw_w, dw_b, ln_g, ln_b],
                     outs=[((t_len, BW), _MXU_DTYPE, tm), ((t_len, BW), F32, tm)], scratch=_conv_scratch(tm))


def _conv_bwd_norm(tag, yc, dy, ln_g, ln_b, tm=512):
    t_len = yc.shape[0]

    def body(t, rows, halos, res, outs, accs, scr):
        lgv = res[0][...]
        yl, xh, rs = _ln_fwd(rows[0][...], lgv, res[1][...])
        sig = _sigmoid(yl)
        dyl = rows[1][...] * (sig * (1.0 + yl * (1.0 - sig)))
        dyc, dlg, dlb = _ln_bwd(dyl, xh, rs, lgv)
        outs[0][...] = dyc
        accs[0][...] += dlg
        accs[1][...] += dlb
        accs[2][...] += jnp.sum(dyc, axis=0, keepdims=True)

    return _row_call("conv_bwd_norm" + tag, body, t_len // tm, rows=[(yc, tm), (dy, tm)], res=[ln_g, ln_b],
                     outs=[((t_len, BW), F32, tm)], accs=[(1, BW), (1, BW), (1, BW)])


def _conv_bwd_taps(tag, proj_d, dyc, dw_w, tm=512):
    t_len = proj_d.shape[0]
    nt = t_len // tm
    lead = CONV_HALO - (CONV_K - 1)

    def body(t, rows, halos, res, outs, accs, scr):
        win, sh, wd, shd, dy0_ref = scr
        a, sig = _conv_stage(t, rows[0], halos[0], win)
        _conv_shifted(win, sh)
        wd[0:tm, :] = rows[1][...]
        wd[tm:tm + CONV_HALO, :] = jnp.where(t < nt - 1, halos[1][...], 0.0)
        _conv_shifted(wd, shd)
        _conv_tap_corr(sh, rows[1], [lead + j for j in range(CONV_K)], accs[0])
        _conv_tap_sum(shd, res[0], [CONV_K - 1 - j for j in range(CONV_K)], dy0_ref)
        dy0 = dy0_ref[...]
        outs[0][...] = jnp.concatenate([dy0 * sig, dy0 * a * sig * (1.0 - sig)], axis=1).astype(_MXU_DTYPE)

    return _row_call("conv_bwd_taps" + tag, body, nt, rows=[(proj_d, tm), (dyc, tm)],
                     halos=[(proj_d, CONV_HALO, tm // CONV_HALO, 'prev'), (dyc, CONV_HALO, tm // CONV_HALO, 'next')],
                     res=[dw_w], outs=[((t_len, 2 * BW), _MXU_DTYPE, tm)], accs=[(CONV_K, BW)],
                     scratch=_conv_scratch(tm) + _conv_scratch(tm) + [((tm, BW), F32)])


def _merge_fwd(tag, h, xn, ys, w_gate, b_gate, w_branch, w_out, tm=512):
    t_len = h.shape[0]

    def body(t, rows, halos, res, outs, accs, scr):
        xnv = rows[1][...]
        wg_ref, bg_ref, wb_ref, wo_ref = res
        merged = jnp.zeros((tm, D_MODEL), F32)
        for n in range(4):
            cs = slice(n * D_MODEL, (n + 1) * D_MODEL)
            z = lax.dot_general(xnv, wg_ref[n], _NN, preferred_element_type=F32) + bg_ref[n:n + 1, :]
            bo = lax.dot_general(rows[2 + n][...], wb_ref[n], _NN, preferred_element_type=F32)
            outs[0][:, cs] = z.astype(_MXU_DTYPE)
            outs[1][:, cs] = bo.astype(_MXU_DTYPE)
            merged = merged + _sigmoid(z) * bo
        mb = merged.astype(_MXU_DTYPE)
        outs[2][...] = mb
        outs[3][...] = rows[0][...] + lax.dot_general(mb, wo_ref[...], _NN, preferred_element_type=F32)

    return _row_call("merge_fwd" + tag, body, t_len // tm, rows=[(h, tm), (xn, tm)] + [(y, tm) for y in ys],
                     res=[w_gate, b_gate, w_branch, w_out],
                     outs=[((t_len, 4 * D_MODEL), _MXU_DTYPE, tm), ((t_len, 4 * D_MODEL), _MXU_DTYPE, tm),
                           ((t_len, D_MODEL), _MXU_DTYPE, tm), ((t_len, D_MODEL), F32, tm)])


def _merge_bwd(tag, dh, gate_pre, bo, w_gate, w_branch, w_out, tm=256):
    t_len = dh.shape[0]

    def body(t, rows, halos, res, outs, accs, scr):
        wg_ref, wb_ref, wo_ref = res
        dhb = rows[0][...].astype(_MXU_DTYPE)
        outs[0][...] = dhb
        dmerged = lax.dot_general(dhb, wo_ref[...], _NT, preferred_element_type=F32)
        dxn = jnp.zeros((tm, D_MODEL), F32)
        dbg = []
        for n in range(4):
            cs = slice(n * D_MODEL, (n + 1) * D_MODEL)
            g = _sigmoid(rows[1][:, cs].astype(F32))
            dbo = (dmerged * g).astype(_MXU_DTYPE)
            dgp = dmerged * rows[2][:, cs].astype(F32) * (g * (1.0 - g))
            dgb = dgp.astype(_MXU_DTYPE)
            outs[1][:, cs] = dbo
            outs[2][:, cs] = dgb
            outs[4 + n][...] = lax.dot_general(dbo, wb_ref[n], _NT, preferred_element_type=F32)
            dxn = dxn + lax.dot_general(dgb, wg_ref[n], _NT, preferred_element_type=F32)
            dbg.append(jnp.sum(dgp, axis=0, keepdims=True))
        outs[3][...] = dxn
        accs[0][...] += jnp.concatenate(dbg, axis=1)

    return _row_call("merge_bwd" + tag, body, t_len // tm, rows=[(dh, tm), (gate_pre, tm), (bo, tm)],
                     res=[w_gate, w_branch, w_out],
                     outs=[((t_len, D_MODEL), _MXU_DTYPE, tm), ((t_len, 4 * D_MODEL), _MXU_DTYPE, tm),
                           ((t_len, 4 * D_MODEL), _MXU_DTYPE, tm), ((t_len, D_MODEL), F32, tm)]
                     + [((t_len, BW), F32, tm)] * 4,
                     accs=[(1, 4 * D_MODEL)])


FF_COLS = 1024


def _ffn_fwd(tag, h, g2, w1, w2, tm=512):
    t_len = h.shape[0]

    def body(t, rows, halos, res, outs, accs, scr):
        hv = rows[0][...]
        hn = (hv * _rms_stat(hv) * res[0][...]).astype(_MXU_DTYPE)
        outs[0][...] = hn
        acc = hv
        for c in range(D_FF // FF_COLS):
            cs = slice(c * FF_COLS, (c + 1) * FF_COLS)
            pre = lax.dot_general(hn, res[1][:, cs], _NN, preferred_element_type=F32)
            outs[1][:, cs] = pre
            ff = jnp.square(jnp.maximum(pre, 0.0)).astype(_MXU_DTYPE)
            acc = acc + lax.dot_general(ff, res[2][cs, :], _NN, preferred_element_type=F32)
        outs[2][...] = acc

    return _row_call("ffn_fwd" + tag, body, t_len // tm, rows=[(h, tm)], res=[g2, w1, w2],
                     outs=[((t_len, D_MODEL), _MXU_DTYPE, tm), ((t_len, D_FF), F32, tm), ((t_len, D_MODEL), F32, tm)])


def _ffn_bwd(tag, dh, h, pre, g2, w1, w2, tm=256, exchange=()):
    t_len = dh.shape[0]

    def body(t, rows, halos, res, outs, accs, scr):
        dhv = rows[0][...]
        hv = rows[1][...]
        dhb = dhv.astype(_MXU_DTYPE)
        outs[0][...] = dhb
        dhn = jnp.zeros((tm, D_MODEL), F32)
        for c in range(D_FF // FF_COLS):
            cs = slice(c * FF_COLS, (c + 1) * FF_COLS)
            r = jnp.maximum(rows[2][:, cs], 0.0)
            outs[1][:, cs] = (r * r).astype(_MXU_DTYPE)
            dpre = (lax.dot_general(dhb, res[2][cs, :], _NT, preferred_element_type=F32) * (2.0 * r)).astype(_MXU_DTYPE)
            outs[2][:, cs] = dpre
            dhn = dhn + lax.dot_general(dpre, res[1][:, cs], _NT, preferred_element_type=F32)
        dres, dg = _rms_bwd(dhn, hv, res[0][...], _rms_stat(hv))
        outs[3][...] = dhv + dres
        accs[0][...] += dg

    return _row_call("ffn_bwd" + tag, body, t_len // tm, rows=[(dh, tm), (h, tm), (pre, tm)], res=[g2, w1, w2],
                     outs=[((t_len, D_MODEL), _MXU_DTYPE, tm), ((t_len, D_FF), _MXU_DTYPE, tm),
                           ((t_len, D_FF), _MXU_DTYPE, tm), ((t_len, D_MODEL), F32, tm)],
                     accs=[(1, D_MODEL)], exchange=exchange)


def _ple_fwd(tag, h, p, g3, w_pg, b_pg, w_ple, tm=512):
    t_len = h.shape[0]

    def body(t, rows, halos, res, outs, accs, scr):
        hv = rows[0][...]
        hg = (hv * _rms_stat(hv) * res[0][...]).astype(_MXU_DTYPE)
        pb = rows[1][...].astype(_MXU_DTYPE)
        pg = _sigmoid(lax.dot_general(hg, res[1][...], _NN, preferred_element_type=F32) + res[2][...])
        pe = lax.dot_general(pb, res[3][...], _NN, preferred_element_type=F32)
        outs[0][...] = hg
        outs[1][...] = pb
        outs[2][...] = pg
        outs[3][...] = hv + pg * pe

    return _row_call("ple_fwd" + tag, body, t_len // tm, rows=[(h, tm), (p, tm)], res=[g3, w_pg, b_pg, w_ple],
                     outs=[((t_len, D_MODEL), _MXU_DTYPE, tm), ((t_len, PLE_DIM), _MXU_DTYPE, tm),
                           ((t_len, D_MODEL), F32, tm), ((t_len, D_MODEL), F32, tm)])


def _ple_bwd(tag, dh, h, pg, p_b, g3, w_pg, w_ple, tm=512):
    t_len = dh.shape[0]

    def body(t, rows, halos, res, outs, accs, scr):
        dhv = rows[0][...]
        hv = rows[1][...]
        pgv = rows[2][...]
        pe = lax.dot_general(rows[3][...], res[2][...], _NN, preferred_element_type=F32)
        dgp = dhv * pe * (pgv * (1.0 - pgv))
        dgb = dgp.astype(_MXU_DTYPE)
        outs[0][...] = dgb
        outs[1][...] = (dhv * pgv).astype(_MXU_DTYPE)
        dhg = lax.dot_general(dgb, res[1][...], _NT, preferred_element_type=F32)
        dres, dg = _rms_bwd(dhg, hv, res[0][...], _rms_stat(hv))
        outs[2][...] = dhv + dres
        accs[0][...] += jnp.sum(dgp, axis=0, keepdims=True)
        accs[1][...] += dg

    return _row_call("ple_bwd" + tag, body, t_len // tm, rows=[(dh, tm), (h, tm), (pg, tm), (p_b, tm)],
                     res=[g3, w_pg, w_ple],
                     outs=[((t_len, D_MODEL), _MXU_DTYPE, tm), ((t_len, D_MODEL), _MXU_DTYPE, tm),
                           ((t_len, D_MODEL), F32, tm)],
                     accs=[(1, D_MODEL), (1, D_MODEL)])


def _inproj_bwd(tag, dh, h, dxn_gate, dprojs, g1, w_in_t, tm=512, exchange=()):
    t_len = dh.shape[0]

    def body(t, rows, halos, res, outs, accs, scr):
        hv = rows[1][...]
        dxn = rows[2][...]
        for dp, (_, s0, n) in zip(rows[3:], IN_GROUPS_BWD):
            dxn = dxn + lax.dot_general(dp[...], res[1][s0:s0 + n, :], _NN, preferred_element_type=F32)
        dres, dg = _rms_bwd(dxn, hv, res[0][...], _rms_stat(hv))
        outs[0][...] = rows[0][...] + dres
        accs[0][...] += dg

    return _row_call("inproj_bwd" + tag, body, t_len // tm,
                     rows=[(dh, tm), (h, tm), (dxn_gate, tm)] + [(d, tm) for d in dprojs],
                     res=[g1, w_in_t], outs=[((t_len, D_MODEL), F32, tm)], accs=[(1, D_MODEL)],
                     exchange=exchange)


def _loss_head(h, target, gf, tm=512):
    t_len = h.shape[0]

    def body(t, rows, halos, res, outs, accs, scr):
        hv = rows[0][...]
        g = res[0][...]
        r = _rms_stat(hv)
        diff = hv * r * g - rows[1][...]
        accs[0][...] += 0.5 * jnp.sum(jnp.mean(diff * diff, axis=-1, keepdims=True), axis=0, keepdims=True)
        dh, dg = _rms_bwd(diff * (1.0 / D_MODEL), hv, g, r)
        outs[0][...] = dh
        accs[1][...] += dg

    return _row_call("loss_head", body, t_len // tm, rows=[(h, tm), (target, tm)], res=[gf],
                     outs=[((t_len, D_MODEL), F32, tm)], accs=[(1, 128), (1, D_MODEL)])


def _row(v):
    return v.reshape(1, -1)


GATHER_DURING = (('inproj', ('w_gate', 'w_branch', 'w_out')), ('gla', ('w_ff1',)), ('attn', ('w_ff2', 'w_ple_gate', 'w_ple')))
SCATTER_DURING_ATTN = ('w_ple_gate', 'w_ple', 'w_ff1', 'w_ff2', 'w_out', 'w_gate', 'w_branch')


def _gather_items(shards, names):
    items = []
    for n in names:
        s = shards[n]
        ax = SHARD_AXIS[n] - 1
        if n == 'w_in':
            items.append(_gather_item(s))
        else:
            items.append(_gather_item(s, s.shape[:ax] + (N_DEV * s.shape[ax],) + s.shape[ax + 1:], ax))
    return items


def _land(w, names, arrays):
    for n, a in zip(names, arrays):
        w[n] = a.reshape(IN_COLS, D_MODEL) if n == 'w_in' else a


def _layer_fwd(i, h, p_i, w, shards, next_shards):
    tag = "_l%d" % i
    during = dict(GATHER_DURING)
    res = _inproj_fwd(tag, h, _row(w['norm1_g']), w['w_in'], exchange=_gather_items(shards, during['inproj']))
    xn, pa, pb, pr, pc, pd = res[:6]
    _land(w, during['inproj'], res[6:])
    sg_bt = w['sg_b'].T
    y_a = _sg_fwd(tag, pa, _row(w['sg_ln_g']), _row(w['sg_ln_b']), w['sg_w'], sg_bt)
    res = _gla_fwd(tag, pb, pr, w['gla_w_a2'], _row(w['gla_b_a']), _row(w['gla_norm_g']),
                   exchange=_gather_items(shards, during['gla']))
    y_b, states = res[:2]
    _land(w, during['gla'], res[2:])
    bias = _bias_tile(tag, _bias_expand(tag, w['att_rel_bias']).reshape(ATT_HEADS, CHUNK, BAND))
    items = _gather_items(shards, during['attn']) + (_gather_items(next_shards, ['w_in']) if next_shards else [])
    res = _attn_fwd(tag, pc, bias, exchange=items)
    y_c, probs = res[:2]
    _land(w, during['attn'], res[2:2 + len(during['attn'])])
    next_w_in = res[-1].reshape(IN_COLS, D_MODEL) if next_shards else None
    y_d, yc = _conv_fwd(tag, pd, w['conv_dw_w'], _row(w['conv_dw_b']), _row(w['conv_ln_g']), _row(w['conv_ln_b']))
    ys = (y_a, y_b, y_c, y_d)
    gate, bo, merged, h1 = _merge_fwd(tag, h, xn, ys, w['w_gate'], w['b_gate'], w['w_branch'], w['w_out'])
    hn, pre, h2 = _ffn_fwd(tag, h1, _row(w['norm2_g']), w['w_ff1'], w['w_ff2'])
    hg, p_b, pg, h3 = _ple_fwd(tag, h2, p_i, _row(w['norm3_g']), w['w_ple_gate'], _row(w['b_ple_gate']), w['w_ple'])
    saved = dict(h=h, xn=xn, pa=pa, pb=pb, pr=pr, pc=pc, pd=pd, states=states, probs=probs, yc=yc, ys=ys, gate=gate,
                 bo=bo, merged=merged, h1=h1, hn=hn, pre=pre, h2=h2, hg=hg, p_b=p_b, pg=pg, sg_bt=sg_bt)
    return h3, saved, next_w_in


def _layer_bwd(i, dh3, s, w, tail=None):
    tag = "_l%d" % i
    g = {}
    dgp, dpe, dh2, db_pg, dg3 = _ple_bwd(tag, dh3, s['h2'], s['pg'], s['p_b'], _row(w['norm3_g']), w['w_ple_gate'],
                                         w['w_ple'])
    g['b_ple_gate'], g['norm3_g'] = db_pg[0], dg3[0]
    g['w_ple_gate'] = _tn_call("dw_ple_gate" + tag, s['hg'], dgp, D_MODEL, D_MODEL)
    g['w_ple'] = _tn_call("dw_ple" + tag, s['p_b'], dpe, PLE_DIM, D_MODEL)

    dh2b, ffb, dpre, dh1, dg2 = _ffn_bwd(tag, dh2, s['h1'], s['pre'], _row(w['norm2_g']), w['w_ff1'], w['w_ff2'])
    g['norm2_g'] = dg2[0]
    g['w_ff1'] = _tn_call("dw_ff1" + tag, s['hn'], dpre, D_MODEL, FF_COLS, nblk=D_FF // FF_COLS)
    g['w_ff2'] = _tn_call("dw_ff2" + tag, ffb, dh2b, FF_COLS, D_MODEL, nblk=D_FF // FF_COLS, a_col=True, b_col=False,
                          out='rows')

    dh1b, dbo, dgpre, dxn_gate, dy_a, dy_b, dy_c, dy_d, db_gate = _merge_bwd(
        tag, dh1, s['gate'], s['bo'], w['w_gate'], w['w_branch'], w['w_out'])
    g['b_gate'] = db_gate.reshape(4, D_MODEL)
    g['w_out'] = _tn_call("dw_out" + tag, s['merged'], dh1b, D_MODEL, D_MODEL)
    g['w_gate'] = _tn_call("dw_gate" + tag, s['xn'], dgpre, D_MODEL, D_MODEL, nblk=4, out='stack')
    g['w_branch'] = jnp.stack([_tn_call("dw_branch%d%s" % (n, tag), s['ys'][n], dbo, BW, D_MODEL, b_off=n)
                             for n in range(4)])

    lg, lb = _row(w['sg_ln_g']), _row(w['sg_ln_b'])
    dpa, dsg_w, dsg_bt, dlg, dlb = _sg_bwd(tag, s['pa'], dy_a, lg, lb, w['sg_w'], s['sg_bt'])
    g['sg_w'], g['sg_b'], g['sg_ln_g'], g['sg_ln_b'] = dsg_w, dsg_bt.T, dlg[0], dlb[0]

    dpb, dwa2, dba, dng = _gla_bwd(tag, s['pb'], s['pr'], dy_b, s['states'], w['gla_w_a2'],
                                   _row(w['gla_b_a']), _row(w['gla_norm_g']))
    g['gla_w_a2'], g['gla_b_a'], g['gla_norm_g'] = dwa2, dba[0], dng[0]

    items = [_scatter_item(g.pop(n), axis=SHARD_AXIS[n] - 1) for n in SCATTER_DURING_ATTN]
    res = _attn_bwd(tag, s['pc'], dy_c, s['probs'], exchange=items)
    dq, dkv, dbias = res[:3]
    parts = dict(zip(SCATTER_DURING_ATTN, res[3:]))
    g['att_rel_bias'] = _bias_reduce(tag, _bias_untile(tag, dbias).reshape(ATT_HEADS, CHUNK * BAND))

    cg, cb = _row(w['conv_ln_g']), _row(w['conv_ln_b'])
    dyc, dcg, dcb, ddwb = _conv_bwd_norm(tag, s['yc'], dy_d, cg, cb)
    dpd, ddw = _conv_bwd_taps(tag, s['pd'], dyc, w['conv_dw_w'])
    g['conv_ln_g'], g['conv_ln_b'], g['conv_dw_b'], g['conv_dw_w'] = dcg[0], dcb[0], ddwb[0], ddw

    dprojs = (dpa, dpb, dq, dkv, dpd)
    dw_in = jnp.concatenate([_tn_call("dw_in%s%s" % (name, tag), dp, s['xn'], n, D_MODEL)
                             for (name, _, n), dp in zip(IN_GROUPS_BWD, dprojs)], axis=0)
    items = [_scatter_item(dw_in.reshape(N_DEV, IN_COLS // N_DEV, D_MODEL))] + (tail(g) if tail else [])
    res = _inproj_bwd(tag, dh1, s['h'], dxn_gate, dprojs, _row(w['norm1_g']), w['w_in'], exchange=items)
    dh0, dg1 = res[:2]
    g['norm1_g'] = dg1[0]
    parts['w_in'] = res[2]
    return dh0, g, parts, res[3:]


def _local_step(x, p, target, final_g, layers, shards, tail):
    h = x
    saved = []
    for i in range(DEPTH):
        nxt = shards[i + 1] if i + 1 < DEPTH else None
        h, s, next_w_in = _layer_fwd(i, h, p[i], layers[i], shards[i], nxt)
        saved.append(s)
        if nxt:
            layers[i + 1]['w_in'] = next_w_in
    dh, loss, dgf = _loss_head(h, target, _row(final_g))
    small, parts, tail_out = [None] * DEPTH, [None] * DEPTH, None
    for i in reversed(range(DEPTH)):
        hook = (lambda g: tail([g] + small[1:])) if i == 0 else None
        dh, small[i], parts[i], out = _layer_bwd(i, dh, saved[i], layers[i], hook)
        if i == 0:
            tail_out = out
    return loss[0, 0], dh, dgf[0], small, parts, tail_out


def _peers():
    x, y, c = lax.axis_index("x"), lax.axis_index("y"), lax.axis_index("c")
    me = 4 * x + 2 * y + c
    out = []
    for k in range(1, N_DEV):
        px = (1 - x) if k & 4 else x
        py = (1 - y) if k & 2 else y
        pc = (1 - c) if k & 1 else c
        out.append((k - 1, (px, py, pc), 4 * px + 2 * py + pc))
    return me, out


def _block(ref, axis, idx, width):
    ix = [slice(None)] * len(ref.shape)
    ix[axis] = pl.ds(pl.multiple_of(idx * width, width), width)
    return ref.at[tuple(ix)]


def _slot(ref, idx):
    return ref.at[idx]


def _whole(ref, idx):
    return ref


def _gather_item(src, out_shape=None, axis=None):
    if axis is None:
        return dict(src=src, out=(N_DEV,) + src.shape, take=_whole, put=_slot)
    return dict(src=src, out=tuple(out_shape), take=_whole,
                put=lambda ref, s: _block(ref, axis, s, src.shape[axis]))


def _scatter_item(src, axis=None, lead=0):
    if axis is None:
        shape = src.shape[:lead] + src.shape[lead + 1:]
        take = lambda ref, s: ref.at[(slice(None),) * lead + (s,)]
    else:
        width = src.shape[axis] // N_DEV
        shape = src.shape[:axis] + (width,) + src.shape[axis + 1:]
        take = lambda ref, s: _block(ref, axis, s, width)
    return dict(src=src, out=(N_DEV,) + shape, take=take, put=_slot)


def _exchange_sems(n):
    return [pltpu.SemaphoreType.DMA((n * (N_DEV - 1),)), pltpu.SemaphoreType.DMA((n * (N_DEV - 1),)),
            pltpu.SemaphoreType.DMA((n,))]


def _exchange_copies(items, src_refs, out_refs, sems, start):
    send_sems, recv_sems, local_sems = sems
    me, peers = _peers()

    def remote(i, k, pos, receiver, sender):
        it = items[i]
        return pltpu.make_async_remote_copy(
            src_ref=it['take'](src_refs[i], receiver), dst_ref=it['put'](out_refs[i], sender),
            send_sem=send_sems.at[i * (N_DEV - 1) + k], recv_sem=recv_sems.at[i * (N_DEV - 1) + k],
            device_id=pos, device_id_type=pl.DeviceIdType.MESH)

    local = [pltpu.make_async_copy(it['take'](src_refs[i], me), it['put'](out_refs[i], me), local_sems.at[i])
             for i, it in enumerate(items)]
    if start:
        for cp in local:
            cp.start()
        for k, pos, flat in peers:
            for i in range(len(items)):
                remote(i, k, pos, flat, me).start()
    else:
        for k, pos, flat in peers:
            for i in range(len(items)):
                remote(i, k, pos, flat, flat).wait_recv()
        for k, pos, flat in peers:
            for i in range(len(items)):
                remote(i, k, pos, flat, me).wait_send()
        for cp in local:
            cp.wait()


def _gather_via_sibling(name, srcs):
    n = len(srcs)

    def body(*refs):
        src_refs, out_refs = refs[:n], refs[n:2 * n]
        send_sems, recv_sems, local_sems = refs[2 * n:]
        x, y, c = lax.axis_index("x"), lax.axis_index("y"), lax.axis_index("c")
        flat = lambda px, py, pc: 4 * px + 2 * py + pc
        me, sibling = (x, y, c), (x, y, 1 - c)
        chips = [(1 - x, y), (x, 1 - y), (1 - x, 1 - y)]

        def copy(i, k, block, to, own):
            return pltpu.make_async_remote_copy(
                src_ref=src_refs[i] if own else out_refs[i].at[flat(*block)], dst_ref=out_refs[i].at[flat(*block)],
                send_sem=send_sems.at[i * (N_DEV - 1) + k], recv_sem=recv_sems.at[i * (N_DEV - 1) + k],
                device_id=to, device_id_type=pl.DeviceIdType.MESH)

        local, sent = [], []
        for i in range(n):
            local.append(pltpu.make_async_copy(src_refs[i], out_refs[i].at[flat(*me)], local_sems.at[i]))
            local[-1].start()
            sent.append(copy(i, 0, me, sibling, True))
            sent += [copy(i, 1 + j, me, (*chip, c), True) for j, chip in enumerate(chips)]
            for cp in sent[-4:]:
                cp.start()
        for i in range(n):
            for j, chip in enumerate(chips):
                copy(i, 1 + j, (*chip, c), me, True).wait_recv()
                sent.append(copy(i, 4 + j, (*chip, c), sibling, False))
                sent[-1].start()
        for i in range(n):
            copy(i, 0, sibling, me, True).wait_recv()
            for j, chip in enumerate(chips):
                copy(i, 4 + j, (*chip, 1 - c), me, False).wait_recv()
        for cp in sent:
            cp.wait_send()
        for cp in local:
            cp.wait()

    any_spec = pl.BlockSpec(memory_space=pl.ANY)
    return pl.pallas_call(
        body, out_shape=[jax.ShapeDtypeStruct((N_DEV,) + s.shape, s.dtype) for s in srcs],
        in_specs=[any_spec] * n, out_specs=[any_spec] * n, scratch_shapes=_exchange_sems(n), name=name)(*srcs)


def _pack(arrays, dtype, lead=None):
    flat = [a.astype(dtype).reshape((lead, -1) if lead else (-1,)) for a in arrays]
    cat = jnp.concatenate(flat, axis=-1)
    n = cat.shape[-1]
    rows = -(-n // (PACK_COLS * SUBLANES)) * SUBLANES
    pad = rows * PACK_COLS - n
    if pad:
        cat = jnp.pad(cat, ((0, 0), (0, pad)) if lead else ((0, pad),))
    return cat.reshape((lead, rows, PACK_COLS) if lead else (rows, PACK_COLS))


def _unpack(buf, shapes, lead=None):
    flat = buf.reshape((lead, -1) if lead else (-1,))
    out, off = [], 0
    for shp in shapes:
        n = int(np.prod(shp))
        piece = flat[..., off:off + n]
        out.append(piece.reshape(((lead,) if lead else ()) + tuple(shp)))
        off += n
    return out


def _to_slabs(full, axis):
    shp = full.shape
    split = full.reshape(shp[:axis] + (N_DEV, shp[axis] // N_DEV) + shp[axis + 1:])
    return jnp.moveaxis(split, axis, 0)


def _from_slabs(slabs, axis):
    moved = jnp.moveaxis(slabs, 0, axis)
    shp = moved.shape
    return moved.reshape(shp[:axis] + (shp[axis] * shp[axis + 1],) + shp[axis + 2:])


def _adamw_block(r, c):
    if r % 8:
        return r, 256
    br = min(r, max(8, ADAMW_TILE * PACK_COLS // c))
    while r % br:
        br //= 2
    return br, c


def _adamw(name, partials, w, m, v):
    n_lead, r, c = w.shape
    br, bc = _adamw_block(r, c)
    ni, nj = r // br, c // bc
    c1 = 1.0 - ADAM_B1 ** ADAM_STEP
    c2 = 1.0 - ADAM_B2 ** ADAM_STEP

    def kern(*refs):
        p_refs = refs[:n_lead]
        w_ref, m_ref, v_ref, g_ref, d_ref, nm_ref, nv_ref = refs[n_lead:]
        layer = pl.program_id(0)
        g = None
        for l, p_ref in enumerate(p_refs):
            gl = p_ref[0].astype(F32)
            for s in range(1, N_DEV):
                gl = gl + p_ref[s].astype(F32)
            g = gl if g is None else jnp.where(layer == l, gl, g)
        nm = ADAM_B1 * m_ref[...] + (1.0 - ADAM_B1) * g
        nv = ADAM_B2 * v_ref[...] + (1.0 - ADAM_B2) * jnp.square(g)
        g_ref[...] = g
        nm_ref[...] = nm
        nv_ref[...] = nv
        d_ref[...] = -ADAM_LR * ((nm / c1) / (jnp.sqrt(nv / c2) + ADAM_EPS) + ADAM_WD * w_ref[...])

    def part_spec(mine):
        def index(l, i, j):
            before, after = l < mine, l > mine
            return (0, jnp.where(before, 0, jnp.where(after, ni - 1, i)), jnp.where(before, 0, jnp.where(after, nj - 1, j)))
        return pl.BlockSpec((N_DEV, br, bc), index)

    blk = pl.BlockSpec((None, br, bc), lambda l, i, j: (l, i, j))
    return pl.pallas_call(
        kern, grid=(n_lead, ni, nj),
        in_specs=[part_spec(l) for l in range(n_lead)] + [blk, blk, blk],
        out_specs=[blk] * 4, out_shape=[jax.ShapeDtypeStruct(w.shape, F32)] * 4,
        compiler_params=pltpu.CompilerParams(dimension_semantics=("arbitrary",) * 3),
        name=name)(*partials, w, m, v)


def _as_rows(a, lead):
    return a.reshape(a.shape[:lead] + (-1, a.shape[-1]))


def kernel(x, p, norm1_g, w_in, sg_ln_g, sg_ln_b, sg_w, sg_b, gla_w_a2, gla_b_a, gla_norm_g, att_rel_bias, conv_dw_w, conv_dw_b, conv_ln_g, conv_ln_b, w_branch, w_gate, b_gate, w_out, norm2_g, w_ff1, w_ff2, norm3_g, w_ple_gate, b_ple_gate, w_ple, final_g, loss_target, m_norm1_g, m_w_in, m_sg_ln_g, m_sg_ln_b, m_sg_w, m_sg_b, m_gla_w_a2, m_gla_b_a, m_gla_norm_g, m_att_rel_bias, m_conv_dw_w, m_conv_dw_b, m_conv_ln_g, m_conv_ln_b, m_w_branch, m_w_gate, m_b_gate, m_w_out, m_norm2_g, m_w_ff1, m_w_ff2, m_norm3_g, m_w_ple_gate, m_b_ple_gate, m_w_ple, m_final_g, v_norm1_g, v_w_in, v_sg_ln_g, v_sg_ln_b, v_sg_w, v_sg_b, v_gla_w_a2, v_gla_b_a, v_gla_norm_g, v_att_rel_bias, v_conv_dw_w, v_conv_dw_b, v_conv_ln_g, v_conv_ln_b, v_w_branch, v_w_gate, v_b_gate, v_w_out, v_norm2_g, v_w_ff1, v_w_ff2, v_norm3_g, v_w_ple_gate, v_b_ple_gate, v_w_ple, v_final_g):
    args = locals()
    wts = {n: args[n] for n in WEIGHTS}
    mom = {n: args['m_' + n] for n in WEIGHTS}
    var = {n: args['v_' + n] for n in WEIGHTS}

    local = {d_name: dict(d, w_in=jnp.swapaxes(d['w_in'], 1, 2))
             for d_name, d in (("w", wts), ("m", mom), ("v", var))}
    shards = [{n: local["w"][n][i].astype(_MXU_DTYPE) for n in MXU_WEIGHTS} for i in range(DEPTH)]

    first_w_in, vec = _gather_via_sibling("gather_first_weights",
                                          [shards[0]['w_in'], _pack([wts[n] for n in VEC_WEIGHTS], F32)])
    vec_full = {n: _from_slabs(slabs, SHARD_AXIS[n])
                for n, slabs in zip(VEC_WEIGHTS, _unpack(vec, [wts[n].shape for n in VEC_WEIGHTS], lead=N_DEV))}
    small_names = [n for n in WEIGHTS if n not in MXU_WEIGHTS and n != 'final_g']
    layers = [{n: (vec_full[n] if n in vec_full else wts[n])[i] for n in small_names} for i in range(DEPTH)]
    _land(layers[0], ['w_in'], [first_w_in])

    def vec_items(small):
        stacked = [jnp.stack([g[n] for g in small]) for n in VEC_WEIGHTS]
        return [_scatter_item(_pack([_to_slabs(a, SHARD_AXIS[n]) for n, a in zip(VEC_WEIGHTS, stacked)], F32,
                                    lead=N_DEV))]

    seq = x.shape[1:]
    loss, grad_x, dgf, small, parts, (vec_parts,) = _local_step(
        x.reshape(seq), p.reshape(DEPTH, seq[0], PLE_DIM), loss_target.reshape(seq), final_g, layers, shards, vec_items)
    loss = lax.psum(loss, ("x", "y", "c"))

    grads = {n: jnp.stack([small[i][n] for i in range(DEPTH)]) for n in REPLICATED if n != 'final_g'}
    grads['final_g'] = dgf
    repl_parts, = _gather_via_sibling("gather_replicated_grads", [_pack([grads[n] for n in REPLICATED], F32)])

    results = {}
    for n in MXU_WEIGHTS:
        w3, m3, v3 = (_as_rows(local[d][n], 1) for d in ("w", "m", "v"))
        outs = _adamw("adamw_" + n, [parts[i][n].reshape((N_DEV,) + w3.shape[1:]) for i in range(DEPTH)], w3, m3, v3)
        for kind, a in zip(("grad", "delta", "new_m", "new_v"), outs):
            a = a.reshape(local["w"][n].shape)
            results[kind, n] = jnp.swapaxes(a, 1, 2) if n == 'w_in' else a
    for names, part, call in ((VEC_WEIGHTS, vec_parts, "adamw_vec"), (REPLICATED, repl_parts, "adamw_replicated")):
        packed = [_pack([d[n] for n in names], F32)[None] for d in (wts, mom, var)]
        outs = _adamw(call, [part], *packed)
        for kind, buf in zip(("grad", "delta", "new_m", "new_v"), outs):
            for n, a in zip(names, _unpack(buf[0], [wts[n].shape for n in names])):
                results[kind, n] = a
    return (loss, grad_x[None]) + tuple(results[kind, n] for kind in ("grad", "delta", "new_m", "new_v")
                                        for n in WEIGHTS)
```

```python
import functools

import numpy as np
import jax
import jax.numpy as jnp
from jax import lax
from jax.experimental import pallas as pl
from jax.experimental.pallas import tpu as pltpu

F32 = jnp.float32
_MXU_DTYPE = jnp.bfloat16
GRAD_DTYPE = jnp.bfloat16

N_DEV = 8
D_MODEL = 1024
DEPTH = 2
CHUNK = 64
PLE_DIM = 256
BW = 512
SG_BLOCK = 128
SG_GROUPS = 4
GLA_HEADS = 4
GLA_DK = 64
GLA_DV = 128
GLA_RANK = 16
GLA_TAU = 16.0
ATT_HEADS = 8
ATT_HD = 64
ATT_BAND = 9
BAND = ATT_BAND * CHUNK
MAX_REL = 256
REL_TABLE = CHUNK + MAX_REL
CONV_K = 31
CONV_HALO = 32
D_FF = 4096
EPS = 1e-6
NEG_INF = -1e30

IN_GROUPS = (("A", 0, 1024), ("B", 1024, 1536), ("a", 2560, 16), ("C", 2576, 1536), ("D", 4112, 1024))
IN_GROUPS_BWD = (IN_GROUPS[0], ("Ba", 1024, 1552), ("Cq", 2576, 512), ("Ckv", 3088, 1024), IN_GROUPS[4])
IN_COLS = 5136

ADAM_LR = 0.001
ADAM_B1 = 0.9
ADAM_B2 = 0.999
ADAM_EPS = 1e-08
ADAM_WD = 0.01
ADAM_STEP = 10

ADAMW_TILE = 128
PACK_COLS = 1024
VMEM_LIMIT_MB = 56

_NN = (((1,), (0,)), ((), ()))
_NT = (((1,), (1,)), ((), ()))
_TN = (((0,), (0,)), ((), ()))

WEIGHTS = ['norm1_g', 'w_in', 'sg_ln_g', 'sg_ln_b', 'sg_w', 'sg_b', 'gla_w_a2', 'gla_b_a', 'gla_norm_g',
           'att_rel_bias', 'conv_dw_w', 'conv_dw_b', 'conv_ln_g', 'conv_ln_b', 'w_branch', 'w_gate', 'b_gate',
           'w_out', 'norm2_g', 'w_ff1', 'w_ff2', 'norm3_g', 'w_ple_gate', 'b_ple_gate', 'w_ple', 'final_g']
SHARD_AXIS = {'w_in': 2, 'gla_w_a2': 2, 'att_rel_bias': 2, 'conv_dw_w': 2, 'w_branch': 3, 'w_gate': 2,
              'b_gate': 2, 'w_out': 1, 'w_ff1': 2, 'w_ff2': 1, 'w_ple_gate': 1, 'w_ple': 2}
MXU_WEIGHTS = ('w_in', 'w_branch', 'w_gate', 'w_out', 'w_ff1', 'w_ff2', 'w_ple_gate', 'w_ple')
VEC_WEIGHTS = ('gla_w_a2', 'att_rel_bias', 'conv_dw_w', 'b_gate')
SHARDED = tuple(n for n in WEIGHTS if n in SHARD_AXIS)
REPLICATED = tuple(n for n in WEIGHTS if n not in SHARD_AXIS)


def _mm(a, b, dims=_NN):
    return lax.dot_general(a.astype(_MXU_DTYPE), b.astype(_MXU_DTYPE), dims, preferred_element_type=F32)


def _split3(x):
    x1 = x.astype(jnp.bfloat16)
    r1 = x - x1.astype(F32)
    x2 = r1.astype(jnp.bfloat16)
    x3 = (r1 - x2.astype(F32)).astype(jnp.bfloat16)
    return x1, x2, x3


def _mm_exact_rhs(m, x, dims=_NN):
    return sum(lax.dot_general(m, xi, dims, preferred_element_type=F32) for xi in _split3(x))


def _mm_exact_lhs(x, m, dims=_NN):
    return sum(lax.dot_general(xi, m, dims, preferred_element_type=F32) for xi in _split3(x))


def _sigmoid(x):
    return 1.0 / (1.0 + jnp.exp(-x))


def _gelu(x):
    c = 0.7978845608028654
    t = jnp.tanh(c * (x + 0.044715 * x * x * x))
    return 0.5 * x * (1.0 + t), t


def _gelu_grad(x, t):
    c = 0.7978845608028654
    return 0.5 * (1.0 + t) + 0.5 * x * (1.0 - t * t) * c * (1.0 + 3.0 * 0.044715 * x * x)


def _rms_stat(h):
    return lax.rsqrt(jnp.mean(h * h, axis=-1, keepdims=True) + EPS)


def _rms_bwd(dy, h, g, r):
    hh = h * r
    dhh = dy * g
    dh = r * (dhh - hh * jnp.mean(dhh * hh, axis=-1, keepdims=True))
    return dh, jnp.sum(dy * hh, axis=0, keepdims=True)


def _ln_fwd(x, g, b):
    mu = jnp.mean(x, axis=-1, keepdims=True)
    xc = x - mu
    rs = lax.rsqrt(jnp.mean(xc * xc, axis=-1, keepdims=True) + EPS)
    xh = xc * rs
    return xh * g + b, xh, rs


def _ln_bwd(dy, xh, rs, g):
    dxh = dy * g
    dx = rs * (dxh - jnp.mean(dxh, axis=-1, keepdims=True) - xh * jnp.mean(dxh * xh, axis=-1, keepdims=True))
    return dx, jnp.sum(dy * xh, axis=0, keepdims=True), jnp.sum(dy, axis=0, keepdims=True)


def _row_call(name, body, nt, rows=(), halos=(), res=(), outs=(), accs=(), scratch=(), reverse=False, exchange=()):
    def pos(i):
        return (nt - 1 - i) if reverse else i

    def lead(ndim, f):
        return lambda i: (f(pos(i)),) + (0,) * (ndim - 1)

    in_specs, operands = [], []
    for a, tile in rows:
        in_specs.append(pl.BlockSpec((tile,) + a.shape[1:], lead(a.ndim, lambda t: t)))
        operands.append(a)
    for a, blk, per, side in halos:
        last = a.shape[0] // blk - 1
        delta = {'prev2': -2, 'prev': -1, 'next': per}[side]
        f = lambda t, per=per, last=last, delta=delta: jnp.clip(t * per + delta, 0, last)
        in_specs.append(pl.BlockSpec((blk,) + a.shape[1:], lead(a.ndim, f)))
        operands.append(a)
    for a in res:
        in_specs.append(pl.BlockSpec(a.shape, lambda i, nd=a.ndim: (0,) * nd, pipeline_mode=pl.Buffered(1)))
        operands.append(a)
    out_specs, out_shape = [], []
    for shape, dtype, tile in outs:
        out_specs.append(pl.BlockSpec((tile,) + tuple(shape[1:]), lead(len(shape), lambda t: t)))
        out_shape.append(jax.ShapeDtypeStruct(tuple(shape), dtype))
    for shape in accs:
        out_specs.append(pl.BlockSpec(tuple(shape), lambda i, nd=len(shape): (0,) * nd))
        out_shape.append(jax.ShapeDtypeStruct(tuple(shape), F32))
    nx = len(exchange)
    any_spec = pl.BlockSpec(memory_space=pl.ANY)
    for it in exchange:
        in_specs.append(any_spec)
        operands.append(it['src'])
        out_specs.append(any_spec)
        out_shape.append(jax.ShapeDtypeStruct(it['out'], it['src'].dtype))
    sizes = (len(rows), len(halos), len(res), nx, len(outs), len(accs), nx, len(scratch), 3 if nx else 0)

    def kern(*refs):
        i = pl.program_id(0)
        groups, at = [], 0
        for n in sizes:
            groups.append(refs[at:at + n])
            at += n
        row_refs, halo_refs, res_refs, x_src, out_refs, acc_refs, x_dst, scr_refs, sems = groups

        @pl.when(i == 0)
        def _():
            for r in tuple(acc_refs) + tuple(scr_refs):
                r[...] = jnp.zeros(r.shape, r.dtype)
            if nx:
                _exchange_copies(exchange, x_src, x_dst, sems, start=True)

        body(pos(i), row_refs, halo_refs, res_refs, out_refs, acc_refs, scr_refs)

        if nx:
            @pl.when(i == nt - 1)
            def _():
                _exchange_copies(exchange, x_src, x_dst, sems, start=False)

    result = pl.pallas_call(
        kern, grid=(nt,), in_specs=in_specs, out_specs=out_specs, out_shape=out_shape,
        scratch_shapes=[pltpu.VMEM(tuple(s), d) for s, d in scratch] + (_exchange_sems(nx) if nx else []),
        compiler_params=pltpu.CompilerParams(dimension_semantics=("arbitrary",),
                                             vmem_limit_bytes=VMEM_LIMIT_MB << 20),
        name=name)(*operands)
    return tuple(result)


def _tn_call(name, a, b, k, n, nblk=1, a_col=False, b_col=True, b_off=0, out='cols', tile=2048):
    tile = min(tile, a.shape[0])
    nt = a.shape[0] // tile
    if out == 'cols':
        o_shape, o_spec = (k, nblk * n), pl.BlockSpec((k, n), lambda j, t: (0, j))
    elif out == 'rows':
        o_shape, o_spec = (nblk * k, n), pl.BlockSpec((k, n), lambda j, t: (j, 0))
    else:
        o_shape, o_spec = (nblk, k, n), pl.BlockSpec((None, k, n), lambda j, t: (j, 0, 0))

    def kern(a_ref, b_ref, o_ref, acc):
        @pl.when(pl.program_id(1) == 0)
        def _():
            acc[...] = jnp.zeros(acc.shape, acc.dtype)

        acc[...] += lax.dot_general(a_ref[...], b_ref[...], _TN, preferred_element_type=F32)

        @pl.when(pl.program_id(1) == nt - 1)
        def _():
            o_ref[...] = acc[...].astype(o_ref.dtype)

    return pl.pallas_call(
        kern, grid=(nblk, nt),
        in_specs=[pl.BlockSpec((tile, k), (lambda j, t: (t, j)) if a_col else (lambda j, t: (t, 0))),
                  pl.BlockSpec((tile, n), (lambda j, t: (t, j + b_off)) if b_col else (lambda j, t: (t, b_off)))],
        out_specs=o_spec, out_shape=jax.ShapeDtypeStruct(o_shape, GRAD_DTYPE),
        scratch_shapes=[pltpu.VMEM((k, n), F32)],
        compiler_params=pltpu.CompilerParams(dimension_semantics=("arbitrary", "arbitrary"),
                                             vmem_limit_bytes=VMEM_LIMIT_MB << 20),
        name=name)(a, b)


def _inproj_fwd(tag, h, g1, w_in_t, tm=512, exchange=()):
    t_len = h.shape[0]

    def body(t, rows, halos, res, outs, accs, scr):
        hv = rows[0][...]
        xn = (hv * _rms_stat(hv) * res[0][...]).astype(_MXU_DTYPE)
        outs[0][...] = xn
        for o, (_, s0, n) in zip(outs[1:], IN_GROUPS):
            o[...] = lax.dot_general(xn, res[1][s0:s0 + n, :], _NT, preferred_element_type=F32)

    outs = [((t_len, D_MODEL), _MXU_DTYPE, tm)] + [((t_len, n), F32, tm) for _, _, n in IN_GROUPS]
    return _row_call("inproj_fwd" + tag, body, t_len // tm, rows=[(h, tm)], res=[g1, w_in_t], outs=outs,
                     exchange=exchange)


def _sg_mask():
    row = lax.broadcasted_iota(jnp.int32, (SG_BLOCK, SG_BLOCK), 0)
    col = lax.broadcasted_iota(jnp.int32, (SG_BLOCK, SG_BLOCK), 1)
    return jnp.logical_or(row >= CHUNK, col < CHUNK)


def _sg_forward_parts(pa, lg, lb, w_ref, bt):
    tm = pa.shape[0]
    nb = tm // SG_BLOCK
    su, sv = pa[:, :BW], pa[:, BW:]
    u, tu = _gelu(su)
    gv, tv = _gelu(sv)
    vn, xh, rs = _ln_fwd(gv, lg, lb)
    mask = _sg_mask()
    wms, xs, ms = [], [], []
    for g in range(SG_GROUPS):
        wm = jnp.where(mask, w_ref[g], 0.0).astype(_MXU_DTYPE)
        xg = jnp.concatenate([vn[b * SG_BLOCK:(b + 1) * SG_BLOCK, g * 128:(g + 1) * 128] for b in range(nb)], axis=1)
        xg = xg.astype(_MXU_DTYPE)
        ms.append(lax.dot_general(wm, xg, _NN, preferred_element_type=F32) + bt[:, g:g + 1])
        wms.append(wm)
        xs.append(xg)
    mixed = _sg_unfold(ms, nb)
    return su, sv, u, tu, tv, xh, rs, wms, xs, mixed


def _sg_unfold(per_group, nb):
    return jnp.concatenate(
        [jnp.concatenate([per_group[g][:, b * 128:(b + 1) * 128] for g in range(SG_GROUPS)], axis=1)
         for b in range(nb)], axis=0)


def _sg_fwd(tag, proj_a, lg, lb, sg_w, sg_bt, tm=512):
    t_len = proj_a.shape[0]

    def body(t, rows, halos, res, outs, accs, scr):
        parts = _sg_forward_parts(rows[0][...], res[0][...], res[1][...], res[2], res[3][...])
        outs[0][...] = (parts[2] * parts[-1]).astype(_MXU_DTYPE)

    return _row_call("sg_fwd" + tag, body, t_len // tm, rows=[(proj_a, tm)], res=[lg, lb, sg_w, sg_bt],
                     outs=[((t_len, BW), _MXU_DTYPE, tm)])[0]


def _sg_bwd(tag, proj_a, dy, lg, lb, sg_w, sg_bt, tm=512):
    t_len = proj_a.shape[0]
    nb = tm // SG_BLOCK

    def body(t, rows, halos, res, outs, accs, scr):
        lgv = res[0][...]
        su, sv, u, tu, tv, xh, rs, wms, xs, mixed = _sg_forward_parts(rows[0][...], lgv, res[1][...], res[2], res[3][...])
        dyv = rows[1][...]
        dsu = dyv * mixed * _gelu_grad(su, tu)
        dmixed = dyv * u
        mask = _sg_mask()
        dxs, dbs = [], []
        for g in range(SG_GROUPS):
            dm = jnp.concatenate([dmixed[b * SG_BLOCK:(b + 1) * SG_BLOCK, g * 128:(g + 1) * 128] for b in range(nb)],
                                 axis=1)
            dmb = dm.astype(_MXU_DTYPE)
            dw = lax.dot_general(dmb, xs[g], _NT, preferred_element_type=F32)
            accs[0][g] += jnp.where(mask, dw, 0.0)
            dbs.append(jnp.sum(dm, axis=1, keepdims=True))
            dxs.append(lax.dot_general(wms[g], dmb, _TN, preferred_element_type=F32))
        accs[1][...] += jnp.concatenate(dbs, axis=1)
        dvn = _sg_unfold(dxs, nb)
        dgv, dlg, dlb = _ln_bwd(dvn, xh, rs, lgv)
        accs[2][...] += dlg
        accs[3][...] += dlb
        dsv = dgv * _gelu_grad(sv, tv)
        outs[0][...] = jnp.concatenate([dsu, dsv], axis=1).astype(_MXU_DTYPE)

    return _row_call("sg_bwd" + tag, body, t_len // tm, rows=[(proj_a, tm), (dy, tm)], res=[lg, lb, sg_w, sg_bt],
                     outs=[((t_len, 2 * BW), _MXU_DTYPE, tm)],
                     accs=[(SG_GROUPS, SG_BLOCK, SG_BLOCK), (SG_BLOCK, SG_GROUPS), (1, BW), (1, BW)])


def _chunk_matrix(tm, kind):
    row = lax.broadcasted_iota(jnp.int32, (tm, tm), 0)
    col = lax.broadcasted_iota(jnp.int32, (tm, tm), 1)
    same = lax.shift_right_logical(row, 6) == lax.shift_right_logical(col, 6)
    if kind == 'cumsum':
        same = jnp.logical_and(same, row >= col)
    elif kind == 'revsum':
        same = jnp.logical_and(same, row <= col)
    return same.astype(jnp.bfloat16)


def _gla_gate(pa, wa2, ba):
    z = _mm(pa, wa2) + ba
    log_a = (jnp.minimum(z, 0.0) - jnp.log(1.0 + jnp.exp(-jnp.abs(z)))) * (1.0 / GLA_TAU)
    return z, log_a


def _gla_decay(pb, log_a):
    tm = pb.shape[0]
    cum = _mm_exact_rhs(_chunk_matrix(tm, 'cumsum'), log_a)
    tot = _mm_exact_rhs(_chunk_matrix(tm, 'total'), log_a)
    w = jnp.exp(tot - cum)
    return w, pb[:, 256:512] * w, jnp.exp(tot)


def _per_head(fn):
    return jnp.concatenate([fn(h) for h in range(GLA_HEADS)], axis=1)


def _gla_read(qs, sb, c):
    rows = slice(c * CHUNK, (c + 1) * CHUNK)
    return _per_head(lambda h: lax.dot_general(qs[rows, h * 64:(h + 1) * 64], sb[:, h * 64:(h + 1) * 64], _NT,
                                               preferred_element_type=F32))


def _gla_fwd(tag, proj_b, proj_a, wa2, ba, ng, tm=512, exchange=()):
    t_len = proj_b.shape[0]
    cpt = tm // CHUNK

    def body(t, rows, halos, res, outs, accs, scr):
        pb = rows[0][...]
        _, log_a = _gla_gate(rows[1][...], res[0][...], res[1][...])
        _, kd, dec = _gla_decay(pb, log_a)
        kdb = kd.astype(_MXU_DTYPE)
        vb = pb[:, 512:1024].astype(_MXU_DTYPE)
        qs = (pb[:, 0:256] * (GLA_DK ** -0.5)).astype(_MXU_DTYPE)
        uts = []
        for c in range(cpt):
            rs = slice(c * CHUNK, (c + 1) * CHUNK)
            uts.append(_per_head(lambda h: lax.dot_general(vb[rs, h * 128:(h + 1) * 128], kdb[rs, h * 64:(h + 1) * 64],
                                                           _TN, preferred_element_type=F32)))
        s_new = scr[0][...]
        o = []
        for c in range(cpt):
            s_new = dec[c * CHUNK:c * CHUNK + 1] * s_new + uts[c]
            outs[1][c] = s_new
            o.append(_gla_read(qs, s_new.astype(_MXU_DTYPE), c))
        scr[0][...] = s_new
        o = jnp.concatenate(o, axis=0)
        on = _per_head(lambda h: o[:, h * 128:(h + 1) * 128] * lax.rsqrt(
            jnp.mean(jnp.square(o[:, h * 128:(h + 1) * 128]), axis=-1, keepdims=True) + EPS))
        r = pb[:, 1024:1536]
        outs[0][...] = (on * res[2][...] * (r * _sigmoid(r))).astype(_MXU_DTYPE)

    return _row_call("gla_fwd" + tag, body, t_len // tm, rows=[(proj_b, tm), (proj_a, tm)], res=[wa2, ba, ng],
                     outs=[((t_len, BW), _MXU_DTYPE, tm), ((t_len // CHUNK, GLA_DV, 256), F32, cpt)],
                     scratch=[((GLA_DV, 256), F32)], exchange=exchange)


def _gla_bwd(tag, proj_b, proj_a, dy, states, wa2, ba, ng, tm=512):
    t_len = proj_b.shape[0]
    cpt = tm // CHUNK

    def body(t, rows, halos, res, outs, accs, scr):
        pb = rows[0][...]
        pa = rows[1][...]
        dyv = rows[2][...]
        st_ref = rows[3]
        wa2v = res[0][...]
        z, log_a = _gla_gate(pa, wa2v, res[1][...])
        ngv = res[2][...]
        w, kd, dec = _gla_decay(pb, log_a)
        kdb = kd.astype(_MXU_DTYPE)
        vb = pb[:, 512:1024].astype(_MXU_DTYPE)
        qs = (pb[:, 0:256] * (GLA_DK ** -0.5)).astype(_MXU_DTYPE)
        chunks = [slice(c * CHUNK, (c + 1) * CHUNK) for c in range(cpt)]
        sbs = [st_ref[c].astype(_MXU_DTYPE) for c in range(cpt)]
        o = jnp.concatenate([_gla_read(qs, sbs[c], c) for c in range(cpt)], axis=0)
        r = pb[:, 1024:1536]
        sig = _sigmoid(r)
        sil = r * sig
        dos, ons = [], []
        for h in range(GLA_HEADS):
            hs = slice(h * 128, (h + 1) * 128)
            oh = o[:, hs]
            rstd = lax.rsqrt(jnp.mean(oh * oh, axis=-1, keepdims=True) + EPS)
            on = oh * rstd
            don = dyv[:, hs] * ngv[:, hs] * sil[:, hs]
            dos.append(rstd * (don - on * jnp.mean(don * on, axis=-1, keepdims=True)))
            ons.append(on)
        on = jnp.concatenate(ons, axis=1)
        accs[2][...] += jnp.sum(dyv * on * sil, axis=0, keepdims=True)
        dr = dyv * on * ngv * (sig * (1.0 + r * (1.0 - sig)))
        dob = jnp.concatenate(dos, axis=1).astype(_MXU_DTYPE)
        reads, dqs = [], []
        for c, rs in enumerate(chunks):
            reads.append(_per_head(lambda h: lax.dot_general(dob[rs, h * 128:(h + 1) * 128], qs[rs, h * 64:(h + 1) * 64],
                                                             _TN, preferred_element_type=F32)))
            dqs.append(_per_head(lambda h: lax.dot_general(dob[rs, h * 128:(h + 1) * 128], sbs[c][:, h * 64:(h + 1) * 64],
                                                           _NN, preferred_element_type=F32)))
        dst = scr[0][...]
        dubs, ddecs = [None] * cpt, [None] * cpt
        for c in reversed(range(cpt)):
            dst_tot = dst + reads[c]
            s_prev = st_ref[c - 1] if c > 0 else jnp.where(t > 0, halos[0][0], 0.0)
            ddecs[c] = jnp.broadcast_to(jnp.sum(dst_tot * s_prev, axis=0, keepdims=True), (CHUNK, 256))
            dst = dec[c * CHUNK:c * CHUNK + 1] * dst_tot
            dubs[c] = dst_tot.astype(_MXU_DTYPE)
        scr[0][...] = dst
        dkd = jnp.concatenate(
            [_per_head(lambda h: lax.dot_general(vb[rs, h * 128:(h + 1) * 128], dubs[c][:, h * 64:(h + 1) * 64], _NN,
                                                 preferred_element_type=F32)) for c, rs in enumerate(chunks)], axis=0)
        dv = jnp.concatenate(
            [_per_head(lambda h: lax.dot_general(kdb[rs, h * 64:(h + 1) * 64], dubs[c][:, h * 64:(h + 1) * 64], _NT,
                                                 preferred_element_type=F32)) for c, rs in enumerate(chunks)], axis=0)
        e = dkd * kd
        dtot = _mm_exact_rhs(_chunk_matrix(tm, 'total'), e) + jnp.concatenate(ddecs, axis=0) * dec
        last = (lax.broadcasted_iota(jnp.int32, e.shape, 0) & (CHUNK - 1)) == CHUNK - 1
        dla = _mm_exact_rhs(_chunk_matrix(tm, 'revsum'), jnp.where(last, dtot - e, -e))
        dz = dla * (1.0 / GLA_TAU) * _sigmoid(-z)
        dzb = dz.astype(_MXU_DTYPE)
        dq = jnp.concatenate(dqs, axis=0) * (GLA_DK ** -0.5)
        da = lax.dot_general(dzb, wa2v.astype(_MXU_DTYPE), _NT, preferred_element_type=F32)
        outs[0][...] = jnp.concatenate([dq, dkd * w, dv, dr, da], axis=1).astype(_MXU_DTYPE)
        accs[0][...] += lax.dot_general(pa.astype(_MXU_DTYPE), dzb, _TN, preferred_element_type=F32)
        accs[1][...] += jnp.sum(dz, axis=0, keepdims=True)

    return _row_call("gla_bwd" + tag, body, t_len // tm,
                     rows=[(proj_b, tm), (proj_a, tm), (dy, tm), (states, cpt)],
                     halos=[(states, 1, cpt, 'prev')], res=[wa2, ba, ng],
                     outs=[((t_len, 1536 + GLA_RANK), _MXU_DTYPE, tm)],
                     accs=[(GLA_RANK, 256), (1, 256), (1, BW)], scratch=[((GLA_DV, 256), F32)], reverse=True)


ATT_TM = 256
ATT_KEYS = ATT_TM + (ATT_BAND - 1) * CHUNK


def _rel_index():
    l_idx = np.arange(CHUNK)[:, None]
    m_idx = np.arange(BAND)[None, :]
    rel = l_idx + (ATT_BAND - 1) * CHUNK - m_idx
    return jnp.asarray((np.clip(rel, -(CHUNK - 1), MAX_REL) + (CHUNK - 1)).reshape(1, CHUNK * BAND), jnp.int32)


BIAS_COLS = 4096


def _bias_expand(tag, rel_bias):
    n = CHUNK * BAND

    def kern(rel_ref, idx_ref, o_ref):
        onehot = (lax.broadcasted_iota(jnp.int32, (REL_TABLE, BIAS_COLS), 0) == idx_ref[...]).astype(jnp.bfloat16)
        o_ref[...] = _mm_exact_lhs(rel_ref[...], onehot)

    return pl.pallas_call(
        kern, grid=(n // BIAS_COLS,),
        in_specs=[pl.BlockSpec((ATT_HEADS, REL_TABLE), lambda i: (0, 0)), pl.BlockSpec((1, BIAS_COLS), lambda i: (0, i))],
        out_specs=pl.BlockSpec((ATT_HEADS, BIAS_COLS), lambda i: (0, i)),
        out_shape=jax.ShapeDtypeStruct((ATT_HEADS, n), F32), name="bias_expand" + tag)(rel_bias, _rel_index())


def _bias_tile(tag, bias):
    per = ATT_TM // CHUNK

    def kern(b_ref, o_ref):
        bv = b_ref[...]
        for j in range(per):
            parts = [jnp.full((CHUNK, j * CHUNK), NEG_INF, F32)] if j else []
            parts.append(bv)
            if j < per - 1:
                parts.append(jnp.full((CHUNK, (per - 1 - j) * CHUNK), NEG_INF, F32))
            o_ref[j * CHUNK:(j + 1) * CHUNK, :] = jnp.concatenate(parts, axis=1)

    return pl.pallas_call(
        kern, grid=(ATT_HEADS,), in_specs=[pl.BlockSpec((None, CHUNK, BAND), lambda h: (h, 0, 0))],
        out_specs=pl.BlockSpec((None, ATT_TM, ATT_KEYS), lambda h: (h, 0, 0)),
        out_shape=jax.ShapeDtypeStruct((ATT_HEADS, ATT_TM, ATT_KEYS), F32), name="bias_tile" + tag)(bias)


def _bias_untile(tag, dbias):
    per = ATT_TM // CHUNK

    def kern(d_ref, o_ref):
        acc = d_ref[0:CHUNK, 0:BAND]
        for j in range(1, per):
            acc = acc + d_ref[j * CHUNK:(j + 1) * CHUNK, j * CHUNK:j * CHUNK + BAND]
        o_ref[...] = acc

    return pl.pallas_call(
        kern, grid=(ATT_HEADS,), in_specs=[pl.BlockSpec((None, ATT_TM, ATT_KEYS), lambda h: (h, 0, 0))],
        out_specs=pl.BlockSpec((None, CHUNK, BAND), lambda h: (h, 0, 0)),
        out_shape=jax.ShapeDtypeStruct((ATT_HEADS, CHUNK, BAND), F32), name="bias_untile" + tag)(dbias)


def _bias_reduce(tag, dbias):
    n = CHUNK * BAND

    def kern(db_ref, idx_ref, o_ref):
        @pl.when(pl.program_id(0) == 0)
        def _():
            o_ref[...] = jnp.zeros(o_ref.shape, o_ref.dtype)

        onehot = (lax.broadcasted_iota(jnp.int32, (REL_TABLE, BIAS_COLS), 0) == idx_ref[...]).astype(jnp.bfloat16)
        o_ref[...] += _mm_exact_lhs(db_ref[...], onehot, _NT)

    return pl.pallas_call(
        kern, grid=(n // BIAS_COLS,),
        in_specs=[pl.BlockSpec((ATT_HEADS, BIAS_COLS), lambda i: (0, i)), pl.BlockSpec((1, BIAS_COLS), lambda i: (0, i))],
        out_specs=pl.BlockSpec((ATT_HEADS, REL_TABLE), lambda i: (0, 0)),
        out_shape=jax.ShapeDtypeStruct((ATT_HEADS, REL_TABLE), F32),
        compiler_params=pltpu.CompilerParams(dimension_semantics=("arbitrary",)),
        name="bias_reduce" + tag)(dbias, _rel_index())


def _attn_stage(t, pc_ref, p1_ref, p2_ref, kv):
    tm = ATT_TM
    kv[0:tm, :] = jnp.where(t > 1, p2_ref[:, 512:1536], 0.0).astype(kv.dtype)
    kv[tm:2 * tm, :] = jnp.where(t > 0, p1_ref[:, 512:1536], 0.0).astype(kv.dtype)
    kv[2 * tm:, :] = pc_ref[:, 512:1536].astype(kv.dtype)
    q = (pc_ref[:, 0:512] * (ATT_HD ** -0.5)).astype(_MXU_DTYPE)
    ok = lax.broadcasted_iota(jnp.int32, (tm, ATT_KEYS), 1) >= (2 - t) * tm
    return q, ok


def _attn_probs(q, kv, bias_h, ok, h):
    hs = slice(h * ATT_HD, (h + 1) * ATT_HD)
    s = lax.dot_general(q[:, hs], kv[:, hs], _NT, preferred_element_type=F32) + bias_h
    if ok is not None:
        s = jnp.where(ok, s, NEG_INF)
    e = jnp.exp(s - jnp.max(s, axis=-1, keepdims=True))
    return e * (1.0 / jnp.sum(e, axis=-1, keepdims=True))


def _attn_halos(proj_c):
    return [(proj_c, ATT_TM, 1, 'prev'), (proj_c, ATT_TM, 1, 'prev2')]


def _attn_fwd(tag, proj_c, bias, exchange=()):
    t_len = proj_c.shape[0]
    tm = ATT_TM

    def body(t, rows, halos, res, outs, accs, scr):
        b_ref, kv = res[0], scr[0]
        q, ok = _attn_stage(t, rows[0], halos[0], halos[1], kv)

        def heads(ok):
            o = []
            for h in range(ATT_HEADS):
                p = _attn_probs(q, kv, b_ref[h], ok, h).astype(_MXU_DTYPE)
                outs[1][:, h * ATT_KEYS:(h + 1) * ATT_KEYS] = p
                o.append(lax.dot_general(p, kv[:, BW + h * ATT_HD:BW + (h + 1) * ATT_HD], _NN,
                                         preferred_element_type=F32))
            outs[0][...] = jnp.concatenate(o, axis=1).astype(_MXU_DTYPE)

        pl.when(t < 2)(lambda: heads(ok))
        pl.when(t >= 2)(lambda: heads(None))

    return _row_call("attn_fwd" + tag, body, t_len // tm, rows=[(proj_c, tm)], halos=_attn_halos(proj_c),
                     res=[bias], outs=[((t_len, BW), _MXU_DTYPE, tm), ((t_len, ATT_HEADS * ATT_KEYS), _MXU_DTYPE, tm)],
                     scratch=[((ATT_KEYS, 1024), _MXU_DTYPE)], exchange=exchange)


def _attn_bwd(tag, proj_c, dy, probs, exchange=()):
    t_len = proj_c.shape[0]
    tm = ATT_TM
    scale = ATT_HD ** -0.5

    def body(t, rows, halos, res, outs, accs, scr):
        kv = scr[0]
        q, _ = _attn_stage(t, rows[0], halos[0], halos[1], kv)
        do = rows[1][...].astype(_MXU_DTYPE)
        dqs, dks, dvs = [], [], []
        for h in range(ATT_HEADS):
            hs = slice(h * ATT_HD, (h + 1) * ATT_HD)
            vs = slice(BW + h * ATT_HD, BW + (h + 1) * ATT_HD)
            pb = rows[2][:, h * ATT_KEYS:(h + 1) * ATT_KEYS]
            p = pb.astype(F32)
            dp = lax.dot_general(do[:, hs], kv[:, vs], _NT, preferred_element_type=F32)
            ds = p * (dp - jnp.sum(dp * p, axis=-1, keepdims=True))
            accs[0][h] += ds
            dsb = ds.astype(_MXU_DTYPE)
            dqs.append(lax.dot_general(dsb, kv[:, hs], _NN, preferred_element_type=F32) * scale)
            dks.append(lax.dot_general(dsb, q[:, hs], _TN, preferred_element_type=F32))
            dvs.append(lax.dot_general(pb, do[:, hs], _TN, preferred_element_type=F32))
        outs[0][...] = jnp.concatenate(dqs, axis=1).astype(_MXU_DTYPE)
        dkv = jnp.concatenate(dks + dvs, axis=1)
        after_one, after_two = scr[1], scr[2]
        outs[1][...] = (dkv[2 * tm:, :] + after_two[...]).astype(_MXU_DTYPE)
        after_two[...] = dkv[tm:2 * tm, :] + after_one[...]
        after_one[...] = dkv[0:tm, :]

    return _row_call("attn_bwd" + tag, body, t_len // tm, rows=[(proj_c, tm), (dy, tm), (probs, tm)],
                     halos=_attn_halos(proj_c),
                     outs=[((t_len, BW), _MXU_DTYPE, tm), ((t_len, 1024), _MXU_DTYPE, tm)],
                     accs=[(ATT_HEADS, ATT_TM, ATT_KEYS)],
                     scratch=[((ATT_KEYS, 1024), _MXU_DTYPE), ((tm, 1024), F32), ((tm, 1024), F32)],
                     reverse=True, exchange=exchange)


def _conv_glu(pd):
    a, g = pd[:, :BW], pd[:, BW:]
    sig = _sigmoid(g)
    return a, sig, a * sig


def _conv_stage(t, pd_ref, ph_ref, win):
    pd = pd_ref[...]
    a, sig, y0 = _conv_glu(pd)
    win[0:CONV_HALO, :] = jnp.where(t > 0, _conv_glu(ph_ref[...])[2], 0.0)
    win[CONV_HALO:CONV_HALO + pd.shape[0], :] = y0
    return a, sig


SUBLANES = 8


def _conv_shifted(win, sh):
    for b in range(SUBLANES):
        sh[b] = win[pl.ds(b, sh.shape[1]), :]


def _conv_taps_by_copy(offsets):
    groups = {}
    for j, o in enumerate(offsets):
        groups.setdefault(o % SUBLANES, []).append((j, o - o % SUBLANES))
    return [(rem, min(a for _, a in taps), max(a for _, a in taps) - min(a for _, a in taps), taps)
            for rem, taps in sorted(groups.items())]


def _conv_span(sh, rem, r0, lo, rows):
    return sh[rem, pl.ds(pl.multiple_of(r0 + lo, SUBLANES), rows), :]


def _conv_tap_sum(sh, w_ref, offsets, out_ref, init=None, rb=32):
    plan = _conv_taps_by_copy(offsets)

    def block(i, carry):
        r0 = pl.multiple_of(i * rb, rb)
        acc = jnp.zeros((rb, BW), F32) if init is None else jnp.broadcast_to(init, (rb, BW))
        for rem, lo, extra, taps in plan:
            span = _conv_span(sh, rem, r0, lo, extra + rb)
            for j, a in taps:
                acc = acc + w_ref[j:j + 1, :] * span[a - lo:a - lo + rb]
        out_ref[pl.ds(r0, rb), :] = acc
        return carry

    lax.fori_loop(0, out_ref.shape[0] // rb, block, 0)


def _conv_tap_corr(sh, d_ref, offsets, acc_ref, rb=32):
    for rem, lo, extra, taps in _conv_taps_by_copy(offsets):
        def block(i, sums, rem=rem, lo=lo, extra=extra, taps=taps):
            r0 = pl.multiple_of(i * rb, rb)
            d = d_ref[pl.ds(r0, rb), :]
            span = _conv_span(sh, rem, r0, lo, extra + rb)
            out = []
            for s, (j, a) in zip(sums, taps):
                prod = d * span[a - lo:a - lo + rb]
                for k in range(0, rb, SUBLANES):
                    s = s + prod[k:k + SUBLANES]
                out.append(s)
            return tuple(out)

        sums = lax.fori_loop(0, d_ref.shape[0] // rb, block,
                             tuple(jnp.zeros((SUBLANES, BW), F32) for _ in taps))
        for (j, _), s in zip(taps, sums):
            acc_ref[j:j + 1, :] += jnp.sum(s, axis=0, keepdims=True)


def _conv_scratch(tm):
    return [((tm + CONV_HALO + SUBLANES, BW), F32), ((SUBLANES, tm + CONV_HALO, BW), F32)]


def _conv_fwd(tag, proj_d, dw_w, dw_b, ln_g, ln_b, tm=512):
    t_len = proj_d.shape[0]
    lead = CONV_HALO - (CONV_K - 1)

    def body(t, rows, halos, res, outs, accs, scr):
        win, sh = scr
        _conv_stage(t, rows[0], halos[0], win)
        _conv_shifted(win, sh)
        _conv_tap_sum(sh, res[0], [lead + j for j in range(CONV_K)], outs[1], init=res[1][...])
        yl, _, _ = _ln_fwd(outs[1][...], res[2][...], res[3][...])
        outs[0][...] = (yl * _sigmoid(yl)).astype(_MXU_DTYPE)

    return _row_call("conv_fwd" + tag, body, t_len // tm, rows=[(proj_d, tm)],
                     halos=[(proj_d, CONV_HALO, tm // CONV_HALO, 'prev')], res=[dw_w, dw_b, ln_g, ln_b],
                     outs=[((t_len, BW), _MXU_DTYPE, tm), ((t_len, BW), F32, tm)], scratch=_conv_scratch(tm))


def _conv_bwd_norm(tag, yc, dy, ln_g, ln_b, tm=512):
    t_len = yc.shape[0]

    def body(t, rows, halos, res, outs, accs, scr):
        lgv = res[0][...]
        yl, xh, rs = _ln_fwd(rows[0][...], lgv, res[1][...])
        sig = _sigmoid(yl)
        dyl = rows[1][...] * (sig * (1.0 + yl * (1.0 - sig)))
        dyc, dlg, dlb = _ln_bwd(dyl, xh, rs, lgv)
        outs[0][...] = dyc
        accs[0][...] += dlg
        accs[1][...] += dlb
        accs[2][...] += jnp.sum(dyc, axis=0, keepdims=True)

    return _row_call("conv_bwd_norm" + tag, body, t_len // tm, rows=[(yc, tm), (dy, tm)], res=[ln_g, ln_b],
                     outs=[((t_len, BW), F32, tm)], accs=[(1, BW), (1, BW), (1, BW)])


def _conv_bwd_taps(tag, proj_d, dyc, dw_w, tm=512):
    t_len = proj_d.shape[0]
    nt = t_len // tm
    lead = CONV_HALO - (CONV_K - 1)

    def body(t, rows, halos, res, outs, accs, scr):
        win, sh, wd, shd, dy0_ref = scr
        a, sig = _conv_stage(t, rows[0], halos[0], win)
        _conv_shifted(win, sh)
        wd[0:tm, :] = rows[1][...]
        wd[tm:tm + CONV_HALO, :] = jnp.where(t < nt - 1, halos[1][...], 0.0)
        _conv_shifted(wd, shd)
        _conv_tap_corr(sh, rows[1], [lead + j for j in range(CONV_K)], accs[0])
        _conv_tap_sum(shd, res[0], [CONV_K - 1 - j for j in range(CONV_K)], dy0_ref)
        dy0 = dy0_ref[...]
        outs[0][...] = jnp.concatenate([dy0 * sig, dy0 * a * sig * (1.0 - sig)], axis=1).astype(_MXU_DTYPE)

    return _row_call("conv_bwd_taps" + tag, body, nt, rows=[(proj_d, tm), (dyc, tm)],
                     halos=[(proj_d, CONV_HALO, tm // CONV_HALO, 'prev'), (dyc, CONV_HALO, tm // CONV_HALO, 'next')],
                     res=[dw_w], outs=[((t_len, 2 * BW), _MXU_DTYPE, tm)], accs=[(CONV_K, BW)],
                     scratch=_conv_scratch(tm) + _conv_scratch(tm) + [((tm, BW), F32)])


def _merge_fwd(tag, h, xn, ys, w_gate, b_gate, w_branch, w_out, tm=512):
    t_len = h.shape[0]

    def body(t, rows, halos, res, outs, accs, scr):
        xnv = rows[1][...]
        wg_ref, bg_ref, wb_ref, wo_ref = res
        merged = jnp.zeros((tm, D_MODEL), F32)
        for n in range(4):
            cs = slice(n * D_MODEL, (n + 1) * D_MODEL)
            z = lax.dot_general(xnv, wg_ref[n], _NN, preferred_element_type=F32) + bg_ref[n:n + 1, :]
            bo = lax.dot_general(rows[2 + n][...], wb_ref[n], _NN, preferred_element_type=F32)
            outs[0][:, cs] = z.astype(_MXU_DTYPE)
            outs[1][:, cs] = bo.astype(_MXU_DTYPE)
            merged = merged + _sigmoid(z) * bo
        mb = merged.astype(_MXU_DTYPE)
        outs[2][...] = mb
        outs[3][...] = rows[0][...] + lax.dot_general(mb, wo_ref[...], _NN, preferred_element_type=F32)

    return _row_call("merge_fwd" + tag, body, t_len // tm, rows=[(h, tm), (xn, tm)] + [(y, tm) for y in ys],
                     res=[w_gate, b_gate, w_branch, w_out],
                     outs=[((t_len, 4 * D_MODEL), _MXU_DTYPE, tm), ((t_len, 4 * D_MODEL), _MXU_DTYPE, tm),
                           ((t_len, D_MODEL), _MXU_DTYPE, tm), ((t_len, D_MODEL), F32, tm)])


def _merge_bwd(tag, dh, gate_pre, bo, w_gate, w_branch, w_out, tm=256):
    t_len = dh.shape[0]

    def body(t, rows, halos, res, outs, accs, scr):
        wg_ref, wb_ref, wo_ref = res
        dhb = rows[0][...].astype(_MXU_DTYPE)
        outs[0][...] = dhb
        dmerged = lax.dot_general(dhb, wo_ref[...], _NT, preferred_element_type=F32)
        dxn = jnp.zeros((tm, D_MODEL), F32)
        dbg = []
        for n in range(4):
            cs = slice(n * D_MODEL, (n + 1) * D_MODEL)
            g = _sigmoid(rows[1][:, cs].astype(F32))
            dbo = (dmerged * g).astype(_MXU_DTYPE)
            dgp = dmerged * rows[2][:, cs].astype(F32) * (g * (1.0 - g))
            dgb = dgp.astype(_MXU_DTYPE)
            outs[1][:, cs] = dbo
            outs[2][:, cs] = dgb
            outs[4 + n][...] = lax.dot_general(dbo, wb_ref[n], _NT, preferred_element_type=F32)
            dxn = dxn + lax.dot_general(dgb, wg_ref[n], _NT, preferred_element_type=F32)
            dbg.append(jnp.sum(dgp, axis=0, keepdims=True))
        outs[3][...] = dxn
        accs[0][...] += jnp.concatenate(dbg, axis=1)

    return _row_call("merge_bwd" + tag, body, t_len // tm, rows=[(dh, tm), (gate_pre, tm), (bo, tm)],
                     res=[w_gate, w_branch, w_out],
                     outs=[((t_len, D_MODEL), _MXU_DTYPE, tm), ((t_len, 4 * D_MODEL), _MXU_DTYPE, tm),
                           ((t_len, 4 * D_MODEL), _MXU_DTYPE, tm), ((t_len, D_MODEL), F32, tm)]
                     + [((t_len, BW), F32, tm)] * 4,
                     accs=[(1, 4 * D_MODEL)])


FF_COLS = 1024


def _ffn_fwd(tag, h, g2, w1, w2, tm=512):
    t_len = h.shape[0]

    def body(t, rows, halos, res, outs, accs, scr):
        hv = rows[0][...]
        hn = (hv * _rms_stat(hv) * res[0][...]).astype(_MXU_DTYPE)
        outs[0][...] = hn
        acc = hv
        for c in range(D_FF // FF_COLS):
            cs = slice(c * FF_COLS, (c + 1) * FF_COLS)
            pre = lax.dot_general(hn, res[1][:, cs], _NN, preferred_element_type=F32)
            outs[1][:, cs] = pre
            ff = jnp.square(jnp.maximum(pre, 0.0)).astype(_MXU_DTYPE)
            acc = acc + lax.dot_general(ff, res[2][cs, :], _NN, preferred_element_type=F32)
        outs[2][...] = acc

    return _row_call("ffn_fwd" + tag, body, t_len // tm, rows=[(h, tm)], res=[g2, w1, w2],
                     outs=[((t_len, D_MODEL), _MXU_DTYPE, tm), ((t_len, D_FF), F32, tm), ((t_len, D_MODEL), F32, tm)])


def _ffn_bwd(tag, dh, h, pre, g2, w1, w2, tm=256, exchange=()):
    t_len = dh.shape[0]

    def body(t, rows, halos, res, outs, accs, scr):
        dhv = rows[0][...]
        hv = rows[1][...]
        dhb = dhv.astype(_MXU_DTYPE)
        outs[0][...] = dhb
        dhn = jnp.zeros((tm, D_MODEL), F32)
        for c in range(D_FF // FF_COLS):
            cs = slice(c * FF_COLS, (c + 1) * FF_COLS)
            r = jnp.maximum(rows[2][:, cs], 0.0)
            outs[1][:, cs] = (r * r).astype(_MXU_DTYPE)
            dpre = (lax.dot_general(dhb, res[2][cs, :], _NT, preferred_element_type=F32) * (2.0 * r)).astype(_MXU_DTYPE)
            outs[2][:, cs] = dpre
            dhn = dhn + lax.dot_general(dpre, res[1][:, cs], _NT, preferred_element_type=F32)
        dres, dg = _rms_bwd(dhn, hv, res[0][...], _rms_stat(hv))
        outs[3][...] = dhv + dres
        accs[0][...] += dg

    return _row_call("ffn_bwd" + tag, body, t_len // tm, rows=[(dh, tm), (h, tm), (pre, tm)], res=[g2, w1, w2],
                     outs=[((t_len, D_MODEL), _MXU_DTYPE, tm), ((t_len, D_FF), _MXU_DTYPE, tm),
                           ((t_len, D_FF), _MXU_DTYPE, tm), ((t_len, D_MODEL), F32, tm)],
                     accs=[(1, D_MODEL)], exchange=exchange)


def _ple_fwd(tag, h, p, g3, w_pg, b_pg, w_ple, head=None, tm=512):
    t_len = h.shape[0]

    def body(t, rows, halos, res, outs, accs, scr):
        hv = rows[0][...]
        hg = (hv * _rms_stat(hv) * res[0][...]).astype(_MXU_DTYPE)
        pb = rows[1][...].astype(_MXU_DTYPE)
        pg = _sigmoid(lax.dot_general(hg, res[1][...], _NN, preferred_element_type=F32) + res[2][...])
        pe = lax.dot_general(pb, res[3][...], _NN, preferred_element_type=F32)
        outs[0][...] = hg
        outs[1][...] = pb
        outs[2][...] = pg
        h3 = hv + pg * pe
        if head is None:
            outs[3][...] = h3
        else:
            g = res[4][...]
            r = _rms_stat(h3)
            diff = h3 * r * g - rows[2][...]
            accs[0][...] += 0.5 * jnp.sum(jnp.mean(diff * diff, axis=-1, keepdims=True), axis=0, keepdims=True)
            outs[3][...], dg = _rms_bwd(diff * (1.0 / D_MODEL), h3, g, r)
            accs[1][...] += dg

    return _row_call("ple_fwd" + tag, body, t_len // tm, rows=[(h, tm), (p, tm)] + ([(head[0], tm)] if head else []),
                     res=[g3, w_pg, b_pg, w_ple] + ([head[1]] if head else []),
                     outs=[((t_len, D_MODEL), _MXU_DTYPE, tm), ((t_len, PLE_DIM), _MXU_DTYPE, tm),
                           ((t_len, D_MODEL), F32, tm), ((t_len, D_MODEL), F32, tm)],
                     accs=[(1, 128), (1, D_MODEL)] if head else [])


def _ple_bwd(tag, dh, h, pg, p_b, g3, w_pg, w_ple, tm=512):
    t_len = dh.shape[0]

    def body(t, rows, halos, res, outs, accs, scr):
        dhv = rows[0][...]
        hv = rows[1][...]
        pgv = rows[2][...]
        pe = lax.dot_general(rows[3][...], res[2][...], _NN, preferred_element_type=F32)
        dgp = dhv * pe * (pgv * (1.0 - pgv))
        dgb = dgp.astype(_MXU_DTYPE)
        outs[0][...] = dgb
        outs[1][...] = (dhv * pgv).astype(_MXU_DTYPE)
        dhg = lax.dot_general(dgb, res[1][...], _NT, preferred_element_type=F32)
        dres, dg = _rms_bwd(dhg, hv, res[0][...], _rms_stat(hv))
        outs[2][...] = dhv + dres
        accs[0][...] += jnp.sum(dgp, axis=0, keepdims=True)
        accs[1][...] += dg

    return _row_call("ple_bwd" + tag, body, t_len // tm, rows=[(dh, tm), (h, tm), (pg, tm), (p_b, tm)],
                     res=[g3, w_pg, w_ple],
                     outs=[((t_len, D_MODEL), _MXU_DTYPE, tm), ((t_len, D_MODEL), _MXU_DTYPE, tm),
                           ((t_len, D_MODEL), F32, tm)],
                     accs=[(1, D_MODEL), (1, D_MODEL)])


def _inproj_bwd(tag, dh, h, dxn_gate, dprojs, g1, w_in_t, tm=512, exchange=()):
    t_len = dh.shape[0]

    def body(t, rows, halos, res, outs, accs, scr):
        hv = rows[1][...]
        dxn = rows[2][...]
        for dp, (_, s0, n) in zip(rows[3:], IN_GROUPS_BWD):
            dxn = dxn + lax.dot_general(dp[...], res[1][s0:s0 + n, :], _NN, preferred_element_type=F32)
        dres, dg = _rms_bwd(dxn, hv, res[0][...], _rms_stat(hv))
        outs[0][...] = rows[0][...] + dres
        accs[0][...] += dg

    return _row_call("inproj_bwd" + tag, body, t_len // tm,
                     rows=[(dh, tm), (h, tm), (dxn_gate, tm)] + [(d, tm) for d in dprojs],
                     res=[g1, w_in_t], outs=[((t_len, D_MODEL), F32, tm)], accs=[(1, D_MODEL)],
                     exchange=exchange)


def _row(v):
    return v.reshape(1, -1)


GATHER_DURING = (('inproj', ('w_gate', 'w_branch', 'w_out')), ('gla', ('w_ff1',)), ('attn', ('w_ff2', 'w_ple_gate', 'w_ple')))
SCATTER_DURING_ATTN = ('w_ple_gate', 'w_ple', 'w_ff1', 'w_ff2', 'w_out', 'w_gate', 'w_branch')


def _gather_items(shards, names):
    items = []
    for n in names:
        s = shards[n]
        ax = SHARD_AXIS[n] - 1
        if n == 'w_in':
            items.append(_gather_item(s))
        else:
            items.append(_gather_item(s, s.shape[:ax] + (N_DEV * s.shape[ax],) + s.shape[ax + 1:], ax))
    return items


def _land(w, names, arrays):
    for n, a in zip(names, arrays):
        w[n] = a.reshape(IN_COLS, D_MODEL) if n == 'w_in' else a


def _layer_fwd(i, h, p_i, w, shards, next_shards, head=None):
    tag = "_l%d" % i
    during = dict(GATHER_DURING)
    res = _inproj_fwd(tag, h, _row(w['norm1_g']), w['w_in'], exchange=_gather_items(shards, during['inproj']))
    xn, pa, pb, pr, pc, pd = res[:6]
    _land(w, during['inproj'], res[6:])
    sg_bt = w['sg_b'].T
    y_a = _sg_fwd(tag, pa, _row(w['sg_ln_g']), _row(w['sg_ln_b']), w['sg_w'], sg_bt)
    res = _gla_fwd(tag, pb, pr, w['gla_w_a2'], _row(w['gla_b_a']), _row(w['gla_norm_g']),
                   exchange=_gather_items(shards, during['gla']))
    y_b, states = res[:2]
    _land(w, during['gla'], res[2:])
    bias = _bias_tile(tag, _bias_expand(tag, w['att_rel_bias']).reshape(ATT_HEADS, CHUNK, BAND))
    items = _gather_items(shards, during['attn']) + (_gather_items(next_shards, ['w_in']) if next_shards else [])
    res = _attn_fwd(tag, pc, bias, exchange=items)
    y_c, probs = res[:2]
    _land(w, during['attn'], res[2:2 + len(during['attn'])])
    next_w_in = res[-1].reshape(IN_COLS, D_MODEL) if next_shards else None
    y_d, yc = _conv_fwd(tag, pd, w['conv_dw_w'], _row(w['conv_dw_b']), _row(w['conv_ln_g']), _row(w['conv_ln_b']))
    ys = (y_a, y_b, y_c, y_d)
    gate, bo, merged, h1 = _merge_fwd(tag, h, xn, ys, w['w_gate'], w['b_gate'], w['w_branch'], w['w_out'])
    hn, pre, h2 = _ffn_fwd(tag, h1, _row(w['norm2_g']), w['w_ff1'], w['w_ff2'])
    res = _ple_fwd(tag, h2, p_i, _row(w['norm3_g']), w['w_ple_gate'], _row(w['b_ple_gate']), w['w_ple'], head=head)
    hg, p_b, pg, out = res[:4]
    saved = dict(h=h, xn=xn, pa=pa, pb=pb, pr=pr, pc=pc, pd=pd, states=states, probs=probs, yc=yc, ys=ys, gate=gate,
                 bo=bo, merged=merged, h1=h1, hn=hn, pre=pre, h2=h2, hg=hg, p_b=p_b, pg=pg, sg_bt=sg_bt)
    return (out, res[4:]), saved, next_w_in


def _layer_bwd(i, dh3, s, w, tail=None):
    tag = "_l%d" % i
    g = {}
    dgp, dpe, dh2, db_pg, dg3 = _ple_bwd(tag, dh3, s['h2'], s['pg'], s['p_b'], _row(w['norm3_g']), w['w_ple_gate'],
                                         w['w_ple'])
    g['b_ple_gate'], g['norm3_g'] = db_pg[0], dg3[0]
    g['w_ple_gate'] = _tn_call("dw_ple_gate" + tag, s['hg'], dgp, D_MODEL, D_MODEL)
    g['w_ple'] = _tn_call("dw_ple" + tag, s['p_b'], dpe, PLE_DIM, D_MODEL)

    dh2b, ffb, dpre, dh1, dg2 = _ffn_bwd(tag, dh2, s['h1'], s['pre'], _row(w['norm2_g']), w['w_ff1'], w['w_ff2'])
    g['norm2_g'] = dg2[0]
    g['w_ff1'] = _tn_call("dw_ff1" + tag, s['hn'], dpre, D_MODEL, FF_COLS, nblk=D_FF // FF_COLS)
    g['w_ff2'] = _tn_call("dw_ff2" + tag, ffb, dh2b, FF_COLS, D_MODEL, nblk=D_FF // FF_COLS, a_col=True, b_col=False,
                          out='rows')

    dh1b, dbo, dgpre, dxn_gate, dy_a, dy_b, dy_c, dy_d, db_gate = _merge_bwd(
        tag, dh1, s['gate'], s['bo'], w['w_gate'], w['w_branch'], w['w_out'])
    g['b_gate'] = db_gate.reshape(4, D_MODEL)
    g['w_out'] = _tn_call("dw_out" + tag, s['merged'], dh1b, D_MODEL, D_MODEL)
    g['w_gate'] = _tn_call("dw_gate" + tag, s['xn'], dgpre, D_MODEL, D_MODEL, nblk=4, out='stack')
    g['w_branch'] = jnp.stack([_tn_call("dw_branch%d%s" % (n, tag), s['ys'][n], dbo, BW, D_MODEL, b_off=n)
                             for n in range(4)])

    lg, lb = _row(w['sg_ln_g']), _row(w['sg_ln_b'])
    dpa, dsg_w, dsg_bt, dlg, dlb = _sg_bwd(tag, s['pa'], dy_a, lg, lb, w['sg_w'], s['sg_bt'])
    g['sg_w'], g['sg_b'], g['sg_ln_g'], g['sg_ln_b'] = dsg_w, dsg_bt.T, dlg[0], dlb[0]

    dpb, dwa2, dba, dng = _gla_bwd(tag, s['pb'], s['pr'], dy_b, s['states'], w['gla_w_a2'],
                                   _row(w['gla_b_a']), _row(w['gla_norm_g']))
    g['gla_w_a2'], g['gla_b_a'], g['gla_norm_g'] = dwa2, dba[0], dng[0]

    items = [_scatter_item(g.pop(n), axis=SHARD_AXIS[n] - 1) for n in SCATTER_DURING_ATTN]
    res = _attn_bwd(tag, s['pc'], dy_c, s['probs'], exchange=items)
    dq, dkv, dbias = res[:3]
    parts = dict(zip(SCATTER_DURING_ATTN, res[3:]))
    g['att_rel_bias'] = _bias_reduce(tag, _bias_untile(tag, dbias).reshape(ATT_HEADS, CHUNK * BAND))

    cg, cb = _row(w['conv_ln_g']), _row(w['conv_ln_b'])
    dyc, dcg, dcb, ddwb = _conv_bwd_norm(tag, s['yc'], dy_d, cg, cb)
    dpd, ddw = _conv_bwd_taps(tag, s['pd'], dyc, w['conv_dw_w'])
    g['conv_ln_g'], g['conv_ln_b'], g['conv_dw_b'], g['conv_dw_w'] = dcg[0], dcb[0], ddwb[0], ddw

    dprojs = (dpa, dpb, dq, dkv, dpd)
    dw_in = jnp.concatenate([_tn_call("dw_in%s%s" % (name, tag), dp, s['xn'], n, D_MODEL)
                             for (name, _, n), dp in zip(IN_GROUPS_BWD, dprojs)], axis=0)
    items = [_scatter_item(dw_in.reshape(N_DEV, IN_COLS // N_DEV, D_MODEL))] + (tail(g) if tail else [])
    res = _inproj_bwd(tag, dh1, s['h'], dxn_gate, dprojs, _row(w['norm1_g']), w['w_in'], exchange=items)
    dh0, dg1 = res[:2]
    g['norm1_g'] = dg1[0]
    parts['w_in'] = res[2]
    return dh0, g, parts, res[3:]


def _local_step(x, p, target, final_g, layers, shards, tail):
    h = x
    saved = []
    for i in range(DEPTH):
        nxt = shards[i + 1] if i + 1 < DEPTH else None
        head = None if nxt else (target, _row(final_g))
        (h, extra), s, next_w_in = _layer_fwd(i, h, p[i], layers[i], shards[i], nxt, head)
        saved.append(s)
        if nxt:
            layers[i + 1]['w_in'] = next_w_in
    dh, (loss, dgf) = h, extra
    small, parts, tail_out = [None] * DEPTH, [None] * DEPTH, None
    for i in reversed(range(DEPTH)):
        hook = (lambda g: tail([g] + small[1:])) if i == 0 else None
        dh, small[i], parts[i], out = _layer_bwd(i, dh, saved[i], layers[i], hook)
        if i == 0:
            tail_out = out
    return loss[0, 0], dh, dgf[0], small, parts, tail_out


def _peers():
    x, y, c = lax.axis_index("x"), lax.axis_index("y"), lax.axis_index("c")
    me = 4 * x + 2 * y + c
    out = []
    for k in range(1, N_DEV):
        px = (1 - x) if k & 4 else x
        py = (1 - y) if k & 2 else y
        pc = (1 - c) if k & 1 else c
        out.append((k - 1, (px, py, pc), 4 * px + 2 * py + pc))
    return me, out


def _block(ref, axis, idx, width):
    ix = [slice(None)] * len(ref.shape)
    ix[axis] = pl.ds(pl.multiple_of(idx * width, width), width)
    return ref.at[tuple(ix)]


def _slot(ref, idx):
    return ref.at[idx]


def _whole(ref, idx):
    return ref


def _gather_item(src, out_shape=None, axis=None):
    if axis is None:
        return dict(src=src, out=(N_DEV,) + src.shape, take=_whole, put=_slot)
    return dict(src=src, out=tuple(out_shape), take=_whole,
                put=lambda ref, s: _block(ref, axis, s, src.shape[axis]))


def _scatter_item(src, axis=None, lead=0):
    if axis is None:
        shape = src.shape[:lead] + src.shape[lead + 1:]
        take = lambda ref, s: ref.at[(slice(None),) * lead + (s,)]
    else:
        width = src.shape[axis] // N_DEV
        shape = src.shape[:axis] + (width,) + src.shape[axis + 1:]
        take = lambda ref, s: _block(ref, axis, s, width)
    return dict(src=src, out=(N_DEV,) + shape, take=take, put=_slot)


def _exchange_sems(n):
    return [pltpu.SemaphoreType.DMA((n * (N_DEV - 1),)), pltpu.SemaphoreType.DMA((n * (N_DEV - 1),)),
            pltpu.SemaphoreType.DMA((n,))]


def _exchange_copies(items, src_refs, out_refs, sems, start):
    send_sems, recv_sems, local_sems = sems
    me, peers = _peers()

    def remote(i, k, pos, receiver, sender):
        it = items[i]
        return pltpu.make_async_remote_copy(
            src_ref=it['take'](src_refs[i], receiver), dst_ref=it['put'](out_refs[i], sender),
            send_sem=send_sems.at[i * (N_DEV - 1) + k], recv_sem=recv_sems.at[i * (N_DEV - 1) + k],
            device_id=pos, device_id_type=pl.DeviceIdType.MESH)

    local = [pltpu.make_async_copy(it['take'](src_refs[i], me), it['put'](out_refs[i], me), local_sems.at[i])
             for i, it in enumerate(items)]
    if start:
        for cp in local:
            cp.start()
        for k, pos, flat in peers:
            for i in range(len(items)):
                remote(i, k, pos, flat, me).start()
    else:
        for k, pos, flat in peers:
            for i in range(len(items)):
                remote(i, k, pos, flat, flat).wait_recv()
        for k, pos, flat in peers:
            for i in range(len(items)):
                remote(i, k, pos, flat, me).wait_send()
        for cp in local:
            cp.wait()


def _gather_via_sibling(name, srcs):
    n = len(srcs)

    def body(*refs):
        src_refs, out_refs = refs[:n], refs[n:2 * n]
        send_sems, recv_sems, local_sems = refs[2 * n:]
        x, y, c = lax.axis_index("x"), lax.axis_index("y"), lax.axis_index("c")
        flat = lambda px, py, pc: 4 * px + 2 * py + pc
        me, sibling = (x, y, c), (x, y, 1 - c)
        chips = [(1 - x, y), (x, 1 - y), (1 - x, 1 - y)]

        def copy(i, k, block, to, own):
            return pltpu.make_async_remote_copy(
                src_ref=src_refs[i] if own else out_refs[i].at[flat(*block)], dst_ref=out_refs[i].at[flat(*block)],
                send_sem=send_sems.at[i * (N_DEV - 1) + k], recv_sem=recv_sems.at[i * (N_DEV - 1) + k],
                device_id=to, device_id_type=pl.DeviceIdType.MESH)

        local, sent = [], []
        for i in range(n):
            local.append(pltpu.make_async_copy(src_refs[i], out_refs[i].at[flat(*me)], local_sems.at[i]))
            local[-1].start()
            sent.append(copy(i, 0, me, sibling, True))
            sent += [copy(i, 1 + j, me, (*chip, c), True) for j, chip in enumerate(chips)]
            for cp in sent[-4:]:
                cp.start()
        for i in range(n):
            for j, chip in enumerate(chips):
                copy(i, 1 + j, (*chip, c), me, True).wait_recv()
                sent.append(copy(i, 4 + j, (*chip, c), sibling, False))
                sent[-1].start()
        for i in range(n):
            copy(i, 0, sibling, me, True).wait_recv()
            for j, chip in enumerate(chips):
                copy(i, 4 + j, (*chip, 1 - c), me, False).wait_recv()
        for cp in sent:
            cp.wait_send()
        for cp in local:
            cp.wait()

    any_spec = pl.BlockSpec(memory_space=pl.ANY)
    return pl.pallas_call(
        body, out_shape=[jax.ShapeDtypeStruct((N_DEV,) + s.shape, s.dtype) for s in srcs],
        in_specs=[any_spec] * n, out_specs=[any_spec] * n, scratch_shapes=_exchange_sems(n), name=name)(*srcs)


def _pack(arrays, dtype, lead=None):
    flat = [a.astype(dtype).reshape((lead, -1) if lead else (-1,)) for a in arrays]
    cat = jnp.concatenate(flat, axis=-1)
    n = cat.shape[-1]
    rows = -(-n // (PACK_COLS * SUBLANES)) * SUBLANES
    pad = rows * PACK_COLS - n
    if pad:
        cat = jnp.pad(cat, ((0, 0), (0, pad)) if lead else ((0, pad),))
    return cat.reshape((lead, rows, PACK_COLS) if lead else (rows, PACK_COLS))


def _unpack(buf, shapes, lead=None):
    flat = buf.reshape((lead, -1) if lead else (-1,))
    out, off = [], 0
    for shp in shapes:
        n = int(np.prod(shp))
        piece = flat[..., off:off + n]
        out.append(piece.reshape(((lead,) if lead else ()) + tuple(shp)))
        off += n
    return out


def _to_slabs(full, axis):
    shp = full.shape
    split = full.reshape(shp[:axis] + (N_DEV, shp[axis] // N_DEV) + shp[axis + 1:])
    return jnp.moveaxis(split, axis, 0)


def _from_slabs(slabs, axis):
    moved = jnp.moveaxis(slabs, 0, axis)
    shp = moved.shape
    return moved.reshape(shp[:axis] + (shp[axis] * shp[axis + 1],) + shp[axis + 2:])


def _adamw_block(r, c):
    if r % 8:
        return r, 256
    br = min(r, max(8, ADAMW_TILE * PACK_COLS // c))
    while r % br:
        br //= 2
    return br, c


def _adamw(name, partials, w, m, v):
    n_lead, r, c = w.shape
    br, bc = _adamw_block(r, c)
    ni, nj = r // br, c // bc
    c1 = 1.0 - ADAM_B1 ** ADAM_STEP
    c2 = 1.0 - ADAM_B2 ** ADAM_STEP

    def kern(*refs):
        p_refs = refs[:n_lead]
        w_ref, m_ref, v_ref, g_ref, d_ref, nm_ref, nv_ref = refs[n_lead:]
        layer = pl.program_id(0)
        g = None
        for l, p_ref in enumerate(p_refs):
            gl = p_ref[0].astype(F32)
            for s in range(1, N_DEV):
                gl = gl + p_ref[s].astype(F32)
            g = gl if g is None else jnp.where(layer == l, gl, g)
        nm = ADAM_B1 * m_ref[...] + (1.0 - ADAM_B1) * g
        nv = ADAM_B2 * v_ref[...] + (1.0 - ADAM_B2) * jnp.square(g)
        g_ref[...] = g
        nm_ref[...] = nm
        nv_ref[...] = nv
        d_ref[...] = -ADAM_LR * ((nm / c1) / (jnp.sqrt(nv / c2) + ADAM_EPS) + ADAM_WD * w_ref[...])

    def part_spec(mine):
        def index(l, i, j):
            before, after = l < mine, l > mine
            return (0, jnp.where(before, 0, jnp.where(after, ni - 1, i)), jnp.where(before, 0, jnp.where(after, nj - 1, j)))
        return pl.BlockSpec((N_DEV, br, bc), index)

    blk = pl.BlockSpec((None, br, bc), lambda l, i, j: (l, i, j))
    return pl.pallas_call(
        kern, grid=(n_lead, ni, nj),
        in_specs=[part_spec(l) for l in range(n_lead)] + [blk, blk, blk],
        out_specs=[blk] * 4, out_shape=[jax.ShapeDtypeStruct(w.shape, F32)] * 4,
        compiler_params=pltpu.CompilerParams(dimension_semantics=("arbitrary",) * 3),
        name=name)(*partials, w, m, v)


def _as_rows(a, lead):
    return a.reshape(a.shape[:lead] + (-1, a.shape[-1]))


def kernel(x, p, norm1_g, w_in, sg_ln_g, sg_ln_b, sg_w, sg_b, gla_w_a2, gla_b_a, gla_norm_g, att_rel_bias, conv_dw_w, conv_dw_b, conv_ln_g, conv_ln_b, w_branch, w_gate, b_gate, w_out, norm2_g, w_ff1, w_ff2, norm3_g, w_ple_gate, b_ple_gate, w_ple, final_g, loss_target, m_norm1_g, m_w_in, m_sg_ln_g, m_sg_ln_b, m_sg_w, m_sg_b, m_gla_w_a2, m_gla_b_a, m_gla_norm_g, m_att_rel_bias, m_conv_dw_w, m_conv_dw_b, m_conv_ln_g, m_conv_ln_b, m_w_branch, m_w_gate, m_b_gate, m_w_out, m_norm2_g, m_w_ff1, m_w_ff2, m_norm3_g, m_w_ple_gate, m_b_ple_gate, m_w_ple, m_final_g, v_norm1_g, v_w_in, v_sg_ln_g, v_sg_ln_b, v_sg_w, v_sg_b, v_gla_w_a2, v_gla_b_a, v_gla_norm_g, v_att_rel_bias, v_conv_dw_w, v_conv_dw_b, v_conv_ln_g, v_conv_ln_b, v_w_branch, v_w_gate, v_b_gate, v_w_out, v_norm2_g, v_w_ff1, v_w_ff2, v_norm3_g, v_w_ple_gate, v_b_ple_gate, v_w_ple, v_final_g):
    args = locals()
    wts = {n: args[n] for n in WEIGHTS}
    mom = {n: args['m_' + n] for n in WEIGHTS}
    var = {n: args['v_' + n] for n in WEIGHTS}

    local = {d_name: dict(d, w_in=jnp.swapaxes(d['w_in'], 1, 2))
             for d_name, d in (("w", wts), ("m", mom), ("v", var))}
    shards = [{n: local["w"][n][i].astype(_MXU_DTYPE) for n in MXU_WEIGHTS} for i in range(DEPTH)]

    first_w_in, vec = _gather_via_sibling("gather_first_weights",
                                          [shards[0]['w_in'], _pack([wts[n] for n in VEC_WEIGHTS], F32)])
    vec_full = {n: _from_slabs(slabs, SHARD_AXIS[n])
                for n, slabs in zip(VEC_WEIGHTS, _unpack(vec, [wts[n].shape for n in VEC_WEIGHTS], lead=N_DEV))}
    small_names = [n for n in WEIGHTS if n not in MXU_WEIGHTS and n != 'final_g']
    layers = [{n: (vec_full[n] if n in vec_full else wts[n])[i] for n in small_names} for i in range(DEPTH)]
    _land(layers[0], ['w_in'], [first_w_in])

    def vec_items(small):
        stacked = [jnp.stack([g[n] for g in small]) for n in VEC_WEIGHTS]
        return [_scatter_item(_pack([_to_slabs(a, SHARD_AXIS[n]) for n, a in zip(VEC_WEIGHTS, stacked)], F32,
                                    lead=N_DEV))]

    seq = x.shape[1:]
    loss, grad_x, dgf, small, parts, (vec_parts,) = _local_step(
        x.reshape(seq), p.reshape(DEPTH, seq[0], PLE_DIM), loss_target.reshape(seq), final_g, layers, shards, vec_items)
    loss = lax.psum(loss, ("x", "y", "c"))

    grads = {n: jnp.stack([small[i][n] for i in range(DEPTH)]) for n in REPLICATED if n != 'final_g'}
    grads['final_g'] = dgf
    repl_parts, = _gather_via_sibling("gather_replicated_grads", [_pack([grads[n] for n in REPLICATED], F32)])

    results = {}
    for n in MXU_WEIGHTS:
        w3, m3, v3 = (_as_rows(local[d][n], 1) for d in ("w", "m", "v"))
        outs = _adamw("adamw_" + n, [parts[i][n].reshape((N_DEV,) + w3.shape[1:]) for i in range(DEPTH)], w3, m3, v3)
        for kind, a in zip(("grad", "delta", "new_m", "new_v"), outs):
            a = a.reshape(local["w"][n].shape)
            results[kind, n] = jnp.swapaxes(a, 1, 2) if n == 'w_in' else a
    for names, part, call in ((VEC_WEIGHTS, vec_parts, "adamw_vec"), (REPLICATED, repl_parts, "adamw_replicated")):
        packed = [_pack([d[n] for n in names], F32)[None] for d in (wts, mom, var)]
        outs = _adamw(call, [part], *packed)
        for kind, buf in zip(("grad", "delta", "new_m", "new_v"), outs):
            for n, a in zip(names, _unpack(buf[0], [wts[n].shape for n in names])):
                results[kind, n] = a
    return (loss, grad_x[None]) + tuple(results[kind, n] for kind in ("grad", "delta", "new_m", "new_v")
                                        for n in WEIGHTS)
```

```python
import numpy as np
import jax
import jax.numpy as jnp
from jax import lax
from jax.experimental import pallas as pl
from jax.experimental.pallas import tpu as pltpu

F32 = jnp.float32
_MXU_DTYPE = jnp.bfloat16
GRAD_DTYPE = jnp.bfloat16

N_DEV = 8
D_MODEL = 1024
DEPTH = 2
CHUNK = 64
PLE_DIM = 256
BW = 512
SG_BLOCK = 128
SG_GROUPS = 4
GLA_HEADS = 4
GLA_DK = 64
GLA_DV = 128
GLA_RANK = 16
GLA_TAU = 16.0
ATT_HEADS = 8
ATT_HD = 64
ATT_BAND = 9
BAND = ATT_BAND * CHUNK
MAX_REL = 256
REL_TABLE = CHUNK + MAX_REL
CONV_K = 31
CONV_HALO = 32
D_FF = 4096
EPS = 1e-6
NEG_INF = -1e30

IN_GROUPS = (("A", 0, 1024), ("B", 1024, 1536), ("a", 2560, 16), ("C", 2576, 1536), ("D", 4112, 1024))
IN_GROUPS_BWD = (IN_GROUPS[0], ("Ba", 1024, 1552), ("Cq", 2576, 512), ("Ckv", 3088, 1024), IN_GROUPS[4])
IN_COLS = 5136

ADAM_LR = 0.001
ADAM_B1 = 0.9
ADAM_B2 = 0.999
ADAM_EPS = 1e-08
ADAM_WD = 0.01
ADAM_STEP = 10

ADAMW_TILE = 128
PACK_COLS = 1024
VMEM_LIMIT_MB = 56

_NN = (((1,), (0,)), ((), ()))
_NT = (((1,), (1,)), ((), ()))
_TN = (((0,), (0,)), ((), ()))

WEIGHTS = ['norm1_g', 'w_in', 'sg_ln_g', 'sg_ln_b', 'sg_w', 'sg_b', 'gla_w_a2', 'gla_b_a', 'gla_norm_g',
           'att_rel_bias', 'conv_dw_w', 'conv_dw_b', 'conv_ln_g', 'conv_ln_b', 'w_branch', 'w_gate', 'b_gate',
           'w_out', 'norm2_g', 'w_ff1', 'w_ff2', 'norm3_g', 'w_ple_gate', 'b_ple_gate', 'w_ple', 'final_g']
SHARD_AXIS = {'w_in': 2, 'gla_w_a2': 2, 'att_rel_bias': 2, 'conv_dw_w': 2, 'w_branch': 3, 'w_gate': 2,
              'b_gate': 2, 'w_out': 1, 'w_ff1': 2, 'w_ff2': 1, 'w_ple_gate': 1, 'w_ple': 2}
MXU_WEIGHTS = ('w_in', 'w_branch', 'w_gate', 'w_out', 'w_ff1', 'w_ff2', 'w_ple_gate', 'w_ple')
VEC_WEIGHTS = ('gla_w_a2', 'att_rel_bias', 'conv_dw_w', 'b_gate')
SHARDED = tuple(n for n in WEIGHTS if n in SHARD_AXIS)
REPLICATED = tuple(n for n in WEIGHTS if n not in SHARD_AXIS)


def _mm(a, b, dims=_NN):
    return lax.dot_general(a.astype(_MXU_DTYPE), b.astype(_MXU_DTYPE), dims, preferred_element_type=F32)


def _split3(x):
    x1 = x.astype(jnp.bfloat16)
    r1 = x - x1.astype(F32)
    x2 = r1.astype(jnp.bfloat16)
    x3 = (r1 - x2.astype(F32)).astype(jnp.bfloat16)
    return x1, x2, x3


def _mm_exact_rhs(m, x, dims=_NN):
    return sum(lax.dot_general(m, xi, dims, preferred_element_type=F32) for xi in _split3(x))


def _mm_exact_lhs(x, m, dims=_NN):
    return sum(lax.dot_general(xi, m, dims, preferred_element_type=F32) for xi in _split3(x))


def _sigmoid(x):
    return 1.0 / (1.0 + jnp.exp(-x))


def _gelu(x):
    c = 0.7978845608028654
    t = jnp.tanh(c * (x + 0.044715 * x * x * x))
    return 0.5 * x * (1.0 + t), t


def _gelu_grad(x, t):
    c = 0.7978845608028654
    return 0.5 * (1.0 + t) + 0.5 * x * (1.0 - t * t) * c * (1.0 + 3.0 * 0.044715 * x * x)


def _rms_stat(h):
    return lax.rsqrt(jnp.mean(h * h, axis=-1, keepdims=True) + EPS)


def _rms_bwd(dy, h, g, r):
    hh = h * r
    dhh = dy * g
    dh = r * (dhh - hh * jnp.mean(dhh * hh, axis=-1, keepdims=True))
    return dh, jnp.sum(dy * hh, axis=0, keepdims=True)


def _ln_fwd(x, g, b):
    mu = jnp.mean(x, axis=-1, keepdims=True)
    xc = x - mu
    rs = lax.rsqrt(jnp.mean(xc * xc, axis=-1, keepdims=True) + EPS)
    xh = xc * rs
    return xh * g + b, xh, rs


def _ln_bwd(dy, xh, rs, g):
    dxh = dy * g
    dx = rs * (dxh - jnp.mean(dxh, axis=-1, keepdims=True) - xh * jnp.mean(dxh * xh, axis=-1, keepdims=True))
    return dx, jnp.sum(dy * xh, axis=0, keepdims=True), jnp.sum(dy, axis=0, keepdims=True)


def _row_call(name, body, nt, rows=(), halos=(), res=(), outs=(), accs=(), scratch=(), reverse=False, exchange=()):
    def pos(i):
        return (nt - 1 - i) if reverse else i

    def lead(ndim, f):
        return lambda i: (f(pos(i)),) + (0,) * (ndim - 1)

    in_specs, operands = [], []
    for a, tile in rows:
        in_specs.append(pl.BlockSpec((tile,) + a.shape[1:], lead(a.ndim, lambda t: t)))
        operands.append(a)
    for a, blk, per, side in halos:
        last = a.shape[0] // blk - 1
        delta = {'prev2': -2, 'prev': -1, 'next': per}[side]
        f = lambda t, per=per, last=last, delta=delta: jnp.clip(t * per + delta, 0, last)
        in_specs.append(pl.BlockSpec((blk,) + a.shape[1:], lead(a.ndim, f)))
        operands.append(a)
    for a in res:
        in_specs.append(pl.BlockSpec(a.shape, lambda i, nd=a.ndim: (0,) * nd, pipeline_mode=pl.Buffered(1)))
        operands.append(a)
    out_specs, out_shape = [], []
    for shape, dtype, tile in outs:
        out_specs.append(pl.BlockSpec((tile,) + tuple(shape[1:]), lead(len(shape), lambda t: t)))
        out_shape.append(jax.ShapeDtypeStruct(tuple(shape), dtype))
    for shape in accs:
        out_specs.append(pl.BlockSpec(tuple(shape), lambda i, nd=len(shape): (0,) * nd))
        out_shape.append(jax.ShapeDtypeStruct(tuple(shape), F32))
    nx = len(exchange)
    any_spec = pl.BlockSpec(memory_space=pl.ANY)
    for it in exchange:
        in_specs.append(any_spec)
        operands.append(it['src'])
        out_specs.append(any_spec)
        out_shape.append(jax.ShapeDtypeStruct(it['out'], it['src'].dtype))
    sizes = (len(rows), len(halos), len(res), nx, len(outs), len(accs), nx, len(scratch), 3 if nx else 0)

    def kern(*refs):
        i = pl.program_id(0)
        groups, at = [], 0
        for n in sizes:
            groups.append(refs[at:at + n])
            at += n
        row_refs, halo_refs, res_refs, x_src, out_refs, acc_refs, x_dst, scr_refs, sems = groups

        @pl.when(i == 0)
        def _():
            for r in tuple(acc_refs) + tuple(scr_refs):
                r[...] = jnp.zeros(r.shape, r.dtype)
            if nx:
                _exchange_copies(exchange, x_src, x_dst, sems, start=True)

        body(pos(i), row_refs, halo_refs, res_refs, out_refs, acc_refs, scr_refs)

        if nx:
            @pl.when(i == nt - 1)
            def _():
                _exchange_copies(exchange, x_src, x_dst, sems, start=False)

    result = pl.pallas_call(
        kern, grid=(nt,), in_specs=in_specs, out_specs=out_specs, out_shape=out_shape,
        scratch_shapes=[pltpu.VMEM(tuple(s), d) for s, d in scratch] + (_exchange_sems(nx) if nx else []),
        compiler_params=pltpu.CompilerParams(dimension_semantics=("arbitrary",),
                                             vmem_limit_bytes=VMEM_LIMIT_MB << 20),
        name=name)(*operands)
    return tuple(result)


def _tn_call(name, a, b, k, n, nblk=1, a_col=False, b_col=True, b_off=0, out='cols', tile=2048):
    tile = min(tile, a.shape[0])
    nt = a.shape[0] // tile
    if out == 'cols':
        o_shape, o_spec = (k, nblk * n), pl.BlockSpec((k, n), lambda j, t: (0, j))
    elif out == 'rows':
        o_shape, o_spec = (nblk * k, n), pl.BlockSpec((k, n), lambda j, t: (j, 0))
    else:
        o_shape, o_spec = (nblk, k, n), pl.BlockSpec((None, k, n), lambda j, t: (j, 0, 0))

    def kern(a_ref, b_ref, o_ref, acc):
        @pl.when(pl.program_id(1) == 0)
        def _():
            acc[...] = jnp.zeros(acc.shape, acc.dtype)

        acc[...] += lax.dot_general(a_ref[...], b_ref[...], _TN, preferred_element_type=F32)

        @pl.when(pl.program_id(1) == nt - 1)
        def _():
            o_ref[...] = acc[...].astype(o_ref.dtype)

    return pl.pallas_call(
        kern, grid=(nblk, nt),
        in_specs=[pl.BlockSpec((tile, k), (lambda j, t: (t, j)) if a_col else (lambda j, t: (t, 0))),
                  pl.BlockSpec((tile, n), (lambda j, t: (t, j + b_off)) if b_col else (lambda j, t: (t, b_off)))],
        out_specs=o_spec, out_shape=jax.ShapeDtypeStruct(o_shape, GRAD_DTYPE),
        scratch_shapes=[pltpu.VMEM((k, n), F32)],
        compiler_params=pltpu.CompilerParams(dimension_semantics=("arbitrary", "arbitrary"),
                                             vmem_limit_bytes=VMEM_LIMIT_MB << 20),
        name=name)(a, b)


def _inproj_fwd(tag, h, g1, w_in_t, tm=512, exchange=()):
    t_len = h.shape[0]

    def body(t, rows, halos, res, outs, accs, scr):
        hv = rows[0][...]
        xn = (hv * _rms_stat(hv) * res[0][...]).astype(_MXU_DTYPE)
        outs[0][...] = xn
        for o, (_, s0, n) in zip(outs[1:], IN_GROUPS):
            o[...] = lax.dot_general(xn, res[1][s0:s0 + n, :], _NT, preferred_element_type=F32)

    outs = [((t_len, D_MODEL), _MXU_DTYPE, tm)] + [((t_len, n), F32, tm) for _, _, n in IN_GROUPS]
    return _row_call("inproj_fwd" + tag, body, t_len // tm, rows=[(h, tm)], res=[g1, w_in_t], outs=outs,
                     exchange=exchange)


def _sg_mask():
    row = lax.broadcasted_iota(jnp.int32, (SG_BLOCK, SG_BLOCK), 0)
    col = lax.broadcasted_iota(jnp.int32, (SG_BLOCK, SG_BLOCK), 1)
    return jnp.logical_or(row >= CHUNK, col < CHUNK)


def _sg_forward_parts(pa, lg, lb, w_ref, bt):
    tm = pa.shape[0]
    nb = tm // SG_BLOCK
    su, sv = pa[:, :BW], pa[:, BW:]
    u, tu = _gelu(su)
    gv, tv = _gelu(sv)
    vn, xh, rs = _ln_fwd(gv, lg, lb)
    mask = _sg_mask()
    wms, xs, ms = [], [], []
    for g in range(SG_GROUPS):
        wm = jnp.where(mask, w_ref[g], 0.0).astype(_MXU_DTYPE)
        xg = jnp.concatenate([vn[b * SG_BLOCK:(b + 1) * SG_BLOCK, g * 128:(g + 1) * 128] for b in range(nb)], axis=1)
        xg = xg.astype(_MXU_DTYPE)
        ms.append(lax.dot_general(wm, xg, _NN, preferred_element_type=F32) + bt[:, g:g + 1])
        wms.append(wm)
        xs.append(xg)
    mixed = _sg_unfold(ms, nb)
    return su, sv, u, tu, tv, xh, rs, wms, xs, mixed


def _sg_unfold(per_group, nb):
    return jnp.concatenate(
        [jnp.concatenate([per_group[g][:, b * 128:(b + 1) * 128] for g in range(SG_GROUPS)], axis=1)
         for b in range(nb)], axis=0)


def _sg_fwd(tag, proj_a, lg, lb, sg_w, sg_bt, tm=512):
    t_len = proj_a.shape[0]

    def body(t, rows, halos, res, outs, accs, scr):
        parts = _sg_forward_parts(rows[0][...], res[0][...], res[1][...], res[2], res[3][...])
        outs[0][...] = (parts[2] * parts[-1]).astype(_MXU_DTYPE)

    return _row_call("sg_fwd" + tag, body, t_len // tm, rows=[(proj_a, tm)], res=[lg, lb, sg_w, sg_bt],
                     outs=[((t_len, BW), _MXU_DTYPE, tm)])[0]


def _sg_bwd(tag, proj_a, dy, lg, lb, sg_w, sg_bt, tm=512):
    t_len = proj_a.shape[0]
    nb = tm // SG_BLOCK

    def body(t, rows, halos, res, outs, accs, scr):
        lgv = res[0][...]
        su, sv, u, tu, tv, xh, rs, wms, xs, mixed = _sg_forward_parts(rows[0][...], lgv, res[1][...], res[2], res[3][...])
        dyv = rows[1][...]
        dsu = dyv * mixed * _gelu_grad(su, tu)
        dmixed = dyv * u
        mask = _sg_mask()
        dxs, dbs = [], []
        for g in range(SG_GROUPS):
            dm = jnp.concatenate([dmixed[b * SG_BLOCK:(b + 1) * SG_BLOCK, g * 128:(g + 1) * 128] for b in range(nb)],
                                 axis=1)
            dmb = dm.astype(_MXU_DTYPE)
            dw = lax.dot_general(dmb, xs[g], _NT, preferred_element_type=F32)
            accs[0][g] += jnp.where(mask, dw, 0.0)
            dbs.append(jnp.sum(dm, axis=1, keepdims=True))
            dxs.append(lax.dot_general(wms[g], dmb, _TN, preferred_element_type=F32))
        accs[1][...] += jnp.concatenate(dbs, axis=1)
        dvn = _sg_unfold(dxs, nb)
        dgv, dlg, dlb = _ln_bwd(dvn, xh, rs, lgv)
        accs[2][...] += dlg
        accs[3][...] += dlb
        dsv = dgv * _gelu_grad(sv, tv)
        outs[0][...] = jnp.concatenate([dsu, dsv], axis=1).astype(_MXU_DTYPE)

    return _row_call("sg_bwd" + tag, body, t_len // tm, rows=[(proj_a, tm), (dy, tm)], res=[lg, lb, sg_w, sg_bt],
                     outs=[((t_len, 2 * BW), _MXU_DTYPE, tm)],
                     accs=[(SG_GROUPS, SG_BLOCK, SG_BLOCK), (SG_BLOCK, SG_GROUPS), (1, BW), (1, BW)])


def _chunk_matrix(tm, kind):
    row = lax.broadcasted_iota(jnp.int32, (tm, tm), 0)
    col = lax.broadcasted_iota(jnp.int32, (tm, tm), 1)
    same = lax.shift_right_logical(row, 6) == lax.shift_right_logical(col, 6)
    if kind == 'cumsum':
        same = jnp.logical_and(same, row >= col)
    elif kind == 'revsum':
        same = jnp.logical_and(same, row <= col)
    return same.astype(jnp.bfloat16)


def _gla_gate(pa, wa2, ba):
    z = _mm(pa, wa2) + ba
    log_a = (jnp.minimum(z, 0.0) - jnp.log(1.0 + jnp.exp(-jnp.abs(z)))) * (1.0 / GLA_TAU)
    return z, log_a


def _gla_decay(pb, log_a):
    tm = pb.shape[0]
    cum = _mm_exact_rhs(_chunk_matrix(tm, 'cumsum'), log_a)
    tot = _mm_exact_rhs(_chunk_matrix(tm, 'total'), log_a)
    w = jnp.exp(tot - cum)
    return w, pb[:, 256:512] * w, jnp.exp(tot)


def _per_head(fn):
    return jnp.concatenate([fn(h) for h in range(GLA_HEADS)], axis=1)


def _gla_read(qs, sb, c):
    rows = slice(c * CHUNK, (c + 1) * CHUNK)
    return _per_head(lambda h: lax.dot_general(qs[rows, h * 64:(h + 1) * 64], sb[:, h * 64:(h + 1) * 64], _NT,
                                               preferred_element_type=F32))


def _gla_fwd(tag, proj_b, proj_a, wa2, ba, ng, tm=512, exchange=()):
    t_len = proj_b.shape[0]
    cpt = tm // CHUNK

    def body(t, rows, halos, res, outs, accs, scr):
        pb = rows[0][...]
        _, log_a = _gla_gate(rows[1][...], res[0][...], res[1][...])
        _, kd, dec = _gla_decay(pb, log_a)
        kdb = kd.astype(_MXU_DTYPE)
        vb = pb[:, 512:1024].astype(_MXU_DTYPE)
        qs = (pb[:, 0:256] * (GLA_DK ** -0.5)).astype(_MXU_DTYPE)
        uts = []
        for c in range(cpt):
            rs = slice(c * CHUNK, (c + 1) * CHUNK)
            uts.append(_per_head(lambda h: lax.dot_general(vb[rs, h * 128:(h + 1) * 128], kdb[rs, h * 64:(h + 1) * 64],
                                                           _TN, preferred_element_type=F32)))
        s_new = scr[0][...]
        o = []
        for c in range(cpt):
            s_new = dec[c * CHUNK:c * CHUNK + 1] * s_new + uts[c]
            outs[1][c] = s_new
            o.append(_gla_read(qs, s_new.astype(_MXU_DTYPE), c))
        scr[0][...] = s_new
        o = jnp.concatenate(o, axis=0)
        on = _per_head(lambda h: o[:, h * 128:(h + 1) * 128] * lax.rsqrt(
            jnp.mean(jnp.square(o[:, h * 128:(h + 1) * 128]), axis=-1, keepdims=True) + EPS))
        r = pb[:, 1024:1536]
        outs[0][...] = (on * res[2][...] * (r * _sigmoid(r))).astype(_MXU_DTYPE)

    return _row_call("gla_fwd" + tag, body, t_len // tm, rows=[(proj_b, tm), (proj_a, tm)], res=[wa2, ba, ng],
                     outs=[((t_len, BW), _MXU_DTYPE, tm), ((t_len // CHUNK, GLA_DV, 256), F32, cpt)],
                     scratch=[((GLA_DV, 256), F32)], exchange=exchange)


def _gla_bwd(tag, proj_b, proj_a, dy, states, wa2, ba, ng, tm=512):
    t_len = proj_b.shape[0]
    cpt = tm // CHUNK

    def body(t, rows, halos, res, outs, accs, scr):
        pb = rows[0][...]
        pa = rows[1][...]
        dyv = rows[2][...]
        st_ref = rows[3]
        wa2v = res[0][...]
        z, log_a = _gla_gate(pa, wa2v, res[1][...])
        ngv = res[2][...]
        w, kd, dec = _gla_decay(pb, log_a)
        kdb = kd.astype(_MXU_DTYPE)
        vb = pb[:, 512:1024].astype(_MXU_DTYPE)
        qs = (pb[:, 0:256] * (GLA_DK ** -0.5)).astype(_MXU_DTYPE)
        chunks = [slice(c * CHUNK, (c + 1) * CHUNK) for c in range(cpt)]
        sbs = [st_ref[c].astype(_MXU_DTYPE) for c in range(cpt)]
        o = jnp.concatenate([_gla_read(qs, sbs[c], c) for c in range(cpt)], axis=0)
        r = pb[:, 1024:1536]
        sig = _sigmoid(r)
        sil = r * sig
        dos, ons = [], []
        for h in range(GLA_HEADS):
            hs = slice(h * 128, (h + 1) * 128)
            oh = o[:, hs]
            rstd = lax.rsqrt(jnp.mean(oh * oh, axis=-1, keepdims=True) + EPS)
            on = oh * rstd
            don = dyv[:, hs] * ngv[:, hs] * sil[:, hs]
            dos.append(rstd * (don - on * jnp.mean(don * on, axis=-1, keepdims=True)))
            ons.append(on)
        on = jnp.concatenate(ons, axis=1)
        accs[2][...] += jnp.sum(dyv * on * sil, axis=0, keepdims=True)
        dr = dyv * on * ngv * (sig * (1.0 + r * (1.0 - sig)))
        dob = jnp.concatenate(dos, axis=1).astype(_MXU_DTYPE)
        reads, dqs = [], []
        for c, rs in enumerate(chunks):
            reads.append(_per_head(lambda h: lax.dot_general(dob[rs, h * 128:(h + 1) * 128], qs[rs, h * 64:(h + 1) * 64],
                                                             _TN, preferred_element_type=F32)))
            dqs.append(_per_head(lambda h: lax.dot_general(dob[rs, h * 128:(h + 1) * 128], sbs[c][:, h * 64:(h + 1) * 64],
                                                           _NN, preferred_element_type=F32)))
        dst = scr[0][...]
        dubs, ddecs = [None] * cpt, [None] * cpt
        for c in reversed(range(cpt)):
            dst_tot = dst + reads[c]
            s_prev = st_ref[c - 1] if c > 0 else jnp.where(t > 0, halos[0][0], 0.0)
            ddecs[c] = jnp.broadcast_to(jnp.sum(dst_tot * s_prev, axis=0, keepdims=True), (CHUNK, 256))
            dst = dec[c * CHUNK:c * CHUNK + 1] * dst_tot
            dubs[c] = dst_tot.astype(_MXU_DTYPE)
        scr[0][...] = dst
        dkd = jnp.concatenate(
            [_per_head(lambda h: lax.dot_general(vb[rs, h * 128:(h + 1) * 128], dubs[c][:, h * 64:(h + 1) * 64], _NN,
                                                 preferred_element_type=F32)) for c, rs in enumerate(chunks)], axis=0)
        dv = jnp.concatenate(
            [_per_head(lambda h: lax.dot_general(kdb[rs, h * 64:(h + 1) * 64], dubs[c][:, h * 64:(h + 1) * 64], _NT,
                                                 preferred_element_type=F32)) for c, rs in enumerate(chunks)], axis=0)
        e = dkd * kd
        dtot = _mm_exact_rhs(_chunk_matrix(tm, 'total'), e) + jnp.concatenate(ddecs, axis=0) * dec
        last = (lax.broadcasted_iota(jnp.int32, e.shape, 0) & (CHUNK - 1)) == CHUNK - 1
        dla = _mm_exact_rhs(_chunk_matrix(tm, 'revsum'), jnp.where(last, dtot - e, -e))
        dz = dla * (1.0 / GLA_TAU) * _sigmoid(-z)
        dzb = dz.astype(_MXU_DTYPE)
        dq = jnp.concatenate(dqs, axis=0) * (GLA_DK ** -0.5)
        da = lax.dot_general(dzb, wa2v.astype(_MXU_DTYPE), _NT, preferred_element_type=F32)
        outs[0][...] = jnp.concatenate([dq, dkd * w, dv, dr, da], axis=1).astype(_MXU_DTYPE)
        accs[0][...] += lax.dot_general(pa.astype(_MXU_DTYPE), dzb, _TN, preferred_element_type=F32)
        accs[1][...] += jnp.sum(dz, axis=0, keepdims=True)

    return _row_call("gla_bwd" + tag, body, t_len // tm,
                     rows=[(proj_b, tm), (proj_a, tm), (dy, tm), (states, cpt)],
                     halos=[(states, 1, cpt, 'prev')], res=[wa2, ba, ng],
                     outs=[((t_len, 1536 + GLA_RANK), _MXU_DTYPE, tm)],
                     accs=[(GLA_RANK, 256), (1, 256), (1, BW)], scratch=[((GLA_DV, 256), F32)], reverse=True)


ATT_TM = 256
ATT_KEYS = ATT_TM + (ATT_BAND - 1) * CHUNK


def _rel_index():
    l_idx = np.arange(CHUNK)[:, None]
    m_idx = np.arange(BAND)[None, :]
    rel = l_idx + (ATT_BAND - 1) * CHUNK - m_idx
    return jnp.asarray((np.clip(rel, -(CHUNK - 1), MAX_REL) + (CHUNK - 1)).reshape(1, CHUNK * BAND), jnp.int32)


BIAS_COLS = 4096


def _bias_expand(tag, rel_bias):
    n = CHUNK * BAND

    def kern(rel_ref, idx_ref, o_ref):
        onehot = (lax.broadcasted_iota(jnp.int32, (REL_TABLE, BIAS_COLS), 0) == idx_ref[...]).astype(jnp.bfloat16)
        o_ref[...] = _mm_exact_lhs(rel_ref[...], onehot)

    return pl.pallas_call(
        kern, grid=(n // BIAS_COLS,),
        in_specs=[pl.BlockSpec((ATT_HEADS, REL_TABLE), lambda i: (0, 0)), pl.BlockSpec((1, BIAS_COLS), lambda i: (0, i))],
        out_specs=pl.BlockSpec((ATT_HEADS, BIAS_COLS), lambda i: (0, i)),
        out_shape=jax.ShapeDtypeStruct((ATT_HEADS, n), F32), name="bias_expand" + tag)(rel_bias, _rel_index())


def _bias_tile(tag, bias):
    per = ATT_TM // CHUNK

    def kern(b_ref, o_ref):
        bv = b_ref[...]
        for j in range(per):
            parts = [jnp.full((CHUNK, j * CHUNK), NEG_INF, F32)] if j else []
            parts.append(bv)
            if j < per - 1:
                parts.append(jnp.full((CHUNK, (per - 1 - j) * CHUNK), NEG_INF, F32))
            o_ref[j * CHUNK:(j + 1) * CHUNK, :] = jnp.concatenate(parts, axis=1)

    return pl.pallas_call(
        kern, grid=(ATT_HEADS,), in_specs=[pl.BlockSpec((None, CHUNK, BAND), lambda h: (h, 0, 0))],
        out_specs=pl.BlockSpec((None, ATT_TM, ATT_KEYS), lambda h: (h, 0, 0)),
        out_shape=jax.ShapeDtypeStruct((ATT_HEADS, ATT_TM, ATT_KEYS), F32), name="bias_tile" + tag)(bias)


def _bias_untile(tag, dbias):
    per = ATT_TM // CHUNK

    def kern(d_ref, o_ref):
        acc = d_ref[0:CHUNK, 0:BAND]
        for j in range(1, per):
            acc = acc + d_ref[j * CHUNK:(j + 1) * CHUNK, j * CHUNK:j * CHUNK + BAND]
        o_ref[...] = acc

    return pl.pallas_call(
        kern, grid=(ATT_HEADS,), in_specs=[pl.BlockSpec((None, ATT_TM, ATT_KEYS), lambda h: (h, 0, 0))],
        out_specs=pl.BlockSpec((None, CHUNK, BAND), lambda h: (h, 0, 0)),
        out_shape=jax.ShapeDtypeStruct((ATT_HEADS, CHUNK, BAND), F32), name="bias_untile" + tag)(dbias)


def _bias_reduce(tag, dbias):
    n = CHUNK * BAND

    def kern(db_ref, idx_ref, o_ref):
        @pl.when(pl.program_id(0) == 0)
        def _():
            o_ref[...] = jnp.zeros(o_ref.shape, o_ref.dtype)

        onehot = (lax.broadcasted_iota(jnp.int32, (REL_TABLE, BIAS_COLS), 0) == idx_ref[...]).astype(jnp.bfloat16)
        o_ref[...] += _mm_exact_lhs(db_ref[...], onehot, _NT)

    return pl.pallas_call(
        kern, grid=(n // BIAS_COLS,),
        in_specs=[pl.BlockSpec((ATT_HEADS, BIAS_COLS), lambda i: (0, i)), pl.BlockSpec((1, BIAS_COLS), lambda i: (0, i))],
        out_specs=pl.BlockSpec((ATT_HEADS, REL_TABLE), lambda i: (0, 0)),
        out_shape=jax.ShapeDtypeStruct((ATT_HEADS, REL_TABLE), F32),
        compiler_params=pltpu.CompilerParams(dimension_semantics=("arbitrary",)),
        name="bias_reduce" + tag)(dbias, _rel_index())


def _attn_stage(t, pc_ref, p1_ref, p2_ref, kv):
    tm = ATT_TM
    kv[0:tm, :] = jnp.where(t > 1, p2_ref[:, 512:1536], 0.0).astype(kv.dtype)
    kv[tm:2 * tm, :] = jnp.where(t > 0, p1_ref[:, 512:1536], 0.0).astype(kv.dtype)
    kv[2 * tm:, :] = pc_ref[:, 512:1536].astype(kv.dtype)
    q = (pc_ref[:, 0:512] * (ATT_HD ** -0.5)).astype(_MXU_DTYPE)
    ok = lax.broadcasted_iota(jnp.int32, (tm, ATT_KEYS), 1) >= (2 - t) * tm
    return q, ok


def _attn_probs(q, kv, bias_h, ok, h):
    hs = slice(h * ATT_HD, (h + 1) * ATT_HD)
    s = lax.dot_general(q[:, hs], kv[:, hs], _NT, preferred_element_type=F32) + bias_h
    if ok is not None:
        s = jnp.where(ok, s, NEG_INF)
    e = jnp.exp(s - jnp.max(s, axis=-1, keepdims=True))
    return e * (1.0 / jnp.sum(e, axis=-1, keepdims=True))


def _attn_halos(proj_c):
    return [(proj_c, ATT_TM, 1, 'prev'), (proj_c, ATT_TM, 1, 'prev2')]


def _attn_fwd(tag, proj_c, bias, exchange=()):
    t_len = proj_c.shape[0]
    tm = ATT_TM

    def body(t, rows, halos, res, outs, accs, scr):
        b_ref, kv = res[0], scr[0]
        q, ok = _attn_stage(t, rows[0], halos[0], halos[1], kv)

        def heads(ok):
            o = []
            for h in range(ATT_HEADS):
                p = _attn_probs(q, kv, b_ref[h], ok, h).astype(_MXU_DTYPE)
                outs[1][:, h * ATT_KEYS:(h + 1) * ATT_KEYS] = p
                o.append(lax.dot_general(p, kv[:, BW + h * ATT_HD:BW + (h + 1) * ATT_HD], _NN,
                                         preferred_element_type=F32))
            outs[0][...] = jnp.concatenate(o, axis=1).astype(_MXU_DTYPE)

        pl.when(t < 2)(lambda: heads(ok))
        pl.when(t >= 2)(lambda: heads(None))

    return _row_call("attn_fwd" + tag, body, t_len // tm, rows=[(proj_c, tm)], halos=_attn_halos(proj_c),
                     res=[bias], outs=[((t_len, BW), _MXU_DTYPE, tm), ((t_len, ATT_HEADS * ATT_KEYS), _MXU_DTYPE, tm)],
                     scratch=[((ATT_KEYS, 1024), _MXU_DTYPE)], exchange=exchange)


def _attn_bwd(tag, proj_c, dy, probs, exchange=()):
    t_len = proj_c.shape[0]
    tm = ATT_TM
    scale = ATT_HD ** -0.5

    def body(t, rows, halos, res, outs, accs, scr):
        kv = scr[0]
        q, _ = _attn_stage(t, rows[0], halos[0], halos[1], kv)
        do = rows[1][...].astype(_MXU_DTYPE)
        dqs, dks, dvs = [], [], []
        for h in range(ATT_HEADS):
            hs = slice(h * ATT_HD, (h + 1) * ATT_HD)
            vs = slice(BW + h * ATT_HD, BW + (h + 1) * ATT_HD)
            pb = rows[2][:, h * ATT_KEYS:(h + 1) * ATT_KEYS]
            p = pb.astype(F32)
            dp = lax.dot_general(do[:, hs], kv[:, vs], _NT, preferred_element_type=F32)
            ds = p * (dp - jnp.sum(dp * p, axis=-1, keepdims=True))
            accs[0][h] += ds
            dsb = ds.astype(_MXU_DTYPE)
            dqs.append(lax.dot_general(dsb, kv[:, hs], _NN, preferred_element_type=F32) * scale)
            dks.append(lax.dot_general(dsb, q[:, hs], _TN, preferred_element_type=F32))
            dvs.append(lax.dot_general(pb, do[:, hs], _TN, preferred_element_type=F32))
        outs[0][...] = jnp.concatenate(dqs, axis=1).astype(_MXU_DTYPE)
        dkv = jnp.concatenate(dks + dvs, axis=1)
        after_one, after_two = scr[1], scr[2]
        outs[1][...] = (dkv[2 * tm:, :] + after_two[...]).astype(_MXU_DTYPE)
        after_two[...] = dkv[tm:2 * tm, :] + after_one[...]
        after_one[...] = dkv[0:tm, :]

    return _row_call("attn_bwd" + tag, body, t_len // tm, rows=[(proj_c, tm), (dy, tm), (probs, tm)],
                     halos=_attn_halos(proj_c),
                     outs=[((t_len, BW), _MXU_DTYPE, tm), ((t_len, 1024), _MXU_DTYPE, tm)],
                     accs=[(ATT_HEADS, ATT_TM, ATT_KEYS)],
                     scratch=[((ATT_KEYS, 1024), _MXU_DTYPE), ((tm, 1024), F32), ((tm, 1024), F32)],
                     reverse=True, exchange=exchange)


def _conv_glu(pd):
    a, g = pd[:, :BW], pd[:, BW:]
    sig = _sigmoid(g)
    return a, sig, a * sig


def _conv_stage(t, pd_ref, ph_ref, win):
    pd = pd_ref[...]
    a, sig, y0 = _conv_glu(pd)
    win[0:CONV_HALO, :] = jnp.where(t > 0, _conv_glu(ph_ref[...])[2], 0.0)
    win[CONV_HALO:CONV_HALO + pd.shape[0], :] = y0
    return a, sig


SUBLANES = 8


def _conv_shifted(win, sh):
    for b in range(SUBLANES):
        sh[b] = win[pl.ds(b, sh.shape[1]), :]


def _conv_taps_by_copy(offsets):
    groups = {}
    for j, o in enumerate(offsets):
        groups.setdefault(o % SUBLANES, []).append((j, o - o % SUBLANES))
    return [(rem, min(a for _, a in taps), max(a for _, a in taps) - min(a for _, a in taps), taps)
            for rem, taps in sorted(groups.items())]


def _conv_span(sh, rem, r0, lo, rows):
    return sh[rem, pl.ds(pl.multiple_of(r0 + lo, SUBLANES), rows), :]


def _conv_tap_sum(sh, w_ref, offsets, out_ref, init=None, rb=32):
    plan = _conv_taps_by_copy(offsets)

    def block(i, carry):
        r0 = pl.multiple_of(i * rb, rb)
        acc = jnp.zeros((rb, BW), F32) if init is None else jnp.broadcast_to(init, (rb, BW))
        for rem, lo, extra, taps in plan:
            span = _conv_span(sh, rem, r0, lo, extra + rb)
            for j, a in taps:
                acc = acc + w_ref[j:j + 1, :] * span[a - lo:a - lo + rb]
        out_ref[pl.ds(r0, rb), :] = acc
        return carry

    lax.fori_loop(0, out_ref.shape[0] // rb, block, 0)


def _conv_tap_corr(sh, d_ref, rows, offsets, acc_ref, rb=32):
    for rem, lo, extra, taps in _conv_taps_by_copy(offsets):
        def block(i, sums, rem=rem, lo=lo, extra=extra, taps=taps):
            r0 = pl.multiple_of(i * rb, rb)
            d = d_ref[pl.ds(r0, rb), :]
            span = _conv_span(sh, rem, r0, lo, extra + rb)
            out = []
            for s, (j, a) in zip(sums, taps):
                prod = d * span[a - lo:a - lo + rb]
                for k in range(0, rb, SUBLANES):
                    s = s + prod[k:k + SUBLANES]
                out.append(s)
            return tuple(out)

        sums = lax.fori_loop(0, rows // rb, block,
                             tuple(jnp.zeros((SUBLANES, BW), F32) for _ in taps))
        for (j, _), s in zip(taps, sums):
            acc_ref[j:j + 1, :] += jnp.sum(s, axis=0, keepdims=True)


def _conv_scratch(tm):
    return [((tm + CONV_HALO + SUBLANES, BW), F32), ((SUBLANES, tm + CONV_HALO, BW), F32)]


def _conv_fwd(tag, proj_d, dw_w, dw_b, ln_g, ln_b, tm=512):
    t_len = proj_d.shape[0]
    lead = CONV_HALO - (CONV_K - 1)

    def body(t, rows, halos, res, outs, accs, scr):
        win, sh = scr
        _conv_stage(t, rows[0], halos[0], win)
        _conv_shifted(win, sh)
        _conv_tap_sum(sh, res[0], [lead + j for j in range(CONV_K)], outs[1], init=res[1][...])
        yl, _, _ = _ln_fwd(outs[1][...], res[2][...], res[3][...])
        outs[0][...] = (yl * _sigmoid(yl)).astype(_MXU_DTYPE)

    return _row_call("conv_fwd" + tag, body, t_len // tm, rows=[(proj_d, tm)],
                     halos=[(proj_d, CONV_HALO, tm // CONV_HALO, 'prev')], res=[dw_w, dw_b, ln_g, ln_b],
                     outs=[((t_len, BW), _MXU_DTYPE, tm), ((t_len, BW), F32, tm)], scratch=_conv_scratch(tm))


def _conv_norm_bwd(yc, dy, lgv, lbv):
    yl, xh, rs = _ln_fwd(yc, lgv, lbv)
    sig = _sigmoid(yl)
    return _ln_bwd(dy * (sig * (1.0 + yl * (1.0 - sig))), xh, rs, lgv)


def _conv_bwd(tag, proj_d, yc, dy, dw_w, ln_g, ln_b, tm=512):
    t_len = proj_d.shape[0]
    nt = t_len // tm
    lead = CONV_HALO - (CONV_K - 1)
    per = tm // CONV_HALO

    def body(t, rows, halos, res, outs, accs, scr):
        win, sh, wd, shd, dy0_ref = scr
        lgv, lbv = res[1][...], res[2][...]
        a, sig = _conv_stage(t, rows[0], halos[0], win)
        _conv_shifted(win, sh)
        dyc, dlg, dlb = _conv_norm_bwd(rows[1][...], rows[2][...], lgv, lbv)
        accs[1][...] += dlg
        accs[2][...] += dlb
        accs[3][...] += jnp.sum(dyc, axis=0, keepdims=True)
        wd[0:tm, :] = dyc
        wd[tm:tm + CONV_HALO, :] = jnp.where(t < nt - 1, _conv_norm_bwd(halos[1][...], halos[2][...], lgv, lbv)[0], 0.0)
        _conv_shifted(wd, shd)
        _conv_tap_corr(sh, wd, tm, [lead + j for j in range(CONV_K)], accs[0])
        _conv_tap_sum(shd, res[0], [CONV_K - 1 - j for j in range(CONV_K)], dy0_ref)
        dy0 = dy0_ref[...]
        outs[0][...] = jnp.concatenate([dy0 * sig, dy0 * a * sig * (1.0 - sig)], axis=1).astype(_MXU_DTYPE)

    return _row_call("conv_bwd" + tag, body, nt, rows=[(proj_d, tm), (yc, tm), (dy, tm)],
                     halos=[(proj_d, CONV_HALO, per, 'prev'), (yc, CONV_HALO, per, 'next'), (dy, CONV_HALO, per, 'next')],
                     res=[dw_w, ln_g, ln_b], outs=[((t_len, 2 * BW), _MXU_DTYPE, tm)],
                     accs=[(CONV_K, BW), (1, BW), (1, BW), (1, BW)],
                     scratch=_conv_scratch(tm) + _conv_scratch(tm) + [((tm, BW), F32)])


def _merge_fwd(tag, h, xn, ys, w_gate, b_gate, w_branch, w_out, tm=512):
    t_len = h.shape[0]

    def body(t, rows, halos, res, outs, accs, scr):
        xnv = rows[1][...]
        wg_ref, bg_ref, wb_ref, wo_ref = res
        merged = jnp.zeros((tm, D_MODEL), F32)
        for n in range(4):
            cs = slice(n * D_MODEL, (n + 1) * D_MODEL)
            z = lax.dot_general(xnv, wg_ref[n], _NN, preferred_element_type=F32) + bg_ref[n:n + 1, :]
            bo = lax.dot_general(rows[2 + n][...], wb_ref[n], _NN, preferred_element_type=F32)
            outs[0][:, cs] = z.astype(_MXU_DTYPE)
            outs[1][:, cs] = bo.astype(_MXU_DTYPE)
            merged = merged + _sigmoid(z) * bo
        mb = merged.astype(_MXU_DTYPE)
        outs[2][...] = mb
        outs[3][...] = rows[0][...] + lax.dot_general(mb, wo_ref[...], _NN, preferred_element_type=F32)

    return _row_call("merge_fwd" + tag, body, t_len // tm, rows=[(h, tm), (xn, tm)] + [(y, tm) for y in ys],
                     res=[w_gate, b_gate, w_branch, w_out],
                     outs=[((t_len, 4 * D_MODEL), _MXU_DTYPE, tm), ((t_len, 4 * D_MODEL), _MXU_DTYPE, tm),
                           ((t_len, D_MODEL), _MXU_DTYPE, tm), ((t_len, D_MODEL), F32, tm)])


def _merge_bwd(tag, dh, gate_pre, bo, w_gate, w_branch, w_out, tm=256):
    t_len = dh.shape[0]

    def body(t, rows, halos, res, outs, accs, scr):
        wg_ref, wb_ref, wo_ref = res
        dhb = rows[0][...].astype(_MXU_DTYPE)
        outs[0][...] = dhb
        dmerged = lax.dot_general(dhb, wo_ref[...], _NT, preferred_element_type=F32)
        dxn = jnp.zeros((tm, D_MODEL), F32)
        dbg = []
        for n in range(4):
            cs = slice(n * D_MODEL, (n + 1) * D_MODEL)
            g = _sigmoid(rows[1][:, cs].astype(F32))
            dbo = (dmerged * g).astype(_MXU_DTYPE)
            dgp = dmerged * rows[2][:, cs].astype(F32) * (g * (1.0 - g))
            dgb = dgp.astype(_MXU_DTYPE)
            outs[1][:, cs] = dbo
            outs[2][:, cs] = dgb
            outs[4 + n][...] = lax.dot_general(dbo, wb_ref[n], _NT, preferred_element_type=F32)
            dxn = dxn + lax.dot_general(dgb, wg_ref[n], _NT, preferred_element_type=F32)
            dbg.append(jnp.sum(dgp, axis=0, keepdims=True))
        outs[3][...] = dxn
        accs[0][...] += jnp.concatenate(dbg, axis=1)

    return _row_call("merge_bwd" + tag, body, t_len // tm, rows=[(dh, tm), (gate_pre, tm), (bo, tm)],
                     res=[w_gate, w_branch, w_out],
                     outs=[((t_len, D_MODEL), _MXU_DTYPE, tm), ((t_len, 4 * D_MODEL), _MXU_DTYPE, tm),
                           ((t_len, 4 * D_MODEL), _MXU_DTYPE, tm), ((t_len, D_MODEL), F32, tm)]
                     + [((t_len, BW), F32, tm)] * 4,
                     accs=[(1, 4 * D_MODEL)])


FF_COLS = 1024


def _ffn_fwd(tag, h, g2, w1, w2, tm=512):
    t_len = h.shape[0]

    def body(t, rows, halos, res, outs, accs, scr):
        hv = rows[0][...]
        hn = (hv * _rms_stat(hv) * res[0][...]).astype(_MXU_DTYPE)
        outs[0][...] = hn
        acc = hv
        for c in range(D_FF // FF_COLS):
            cs = slice(c * FF_COLS, (c + 1) * FF_COLS)
            pre = lax.dot_general(hn, res[1][:, cs], _NN, preferred_element_type=F32)
            outs[1][:, cs] = pre
            ff = jnp.square(jnp.maximum(pre, 0.0)).astype(_MXU_DTYPE)
            acc = acc + lax.dot_general(ff, res[2][cs, :], _NN, preferred_element_type=F32)
        outs[2][...] = acc

    return _row_call("ffn_fwd" + tag, body, t_len // tm, rows=[(h, tm)], res=[g2, w1, w2],
                     outs=[((t_len, D_MODEL), _MXU_DTYPE, tm), ((t_len, D_FF), F32, tm), ((t_len, D_MODEL), F32, tm)])


def _ffn_bwd(tag, dh, h, pre, g2, w1, w2, tm=256, exchange=()):
    t_len = dh.shape[0]

    def body(t, rows, halos, res, outs, accs, scr):
        dhv = rows[0][...]
        hv = rows[1][...]
        dhb = dhv.astype(_MXU_DTYPE)
        outs[0][...] = dhb
        dhn = jnp.zeros((tm, D_MODEL), F32)
        for c in range(D_FF // FF_COLS):
            cs = slice(c * FF_COLS, (c + 1) * FF_COLS)
            r = jnp.maximum(rows[2][:, cs], 0.0)
            outs[1][:, cs] = (r * r).astype(_MXU_DTYPE)
            dpre = (lax.dot_general(dhb, res[2][cs, :], _NT, preferred_element_type=F32) * (2.0 * r)).astype(_MXU_DTYPE)
            outs[2][:, cs] = dpre
            dhn = dhn + lax.dot_general(dpre, res[1][:, cs], _NT, preferred_element_type=F32)
        dres, dg = _rms_bwd(dhn, hv, res[0][...], _rms_stat(hv))
        outs[3][...] = dhv + dres
        accs[0][...] += dg

    return _row_call("ffn_bwd" + tag, body, t_len // tm, rows=[(dh, tm), (h, tm), (pre, tm)], res=[g2, w1, w2],
                     outs=[((t_len, D_MODEL), _MXU_DTYPE, tm), ((t_len, D_FF), _MXU_DTYPE, tm),
                           ((t_len, D_FF), _MXU_DTYPE, tm), ((t_len, D_MODEL), F32, tm)],
                     accs=[(1, D_MODEL)], exchange=exchange)


def _ple_fwd(tag, h, p, g3, w_pg, b_pg, w_ple, head=None, tm=512):
    t_len = h.shape[0]

    def body(t, rows, halos, res, outs, accs, scr):
        hv = rows[0][...]
        hg = (hv * _rms_stat(hv) * res[0][...]).astype(_MXU_DTYPE)
        pb = rows[1][...].astype(_MXU_DTYPE)
        pg = _sigmoid(lax.dot_general(hg, res[1][...], _NN, preferred_element_type=F32) + res[2][...])
        pe = lax.dot_general(pb, res[3][...], _NN, preferred_element_type=F32)
        outs[0][...] = hg
        outs[1][...] = pb
        outs[2][...] = pg
        h3 = hv + pg * pe
        if head is None:
            outs[3][...] = h3
        else:
            g = res[4][...]
            r = _rms_stat(h3)
            diff = h3 * r * g - rows[2][...]
            accs[0][...] += 0.5 * jnp.sum(jnp.mean(diff * diff, axis=-1, keepdims=True), axis=0, keepdims=True)
            outs[3][...], dg = _rms_bwd(diff * (1.0 / D_MODEL), h3, g, r)
            accs[1][...] += dg

    return _row_call("ple_fwd" + tag, body, t_len // tm, rows=[(h, tm), (p, tm)] + ([(head[0], tm)] if head else []),
                     res=[g3, w_pg, b_pg, w_ple] + ([head[1]] if head else []),
                     outs=[((t_len, D_MODEL), _MXU_DTYPE, tm), ((t_len, PLE_DIM), _MXU_DTYPE, tm),
                           ((t_len, D_MODEL), F32, tm), ((t_len, D_MODEL), F32, tm)],
                     accs=[(1, 128), (1, D_MODEL)] if head else [])


def _ple_bwd(tag, dh, h, pg, p_b, g3, w_pg, w_ple, tm=512):
    t_len = dh.shape[0]

    def body(t, rows, halos, res, outs, accs, scr):
        dhv = rows[0][...]
        hv = rows[1][...]
        pgv = rows[2][...]
        pe = lax.dot_general(rows[3][...], res[2][...], _NN, preferred_element_type=F32)
        dgp = dhv * pe * (pgv * (1.0 - pgv))
        dgb = dgp.astype(_MXU_DTYPE)
        outs[0][...] = dgb
        outs[1][...] = (dhv * pgv).astype(_MXU_DTYPE)
        dhg = lax.dot_general(dgb, res[1][...], _NT, preferred_element_type=F32)
        dres, dg = _rms_bwd(dhg, hv, res[0][...], _rms_stat(hv))
        outs[2][...] = dhv + dres
        accs[0][...] += jnp.sum(dgp, axis=0, keepdims=True)
        accs[1][...] += dg

    return _row_call("ple_bwd" + tag, body, t_len // tm, rows=[(dh, tm), (h, tm), (pg, tm), (p_b, tm)],
                     res=[g3, w_pg, w_ple],
                     outs=[((t_len, D_MODEL), _MXU_DTYPE, tm), ((t_len, D_MODEL), _MXU_DTYPE, tm),
                           ((t_len, D_MODEL), F32, tm)],
                     accs=[(1, D_MODEL), (1, D_MODEL)])


def _inproj_bwd(tag, dh, h, dxn_gate, dprojs, g1, w_in_t, tm=512, exchange=()):
    t_len = dh.shape[0]

    def body(t, rows, halos, res, outs, accs, scr):
        hv = rows[1][...]
        dxn = rows[2][...]
        for dp, (_, s0, n) in zip(rows[3:], IN_GROUPS_BWD):
            dxn = dxn + lax.dot_general(dp[...], res[1][s0:s0 + n, :], _NN, preferred_element_type=F32)
        dres, dg = _rms_bwd(dxn, hv, res[0][...], _rms_stat(hv))
        outs[0][...] = rows[0][...] + dres
        accs[0][...] += dg

    return _row_call("inproj_bwd" + tag, body, t_len // tm,
                     rows=[(dh, tm), (h, tm), (dxn_gate, tm)] + [(d, tm) for d in dprojs],
                     res=[g1, w_in_t], outs=[((t_len, D_MODEL), F32, tm)], accs=[(1, D_MODEL)],
                     exchange=exchange)


def _row(v):
    return v.reshape(1, -1)


GATHER_DURING = (('inproj', ('w_gate', 'w_branch', 'w_out')), ('gla', ('w_ff1',)), ('attn', ('w_ff2', 'w_ple_gate', 'w_ple')))
SCATTER_DURING_ATTN = ('w_ple_gate', 'w_ple', 'w_ff1', 'w_ff2', 'w_out', 'w_gate', 'w_branch')


def _gather_items(shards, names):
    items = []
    for n in names:
        s = shards[n]
        ax = SHARD_AXIS[n] - 1
        if n == 'w_in':
            items.append(_gather_item(s))
        else:
            items.append(_gather_item(s, s.shape[:ax] + (N_DEV * s.shape[ax],) + s.shape[ax + 1:], ax))
    return items


def _land(w, names, arrays):
    for n, a in zip(names, arrays):
        w[n] = a.reshape(IN_COLS, D_MODEL) if n == 'w_in' else a


def _layer_fwd(i, h, p_i, w, shards, next_shards, head=None):
    tag = "_l%d" % i
    during = dict(GATHER_DURING)
    res = _inproj_fwd(tag, h, _row(w['norm1_g']), w['w_in'], exchange=_gather_items(shards, during['inproj']))
    xn, pa, pb, pr, pc, pd = res[:6]
    _land(w, during['inproj'], res[6:])
    sg_bt = w['sg_b'].T
    y_a = _sg_fwd(tag, pa, _row(w['sg_ln_g']), _row(w['sg_ln_b']), w['sg_w'], sg_bt)
    res = _gla_fwd(tag, pb, pr, w['gla_w_a2'], _row(w['gla_b_a']), _row(w['gla_norm_g']),
                   exchange=_gather_items(shards, during['gla']))
    y_b, states = res[:2]
    _land(w, during['gla'], res[2:])
    bias = _bias_tile(tag, _bias_expand(tag, w['att_rel_bias']).reshape(ATT_HEADS, CHUNK, BAND))
    items = _gather_items(shards, during['attn']) + (_gather_items(next_shards, ['w_in']) if next_shards else [])
    res = _attn_fwd(tag, pc, bias, exchange=items)
    y_c, probs = res[:2]
    _land(w, during['attn'], res[2:2 + len(during['attn'])])
    next_w_in = res[-1].reshape(IN_COLS, D_MODEL) if next_shards else None
    y_d, yc = _conv_fwd(tag, pd, w['conv_dw_w'], _row(w['conv_dw_b']), _row(w['conv_ln_g']), _row(w['conv_ln_b']))
    ys = (y_a, y_b, y_c, y_d)
    gate, bo, merged, h1 = _merge_fwd(tag, h, xn, ys, w['w_gate'], w['b_gate'], w['w_branch'], w['w_out'])
    hn, pre, h2 = _ffn_fwd(tag, h1, _row(w['norm2_g']), w['w_ff1'], w['w_ff2'])
    res = _ple_fwd(tag, h2, p_i, _row(w['norm3_g']), w['w_ple_gate'], _row(w['b_ple_gate']), w['w_ple'], head=head)
    hg, p_b, pg, out = res[:4]
    saved = dict(h=h, xn=xn, pa=pa, pb=pb, pr=pr, pc=pc, pd=pd, states=states, probs=probs, yc=yc, ys=ys, gate=gate,
                 bo=bo, merged=merged, h1=h1, hn=hn, pre=pre, h2=h2, hg=hg, p_b=p_b, pg=pg, sg_bt=sg_bt)
    return (out, res[4:]), saved, next_w_in


def _layer_bwd(i, dh3, s, w, tail=None):
    tag = "_l%d" % i
    g = {}
    dgp, dpe, dh2, db_pg, dg3 = _ple_bwd(tag, dh3, s['h2'], s['pg'], s['p_b'], _row(w['norm3_g']), w['w_ple_gate'],
                                         w['w_ple'])
    g['b_ple_gate'], g['norm3_g'] = db_pg[0], dg3[0]
    g['w_ple_gate'] = _tn_call("dw_ple_gate" + tag, s['hg'], dgp, D_MODEL, D_MODEL)
    g['w_ple'] = _tn_call("dw_ple" + tag, s['p_b'], dpe, PLE_DIM, D_MODEL)

    dh2b, ffb, dpre, dh1, dg2 = _ffn_bwd(tag, dh2, s['h1'], s['pre'], _row(w['norm2_g']), w['w_ff1'], w['w_ff2'])
    g['norm2_g'] = dg2[0]
    g['w_ff1'] = _tn_call("dw_ff1" + tag, s['hn'], dpre, D_MODEL, FF_COLS, nblk=D_FF // FF_COLS)
    g['w_ff2'] = _tn_call("dw_ff2" + tag, ffb, dh2b, FF_COLS, D_MODEL, nblk=D_FF // FF_COLS, a_col=True, b_col=False,
                          out='rows')

    dh1b, dbo, dgpre, dxn_gate, dy_a, dy_b, dy_c, dy_d, db_gate = _merge_bwd(
        tag, dh1, s['gate'], s['bo'], w['w_gate'], w['w_branch'], w['w_out'])
    g['b_gate'] = db_gate.reshape(4, D_MODEL)
    g['w_out'] = _tn_call("dw_out" + tag, s['merged'], dh1b, D_MODEL, D_MODEL)
    g['w_gate'] = _tn_call("dw_gate" + tag, s['xn'], dgpre, D_MODEL, D_MODEL, nblk=4, out='stack')
    g['w_branch'] = jnp.stack([_tn_call("dw_branch%d%s" % (n, tag), s['ys'][n], dbo, BW, D_MODEL, b_off=n)
                             for n in range(4)])

    lg, lb = _row(w['sg_ln_g']), _row(w['sg_ln_b'])
    dpa, dsg_w, dsg_bt, dlg, dlb = _sg_bwd(tag, s['pa'], dy_a, lg, lb, w['sg_w'], s['sg_bt'])
    g['sg_w'], g['sg_b'], g['sg_ln_g'], g['sg_ln_b'] = dsg_w, dsg_bt.T, dlg[0], dlb[0]

    dpb, dwa2, dba, dng = _gla_bwd(tag, s['pb'], s['pr'], dy_b, s['states'], w['gla_w_a2'],
                                   _row(w['gla_b_a']), _row(w['gla_norm_g']))
    g['gla_w_a2'], g['gla_b_a'], g['gla_norm_g'] = dwa2, dba[0], dng[0]

    items = [_scatter_item(g.pop(n), axis=SHARD_AXIS[n] - 1) for n in SCATTER_DURING_ATTN]
    res = _attn_bwd(tag, s['pc'], dy_c, s['probs'], exchange=items)
    dq, dkv, dbias = res[:3]
    parts = dict(zip(SCATTER_DURING_ATTN, res[3:]))
    g['att_rel_bias'] = _bias_reduce(tag, _bias_untile(tag, dbias).reshape(ATT_HEADS, CHUNK * BAND))

    cg, cb = _row(w['conv_ln_g']), _row(w['conv_ln_b'])
    dpd, ddw, dcg, dcb, ddwb = _conv_bwd(tag, s['pd'], s['yc'], dy_d, w['conv_dw_w'], cg, cb)
    g['conv_ln_g'], g['conv_ln_b'], g['conv_dw_b'], g['conv_dw_w'] = dcg[0], dcb[0], ddwb[0], ddw

    dprojs = (dpa, dpb, dq, dkv, dpd)
    dw_in = jnp.concatenate([_tn_call("dw_in%s%s" % (name, tag), dp, s['xn'], n, D_MODEL)
                             for (name, _, n), dp in zip(IN_GROUPS_BWD, dprojs)], axis=0)
    items = [_scatter_item(dw_in.reshape(N_DEV, IN_COLS // N_DEV, D_MODEL))] + (tail(g) if tail else [])
    res = _inproj_bwd(tag, dh1, s['h'], dxn_gate, dprojs, _row(w['norm1_g']), w['w_in'], exchange=items)
    dh0, dg1 = res[:2]
    g['norm1_g'] = dg1[0]
    parts['w_in'] = res[2]
    return dh0, g, parts, res[3:]


def _local_step(x, p, target, final_g, layers, shards, tail):
    h = x
    saved = []
    for i in range(DEPTH):
        nxt = shards[i + 1] if i + 1 < DEPTH else None
        head = None if nxt else (target, _row(final_g))
        (h, extra), s, next_w_in = _layer_fwd(i, h, p[i], layers[i], shards[i], nxt, head)
        saved.append(s)
        if nxt:
            layers[i + 1]['w_in'] = next_w_in
    dh, (loss, dgf) = h, extra
    small, parts, tail_out = [None] * DEPTH, [None] * DEPTH, None
    for i in reversed(range(DEPTH)):
        hook = (lambda g: tail([g] + small[1:])) if i == 0 else None
        dh, small[i], parts[i], out = _layer_bwd(i, dh, saved[i], layers[i], hook)
        if i == 0:
            tail_out = out
    return loss[0, 0], dh, dgf[0], small, parts, tail_out


def _peers():
    x, y, c = lax.axis_index("x"), lax.axis_index("y"), lax.axis_index("c")
    me = 4 * x + 2 * y + c
    out = []
    for k in range(1, N_DEV):
        px = (1 - x) if k & 4 else x
        py = (1 - y) if k & 2 else y
        pc = (1 - c) if k & 1 else c
        out.append((k - 1, (px, py, pc), 4 * px + 2 * py + pc))
    return me, out


def _block(ref, axis, idx, width):
    ix = [slice(None)] * len(ref.shape)
    ix[axis] = pl.ds(pl.multiple_of(idx * width, width), width)
    return ref.at[tuple(ix)]


def _slot(ref, idx):
    return ref.at[idx]


def _whole(ref, idx):
    return ref


def _gather_item(src, out_shape=None, axis=None):
    if axis is None:
        return dict(src=src, out=(N_DEV,) + src.shape, take=_whole, put=_slot)
    return dict(src=src, out=tuple(out_shape), take=_whole,
                put=lambda ref, s: _block(ref, axis, s, src.shape[axis]))


def _scatter_item(src, axis=None, lead=0):
    if axis is None:
        shape = src.shape[:lead] + src.shape[lead + 1:]
        take = lambda ref, s: ref.at[(slice(None),) * lead + (s,)]
    else:
        width = src.shape[axis] // N_DEV
        shape = src.shape[:axis] + (width,) + src.shape[axis + 1:]
        take = lambda ref, s: _block(ref, axis, s, width)
    return dict(src=src, out=(N_DEV,) + shape, take=take, put=_slot)


def _exchange_sems(n):
    return [pltpu.SemaphoreType.DMA((n * (N_DEV - 1),)), pltpu.SemaphoreType.DMA((n * (N_DEV - 1),)),
            pltpu.SemaphoreType.DMA((n,))]


def _exchange_copies(items, src_refs, out_refs, sems, start):
    send_sems, recv_sems, local_sems = sems
    me, peers = _peers()

    def remote(i, k, pos, receiver, sender):
        it = items[i]
        return pltpu.make_async_remote_copy(
            src_ref=it['take'](src_refs[i], receiver), dst_ref=it['put'](out_refs[i], sender),
            send_sem=send_sems.at[i * (N_DEV - 1) + k], recv_sem=recv_sems.at[i * (N_DEV - 1) + k],
            device_id=pos, device_id_type=pl.DeviceIdType.MESH)

    local = [pltpu.make_async_copy(it['take'](src_refs[i], me), it['put'](out_refs[i], me), local_sems.at[i])
             for i, it in enumerate(items)]
    if start:
        for cp in local:
            cp.start()
        for k, pos, flat in peers:
            for i in range(len(items)):
                remote(i, k, pos, flat, me).start()
    else:
        for k, pos, flat in peers:
            for i in range(len(items)):
                remote(i, k, pos, flat, flat).wait_recv()
        for k, pos, flat in peers:
            for i in range(len(items)):
                remote(i, k, pos, flat, me).wait_send()
        for cp in local:
            cp.wait()


def _gather_via_sibling(name, srcs):
    n = len(srcs)

    def body(*refs):
        src_refs, out_refs = refs[:n], refs[n:2 * n]
        send_sems, recv_sems, local_sems = refs[2 * n:]
        x, y, c = lax.axis_index("x"), lax.axis_index("y"), lax.axis_index("c")
        flat = lambda px, py, pc: 4 * px + 2 * py + pc
        me, sibling = (x, y, c), (x, y, 1 - c)
        chips = [(1 - x, y), (x, 1 - y), (1 - x, 1 - y)]

        def copy(i, k, block, to, own):
            return pltpu.make_async_remote_copy(
                src_ref=src_refs[i] if own else out_refs[i].at[flat(*block)], dst_ref=out_refs[i].at[flat(*block)],
                send_sem=send_sems.at[i * (N_DEV - 1) + k], recv_sem=recv_sems.at[i * (N_DEV - 1) + k],
                device_id=to, device_id_type=pl.DeviceIdType.MESH)

        local, sent = [], []
        for i in range(n):
            local.append(pltpu.make_async_copy(src_refs[i], out_refs[i].at[flat(*me)], local_sems.at[i]))
            local[-1].start()
            sent.append(copy(i, 0, me, sibling, True))
            sent += [copy(i, 1 + j, me, (*chip, c), True) for j, chip in enumerate(chips)]
            for cp in sent[-4:]:
                cp.start()
        for i in range(n):
            for j, chip in enumerate(chips):
                copy(i, 1 + j, (*chip, c), me, True).wait_recv()
                sent.append(copy(i, 4 + j, (*chip, c), sibling, False))
                sent[-1].start()
        for i in range(n):
            copy(i, 0, sibling, me, True).wait_recv()
            for j, chip in enumerate(chips):
                copy(i, 4 + j, (*chip, 1 - c), me, False).wait_recv()
        for cp in sent:
            cp.wait_send()
        for cp in local:
            cp.wait()

    any_spec = pl.BlockSpec(memory_space=pl.ANY)
    return pl.pallas_call(
        body, out_shape=[jax.ShapeDtypeStruct((N_DEV,) + s.shape, s.dtype) for s in srcs],
        in_specs=[any_spec] * n, out_specs=[any_spec] * n, scratch_shapes=_exchange_sems(n), name=name)(*srcs)


def _pack(arrays, dtype, lead=None):
    flat = [a.astype(dtype).reshape((lead, -1) if lead else (-1,)) for a in arrays]
    cat = jnp.concatenate(flat, axis=-1)
    n = cat.shape[-1]
    rows = -(-n // (PACK_COLS * SUBLANES)) * SUBLANES
    pad = rows * PACK_COLS - n
    if pad:
        cat = jnp.pad(cat, ((0, 0), (0, pad)) if lead else ((0, pad),))
    return cat.reshape((lead, rows, PACK_COLS) if lead else (rows, PACK_COLS))


def _unpack(buf, shapes, lead=None):
    flat = buf.reshape((lead, -1) if lead else (-1,))
    out, off = [], 0
    for shp in shapes:
        n = int(np.prod(shp))
        piece = flat[..., off:off + n]
        out.append(piece.reshape(((lead,) if lead else ()) + tuple(shp)))
        off += n
    return out


def _to_slabs(full, axis):
    shp = full.shape
    split = full.reshape(shp[:axis] + (N_DEV, shp[axis] // N_DEV) + shp[axis + 1:])
    return jnp.moveaxis(split, axis, 0)


def _from_slabs(slabs, axis):
    moved = jnp.moveaxis(slabs, 0, axis)
    shp = moved.shape
    return moved.reshape(shp[:axis] + (shp[axis] * shp[axis + 1],) + shp[axis + 2:])


def _adamw_block(r, c):
    if r % 8:
        return r, 256
    br = min(r, max(8, ADAMW_TILE * PACK_COLS // c))
    while r % br:
        br //= 2
    return br, c


def _adamw(name, partials, w, m, v):
    n_lead, r, c = w.shape
    br, bc = _adamw_block(r, c)
    ni, nj = r // br, c // bc
    c1 = 1.0 - ADAM_B1 ** ADAM_STEP
    c2 = 1.0 - ADAM_B2 ** ADAM_STEP

    def kern(*refs):
        p_refs = refs[:n_lead]
        w_ref, m_ref, v_ref, g_ref, d_ref, nm_ref, nv_ref = refs[n_lead:]
        layer = pl.program_id(0)
        g = None
        for l, p_ref in enumerate(p_refs):
            gl = p_ref[0].astype(F32)
            for s in range(1, N_DEV):
                gl = gl + p_ref[s].astype(F32)
            g = gl if g is None else jnp.where(layer == l, gl, g)
        nm = ADAM_B1 * m_ref[...] + (1.0 - ADAM_B1) * g
        nv = ADAM_B2 * v_ref[...] + (1.0 - ADAM_B2) * jnp.square(g)
        g_ref[...] = g
        nm_ref[...] = nm
        nv_ref[...] = nv
        d_ref[...] = -ADAM_LR * ((nm / c1) / (jnp.sqrt(nv / c2) + ADAM_EPS) + ADAM_WD * w_ref[...])

    def part_spec(mine):
        def index(l, i, j):
            before, after = l < mine, l > mine
            return (0, jnp.where(before, 0, jnp.where(after, ni - 1, i)), jnp.where(before, 0, jnp.where(after, nj - 1, j)))
        return pl.BlockSpec((N_DEV, br, bc), index)

    blk = pl.BlockSpec((None, br, bc), lambda l, i, j: (l, i, j))
    return pl.pallas_call(
        kern, grid=(n_lead, ni, nj),
        in_specs=[part_spec(l) for l in range(n_lead)] + [blk, blk, blk],
        out_specs=[blk] * 4, out_shape=[jax.ShapeDtypeStruct(w.shape, F32)] * 4,
        compiler_params=pltpu.CompilerParams(dimension_semantics=("arbitrary",) * 3),
        name=name)(*partials, w, m, v)


def _as_rows(a, lead):
    return a.reshape(a.shape[:lead] + (-1, a.shape[-1]))


def kernel(x, p, norm1_g, w_in, sg_ln_g, sg_ln_b, sg_w, sg_b, gla_w_a2, gla_b_a, gla_norm_g, att_rel_bias, conv_dw_w, conv_dw_b, conv_ln_g, conv_ln_b, w_branch, w_gate, b_gate, w_out, norm2_g, w_ff1, w_ff2, norm3_g, w_ple_gate, b_ple_gate, w_ple, final_g, loss_target, m_norm1_g, m_w_in, m_sg_ln_g, m_sg_ln_b, m_sg_w, m_sg_b, m_gla_w_a2, m_gla_b_a, m_gla_norm_g, m_att_rel_bias, m_conv_dw_w, m_conv_dw_b, m_conv_ln_g, m_conv_ln_b, m_w_branch, m_w_gate, m_b_gate, m_w_out, m_norm2_g, m_w_ff1, m_w_ff2, m_norm3_g, m_w_ple_gate, m_b_ple_gate, m_w_ple, m_final_g, v_norm1_g, v_w_in, v_sg_ln_g, v_sg_ln_b, v_sg_w, v_sg_b, v_gla_w_a2, v_gla_b_a, v_gla_norm_g, v_att_rel_bias, v_conv_dw_w, v_conv_dw_b, v_conv_ln_g, v_conv_ln_b, v_w_branch, v_w_gate, v_b_gate, v_w_out, v_norm2_g, v_w_ff1, v_w_ff2, v_norm3_g, v_w_ple_gate, v_b_ple_gate, v_w_ple, v_final_g):
    args = locals()
    wts = {n: args[n] for n in WEIGHTS}
    mom = {n: args['m_' + n] for n in WEIGHTS}
    var = {n: args['v_' + n] for n in WEIGHTS}

    local = {d_name: dict(d, w_in=jnp.swapaxes(d['w_in'], 1, 2))
             for d_name, d in (("w", wts), ("m", mom), ("v", var))}
    shards = [{n: local["w"][n][i].astype(_MXU_DTYPE) for n in MXU_WEIGHTS} for i in range(DEPTH)]

    first_w_in, vec = _gather_via_sibling("gather_first_weights",
                                          [shards[0]['w_in'], _pack([wts[n] for n in VEC_WEIGHTS], F32)])
    vec_full = {n: _from_slabs(slabs, SHARD_AXIS[n])
                for n, slabs in zip(VEC_WEIGHTS, _unpack(vec, [wts[n].shape for n in VEC_WEIGHTS], lead=N_DEV))}
    small_names = [n for n in WEIGHTS if n not in MXU_WEIGHTS and n != 'final_g']
    layers = [{n: (vec_full[n] if n in vec_full else wts[n])[i] for n in small_names} for i in range(DEPTH)]
    _land(layers[0], ['w_in'], [first_w_in])

    def vec_items(small):
        stacked = [jnp.stack([g[n] for g in small]) for n in VEC_WEIGHTS]
        return [_scatter_item(_pack([_to_slabs(a, SHARD_AXIS[n]) for n, a in zip(VEC_WEIGHTS, stacked)], F32,
                                    lead=N_DEV))]

    seq = x.shape[1:]
    loss, grad_x, dgf, small, parts, (vec_parts,) = _local_step(
        x.reshape(seq), p.reshape(DEPTH, seq[0], PLE_DIM), loss_target.reshape(seq), final_g, layers, shards, vec_items)
    loss = lax.psum(loss, ("x", "y", "c"))

    grads = {n: jnp.stack([small[i][n] for i in range(DEPTH)]) for n in REPLICATED if n != 'final_g'}
    grads['final_g'] = dgf
    repl_parts, = _gather_via_sibling("gather_replicated_grads", [_pack([grads[n] for n in REPLICATED], F32)])

    results = {}
    for n in MXU_WEIGHTS:
        w3, m3, v3 = (_as_rows(local[d][n], 1) for d in ("w", "m", "v"))
        outs = _adamw("adamw_" + n, [parts[i][n].reshape((N_DEV,) + w3.shape[1:]) for i in range(DEPTH)], w3, m3, v3)
        for kind, a in zip(("grad", "delta", "new_m", "new_v"), outs):
            a = a.reshape(local["w"][n].shape)
            results[kind, n] = jnp.swapaxes(a, 1, 2) if n == 'w_in' else a
    for names, part, call in ((VEC_WEIGHTS, vec_parts, "adamw_vec"), (REPLICATED, repl_parts, "adamw_replicated")):
        packed = [_pack([d[n] for n in names], F32)[None] for d in (wts, mom, var)]
        outs = _adamw(call, [part], *packed)
        for kind, buf in zip(("grad", "delta", "new_m", "new_v"), outs):
            for n, a in zip(names, _unpack(buf[0], [wts[n].shape for n in names])):
                results[kind, n] = a
    return (loss, grad_x[None]) + tuple(results[kind, n] for kind in ("grad", "delta", "new_m", "new_v")
                                        for n in WEIGHTS)
```

```python
import numpy as np
import jax
import jax.numpy as jnp
from jax import lax
from jax.experimental import pallas as pl
from jax.experimental.pallas import tpu as pltpu

F32 = jnp.float32
_MXU_DTYPE = jnp.bfloat16
GRAD_DTYPE = jnp.bfloat16

N_DEV = 8
D_MODEL = 1024
DEPTH = 2
CHUNK = 64
PLE_DIM = 256
BW = 512
SG_BLOCK = 128
SG_GROUPS = 4
GLA_HEADS = 4
GLA_DK = 64
GLA_DV = 128
GLA_RANK = 16
GLA_TAU = 16.0
ATT_HEADS = 8
ATT_HD = 64
ATT_BAND = 9
BAND = ATT_BAND * CHUNK
MAX_REL = 256
REL_TABLE = CHUNK + MAX_REL
CONV_K = 31
CONV_HALO = 32
D_FF = 4096
EPS = 1e-6
NEG_INF = -1e30

IN_GROUPS = (("A", 0, 1024), ("B", 1024, 1536), ("a", 2560, 16), ("C", 2576, 1536), ("D", 4112, 1024))
IN_GROUPS_BWD = (IN_GROUPS[0], ("Ba", 1024, 1552), ("Cq", 2576, 512), ("Ckv", 3088, 1024), IN_GROUPS[4])
IN_COLS = 5136

ADAM_LR = 0.001
ADAM_B1 = 0.9
ADAM_B2 = 0.999
ADAM_EPS = 1e-08
ADAM_WD = 0.01
ADAM_STEP = 10

ADAMW_TILE = 128
PACK_COLS = 1024
VMEM_LIMIT_MB = 56

_NN = (((1,), (0,)), ((), ()))
_NT = (((1,), (1,)), ((), ()))
_TN = (((0,), (0,)), ((), ()))

WEIGHTS = ['norm1_g', 'w_in', 'sg_ln_g', 'sg_ln_b', 'sg_w', 'sg_b', 'gla_w_a2', 'gla_b_a', 'gla_norm_g',
           'att_rel_bias', 'conv_dw_w', 'conv_dw_b', 'conv_ln_g', 'conv_ln_b', 'w_branch', 'w_gate', 'b_gate',
           'w_out', 'norm2_g', 'w_ff1', 'w_ff2', 'norm3_g', 'w_ple_gate', 'b_ple_gate', 'w_ple', 'final_g']
SHARD_AXIS = {'w_in': 2, 'gla_w_a2': 2, 'att_rel_bias': 2, 'conv_dw_w': 2, 'w_branch': 3, 'w_gate': 2,
              'b_gate': 2, 'w_out': 1, 'w_ff1': 2, 'w_ff2': 1, 'w_ple_gate': 1, 'w_ple': 2}
MXU_WEIGHTS = ('w_in', 'w_branch', 'w_gate', 'w_out', 'w_ff1', 'w_ff2', 'w_ple_gate', 'w_ple')
VEC_WEIGHTS = ('gla_w_a2', 'att_rel_bias', 'conv_dw_w', 'b_gate')
REPLICATED = tuple(n for n in WEIGHTS if n not in SHARD_AXIS)


def _mm(a, b, dims=_NN):
    return lax.dot_general(a.astype(_MXU_DTYPE), b.astype(_MXU_DTYPE), dims, preferred_element_type=F32)


def _split3(x):
    x1 = x.astype(jnp.bfloat16)
    r1 = x - x1.astype(F32)
    x2 = r1.astype(jnp.bfloat16)
    x3 = (r1 - x2.astype(F32)).astype(jnp.bfloat16)
    return x1, x2, x3


def _mm_exact_rhs(m, x, dims=_NN):
    return sum(lax.dot_general(m, xi, dims, preferred_element_type=F32) for xi in _split3(x))


def _mm_exact_lhs(x, m, dims=_NN):
    return sum(lax.dot_general(xi, m, dims, preferred_element_type=F32) for xi in _split3(x))


def _sigmoid(x):
    return 1.0 / (1.0 + jnp.exp(-x))


def _gelu(x):
    c = 0.7978845608028654
    t = jnp.tanh(c * (x + 0.044715 * x * x * x))
    return 0.5 * x * (1.0 + t), t


def _gelu_grad(x, t):
    c = 0.7978845608028654
    return 0.5 * (1.0 + t) + 0.5 * x * (1.0 - t * t) * c * (1.0 + 3.0 * 0.044715 * x * x)


def _rms_stat(h):
    return lax.rsqrt(jnp.mean(h * h, axis=-1, keepdims=True) + EPS)


def _rms_bwd(dy, h, g, r):
    hh = h * r
    dhh = dy * g
    dh = r * (dhh - hh * jnp.mean(dhh * hh, axis=-1, keepdims=True))
    return dh, jnp.sum(dy * hh, axis=0, keepdims=True)


def _ln_fwd(x, g, b):
    mu = jnp.mean(x, axis=-1, keepdims=True)
    xc = x - mu
    rs = lax.rsqrt(jnp.mean(xc * xc, axis=-1, keepdims=True) + EPS)
    xh = xc * rs
    return xh * g + b, xh, rs


def _ln_bwd(dy, xh, rs, g):
    dxh = dy * g
    dx = rs * (dxh - jnp.mean(dxh, axis=-1, keepdims=True) - xh * jnp.mean(dxh * xh, axis=-1, keepdims=True))
    return dx, jnp.sum(dy * xh, axis=0, keepdims=True), jnp.sum(dy, axis=0, keepdims=True)


def _row_call(name, body, nt, rows=(), halos=(), res=(), outs=(), accs=(), scratch=(), reverse=False, exchange=()):
    def pos(i):
        return (nt - 1 - i) if reverse else i

    def lead(ndim, f):
        return lambda i: (f(pos(i)),) + (0,) * (ndim - 1)

    in_specs, operands = [], []
    for a, tile in rows:
        in_specs.append(pl.BlockSpec((tile,) + a.shape[1:], lead(a.ndim, lambda t: t)))
        operands.append(a)
    for a, blk, per, side in halos:
        last = a.shape[0] // blk - 1
        delta = {'prev2': -2, 'prev': -1, 'next': per}[side]
        f = lambda t, per=per, last=last, delta=delta: jnp.clip(t * per + delta, 0, last)
        in_specs.append(pl.BlockSpec((blk,) + a.shape[1:], lead(a.ndim, f)))
        operands.append(a)
    for a in res:
        in_specs.append(pl.BlockSpec(a.shape, lambda i, nd=a.ndim: (0,) * nd, pipeline_mode=pl.Buffered(1)))
        operands.append(a)
    out_specs, out_shape = [], []
    for shape, dtype, tile in outs:
        out_specs.append(pl.BlockSpec((tile,) + tuple(shape[1:]), lead(len(shape), lambda t: t)))
        out_shape.append(jax.ShapeDtypeStruct(tuple(shape), dtype))
    for shape in accs:
        out_specs.append(pl.BlockSpec(tuple(shape), lambda i, nd=len(shape): (0,) * nd))
        out_shape.append(jax.ShapeDtypeStruct(tuple(shape), F32))
    nx = len(exchange)
    any_spec = pl.BlockSpec(memory_space=pl.ANY)
    for it in exchange:
        in_specs.append(any_spec)
        operands.append(it['src'])
        out_specs.append(any_spec)
        out_shape.append(jax.ShapeDtypeStruct(it['out'], it['src'].dtype))
    sizes = (len(rows), len(halos), len(res), nx, len(outs), len(accs), nx, len(scratch), 3 if nx else 0)

    def kern(*refs):
        i = pl.program_id(0)
        groups, at = [], 0
        for n in sizes:
            groups.append(refs[at:at + n])
            at += n
        row_refs, halo_refs, res_refs, x_src, out_refs, acc_refs, x_dst, scr_refs, sems = groups

        @pl.when(i == 0)
        def _():
            for r in tuple(acc_refs) + tuple(scr_refs):
                r[...] = jnp.zeros(r.shape, r.dtype)
            if nx:
                _exchange_copies(exchange, x_src, x_dst, sems, start=True)

        body(pos(i), row_refs, halo_refs, res_refs, out_refs, acc_refs, scr_refs)

        if nx:
            @pl.when(i == nt - 1)
            def _():
                _exchange_copies(exchange, x_src, x_dst, sems, start=False)

    result = pl.pallas_call(
        kern, grid=(nt,), in_specs=in_specs, out_specs=out_specs, out_shape=out_shape,
        scratch_shapes=[pltpu.VMEM(tuple(s), d) for s, d in scratch] + (_exchange_sems(nx) if nx else []),
        compiler_params=pltpu.CompilerParams(dimension_semantics=("arbitrary",),
                                             vmem_limit_bytes=VMEM_LIMIT_MB << 20),
        name=name)(*operands)
    return tuple(result)


def _tn_call(name, a, b, k, n, nblk=1, a_col=False, b_col=True, b_off=0, out='cols', tile=2048):
    tile = min(tile, a.shape[0])
    nt = a.shape[0] // tile
    if out == 'cols':
        o_shape, o_spec = (k, nblk * n), pl.BlockSpec((k, n), lambda j, t: (0, j))
    elif out == 'rows':
        o_shape, o_spec = (nblk * k, n), pl.BlockSpec((k, n), lambda j, t: (j, 0))
    else:
        o_shape, o_spec = (nblk, k, n), pl.BlockSpec((None, k, n), lambda j, t: (j, 0, 0))

    def kern(a_ref, b_ref, o_ref, acc):
        @pl.when(pl.program_id(1) == 0)
        def _():
            acc[...] = jnp.zeros(acc.shape, acc.dtype)

        acc[...] += lax.dot_general(a_ref[...], b_ref[...], _TN, preferred_element_type=F32)

        @pl.when(pl.program_id(1) == nt - 1)
        def _():
            o_ref[...] = acc[...].astype(o_ref.dtype)

    return pl.pallas_call(
        kern, grid=(nblk, nt),
        in_specs=[pl.BlockSpec((tile, k), (lambda j, t: (t, j)) if a_col else (lambda j, t: (t, 0))),
                  pl.BlockSpec((tile, n), (lambda j, t: (t, j + b_off)) if b_col else (lambda j, t: (t, b_off)))],
        out_specs=o_spec, out_shape=jax.ShapeDtypeStruct(o_shape, GRAD_DTYPE),
        scratch_shapes=[pltpu.VMEM((k, n), F32)],
        compiler_params=pltpu.CompilerParams(dimension_semantics=("arbitrary", "arbitrary"),
                                             vmem_limit_bytes=VMEM_LIMIT_MB << 20),
        name=name)(a, b)


def _inproj_fwd(tag, h, g1, w_in_t, tm=512, exchange=()):
    t_len = h.shape[0]

    def body(t, rows, halos, res, outs, accs, scr):
        hv = rows[0][...]
        xn = (hv * _rms_stat(hv) * res[0][...]).astype(_MXU_DTYPE)
        outs[0][...] = xn
        for o, (_, s0, n) in zip(outs[1:], IN_GROUPS):
            o[...] = lax.dot_general(xn, res[1][s0:s0 + n, :], _NT, preferred_element_type=F32)

    outs = [((t_len, D_MODEL), _MXU_DTYPE, tm)] + [((t_len, n), F32, tm) for _, _, n in IN_GROUPS]
    return _row_call("inproj_fwd" + tag, body, t_len // tm, rows=[(h, tm)], res=[g1, w_in_t], outs=outs,
                     exchange=exchange)


def _sg_mask():
    row = lax.broadcasted_iota(jnp.int32, (SG_BLOCK, SG_BLOCK), 0)
    col = lax.broadcasted_iota(jnp.int32, (SG_BLOCK, SG_BLOCK), 1)
    return jnp.logical_or(row >= CHUNK, col < CHUNK)


def _sg_forward_parts(pa, lg, lb, w_ref, bt):
    tm = pa.shape[0]
    nb = tm // SG_BLOCK
    su, sv = pa[:, :BW], pa[:, BW:]
    u, tu = _gelu(su)
    gv, tv = _gelu(sv)
    vn, xh, rs = _ln_fwd(gv, lg, lb)
    mask = _sg_mask()
    wms, xs, ms = [], [], []
    for g in range(SG_GROUPS):
        wm = jnp.where(mask, w_ref[g], 0.0).astype(_MXU_DTYPE)
        xg = jnp.concatenate([vn[b * SG_BLOCK:(b + 1) * SG_BLOCK, g * 128:(g + 1) * 128] for b in range(nb)], axis=1)
        xg = xg.astype(_MXU_DTYPE)
        ms.append(lax.dot_general(wm, xg, _NN, preferred_element_type=F32) + bt[:, g:g + 1])
        wms.append(wm)
        xs.append(xg)
    mixed = _sg_unfold(ms, nb)
    return su, sv, u, tu, tv, xh, rs, wms, xs, mixed


def _sg_unfold(per_group, nb):
    return jnp.concatenate(
        [jnp.concatenate([per_group[g][:, b * 128:(b + 1) * 128] for g in range(SG_GROUPS)], axis=1)
         for b in range(nb)], axis=0)


def _sg_fwd(tag, proj_a, lg, lb, sg_w, sg_bt, tm=1024):
    t_len = proj_a.shape[0]

    def body(t, rows, halos, res, outs, accs, scr):
        parts = _sg_forward_parts(rows[0][...], res[0][...], res[1][...], res[2], res[3][...])
        outs[0][...] = (parts[2] * parts[-1]).astype(_MXU_DTYPE)

    return _row_call("sg_fwd" + tag, body, t_len // tm, rows=[(proj_a, tm)], res=[lg, lb, sg_w, sg_bt],
                     outs=[((t_len, BW), _MXU_DTYPE, tm)])[0]


def _sg_bwd(tag, proj_a, dy, lg, lb, sg_w, sg_bt, tm=1024):
    t_len = proj_a.shape[0]
    nb = tm // SG_BLOCK

    def body(t, rows, halos, res, outs, accs, scr):
        lgv = res[0][...]
        su, sv, u, tu, tv, xh, rs, wms, xs, mixed = _sg_forward_parts(rows[0][...], lgv, res[1][...], res[2], res[3][...])
        dyv = rows[1][...]
        dsu = dyv * mixed * _gelu_grad(su, tu)
        dmixed = dyv * u
        mask = _sg_mask()
        dxs, dbs = [], []
        for g in range(SG_GROUPS):
            dm = jnp.concatenate([dmixed[b * SG_BLOCK:(b + 1) * SG_BLOCK, g * 128:(g + 1) * 128] for b in range(nb)],
                                 axis=1)
            dmb = dm.astype(_MXU_DTYPE)
            dw = lax.dot_general(dmb, xs[g], _NT, preferred_element_type=F32)
            accs[0][g] += jnp.where(mask, dw, 0.0)
            dbs.append(jnp.sum(dm, axis=1, keepdims=True))
            dxs.append(lax.dot_general(wms[g], dmb, _TN, preferred_element_type=F32))
        accs[1][...] += jnp.concatenate(dbs, axis=1)
        dvn = _sg_unfold(dxs, nb)
        dgv, dlg, dlb = _ln_bwd(dvn, xh, rs, lgv)
        accs[2][...] += dlg
        accs[3][...] += dlb
        dsv = dgv * _gelu_grad(sv, tv)
        outs[0][...] = jnp.concatenate([dsu, dsv], axis=1).astype(_MXU_DTYPE)

    return _row_call("sg_bwd" + tag, body, t_len // tm, rows=[(proj_a, tm), (dy, tm)], res=[lg, lb, sg_w, sg_bt],
                     outs=[((t_len, 2 * BW), _MXU_DTYPE, tm)],
                     accs=[(SG_GROUPS, SG_BLOCK, SG_BLOCK), (SG_BLOCK, SG_GROUPS), (1, BW), (1, BW)])


def _chunk_matrix(tm, kind):
    row = lax.broadcasted_iota(jnp.int32, (tm, tm), 0)
    col = lax.broadcasted_iota(jnp.int32, (tm, tm), 1)
    same = lax.shift_right_logical(row, 6) == lax.shift_right_logical(col, 6)
    if kind == 'cumsum':
        same = jnp.logical_and(same, row >= col)
    elif kind == 'revsum':
        same = jnp.logical_and(same, row <= col)
    return same.astype(jnp.bfloat16)


def _gla_gate(pa, wa2, ba):
    z = _mm(pa, wa2) + ba
    log_a = (jnp.minimum(z, 0.0) - jnp.log(1.0 + jnp.exp(-jnp.abs(z)))) * (1.0 / GLA_TAU)
    return z, log_a


def _gla_decay(pb, log_a):
    tm = pb.shape[0]
    cum = _mm_exact_rhs(_chunk_matrix(tm, 'cumsum'), log_a)
    tot = _mm_exact_rhs(_chunk_matrix(tm, 'total'), log_a)
    w = jnp.exp(tot - cum)
    return w, pb[:, 256:512] * w, jnp.exp(tot)


def _per_head(fn):
    return jnp.concatenate([fn(h) for h in range(GLA_HEADS)], axis=1)


def _gla_read(qs, sb, c):
    rows = slice(c * CHUNK, (c + 1) * CHUNK)
    return _per_head(lambda h: lax.dot_general(qs[rows, h * 64:(h + 1) * 64], sb[:, h * 64:(h + 1) * 64], _NT,
                                               preferred_element_type=F32))


def _gla_fwd(tag, proj_b, proj_a, wa2, ba, ng, tm=512, exchange=()):
    t_len = proj_b.shape[0]
    cpt = tm // CHUNK

    def body(t, rows, halos, res, outs, accs, scr):
        pb = rows[0][...]
        _, log_a = _gla_gate(rows[1][...], res[0][...], res[1][...])
        _, kd, dec = _gla_decay(pb, log_a)
        kdb = kd.astype(_MXU_DTYPE)
        vb = pb[:, 512:1024].astype(_MXU_DTYPE)
        qs = (pb[:, 0:256] * (GLA_DK ** -0.5)).astype(_MXU_DTYPE)
        uts = []
        for c in range(cpt):
            rs = slice(c * CHUNK, (c + 1) * CHUNK)
            uts.append(_per_head(lambda h: lax.dot_general(vb[rs, h * 128:(h + 1) * 128], kdb[rs, h * 64:(h + 1) * 64],
                                                           _TN, preferred_element_type=F32)))
        s_new = scr[0][...]
        o = []
        for c in range(cpt):
            s_new = dec[c * CHUNK:c * CHUNK + 1] * s_new + uts[c]
            outs[1][c] = s_new
            o.append(_gla_read(qs, s_new.astype(_MXU_DTYPE), c))
        scr[0][...] = s_new
        o = jnp.concatenate(o, axis=0)
        on = _per_head(lambda h: o[:, h * 128:(h + 1) * 128] * lax.rsqrt(
            jnp.mean(jnp.square(o[:, h * 128:(h + 1) * 128]), axis=-1, keepdims=True) + EPS))
        r = pb[:, 1024:1536]
        outs[0][...] = (on * res[2][...] * (r * _sigmoid(r))).astype(_MXU_DTYPE)

    return _row_call("gla_fwd" + tag, body, t_len // tm, rows=[(proj_b, tm), (proj_a, tm)], res=[wa2, ba, ng],
                     outs=[((t_len, BW), _MXU_DTYPE, tm), ((t_len // CHUNK, GLA_DV, 256), F32, cpt)],
                     scratch=[((GLA_DV, 256), F32)], exchange=exchange)


def _gla_bwd(tag, proj_b, proj_a, dy, states, wa2, ba, ng, tm=512):
    t_len = proj_b.shape[0]
    cpt = tm // CHUNK

    def body(t, rows, halos, res, outs, accs, scr):
        pb = rows[0][...]
        pa = rows[1][...]
        dyv = rows[2][...]
        st_ref = rows[3]
        wa2v = res[0][...]
        z, log_a = _gla_gate(pa, wa2v, res[1][...])
        ngv = res[2][...]
        w, kd, dec = _gla_decay(pb, log_a)
        kdb = kd.astype(_MXU_DTYPE)
        vb = pb[:, 512:1024].astype(_MXU_DTYPE)
        qs = (pb[:, 0:256] * (GLA_DK ** -0.5)).astype(_MXU_DTYPE)
        chunks = [slice(c * CHUNK, (c + 1) * CHUNK) for c in range(cpt)]
        sbs = [st_ref[c].astype(_MXU_DTYPE) for c in range(cpt)]
        o = jnp.concatenate([_gla_read(qs, sbs[c], c) for c in range(cpt)], axis=0)
        r = pb[:, 1024:1536]
        sig = _sigmoid(r)
        sil = r * sig
        dos, ons = [], []
        for h in range(GLA_HEADS):
            hs = slice(h * 128, (h + 1) * 128)
            oh = o[:, hs]
            rstd = lax.rsqrt(jnp.mean(oh * oh, axis=-1, keepdims=True) + EPS)
            on = oh * rstd
            don = dyv[:, hs] * ngv[:, hs] * sil[:, hs]
            dos.append(rstd * (don - on * jnp.mean(don * on, axis=-1, keepdims=True)))
            ons.append(on)
        on = jnp.concatenate(ons, axis=1)
        accs[2][...] += jnp.sum(dyv * on * sil, axis=0, keepdims=True)
        dr = dyv * on * ngv * (sig * (1.0 + r * (1.0 - sig)))
        dob = jnp.concatenate(dos, axis=1).astype(_MXU_DTYPE)
        reads, dqs = [], []
        for c, rs in enumerate(chunks):
            reads.append(_per_head(lambda h: lax.dot_general(dob[rs, h * 128:(h + 1) * 128], qs[rs, h * 64:(h + 1) * 64],
                                                             _TN, preferred_element_type=F32)))
            dqs.append(_per_head(lambda h: lax.dot_general(dob[rs, h * 128:(h + 1) * 128], sbs[c][:, h * 64:(h + 1) * 64],
                                                           _NN, preferred_element_type=F32)))
        dst = scr[0][...]
        dubs, ddecs = [None] * cpt, [None] * cpt
        for c in reversed(range(cpt)):
            dst_tot = dst + reads[c]
            s_prev = st_ref[c - 1] if c > 0 else jnp.where(t > 0, halos[0][0], 0.0)
            ddecs[c] = jnp.broadcast_to(jnp.sum(dst_tot * s_prev, axis=0, keepdims=True), (CHUNK, 256))
            dst = dec[c * CHUNK:c * CHUNK + 1] * dst_tot
            dubs[c] = dst_tot.astype(_MXU_DTYPE)
        scr[0][...] = dst
        dkd = jnp.concatenate(
            [_per_head(lambda h: lax.dot_general(vb[rs, h * 128:(h + 1) * 128], dubs[c][:, h * 64:(h + 1) * 64], _NN,
                                                 preferred_element_type=F32)) for c, rs in enumerate(chunks)], axis=0)
        dv = jnp.concatenate(
            [_per_head(lambda h: lax.dot_general(kdb[rs, h * 64:(h + 1) * 64], dubs[c][:, h * 64:(h + 1) * 64], _NT,
                                                 preferred_element_type=F32)) for c, rs in enumerate(chunks)], axis=0)
        e = dkd * kd
        dtot = _mm_exact_rhs(_chunk_matrix(tm, 'total'), e) + jnp.concatenate(ddecs, axis=0) * dec
        last = (lax.broadcasted_iota(jnp.int32, e.shape, 0) & (CHUNK - 1)) == CHUNK - 1
        dla = _mm_exact_rhs(_chunk_matrix(tm, 'revsum'), jnp.where(last, dtot - e, -e))
        dz = dla * (1.0 / GLA_TAU) * _sigmoid(-z)
        dzb = dz.astype(_MXU_DTYPE)
        dq = jnp.concatenate(dqs, axis=0) * (GLA_DK ** -0.5)
        da = lax.dot_general(dzb, wa2v.astype(_MXU_DTYPE), _NT, preferred_element_type=F32)
        outs[0][...] = jnp.concatenate([dq, dkd * w, dv, dr, da], axis=1).astype(_MXU_DTYPE)
        accs[0][...] += lax.dot_general(pa.astype(_MXU_DTYPE), dzb, _TN, preferred_element_type=F32)
        accs[1][...] += jnp.sum(dz, axis=0, keepdims=True)

    return _row_call("gla_bwd" + tag, body, t_len // tm,
                     rows=[(proj_b, tm), (proj_a, tm), (dy, tm), (states, cpt)],
                     halos=[(states, 1, cpt, 'prev')], res=[wa2, ba, ng],
                     outs=[((t_len, 1536 + GLA_RANK), _MXU_DTYPE, tm)],
                     accs=[(GLA_RANK, 256), (1, 256), (1, BW)], scratch=[((GLA_DV, 256), F32)], reverse=True)


ATT_TM = 256
ATT_KEYS = ATT_TM + (ATT_BAND - 1) * CHUNK


def _rel_index():
    l_idx = np.arange(CHUNK)[:, None]
    m_idx = np.arange(BAND)[None, :]
    rel = l_idx + (ATT_BAND - 1) * CHUNK - m_idx
    return jnp.asarray((np.clip(rel, -(CHUNK - 1), MAX_REL) + (CHUNK - 1)).reshape(1, CHUNK * BAND), jnp.int32)


BIAS_COLS = 4096


def _bias_expand(tag, rel_bias):
    n = CHUNK * BAND

    def kern(rel_ref, idx_ref, o_ref):
        onehot = (lax.broadcasted_iota(jnp.int32, (REL_TABLE, BIAS_COLS), 0) == idx_ref[...]).astype(jnp.bfloat16)
        o_ref[...] = _mm_exact_lhs(rel_ref[...], onehot)

    return pl.pallas_call(
        kern, grid=(n // BIAS_COLS,),
        in_specs=[pl.BlockSpec((ATT_HEADS, REL_TABLE), lambda i: (0, 0)), pl.BlockSpec((1, BIAS_COLS), lambda i: (0, i))],
        out_specs=pl.BlockSpec((ATT_HEADS, BIAS_COLS), lambda i: (0, i)),
        out_shape=jax.ShapeDtypeStruct((ATT_HEADS, n), F32), name="bias_expand" + tag)(rel_bias, _rel_index())


def _bias_tile(tag, bias):
    per = ATT_TM // CHUNK

    def kern(b_ref, o_ref):
        bv = b_ref[...]
        for j in range(per):
            parts = [jnp.full((CHUNK, j * CHUNK), NEG_INF, F32)] if j else []
            parts.append(bv)
            if j < per - 1:
                parts.append(jnp.full((CHUNK, (per - 1 - j) * CHUNK), NEG_INF, F32))
            o_ref[j * CHUNK:(j + 1) * CHUNK, :] = jnp.concatenate(parts, axis=1)

    return pl.pallas_call(
        kern, grid=(ATT_HEADS,), in_specs=[pl.BlockSpec((None, CHUNK, BAND), lambda h: (h, 0, 0))],
        out_specs=pl.BlockSpec((None, ATT_TM, ATT_KEYS), lambda h: (h, 0, 0)),
        out_shape=jax.ShapeDtypeStruct((ATT_HEADS, ATT_TM, ATT_KEYS), F32), name="bias_tile" + tag)(bias)


def _bias_untile(tag, dbias):
    per = ATT_TM // CHUNK

    def kern(d_ref, o_ref):
        acc = d_ref[0:CHUNK, 0:BAND]
        for j in range(1, per):
            acc = acc + d_ref[j * CHUNK:(j + 1) * CHUNK, j * CHUNK:j * CHUNK + BAND]
        o_ref[...] = acc

    return pl.pallas_call(
        kern, grid=(ATT_HEADS,), in_specs=[pl.BlockSpec((None, ATT_TM, ATT_KEYS), lambda h: (h, 0, 0))],
        out_specs=pl.BlockSpec((None, CHUNK, BAND), lambda h: (h, 0, 0)),
        out_shape=jax.ShapeDtypeStruct((ATT_HEADS, CHUNK, BAND), F32), name="bias_untile" + tag)(dbias)


def _bias_reduce(tag, dbias):
    n = CHUNK * BAND

    def kern(db_ref, idx_ref, o_ref):
        @pl.when(pl.program_id(0) == 0)
        def _():
            o_ref[...] = jnp.zeros(o_ref.shape, o_ref.dtype)

        onehot = (lax.broadcasted_iota(jnp.int32, (REL_TABLE, BIAS_COLS), 0) == idx_ref[...]).astype(jnp.bfloat16)
        o_ref[...] += _mm_exact_lhs(db_ref[...], onehot, _NT)

    return pl.pallas_call(
        kern, grid=(n // BIAS_COLS,),
        in_specs=[pl.BlockSpec((ATT_HEADS, BIAS_COLS), lambda i: (0, i)), pl.BlockSpec((1, BIAS_COLS), lambda i: (0, i))],
        out_specs=pl.BlockSpec((ATT_HEADS, REL_TABLE), lambda i: (0, 0)),
        out_shape=jax.ShapeDtypeStruct((ATT_HEADS, REL_TABLE), F32),
        compiler_params=pltpu.CompilerParams(dimension_semantics=("arbitrary",)),
        name="bias_reduce" + tag)(dbias, _rel_index())


def _attn_stage(t, pc_ref, p1_ref, p2_ref, kv):
    tm = ATT_TM
    kv[0:tm, :] = jnp.where(t > 1, p2_ref[:, 512:1536], 0.0).astype(kv.dtype)
    kv[tm:2 * tm, :] = jnp.where(t > 0, p1_ref[:, 512:1536], 0.0).astype(kv.dtype)
    kv[2 * tm:, :] = pc_ref[:, 512:1536].astype(kv.dtype)
    q = (pc_ref[:, 0:512] * (ATT_HD ** -0.5)).astype(_MXU_DTYPE)
    ok = lax.broadcasted_iota(jnp.int32, (tm, ATT_KEYS), 1) >= (2 - t) * tm
    return q, ok


def _attn_probs(q, kv, bias_h, ok, h):
    hs = slice(h * ATT_HD, (h + 1) * ATT_HD)
    s = lax.dot_general(q[:, hs], kv[:, hs], _NT, preferred_element_type=F32) + bias_h
    if ok is not None:
        s = jnp.where(ok, s, NEG_INF)
    e = jnp.exp(s - jnp.max(s, axis=-1, keepdims=True))
    return e * (1.0 / jnp.sum(e, axis=-1, keepdims=True))


def _attn_halos(proj_c):
    return [(proj_c, ATT_TM, 1, 'prev'), (proj_c, ATT_TM, 1, 'prev2')]


def _attn_fwd(tag, proj_c, bias, exchange=()):
    t_len = proj_c.shape[0]
    tm = ATT_TM

    def body(t, rows, halos, res, outs, accs, scr):
        b_ref, kv = res[0], scr[0]
        q, ok = _attn_stage(t, rows[0], halos[0], halos[1], kv)

        def heads(ok):
            o = []
            for h in range(ATT_HEADS):
                p = _attn_probs(q, kv, b_ref[h], ok, h).astype(_MXU_DTYPE)
                outs[1][:, h * ATT_KEYS:(h + 1) * ATT_KEYS] = p
                o.append(lax.dot_general(p, kv[:, BW + h * ATT_HD:BW + (h + 1) * ATT_HD], _NN,
                                         preferred_element_type=F32))
            outs[0][...] = jnp.concatenate(o, axis=1).astype(_MXU_DTYPE)

        pl.when(t < 2)(lambda: heads(ok))
        pl.when(t >= 2)(lambda: heads(None))

    return _row_call("attn_fwd" + tag, body, t_len // tm, rows=[(proj_c, tm)], halos=_attn_halos(proj_c),
                     res=[bias], outs=[((t_len, BW), _MXU_DTYPE, tm), ((t_len, ATT_HEADS * ATT_KEYS), _MXU_DTYPE, tm)],
                     scratch=[((ATT_KEYS, 1024), _MXU_DTYPE)], exchange=exchange)


def _attn_bwd(tag, proj_c, dy, probs, exchange=()):
    t_len = proj_c.shape[0]
    tm = ATT_TM
    scale = ATT_HD ** -0.5

    def body(t, rows, halos, res, outs, accs, scr):
        kv = scr[0]
        q, _ = _attn_stage(t, rows[0], halos[0], halos[1], kv)
        do = rows[1][...].astype(_MXU_DTYPE)
        dqs, dks, dvs = [], [], []
        for h in range(ATT_HEADS):
            hs = slice(h * ATT_HD, (h + 1) * ATT_HD)
            vs = slice(BW + h * ATT_HD, BW + (h + 1) * ATT_HD)
            pb = rows[2][:, h * ATT_KEYS:(h + 1) * ATT_KEYS]
            p = pb.astype(F32)
            dp = lax.dot_general(do[:, hs], kv[:, vs], _NT, preferred_element_type=F32)
            ds = p * (dp - jnp.sum(dp * p, axis=-1, keepdims=True))
            accs[0][h] += ds
            dsb = ds.astype(_MXU_DTYPE)
            dqs.append(lax.dot_general(dsb, kv[:, hs], _NN, preferred_element_type=F32) * scale)
            dks.append(lax.dot_general(dsb, q[:, hs], _TN, preferred_element_type=F32))
            dvs.append(lax.dot_general(pb, do[:, hs], _TN, preferred_element_type=F32))
        outs[0][...] = jnp.concatenate(dqs, axis=1).astype(_MXU_DTYPE)
        dkv = jnp.concatenate(dks + dvs, axis=1)
        after_one, after_two = scr[1], scr[2]
        outs[1][...] = (dkv[2 * tm:, :] + after_two[...]).astype(_MXU_DTYPE)
        after_two[...] = dkv[tm:2 * tm, :] + after_one[...]
        after_one[...] = dkv[0:tm, :]

    return _row_call("attn_bwd" + tag, body, t_len // tm, rows=[(proj_c, tm), (dy, tm), (probs, tm)],
                     halos=_attn_halos(proj_c),
                     outs=[((t_len, BW), _MXU_DTYPE, tm), ((t_len, 1024), _MXU_DTYPE, tm)],
                     accs=[(ATT_HEADS, ATT_TM, ATT_KEYS)],
                     scratch=[((ATT_KEYS, 1024), _MXU_DTYPE), ((tm, 1024), F32), ((tm, 1024), F32)],
                     reverse=True, exchange=exchange)


def _conv_glu(pd):
    a, g = pd[:, :BW], pd[:, BW:]
    sig = _sigmoid(g)
    return a, sig, a * sig


def _conv_stage(t, pd_ref, ph_ref, win):
    pd = pd_ref[...]
    a, sig, y0 = _conv_glu(pd)
    win[0:CONV_HALO, :] = jnp.where(t > 0, _conv_glu(ph_ref[...])[2], 0.0)
    win[CONV_HALO:CONV_HALO + pd.shape[0], :] = y0
    return a, sig


SUBLANES = 8


def _conv_shifted(win, sh):
    for b in range(SUBLANES):
        sh[b] = win[pl.ds(b, sh.shape[1]), :]


def _conv_taps_by_copy(offsets):
    groups = {}
    for j, o in enumerate(offsets):
        groups.setdefault(o % SUBLANES, []).append((j, o - o % SUBLANES))
    return [(rem, min(a for _, a in taps), max(a for _, a in taps) - min(a for _, a in taps), taps)
            for rem, taps in sorted(groups.items())]


def _conv_span(sh, rem, r0, lo, rows):
    return sh[rem, pl.ds(pl.multiple_of(r0 + lo, SUBLANES), rows), :]


def _conv_tap_sum(sh, w_ref, offsets, out_ref, init=None, rb=32):
    plan = _conv_taps_by_copy(offsets)

    def block(i, carry):
        r0 = pl.multiple_of(i * rb, rb)
        acc = jnp.zeros((rb, BW), F32) if init is None else jnp.broadcast_to(init, (rb, BW))
        for rem, lo, extra, taps in plan:
            span = _conv_span(sh, rem, r0, lo, extra + rb)
            for j, a in taps:
                acc = acc + w_ref[j:j + 1, :] * span[a - lo:a - lo + rb]
        out_ref[pl.ds(r0, rb), :] = acc
        return carry

    lax.fori_loop(0, out_ref.shape[0] // rb, block, 0)


def _conv_tap_corr(sh, d_ref, rows, offsets, acc_ref, rb=32):
    for rem, lo, extra, taps in _conv_taps_by_copy(offsets):
        def block(i, sums, rem=rem, lo=lo, extra=extra, taps=taps):
            r0 = pl.multiple_of(i * rb, rb)
            d = d_ref[pl.ds(r0, rb), :]
            span = _conv_span(sh, rem, r0, lo, extra + rb)
            out = []
            for s, (j, a) in zip(sums, taps):
                prod = d * span[a - lo:a - lo + rb]
                for k in range(0, rb, SUBLANES):
                    s = s + prod[k:k + SUBLANES]
                out.append(s)
            return tuple(out)

        sums = lax.fori_loop(0, rows // rb, block,
                             tuple(jnp.zeros((SUBLANES, BW), F32) for _ in taps))
        for (j, _), s in zip(taps, sums):
            acc_ref[j:j + 1, :] += jnp.sum(s, axis=0, keepdims=True)


def _conv_scratch(tm):
    return [((tm + CONV_HALO + SUBLANES, BW), F32), ((SUBLANES, tm + CONV_HALO, BW), F32)]


def _conv_fwd(tag, proj_d, dw_w, dw_b, ln_g, ln_b, tm=512):
    t_len = proj_d.shape[0]
    lead = CONV_HALO - (CONV_K - 1)

    def body(t, rows, halos, res, outs, accs, scr):
        win, sh = scr
        _conv_stage(t, rows[0], halos[0], win)
        _conv_shifted(win, sh)
        _conv_tap_sum(sh, res[0], [lead + j for j in range(CONV_K)], outs[1], init=res[1][...])
        yl, _, _ = _ln_fwd(outs[1][...], res[2][...], res[3][...])
        outs[0][...] = (yl * _sigmoid(yl)).astype(_MXU_DTYPE)

    return _row_call("conv_fwd" + tag, body, t_len // tm, rows=[(proj_d, tm)],
                     halos=[(proj_d, CONV_HALO, tm // CONV_HALO, 'prev')], res=[dw_w, dw_b, ln_g, ln_b],
                     outs=[((t_len, BW), _MXU_DTYPE, tm), ((t_len, BW), F32, tm)], scratch=_conv_scratch(tm))


def _conv_norm_bwd(yc, dy, lgv, lbv):
    yl, xh, rs = _ln_fwd(yc, lgv, lbv)
    sig = _sigmoid(yl)
    return _ln_bwd(dy * (sig * (1.0 + yl * (1.0 - sig))), xh, rs, lgv)


def _conv_bwd(tag, proj_d, yc, dy, dw_w, ln_g, ln_b, tm=512):
    t_len = proj_d.shape[0]
    nt = t_len // tm
    lead = CONV_HALO - (CONV_K - 1)
    per = tm // CONV_HALO

    def body(t, rows, halos, res, outs, accs, scr):
        win, sh, wd, shd, dy0_ref = scr
        lgv, lbv = res[1][...], res[2][...]
        a, sig = _conv_stage(t, rows[0], halos[0], win)
        _conv_shifted(win, sh)
        dyc, dlg, dlb = _conv_norm_bwd(rows[1][...], rows[2][...], lgv, lbv)
        accs[1][...] += dlg
        accs[2][...] += dlb
        accs[3][...] += jnp.sum(dyc, axis=0, keepdims=True)
        wd[0:tm, :] = dyc
        wd[tm:tm + CONV_HALO, :] = jnp.where(t < nt - 1, _conv_norm_bwd(halos[1][...], halos[2][...], lgv, lbv)[0], 0.0)
        _conv_shifted(wd, shd)
        _conv_tap_corr(sh, wd, tm, [lead + j for j in range(CONV_K)], accs[0])
        _conv_tap_sum(shd, res[0], [CONV_K - 1 - j for j in range(CONV_K)], dy0_ref)
        dy0 = dy0_ref[...]
        outs[0][...] = jnp.concatenate([dy0 * sig, dy0 * a * sig * (1.0 - sig)], axis=1).astype(_MXU_DTYPE)

    return _row_call("conv_bwd" + tag, body, nt, rows=[(proj_d, tm), (yc, tm), (dy, tm)],
                     halos=[(proj_d, CONV_HALO, per, 'prev'), (yc, CONV_HALO, per, 'next'), (dy, CONV_HALO, per, 'next')],
                     res=[dw_w, ln_g, ln_b], outs=[((t_len, 2 * BW), _MXU_DTYPE, tm)],
                     accs=[(CONV_K, BW), (1, BW), (1, BW), (1, BW)],
                     scratch=_conv_scratch(tm) + _conv_scratch(tm) + [((tm, BW), F32)])


def _merge_fwd(tag, h, xn, ys, w_gate, b_gate, w_branch, w_out, tm=512):
    t_len = h.shape[0]

    def body(t, rows, halos, res, outs, accs, scr):
        xnv = rows[1][...]
        wg_ref, bg_ref, wb_ref, wo_ref = res
        merged = jnp.zeros((tm, D_MODEL), F32)
        for n in range(4):
            cs = slice(n * D_MODEL, (n + 1) * D_MODEL)
            z = lax.dot_general(xnv, wg_ref[n], _NN, preferred_element_type=F32) + bg_ref[n:n + 1, :]
            bo = lax.dot_general(rows[2 + n][...], wb_ref[n], _NN, preferred_element_type=F32)
            outs[0][:, cs] = z.astype(_MXU_DTYPE)
            outs[1][:, cs] = bo.astype(_MXU_DTYPE)
            merged = merged + _sigmoid(z) * bo
        mb = merged.astype(_MXU_DTYPE)
        outs[2][...] = mb
        outs[3][...] = rows[0][...] + lax.dot_general(mb, wo_ref[...], _NN, preferred_element_type=F32)

    return _row_call("merge_fwd" + tag, body, t_len // tm, rows=[(h, tm), (xn, tm)] + [(y, tm) for y in ys],
                     res=[w_gate, b_gate, w_branch, w_out],
                     outs=[((t_len, 4 * D_MODEL), _MXU_DTYPE, tm), ((t_len, 4 * D_MODEL), _MXU_DTYPE, tm),
                           ((t_len, D_MODEL), _MXU_DTYPE, tm), ((t_len, D_MODEL), F32, tm)])


def _merge_bwd(tag, dh, gate_pre, bo, w_gate, w_branch, w_out, tm=256):
    t_len = dh.shape[0]

    def body(t, rows, halos, res, outs, accs, scr):
        wg_ref, wb_ref, wo_ref = res
        dhb = rows[0][...].astype(_MXU_DTYPE)
        outs[0][...] = dhb
        dmerged = lax.dot_general(dhb, wo_ref[...], _NT, preferred_element_type=F32)
        dxn = jnp.zeros((tm, D_MODEL), F32)
        dbg = []
        for n in range(4):
            cs = slice(n * D_MODEL, (n + 1) * D_MODEL)
            g = _sigmoid(rows[1][:, cs].astype(F32))
            dbo = (dmerged * g).astype(_MXU_DTYPE)
            dgp = dmerged * rows[2][:, cs].astype(F32) * (g * (1.0 - g))
            dgb = dgp.astype(_MXU_DTYPE)
            outs[1][:, cs] = dbo
            outs[2][:, cs] = dgb
            outs[4 + n][...] = lax.dot_general(dbo, wb_ref[n], _NT, preferred_element_type=F32)
            dxn = dxn + lax.dot_general(dgb, wg_ref[n], _NT, preferred_element_type=F32)
            dbg.append(jnp.sum(dgp, axis=0, keepdims=True))
        outs[3][...] = dxn
        accs[0][...] += jnp.concatenate(dbg, axis=1)

    return _row_call("merge_bwd" + tag, body, t_len // tm, rows=[(dh, tm), (gate_pre, tm), (bo, tm)],
                     res=[w_gate, w_branch, w_out],
                     outs=[((t_len, D_MODEL), _MXU_DTYPE, tm), ((t_len, 4 * D_MODEL), _MXU_DTYPE, tm),
                           ((t_len, 4 * D_MODEL), _MXU_DTYPE, tm), ((t_len, D_MODEL), F32, tm)]
                     + [((t_len, BW), F32, tm)] * 4,
                     accs=[(1, 4 * D_MODEL)])


FF_COLS = 1024


def _ffn_fwd(tag, h, g2, w1, w2, tm=512):
    t_len = h.shape[0]

    def body(t, rows, halos, res, outs, accs, scr):
        hv = rows[0][...]
        hn = (hv * _rms_stat(hv) * res[0][...]).astype(_MXU_DTYPE)
        outs[0][...] = hn
        acc = hv
        for c in range(D_FF // FF_COLS):
            cs = slice(c * FF_COLS, (c + 1) * FF_COLS)
            pre = lax.dot_general(hn, res[1][:, cs], _NN, preferred_element_type=F32)
            outs[1][:, cs] = pre
            ff = jnp.square(jnp.maximum(pre, 0.0)).astype(_MXU_DTYPE)
            acc = acc + lax.dot_general(ff, res[2][cs, :], _NN, preferred_element_type=F32)
        outs[2][...] = acc

    return _row_call("ffn_fwd" + tag, body, t_len // tm, rows=[(h, tm)], res=[g2, w1, w2],
                     outs=[((t_len, D_MODEL), _MXU_DTYPE, tm), ((t_len, D_FF), F32, tm), ((t_len, D_MODEL), F32, tm)])


def _ffn_bwd(tag, dh, h, pre, g2, w1, w2, tm=256, exchange=()):
    t_len = dh.shape[0]

    def body(t, rows, halos, res, outs, accs, scr):
        dhv = rows[0][...]
        hv = rows[1][...]
        dhb = dhv.astype(_MXU_DTYPE)
        outs[0][...] = dhb
        dhn = jnp.zeros((tm, D_MODEL), F32)
        for c in range(D_FF // FF_COLS):
            cs = slice(c * FF_COLS, (c + 1) * FF_COLS)
            r = jnp.maximum(rows[2][:, cs], 0.0)
            outs[1][:, cs] = (r * r).astype(_MXU_DTYPE)
            dpre = (lax.dot_general(dhb, res[2][cs, :], _NT, preferred_element_type=F32) * (2.0 * r)).astype(_MXU_DTYPE)
            outs[2][:, cs] = dpre
            dhn = dhn + lax.dot_general(dpre, res[1][:, cs], _NT, preferred_element_type=F32)
        dres, dg = _rms_bwd(dhn, hv, res[0][...], _rms_stat(hv))
        outs[3][...] = dhv + dres
        accs[0][...] += dg

    return _row_call("ffn_bwd" + tag, body, t_len // tm, rows=[(dh, tm), (h, tm), (pre, tm)], res=[g2, w1, w2],
                     outs=[((t_len, D_MODEL), _MXU_DTYPE, tm), ((t_len, D_FF), _MXU_DTYPE, tm),
                           ((t_len, D_FF), _MXU_DTYPE, tm), ((t_len, D_MODEL), F32, tm)],
                     accs=[(1, D_MODEL)], exchange=exchange)


def _ple_fwd(tag, h, p, g3, w_pg, b_pg, w_ple, head=None, tm=512):
    t_len = h.shape[0]

    def body(t, rows, halos, res, outs, accs, scr):
        hv = rows[0][...]
        hg = (hv * _rms_stat(hv) * res[0][...]).astype(_MXU_DTYPE)
        pb = rows[1][...].astype(_MXU_DTYPE)
        pg = _sigmoid(lax.dot_general(hg, res[1][...], _NN, preferred_element_type=F32) + res[2][...])
        pe = lax.dot_general(pb, res[3][...], _NN, preferred_element_type=F32)
        outs[0][...] = hg
        outs[1][...] = pb
        outs[2][...] = pg
        h3 = hv + pg * pe
        if head is None:
            outs[3][...] = h3
        else:
            g = res[4][...]
            r = _rms_stat(h3)
            diff = h3 * r * g - rows[2][...]
            accs[0][...] += 0.5 * jnp.sum(jnp.mean(diff * diff, axis=-1, keepdims=True), axis=0, keepdims=True)
            outs[3][...], dg = _rms_bwd(diff * (1.0 / D_MODEL), h3, g, r)
            accs[1][...] += dg

    return _row_call("ple_fwd" + tag, body, t_len // tm, rows=[(h, tm), (p, tm)] + ([(head[0], tm)] if head else []),
                     res=[g3, w_pg, b_pg, w_ple] + ([head[1]] if head else []),
                     outs=[((t_len, D_MODEL), _MXU_DTYPE, tm), ((t_len, PLE_DIM), _MXU_DTYPE, tm),
                           ((t_len, D_MODEL), F32, tm), ((t_len, D_MODEL), F32, tm)],
                     accs=[(1, 128), (1, D_MODEL)] if head else [])


def _ple_bwd(tag, dh, h, pg, p_b, g3, w_pg, w_ple, tm=512):
    t_len = dh.shape[0]

    def body(t, rows, halos, res, outs, accs, scr):
        dhv = rows[0][...]
        hv = rows[1][...]
        pgv = rows[2][...]
        pe = lax.dot_general(rows[3][...], res[2][...], _NN, preferred_element_type=F32)
        dgp = dhv * pe * (pgv * (1.0 - pgv))
        dgb = dgp.astype(_MXU_DTYPE)
        outs[0][...] = dgb
        outs[1][...] = (dhv * pgv).astype(_MXU_DTYPE)
        dhg = lax.dot_general(dgb, res[1][...], _NT, preferred_element_type=F32)
        dres, dg = _rms_bwd(dhg, hv, res[0][...], _rms_stat(hv))
        outs[2][...] = dhv + dres
        accs[0][...] += jnp.sum(dgp, axis=0, keepdims=True)
        accs[1][...] += dg

    return _row_call("ple_bwd" + tag, body, t_len // tm, rows=[(dh, tm), (h, tm), (pg, tm), (p_b, tm)],
                     res=[g3, w_pg, w_ple],
                     outs=[((t_len, D_MODEL), _MXU_DTYPE, tm), ((t_len, D_MODEL), _MXU_DTYPE, tm),
                           ((t_len, D_MODEL), F32, tm)],
                     accs=[(1, D_MODEL), (1, D_MODEL)])


def _inproj_bwd(tag, dh, h, dxn_gate, dprojs, g1, w_in_t, tm=512, exchange=()):
    t_len = dh.shape[0]

    def body(t, rows, halos, res, outs, accs, scr):
        hv = rows[1][...]
        dxn = rows[2][...]
        for dp, (_, s0, n) in zip(rows[3:], IN_GROUPS_BWD):
            dxn = dxn + lax.dot_general(dp[...], res[1][s0:s0 + n, :], _NN, preferred_element_type=F32)
        dres, dg = _rms_bwd(dxn, hv, res[0][...], _rms_stat(hv))
        outs[0][...] = rows[0][...] + dres
        accs[0][...] += dg

    return _row_call("inproj_bwd" + tag, body, t_len // tm,
                     rows=[(dh, tm), (h, tm), (dxn_gate, tm)] + [(d, tm) for d in dprojs],
                     res=[g1, w_in_t], outs=[((t_len, D_MODEL), F32, tm)], accs=[(1, D_MODEL)],
                     exchange=exchange)


def _row(v):
    return v.reshape(1, -1)


GATHER_DURING = (('inproj', ('w_gate', 'w_branch', 'w_out')), ('gla', ('w_ff1',)), ('attn', ('w_ff2', 'w_ple_gate', 'w_ple')))
SCATTER_DURING_ATTN = ('w_ple_gate', 'w_ple', 'w_ff1', 'w_ff2', 'w_out', 'w_gate', 'w_branch')


def _gather_items(shards, names):
    items = []
    for n in names:
        s = shards[n]
        ax = SHARD_AXIS[n] - 1
        if n == 'w_in':
            items.append(_gather_item(s))
        else:
            items.append(_gather_item(s, s.shape[:ax] + (N_DEV * s.shape[ax],) + s.shape[ax + 1:], ax))
    return items


def _land(w, names, arrays):
    for n, a in zip(names, arrays):
        w[n] = a.reshape(IN_COLS, D_MODEL) if n == 'w_in' else a


def _layer_fwd(i, h, p_i, w, shards, next_shards, head=None):
    tag = "_l%d" % i
    during = dict(GATHER_DURING)
    res = _inproj_fwd(tag, h, _row(w['norm1_g']), w['w_in'], exchange=_gather_items(shards, during['inproj']))
    xn, pa, pb, pr, pc, pd = res[:6]
    _land(w, during['inproj'], res[6:])
    sg_bt = w['sg_b'].T
    y_a = _sg_fwd(tag, pa, _row(w['sg_ln_g']), _row(w['sg_ln_b']), w['sg_w'], sg_bt)
    res = _gla_fwd(tag, pb, pr, w['gla_w_a2'], _row(w['gla_b_a']), _row(w['gla_norm_g']),
                   exchange=_gather_items(shards, during['gla']))
    y_b, states = res[:2]
    _land(w, during['gla'], res[2:])
    bias = _bias_tile(tag, _bias_expand(tag, w['att_rel_bias']).reshape(ATT_HEADS, CHUNK, BAND))
    items = _gather_items(shards, during['attn']) + (_gather_items(next_shards, ['w_in']) if next_shards else [])
    res = _attn_fwd(tag, pc, bias, exchange=items)
    y_c, probs = res[:2]
    _land(w, during['attn'], res[2:2 + len(during['attn'])])
    next_w_in = res[-1].reshape(IN_COLS, D_MODEL) if next_shards else None
    y_d, yc = _conv_fwd(tag, pd, w['conv_dw_w'], _row(w['conv_dw_b']), _row(w['conv_ln_g']), _row(w['conv_ln_b']))
    ys = (y_a, y_b, y_c, y_d)
    gate, bo, merged, h1 = _merge_fwd(tag, h, xn, ys, w['w_gate'], w['b_gate'], w['w_branch'], w['w_out'])
    hn, pre, h2 = _ffn_fwd(tag, h1, _row(w['norm2_g']), w['w_ff1'], w['w_ff2'])
    res = _ple_fwd(tag, h2, p_i, _row(w['norm3_g']), w['w_ple_gate'], _row(w['b_ple_gate']), w['w_ple'], head=head)
    hg, p_b, pg, out = res[:4]
    saved = dict(h=h, xn=xn, pa=pa, pb=pb, pr=pr, pc=pc, pd=pd, states=states, probs=probs, yc=yc, ys=ys, gate=gate,
                 bo=bo, merged=merged, h1=h1, hn=hn, pre=pre, h2=h2, hg=hg, p_b=p_b, pg=pg, sg_bt=sg_bt)
    return (out, res[4:]), saved, next_w_in


def _layer_bwd(i, dh3, s, w, tail=None):
    tag = "_l%d" % i
    g = {}
    dgp, dpe, dh2, db_pg, dg3 = _ple_bwd(tag, dh3, s['h2'], s['pg'], s['p_b'], _row(w['norm3_g']), w['w_ple_gate'],
                                         w['w_ple'])
    g['b_ple_gate'], g['norm3_g'] = db_pg[0], dg3[0]
    g['w_ple_gate'] = _tn_call("dw_ple_gate" + tag, s['hg'], dgp, D_MODEL, D_MODEL)
    g['w_ple'] = _tn_call("dw_ple" + tag, s['p_b'], dpe, PLE_DIM, D_MODEL)

    dh2b, ffb, dpre, dh1, dg2 = _ffn_bwd(tag, dh2, s['h1'], s['pre'], _row(w['norm2_g']), w['w_ff1'], w['w_ff2'])
    g['norm2_g'] = dg2[0]
    g['w_ff1'] = _tn_call("dw_ff1" + tag, s['hn'], dpre, D_MODEL, FF_COLS, nblk=D_FF // FF_COLS)
    g['w_ff2'] = _tn_call("dw_ff2" + tag, ffb, dh2b, FF_COLS, D_MODEL, nblk=D_FF // FF_COLS, a_col=True, b_col=False,
                          out='rows')

    dh1b, dbo, dgpre, dxn_gate, dy_a, dy_b, dy_c, dy_d, db_gate = _merge_bwd(
        tag, dh1, s['gate'], s['bo'], w['w_gate'], w['w_branch'], w['w_out'])
    g['b_gate'] = db_gate.reshape(4, D_MODEL)
    g['w_out'] = _tn_call("dw_out" + tag, s['merged'], dh1b, D_MODEL, D_MODEL)
    g['w_gate'] = _tn_call("dw_gate" + tag, s['xn'], dgpre, D_MODEL, D_MODEL, nblk=4, out='stack')
    g['w_branch'] = jnp.stack([_tn_call("dw_branch%d%s" % (n, tag), s['ys'][n], dbo, BW, D_MODEL, b_off=n)
                             for n in range(4)])

    lg, lb = _row(w['sg_ln_g']), _row(w['sg_ln_b'])
    dpa, dsg_w, dsg_bt, dlg, dlb = _sg_bwd(tag, s['pa'], dy_a, lg, lb, w['sg_w'], s['sg_bt'])
    g['sg_w'], g['sg_b'], g['sg_ln_g'], g['sg_ln_b'] = dsg_w, dsg_bt.T, dlg[0], dlb[0]

    dpb, dwa2, dba, dng = _gla_bwd(tag, s['pb'], s['pr'], dy_b, s['states'], w['gla_w_a2'],
                                   _row(w['gla_b_a']), _row(w['gla_norm_g']))
    g['gla_w_a2'], g['gla_b_a'], g['gla_norm_g'] = dwa2, dba[0], dng[0]

    items = [_scatter_item(g.pop(n), axis=SHARD_AXIS[n] - 1) for n in SCATTER_DURING_ATTN]
    res = _attn_bwd(tag, s['pc'], dy_c, s['probs'], exchange=items)
    dq, dkv, dbias = res[:3]
    parts = dict(zip(SCATTER_DURING_ATTN, res[3:]))
    g['att_rel_bias'] = _bias_reduce(tag, _bias_untile(tag, dbias).reshape(ATT_HEADS, CHUNK * BAND))

    cg, cb = _row(w['conv_ln_g']), _row(w['conv_ln_b'])
    dpd, ddw, dcg, dcb, ddwb = _conv_bwd(tag, s['pd'], s['yc'], dy_d, w['conv_dw_w'], cg, cb)
    g['conv_ln_g'], g['conv_ln_b'], g['conv_dw_b'], g['conv_dw_w'] = dcg[0], dcb[0], ddwb[0], ddw

    dprojs = (dpa, dpb, dq, dkv, dpd)
    dw_in = jnp.concatenate([_tn_call("dw_in%s%s" % (name, tag), dp, s['xn'], n, D_MODEL)
                             for (name, _, n), dp in zip(IN_GROUPS_BWD, dprojs)], axis=0)
    items = [_scatter_item(dw_in.reshape(N_DEV, IN_COLS // N_DEV, D_MODEL))] + (tail(g) if tail else [])
    res = _inproj_bwd(tag, dh1, s['h'], dxn_gate, dprojs, _row(w['norm1_g']), w['w_in'], exchange=items)
    dh0, dg1 = res[:2]
    g['norm1_g'] = dg1[0]
    parts['w_in'] = res[2]
    return dh0, g, parts, res[3:]


def _local_step(x, p, target, final_g, layers, shards, tail):
    h = x
    saved = []
    for i in range(DEPTH):
        nxt = shards[i + 1] if i + 1 < DEPTH else None
        head = None if nxt else (target, _row(final_g))
        (h, extra), s, next_w_in = _layer_fwd(i, h, p[i], layers[i], shards[i], nxt, head)
        saved.append(s)
        if nxt:
            layers[i + 1]['w_in'] = next_w_in
    dh, (loss, dgf) = h, extra
    small, parts, tail_out = [None] * DEPTH, [None] * DEPTH, None
    for i in reversed(range(DEPTH)):
        hook = (lambda g: tail([g] + small[1:])) if i == 0 else None
        dh, small[i], parts[i], out = _layer_bwd(i, dh, saved[i], layers[i], hook)
        if i == 0:
            tail_out = out
    return loss[0, 0], dh, dgf[0], small, parts, tail_out


def _peers():
    x, y, c = lax.axis_index("x"), lax.axis_index("y"), lax.axis_index("c")
    me = 4 * x + 2 * y + c
    out = []
    for k in range(1, N_DEV):
        px = (1 - x) if k & 4 else x
        py = (1 - y) if k & 2 else y
        pc = (1 - c) if k & 1 else c
        out.append((k - 1, (px, py, pc), 4 * px + 2 * py + pc))
    return me, out


def _block(ref, axis, idx, width):
    ix = [slice(None)] * len(ref.shape)
    ix[axis] = pl.ds(pl.multiple_of(idx * width, width), width)
    return ref.at[tuple(ix)]


def _slot(ref, idx):
    return ref.at[idx]


def _whole(ref, idx):
    return ref


def _gather_item(src, out_shape=None, axis=None):
    if axis is None:
        return dict(src=src, out=(N_DEV,) + src.shape, take=_whole, put=_slot)
    return dict(src=src, out=tuple(out_shape), take=_whole,
                put=lambda ref, s: _block(ref, axis, s, src.shape[axis]))


def _scatter_item(src, axis=None, lead=0):
    if axis is None:
        shape = src.shape[:lead] + src.shape[lead + 1:]
        take = lambda ref, s: ref.at[(slice(None),) * lead + (s,)]
    else:
        width = src.shape[axis] // N_DEV
        shape = src.shape[:axis] + (width,) + src.shape[axis + 1:]
        take = lambda ref, s: _block(ref, axis, s, width)
    return dict(src=src, out=(N_DEV,) + shape, take=take, put=_slot)


def _exchange_sems(n):
    return [pltpu.SemaphoreType.DMA((n * (N_DEV - 1),)), pltpu.SemaphoreType.DMA((n * (N_DEV - 1),)),
            pltpu.SemaphoreType.DMA((n,))]


def _exchange_copies(items, src_refs, out_refs, sems, start):
    send_sems, recv_sems, local_sems = sems
    me, peers = _peers()

    def remote(i, k, pos, receiver, sender):
        it = items[i]
        return pltpu.make_async_remote_copy(
            src_ref=it['take'](src_refs[i], receiver), dst_ref=it['put'](out_refs[i], sender),
            send_sem=send_sems.at[i * (N_DEV - 1) + k], recv_sem=recv_sems.at[i * (N_DEV - 1) + k],
            device_id=pos, device_id_type=pl.DeviceIdType.MESH)

    local = [pltpu.make_async_copy(it['take'](src_refs[i], me), it['put'](out_refs[i], me), local_sems.at[i])
             for i, it in enumerate(items)]
    if start:
        for cp in local:
            cp.start()
        for k, pos, flat in peers:
            for i in range(len(items)):
                remote(i, k, pos, flat, me).start()
    else:
        for k, pos, flat in peers:
            for i in range(len(items)):
                remote(i, k, pos, flat, flat).wait_recv()
        for k, pos, flat in peers:
            for i in range(len(items)):
                remote(i, k, pos, flat, me).wait_send()
        for cp in local:
            cp.wait()


def _gather_via_sibling(name, srcs):
    n = len(srcs)

    def body(*refs):
        src_refs, out_refs = refs[:n], refs[n:2 * n]
        send_sems, recv_sems, local_sems = refs[2 * n:]
        x, y, c = lax.axis_index("x"), lax.axis_index("y"), lax.axis_index("c")
        flat = lambda px, py, pc: 4 * px + 2 * py + pc
        me, sibling = (x, y, c), (x, y, 1 - c)
        chips = [(1 - x, y), (x, 1 - y), (1 - x, 1 - y)]

        def copy(i, k, block, to, own):
            return pltpu.make_async_remote_copy(
                src_ref=src_refs[i] if own else out_refs[i].at[flat(*block)], dst_ref=out_refs[i].at[flat(*block)],
                send_sem=send_sems.at[i * (N_DEV - 1) + k], recv_sem=recv_sems.at[i * (N_DEV - 1) + k],
                device_id=to, device_id_type=pl.DeviceIdType.MESH)

        local, sent = [], []
        for i in range(n):
            local.append(pltpu.make_async_copy(src_refs[i], out_refs[i].at[flat(*me)], local_sems.at[i]))
            local[-1].start()
            sent.append(copy(i, 0, me, sibling, True))
            sent += [copy(i, 1 + j, me, (*chip, c), True) for j, chip in enumerate(chips)]
            for cp in sent[-4:]:
                cp.start()
        for i in range(n):
            for j, chip in enumerate(chips):
                copy(i, 1 + j, (*chip, c), me, True).wait_recv()
                sent.append(copy(i, 4 + j, (*chip, c), sibling, False))
                sent[-1].start()
        for i in range(n):
            copy(i, 0, sibling, me, True).wait_recv()
            for j, chip in enumerate(chips):
                copy(i, 4 + j, (*chip, 1 - c), me, False).wait_recv()
        for cp in sent:
            cp.wait_send()
        for cp in local:
            cp.wait()

    any_spec = pl.BlockSpec(memory_space=pl.ANY)
    return pl.pallas_call(
        body, out_shape=[jax.ShapeDtypeStruct((N_DEV,) + s.shape, s.dtype) for s in srcs],
        in_specs=[any_spec] * n, out_specs=[any_spec] * n, scratch_shapes=_exchange_sems(n), name=name)(*srcs)


def _pack(arrays, dtype, lead=None):
    flat = [a.astype(dtype).reshape((lead, -1) if lead else (-1,)) for a in arrays]
    cat = jnp.concatenate(flat, axis=-1)
    n = cat.shape[-1]
    rows = -(-n // (PACK_COLS * SUBLANES)) * SUBLANES
    pad = rows * PACK_COLS - n
    if pad:
        cat = jnp.pad(cat, ((0, 0), (0, pad)) if lead else ((0, pad),))
    return cat.reshape((lead, rows, PACK_COLS) if lead else (rows, PACK_COLS))


def _unpack(buf, shapes, lead=None):
    flat = buf.reshape((lead, -1) if lead else (-1,))
    out, off = [], 0
    for shp in shapes:
        n = int(np.prod(shp))
        piece = flat[..., off:off + n]
        out.append(piece.reshape(((lead,) if lead else ()) + tuple(shp)))
        off += n
    return out


def _to_slabs(full, axis):
    shp = full.shape
    split = full.reshape(shp[:axis] + (N_DEV, shp[axis] // N_DEV) + shp[axis + 1:])
    return jnp.moveaxis(split, axis, 0)


def _from_slabs(slabs, axis):
    moved = jnp.moveaxis(slabs, 0, axis)
    shp = moved.shape
    return moved.reshape(shp[:axis] + (shp[axis] * shp[axis + 1],) + shp[axis + 2:])


def _adamw_block(r, c):
    if r % 8:
        return r, 256
    br = min(r, max(8, ADAMW_TILE * PACK_COLS // c))
    while r % br:
        br //= 2
    return br, c


def _adamw(name, partials, w, m, v):
    n_lead, r, c = w.shape
    br, bc = _adamw_block(r, c)
    ni, nj = r // br, c // bc
    c1 = 1.0 - ADAM_B1 ** ADAM_STEP
    c2 = 1.0 - ADAM_B2 ** ADAM_STEP

    def kern(*refs):
        p_refs = refs[:n_lead]
        w_ref, m_ref, v_ref, g_ref, d_ref, nm_ref, nv_ref = refs[n_lead:]
        layer = pl.program_id(0)
        g = None
        for l, p_ref in enumerate(p_refs):
            gl = p_ref[0].astype(F32)
            for s in range(1, N_DEV):
                gl = gl + p_ref[s].astype(F32)
            g = gl if g is None else jnp.where(layer == l, gl, g)
        nm = ADAM_B1 * m_ref[...] + (1.0 - ADAM_B1) * g
        nv = ADAM_B2 * v_ref[...] + (1.0 - ADAM_B2) * jnp.square(g)
        g_ref[...] = g
        nm_ref[...] = nm
        nv_ref[...] = nv
        d_ref[...] = -ADAM_LR * ((nm / c1) / (jnp.sqrt(nv / c2) + ADAM_EPS) + ADAM_WD * w_ref[...])

    def part_spec(mine):
        def index(l, i, j):
            before, after = l < mine, l > mine
            return (0, jnp.where(before, 0, jnp.where(after, ni - 1, i)), jnp.where(before, 0, jnp.where(after, nj - 1, j)))
        return pl.BlockSpec((N_DEV, br, bc), index)

    blk = pl.BlockSpec((None, br, bc), lambda l, i, j: (l, i, j))
    return pl.pallas_call(
        kern, grid=(n_lead, ni, nj),
        in_specs=[part_spec(l) for l in range(n_lead)] + [blk, blk, blk],
        out_specs=[blk] * 4, out_shape=[jax.ShapeDtypeStruct(w.shape, F32)] * 4,
        compiler_params=pltpu.CompilerParams(dimension_semantics=("arbitrary",) * 3),
        name=name)(*partials, w, m, v)


def _as_rows(a, lead):
    return a.reshape(a.shape[:lead] + (-1, a.shape[-1]))


def kernel(x, p, norm1_g, w_in, sg_ln_g, sg_ln_b, sg_w, sg_b, gla_w_a2, gla_b_a, gla_norm_g, att_rel_bias, conv_dw_w, conv_dw_b, conv_ln_g, conv_ln_b, w_branch, w_gate, b_gate, w_out, norm2_g, w_ff1, w_ff2, norm3_g, w_ple_gate, b_ple_gate, w_ple, final_g, loss_target, m_norm1_g, m_w_in, m_sg_ln_g, m_sg_ln_b, m_sg_w, m_sg_b, m_gla_w_a2, m_gla_b_a, m_gla_norm_g, m_att_rel_bias, m_conv_dw_w, m_conv_dw_b, m_conv_ln_g, m_conv_ln_b, m_w_branch, m_w_gate, m_b_gate, m_w_out, m_norm2_g, m_w_ff1, m_w_ff2, m_norm3_g, m_w_ple_gate, m_b_ple_gate, m_w_ple, m_final_g, v_norm1_g, v_w_in, v_sg_ln_g, v_sg_ln_b, v_sg_w, v_sg_b, v_gla_w_a2, v_gla_b_a, v_gla_norm_g, v_att_rel_bias, v_conv_dw_w, v_conv_dw_b, v_conv_ln_g, v_conv_ln_b, v_w_branch, v_w_gate, v_b_gate, v_w_out, v_norm2_g, v_w_ff1, v_w_ff2, v_norm3_g, v_w_ple_gate, v_b_ple_gate, v_w_ple, v_final_g):
    args = locals()
    wts = {n: args[n] for n in WEIGHTS}
    mom = {n: args['m_' + n] for n in WEIGHTS}
    var = {n: args['v_' + n] for n in WEIGHTS}

    local = {d_name: dict(d, w_in=jnp.swapaxes(d['w_in'], 1, 2))
             for d_name, d in (("w", wts), ("m", mom), ("v", var))}
    shards = [{n: local["w"][n][i].astype(_MXU_DTYPE) for n in MXU_WEIGHTS} for i in range(DEPTH)]

    first_w_in, vec = _gather_via_sibling("gather_first_weights",
                                          [shards[0]['w_in'], _pack([wts[n] for n in VEC_WEIGHTS], F32)])
    vec_full = {n: _from_slabs(slabs, SHARD_AXIS[n])
                for n, slabs in zip(VEC_WEIGHTS, _unpack(vec, [wts[n].shape for n in VEC_WEIGHTS], lead=N_DEV))}
    small_names = [n for n in WEIGHTS if n not in MXU_WEIGHTS and n != 'final_g']
    layers = [{n: (vec_full[n] if n in vec_full else wts[n])[i] for n in small_names} for i in range(DEPTH)]
    _land(layers[0], ['w_in'], [first_w_in])

    def vec_items(small):
        stacked = [jnp.stack([g[n] for g in small]) for n in VEC_WEIGHTS]
        return [_scatter_item(_pack([_to_slabs(a, SHARD_AXIS[n]) for n, a in zip(VEC_WEIGHTS, stacked)], F32,
                                    lead=N_DEV))]

    seq = x.shape[1:]
    loss, grad_x, dgf, small, parts, (vec_parts,) = _local_step(
        x.reshape(seq), p.reshape(DEPTH, seq[0], PLE_DIM), loss_target.reshape(seq), final_g, layers, shards, vec_items)
    loss = lax.psum(loss, ("x", "y", "c"))

    grads = {n: jnp.stack([small[i][n] for i in range(DEPTH)]) for n in REPLICATED if n != 'final_g'}
    grads['final_g'] = dgf
    repl_parts, = _gather_via_sibling("gather_replicated_grads", [_pack([grads[n] for n in REPLICATED], F32)])

    results = {}
    for n in MXU_WEIGHTS:
        w3, m3, v3 = (_as_rows(local[d][n], 1) for d in ("w", "m", "v"))
        outs = _adamw("adamw_" + n, [parts[i][n].reshape((N_DEV,) + w3.shape[1:]) for i in range(DEPTH)], w3, m3, v3)
        for kind, a in zip(("grad", "delta", "new_m", "new_v"), outs):
            a = a.reshape(local["w"][n].shape)
            results[kind, n] = jnp.swapaxes(a, 1, 2) if n == 'w_in' else a
    for names, part, call in ((VEC_WEIGHTS, vec_parts, "adamw_vec"), (REPLICATED, repl_parts, "adamw_replicated")):
        packed = [_pack([d[n] for n in names], F32)[None] for d in (wts, mom, var)]
        outs = _adamw(call, [part], *packed)
        for kind, buf in zip(("grad", "delta", "new_m", "new_v"), outs):
            for n, a in zip(names, _unpack(buf[0], [wts[n].shape for n in names])):
                results[kind, n] = a
    return (loss, grad_x[None]) + tuple(results[kind, n] for kind in ("grad", "delta", "new_m", "new_v")
                                        for n in WEIGHTS)
```

```python
import numpy as np
import jax
import jax.numpy as jnp
from jax import lax
from jax.experimental import pallas as pl
from jax.experimental.pallas import tpu as pltpu

F32 = jnp.float32
_MXU_DTYPE = jnp.bfloat16
GRAD_DTYPE = jnp.bfloat16

N_DEV = 8
D_MODEL = 1024
DEPTH = 2
CHUNK = 64
PLE_DIM = 256
BW = 512
SG_BLOCK = 128
SG_GROUPS = 4
GLA_HEADS = 4
GLA_DK = 64
GLA_DV = 128
GLA_RANK = 16
GLA_TAU = 16.0
ATT_HEADS = 8
ATT_HD = 64
ATT_BAND = 9
BAND = ATT_BAND * CHUNK
MAX_REL = 256
REL_TABLE = CHUNK + MAX_REL
CONV_K = 31
CONV_HALO = 32
D_FF = 4096
EPS = 1e-6
NEG_INF = -1e30

IN_GROUPS = (("A", 0, 1024), ("B", 1024, 1536), ("a", 2560, 16), ("C", 2576, 1536), ("D", 4112, 1024))
IN_GROUPS_BWD = (IN_GROUPS[0], ("Ba", 1024, 1552), ("Cq", 2576, 512), ("Ckv", 3088, 1024), IN_GROUPS[4])
IN_COLS = 5136

ADAM_LR = 0.001
ADAM_B1 = 0.9
ADAM_B2 = 0.999
ADAM_EPS = 1e-08
ADAM_WD = 0.01
ADAM_STEP = 10

ADAMW_TILE = 128
PACK_COLS = 1024
VMEM_LIMIT_MB = 56

_NN = (((1,), (0,)), ((), ()))
_NT = (((1,), (1,)), ((), ()))
_TN = (((0,), (0,)), ((), ()))

WEIGHTS = ['norm1_g', 'w_in', 'sg_ln_g', 'sg_ln_b', 'sg_w', 'sg_b', 'gla_w_a2', 'gla_b_a', 'gla_norm_g',
           'att_rel_bias', 'conv_dw_w', 'conv_dw_b', 'conv_ln_g', 'conv_ln_b', 'w_branch', 'w_gate', 'b_gate',
           'w_out', 'norm2_g', 'w_ff1', 'w_ff2', 'norm3_g', 'w_ple_gate', 'b_ple_gate', 'w_ple', 'final_g']
SHARD_AXIS = {'w_in': 2, 'gla_w_a2': 2, 'att_rel_bias': 2, 'conv_dw_w': 2, 'w_branch': 3, 'w_gate': 2,
              'b_gate': 2, 'w_out': 1, 'w_ff1': 2, 'w_ff2': 1, 'w_ple_gate': 1, 'w_ple': 2}
MXU_WEIGHTS = ('w_in', 'w_branch', 'w_gate', 'w_out', 'w_ff1', 'w_ff2', 'w_ple_gate', 'w_ple')
VEC_WEIGHTS = ('gla_w_a2', 'att_rel_bias', 'conv_dw_w', 'b_gate')
REPLICATED = tuple(n for n in WEIGHTS if n not in SHARD_AXIS)


def _mm(a, b, dims=_NN):
    return lax.dot_general(a.astype(_MXU_DTYPE), b.astype(_MXU_DTYPE), dims, preferred_element_type=F32)


def _split3(x):
    x1 = x.astype(jnp.bfloat16)
    r1 = x - x1.astype(F32)
    x2 = r1.astype(jnp.bfloat16)
    x3 = (r1 - x2.astype(F32)).astype(jnp.bfloat16)
    return x1, x2, x3


def _mm_exact_rhs(m, x, dims=_NN):
    return sum(lax.dot_general(m, xi, dims, preferred_element_type=F32) for xi in _split3(x))


def _mm_exact_lhs(x, m, dims=_NN):
    return sum(lax.dot_general(xi, m, dims, preferred_element_type=F32) for xi in _split3(x))


def _sigmoid(x):
    return 1.0 / (1.0 + jnp.exp(-x))


def _gelu(x):
    c = 0.7978845608028654
    t = jnp.tanh(c * (x + 0.044715 * x * x * x))
    return 0.5 * x * (1.0 + t), t


def _gelu_grad(x, t):
    c = 0.7978845608028654
    return 0.5 * (1.0 + t) + 0.5 * x * (1.0 - t * t) * c * (1.0 + 3.0 * 0.044715 * x * x)


def _rms_stat(h):
    return lax.rsqrt(jnp.mean(h * h, axis=-1, keepdims=True) + EPS)


def _rms_bwd(dy, h, g, r):
    hh = h * r
    dhh = dy * g
    dh = r * (dhh - hh * jnp.mean(dhh * hh, axis=-1, keepdims=True))
    return dh, jnp.sum(dy * hh, axis=0, keepdims=True)


def _ln_fwd(x, g, b):
    mu = jnp.mean(x, axis=-1, keepdims=True)
    xc = x - mu
    rs = lax.rsqrt(jnp.mean(xc * xc, axis=-1, keepdims=True) + EPS)
    xh = xc * rs
    return xh * g + b, xh, rs


def _ln_bwd(dy, xh, rs, g):
    dxh = dy * g
    dx = rs * (dxh - jnp.mean(dxh, axis=-1, keepdims=True) - xh * jnp.mean(dxh * xh, axis=-1, keepdims=True))
    return dx, jnp.sum(dy * xh, axis=0, keepdims=True), jnp.sum(dy, axis=0, keepdims=True)


def _row_call(name, body, nt, rows=(), halos=(), res=(), outs=(), accs=(), scratch=(), reverse=False, exchange=()):
    def pos(i):
        return (nt - 1 - i) if reverse else i

    def lead(ndim, f):
        return lambda i: (f(pos(i)),) + (0,) * (ndim - 1)

    in_specs, operands = [], []
    for a, tile in rows:
        in_specs.append(pl.BlockSpec((tile,) + a.shape[1:], lead(a.ndim, lambda t: t)))
        operands.append(a)
    for a, blk, per, side in halos:
        last = a.shape[0] // blk - 1
        delta = {'prev2': -2, 'prev': -1, 'next': per}[side]
        f = lambda t, per=per, last=last, delta=delta: jnp.clip(t * per + delta, 0, last)
        in_specs.append(pl.BlockSpec((blk,) + a.shape[1:], lead(a.ndim, f)))
        operands.append(a)
    for a in res:
        in_specs.append(pl.BlockSpec(a.shape, lambda i, nd=a.ndim: (0,) * nd, pipeline_mode=pl.Buffered(1)))
        operands.append(a)
    out_specs, out_shape = [], []
    for shape, dtype, tile in outs:
        out_specs.append(pl.BlockSpec((tile,) + tuple(shape[1:]), lead(len(shape), lambda t: t)))
        out_shape.append(jax.ShapeDtypeStruct(tuple(shape), dtype))
    for shape in accs:
        out_specs.append(pl.BlockSpec(tuple(shape), lambda i, nd=len(shape): (0,) * nd))
        out_shape.append(jax.ShapeDtypeStruct(tuple(shape), F32))
    nx = len(exchange)
    any_spec = pl.BlockSpec(memory_space=pl.ANY)
    for it in exchange:
        in_specs.append(any_spec)
        operands.append(it['src'])
        out_specs.append(any_spec)
        out_shape.append(jax.ShapeDtypeStruct(it['out'], it['src'].dtype))
    sizes = (len(rows), len(halos), len(res), nx, len(outs), len(accs), nx, len(scratch), 3 if nx else 0)

    def kern(*refs):
        i = pl.program_id(0)
        groups, at = [], 0
        for n in sizes:
            groups.append(refs[at:at + n])
            at += n
        row_refs, halo_refs, res_refs, x_src, out_refs, acc_refs, x_dst, scr_refs, sems = groups

        @pl.when(i == 0)
        def _():
            for r in tuple(acc_refs) + tuple(scr_refs):
                r[...] = jnp.zeros(r.shape, r.dtype)
            if nx:
                _exchange_copies(exchange, x_src, x_dst, sems, start=True)

        body(pos(i), row_refs, halo_refs, res_refs, out_refs, acc_refs, scr_refs)

        if nx:
            @pl.when(i == nt - 1)
            def _():
                _exchange_copies(exchange, x_src, x_dst, sems, start=False)

    result = pl.pallas_call(
        kern, grid=(nt,), in_specs=in_specs, out_specs=out_specs, out_shape=out_shape,
        scratch_shapes=[pltpu.VMEM(tuple(s), d) for s, d in scratch] + (_exchange_sems(nx) if nx else []),
        compiler_params=pltpu.CompilerParams(dimension_semantics=("arbitrary",),
                                             vmem_limit_bytes=VMEM_LIMIT_MB << 20),
        name=name)(*operands)
    return tuple(result)


def _tn_call(name, a, b, k, n, nblk=1, a_col=False, b_col=True, b_off=0, out='cols'):
    tile = 4096 if k + n <= 2048 else 2048
    tile = min(tile, a.shape[0])
    nt = a.shape[0] // tile
    if out == 'cols':
        o_shape, o_spec = (k, nblk * n), pl.BlockSpec((k, n), lambda j, t: (0, j))
    elif out == 'rows':
        o_shape, o_spec = (nblk * k, n), pl.BlockSpec((k, n), lambda j, t: (j, 0))
    else:
        o_shape, o_spec = (nblk, k, n), pl.BlockSpec((None, k, n), lambda j, t: (j, 0, 0))

    def kern(a_ref, b_ref, o_ref, acc):
        @pl.when(pl.program_id(1) == 0)
        def _():
            acc[...] = jnp.zeros(acc.shape, acc.dtype)

        acc[...] += lax.dot_general(a_ref[...], b_ref[...], _TN, preferred_element_type=F32)

        @pl.when(pl.program_id(1) == nt - 1)
        def _():
            o_ref[...] = acc[...].astype(o_ref.dtype)

    return pl.pallas_call(
        kern, grid=(nblk, nt),
        in_specs=[pl.BlockSpec((tile, k), (lambda j, t: (t, j)) if a_col else (lambda j, t: (t, 0))),
                  pl.BlockSpec((tile, n), (lambda j, t: (t, j + b_off)) if b_col else (lambda j, t: (t, b_off)))],
        out_specs=o_spec, out_shape=jax.ShapeDtypeStruct(o_shape, GRAD_DTYPE),
        scratch_shapes=[pltpu.VMEM((k, n), F32)],
        compiler_params=pltpu.CompilerParams(dimension_semantics=("arbitrary", "arbitrary"),
                                             vmem_limit_bytes=VMEM_LIMIT_MB << 20),
        name=name)(a, b)


def _inproj_fwd(tag, h, g1, w_in_t, tm=512, exchange=()):
    t_len = h.shape[0]

    def body(t, rows, halos, res, outs, accs, scr):
        hv = rows[0][...]
        xn = (hv * _rms_stat(hv) * res[0][...]).astype(_MXU_DTYPE)
        outs[0][...] = xn
        for o, (_, s0, n) in zip(outs[1:], IN_GROUPS):
            o[...] = lax.dot_general(xn, res[1][s0:s0 + n, :], _NT, preferred_element_type=F32)

    outs = [((t_len, D_MODEL), _MXU_DTYPE, tm)] + [((t_len, n), F32, tm) for _, _, n in IN_GROUPS]
    return _row_call("inproj_fwd" + tag, body, t_len // tm, rows=[(h, tm)], res=[g1, w_in_t], outs=outs,
                     exchange=exchange)


def _sg_mask():
    row = lax.broadcasted_iota(jnp.int32, (SG_BLOCK, SG_BLOCK), 0)
    col = lax.broadcasted_iota(jnp.int32, (SG_BLOCK, SG_BLOCK), 1)
    return jnp.logical_or(row >= CHUNK, col < CHUNK)


def _sg_forward_parts(pa, lg, lb, w_ref, bt):
    tm = pa.shape[0]
    nb = tm // SG_BLOCK
    su, sv = pa[:, :BW], pa[:, BW:]
    u, tu = _gelu(su)
    gv, tv = _gelu(sv)
    vn, xh, rs = _ln_fwd(gv, lg, lb)
    mask = _sg_mask()
    wms, xs, ms = [], [], []
    for g in range(SG_GROUPS):
        wm = jnp.where(mask, w_ref[g], 0.0).astype(_MXU_DTYPE)
        xg = jnp.concatenate([vn[b * SG_BLOCK:(b + 1) * SG_BLOCK, g * 128:(g + 1) * 128] for b in range(nb)], axis=1)
        xg = xg.astype(_MXU_DTYPE)
        ms.append(lax.dot_general(wm, xg, _NN, preferred_element_type=F32) + bt[:, g:g + 1])
        wms.append(wm)
        xs.append(xg)
    mixed = _sg_unfold(ms, nb)
    return su, sv, u, tu, tv, xh, rs, wms, xs, mixed


def _sg_unfold(per_group, nb):
    return jnp.concatenate(
        [jnp.concatenate([per_group[g][:, b * 128:(b + 1) * 128] for g in range(SG_GROUPS)], axis=1)
         for b in range(nb)], axis=0)


def _sg_fwd(tag, proj_a, lg, lb, sg_w, sg_bt, tm=1024):
    t_len = proj_a.shape[0]

    def body(t, rows, halos, res, outs, accs, scr):
        parts = _sg_forward_parts(rows[0][...], res[0][...], res[1][...], res[2], res[3][...])
        outs[0][...] = (parts[2] * parts[-1]).astype(_MXU_DTYPE)

    return _row_call("sg_fwd" + tag, body, t_len // tm, rows=[(proj_a, tm)], res=[lg, lb, sg_w, sg_bt],
                     outs=[((t_len, BW), _MXU_DTYPE, tm)])[0]


def _sg_bwd(tag, proj_a, dy, lg, lb, sg_w, sg_bt, tm=1024):
    t_len = proj_a.shape[0]
    nb = tm // SG_BLOCK

    def body(t, rows, halos, res, outs, accs, scr):
        lgv = res[0][...]
        su, sv, u, tu, tv, xh, rs, wms, xs, mixed = _sg_forward_parts(rows[0][...], lgv, res[1][...], res[2], res[3][...])
        dyv = rows[1][...]
        dsu = dyv * mixed * _gelu_grad(su, tu)
        dmixed = dyv * u
        mask = _sg_mask()
        dxs, dbs = [], []
        for g in range(SG_GROUPS):
            dm = jnp.concatenate([dmixed[b * SG_BLOCK:(b + 1) * SG_BLOCK, g * 128:(g + 1) * 128] for b in range(nb)],
                                 axis=1)
            dmb = dm.astype(_MXU_DTYPE)
            dw = lax.dot_general(dmb, xs[g], _NT, preferred_element_type=F32)
            accs[0][g] += jnp.where(mask, dw, 0.0)
            dbs.append(jnp.sum(dm, axis=1, keepdims=True))
            dxs.append(lax.dot_general(wms[g], dmb, _TN, preferred_element_type=F32))
        accs[1][...] += jnp.concatenate(dbs, axis=1)
        dvn = _sg_unfold(dxs, nb)
        dgv, dlg, dlb = _ln_bwd(dvn, xh, rs, lgv)
        accs[2][...] += dlg
        accs[3][...] += dlb
        dsv = dgv * _gelu_grad(sv, tv)
        outs[0][...] = jnp.concatenate([dsu, dsv], axis=1).astype(_MXU_DTYPE)

    return _row_call("sg_bwd" + tag, body, t_len // tm, rows=[(proj_a, tm), (dy, tm)], res=[lg, lb, sg_w, sg_bt],
                     outs=[((t_len, 2 * BW), _MXU_DTYPE, tm)],
                     accs=[(SG_GROUPS, SG_BLOCK, SG_BLOCK), (SG_BLOCK, SG_GROUPS), (1, BW), (1, BW)])


def _chunk_matrix(tm, kind):
    row = lax.broadcasted_iota(jnp.int32, (tm, tm), 0)
    col = lax.broadcasted_iota(jnp.int32, (tm, tm), 1)
    same = lax.shift_right_logical(row, 6) == lax.shift_right_logical(col, 6)
    if kind == 'cumsum':
        same = jnp.logical_and(same, row >= col)
    elif kind == 'revsum':
        same = jnp.logical_and(same, row <= col)
    return same.astype(jnp.bfloat16)


def _gla_gate(pa, wa2, ba):
    z = _mm(pa, wa2) + ba
    log_a = (jnp.minimum(z, 0.0) - jnp.log(1.0 + jnp.exp(-jnp.abs(z)))) * (1.0 / GLA_TAU)
    return z, log_a


def _gla_decay(pb, log_a):
    tm = pb.shape[0]
    cum = _mm_exact_rhs(_chunk_matrix(tm, 'cumsum'), log_a)
    tot = _mm_exact_rhs(_chunk_matrix(tm, 'total'), log_a)
    w = jnp.exp(tot - cum)
    return w, pb[:, 256:512] * w, jnp.exp(tot)


def _per_head(fn):
    return jnp.concatenate([fn(h) for h in range(GLA_HEADS)], axis=1)


def _gla_read(qs, sb, c):
    rows = slice(c * CHUNK, (c + 1) * CHUNK)
    return _per_head(lambda h: lax.dot_general(qs[rows, h * 64:(h + 1) * 64], sb[:, h * 64:(h + 1) * 64], _NT,
                                               preferred_element_type=F32))


def _gla_fwd(tag, proj_b, proj_a, wa2, ba, ng, tm=512, exchange=()):
    t_len = proj_b.shape[0]
    cpt = tm // CHUNK

    def body(t, rows, halos, res, outs, accs, scr):
        pb = rows[0][...]
        _, log_a = _gla_gate(rows[1][...], res[0][...], res[1][...])
        _, kd, dec = _gla_decay(pb, log_a)
        kdb = kd.astype(_MXU_DTYPE)
        vb = pb[:, 512:1024].astype(_MXU_DTYPE)
        qs = (pb[:, 0:256] * (GLA_DK ** -0.5)).astype(_MXU_DTYPE)
        uts = []
        for c in range(cpt):
            rs = slice(c * CHUNK, (c + 1) * CHUNK)
            uts.append(_per_head(lambda h: lax.dot_general(vb[rs, h * 128:(h + 1) * 128], kdb[rs, h * 64:(h + 1) * 64],
                                                           _TN, preferred_element_type=F32)))
        s_new = scr[0][...]
        o = []
        for c in range(cpt):
            s_new = dec[c * CHUNK:c * CHUNK + 1] * s_new + uts[c]
            outs[1][c] = s_new
            o.append(_gla_read(qs, s_new.astype(_MXU_DTYPE), c))
        scr[0][...] = s_new
        o = jnp.concatenate(o, axis=0)
        on = _per_head(lambda h: o[:, h * 128:(h + 1) * 128] * lax.rsqrt(
            jnp.mean(jnp.square(o[:, h * 128:(h + 1) * 128]), axis=-1, keepdims=True) + EPS))
        r = pb[:, 1024:1536]
        outs[0][...] = (on * res[2][...] * (r * _sigmoid(r))).astype(_MXU_DTYPE)

    return _row_call("gla_fwd" + tag, body, t_len // tm, rows=[(proj_b, tm), (proj_a, tm)], res=[wa2, ba, ng],
                     outs=[((t_len, BW), _MXU_DTYPE, tm), ((t_len // CHUNK, GLA_DV, 256), F32, cpt)],
                     scratch=[((GLA_DV, 256), F32)], exchange=exchange)


def _gla_bwd(tag, proj_b, proj_a, dy, states, wa2, ba, ng, tm=512):
    t_len = proj_b.shape[0]
    cpt = tm // CHUNK

    def body(t, rows, halos, res, outs, accs, scr):
        pb = rows[0][...]
        pa = rows[1][...]
        dyv = rows[2][...]
        st_ref = rows[3]
        wa2v = res[0][...]
        z, log_a = _gla_gate(pa, wa2v, res[1][...])
        ngv = res[2][...]
        w, kd, dec = _gla_decay(pb, log_a)
        kdb = kd.astype(_MXU_DTYPE)
        vb = pb[:, 512:1024].astype(_MXU_DTYPE)
        qs = (pb[:, 0:256] * (GLA_DK ** -0.5)).astype(_MXU_DTYPE)
        chunks = [slice(c * CHUNK, (c + 1) * CHUNK) for c in range(cpt)]
        sbs = [st_ref[c].astype(_MXU_DTYPE) for c in range(cpt)]
        o = jnp.concatenate([_gla_read(qs, sbs[c], c) for c in range(cpt)], axis=0)
        r = pb[:, 1024:1536]
        sig = _sigmoid(r)
        sil = r * sig
        dos, ons = [], []
        for h in range(GLA_HEADS):
            hs = slice(h * 128, (h + 1) * 128)
            oh = o[:, hs]
            rstd = lax.rsqrt(jnp.mean(oh * oh, axis=-1, keepdims=True) + EPS)
            on = oh * rstd
            don = dyv[:, hs] * ngv[:, hs] * sil[:, hs]
            dos.append(rstd * (don - on * jnp.mean(don * on, axis=-1, keepdims=True)))
            ons.append(on)
        on = jnp.concatenate(ons, axis=1)
        accs[2][...] += jnp.sum(dyv * on * sil, axis=0, keepdims=True)
        dr = dyv * on * ngv * (sig * (1.0 + r * (1.0 - sig)))
        dob = jnp.concatenate(dos, axis=1).astype(_MXU_DTYPE)
        reads, dqs = [], []
        for c, rs in enumerate(chunks):
            reads.append(_per_head(lambda h: lax.dot_general(dob[rs, h * 128:(h + 1) * 128], qs[rs, h * 64:(h + 1) * 64],
                                                             _TN, preferred_element_type=F32)))
            dqs.append(_per_head(lambda h: lax.dot_general(dob[rs, h * 128:(h + 1) * 128], sbs[c][:, h * 64:(h + 1) * 64],
                                                           _NN, preferred_element_type=F32)))
        dst = scr[0][...]
        dubs, ddecs = [None] * cpt, [None] * cpt
        for c in reversed(range(cpt)):
            dst_tot = dst + reads[c]
            s_prev = st_ref[c - 1] if c > 0 else jnp.where(t > 0, halos[0][0], 0.0)
            ddecs[c] = jnp.broadcast_to(jnp.sum(dst_tot * s_prev, axis=0, keepdims=True), (CHUNK, 256))
            dst = dec[c * CHUNK:c * CHUNK + 1] * dst_tot
            dubs[c] = dst_tot.astype(_MXU_DTYPE)
        scr[0][...] = dst
        dkd = jnp.concatenate(
            [_per_head(lambda h: lax.dot_general(vb[rs, h * 128:(h + 1) * 128], dubs[c][:, h * 64:(h + 1) * 64], _NN,
                                                 preferred_element_type=F32)) for c, rs in enumerate(chunks)], axis=0)
        dv = jnp.concatenate(
            [_per_head(lambda h: lax.dot_general(kdb[rs, h * 64:(h + 1) * 64], dubs[c][:, h * 64:(h + 1) * 64], _NT,
                                                 preferred_element_type=F32)) for c, rs in enumerate(chunks)], axis=0)
        e = dkd * kd
        dtot = _mm_exact_rhs(_chunk_matrix(tm, 'total'), e) + jnp.concatenate(ddecs, axis=0) * dec
        last = (lax.broadcasted_iota(jnp.int32, e.shape, 0) & (CHUNK - 1)) == CHUNK - 1
        dla = _mm_exact_rhs(_chunk_matrix(tm, 'revsum'), jnp.where(last, dtot - e, -e))
        dz = dla * (1.0 / GLA_TAU) * _sigmoid(-z)
        dzb = dz.astype(_MXU_DTYPE)
        dq = jnp.concatenate(dqs, axis=0) * (GLA_DK ** -0.5)
        da = lax.dot_general(dzb, wa2v.astype(_MXU_DTYPE), _NT, preferred_element_type=F32)
        outs[0][...] = jnp.concatenate([dq, dkd * w, dv, dr, da], axis=1).astype(_MXU_DTYPE)
        accs[0][...] += lax.dot_general(pa.astype(_MXU_DTYPE), dzb, _TN, preferred_element_type=F32)
        accs[1][...] += jnp.sum(dz, axis=0, keepdims=True)

    return _row_call("gla_bwd" + tag, body, t_len // tm,
                     rows=[(proj_b, tm), (proj_a, tm), (dy, tm), (states, cpt)],
                     halos=[(states, 1, cpt, 'prev')], res=[wa2, ba, ng],
                     outs=[((t_len, 1536 + GLA_RANK), _MXU_DTYPE, tm)],
                     accs=[(GLA_RANK, 256), (1, 256), (1, BW)], scratch=[((GLA_DV, 256), F32)], reverse=True)


ATT_TM = 256
ATT_KEYS = ATT_TM + (ATT_BAND - 1) * CHUNK


def _rel_index():
    l_idx = np.arange(CHUNK)[:, None]
    m_idx = np.arange(BAND)[None, :]
    rel = l_idx + (ATT_BAND - 1) * CHUNK - m_idx
    return jnp.asarray((np.clip(rel, -(CHUNK - 1), MAX_REL) + (CHUNK - 1)).reshape(1, CHUNK * BAND), jnp.int32)


BIAS_COLS = 4096


def _bias_expand(tag, rel_bias):
    n = CHUNK * BAND

    def kern(rel_ref, idx_ref, o_ref):
        onehot = (lax.broadcasted_iota(jnp.int32, (REL_TABLE, BIAS_COLS), 0) == idx_ref[...]).astype(jnp.bfloat16)
        o_ref[...] = _mm_exact_lhs(rel_ref[...], onehot)

    return pl.pallas_call(
        kern, grid=(n // BIAS_COLS,),
        in_specs=[pl.BlockSpec((ATT_HEADS, REL_TABLE), lambda i: (0, 0)), pl.BlockSpec((1, BIAS_COLS), lambda i: (0, i))],
        out_specs=pl.BlockSpec((ATT_HEADS, BIAS_COLS), lambda i: (0, i)),
        out_shape=jax.ShapeDtypeStruct((ATT_HEADS, n), F32), name="bias_expand" + tag)(rel_bias, _rel_index())


def _bias_tile(tag, bias):
    per = ATT_TM // CHUNK

    def kern(b_ref, o_ref):
        bv = b_ref[...]
        for j in range(per):
            parts = [jnp.full((CHUNK, j * CHUNK), NEG_INF, F32)] if j else []
            parts.append(bv)
            if j < per - 1:
                parts.append(jnp.full((CHUNK, (per - 1 - j) * CHUNK), NEG_INF, F32))
            o_ref[j * CHUNK:(j + 1) * CHUNK, :] = jnp.concatenate(parts, axis=1)

    return pl.pallas_call(
        kern, grid=(ATT_HEADS,), in_specs=[pl.BlockSpec((None, CHUNK, BAND), lambda h: (h, 0, 0))],
        out_specs=pl.BlockSpec((None, ATT_TM, ATT_KEYS), lambda h: (h, 0, 0)),
        out_shape=jax.ShapeDtypeStruct((ATT_HEADS, ATT_TM, ATT_KEYS), F32), name="bias_tile" + tag)(bias)


def _bias_untile(tag, dbias):
    per = ATT_TM // CHUNK

    def kern(d_ref, o_ref):
        acc = d_ref[0:CHUNK, 0:BAND]
        for j in range(1, per):
            acc = acc + d_ref[j * CHUNK:(j + 1) * CHUNK, j * CHUNK:j * CHUNK + BAND]
        o_ref[...] = acc

    return pl.pallas_call(
        kern, grid=(ATT_HEADS,), in_specs=[pl.BlockSpec((None, ATT_TM, ATT_KEYS), lambda h: (h, 0, 0))],
        out_specs=pl.BlockSpec((None, CHUNK, BAND), lambda h: (h, 0, 0)),
        out_shape=jax.ShapeDtypeStruct((ATT_HEADS, CHUNK, BAND), F32), name="bias_untile" + tag)(dbias)


def _bias_reduce(tag, dbias):
    n = CHUNK * BAND

    def kern(db_ref, idx_ref, o_ref):
        @pl.when(pl.program_id(0) == 0)
        def _():
            o_ref[...] = jnp.zeros(o_ref.shape, o_ref.dtype)

        onehot = (lax.broadcasted_iota(jnp.int32, (REL_TABLE, BIAS_COLS), 0) == idx_ref[...]).astype(jnp.bfloat16)
        o_ref[...] += _mm_exact_lhs(db_ref[...], onehot, _NT)

    return pl.pallas_call(
        kern, grid=(n // BIAS_COLS,),
        in_specs=[pl.BlockSpec((ATT_HEADS, BIAS_COLS), lambda i: (0, i)), pl.BlockSpec((1, BIAS_COLS), lambda i: (0, i))],
        out_specs=pl.BlockSpec((ATT_HEADS, REL_TABLE), lambda i: (0, 0)),
        out_shape=jax.ShapeDtypeStruct((ATT_HEADS, REL_TABLE), F32),
        compiler_params=pltpu.CompilerParams(dimension_semantics=("arbitrary",)),
        name="bias_reduce" + tag)(dbias, _rel_index())


def _attn_stage(t, pc_ref, p1_ref, p2_ref, kv):
    tm = ATT_TM
    kv[0:tm, :] = jnp.where(t > 1, p2_ref[:, 512:1536], 0.0).astype(kv.dtype)
    kv[tm:2 * tm, :] = jnp.where(t > 0, p1_ref[:, 512:1536], 0.0).astype(kv.dtype)
    kv[2 * tm:, :] = pc_ref[:, 512:1536].astype(kv.dtype)
    q = (pc_ref[:, 0:512] * (ATT_HD ** -0.5)).astype(_MXU_DTYPE)
    ok = lax.broadcasted_iota(jnp.int32, (tm, ATT_KEYS), 1) >= (2 - t) * tm
    return q, ok


def _attn_probs(q, kv, bias_h, ok, h):
    hs = slice(h * ATT_HD, (h + 1) * ATT_HD)
    s = lax.dot_general(q[:, hs], kv[:, hs], _NT, preferred_element_type=F32) + bias_h
    if ok is not None:
        s = jnp.where(ok, s, NEG_INF)
    e = jnp.exp(s - jnp.max(s, axis=-1, keepdims=True))
    return e * (1.0 / jnp.sum(e, axis=-1, keepdims=True))


def _attn_halos(proj_c):
    return [(proj_c, ATT_TM, 1, 'prev'), (proj_c, ATT_TM, 1, 'prev2')]


def _attn_fwd(tag, proj_c, bias, exchange=()):
    t_len = proj_c.shape[0]
    tm = ATT_TM

    def body(t, rows, halos, res, outs, accs, scr):
        b_ref, kv = res[0], scr[0]
        q, ok = _attn_stage(t, rows[0], halos[0], halos[1], kv)

        def heads(ok):
            o = []
            for h in range(ATT_HEADS):
                p = _attn_probs(q, kv, b_ref[h], ok, h).astype(_MXU_DTYPE)
                outs[1][:, h * ATT_KEYS:(h + 1) * ATT_KEYS] = p
                o.append(lax.dot_general(p, kv[:, BW + h * ATT_HD:BW + (h + 1) * ATT_HD], _NN,
                                         preferred_element_type=F32))
            outs[0][...] = jnp.concatenate(o, axis=1).astype(_MXU_DTYPE)

        pl.when(t < 2)(lambda: heads(ok))
        pl.when(t >= 2)(lambda: heads(None))

    return _row_call("attn_fwd" + tag, body, t_len // tm, rows=[(proj_c, tm)], halos=_attn_halos(proj_c),
                     res=[bias], outs=[((t_len, BW), _MXU_DTYPE, tm), ((t_len, ATT_HEADS * ATT_KEYS), _MXU_DTYPE, tm)],
                     scratch=[((ATT_KEYS, 1024), _MXU_DTYPE)], exchange=exchange)


def _attn_bwd(tag, proj_c, dy, probs, exchange=()):
    t_len = proj_c.shape[0]
    tm = ATT_TM
    scale = ATT_HD ** -0.5

    def body(t, rows, halos, res, outs, accs, scr):
        kv = scr[0]
        q, _ = _attn_stage(t, rows[0], halos[0], halos[1], kv)
        do = rows[1][...].astype(_MXU_DTYPE)
        dqs, dks, dvs = [], [], []
        for h in range(ATT_HEADS):
            hs = slice(h * ATT_HD, (h + 1) * ATT_HD)
            vs = slice(BW + h * ATT_HD, BW + (h + 1) * ATT_HD)
            pb = rows[2][:, h * ATT_KEYS:(h + 1) * ATT_KEYS]
            p = pb.astype(F32)
            dp = lax.dot_general(do[:, hs], kv[:, vs], _NT, preferred_element_type=F32)
            ds = p * (dp - jnp.sum(dp * p, axis=-1, keepdims=True))
            accs[0][h] += ds
            dsb = ds.astype(_MXU_DTYPE)
            dqs.append(lax.dot_general(dsb, kv[:, hs], _NN, preferred_element_type=F32) * scale)
            dks.append(lax.dot_general(dsb, q[:, hs], _TN, preferred_element_type=F32))
            dvs.append(lax.dot_general(pb, do[:, hs], _TN, preferred_element_type=F32))
        outs[0][...] = jnp.concatenate(dqs, axis=1).astype(_MXU_DTYPE)
        dkv = jnp.concatenate(dks + dvs, axis=1)
        after_one, after_two = scr[1], scr[2]
        outs[1][...] = (dkv[2 * tm:, :] + after_two[...]).astype(_MXU_DTYPE)
        after_two[...] = dkv[tm:2 * tm, :] + after_one[...]
        after_one[...] = dkv[0:tm, :]

    return _row_call("attn_bwd" + tag, body, t_len // tm, rows=[(proj_c, tm), (dy, tm), (probs, tm)],
                     halos=_attn_halos(proj_c),
                     outs=[((t_len, BW), _MXU_DTYPE, tm), ((t_len, 1024), _MXU_DTYPE, tm)],
                     accs=[(ATT_HEADS, ATT_TM, ATT_KEYS)],
                     scratch=[((ATT_KEYS, 1024), _MXU_DTYPE), ((tm, 1024), F32), ((tm, 1024), F32)],
                     reverse=True, exchange=exchange)


def _conv_glu(pd):
    a, g = pd[:, :BW], pd[:, BW:]
    sig = _sigmoid(g)
    return a, sig, a * sig


def _conv_stage(t, pd_ref, ph_ref, win):
    pd = pd_ref[...]
    a, sig, y0 = _conv_glu(pd)
    win[0:CONV_HALO, :] = jnp.where(t > 0, _conv_glu(ph_ref[...])[2], 0.0)
    win[CONV_HALO:CONV_HALO + pd.shape[0], :] = y0
    return a, sig


SUBLANES = 8


def _conv_shifted(win, sh):
    for b in range(SUBLANES):
        sh[b] = win[pl.ds(b, sh.shape[1]), :]


def _conv_taps_by_copy(offsets):
    groups = {}
    for j, o in enumerate(offsets):
        groups.setdefault(o % SUBLANES, []).append((j, o - o % SUBLANES))
    return [(rem, min(a for _, a in taps), max(a for _, a in taps) - min(a for _, a in taps), taps)
            for rem, taps in sorted(groups.items())]


def _conv_span(sh, rem, r0, lo, rows):
    return sh[rem, pl.ds(pl.multiple_of(r0 + lo, SUBLANES), rows), :]


def _conv_tap_sum(sh, w_ref, offsets, out_ref, init=None, rb=32):
    plan = _conv_taps_by_copy(offsets)

    def block(i, carry):
        r0 = pl.multiple_of(i * rb, rb)
        acc = jnp.zeros((rb, BW), F32) if init is None else jnp.broadcast_to(init, (rb, BW))
        for rem, lo, extra, taps in plan:
            span = _conv_span(sh, rem, r0, lo, extra + rb)
            for j, a in taps:
                acc = acc + w_ref[j:j + 1, :] * span[a - lo:a - lo + rb]
        out_ref[pl.ds(r0, rb), :] = acc
        return carry

    lax.fori_loop(0, out_ref.shape[0] // rb, block, 0)


def _conv_tap_corr(sh, d_ref, rows, offsets, acc_ref, rb=32):
    for rem, lo, extra, taps in _conv_taps_by_copy(offsets):
        def block(i, sums, rem=rem, lo=lo, extra=extra, taps=taps):
            r0 = pl.multiple_of(i * rb, rb)
            d = d_ref[pl.ds(r0, rb), :]
            span = _conv_span(sh, rem, r0, lo, extra + rb)
            out = []
            for s, (j, a) in zip(sums, taps):
                prod = d * span[a - lo:a - lo + rb]
                for k in range(0, rb, SUBLANES):
                    s = s + prod[k:k + SUBLANES]
                out.append(s)
            return tuple(out)

        sums = lax.fori_loop(0, rows // rb, block,
                             tuple(jnp.zeros((SUBLANES, BW), F32) for _ in taps))
        for (j, _), s in zip(taps, sums):
            acc_ref[j:j + 1, :] += jnp.sum(s, axis=0, keepdims=True)


def _conv_scratch(tm):
    return [((tm + CONV_HALO + SUBLANES, BW), F32), ((SUBLANES, tm + CONV_HALO, BW), F32)]


def _conv_fwd(tag, proj_d, dw_w, dw_b, ln_g, ln_b, tm=512):
    t_len = proj_d.shape[0]
    lead = CONV_HALO - (CONV_K - 1)

    def body(t, rows, halos, res, outs, accs, scr):
        win, sh = scr
        _conv_stage(t, rows[0], halos[0], win)
        _conv_shifted(win, sh)
        _conv_tap_sum(sh, res[0], [lead + j for j in range(CONV_K)], outs[1], init=res[1][...])
        yl, _, _ = _ln_fwd(outs[1][...], res[2][...], res[3][...])
        outs[0][...] = (yl * _sigmoid(yl)).astype(_MXU_DTYPE)

    return _row_call("conv_fwd" + tag, body, t_len // tm, rows=[(proj_d, tm)],
                     halos=[(proj_d, CONV_HALO, tm // CONV_HALO, 'prev')], res=[dw_w, dw_b, ln_g, ln_b],
                     outs=[((t_len, BW), _MXU_DTYPE, tm), ((t_len, BW), F32, tm)], scratch=_conv_scratch(tm))


def _conv_norm_bwd(yc, dy, lgv, lbv):
    yl, xh, rs = _ln_fwd(yc, lgv, lbv)
    sig = _sigmoid(yl)
    return _ln_bwd(dy * (sig * (1.0 + yl * (1.0 - sig))), xh, rs, lgv)


def _conv_bwd(tag, proj_d, yc, dy, dw_w, ln_g, ln_b, tm=512):
    t_len = proj_d.shape[0]
    nt = t_len // tm
    lead = CONV_HALO - (CONV_K - 1)
    per = tm // CONV_HALO

    def body(t, rows, halos, res, outs, accs, scr):
        win, sh, wd, shd, dy0_ref = scr
        lgv, lbv = res[1][...], res[2][...]
        a, sig = _conv_stage(t, rows[0], halos[0], win)
        _conv_shifted(win, sh)
        dyc, dlg, dlb = _conv_norm_bwd(rows[1][...], rows[2][...], lgv, lbv)
        accs[1][...] += dlg
        accs[2][...] += dlb
        accs[3][...] += jnp.sum(dyc, axis=0, keepdims=True)
        wd[0:tm, :] = dyc
        wd[tm:tm + CONV_HALO, :] = jnp.where(t < nt - 1, _conv_norm_bwd(halos[1][...], halos[2][...], lgv, lbv)[0], 0.0)
        _conv_shifted(wd, shd)
        _conv_tap_corr(sh, wd, tm, [lead + j for j in range(CONV_K)], accs[0])
        _conv_tap_sum(shd, res[0], [CONV_K - 1 - j for j in range(CONV_K)], dy0_ref)
        dy0 = dy0_ref[...]
        outs[0][...] = jnp.concatenate([dy0 * sig, dy0 * a * sig * (1.0 - sig)], axis=1).astype(_MXU_DTYPE)

    return _row_call("conv_bwd" + tag, body, nt, rows=[(proj_d, tm), (yc, tm), (dy, tm)],
                     halos=[(proj_d, CONV_HALO, per, 'prev'), (yc, CONV_HALO, per, 'next'), (dy, CONV_HALO, per, 'next')],
                     res=[dw_w, ln_g, ln_b], outs=[((t_len, 2 * BW), _MXU_DTYPE, tm)],
                     accs=[(CONV_K, BW), (1, BW), (1, BW), (1, BW)],
                     scratch=_conv_scratch(tm) + _conv_scratch(tm) + [((tm, BW), F32)])


def _merge_fwd(tag, h, xn, ys, w_gate, b_gate, w_branch, w_out, tm=512):
    t_len = h.shape[0]

    def body(t, rows, halos, res, outs, accs, scr):
        xnv = rows[1][...]
        wg_ref, bg_ref, wb_ref, wo_ref = res
        merged = jnp.zeros((tm, D_MODEL), F32)
        for n in range(4):
            cs = slice(n * D_MODEL, (n + 1) * D_MODEL)
            z = lax.dot_general(xnv, wg_ref[n], _NN, preferred_element_type=F32) + bg_ref[n:n + 1, :]
            bo = lax.dot_general(rows[2 + n][...], wb_ref[n], _NN, preferred_element_type=F32)
            outs[0][:, cs] = z.astype(_MXU_DTYPE)
            outs[1][:, cs] = bo.astype(_MXU_DTYPE)
            merged = merged + _sigmoid(z) * bo
        mb = merged.astype(_MXU_DTYPE)
        outs[2][...] = mb
        outs[3][...] = rows[0][...] + lax.dot_general(mb, wo_ref[...], _NN, preferred_element_type=F32)

    return _row_call("merge_fwd" + tag, body, t_len // tm, rows=[(h, tm), (xn, tm)] + [(y, tm) for y in ys],
                     res=[w_gate, b_gate, w_branch, w_out],
                     outs=[((t_len, 4 * D_MODEL), _MXU_DTYPE, tm), ((t_len, 4 * D_MODEL), _MXU_DTYPE, tm),
                           ((t_len, D_MODEL), _MXU_DTYPE, tm), ((t_len, D_MODEL), F32, tm)])


def _merge_bwd(tag, dh, gate_pre, bo, w_gate, w_branch, w_out, tm=256):
    t_len = dh.shape[0]

    def body(t, rows, halos, res, outs, accs, scr):
        wg_ref, wb_ref, wo_ref = res
        dhb = rows[0][...].astype(_MXU_DTYPE)
        outs[0][...] = dhb
        dmerged = lax.dot_general(dhb, wo_ref[...], _NT, preferred_element_type=F32)
        dxn = jnp.zeros((tm, D_MODEL), F32)
        dbg = []
        for n in range(4):
            cs = slice(n * D_MODEL, (n + 1) * D_MODEL)
            g = _sigmoid(rows[1][:, cs].astype(F32))
            dbo = (dmerged * g).astype(_MXU_DTYPE)
            dgp = dmerged * rows[2][:, cs].astype(F32) * (g * (1.0 - g))
            dgb = dgp.astype(_MXU_DTYPE)
            outs[1][:, cs] = dbo
            outs[2][:, cs] = dgb
            outs[4 + n][...] = lax.dot_general(dbo, wb_ref[n], _NT, preferred_element_type=F32)
            dxn = dxn + lax.dot_general(dgb, wg_ref[n], _NT, preferred_element_type=F32)
            dbg.append(jnp.sum(dgp, axis=0, keepdims=True))
        outs[3][...] = dxn
        accs[0][...] += jnp.concatenate(dbg, axis=1)

    return _row_call("merge_bwd" + tag, body, t_len // tm, rows=[(dh, tm), (gate_pre, tm), (bo, tm)],
                     res=[w_gate, w_branch, w_out],
                     outs=[((t_len, D_MODEL), _MXU_DTYPE, tm), ((t_len, 4 * D_MODEL), _MXU_DTYPE, tm),
                           ((t_len, 4 * D_MODEL), _MXU_DTYPE, tm), ((t_len, D_MODEL), F32, tm)]
                     + [((t_len, BW), F32, tm)] * 4,
                     accs=[(1, 4 * D_MODEL)])


FF_COLS = 1024


def _ffn_fwd(tag, h, g2, w1, w2, tm=512):
    t_len = h.shape[0]

    def body(t, rows, halos, res, outs, accs, scr):
        hv = rows[0][...]
        hn = (hv * _rms_stat(hv) * res[0][...]).astype(_MXU_DTYPE)
        outs[0][...] = hn
        acc = hv
        for c in range(D_FF // FF_COLS):
            cs = slice(c * FF_COLS, (c + 1) * FF_COLS)
            pre = lax.dot_general(hn, res[1][:, cs], _NN, preferred_element_type=F32)
            outs[1][:, cs] = pre
            ff = jnp.square(jnp.maximum(pre, 0.0)).astype(_MXU_DTYPE)
            acc = acc + lax.dot_general(ff, res[2][cs, :], _NN, preferred_element_type=F32)
        outs[2][...] = acc

    return _row_call("ffn_fwd" + tag, body, t_len // tm, rows=[(h, tm)], res=[g2, w1, w2],
                     outs=[((t_len, D_MODEL), _MXU_DTYPE, tm), ((t_len, D_FF), F32, tm), ((t_len, D_MODEL), F32, tm)])


def _ffn_bwd(tag, dh, h, pre, g2, w1, w2, tm=256, exchange=()):
    t_len = dh.shape[0]

    def body(t, rows, halos, res, outs, accs, scr):
        dhv = rows[0][...]
        hv = rows[1][...]
        dhb = dhv.astype(_MXU_DTYPE)
        outs[0][...] = dhb
        dhn = jnp.zeros((tm, D_MODEL), F32)
        for c in range(D_FF // FF_COLS):
            cs = slice(c * FF_COLS, (c + 1) * FF_COLS)
            r = jnp.maximum(rows[2][:, cs], 0.0)
            outs[1][:, cs] = (r * r).astype(_MXU_DTYPE)
            dpre = (lax.dot_general(dhb, res[2][cs, :], _NT, preferred_element_type=F32) * (2.0 * r)).astype(_MXU_DTYPE)
            outs[2][:, cs] = dpre
            dhn = dhn + lax.dot_general(dpre, res[1][:, cs], _NT, preferred_element_type=F32)
        dres, dg = _rms_bwd(dhn, hv, res[0][...], _rms_stat(hv))
        outs[3][...] = dhv + dres
        accs[0][...] += dg

    return _row_call("ffn_bwd" + tag, body, t_len // tm, rows=[(dh, tm), (h, tm), (pre, tm)], res=[g2, w1, w2],
                     outs=[((t_len, D_MODEL), _MXU_DTYPE, tm), ((t_len, D_FF), _MXU_DTYPE, tm),
                           ((t_len, D_FF), _MXU_DTYPE, tm), ((t_len, D_MODEL), F32, tm)],
                     accs=[(1, D_MODEL)], exchange=exchange)


def _ple_fwd(tag, h, p, g3, w_pg, b_pg, w_ple, head=None, tm=512):
    t_len = h.shape[0]

    def body(t, rows, halos, res, outs, accs, scr):
        hv = rows[0][...]
        hg = (hv * _rms_stat(hv) * res[0][...]).astype(_MXU_DTYPE)
        pb = rows[1][...].astype(_MXU_DTYPE)
        pg = _sigmoid(lax.dot_general(hg, res[1][...], _NN, preferred_element_type=F32) + res[2][...])
        pe = lax.dot_general(pb, res[3][...], _NN, preferred_element_type=F32)
        outs[0][...] = hg
        outs[1][...] = pb
        outs[2][...] = pg
        h3 = hv + pg * pe
        if head is None:
            outs[3][...] = h3
        else:
            g = res[4][...]
            r = _rms_stat(h3)
            diff = h3 * r * g - rows[2][...]
            accs[0][...] += 0.5 * jnp.sum(jnp.mean(diff * diff, axis=-1, keepdims=True), axis=0, keepdims=True)
            outs[3][...], dg = _rms_bwd(diff * (1.0 / D_MODEL), h3, g, r)
            accs[1][...] += dg

    return _row_call("ple_fwd" + tag, body, t_len // tm, rows=[(h, tm), (p, tm)] + ([(head[0], tm)] if head else []),
                     res=[g3, w_pg, b_pg, w_ple] + ([head[1]] if head else []),
                     outs=[((t_len, D_MODEL), _MXU_DTYPE, tm), ((t_len, PLE_DIM), _MXU_DTYPE, tm),
                           ((t_len, D_MODEL), F32, tm), ((t_len, D_MODEL), F32, tm)],
                     accs=[(1, 128), (1, D_MODEL)] if head else [])


def _ple_bwd(tag, dh, h, pg, p_b, g3, w_pg, w_ple, tm=512):
    t_len = dh.shape[0]

    def body(t, rows, halos, res, outs, accs, scr):
        dhv = rows[0][...]
        hv = rows[1][...]
        pgv = rows[2][...]
        pe = lax.dot_general(rows[3][...], res[2][...], _NN, preferred_element_type=F32)
        dgp = dhv * pe * (pgv * (1.0 - pgv))
        dgb = dgp.astype(_MXU_DTYPE)
        outs[0][...] = dgb
        outs[1][...] = (dhv * pgv).astype(_MXU_DTYPE)
        dhg = lax.dot_general(dgb, res[1][...], _NT, preferred_element_type=F32)
        dres, dg = _rms_bwd(dhg, hv, res[0][...], _rms_stat(hv))
        outs[2][...] = dhv + dres
        accs[0][...] += jnp.sum(dgp, axis=0, keepdims=True)
        accs[1][...] += dg

    return _row_call("ple_bwd" + tag, body, t_len // tm, rows=[(dh, tm), (h, tm), (pg, tm), (p_b, tm)],
                     res=[g3, w_pg, w_ple],
                     outs=[((t_len, D_MODEL), _MXU_DTYPE, tm), ((t_len, D_MODEL), _MXU_DTYPE, tm),
                           ((t_len, D_MODEL), F32, tm)],
                     accs=[(1, D_MODEL), (1, D_MODEL)])


def _inproj_bwd(tag, dh, h, dxn_gate, dprojs, g1, w_in_t, tm=512, exchange=()):
    t_len = dh.shape[0]

    def body(t, rows, halos, res, outs, accs, scr):
        hv = rows[1][...]
        dxn = rows[2][...]
        for dp, (_, s0, n) in zip(rows[3:], IN_GROUPS_BWD):
            dxn = dxn + lax.dot_general(dp[...], res[1][s0:s0 + n, :], _NN, preferred_element_type=F32)
        dres, dg = _rms_bwd(dxn, hv, res[0][...], _rms_stat(hv))
        outs[0][...] = rows[0][...] + dres
        accs[0][...] += dg

    return _row_call("inproj_bwd" + tag, body, t_len // tm,
                     rows=[(dh, tm), (h, tm), (dxn_gate, tm)] + [(d, tm) for d in dprojs],
                     res=[g1, w_in_t], outs=[((t_len, D_MODEL), F32, tm)], accs=[(1, D_MODEL)],
                     exchange=exchange)


def _row(v):
    return v.reshape(1, -1)


GATHER_DURING = (('inproj', ('w_gate', 'w_branch', 'w_out')), ('gla', ('w_ff1',)), ('attn', ('w_ff2', 'w_ple_gate', 'w_ple')))
SCATTER_DURING_ATTN = ('w_ple_gate', 'w_ple', 'w_ff1', 'w_ff2', 'w_out', 'w_gate', 'w_branch')


def _gather_items(shards, names):
    items = []
    for n in names:
        s = shards[n]
        ax = SHARD_AXIS[n] - 1
        if n == 'w_in':
            items.append(_gather_item(s))
        else:
            items.append(_gather_item(s, s.shape[:ax] + (N_DEV * s.shape[ax],) + s.shape[ax + 1:], ax))
    return items


def _land(w, names, arrays):
    for n, a in zip(names, arrays):
        w[n] = a.reshape(IN_COLS, D_MODEL) if n == 'w_in' else a


def _layer_fwd(i, h, p_i, w, shards, next_shards, head=None):
    tag = "_l%d" % i
    during = dict(GATHER_DURING)
    res = _inproj_fwd(tag, h, _row(w['norm1_g']), w['w_in'], exchange=_gather_items(shards, during['inproj']))
    xn, pa, pb, pr, pc, pd = res[:6]
    _land(w, during['inproj'], res[6:])
    sg_bt = w['sg_b'].T
    y_a = _sg_fwd(tag, pa, _row(w['sg_ln_g']), _row(w['sg_ln_b']), w['sg_w'], sg_bt)
    res = _gla_fwd(tag, pb, pr, w['gla_w_a2'], _row(w['gla_b_a']), _row(w['gla_norm_g']),
                   exchange=_gather_items(shards, during['gla']))
    y_b, states = res[:2]
    _land(w, during['gla'], res[2:])
    bias = _bias_tile(tag, _bias_expand(tag, w['att_rel_bias']).reshape(ATT_HEADS, CHUNK, BAND))
    items = _gather_items(shards, during['attn']) + (_gather_items(next_shards, ['w_in']) if next_shards else [])
    res = _attn_fwd(tag, pc, bias, exchange=items)
    y_c, probs = res[:2]
    _land(w, during['attn'], res[2:2 + len(during['attn'])])
    next_w_in = res[-1].reshape(IN_COLS, D_MODEL) if next_shards else None
    y_d, yc = _conv_fwd(tag, pd, w['conv_dw_w'], _row(w['conv_dw_b']), _row(w['conv_ln_g']), _row(w['conv_ln_b']))
    ys = (y_a, y_b, y_c, y_d)
    gate, bo, merged, h1 = _merge_fwd(tag, h, xn, ys, w['w_gate'], w['b_gate'], w['w_branch'], w['w_out'])
    hn, pre, h2 = _ffn_fwd(tag, h1, _row(w['norm2_g']), w['w_ff1'], w['w_ff2'])
    res = _ple_fwd(tag, h2, p_i, _row(w['norm3_g']), w['w_ple_gate'], _row(w['b_ple_gate']), w['w_ple'], head=head)
    hg, p_b, pg, out = res[:4]
    saved = dict(h=h, xn=xn, pa=pa, pb=pb, pr=pr, pc=pc, pd=pd, states=states, probs=probs, yc=yc, ys=ys, gate=gate,
                 bo=bo, merged=merged, h1=h1, hn=hn, pre=pre, h2=h2, hg=hg, p_b=p_b, pg=pg, sg_bt=sg_bt)
    return (out, res[4:]), saved, next_w_in


def _layer_bwd(i, dh3, s, w, tail=None):
    tag = "_l%d" % i
    g = {}
    dgp, dpe, dh2, db_pg, dg3 = _ple_bwd(tag, dh3, s['h2'], s['pg'], s['p_b'], _row(w['norm3_g']), w['w_ple_gate'],
                                         w['w_ple'])
    g['b_ple_gate'], g['norm3_g'] = db_pg[0], dg3[0]
    g['w_ple_gate'] = _tn_call("dw_ple_gate" + tag, s['hg'], dgp, D_MODEL, D_MODEL)
    g['w_ple'] = _tn_call("dw_ple" + tag, s['p_b'], dpe, PLE_DIM, D_MODEL)

    dh2b, ffb, dpre, dh1, dg2 = _ffn_bwd(tag, dh2, s['h1'], s['pre'], _row(w['norm2_g']), w['w_ff1'], w['w_ff2'])
    g['norm2_g'] = dg2[0]
    g['w_ff1'] = _tn_call("dw_ff1" + tag, s['hn'], dpre, D_MODEL, FF_COLS, nblk=D_FF // FF_COLS)
    g['w_ff2'] = _tn_call("dw_ff2" + tag, ffb, dh2b, FF_COLS, D_MODEL, nblk=D_FF // FF_COLS, a_col=True, b_col=False,
                          out='rows')

    dh1b, dbo, dgpre, dxn_gate, dy_a, dy_b, dy_c, dy_d, db_gate = _merge_bwd(
        tag, dh1, s['gate'], s['bo'], w['w_gate'], w['w_branch'], w['w_out'])
    g['b_gate'] = db_gate.reshape(4, D_MODEL)
    g['w_out'] = _tn_call("dw_out" + tag, s['merged'], dh1b, D_MODEL, D_MODEL)
    g['w_gate'] = _tn_call("dw_gate" + tag, s['xn'], dgpre, D_MODEL, D_MODEL, nblk=4, out='stack')
    g['w_branch'] = jnp.stack([_tn_call("dw_branch%d%s" % (n, tag), s['ys'][n], dbo, BW, D_MODEL, b_off=n)
                             for n in range(4)])

    lg, lb = _row(w['sg_ln_g']), _row(w['sg_ln_b'])
    dpa, dsg_w, dsg_bt, dlg, dlb = _sg_bwd(tag, s['pa'], dy_a, lg, lb, w['sg_w'], s['sg_bt'])
    g['sg_w'], g['sg_b'], g['sg_ln_g'], g['sg_ln_b'] = dsg_w, dsg_bt.T, dlg[0], dlb[0]

    dpb, dwa2, dba, dng = _gla_bwd(tag, s['pb'], s['pr'], dy_b, s['states'], w['gla_w_a2'],
                                   _row(w['gla_b_a']), _row(w['gla_norm_g']))
    g['gla_w_a2'], g['gla_b_a'], g['gla_norm_g'] = dwa2, dba[0], dng[0]

    items = [_scatter_item(g.pop(n), axis=SHARD_AXIS[n] - 1) for n in SCATTER_DURING_ATTN]
    res = _attn_bwd(tag, s['pc'], dy_c, s['probs'], exchange=items)
    dq, dkv, dbias = res[:3]
    parts = dict(zip(SCATTER_DURING_ATTN, res[3:]))
    g['att_rel_bias'] = _bias_reduce(tag, _bias_untile(tag, dbias).reshape(ATT_HEADS, CHUNK * BAND))

    cg, cb = _row(w['conv_ln_g']), _row(w['conv_ln_b'])
    dpd, ddw, dcg, dcb, ddwb = _conv_bwd(tag, s['pd'], s['yc'], dy_d, w['conv_dw_w'], cg, cb)
    g['conv_ln_g'], g['conv_ln_b'], g['conv_dw_b'], g['conv_dw_w'] = dcg[0], dcb[0], ddwb[0], ddw

    dprojs = (dpa, dpb, dq, dkv, dpd)
    dw_in = jnp.concatenate([_tn_call("dw_in%s%s" % (name, tag), dp, s['xn'], n, D_MODEL)
                             for (name, _, n), dp in zip(IN_GROUPS_BWD, dprojs)], axis=0)
    items = [_scatter_item(dw_in.reshape(N_DEV, IN_COLS // N_DEV, D_MODEL))] + (tail(g) if tail else [])
    res = _inproj_bwd(tag, dh1, s['h'], dxn_gate, dprojs, _row(w['norm1_g']), w['w_in'], exchange=items)
    dh0, dg1 = res[:2]
    g['norm1_g'] = dg1[0]
    parts['w_in'] = res[2]
    return dh0, g, parts, res[3:]


def _local_step(x, p, target, final_g, layers, shards, tail):
    h = x
    saved = []
    for i in range(DEPTH):
        nxt = shards[i + 1] if i + 1 < DEPTH else None
        head = None if nxt else (target, _row(final_g))
        (h, extra), s, next_w_in = _layer_fwd(i, h, p[i], layers[i], shards[i], nxt, head)
        saved.append(s)
        if nxt:
            layers[i + 1]['w_in'] = next_w_in
    dh, (loss, dgf) = h, extra
    small, parts, tail_out = [None] * DEPTH, [None] * DEPTH, None
    for i in reversed(range(DEPTH)):
        hook = (lambda g: tail([g] + small[1:])) if i == 0 else None
        dh, small[i], parts[i], out = _layer_bwd(i, dh, saved[i], layers[i], hook)
        if i == 0:
            tail_out = out
    return loss[0, 0], dh, dgf[0], small, parts, tail_out


def _peers():
    x, y, c = lax.axis_index("x"), lax.axis_index("y"), lax.axis_index("c")
    me = 4 * x + 2 * y + c
    out = []
    for k in range(1, N_DEV):
        px = (1 - x) if k & 4 else x
        py = (1 - y) if k & 2 else y
        pc = (1 - c) if k & 1 else c
        out.append((k - 1, (px, py, pc), 4 * px + 2 * py + pc))
    return me, out


def _block(ref, axis, idx, width):
    ix = [slice(None)] * len(ref.shape)
    ix[axis] = pl.ds(pl.multiple_of(idx * width, width), width)
    return ref.at[tuple(ix)]


def _slot(ref, idx):
    return ref.at[idx]


def _whole(ref, idx):
    return ref


def _gather_item(src, out_shape=None, axis=None):
    if axis is None:
        return dict(src=src, out=(N_DEV,) + src.shape, take=_whole, put=_slot)
    return dict(src=src, out=tuple(out_shape), take=_whole,
                put=lambda ref, s: _block(ref, axis, s, src.shape[axis]))


def _scatter_item(src, axis=None, lead=0):
    if axis is None:
        shape = src.shape[:lead] + src.shape[lead + 1:]
        take = lambda ref, s: ref.at[(slice(None),) * lead + (s,)]
    else:
        width = src.shape[axis] // N_DEV
        shape = src.shape[:axis] + (width,) + src.shape[axis + 1:]
        take = lambda ref, s: _block(ref, axis, s, width)
    return dict(src=src, out=(N_DEV,) + shape, take=take, put=_slot)


def _exchange_sems(n):
    return [pltpu.SemaphoreType.DMA((n * (N_DEV - 1),)), pltpu.SemaphoreType.DMA((n * (N_DEV - 1),)),
            pltpu.SemaphoreType.DMA((n,))]


def _exchange_copies(items, src_refs, out_refs, sems, start):
    send_sems, recv_sems, local_sems = sems
    me, peers = _peers()

    def remote(i, k, pos, receiver, sender):
        it = items[i]
        return pltpu.make_async_remote_copy(
            src_ref=it['take'](src_refs[i], receiver), dst_ref=it['put'](out_refs[i], sender),
            send_sem=send_sems.at[i * (N_DEV - 1) + k], recv_sem=recv_sems.at[i * (N_DEV - 1) + k],
            device_id=pos, device_id_type=pl.DeviceIdType.MESH)

    local = [pltpu.make_async_copy(it['take'](src_refs[i], me), it['put'](out_refs[i], me), local_sems.at[i])
             for i, it in enumerate(items)]
    if start:
        for cp in local:
            cp.start()
        for k, pos, flat in peers:
            for i in range(len(items)):
                remote(i, k, pos, flat, me).start()
    else:
        for k, pos, flat in peers:
            for i in range(len(items)):
                remote(i, k, pos, flat, flat).wait_recv()
        for k, pos, flat in peers:
            for i in range(len(items)):
                remote(i, k, pos, flat, me).wait_send()
        for cp in local:
            cp.wait()


def _gather_via_sibling(name, srcs):
    n = len(srcs)

    def body(*refs):
        src_refs, out_refs = refs[:n], refs[n:2 * n]
        send_sems, recv_sems, local_sems = refs[2 * n:]
        x, y, c = lax.axis_index("x"), lax.axis_index("y"), lax.axis_index("c")
        flat = lambda px, py, pc: 4 * px + 2 * py + pc
        me, sibling = (x, y, c), (x, y, 1 - c)
        chips = [(1 - x, y), (x, 1 - y), (1 - x, 1 - y)]

        def copy(i, k, block, to, own):
            return pltpu.make_async_remote_copy(
                src_ref=src_refs[i] if own else out_refs[i].at[flat(*block)], dst_ref=out_refs[i].at[flat(*block)],
                send_sem=send_sems.at[i * (N_DEV - 1) + k], recv_sem=recv_sems.at[i * (N_DEV - 1) + k],
                device_id=to, device_id_type=pl.DeviceIdType.MESH)

        local, sent = [], []
        for i in range(n):
            local.append(pltpu.make_async_copy(src_refs[i], out_refs[i].at[flat(*me)], local_sems.at[i]))
            local[-1].start()
            sent.append(copy(i, 0, me, sibling, True))
            sent += [copy(i, 1 + j, me, (*chip, c), True) for j, chip in enumerate(chips)]
            for cp in sent[-4:]:
                cp.start()
        for i in range(n):
            for j, chip in enumerate(chips):
                copy(i, 1 + j, (*chip, c), me, True).wait_recv()
                sent.append(copy(i, 4 + j, (*chip, c), sibling, False))
                sent[-1].start()
        for i in range(n):
            copy(i, 0, sibling, me, True).wait_recv()
            for j, chip in enumerate(chips):
                copy(i, 4 + j, (*chip, 1 - c), me, False).wait_recv()
        for cp in sent:
            cp.wait_send()
        for cp in local:
            cp.wait()

    any_spec = pl.BlockSpec(memory_space=pl.ANY)
    return pl.pallas_call(
        body, out_shape=[jax.ShapeDtypeStruct((N_DEV,) + s.shape, s.dtype) for s in srcs],
        in_specs=[any_spec] * n, out_specs=[any_spec] * n, scratch_shapes=_exchange_sems(n), name=name)(*srcs)


def _pack(arrays, dtype, lead=None):
    flat = [a.astype(dtype).reshape((lead, -1) if lead else (-1,)) for a in arrays]
    cat = jnp.concatenate(flat, axis=-1)
    n = cat.shape[-1]
    rows = -(-n // (PACK_COLS * SUBLANES)) * SUBLANES
    pad = rows * PACK_COLS - n
    if pad:
        cat = jnp.pad(cat, ((0, 0), (0, pad)) if lead else ((0, pad),))
    return cat.reshape((lead, rows, PACK_COLS) if lead else (rows, PACK_COLS))


def _unpack(buf, shapes, lead=None):
    flat = buf.reshape((lead, -1) if lead else (-1,))
    out, off = [], 0
    for shp in shapes:
        n = int(np.prod(shp))
        piece = flat[..., off:off + n]
        out.append(piece.reshape(((lead,) if lead else ()) + tuple(shp)))
        off += n
    return out


def _to_slabs(full, axis):
    shp = full.shape
    split = full.reshape(shp[:axis] + (N_DEV, shp[axis] // N_DEV) + shp[axis + 1:])
    return jnp.moveaxis(split, axis, 0)


def _from_slabs(slabs, axis):
    moved = jnp.moveaxis(slabs, 0, axis)
    shp = moved.shape
    return moved.reshape(shp[:axis] + (shp[axis] * shp[axis + 1],) + shp[axis + 2:])


def _adamw_block(r, c):
    if r % 8:
        return r, 256
    br = min(r, max(8, ADAMW_TILE * PACK_COLS // c))
    while r % br:
        br //= 2
    return br, c


def _adamw(name, partials, w, m, v):
    n_lead, r, c = w.shape
    br, bc = _adamw_block(r, c)
    ni, nj = r // br, c // bc
    c1 = 1.0 - ADAM_B1 ** ADAM_STEP
    c2 = 1.0 - ADAM_B2 ** ADAM_STEP

    def kern(*refs):
        p_refs = refs[:n_lead]
        w_ref, m_ref, v_ref, g_ref, d_ref, nm_ref, nv_ref = refs[n_lead:]
        layer = pl.program_id(0)
        g = None
        for l, p_ref in enumerate(p_refs):
            gl = p_ref[0].astype(F32)
            for s in range(1, N_DEV):
                gl = gl + p_ref[s].astype(F32)
            g = gl if g is None else jnp.where(layer == l, gl, g)
        nm = ADAM_B1 * m_ref[...] + (1.0 - ADAM_B1) * g
        nv = ADAM_B2 * v_ref[...] + (1.0 - ADAM_B2) * jnp.square(g)
        g_ref[...] = g
        nm_ref[...] = nm
        nv_ref[...] = nv
        d_ref[...] = -ADAM_LR * ((nm / c1) / (jnp.sqrt(nv / c2) + ADAM_EPS) + ADAM_WD * w_ref[...])

    def part_spec(mine):
        def index(l, i, j):
            before, after = l < mine, l > mine
            return (0, jnp.where(before, 0, jnp.where(after, ni - 1, i)), jnp.where(before, 0, jnp.where(after, nj - 1, j)))
        return pl.BlockSpec((N_DEV, br, bc), index)

    blk = pl.BlockSpec((None, br, bc), lambda l, i, j: (l, i, j))
    return pl.pallas_call(
        kern, grid=(n_lead, ni, nj),
        in_specs=[part_spec(l) for l in range(n_lead)] + [blk, blk, blk],
        out_specs=[blk] * 4, out_shape=[jax.ShapeDtypeStruct(w.shape, F32)] * 4,
        compiler_params=pltpu.CompilerParams(dimension_semantics=("arbitrary",) * 3),
        name=name)(*partials, w, m, v)


def _as_rows(a, lead):
    return a.reshape(a.shape[:lead] + (-1, a.shape[-1]))


def kernel(x, p, norm1_g, w_in, sg_ln_g, sg_ln_b, sg_w, sg_b, gla_w_a2, gla_b_a, gla_norm_g, att_rel_bias, conv_dw_w, conv_dw_b, conv_ln_g, conv_ln_b, w_branch, w_gate, b_gate, w_out, norm2_g, w_ff1, w_ff2, norm3_g, w_ple_gate, b_ple_gate, w_ple, final_g, loss_target, m_norm1_g, m_w_in, m_sg_ln_g, m_sg_ln_b, m_sg_w, m_sg_b, m_gla_w_a2, m_gla_b_a, m_gla_norm_g, m_att_rel_bias, m_conv_dw_w, m_conv_dw_b, m_conv_ln_g, m_conv_ln_b, m_w_branch, m_w_gate, m_b_gate, m_w_out, m_norm2_g, m_w_ff1, m_w_ff2, m_norm3_g, m_w_ple_gate, m_b_ple_gate, m_w_ple, m_final_g, v_norm1_g, v_w_in, v_sg_ln_g, v_sg_ln_b, v_sg_w, v_sg_b, v_gla_w_a2, v_gla_b_a, v_gla_norm_g, v_att_rel_bias, v_conv_dw_w, v_conv_dw_b, v_conv_ln_g, v_conv_ln_b, v_w_branch, v_w_gate, v_b_gate, v_w_out, v_norm2_g, v_w_ff1, v_w_ff2, v_norm3_g, v_w_ple_gate, v_b_ple_gate, v_w_ple, v_final_g):
    args = locals()
    wts = {n: args[n] for n in WEIGHTS}
    mom = {n: args['m_' + n] for n in WEIGHTS}
    var = {n: args['v_' + n] for n in WEIGHTS}

    local = {d_name: dict(d, w_in=jnp.swapaxes(d['w_in'], 1, 2))
             for d_name, d in (("w", wts), ("m", mom), ("v", var))}
    shards = [{n: local["w"][n][i].astype(_MXU_DTYPE) for n in MXU_WEIGHTS} for i in range(DEPTH)]

    first_w_in, vec = _gather_via_sibling("gather_first_weights",
                                          [shards[0]['w_in'], _pack([wts[n] for n in VEC_WEIGHTS], F32)])
    vec_full = {n: _from_slabs(slabs, SHARD_AXIS[n])
                for n, slabs in zip(VEC_WEIGHTS, _unpack(vec, [wts[n].shape for n in VEC_WEIGHTS], lead=N_DEV))}
    small_names = [n for n in WEIGHTS if n not in MXU_WEIGHTS and n != 'final_g']
    layers = [{n: (vec_full[n] if n in vec_full else wts[n])[i] for n in small_names} for i in range(DEPTH)]
    _land(layers[0], ['w_in'], [first_w_in])

    def vec_items(small):
        stacked = [jnp.stack([g[n] for g in small]) for n in VEC_WEIGHTS]
        return [_scatter_item(_pack([_to_slabs(a, SHARD_AXIS[n]) for n, a in zip(VEC_WEIGHTS, stacked)], F32,
                                    lead=N_DEV))]

    seq = x.shape[1:]
    loss, grad_x, dgf, small, parts, (vec_parts,) = _local_step(
        x.reshape(seq), p.reshape(DEPTH, seq[0], PLE_DIM), loss_target.reshape(seq), final_g, layers, shards, vec_items)
    loss = lax.psum(loss, ("x", "y", "c"))

    grads = {n: jnp.stack([small[i][n] for i in range(DEPTH)]) for n in REPLICATED if n != 'final_g'}
    grads['final_g'] = dgf
    repl_parts, = _gather_via_sibling("gather_replicated_grads", [_pack([grads[n] for n in REPLICATED], F32)])

    results = {}
    for n in MXU_WEIGHTS:
        w3, m3, v3 = (_as_rows(local[d][n], 1) for d in ("w", "m", "v"))
        outs = _adamw("adamw_" + n, [parts[i][n].reshape((N_DEV,) + w3.shape[1:]) for i in range(DEPTH)], w3, m3, v3)
        for kind, a in zip(("grad", "delta", "new_m", "new_v"), outs):
            a = a.reshape(local["w"][n].shape)
            results[kind, n] = jnp.swapaxes(a, 1, 2) if n == 'w_in' else a
    for names, part, call in ((VEC_WEIGHTS, vec_parts, "adamw_vec"), (REPLICATED, repl_parts, "adamw_replicated")):
        packed = [_pack([d[n] for n in names], F32)[None] for d in (wts, mom, var)]
        outs = _adamw(call, [part], *packed)
        for kind, buf in zip(("grad", "delta", "new_m", "new_v"), outs):
            for n, a in zip(names, _unpack(buf[0], [wts[n].shape for n in names])):
                results[kind, n] = a
    return (loss, grad_x[None]) + tuple(results[kind, n] for kind in ("grad", "delta", "new_m", "new_v")
                                        for n in WEIGHTS)
```
